```python
import math
import jax, jax.numpy as jnp
from jax import lax
import numpy as np

D_MODEL = 2048
BATCH = 4
SEQ = 2048
DEPTH = 1
DEC_BATCH = 128
DEC_SEQ = 8
PAST_LEN = 16384
PAGE_SIZE = 128

D_MIX = D_MODEL
D_GLA = D_MIX // 2
D_S5 = D_MIX - D_GLA
GLA_HEADS = 4
GLA_DV = D_GLA // GLA_HEADS
GLA_DK = GLA_DV // 2
GLA_RANK = 16
GLA_GATE_NORM = 16.0
GLA_CHUNK = 64
S5_GROUP = 16
S5_GROUPS = D_S5 // S5_GROUP
S5_STATE = 64
N_EGROUPS = 4
N_EPG = 8
N_EXPERTS = N_EGROUPS * N_EPG
TOP_K = 2
D_EXPERT = 512
MOE_BLOCK = 128
D_PLE = 256
EPS = 1e-6
D_IN = 2 * GLA_HEADS * GLA_DK + 2 * D_GLA + GLA_RANK + D_S5

kernel_name = 'hymba_gla_s5_hmoe_step'

F32 = jnp.float32


def _rmsnorm(x, g):
    xf = x.astype(F32)
    return xf * lax.rsqrt(jnp.mean(xf * xf, axis=-1, keepdims=True) + EPS) * g.astype(F32)


def _gla_chunked(q, k, v, log_a, s0):
    b_, l_ = q.shape[:2]
    c = math.gcd(l_, GLA_CHUNK)
    nc = l_ // c

    def to_chunks(t):
        return jnp.moveaxis(t.reshape(b_, nc, c, *t.shape[2:]), 1, 0)

    mask = jnp.tril(jnp.ones((c, c), dtype=bool))[None, :, :, None, None]

    def step(s, xs):
        qc, kc, vc, ac = xs
        cum = jnp.cumsum(ac, axis=1)
        o_inter = jnp.einsum('bthk,bhkv->bthv', qc * jnp.exp(cum), s)
        diff = cum[:, :, None] - cum[:, None, :]
        decay = jnp.where(mask, jnp.exp(jnp.where(mask, diff, 0.0)), 0.0)
        scores = jnp.einsum('bthk,bshk,btshk->bhts', qc, kc, decay)
        o_intra = jnp.einsum('bhts,bshv->bthv', scores, vc)
        last = cum[:, -1]
        k_dec = kc * jnp.exp(last[:, None] - cum)
        s_new = jnp.exp(last)[..., None] * s + jnp.einsum('bshk,bshv->bhkv', k_dec, vc)
        return s_new, o_inter + o_intra

    s_fin, o = lax.scan(step, s0, (to_chunks(q), to_chunks(k), to_chunks(v), to_chunks(log_a)))
    o = jnp.moveaxis(o, 0, 1).reshape(b_, l_, GLA_HEADS, GLA_DV)
    return o, s_fin


def _s5_scan(u, lam_re, lam_im, log_dt, b_re, b_im, c_re, c_im, d_skip, h0_re, h0_im):
    lam_re = lam_re.astype(F32)
    lam_im = lam_im.astype(F32)
    dt = jnp.exp(log_dt.astype(F32))[:, None]
    mag = jnp.exp(lam_re * dt)
    ab_re = mag * jnp.cos(lam_im * dt)
    ab_im = mag * jnp.sin(lam_im * dt)
    den = lam_re * lam_re + lam_im * lam_im
    nr = ab_re - 1.0
    f_re = (nr * lam_re + ab_im * lam_im) / den
    f_im = (ab_im * lam_re - nr * lam_im) / den
    b_re = b_re.astype(F32)
    b_im = b_im.astype(F32)
    bb_re = f_re[..., None] * b_re - f_im[..., None] * b_im
    bb_im = f_re[..., None] * b_im + f_im[..., None] * b_re
    bu_re = jnp.einsum('blgh,gph->blgp', u, bb_re)
    bu_im = jnp.einsum('blgh,gph->blgp', u, bb_im)
    a_re = jnp.broadcast_to(ab_re, bu_re.shape)
    a_im = jnp.broadcast_to(ab_im, bu_im.shape)

    def combine(e1, e2):
        a1r, a1i, b1r, b1i = e1
        a2r, a2i, b2r, b2i = e2
        return (a1r * a2r - a1i * a2i,
                a1r * a2i + a1i * a2r,
                a2r * b1r - a2i * b1i + b2r,
                a2r * b1i + a2i * b1r + b2i)

    pr, pi, xr, xi = lax.associative_scan(combine, (a_re, a_im, bu_re, bu_im), axis=1)
    h0_re = h0_re.astype(F32)[:, None]
    h0_im = h0_im.astype(F32)[:, None]
    xr = xr + pr * h0_re - pi * h0_im
    xi = xi + pr * h0_im + pi * h0_re
    y = (jnp.einsum('blgp,ghp->blgh', xr, c_re.astype(F32))
         - jnp.einsum('blgp,ghp->blgh', xi, c_im.astype(F32))
         + d_skip.astype(F32) * u)
    return y, xr[:, -1], xi[:, -1]


def _mixer(hn, i, s_gla0, s_re0, s_im0, prm):
    b_, l_, _ = hn.shape
    z = jnp.einsum('bld,de->ble', hn, prm['w_in'][i].astype(F32))
    splits = np.cumsum([GLA_HEADS * GLA_DK, GLA_HEADS * GLA_DK, D_GLA, D_GLA, GLA_RANK]).tolist()
    q, k, v, r, g_lr, u = jnp.split(z, splits, axis=-1)
    q = q.reshape(b_, l_, GLA_HEADS, GLA_DK) * (GLA_DK ** -0.5)
    k = k.reshape(b_, l_, GLA_HEADS, GLA_DK)
    v = v.reshape(b_, l_, GLA_HEADS, GLA_DV)
    log_a = jax.nn.log_sigmoid(g_lr @ prm['gla_w_gate_up'][i].astype(F32)
                               + prm['gla_gate_bias'][i].astype(F32)) / GLA_GATE_NORM
    log_a = log_a.reshape(b_, l_, GLA_HEADS, GLA_DK)
    o, s_gla = _gla_chunked(q, k, v, log_a, s_gla0.astype(F32))
    o = _rmsnorm(o, prm['gla_norm'][i]).reshape(b_, l_, D_GLA) * jax.nn.silu(r)
    ug = u.reshape(b_, l_, S5_GROUPS, S5_GROUP)
    y, s_re, s_im = _s5_scan(ug, prm['s5_lam_re'][i], prm['s5_lam_im'][i], prm['s5_log_dt'][i],
                             prm['s5_b_re'][i], prm['s5_b_im'][i], prm['s5_c_re'][i], prm['s5_c_im'][i],
                             prm['s5_d'][i], s_re0, s_im0)
    y = jax.nn.gelu(y.reshape(b_, l_, D_S5))
    y = y * jax.nn.sigmoid(y @ prm['s5_glu_w'][i].astype(F32) + prm['s5_glu_b'][i].astype(F32))
    y = _rmsnorm(y, prm['s5_norm'][i])
    out = jnp.concatenate([o, y], axis=-1) @ prm['w_out'][i].astype(F32)
    return out, s_gla, s_re, s_im


def _hier_moe(x, w_rg, w_re, w_gate, w_up, w_down):
    t_ = x.shape[0]
    lg = x @ w_rg.astype(F32)
    pg = jax.nn.softmax(lg, axis=-1)
    g_sel = jnp.argmax(lg, axis=-1)
    p_g = jnp.take_along_axis(pg, g_sel[:, None], axis=-1)
    le = (x @ w_re.astype(F32)).reshape(t_, N_EGROUPS, N_EPG)
    le_sel = jnp.take_along_axis(le, g_sel[:, None, None], axis=1)[:, 0]
    lt, it = lax.top_k(le_sel, TOP_K)
    weights = p_g * jax.nn.softmax(lt, axis=-1)
    experts = g_sel[:, None] * N_EPG + it
    n_assign = t_ * TOP_K
    e_flat = experts.reshape(-1).astype(jnp.int32)
    tok_flat = jnp.repeat(jnp.arange(t_, dtype=jnp.int32), TOP_K)
    w_flat = weights.reshape(-1)
    order = jnp.argsort(e_flat)
    e_s, tok_s, w_s = e_flat[order], tok_flat[order], w_flat[order]
    counts = jnp.zeros((N_EXPERTS,), jnp.int32).at[e_flat].add(1)
    start = jnp.cumsum(counts) - counts
    padded = (counts + MOE_BLOCK - 1) // MOE_BLOCK * MOE_BLOCK
    pad_end = jnp.cumsum(padded)
    pad_start = pad_end - padded
    dest = pad_start[e_s] + jnp.arange(n_assign, dtype=jnp.int32) - start[e_s]
    n_blocks = -(-(n_assign + N_EXPERTS * (MOE_BLOCK - 1)) // MOE_BLOCK)
    n_slots = n_blocks * MOE_BLOCK
    slot_tok = jnp.full((n_slots,), t_, jnp.int32).at[dest].set(tok_s)
    slot_w = jnp.zeros((n_slots,), F32).at[dest].set(w_s)
    block_e = jnp.minimum(jnp.searchsorted(pad_end, jnp.arange(n_blocks, dtype=jnp.int32) * MOE_BLOCK,
                                           side='right'), N_EXPERTS - 1)
    x_pad = jnp.concatenate([x, jnp.zeros((1, x.shape[1]), F32)], axis=0)

    def run_block(args):
        toks, e = args
        xb = x_pad[toks]
        hb = jax.nn.silu(xb @ w_gate[e].astype(F32)) * (xb @ w_up[e].astype(F32))
        return hb @ w_down[e].astype(F32)

    y_slots = lax.map(run_block, (slot_tok.reshape(n_blocks, MOE_BLOCK), block_e))
    y_slots = y_slots.reshape(n_slots, -1) * slot_w[:, None]
    return jnp.zeros((t_ + 1, x.shape[1]), F32).at[slot_tok].add(y_slots)[:t_]


def _trunk(x, p, st_gla, st_re, st_im, prm):
    b_, l_, _ = x.shape
    h = x.astype(F32)
    new_gla, new_re, new_im = [], [], []
    for i in range(DEPTH):
        hn = _rmsnorm(h, prm['norm_mix'][i])
        mix, sg, sr, si = _mixer(hn, i, st_gla[i], st_re[i], st_im[i], prm)
        h = h + mix
        hn = _rmsnorm(h, prm['norm_ffn'][i]).reshape(b_ * l_, D_MODEL)
        h = h + _hier_moe(hn, prm['router_group'][i], prm['router_expert'][i], prm['w_gate'][i],
                          prm['w_up'][i], prm['w_down'][i]).reshape(b_, l_, D_MODEL)
        gate = jax.nn.sigmoid(_rmsnorm(h, prm['norm_ple'][i]) @ prm['w_ple_gate'][i].astype(F32))
        h = h + (p[i].astype(F32) @ prm['w_ple'][i].astype(F32)) * gate
        new_gla.append(sg)
        new_re.append(sr)
        new_im.append(si)
    y = _rmsnorm(h, prm['norm_final']).astype(x.dtype)
    return y, jnp.stack(new_gla).astype(x.dtype), jnp.stack(new_re).astype(x.dtype), jnp.stack(new_im).astype(x.dtype)


def setup_inputs(seed: int = 0) -> dict:
    key = jax.random.key(seed)
    ks = iter(jax.random.split(key, 40))
    nrm = lambda shape, scale: jax.random.normal(next(ks), shape, F32) * scale
    gain = lambda shape: 1.0 + jax.random.normal(next(ks), shape, F32) * 0.02
    n_idx = jnp.arange(S5_STATE, dtype=F32)
    return {
        'x_prompt': nrm((BATCH, SEQ, D_MODEL), 1.0),
        'x_sample': nrm((DEC_BATCH, DEC_SEQ, D_MODEL), 1.0),
        'p_prompt': nrm((DEPTH, BATCH, SEQ, D_PLE), 1.0),
        'p_sample': nrm((DEPTH, DEC_BATCH, DEC_SEQ, D_PLE), 1.0),
        'state_gla': nrm((DEPTH, DEC_BATCH, GLA_HEADS, GLA_DK, GLA_DV), 0.5),
        'state_s5_re': nrm((DEPTH, DEC_BATCH, S5_GROUPS, S5_STATE), 0.5),
        'state_s5_im': nrm((DEPTH, DEC_BATCH, S5_GROUPS, S5_STATE), 0.5),
        'norm_mix': gain((DEPTH, D_MODEL)),
        'w_in': nrm((DEPTH, D_MODEL, D_IN), D_MODEL ** -0.5),
        'gla_w_gate_up': nrm((DEPTH, GLA_RANK, GLA_HEADS * GLA_DK), GLA_RANK ** -0.5),
        'gla_gate_bias': nrm((DEPTH, GLA_HEADS * GLA_DK), 0.1),
        'gla_norm': gain((DEPTH, GLA_DV)),
        's5_lam_re': -0.5 + nrm((DEPTH, S5_GROUPS, S5_STATE), 0.01),
        's5_lam_im': math.pi * n_idx + nrm((DEPTH, S5_GROUPS, S5_STATE), 0.01),
        's5_log_dt': jax.random.uniform(next(ks), (DEPTH, S5_GROUPS), F32, math.log(1e-3), math.log(1e-1)),
        's5_b_re': nrm((DEPTH, S5_GROUPS, S5_STATE, S5_GROUP), (2 * S5_GROUP) ** -0.5),
        's5_b_im': nrm((DEPTH, S5_GROUPS, S5_STATE, S5_GROUP), (2 * S5_GROUP) ** -0.5),
        's5_c_re': nrm((DEPTH, S5_GROUPS, S5_GROUP, S5_STATE), (2 * S5_STATE) ** -0.5),
        's5_c_im': nrm((DEPTH, S5_GROUPS, S5_GROUP, S5_STATE), (2 * S5_STATE) ** -0.5),
        's5_d': nrm((DEPTH, S5_GROUPS, S5_GROUP), 1.0),
        's5_glu_w': nrm((DEPTH, D_S5, D_S5), D_S5 ** -0.5),
        's5_glu_b': nrm((DEPTH, D_S5), 0.02),
        's5_norm': gain((DEPTH, D_S5)),
        'w_out': nrm((DEPTH, D_MIX, D_MODEL), D_MIX ** -0.5),
        'norm_ffn': gain((DEPTH, D_MODEL)),
        'router_group': nrm((DEPTH, D_MODEL, N_EGROUPS), D_MODEL ** -0.5),
        'router_expert': nrm((DEPTH, D_MODEL, N_EGROUPS * N_EPG), D_MODEL ** -0.5),
        'w_gate': nrm((DEPTH, N_EXPERTS, D_MODEL, D_EXPERT), D_MODEL ** -0.5),
        'w_up': nrm((DEPTH, N_EXPERTS, D_MODEL, D_EXPERT), D_MODEL ** -0.5),
        'w_down': nrm((DEPTH, N_EXPERTS, D_EXPERT, D_MODEL), D_EXPERT ** -0.5),
        'norm_ple': gain((DEPTH, D_MODEL)),
        'w_ple': nrm((DEPTH, D_PLE, D_MODEL), 0.5 * D_PLE ** -0.5),
        'w_ple_gate': nrm((DEPTH, D_MODEL, D_MODEL), D_MODEL ** -0.5),
        'norm_final': gain((D_MODEL,)),
    }


def reference(x_prompt, x_sample, p_prompt, p_sample, state_gla, state_s5_re, state_s5_im,
              norm_mix, w_in, gla_w_gate_up, gla_gate_bias, gla_norm,
              s5_lam_re, s5_lam_im, s5_log_dt, s5_b_re, s5_b_im, s5_c_re, s5_c_im, s5_d,
              s5_glu_w, s5_glu_b, s5_norm, w_out, norm_ffn, router_group, router_expert,
              w_gate, w_up, w_down, norm_ple, w_ple, w_ple_gate, norm_final):
    prm = dict(norm_mix=norm_mix, w_in=w_in, gla_w_gate_up=gla_w_gate_up, gla_gate_bias=gla_gate_bias,
               gla_norm=gla_norm, s5_lam_re=s5_lam_re, s5_lam_im=s5_lam_im, s5_log_dt=s5_log_dt,
               s5_b_re=s5_b_re, s5_b_im=s5_b_im, s5_c_re=s5_c_re, s5_c_im=s5_c_im, s5_d=s5_d,
               s5_glu_w=s5_glu_w, s5_glu_b=s5_glu_b, s5_norm=s5_norm, w_out=w_out, norm_ffn=norm_ffn,
               router_group=router_group, router_expert=router_expert, w_gate=w_gate, w_up=w_up,
               w_down=w_down, norm_ple=norm_ple, w_ple=w_ple, w_ple_gate=w_ple_gate, norm_final=norm_final)
    b_ = x_prompt.shape[0]
    zg = jnp.zeros((DEPTH, b_, GLA_HEADS, GLA_DK, GLA_DV), F32)
    zs = jnp.zeros((DEPTH, b_, S5_GROUPS, S5_STATE), F32)
    y_prompt, gla_p, re_p, im_p = _trunk(x_prompt, p_prompt, zg, zs, zs, prm)
    y_sample, gla_s, re_s, im_s = _trunk(x_sample, p_sample, state_gla, state_s5_re, state_s5_im, prm)
    return (y_prompt, y_sample, gla_p, re_p, im_p, gla_s, re_s, im_s)
```

```python
import functools
import math

import jax
import jax.numpy as jnp
from jax import lax
from jax.experimental import pallas as pl
from jax.experimental.pallas import tpu as pltpu

F32 = jnp.float32
BF16 = jnp.bfloat16

D_MODEL = 2048
D_GLA = 1024
D_S5 = 1024
GLA_HEADS = 4
GLA_DV = 256
GLA_DK = 128
GLA_RANK = 16
GLA_CHUNK = 64
S5_GROUP = 16
S5_GROUPS = 64
S5_STATE = 64
N_EGROUPS = 4
N_EPG = 8
N_EXPERTS = 32
D_EXPERT = 512
D_PLE = 256
EPS = 1e-6

LANES = 128
QK_W = GLA_HEADS * GLA_DK
S5_GB = 8
S5_NGB = S5_GROUPS // S5_GB
S5_SL = S5_GB * S5_STATE
TOK_TILE = 256
MOE_BLK = 256
VMEM_LIMIT = 56 * 1024 * 1024


def _const_spec(shape):
    nd = len(shape)
    return pl.BlockSpec(shape, lambda *_: (0,) * nd, pipeline_mode=pl.Buffered(1))


def _rms(x, g):
    return x * lax.rsqrt(jnp.mean(x * x, axis=-1, keepdims=True) + EPS) * g


def _dot(a, b):
    return jnp.dot(a, b, preferred_element_type=F32)


def _log_sigmoid(x):
    return -(jnp.maximum(-x, 0.0) + jnp.log1p(jnp.exp(-jnp.abs(x))))


def _in_proj_kernel(x_ref, g_ref, wm_ref, wg_ref, wgu_ref, gb_ref,
                    q_ref, k_ref, v_ref, r_ref, la_ref, u_ref):
    hb = _rms(x_ref[...], g_ref[...]).astype(BF16)

    def seg(a, b):
        return _dot(hb, wm_ref[:, a:b])

    q_ref[...] = (seg(0, QK_W) * (GLA_DK ** -0.5)).astype(q_ref.dtype)
    k_ref[...] = seg(QK_W, 2 * QK_W).astype(k_ref.dtype)
    v_ref[...] = seg(2 * QK_W, 2 * QK_W + D_GLA).astype(v_ref.dtype)
    r_ref[...] = seg(2 * QK_W + D_GLA, 2 * QK_W + 2 * D_GLA).astype(r_ref.dtype)
    u_ref[...] = seg(2 * QK_W + 2 * D_GLA, 2 * QK_W + 2 * D_GLA + D_S5)
    zg = _dot(hb, wg_ref[...])
    xg = _dot(zg.astype(BF16), wgu_ref[...]) + gb_ref[...]
    la_ref[...] = _log_sigmoid(xg) * (1.0 / 16.0)


def _in_proj(x2d, g, wm, wg, wgu, gbias, act_dtype):
    t = x2d.shape[0]
    tm = TOK_TILE
    row = lambda w: pl.BlockSpec((tm, w), lambda i: (i, 0))
    return pl.pallas_call(
        _in_proj_kernel,
        grid=(t // tm,),
        in_specs=[row(D_MODEL), _const_spec((1, D_MODEL)), _const_spec(wm.shape), _const_spec(wg.shape),
                  _const_spec(wgu.shape), _const_spec((1, QK_W))],
        out_specs=[row(QK_W), row(QK_W), row(D_GLA), row(D_GLA), row(QK_W), row(D_S5)],
        out_shape=[jax.ShapeDtypeStruct((t, QK_W), act_dtype), jax.ShapeDtypeStruct((t, QK_W), act_dtype),
                   jax.ShapeDtypeStruct((t, D_GLA), act_dtype), jax.ShapeDtypeStruct((t, D_GLA), act_dtype),
                   jax.ShapeDtypeStruct((t, QK_W), F32), jax.ShapeDtypeStruct((t, D_S5), F32)],
        compiler_params=pltpu.CompilerParams(dimension_semantics=("arbitrary",), vmem_limit_bytes=VMEM_LIMIT),
        name="in_proj",
    )(x2d, g, wm, wg, wgu, gbias)


def _tri_mask(c):
    r = lax.broadcasted_iota(jnp.int32, (c, c), 0)
    s = lax.broadcasted_iota(jnp.int32, (c, c), 1)
    return r >= s


def _gla_head_chunk(q, k, v, la, s, mask):
    tri = jnp.where(mask, 1.0, 0.0).astype(BF16)
    hi = la.astype(BF16)
    r1 = la - hi.astype(F32)
    mid = r1.astype(BF16)
    lo = (r1 - mid.astype(F32)).astype(BF16)
    cum = _dot(tri, hi) + _dot(tri, mid) + _dot(tri, lo)
    c = cum.shape[0]
    last = cum[c - 1:c, :]
    qe = (q * jnp.exp(cum)).astype(BF16)
    ke = (k * jnp.exp(-cum)).astype(BF16)
    kd = (k * jnp.exp(last - cum)).astype(BF16)
    o = _dot(qe, s.astype(BF16))
    sc = lax.dot_general(qe, ke, (((1,), (1,)), ((), ())), preferred_element_type=F32)
    sc = jnp.where(mask, sc, 0.0).astype(BF16)
    o = o + _dot(sc, v)
    dec = jnp.exp(jnp.broadcast_to(last, (GLA_DK, GLA_DK)).T)
    s_new = (s * jnp.concatenate([dec, dec], axis=1)
             + lax.dot_general(kd, v, (((0,), (0,)), ((), ())), preferred_element_type=F32))
    return o, s_new


def _gla_finish(o, r, g):
    rf = r.astype(F32)
    return _rms(o, g) * (rf * jax.nn.sigmoid(rf))


def _gla_prompt_kernel(q_ref, k_ref, v_ref, r_ref, la_ref, g_ref, o_ref, sfin_ref, s_ref, *, n_chunks):
    j = pl.program_id(1)

    @pl.when(j == 0)
    def _():
        s_ref[...] = jnp.zeros_like(s_ref)

    mask = _tri_mask(GLA_CHUNK)
    g = g_ref[...]

    def chunk(ci, carry):
        r0 = pl.multiple_of(ci * GLA_CHUNK, GLA_CHUNK)
        rows = pl.ds(r0, GLA_CHUNK)
        for h in range(GLA_HEADS):
            kc = slice(h * GLA_DK, (h + 1) * GLA_DK)
            vc = slice(h * GLA_DV, (h + 1) * GLA_DV)
            o, s_new = _gla_head_chunk(q_ref[rows, kc].astype(F32), k_ref[rows, kc].astype(F32),
                                       v_ref[rows, vc], la_ref[rows, kc], s_ref[h], mask)
            s_ref[h] = s_new
            o_ref[rows, vc] = _gla_finish(o, r_ref[rows, vc], g).astype(o_ref.dtype)
        return carry

    lax.fori_loop(0, n_chunks, chunk, 0)

    @pl.when(j == pl.num_programs(1) - 1)
    def _():
        sfin_ref[0] = s_ref[...]


def _gla_prompt(q, k, v, r, la, g, batch, seq):
    rb = 4 * GLA_CHUNK
    nj = seq // rb
    row = lambda w: pl.BlockSpec((rb, w), lambda b, j: (b * nj + j, 0))
    return pl.pallas_call(
        functools.partial(_gla_prompt_kernel, n_chunks=rb // GLA_CHUNK),
        grid=(batch, nj),
        in_specs=[row(QK_W), row(QK_W), row(D_GLA), row(D_GLA), row(QK_W), _const_spec((1, GLA_DV))],
        out_specs=[row(D_GLA),
                   pl.BlockSpec((1, GLA_HEADS, GLA_DK, GLA_DV), lambda b, j: (b, 0, 0, 0))],
        out_shape=[jax.ShapeDtypeStruct((batch * seq, D_GLA), BF16),
                   jax.ShapeDtypeStruct((batch, GLA_HEADS, GLA_DK, GLA_DV), F32)],
        scratch_shapes=[pltpu.VMEM((GLA_HEADS, GLA_DK, GLA_DV), F32)],
        compiler_params=pltpu.CompilerParams(dimension_semantics=("arbitrary", "arbitrary"),
                                             vmem_limit_bytes=VMEM_LIMIT),
        name="gla_prompt",
    )(q, k, v, r, la, g)


def _gla_sample_kernel(q_ref, k_ref, v_ref, r_ref, la_ref, g_ref, s0_ref, o_ref, sfin_ref, *, n_seq, seq):
    cp = 2 * seq
    mask = _tri_mask(cp)
    g = g_ref[...]

    def pad(x):
        return jnp.concatenate([x, jnp.zeros_like(x)], axis=0)

    def one_seq(i, carry):
        r0 = pl.multiple_of(i * seq, seq)
        rows = pl.ds(r0, seq)
        for h in range(GLA_HEADS):
            kc = slice(h * GLA_DK, (h + 1) * GLA_DK)
            vc = slice(h * GLA_DV, (h + 1) * GLA_DV)
            o, s_new = _gla_head_chunk(pad(q_ref[rows, kc]), pad(k_ref[rows, kc]),
                                       pad(v_ref[rows, vc]).astype(BF16), pad(la_ref[rows, kc]),
                                       s0_ref[i, h], mask)
            sfin_ref[i, h] = s_new
            o_ref[rows, vc] = _gla_finish(o[:seq], r_ref[rows, vc], g).astype(o_ref.dtype)
        return carry

    lax.fori_loop(0, n_seq, one_seq, 0)


def _gla_sample(q, k, v, r, la, g, s0, batch, seq):
    ns = 8
    rb = ns * seq
    row = lambda w: pl.BlockSpec((rb, w), lambda i: (i, 0))
    st = pl.BlockSpec((ns, GLA_HEADS, GLA_DK, GLA_DV), lambda i: (i, 0, 0, 0))
    return pl.pallas_call(
        functools.partial(_gla_sample_kernel, n_seq=ns, seq=seq),
        grid=(batch // ns,),
        in_specs=[row(QK_W), row(QK_W), row(D_GLA), row(D_GLA), row(QK_W), _const_spec((1, GLA_DV)), st],
        out_specs=[row(D_GLA), st],
        out_shape=[jax.ShapeDtypeStruct((batch * seq, D_GLA), BF16),
                   jax.ShapeDtypeStruct((batch, GLA_HEADS, GLA_DK, GLA_DV), F32)],
        compiler_params=pltpu.CompilerParams(dimension_semantics=("arbitrary",), vmem_limit_bytes=VMEM_LIMIT),
        name="gla_sample",
    )(q, k, v, r, la, g, s0)


def _s5_kernel(u_ref, wbu_ref, wc_ref, are_ref, aim_ref, d_ref, h0r_ref, h0i_ref,
               y_ref, sre_ref, sim_ref, bu_ref, xs_ref, car_ref, *, nb, tc, has_state):
    j = pl.program_id(2)

    @pl.when(j == 0)
    def _():
        if has_state:
            car_ref[0] = h0r_ref[...]
            car_ref[1] = h0i_ref[...]
        else:
            car_ref[...] = jnp.zeros_like(car_ref)

    u2 = u_ref[...].reshape(nb * tc, LANES)
    ub = u2.astype(BF16)
    nl = S5_SL // LANES
    for l in range(2 * nl):
        bu_ref[l] = _dot(ub, wbu_ref[0, :, l * LANES:(l + 1) * LANES])
    a_r = [jnp.broadcast_to(are_ref[0, :, l * LANES:(l + 1) * LANES], (nb, LANES)) for l in range(nl)]
    a_i = [jnp.broadcast_to(aim_ref[0, :, l * LANES:(l + 1) * LANES], (nb, LANES)) for l in range(nl)]

    def step(t, carry):
        rows = pl.ds(t, nb, stride=tc)
        out = []
        for l in range(nl):
            xr, xi = carry[2 * l], carry[2 * l + 1]
            nr = a_r[l] * xr - a_i[l] * xi + bu_ref[l, rows, :]
            ni = a_r[l] * xi + a_i[l] * xr + bu_ref[nl + l, rows, :]
            xs_ref[l, rows, :] = nr
            xs_ref[nl + l, rows, :] = ni
            out += [nr, ni]
        return tuple(out)

    init = []
    for l in range(nl):
        init += [car_ref[0, :, l * LANES:(l + 1) * LANES], car_ref[1, :, l * LANES:(l + 1) * LANES]]
    fin = lax.fori_loop(0, tc, step, tuple(init), unroll=8)
    xr = jnp.concatenate([fin[2 * l] for l in range(nl)], axis=1)
    xi = jnp.concatenate([fin[2 * l + 1] for l in range(nl)], axis=1)
    car_ref[0] = xr
    car_ref[1] = xi
    y = d_ref[0] * u2
    for l in range(2 * nl):
        y = y + _dot(xs_ref[l].astype(BF16), wc_ref[0, l * LANES:(l + 1) * LANES, :])
    y_ref[...] = y.reshape(nb, tc, LANES)

    @pl.when(j == pl.num_programs(2) - 1)
    def _():
        sre_ref[...] = xr
        sim_ref[...] = xi


def _s5(u3d, wbu, wc, a_re, a_im, dsk, h0r, h0i, nb, tc, has_state):
    batch, seq, _ = u3d.shape
    grid = (S5_NGB, batch // nb, seq // tc)
    st = pl.BlockSpec((nb, S5_SL), lambda g, b, j: (b, g))
    par = lambda w: pl.BlockSpec((1, 1, w), lambda g, b, j: (g, 0, 0))
    ublk = pl.BlockSpec((nb, tc, LANES), lambda g, b, j: (b, j, g))
    return pl.pallas_call(
        functools.partial(_s5_kernel, nb=nb, tc=tc, has_state=has_state),
        grid=grid,
        in_specs=[ublk,
                  pl.BlockSpec((1, LANES, 2 * S5_SL), lambda g, b, j: (g, 0, 0)),
                  pl.BlockSpec((1, 2 * S5_SL, LANES), lambda g, b, j: (g, 0, 0)),
                  par(S5_SL), par(S5_SL), par(LANES), st, st],
        out_specs=[ublk, st, st],
        out_shape=[jax.ShapeDtypeStruct(u3d.shape, F32),
                   jax.ShapeDtypeStruct((batch, S5_GROUPS * S5_STATE), F32),
                   jax.ShapeDtypeStruct((batch, S5_GROUPS * S5_STATE), F32)],
        scratch_shapes=[pltpu.VMEM((2 * S5_SL // LANES, nb * tc, LANES), F32),
                        pltpu.VMEM((2 * S5_SL // LANES, nb * tc, LANES), F32),
                        pltpu.VMEM((2, nb, S5_SL), F32)],
        compiler_params=pltpu.CompilerParams(dimension_semantics=("arbitrary",) * 3,
                                             vmem_limit_bytes=VMEM_LIMIT),
        name="s5_state" if has_state else "s5_zero",
    )(u3d, wbu, wc, a_re, a_im, dsk, h0r, h0i)


def _mix_out_kernel(x_ref, og_ref, ys_ref, glu_w_ref, glu_b_ref, s5n_ref, wo_ref, nffn_ref, wr_ref,
                    h1_ref, hn_ref, rt_ref):
    y = jax.nn.gelu(ys_ref[...])
    y = y * jax.nn.sigmoid(_dot(y.astype(BF16), glu_w_ref[...]) + glu_b_ref[...])
    yn = _rms(y, s5n_ref[...]).astype(BF16)
    mix = _dot(og_ref[...], wo_ref[0:D_GLA, :]) + _dot(yn, wo_ref[D_GLA:D_GLA + D_S5, :])
    h1 = x_ref[...] + mix
    h1_ref[...] = h1
    hn = _rms(h1, nffn_ref[...])
    hn_ref[...] = hn
    logits = jnp.dot(hn, wr_ref[...], preferred_element_type=F32, precision=lax.Precision.HIGHEST)
    rt_ref[...] = _route(logits)


def _route(logits):
    col = lax.broadcasted_iota(jnp.int32, logits.shape, 1)
    colf = col.astype(F32)
    neg = -jnp.inf

    def first_argmax(vals):
        m = jnp.max(vals, axis=-1, keepdims=True)
        idx = jnp.min(jnp.where(vals == m, colf, float(LANES)), axis=-1, keepdims=True)
        return m, idx

    lg = jnp.where(col < N_EGROUPS, logits, neg)
    gmax, gsel = first_argmax(lg)
    p_g = 1.0 / jnp.sum(jnp.exp(lg - gmax), axis=-1, keepdims=True)
    ecol = col - N_EGROUPS
    egrp = (ecol >> 3).astype(F32)
    in_group = (ecol >= 0) & (ecol < N_EXPERTS) & (egrp == gsel)
    le = jnp.where(in_group, logits, neg)
    m1, i1 = first_argmax(le)
    le2 = jnp.where(colf == i1, neg, le)
    m2, i2 = first_argmax(le2)
    e2 = jnp.exp(m2 - m1)
    den = 1.0 + e2
    w1 = p_g * (1.0 / den)
    w2 = p_g * (e2 / den)
    e1f = i1 - float(N_EGROUPS)
    e2f = i2 - float(N_EGROUPS)
    out = jnp.where(col == 0, e1f, jnp.where(col == 1, e2f, jnp.where(col == 2, w1, jnp.where(col == 3, w2, 0.0))))
    return out


def _mix_out(x2d, og, ys, glu_w, glu_b, s5n, wo, nffn, wr):
    t = x2d.shape[0]
    tm = TOK_TILE
    row = lambda w: pl.BlockSpec((tm, w), lambda i: (i, 0))
    return pl.pallas_call(
        _mix_out_kernel,
        grid=(t // tm,),
        in_specs=[row(D_MODEL), row(D_GLA), row(D_S5), _const_spec(glu_w.shape), _const_spec((1, D_S5)),
                  _const_spec((1, D_S5)), _const_spec(wo.shape), _const_spec((1, D_MODEL)),
                  _const_spec(wr.shape)],
        out_specs=[row(D_MODEL), row(D_MODEL), row(LANES)],
        out_shape=[jax.ShapeDtypeStruct((t, D_MODEL), F32), jax.ShapeDtypeStruct((t, D_MODEL), F32),
                   jax.ShapeDtypeStruct((t, LANES), F32)],
        compiler_params=pltpu.CompilerParams(dimension_semantics=("arbitrary",), vmem_limit_bytes=VMEM_LIMIT),
        name="mix_out",
    )(x2d, og, ys, glu_w, glu_b, s5n, wo, nffn, wr)


def _moe_kernel(be_ref, nv_ref, tok_ref, dst_ref, hnp_hbm, hns_hbm, wg_ref, wu_ref, wd_ref, out_hbm,
                xb_ref, yb_ref, wgb_ref, wub_ref, wdb_ref, sem):
    b = pl.program_id(0)
    n = nv_ref[b]
    tp = hnp_hbm.shape[0]

    @pl.when(b == 0)
    def _():
        xb_ref[...] = jnp.zeros_like(xb_ref)

    def gather_copy(src, i, tok):
        return pltpu.make_async_copy(src.at[pl.ds(tok, 1), :], xb_ref.at[pl.ds(i, 1), :], sem.at[0])

    def scatter_copy(i, dst):
        return pltpu.make_async_copy(yb_ref.at[pl.ds(i, 1), :], out_hbm.at[pl.ds(dst, 1), :], sem.at[1])

    @pl.when(n > 0)
    def _():
        def g_start(i, c):
            tok = tok_ref[0, 0, i]

            @pl.when(tok < tp)
            def _():
                gather_copy(hnp_hbm, i, tok).start()

            @pl.when(tok >= tp)
            def _():
                gather_copy(hns_hbm, i, tok - tp).start()

            return c

        lax.fori_loop(0, n, g_start, 0)

        prev = be_ref[jnp.maximum(b - 1, 0)]

        @pl.when((b == 0) | (prev != be_ref[b]))
        def _():
            wgb_ref[...] = wg_ref[...].astype(BF16)
            wub_ref[...] = wu_ref[...].astype(BF16)
            wdb_ref[...] = wd_ref[...].astype(BF16)

        def g_wait(i, c):
            gather_copy(hnp_hbm, i, 0).wait()
            return c

        lax.fori_loop(0, n, g_wait, 0)

        x = xb_ref[...].astype(BF16)
        gate = _dot(x, wgb_ref[...])
        up = _dot(x, wub_ref[...])
        hid = (gate * jax.nn.sigmoid(gate) * up).astype(BF16)
        yb_ref[...] = _dot(hid, wdb_ref[...])

        def s_start(i, c):
            scatter_copy(i, dst_ref[0, 0, i]).start()
            return c

        lax.fori_loop(0, n, s_start, 0)

        def s_wait(i, c):
            scatter_copy(i, 0).wait()
            return c

        lax.fori_loop(0, n, s_wait, 0)


def _moe(hn_p, hn_s, blk_e, blk_n, slot_tok, slot_dst, w_gate, w_up, w_down, n_out_rows):
    nblk = blk_e.shape[0]
    smem_blk = pl.BlockSpec((1, 1, MOE_BLK), lambda b, be, nv: (b, 0, 0), memory_space=pltpu.SMEM)
    wspec = lambda s: pl.BlockSpec((None,) + s, lambda b, be, nv: (be[b], 0, 0))
    grid_spec = pltpu.PrefetchScalarGridSpec(
        num_scalar_prefetch=2,
        grid=(nblk,),
        in_specs=[smem_blk, smem_blk, pl.BlockSpec(memory_space=pl.ANY), pl.BlockSpec(memory_space=pl.ANY),
                  wspec((D_MODEL, D_EXPERT)), wspec((D_MODEL, D_EXPERT)), wspec((D_EXPERT, D_MODEL))],
        out_specs=pl.BlockSpec(memory_space=pl.ANY),
        scratch_shapes=[pltpu.VMEM((MOE_BLK, D_MODEL), F32), pltpu.VMEM((MOE_BLK, D_MODEL), F32),
                        pltpu.VMEM((D_MODEL, D_EXPERT), BF16), pltpu.VMEM((D_MODEL, D_EXPERT), BF16),
                        pltpu.VMEM((D_EXPERT, D_MODEL), BF16), pltpu.SemaphoreType.DMA((2,))],
    )
    return pl.pallas_call(
        _moe_kernel,
        grid_spec=grid_spec,
        out_shape=jax.ShapeDtypeStruct((n_out_rows, D_MODEL), F32),
        compiler_params=pltpu.CompilerParams(dimension_semantics=("arbitrary",), vmem_limit_bytes=VMEM_LIMIT),
        name="moe_experts",
    )(blk_e, blk_n, slot_tok, slot_dst, hn_p, hn_s, w_gate, w_up, w_down)


def _moe_plan(rt_all):
    t_all = rt_all.shape[0]
    n_assign = 2 * t_all
    e_flat = rt_all[:, 0:2].astype(jnp.int32).reshape(-1)
    onehot = (e_flat[:, None] == jnp.arange(N_EXPERTS, dtype=jnp.int32)[None, :]).astype(jnp.int32)
    csum = jnp.cumsum(onehot, axis=0)
    counts = csum[-1]
    rank = jnp.take_along_axis(csum, e_flat[:, None], axis=1)[:, 0] - 1
    padded = (counts + MOE_BLK - 1) // MOE_BLK * MOE_BLK
    pad_end = jnp.cumsum(padded)
    pad_start = pad_end - padded
    dest = pad_start[e_flat] + rank
    nblk = -(-(n_assign + N_EXPERTS * (MOE_BLK - 1)) // MOE_BLK)
    n_slots = nblk * MOE_BLK
    a = jnp.arange(n_assign, dtype=jnp.int32)
    tok = a // 2
    slot_tok = jnp.zeros((n_slots,), jnp.int32).at[dest].set(tok)
    slot_dst = jnp.zeros((n_slots,), jnp.int32).at[dest].set((a % 2) * t_all + tok)
    blk_start = jnp.arange(nblk, dtype=jnp.int32) * MOE_BLK
    blk_e = jnp.minimum(jnp.searchsorted(pad_end, blk_start, side='right'), N_EXPERTS - 1).astype(jnp.int32)
    blk_n = jnp.clip(counts[blk_e] - (blk_start - pad_start[blk_e]), 0, MOE_BLK).astype(jnp.int32)
    return blk_e, blk_n, slot_tok.reshape(nblk, 1, MOE_BLK), slot_dst.reshape(nblk, 1, MOE_BLK)


def _ple_out_kernel(h1_ref, y0_ref, y1_ref, rt_ref, p_ref, nple_ref, wpg_ref, wp_ref, nfin_ref, o_ref):
    rt = rt_ref[...]
    h2 = h1_ref[...] + rt[:, 2:3] * y0_ref[...] + rt[:, 3:4] * y1_ref[...]
    gate = jax.nn.sigmoid(_dot(_rms(h2, nple_ref[...]).astype(BF16), wpg_ref[...]))
    h3 = h2 + _dot(p_ref[...].astype(BF16), wp_ref[...]) * gate
    o_ref[...] = _rms(h3, nfin_ref[...])


def _ple_out(h1, ymoe, rt, p2d, nple, wpg, wp, nfin, row_off, t_all):
    t = h1.shape[0]
    tm = TOK_TILE
    ob = row_off // tm
    kb = t_all // tm
    row = lambda w: pl.BlockSpec((tm, w), lambda i: (i, 0))
    return pl.pallas_call(
        _ple_out_kernel,
        grid=(t // tm,),
        in_specs=[row(D_MODEL),
                  pl.BlockSpec((tm, D_MODEL), lambda i: (i + ob, 0)),
                  pl.BlockSpec((tm, D_MODEL), lambda i: (i + ob + kb, 0)),
                  row(LANES), row(D_PLE), _const_spec((1, D_MODEL)), _const_spec(wpg.shape),
                  _const_spec(wp.shape), _const_spec((1, D_MODEL))],
        out_specs=row(D_MODEL),
        out_shape=jax.ShapeDtypeStruct((t, D_MODEL), F32),
        compiler_params=pltpu.CompilerParams(dimension_semantics=("arbitrary",), vmem_limit_bytes=VMEM_LIMIT),
        name="ple_out",
    )(h1, ymoe, ymoe, rt, p2d, nple, wpg, wp, nfin)


def _s5_params(lam_re, lam_im, log_dt, b_re, b_im, c_re, c_im, d_skip):
    dt = jnp.exp(log_dt)[:, None]
    mag = jnp.exp(lam_re * dt)
    ab_re = mag * jnp.cos(lam_im * dt)
    ab_im = mag * jnp.sin(lam_im * dt)
    den = lam_re * lam_re + lam_im * lam_im
    nr = ab_re - 1.0
    f_re = (nr * lam_re + ab_im * lam_im) / den
    f_im = (ab_im * lam_re - nr * lam_im) / den
    bb_re = f_re[..., None] * b_re - f_im[..., None] * b_im
    bb_im = f_re[..., None] * b_im + f_im[..., None] * b_re
    eye = jnp.eye(S5_GB, dtype=F32)

    def bu_w(bb):
        bb = bb.reshape(S5_NGB, S5_GB, S5_STATE, S5_GROUP)
        w = jnp.einsum('nlph,lm->nlhmp', bb, eye)
        return w.reshape(S5_NGB, S5_GB * S5_GROUP, S5_SL)

    def c_w(c):
        c = c.reshape(S5_NGB, S5_GB, S5_GROUP, S5_STATE)
        w = jnp.einsum('nlhp,lm->nlpmh', c, eye)
        return w.reshape(S5_NGB, S5_SL, S5_GB * S5_GROUP)

    wbu = jnp.concatenate([bu_w(bb_re), bu_w(bb_im)], axis=2).astype(BF16)
    wc = jnp.concatenate([c_w(c_re), -c_w(c_im)], axis=1).astype(BF16)
    a_re = ab_re.reshape(S5_NGB, 1, S5_SL)
    a_im = ab_im.reshape(S5_NGB, 1, S5_SL)
    dsk = d_skip.reshape(S5_NGB, 1, S5_GB * S5_GROUP)
    return wbu, wc, a_re, a_im, dsk


def kernel(x_prompt, x_sample, p_prompt, p_sample, state_gla, state_s5_re, state_s5_im, norm_mix, w_in, gla_w_gate_up, gla_gate_bias, gla_norm, s5_lam_re, s5_lam_im, s5_log_dt, s5_b_re, s5_b_im, s5_c_re, s5_c_im, s5_d, s5_glu_w, s5_glu_b, s5_norm, w_out, norm_ffn, router_group, router_expert, w_gate, w_up, w_down, norm_ple, w_ple, w_ple_gate, norm_final):
    depth = w_in.shape[0]
    assert depth == 1
    i = 0
    bp, lp, _ = x_prompt.shape
    bs, ls, _ = x_sample.shape
    tp, ts = bp * lp, bs * ls
    t_all = tp + ts

    n_qkvr = 2 * QK_W + 2 * D_GLA
    wi = w_in[i]
    wm = jnp.concatenate([wi[:, :n_qkvr], wi[:, n_qkvr + GLA_RANK:]], axis=1).astype(BF16)
    wg = jnp.pad(wi[:, n_qkvr:n_qkvr + GLA_RANK], ((0, 0), (0, LANES - GLA_RANK))).astype(BF16)
    wgu = jnp.pad(gla_w_gate_up[i], ((0, LANES - GLA_RANK), (0, 0))).astype(BF16)
    gbias = gla_gate_bias[i].reshape(1, QK_W)
    wbu, wc, a_re, a_im, dsk = _s5_params(s5_lam_re[i], s5_lam_im[i], s5_log_dt[i], s5_b_re[i], s5_b_im[i],
                                          s5_c_re[i], s5_c_im[i], s5_d[i])
    glu_w = s5_glu_w[i].astype(BF16)
    wo = w_out[i].astype(BF16)
    wr = jnp.pad(jnp.concatenate([router_group[i], router_expert[i]], axis=1),
                 ((0, 0), (0, LANES - N_EGROUPS - N_EXPERTS)))
    wpg = w_ple_gate[i].astype(BF16)
    wp = w_ple[i].astype(BF16)
    vec = lambda a: a.reshape(1, -1)

    xp = x_prompt.reshape(tp, D_MODEL)
    xs = x_sample.reshape(ts, D_MODEL)

    qp, kp, vp, rp, lap, up = _in_proj(xp, vec(norm_mix[i]), wm, wg, wgu, gbias, BF16)
    qs, ks, vs, rs, las, us = _in_proj(xs, vec(norm_mix[i]), wm, wg, wgu, gbias, F32)
    ogp, gla_p = _gla_prompt(qp, kp, vp, rp, lap, vec(gla_norm[i]), bp, lp)
    ogs, gla_s = _gla_sample(qs, ks, vs, rs, las, vec(gla_norm[i]), state_gla[i], bs, ls)
    zst = jnp.zeros((bp, S5_GROUPS * S5_STATE), F32)
    ysp, re_p, im_p = _s5(up.reshape(bp, lp, D_S5), wbu, wc, a_re, a_im, dsk, zst, zst,
                          nb=bp, tc=256, has_state=False)
    yss, re_s, im_s = _s5(us.reshape(bs, ls, D_S5), wbu, wc, a_re, a_im, dsk,
                          state_s5_re[i].reshape(bs, -1), state_s5_im[i].reshape(bs, -1),
                          nb=32, tc=ls, has_state=True)

    mo = functools.partial(_mix_out, glu_w=glu_w, glu_b=vec(s5_glu_b[i]), s5n=vec(s5_norm[i]), wo=wo,
                           nffn=vec(norm_ffn[i]), wr=wr)
    h1p, hnp, rtp = mo(xp, ogp, ysp.reshape(tp, D_S5))
    h1s, hns, rts = mo(xs, ogs, yss.reshape(ts, D_S5))

    rt_all = jnp.concatenate([rtp, rts], axis=0)
    blk_e, blk_n, slot_tok, slot_dst = _moe_plan(rt_all)
    ymoe = _moe(hnp, hns, blk_e, blk_n, slot_tok, slot_dst, w_gate[i], w_up[i], w_down[i], 2 * t_all)

    po = functools.partial(_ple_out, nple=vec(norm_ple[i]), wpg=wpg, wp=wp, nfin=vec(norm_final), t_all=t_all)
    y_p = po(h1p, ymoe, rtp, p_prompt[i].reshape(tp, D_PLE), row_off=0)
    y_s = po(h1s, ymoe, rts, p_sample[i].reshape(ts, D_PLE), row_off=tp)

    s5shape = lambda a, b: a.reshape(1, b, S5_GROUPS, S5_STATE)
    return (y_p.reshape(bp, lp, D_MODEL), y_s.reshape(bs, ls, D_MODEL),
            gla_p[None], s5shape(re_p, bp), s5shape(im_p, bp),
            gla_s[None], s5shape(re_s, bs), s5shape(im_s, bs))
```

```python
import functools
import math

import jax
import jax.numpy as jnp
from jax import lax
from jax.experimental import pallas as pl
from jax.experimental.pallas import tpu as pltpu

F32 = jnp.float32
BF16 = jnp.bfloat16

D_MODEL = 2048
D_GLA = 1024
D_S5 = 1024
GLA_HEADS = 4
GLA_DV = 256
GLA_DK = 128
GLA_RANK = 16
GLA_CHUNK = 64
S5_GROUP = 16
S5_GROUPS = 64
S5_STATE = 64
N_EGROUPS = 4
N_EPG = 8
N_EXPERTS = 32
D_EXPERT = 512
D_PLE = 256
EPS = 1e-6

LANES = 128
QK_W = GLA_HEADS * GLA_DK
S5_GB = 8
S5_NGB = S5_GROUPS // S5_GB
S5_SL = S5_GB * S5_STATE
TOK_TILE = 256
MOE_BLK = 256
VMEM_LIMIT = 56 * 1024 * 1024


def _const_spec(shape):
    nd = len(shape)
    return pl.BlockSpec(shape, lambda *_: (0,) * nd, pipeline_mode=pl.Buffered(1))


def _rms(x, g):
    return x * lax.rsqrt(jnp.mean(x * x, axis=-1, keepdims=True) + EPS) * g


def _dot(a, b):
    return jnp.dot(a, b, preferred_element_type=F32)


def _log_sigmoid(x):
    return -(jnp.maximum(-x, 0.0) + jnp.log1p(jnp.exp(-jnp.abs(x))))


def _in_proj_kernel(x_ref, g_ref, wm_ref, wg_ref, wgu_ref, gb_ref,
                    q_ref, k_ref, v_ref, r_ref, la_ref, u_ref):
    hb = _rms(x_ref[...], g_ref[...]).astype(BF16)

    def seg(a, b):
        return _dot(hb, wm_ref[:, a:b])

    q_ref[...] = (seg(0, QK_W) * (GLA_DK ** -0.5)).astype(q_ref.dtype)
    k_ref[...] = seg(QK_W, 2 * QK_W).astype(k_ref.dtype)
    v_ref[...] = seg(2 * QK_W, 2 * QK_W + D_GLA).astype(v_ref.dtype)
    r_ref[...] = seg(2 * QK_W + D_GLA, 2 * QK_W + 2 * D_GLA).astype(r_ref.dtype)
    u_ref[...] = seg(2 * QK_W + 2 * D_GLA, 2 * QK_W + 2 * D_GLA + D_S5)
    zg = _dot(hb, wg_ref[...])
    xg = _dot(zg.astype(BF16), wgu_ref[...]) + gb_ref[...]
    la_ref[...] = _log_sigmoid(xg) * (1.0 / 16.0)


def _in_proj(x2d, g, wm, wg, wgu, gbias, act_dtype):
    t = x2d.shape[0]
    tm = TOK_TILE
    row = lambda w: pl.BlockSpec((tm, w), lambda i: (i, 0))
    return pl.pallas_call(
        _in_proj_kernel,
        grid=(t // tm,),
        in_specs=[row(D_MODEL), _const_spec((1, D_MODEL)), _const_spec(wm.shape), _const_spec(wg.shape),
                  _const_spec(wgu.shape), _const_spec((1, QK_W))],
        out_specs=[row(QK_W), row(QK_W), row(D_GLA), row(D_GLA), row(QK_W), row(D_S5)],
        out_shape=[jax.ShapeDtypeStruct((t, QK_W), act_dtype), jax.ShapeDtypeStruct((t, QK_W), act_dtype),
                   jax.ShapeDtypeStruct((t, D_GLA), act_dtype), jax.ShapeDtypeStruct((t, D_GLA), act_dtype),
                   jax.ShapeDtypeStruct((t, QK_W), F32), jax.ShapeDtypeStruct((t, D_S5), F32)],
        compiler_params=pltpu.CompilerParams(dimension_semantics=("arbitrary",), vmem_limit_bytes=VMEM_LIMIT),
        name="in_proj",
    )(x2d, g, wm, wg, wgu, gbias)


def _tri_mask(c):
    r = lax.broadcasted_iota(jnp.int32, (c, c), 0)
    s = lax.broadcasted_iota(jnp.int32, (c, c), 1)
    return r >= s


def _gla_head_chunk(q, k, v, la, s, mask):
    tri = jnp.where(mask, 1.0, 0.0).astype(BF16)
    hi = la.astype(BF16)
    r1 = la - hi.astype(F32)
    mid = r1.astype(BF16)
    lo = (r1 - mid.astype(F32)).astype(BF16)
    cum = _dot(tri, hi) + _dot(tri, mid) + _dot(tri, lo)
    c = cum.shape[0]
    last = cum[c - 1:c, :]
    qe = (q * jnp.exp(cum)).astype(BF16)
    ke = (k * jnp.exp(-cum)).astype(BF16)
    kd = (k * jnp.exp(last - cum)).astype(BF16)
    o = _dot(qe, s.astype(BF16))
    sc = lax.dot_general(qe, ke, (((1,), (1,)), ((), ())), preferred_element_type=F32)
    sc = jnp.where(mask, sc, 0.0).astype(BF16)
    o = o + _dot(sc, v)
    dec = jnp.exp(jnp.broadcast_to(last, (GLA_DK, GLA_DK)).T)
    s_new = (s * jnp.concatenate([dec, dec], axis=1)
             + lax.dot_general(kd, v, (((0,), (0,)), ((), ())), preferred_element_type=F32))
    return o, s_new


def _gla_finish(o, r, g):
    rf = r.astype(F32)
    return _rms(o, g) * (rf * jax.nn.sigmoid(rf))


def _gla_prompt_kernel(q_ref, k_ref, v_ref, r_ref, la_ref, g_ref, o_ref, sfin_ref, s_ref, *, n_chunks):
    j = pl.program_id(1)

    @pl.when(j == 0)
    def _():
        s_ref[...] = jnp.zeros_like(s_ref)

    mask = _tri_mask(GLA_CHUNK)
    g = g_ref[...]

    def chunk(ci, carry):
        r0 = pl.multiple_of(ci * GLA_CHUNK, GLA_CHUNK)
        rows = pl.ds(r0, GLA_CHUNK)
        for h in range(GLA_HEADS):
            kc = slice(h * GLA_DK, (h + 1) * GLA_DK)
            vc = slice(h * GLA_DV, (h + 1) * GLA_DV)
            o, s_new = _gla_head_chunk(q_ref[rows, kc].astype(F32), k_ref[rows, kc].astype(F32),
                                       v_ref[rows, vc], la_ref[rows, kc], s_ref[h], mask)
            s_ref[h] = s_new
            o_ref[rows, vc] = _gla_finish(o, r_ref[rows, vc], g).astype(o_ref.dtype)
        return carry

    lax.fori_loop(0, n_chunks, chunk, 0)

    @pl.when(j == pl.num_programs(1) - 1)
    def _():
        sfin_ref[0] = s_ref[...]


def _gla_prompt(q, k, v, r, la, g, batch, seq):
    rb = 4 * GLA_CHUNK
    nj = seq // rb
    row = lambda w: pl.BlockSpec((rb, w), lambda b, j: (b * nj + j, 0))
    return pl.pallas_call(
        functools.partial(_gla_prompt_kernel, n_chunks=rb // GLA_CHUNK),
        grid=(batch, nj),
        in_specs=[row(QK_W), row(QK_W), row(D_GLA), row(D_GLA), row(QK_W), _const_spec((1, GLA_DV))],
        out_specs=[row(D_GLA),
                   pl.BlockSpec((1, GLA_HEADS, GLA_DK, GLA_DV), lambda b, j: (b, 0, 0, 0))],
        out_shape=[jax.ShapeDtypeStruct((batch * seq, D_GLA), BF16),
                   jax.ShapeDtypeStruct((batch, GLA_HEADS, GLA_DK, GLA_DV), F32)],
        scratch_shapes=[pltpu.VMEM((GLA_HEADS, GLA_DK, GLA_DV), F32)],
        compiler_params=pltpu.CompilerParams(dimension_semantics=("arbitrary", "arbitrary"),
                                             vmem_limit_bytes=VMEM_LIMIT),
        name="gla_prompt",
    )(q, k, v, r, la, g)


def _gla_sample_kernel(q_ref, k_ref, v_ref, r_ref, la_ref, g_ref, s0_ref, o_ref, sfin_ref, *, n_seq, seq):
    cp = 2 * seq
    mask = _tri_mask(cp)
    g = g_ref[...]

    def pad(x):
        return jnp.concatenate([x, jnp.zeros_like(x)], axis=0)

    def one_seq(i, carry):
        r0 = pl.multiple_of(i * seq, seq)
        rows = pl.ds(r0, seq)
        for h in range(GLA_HEADS):
            kc = slice(h * GLA_DK, (h + 1) * GLA_DK)
            vc = slice(h * GLA_DV, (h + 1) * GLA_DV)
            o, s_new = _gla_head_chunk(pad(q_ref[rows, kc]), pad(k_ref[rows, kc]),
                                       pad(v_ref[rows, vc]).astype(BF16), pad(la_ref[rows, kc]),
                                       s0_ref[i, h], mask)
            sfin_ref[i, h] = s_new
            o_ref[rows, vc] = _gla_finish(o[:seq], r_ref[rows, vc], g).astype(o_ref.dtype)
        return carry

    lax.fori_loop(0, n_seq, one_seq, 0)


def _gla_sample(q, k, v, r, la, g, s0, batch, seq):
    ns = 8
    rb = ns * seq
    row = lambda w: pl.BlockSpec((rb, w), lambda i: (i, 0))
    st = pl.BlockSpec((ns, GLA_HEADS, GLA_DK, GLA_DV), lambda i: (i, 0, 0, 0))
    return pl.pallas_call(
        functools.partial(_gla_sample_kernel, n_seq=ns, seq=seq),
        grid=(batch // ns,),
        in_specs=[row(QK_W), row(QK_W), row(D_GLA), row(D_GLA), row(QK_W), _const_spec((1, GLA_DV)), st],
        out_specs=[row(D_GLA), st],
        out_shape=[jax.ShapeDtypeStruct((batch * seq, D_GLA), BF16),
                   jax.ShapeDtypeStruct((batch, GLA_HEADS, GLA_DK, GLA_DV), F32)],
        compiler_params=pltpu.CompilerParams(dimension_semantics=("arbitrary",), vmem_limit_bytes=VMEM_LIMIT),
        name="gla_sample",
    )(q, k, v, r, la, g, s0)


def _s5_kernel(u_ref, wbu_ref, wc_ref, are_ref, aim_ref, d_ref, h0r_ref, h0i_ref,
               y_ref, sre_ref, sim_ref, bu_ref, xs_ref, car_ref, *, nb, tc, has_state):
    j = pl.program_id(2)

    @pl.when(j == 0)
    def _():
        if has_state:
            car_ref[0] = h0r_ref[...]
            car_ref[1] = h0i_ref[...]
        else:
            car_ref[...] = jnp.zeros_like(car_ref)

    u2 = u_ref[...].reshape(nb * tc, LANES)
    ub = u2.astype(BF16)
    nl = S5_SL // LANES
    for l in range(2 * nl):
        bu_ref[l] = _dot(ub, wbu_ref[0, :, l * LANES:(l + 1) * LANES])
    a_r = [jnp.broadcast_to(are_ref[0, :, l * LANES:(l + 1) * LANES], (nb, LANES)) for l in range(nl)]
    a_i = [jnp.broadcast_to(aim_ref[0, :, l * LANES:(l + 1) * LANES], (nb, LANES)) for l in range(nl)]

    def step(t, carry):
        rows = pl.ds(t, nb, stride=tc)
        out = []
        for l in range(nl):
            xr, xi = carry[2 * l], carry[2 * l + 1]
            nr = a_r[l] * xr - a_i[l] * xi + bu_ref[l, rows, :]
            ni = a_r[l] * xi + a_i[l] * xr + bu_ref[nl + l, rows, :]
            xs_ref[l, rows, :] = nr
            xs_ref[nl + l, rows, :] = ni
            out += [nr, ni]
        return tuple(out)

    init = []
    for l in range(nl):
        init += [car_ref[0, :, l * LANES:(l + 1) * LANES], car_ref[1, :, l * LANES:(l + 1) * LANES]]
    fin = lax.fori_loop(0, tc, step, tuple(init), unroll=8)
    xr = jnp.concatenate([fin[2 * l] for l in range(nl)], axis=1)
    xi = jnp.concatenate([fin[2 * l + 1] for l in range(nl)], axis=1)
    car_ref[0] = xr
    car_ref[1] = xi
    y = d_ref[0] * u2
    for l in range(2 * nl):
        y = y + _dot(xs_ref[l].astype(BF16), wc_ref[0, l * LANES:(l + 1) * LANES, :])
    y_ref[...] = y.reshape(nb, tc, LANES)

    @pl.when(j == pl.num_programs(2) - 1)
    def _():
        sre_ref[...] = xr
        sim_ref[...] = xi


def _s5(u3d, wbu, wc, a_re, a_im, dsk, h0r, h0i, nb, tc, has_state):
    batch, seq, _ = u3d.shape
    grid = (S5_NGB, batch // nb, seq // tc)
    st = pl.BlockSpec((nb, S5_SL), lambda g, b, j: (b, g))
    par = lambda w: pl.BlockSpec((1, 1, w), lambda g, b, j: (g, 0, 0))
    ublk = pl.BlockSpec((nb, tc, LANES), lambda g, b, j: (b, j, g))
    return pl.pallas_call(
        functools.partial(_s5_kernel, nb=nb, tc=tc, has_state=has_state),
        grid=grid,
        in_specs=[ublk,
                  pl.BlockSpec((1, LANES, 2 * S5_SL), lambda g, b, j: (g, 0, 0)),
                  pl.BlockSpec((1, 2 * S5_SL, LANES), lambda g, b, j: (g, 0, 0)),
                  par(S5_SL), par(S5_SL), par(LANES), st, st],
        out_specs=[ublk, st, st],
        out_shape=[jax.ShapeDtypeStruct(u3d.shape, F32),
                   jax.ShapeDtypeStruct((batch, S5_GROUPS * S5_STATE), F32),
                   jax.ShapeDtypeStruct((batch, S5_GROUPS * S5_STATE), F32)],
        scratch_shapes=[pltpu.VMEM((2 * S5_SL // LANES, nb * tc, LANES), F32),
                        pltpu.VMEM((2 * S5_SL // LANES, nb * tc, LANES), F32),
                        pltpu.VMEM((2, nb, S5_SL), F32)],
        compiler_params=pltpu.CompilerParams(dimension_semantics=("arbitrary",) * 3,
                                             vmem_limit_bytes=VMEM_LIMIT),
        name="s5_state" if has_state else "s5_zero",
    )(u3d, wbu, wc, a_re, a_im, dsk, h0r, h0i)


def _mix_out_kernel(xp_ref, xs_ref, ogp_ref, ogs_ref, ysp_ref, yss_ref, glu_w_ref, glu_b_ref, s5n_ref, wo_ref,
                    nffn_ref, wr_ref, h1_ref, hn_ref, rt_ref, *, n_prompt_tiles):
    is_p = pl.program_id(0) < n_prompt_tiles
    x = jnp.where(is_p, xp_ref[...], xs_ref[...])
    og = jnp.where(is_p, ogp_ref[...], ogs_ref[...])
    y = jax.nn.gelu(jnp.where(is_p, ysp_ref[...], yss_ref[...]))
    y = y * jax.nn.sigmoid(_dot(y.astype(BF16), glu_w_ref[...]) + glu_b_ref[...])
    yn = _rms(y, s5n_ref[...]).astype(BF16)
    mix = _dot(og, wo_ref[0:D_GLA, :]) + _dot(yn, wo_ref[D_GLA:D_GLA + D_S5, :])
    h1 = x + mix
    h1_ref[...] = h1
    hn = _rms(h1, nffn_ref[...])
    hn_ref[...] = hn
    logits = jnp.dot(hn, wr_ref[...], preferred_element_type=F32, precision=lax.Precision.HIGHEST)
    rt_ref[...] = _route(logits)


def _route(logits):
    col = lax.broadcasted_iota(jnp.int32, logits.shape, 1)
    colf = col.astype(F32)
    neg = -jnp.inf

    def first_argmax(vals):
        m = jnp.max(vals, axis=-1, keepdims=True)
        idx = jnp.min(jnp.where(vals == m, colf, float(LANES)), axis=-1, keepdims=True)
        return m, idx

    lg = jnp.where(col < N_EGROUPS, logits, neg)
    gmax, gsel = first_argmax(lg)
    p_g = 1.0 / jnp.sum(jnp.exp(lg - gmax), axis=-1, keepdims=True)
    ecol = col - N_EGROUPS
    egrp = (ecol >> 3).astype(F32)
    in_group = (ecol >= 0) & (ecol < N_EXPERTS) & (egrp == gsel)
    le = jnp.where(in_group, logits, neg)
    m1, i1 = first_argmax(le)
    le2 = jnp.where(colf == i1, neg, le)
    m2, i2 = first_argmax(le2)
    e2 = jnp.exp(m2 - m1)
    den = 1.0 + e2
    w1 = p_g * (1.0 / den)
    w2 = p_g * (e2 / den)
    e1f = i1 - float(N_EGROUPS)
    e2f = i2 - float(N_EGROUPS)
    out = jnp.where(col == 0, e1f, jnp.where(col == 1, e2f, jnp.where(col == 2, w1, jnp.where(col == 3, w2, 0.0))))
    return out


def _mix_out(xp, xs, ogp, ogs, ysp, yss, glu_w, glu_b, s5n, wo, nffn, wr):
    tm = TOK_TILE
    npt, nst = xp.shape[0] // tm, xs.shape[0] // tm
    t = (npt + nst) * tm
    row = lambda w: pl.BlockSpec((tm, w), lambda i: (i, 0))
    prow = lambda w: pl.BlockSpec((tm, w), lambda i: (jnp.minimum(i, npt - 1), 0))
    srow = lambda w: pl.BlockSpec((tm, w), lambda i: (jnp.maximum(i - npt, 0), 0))
    return pl.pallas_call(
        functools.partial(_mix_out_kernel, n_prompt_tiles=npt),
        grid=(npt + nst,),
        in_specs=[prow(D_MODEL), srow(D_MODEL), prow(D_GLA), srow(D_GLA), prow(D_S5), srow(D_S5),
                  _const_spec(glu_w.shape), _const_spec((1, D_S5)), _const_spec((1, D_S5)),
                  _const_spec(wo.shape), _const_spec((1, D_MODEL)), _const_spec(wr.shape)],
        out_specs=[row(D_MODEL), row(D_MODEL), row(LANES)],
        out_shape=[jax.ShapeDtypeStruct((t, D_MODEL), F32), jax.ShapeDtypeStruct((t, D_MODEL), F32),
                   jax.ShapeDtypeStruct((t, LANES), F32)],
        compiler_params=pltpu.CompilerParams(dimension_semantics=("arbitrary",), vmem_limit_bytes=VMEM_LIMIT),
        name="mix_out",
    )(xp, xs, ogp, ogs, ysp, yss, glu_w, glu_b, s5n, wo, nffn, wr)


DMA_UNROLL = 8


def _moe_kernel(be_ref, nv_ref, sa_ref, hn_hbm, wg_ref, wu_ref, wd_ref, out_hbm,
                xb_ref, yb_ref, wgb_ref, wub_ref, wdb_ref, gsem, ssem):
    b = pl.program_id(0)
    nb = pl.num_programs(0)
    n = nv_ref[b]
    slot = lax.rem(b, 2)
    t_all = hn_hbm.shape[0]

    def round_up(cnt):
        return (cnt + DMA_UNROLL - 1) // DMA_UNROLL * DMA_UNROLL

    def issue_gather(blk, sl):
        base = blk * MOE_BLK

        def grp(g, c):
            for j in range(DMA_UNROLL):
                i = g * DMA_UNROLL + j
                tok = sa_ref[base + i] >> 1
                pltpu.make_async_copy(hn_hbm.at[pl.ds(tok, 1), :], xb_ref.at[sl, pl.ds(i, 1), :],
                                      gsem.at[sl]).start()
            return c

        lax.fori_loop(0, round_up(nv_ref[blk]) // DMA_UNROLL, grp, 0)

    def wait_gather(cnt, sl):
        rows = pl.ds(0, pl.multiple_of(round_up(cnt), DMA_UNROLL))
        pltpu.make_async_copy(hn_hbm.at[rows, :], xb_ref.at[sl, rows, :], gsem.at[sl]).wait()

    def scatter_row(base, i):
        a = sa_ref[base + i]
        dst = (a & 1) * t_all + (a >> 1)
        pltpu.make_async_copy(yb_ref.at[pl.ds(i, 1), :], out_hbm.at[pl.ds(dst, 1), :], ssem.at[0]).start()

    def issue_scatter(blk, cnt):
        base = blk * MOE_BLK
        full = cnt // DMA_UNROLL

        def grp(g, c):
            for j in range(DMA_UNROLL):
                scatter_row(base, g * DMA_UNROLL + j)
            return c

        def one(i, c):
            scatter_row(base, i)
            return c

        lax.fori_loop(0, full, grp, 0)
        lax.fori_loop(full * DMA_UNROLL, cnt, one, 0)

    def wait_scatter(cnt):
        full = cnt // DMA_UNROLL * DMA_UNROLL

        @pl.when(full > 0)
        def _():
            rows = pl.ds(0, pl.multiple_of(full, DMA_UNROLL))
            pltpu.make_async_copy(yb_ref.at[rows, :], out_hbm.at[rows, :], ssem.at[0]).wait()

        def one(i, c):
            pltpu.make_async_copy(yb_ref.at[pl.ds(i, 1), :], out_hbm.at[pl.ds(0, 1), :], ssem.at[0]).wait()
            return c

        lax.fori_loop(full, cnt, one, 0)

    @pl.when(b == 0)
    def _():
        xb_ref[...] = jnp.zeros_like(xb_ref)
        issue_gather(0, 0)

    @pl.when(b + 1 < nb)
    def _():
        issue_gather(b + 1, 1 - slot)

    n_prev = nv_ref[jnp.maximum(b - 1, 0)]
    has_prev = (b > 0) & (n_prev > 0)

    @pl.when(n > 0)
    def _():
        prev_e = be_ref[jnp.maximum(b - 1, 0)]

        @pl.when((b == 0) | (prev_e != be_ref[b]))
        def _():
            wgb_ref[...] = wg_ref[...].astype(BF16)
            wub_ref[...] = wu_ref[...].astype(BF16)
            wdb_ref[...] = wd_ref[...].astype(BF16)

        wait_gather(n, slot)
        x = xb_ref[slot].astype(BF16)
        gate = _dot(x, wgb_ref[...])
        up = _dot(x, wub_ref[...])
        hid = (gate * jax.nn.sigmoid(gate) * up).astype(BF16)

        @pl.when(has_prev)
        def _():
            wait_scatter(n_prev)

        yb_ref[...] = _dot(hid, wdb_ref[...])
        issue_scatter(b, n)

        @pl.when(b == nb - 1)
        def _():
            wait_scatter(n)

    @pl.when((n == 0) & has_prev)
    def _():
        wait_scatter(n_prev)


def _moe(hn_all, blk_e, blk_n, slot_a, w_gate, w_up, w_down, n_out_rows):
    nblk = blk_e.shape[0]
    wspec = lambda s: pl.BlockSpec((None,) + s, lambda b, be, nv, sa: (be[b], 0, 0))
    grid_spec = pltpu.PrefetchScalarGridSpec(
        num_scalar_prefetch=3,
        grid=(nblk,),
        in_specs=[pl.BlockSpec(memory_space=pl.ANY),
                  wspec((D_MODEL, D_EXPERT)), wspec((D_MODEL, D_EXPERT)), wspec((D_EXPERT, D_MODEL))],
        out_specs=pl.BlockSpec(memory_space=pl.ANY),
        scratch_shapes=[pltpu.VMEM((2, MOE_BLK, D_MODEL), F32), pltpu.VMEM((MOE_BLK, D_MODEL), F32),
                        pltpu.VMEM((D_MODEL, D_EXPERT), BF16), pltpu.VMEM((D_MODEL, D_EXPERT), BF16),
                        pltpu.VMEM((D_EXPERT, D_MODEL), BF16), pltpu.SemaphoreType.DMA((2,)),
                        pltpu.SemaphoreType.DMA((1,))],
    )
    return pl.pallas_call(
        _moe_kernel,
        grid_spec=grid_spec,
        out_shape=jax.ShapeDtypeStruct((n_out_rows, D_MODEL), F32),
        compiler_params=pltpu.CompilerParams(dimension_semantics=("arbitrary",), vmem_limit_bytes=VMEM_LIMIT),
        name="moe_experts",
    )(blk_e, blk_n, slot_a, hn_all, w_gate, w_up, w_down)


def _moe_plan(rt_all):
    t_all = rt_all.shape[0]
    n_assign = 2 * t_all
    e_flat = rt_all[:, 0:2].astype(jnp.int32).reshape(-1)
    onehot = (e_flat[:, None] == jnp.arange(N_EXPERTS, dtype=jnp.int32)[None, :]).astype(jnp.int32)
    csum = jnp.cumsum(onehot, axis=0)
    counts = csum[-1]
    rank = jnp.take_along_axis(csum, e_flat[:, None], axis=1)[:, 0] - 1
    padded = (counts + MOE_BLK - 1) // MOE_BLK * MOE_BLK
    pad_end = jnp.cumsum(padded)
    pad_start = pad_end - padded
    dest = pad_start[e_flat] + rank
    nblk = -(-(n_assign + N_EXPERTS * (MOE_BLK - 1)) // MOE_BLK)
    n_slots = nblk * MOE_BLK
    slot_a = jnp.zeros((n_slots,), jnp.int32).at[dest].set(jnp.arange(n_assign, dtype=jnp.int32))
    blk_start = jnp.arange(nblk, dtype=jnp.int32) * MOE_BLK
    blk_e = jnp.minimum(jnp.searchsorted(pad_end, blk_start, side='right'), N_EXPERTS - 1).astype(jnp.int32)
    blk_n = jnp.clip(counts[blk_e] - (blk_start - pad_start[blk_e]), 0, MOE_BLK).astype(jnp.int32)
    return blk_e, blk_n, slot_a


def _ple_out_kernel(h1_ref, y0_ref, y1_ref, rt_ref, p_ref, nple_ref, wpg_ref, wp_ref, nfin_ref, o_ref):
    rt = rt_ref[...]
    h2 = h1_ref[...] + rt[:, 2:3] * y0_ref[...] + rt[:, 3:4] * y1_ref[...]
    gate = jax.nn.sigmoid(_dot(_rms(h2, nple_ref[...]).astype(BF16), wpg_ref[...]))
    h3 = h2 + _dot(p_ref[...].astype(BF16), wp_ref[...]) * gate
    o_ref[...] = _rms(h3, nfin_ref[...])


def _ple_out(h1, ymoe, rt, p2d, nple, wpg, wp, nfin, row_off, t_all):
    t = p2d.shape[0]
    tm = TOK_TILE
    ob = row_off // tm
    kb = t_all // tm
    row = lambda w: pl.BlockSpec((tm, w), lambda i: (i, 0))
    orow = lambda w: pl.BlockSpec((tm, w), lambda i: (i + ob, 0))
    return pl.pallas_call(
        _ple_out_kernel,
        grid=(t // tm,),
        in_specs=[orow(D_MODEL), orow(D_MODEL),
                  pl.BlockSpec((tm, D_MODEL), lambda i: (i + ob + kb, 0)),
                  orow(LANES), row(D_PLE), _const_spec((1, D_MODEL)), _const_spec(wpg.shape),
                  _const_spec(wp.shape), _const_spec((1, D_MODEL))],
        out_specs=row(D_MODEL),
        out_shape=jax.ShapeDtypeStruct((t, D_MODEL), F32),
        compiler_params=pltpu.CompilerParams(dimension_semantics=("arbitrary",), vmem_limit_bytes=VMEM_LIMIT),
        name="ple_out",
    )(h1, ymoe, ymoe, rt, p2d, nple, wpg, wp, nfin)


def _s5_params(lam_re, lam_im, log_dt, b_re, b_im, c_re, c_im, d_skip):
    dt = jnp.exp(log_dt)[:, None]
    mag = jnp.exp(lam_re * dt)
    ab_re = mag * jnp.cos(lam_im * dt)
    ab_im = mag * jnp.sin(lam_im * dt)
    den = lam_re * lam_re + lam_im * lam_im
    nr = ab_re - 1.0
    f_re = (nr * lam_re + ab_im * lam_im) / den
    f_im = (ab_im * lam_re - nr * lam_im) / den
    bb_re = f_re[..., None] * b_re - f_im[..., None] * b_im
    bb_im = f_re[..., None] * b_im + f_im[..., None] * b_re
    eye = jnp.eye(S5_GB, dtype=F32)

    def bu_w(bb):
        bb = bb.reshape(S5_NGB, S5_GB, S5_STATE, S5_GROUP)
        w = jnp.einsum('nlph,lm->nlhmp', bb, eye)
        return w.reshape(S5_NGB, S5_GB * S5_GROUP, S5_SL)

    def c_w(c):
        c = c.reshape(S5_NGB, S5_GB, S5_GROUP, S5_STATE)
        w = jnp.einsum('nlhp,lm->nlpmh', c, eye)
        return w.reshape(S5_NGB, S5_SL, S5_GB * S5_GROUP)

    wbu = jnp.concatenate([bu_w(bb_re), bu_w(bb_im)], axis=2).astype(BF16)
    wc = jnp.concatenate([c_w(c_re), -c_w(c_im)], axis=1).astype(BF16)
    a_re = ab_re.reshape(S5_NGB, 1, S5_SL)
    a_im = ab_im.reshape(S5_NGB, 1, S5_SL)
    dsk = d_skip.reshape(S5_NGB, 1, S5_GB * S5_GROUP)
    return wbu, wc, a_re, a_im, dsk


def kernel(x_prompt, x_sample, p_prompt, p_sample, state_gla, state_s5_re, state_s5_im, norm_mix, w_in, gla_w_gate_up, gla_gate_bias, gla_norm, s5_lam_re, s5_lam_im, s5_log_dt, s5_b_re, s5_b_im, s5_c_re, s5_c_im, s5_d, s5_glu_w, s5_glu_b, s5_norm, w_out, norm_ffn, router_group, router_expert, w_gate, w_up, w_down, norm_ple, w_ple, w_ple_gate, norm_final):
    depth = w_in.shape[0]
    assert depth == 1
    i = 0
    bp, lp, _ = x_prompt.shape
    bs, ls, _ = x_sample.shape
    tp, ts = bp * lp, bs * ls
    t_all = tp + ts

    n_qkvr = 2 * QK_W + 2 * D_GLA
    wi = w_in[i]
    wm = jnp.concatenate([wi[:, :n_qkvr], wi[:, n_qkvr + GLA_RANK:]], axis=1).astype(BF16)
    wg = jnp.pad(wi[:, n_qkvr:n_qkvr + GLA_RANK], ((0, 0), (0, LANES - GLA_RANK))).astype(BF16)
    wgu = jnp.pad(gla_w_gate_up[i], ((0, LANES - GLA_RANK), (0, 0))).astype(BF16)
    gbias = gla_gate_bias[i].reshape(1, QK_W)
    wbu, wc, a_re, a_im, dsk = _s5_params(s5_lam_re[i], s5_lam_im[i], s5_log_dt[i], s5_b_re[i], s5_b_im[i],
                                          s5_c_re[i], s5_c_im[i], s5_d[i])
    glu_w = s5_glu_w[i].astype(BF16)
    wo = w_out[i].astype(BF16)
    wr = jnp.pad(jnp.concatenate([router_group[i], router_expert[i]], axis=1),
                 ((0, 0), (0, LANES - N_EGROUPS - N_EXPERTS)))
    wpg = w_ple_gate[i].astype(BF16)
    wp = w_ple[i].astype(BF16)
    vec = lambda a: a.reshape(1, -1)

    xp = x_prompt.reshape(tp, D_MODEL)
    xs = x_sample.reshape(ts, D_MODEL)

    qp, kp, vp, rp, lap, up = _in_proj(xp, vec(norm_mix[i]), wm, wg, wgu, gbias, BF16)
    qs, ks, vs, rs, las, us = _in_proj(xs, vec(norm_mix[i]), wm, wg, wgu, gbias, F32)
    ogp, gla_p = _gla_prompt(qp, kp, vp, rp, lap, vec(gla_norm[i]), bp, lp)
    ogs, gla_s = _gla_sample(qs, ks, vs, rs, las, vec(gla_norm[i]), state_gla[i], bs, ls)
    zst = jnp.zeros((bp, S5_GROUPS * S5_STATE), F32)
    ysp, re_p, im_p = _s5(up.reshape(bp, lp, D_S5), wbu, wc, a_re, a_im, dsk, zst, zst,
                          nb=bp, tc=256, has_state=False)
    yss, re_s, im_s = _s5(us.reshape(bs, ls, D_S5), wbu, wc, a_re, a_im, dsk,
                          state_s5_re[i].reshape(bs, -1), state_s5_im[i].reshape(bs, -1),
                          nb=32, tc=ls, has_state=True)

    h1, hn_all, rt_all = _mix_out(xp, xs, ogp, ogs, ysp.reshape(tp, D_S5), yss.reshape(ts, D_S5), glu_w,
                                  vec(s5_glu_b[i]), vec(s5_norm[i]), wo, vec(norm_ffn[i]), wr)

    blk_e, blk_n, slot_a = _moe_plan(rt_all)
    ymoe = _moe(hn_all, blk_e, blk_n, slot_a, w_gate[i], w_up[i], w_down[i], 2 * t_all)

    po = functools.partial(_ple_out, h1, ymoe, rt_all, nple=vec(norm_ple[i]), wpg=wpg, wp=wp,
                           nfin=vec(norm_final), t_all=t_all)
    y_p = po(p_prompt[i].reshape(tp, D_PLE), row_off=0)
    y_s = po(p_sample[i].reshape(ts, D_PLE), row_off=tp)

    s5shape = lambda a, b: a.reshape(1, b, S5_GROUPS, S5_STATE)
    return (y_p.reshape(bp, lp, D_MODEL), y_s.reshape(bs, ls, D_MODEL),
            gla_p[None], s5shape(re_p, bp), s5shape(im_p, bp),
            gla_s[None], s5shape(re_s, bs), s5shape(im_s, bs))
```

```python
import functools
import math

import jax
import jax.numpy as jnp
from jax import lax
from jax.experimental import pallas as pl
from jax.experimental.pallas import tpu as pltpu

F32 = jnp.float32
BF16 = jnp.bfloat16

D_MODEL = 2048
D_GLA = 1024
D_S5 = 1024
GLA_HEADS = 4
GLA_DV = 256
GLA_DK = 128
GLA_RANK = 16
GLA_CHUNK = 64
S5_GROUP = 16
S5_GROUPS = 64
S5_STATE = 64
N_EGROUPS = 4
N_EPG = 8
N_EXPERTS = 32
D_EXPERT = 512
D_PLE = 256
EPS = 1e-6

LANES = 128
QK_W = GLA_HEADS * GLA_DK
S5_GB = 8
S5_NGB = S5_GROUPS // S5_GB
S5_SL = S5_GB * S5_STATE
TOK_TILE = 256
MOE_BLK = 256
VMEM_LIMIT = 56 * 1024 * 1024


def _const_spec(shape):
    nd = len(shape)
    return pl.BlockSpec(shape, lambda *_: (0,) * nd, pipeline_mode=pl.Buffered(1))


def _rms(x, g):
    return x * lax.rsqrt(jnp.mean(x * x, axis=-1, keepdims=True) + EPS) * g


def _dot(a, b):
    return jnp.dot(a, b, preferred_element_type=F32)


def _log_sigmoid(x):
    return -(jnp.maximum(-x, 0.0) + jnp.log1p(jnp.exp(-jnp.abs(x))))


def _in_proj_kernel(x_ref, g_ref, wm_ref, wg_ref, wgu_ref, gb_ref,
                    q_ref, k_ref, v_ref, r_ref, la_ref, u_ref):
    hb = _rms(x_ref[...], g_ref[...]).astype(BF16)

    def seg(a, b):
        return _dot(hb, wm_ref[:, a:b])

    q_ref[...] = (seg(0, QK_W) * (GLA_DK ** -0.5)).astype(q_ref.dtype)
    k_ref[...] = seg(QK_W, 2 * QK_W).astype(k_ref.dtype)
    v_ref[...] = seg(2 * QK_W, 2 * QK_W + D_GLA).astype(v_ref.dtype)
    r_ref[...] = seg(2 * QK_W + D_GLA, 2 * QK_W + 2 * D_GLA).astype(r_ref.dtype)
    u_ref[...] = seg(2 * QK_W + 2 * D_GLA, 2 * QK_W + 2 * D_GLA + D_S5)
    zg = _dot(hb, wg_ref[...])
    xg = _dot(zg.astype(BF16), wgu_ref[...]) + gb_ref[...]
    la_ref[...] = _log_sigmoid(xg) * (1.0 / 16.0)


def _in_proj(x2d, g, wm, wg, wgu, gbias, act_dtype):
    t = x2d.shape[0]
    tm = TOK_TILE
    row = lambda w: pl.BlockSpec((tm, w), lambda i: (i, 0))
    return pl.pallas_call(
        _in_proj_kernel,
        grid=(t // tm,),
        in_specs=[row(D_MODEL), _const_spec((1, D_MODEL)), _const_spec(wm.shape), _const_spec(wg.shape),
                  _const_spec(wgu.shape), _const_spec((1, QK_W))],
        out_specs=[row(QK_W), row(QK_W), row(D_GLA), row(D_GLA), row(QK_W), row(D_S5)],
        out_shape=[jax.ShapeDtypeStruct((t, QK_W), act_dtype), jax.ShapeDtypeStruct((t, QK_W), act_dtype),
                   jax.ShapeDtypeStruct((t, D_GLA), act_dtype), jax.ShapeDtypeStruct((t, D_GLA), act_dtype),
                   jax.ShapeDtypeStruct((t, QK_W), F32), jax.ShapeDtypeStruct((t, D_S5), F32)],
        compiler_params=pltpu.CompilerParams(dimension_semantics=("arbitrary",), vmem_limit_bytes=VMEM_LIMIT),
        name="in_proj",
    )(x2d, g, wm, wg, wgu, gbias)


def _tri_mask(c):
    r = lax.broadcasted_iota(jnp.int32, (c, c), 0)
    s = lax.broadcasted_iota(jnp.int32, (c, c), 1)
    return r >= s


def _gla_head_chunk(q, k, v, la, s, mask):
    tri = jnp.where(mask, 1.0, 0.0).astype(BF16)
    hi = la.astype(BF16)
    r1 = la - hi.astype(F32)
    mid = r1.astype(BF16)
    lo = (r1 - mid.astype(F32)).astype(BF16)
    cum = _dot(tri, hi) + _dot(tri, mid) + _dot(tri, lo)
    c = cum.shape[0]
    last = cum[c - 1:c, :]
    qe = (q * jnp.exp(cum)).astype(BF16)
    ke = (k * jnp.exp(-cum)).astype(BF16)
    kd = (k * jnp.exp(last - cum)).astype(BF16)
    o = _dot(qe, s.astype(BF16))
    sc = lax.dot_general(qe, ke, (((1,), (1,)), ((), ())), preferred_element_type=F32)
    sc = jnp.where(mask, sc, 0.0).astype(BF16)
    o = o + _dot(sc, v)
    dec = jnp.exp(jnp.broadcast_to(last, (GLA_DK, GLA_DK)).T)
    s_new = (s * jnp.concatenate([dec, dec], axis=1)
             + lax.dot_general(kd, v, (((0,), (0,)), ((), ())), preferred_element_type=F32))
    return o, s_new


def _gla_finish(o, r, g):
    rf = r.astype(F32)
    return _rms(o, g) * (rf * jax.nn.sigmoid(rf))


def _gla_prompt_kernel(q_ref, k_ref, v_ref, r_ref, la_ref, g_ref, o_ref, sfin_ref, s_ref, *, n_chunks):
    j = pl.program_id(1)

    @pl.when(j == 0)
    def _():
        s_ref[...] = jnp.zeros_like(s_ref)

    mask = _tri_mask(GLA_CHUNK)
    g = g_ref[...]

    def chunk(ci, carry):
        r0 = pl.multiple_of(ci * GLA_CHUNK, GLA_CHUNK)
        rows = pl.ds(r0, GLA_CHUNK)
        for h in range(GLA_HEADS):
            kc = slice(h * GLA_DK, (h + 1) * GLA_DK)
            vc = slice(h * GLA_DV, (h + 1) * GLA_DV)
            o, s_new = _gla_head_chunk(q_ref[rows, kc].astype(F32), k_ref[rows, kc].astype(F32),
                                       v_ref[rows, vc], la_ref[rows, kc], s_ref[h], mask)
            s_ref[h] = s_new
            o_ref[rows, vc] = _gla_finish(o, r_ref[rows, vc], g).astype(o_ref.dtype)
        return carry

    lax.fori_loop(0, n_chunks, chunk, 0)

    @pl.when(j == pl.num_programs(1) - 1)
    def _():
        sfin_ref[0] = s_ref[...]


def _gla_prompt(q, k, v, r, la, g, batch, seq):
    rb = 4 * GLA_CHUNK
    nj = seq // rb
    row = lambda w: pl.BlockSpec((rb, w), lambda b, j: (b * nj + j, 0))
    return pl.pallas_call(
        functools.partial(_gla_prompt_kernel, n_chunks=rb // GLA_CHUNK),
        grid=(batch, nj),
        in_specs=[row(QK_W), row(QK_W), row(D_GLA), row(D_GLA), row(QK_W), _const_spec((1, GLA_DV))],
        out_specs=[row(D_GLA),
                   pl.BlockSpec((1, GLA_HEADS, GLA_DK, GLA_DV), lambda b, j: (b, 0, 0, 0))],
        out_shape=[jax.ShapeDtypeStruct((batch * seq, D_GLA), BF16),
                   jax.ShapeDtypeStruct((batch, GLA_HEADS, GLA_DK, GLA_DV), F32)],
        scratch_shapes=[pltpu.VMEM((GLA_HEADS, GLA_DK, GLA_DV), F32)],
        compiler_params=pltpu.CompilerParams(dimension_semantics=("arbitrary", "arbitrary"),
                                             vmem_limit_bytes=VMEM_LIMIT),
        name="gla_prompt",
    )(q, k, v, r, la, g)


def _gla_sample_kernel(q_ref, k_ref, v_ref, r_ref, la_ref, g_ref, s0_ref, o_ref, sfin_ref, *, n_seq, seq):
    cp = 2 * seq
    mask = _tri_mask(cp)
    g = g_ref[...]

    def pad(x):
        return jnp.concatenate([x, jnp.zeros_like(x)], axis=0)

    def one_seq(i, carry):
        r0 = pl.multiple_of(i * seq, seq)
        rows = pl.ds(r0, seq)
        for h in range(GLA_HEADS):
            kc = slice(h * GLA_DK, (h + 1) * GLA_DK)
            vc = slice(h * GLA_DV, (h + 1) * GLA_DV)
            o, s_new = _gla_head_chunk(pad(q_ref[rows, kc]), pad(k_ref[rows, kc]),
                                       pad(v_ref[rows, vc]).astype(BF16), pad(la_ref[rows, kc]),
                                       s0_ref[i, h], mask)
            sfin_ref[i, h] = s_new
            o_ref[rows, vc] = _gla_finish(o[:seq], r_ref[rows, vc], g).astype(o_ref.dtype)
        return carry

    lax.fori_loop(0, n_seq, one_seq, 0)


def _gla_sample(q, k, v, r, la, g, s0, batch, seq):
    ns = 8
    rb = ns * seq
    row = lambda w: pl.BlockSpec((rb, w), lambda i: (i, 0))
    st = pl.BlockSpec((ns, GLA_HEADS, GLA_DK, GLA_DV), lambda i: (i, 0, 0, 0))
    return pl.pallas_call(
        functools.partial(_gla_sample_kernel, n_seq=ns, seq=seq),
        grid=(batch // ns,),
        in_specs=[row(QK_W), row(QK_W), row(D_GLA), row(D_GLA), row(QK_W), _const_spec((1, GLA_DV)), st],
        out_specs=[row(D_GLA), st],
        out_shape=[jax.ShapeDtypeStruct((batch * seq, D_GLA), BF16),
                   jax.ShapeDtypeStruct((batch, GLA_HEADS, GLA_DK, GLA_DV), F32)],
        compiler_params=pltpu.CompilerParams(dimension_semantics=("arbitrary",), vmem_limit_bytes=VMEM_LIMIT),
        name="gla_sample",
    )(q, k, v, r, la, g, s0)


def _s5_kernel(u_ref, wbu_ref, wc_ref, are_ref, aim_ref, d_ref, h0r_ref, h0i_ref,
               y_ref, sre_ref, sim_ref, bu_ref, xs_ref, car_ref, *, nb, tc, has_state):
    j = pl.program_id(2)

    @pl.when(j == 0)
    def _():
        if has_state:
            car_ref[0] = h0r_ref[...]
            car_ref[1] = h0i_ref[...]
        else:
            car_ref[...] = jnp.zeros_like(car_ref)

    u2 = u_ref[...].reshape(nb * tc, LANES)
    ub = u2.astype(BF16)
    nl = S5_SL // LANES
    for l in range(2 * nl):
        bu_ref[l] = _dot(ub, wbu_ref[0, :, l * LANES:(l + 1) * LANES])
    a_r = [jnp.broadcast_to(are_ref[0, :, l * LANES:(l + 1) * LANES], (nb, LANES)) for l in range(nl)]
    a_i = [jnp.broadcast_to(aim_ref[0, :, l * LANES:(l + 1) * LANES], (nb, LANES)) for l in range(nl)]

    def step(t, carry):
        rows = pl.ds(t, nb, stride=tc)
        out = []
        for l in range(nl):
            xr, xi = carry[2 * l], carry[2 * l + 1]
            nr = a_r[l] * xr - a_i[l] * xi + bu_ref[l, rows, :]
            ni = a_r[l] * xi + a_i[l] * xr + bu_ref[nl + l, rows, :]
            xs_ref[l, rows, :] = nr
            xs_ref[nl + l, rows, :] = ni
            out += [nr, ni]
        return tuple(out)

    init = []
    for l in range(nl):
        init += [car_ref[0, :, l * LANES:(l + 1) * LANES], car_ref[1, :, l * LANES:(l + 1) * LANES]]
    fin = lax.fori_loop(0, tc, step, tuple(init), unroll=8)
    xr = jnp.concatenate([fin[2 * l] for l in range(nl)], axis=1)
    xi = jnp.concatenate([fin[2 * l + 1] for l in range(nl)], axis=1)
    car_ref[0] = xr
    car_ref[1] = xi
    y = d_ref[0] * u2
    for l in range(2 * nl):
        y = y + _dot(xs_ref[l].astype(BF16), wc_ref[0, l * LANES:(l + 1) * LANES, :])
    y_ref[...] = y.reshape(nb, tc, LANES)

    @pl.when(j == pl.num_programs(2) - 1)
    def _():
        sre_ref[...] = xr
        sim_ref[...] = xi


def _s5(u3d, wbu, wc, a_re, a_im, dsk, h0r, h0i, nb, tc, has_state):
    batch, seq, _ = u3d.shape
    grid = (S5_NGB, batch // nb, seq // tc)
    st = pl.BlockSpec((nb, S5_SL), lambda g, b, j: (b, g))
    par = lambda w: pl.BlockSpec((1, 1, w), lambda g, b, j: (g, 0, 0))
    ublk = pl.BlockSpec((nb, tc, LANES), lambda g, b, j: (b, j, g))
    return pl.pallas_call(
        functools.partial(_s5_kernel, nb=nb, tc=tc, has_state=has_state),
        grid=grid,
        in_specs=[ublk,
                  pl.BlockSpec((1, LANES, 2 * S5_SL), lambda g, b, j: (g, 0, 0)),
                  pl.BlockSpec((1, 2 * S5_SL, LANES), lambda g, b, j: (g, 0, 0)),
                  par(S5_SL), par(S5_SL), par(LANES), st, st],
        out_specs=[ublk, st, st],
        out_shape=[jax.ShapeDtypeStruct(u3d.shape, F32),
                   jax.ShapeDtypeStruct((batch, S5_GROUPS * S5_STATE), F32),
                   jax.ShapeDtypeStruct((batch, S5_GROUPS * S5_STATE), F32)],
        scratch_shapes=[pltpu.VMEM((2 * S5_SL // LANES, nb * tc, LANES), F32),
                        pltpu.VMEM((2 * S5_SL // LANES, nb * tc, LANES), F32),
                        pltpu.VMEM((2, nb, S5_SL), F32)],
        compiler_params=pltpu.CompilerParams(dimension_semantics=("arbitrary",) * 3,
                                             vmem_limit_bytes=VMEM_LIMIT),
        name="s5_state" if has_state else "s5_zero",
    )(u3d, wbu, wc, a_re, a_im, dsk, h0r, h0i)


S5_J = 2
S5_HL = S5_SL // S5_J


def _s5_prompt_kernel(u_ref, wbu_ref, wc_ref, are_ref, aim_ref, d_ref, y_ref, sre_ref, sim_ref,
                      u2_ref, lhs_ref, bu_ref, xs_ref, y2_ref, yo_ref, car_ref, *, nb, tc):
    g = pl.program_id(0)
    j = pl.program_id(1)
    rows = nb * S5_J

    @pl.when((g == 0) & (j == 0))
    def _():
        lhs_ref[...] = jnp.zeros_like(lhs_ref)

    @pl.when(j == 0)
    def _():
        car_ref[...] = jnp.zeros_like(car_ref)

    u2 = u_ref[...].reshape(nb * tc, LANES)
    u2_ref[...] = u2

    def build(t, c):
        u4 = u2_ref[pl.ds(t, nb, stride=tc), :]
        for jj in range(S5_J):
            lhs_ref[t, jj * nb:(jj + 1) * nb, jj * LANES:(jj + 1) * LANES] = u4
        return c

    lax.fori_loop(0, tc, build, 0, unroll=8)
    lhs = lhs_ref[...].reshape(tc * rows, S5_J * LANES).astype(BF16)
    bu_ref[...] = _dot(lhs, wbu_ref[0]).reshape(tc, rows, 2 * S5_HL)
    ar = are_ref[0]
    ai = aim_ref[0]

    def step(t, carry):
        xr, xi = carry
        tile = bu_ref[t]
        nr = ar * xr - ai * xi + tile[:, 0:S5_HL]
        ni = ar * xi + ai * xr + tile[:, S5_HL:2 * S5_HL]
        xs_ref[t] = jnp.concatenate([nr, ni], axis=1)
        return nr, ni

    xr, xi = lax.fori_loop(0, tc, step, (car_ref[0], car_ref[1]), unroll=8)
    car_ref[0] = xr
    car_ref[1] = xi
    xs = xs_ref[...].reshape(tc * rows, 2 * S5_HL).astype(BF16)
    y2_ref[...] = _dot(xs, wc_ref[0]).reshape(tc, rows, S5_J * LANES)
    first_half = lax.broadcasted_iota(jnp.int32, (rows, LANES), 0) < nb

    def unperm(t, c):
        t2 = y2_ref[t]
        part = jnp.where(first_half, t2[:, 0:LANES], t2[:, LANES:2 * LANES])
        tot = part + pltpu.roll(part, nb, axis=0)
        yo_ref[pl.ds(t, nb, stride=tc), :] = tot[0:nb]
        return c

    lax.fori_loop(0, tc, unperm, 0, unroll=8)
    y_ref[...] = (yo_ref[...] + d_ref[0] * u2).reshape(nb, tc, LANES)

    @pl.when(j == pl.num_programs(1) - 1)
    def _():
        sre_ref[...] = jnp.concatenate([xr[jj * nb:(jj + 1) * nb] for jj in range(S5_J)], axis=1)
        sim_ref[...] = jnp.concatenate([xi[jj * nb:(jj + 1) * nb] for jj in range(S5_J)], axis=1)


def _s5_prompt(u3d, wbu2, wc2, a_re2, a_im2, dsk, tc):
    nb, seq, _ = u3d.shape
    rows = nb * S5_J
    assert rows == 8
    st = pl.BlockSpec((nb, S5_SL), lambda g, j: (0, g))
    gblk = lambda s: pl.BlockSpec((1,) + s, lambda g, j: (g, 0, 0))
    ublk = pl.BlockSpec((nb, tc, LANES), lambda g, j: (0, j, g))
    return pl.pallas_call(
        functools.partial(_s5_prompt_kernel, nb=nb, tc=tc),
        grid=(S5_NGB, seq // tc),
        in_specs=[ublk, gblk((S5_J * LANES, 2 * S5_HL)), gblk((2 * S5_HL, S5_J * LANES)),
                  gblk((rows, S5_HL)), gblk((rows, S5_HL)), gblk((1, LANES))],
        out_specs=[ublk, st, st],
        out_shape=[jax.ShapeDtypeStruct(u3d.shape, F32),
                   jax.ShapeDtypeStruct((nb, S5_GROUPS * S5_STATE), F32),
                   jax.ShapeDtypeStruct((nb, S5_GROUPS * S5_STATE), F32)],
        scratch_shapes=[pltpu.VMEM((nb * tc, LANES), F32),
                        pltpu.VMEM((tc, rows, S5_J * LANES), F32),
                        pltpu.VMEM((tc, rows, 2 * S5_HL), F32),
                        pltpu.VMEM((tc, rows, 2 * S5_HL), F32),
                        pltpu.VMEM((tc, rows, S5_J * LANES), F32),
                        pltpu.VMEM((nb * tc, LANES), F32),
                        pltpu.VMEM((2, rows, S5_HL), F32)],
        compiler_params=pltpu.CompilerParams(dimension_semantics=("arbitrary",) * 2,
                                             vmem_limit_bytes=VMEM_LIMIT),
        name="s5_prompt",
    )(u3d, wbu2, wc2, a_re2, a_im2, dsk)


def _s5_prompt_params(wbu, wc, a_re, a_im, nb):
    h = S5_HL
    top = jnp.concatenate([wbu[:, :, 0:h], wbu[:, :, S5_SL:S5_SL + h]], axis=2)
    bot = jnp.concatenate([wbu[:, :, h:2 * h], wbu[:, :, S5_SL + h:S5_SL + 2 * h]], axis=2)
    wbu2 = jnp.concatenate([top, bot], axis=1)
    wc_j = [jnp.concatenate([wc[:, jj * h:(jj + 1) * h, :], wc[:, S5_SL + jj * h:S5_SL + (jj + 1) * h, :]],
                            axis=1) for jj in range(S5_J)]
    wc2 = jnp.concatenate(wc_j, axis=2)
    tile = lambda a: jnp.repeat(a.reshape(S5_NGB, S5_J, h), nb, axis=1)
    return wbu2, wc2, tile(a_re), tile(a_im)


def _mix_out_kernel(xp_ref, xs_ref, ogp_ref, ogs_ref, ysp_ref, yss_ref, glu_w_ref, glu_b_ref, s5n_ref, wo_ref,
                    nffn_ref, wr_ref, h1_ref, hn_ref, rt_ref, *, n_prompt_tiles):
    is_p = pl.program_id(0) < n_prompt_tiles
    x = jnp.where(is_p, xp_ref[...], xs_ref[...])
    og = jnp.where(is_p, ogp_ref[...], ogs_ref[...])
    y = jax.nn.gelu(jnp.where(is_p, ysp_ref[...], yss_ref[...]))
    y = y * jax.nn.sigmoid(_dot(y.astype(BF16), glu_w_ref[...]) + glu_b_ref[...])
    yn = _rms(y, s5n_ref[...]).astype(BF16)
    mix = _dot(og, wo_ref[0:D_GLA, :]) + _dot(yn, wo_ref[D_GLA:D_GLA + D_S5, :])
    h1 = x + mix
    h1_ref[...] = h1
    hn = _rms(h1, nffn_ref[...])
    hn_ref[...] = hn
    logits = jnp.dot(hn, wr_ref[...], preferred_element_type=F32, precision=lax.Precision.HIGHEST)
    rt_ref[...] = _route(logits)


def _route(logits):
    col = lax.broadcasted_iota(jnp.int32, logits.shape, 1)
    colf = col.astype(F32)
    neg = -jnp.inf

    def first_argmax(vals):
        m = jnp.max(vals, axis=-1, keepdims=True)
        idx = jnp.min(jnp.where(vals == m, colf, float(LANES)), axis=-1, keepdims=True)
        return m, idx

    lg = jnp.where(col < N_EGROUPS, logits, neg)
    gmax, gsel = first_argmax(lg)
    p_g = 1.0 / jnp.sum(jnp.exp(lg - gmax), axis=-1, keepdims=True)
    ecol = col - N_EGROUPS
    egrp = (ecol >> 3).astype(F32)
    in_group = (ecol >= 0) & (ecol < N_EXPERTS) & (egrp == gsel)
    le = jnp.where(in_group, logits, neg)
    m1, i1 = first_argmax(le)
    le2 = jnp.where(colf == i1, neg, le)
    m2, i2 = first_argmax(le2)
    e2 = jnp.exp(m2 - m1)
    den = 1.0 + e2
    w1 = p_g * (1.0 / den)
    w2 = p_g * (e2 / den)
    e1f = i1 - float(N_EGROUPS)
    e2f = i2 - float(N_EGROUPS)
    out = jnp.where(col == 0, e1f, jnp.where(col == 1, e2f, jnp.where(col == 2, w1, jnp.where(col == 3, w2, 0.0))))
    return out


def _mix_out(xp, xs, ogp, ogs, ysp, yss, glu_w, glu_b, s5n, wo, nffn, wr):
    tm = TOK_TILE
    npt, nst = xp.shape[0] // tm, xs.shape[0] // tm
    t = (npt + nst) * tm
    row = lambda w: pl.BlockSpec((tm, w), lambda i: (i, 0))
    prow = lambda w: pl.BlockSpec((tm, w), lambda i: (jnp.minimum(i, npt - 1), 0))
    srow = lambda w: pl.BlockSpec((tm, w), lambda i: (jnp.maximum(i - npt, 0), 0))
    return pl.pallas_call(
        functools.partial(_mix_out_kernel, n_prompt_tiles=npt),
        grid=(npt + nst,),
        in_specs=[prow(D_MODEL), srow(D_MODEL), prow(D_GLA), srow(D_GLA), prow(D_S5), srow(D_S5),
                  _const_spec(glu_w.shape), _const_spec((1, D_S5)), _const_spec((1, D_S5)),
                  _const_spec(wo.shape), _const_spec((1, D_MODEL)), _const_spec(wr.shape)],
        out_specs=[row(D_MODEL), row(D_MODEL), row(LANES)],
        out_shape=[jax.ShapeDtypeStruct((t, D_MODEL), F32), jax.ShapeDtypeStruct((t, D_MODEL), F32),
                   jax.ShapeDtypeStruct((t, LANES), F32)],
        compiler_params=pltpu.CompilerParams(dimension_semantics=("arbitrary",), vmem_limit_bytes=VMEM_LIMIT),
        name="mix_out",
    )(xp, xs, ogp, ogs, ysp, yss, glu_w, glu_b, s5n, wo, nffn, wr)


DMA_UNROLL = 8


def _moe_kernel(be_ref, nv_ref, sa_ref, hn_hbm, wg_ref, wu_ref, wd_ref, out_hbm,
                xb_ref, yb_ref, wgb_ref, wub_ref, wdb_ref, gsem, ssem):
    b = pl.program_id(0)
    nb = pl.num_programs(0)
    n = nv_ref[b]
    slot = lax.rem(b, 2)
    t_all = hn_hbm.shape[0]

    def round_up(cnt):
        return (cnt + DMA_UNROLL - 1) // DMA_UNROLL * DMA_UNROLL

    def issue_gather(blk, sl):
        base = blk * MOE_BLK

        def grp(g, c):
            for j in range(DMA_UNROLL):
                i = g * DMA_UNROLL + j
                tok = sa_ref[base + i] >> 1
                pltpu.make_async_copy(hn_hbm.at[pl.ds(tok, 1), :], xb_ref.at[sl, pl.ds(i, 1), :],
                                      gsem.at[sl]).start()
            return c

        lax.fori_loop(0, round_up(nv_ref[blk]) // DMA_UNROLL, grp, 0)

    def wait_gather(cnt, sl):
        rows = pl.ds(0, pl.multiple_of(round_up(cnt), DMA_UNROLL))
        pltpu.make_async_copy(hn_hbm.at[rows, :], xb_ref.at[sl, rows, :], gsem.at[sl]).wait()

    def scatter_row(base, i):
        a = sa_ref[base + i]
        dst = (a & 1) * t_all + (a >> 1)
        pltpu.make_async_copy(yb_ref.at[pl.ds(i, 1), :], out_hbm.at[pl.ds(dst, 1), :], ssem.at[0]).start()

    def issue_scatter(blk, cnt):
        base = blk * MOE_BLK
        full = cnt // DMA_UNROLL

        def grp(g, c):
            for j in range(DMA_UNROLL):
                scatter_row(base, g * DMA_UNROLL + j)
            return c

        def one(i, c):
            scatter_row(base, i)
            return c

        lax.fori_loop(0, full, grp, 0)
        lax.fori_loop(full * DMA_UNROLL, cnt, one, 0)

    def wait_scatter(cnt):
        full = cnt // DMA_UNROLL * DMA_UNROLL

        @pl.when(full > 0)
        def _():
            rows = pl.ds(0, pl.multiple_of(full, DMA_UNROLL))
            pltpu.make_async_copy(yb_ref.at[rows, :], out_hbm.at[rows, :], ssem.at[0]).wait()

        def one(i, c):
            pltpu.make_async_copy(yb_ref.at[pl.ds(i, 1), :], out_hbm.at[pl.ds(0, 1), :], ssem.at[0]).wait()
            return c

        lax.fori_loop(full, cnt, one, 0)

    @pl.when(b == 0)
    def _():
        xb_ref[...] = jnp.zeros_like(xb_ref)
        issue_gather(0, 0)

    @pl.when(b + 1 < nb)
    def _():
        issue_gather(b + 1, 1 - slot)

    n_prev = nv_ref[jnp.maximum(b - 1, 0)]
    has_prev = (b > 0) & (n_prev > 0)

    @pl.when(n > 0)
    def _():
        prev_e = be_ref[jnp.maximum(b - 1, 0)]

        @pl.when((b == 0) | (prev_e != be_ref[b]))
        def _():
            wgb_ref[...] = wg_ref[...].astype(BF16)
            wub_ref[...] = wu_ref[...].astype(BF16)
            wdb_ref[...] = wd_ref[...].astype(BF16)

        wait_gather(n, slot)
        x = xb_ref[slot].astype(BF16)
        gate = _dot(x, wgb_ref[...])
        up = _dot(x, wub_ref[...])
        hid = (gate * jax.nn.sigmoid(gate) * up).astype(BF16)

        @pl.when(has_prev)
        def _():
            wait_scatter(n_prev)

        yb_ref[...] = _dot(hid, wdb_ref[...])
        issue_scatter(b, n)

        @pl.when(b == nb - 1)
        def _():
            wait_scatter(n)

    @pl.when((n == 0) & has_prev)
    def _():
        wait_scatter(n_prev)


def _moe(hn_all, blk_e, blk_n, slot_a, w_gate, w_up, w_down, n_out_rows):
    nblk = blk_e.shape[0]
    wspec = lambda s: pl.BlockSpec((None,) + s, lambda b, be, nv, sa: (be[b], 0, 0))
    grid_spec = pltpu.PrefetchScalarGridSpec(
        num_scalar_prefetch=3,
        grid=(nblk,),
        in_specs=[pl.BlockSpec(memory_space=pl.ANY),
                  wspec((D_MODEL, D_EXPERT)), wspec((D_MODEL, D_EXPERT)), wspec((D_EXPERT, D_MODEL))],
        out_specs=pl.BlockSpec(memory_space=pl.ANY),
        scratch_shapes=[pltpu.VMEM((2, MOE_BLK, D_MODEL), F32), pltpu.VMEM((MOE_BLK, D_MODEL), F32),
                        pltpu.VMEM((D_MODEL, D_EXPERT), BF16), pltpu.VMEM((D_MODEL, D_EXPERT), BF16),
                        pltpu.VMEM((D_EXPERT, D_MODEL), BF16), pltpu.SemaphoreType.DMA((2,)),
                        pltpu.SemaphoreType.DMA((1,))],
    )
    return pl.pallas_call(
        _moe_kernel,
        grid_spec=grid_spec,
        out_shape=jax.ShapeDtypeStruct((n_out_rows, D_MODEL), F32),
        compiler_params=pltpu.CompilerParams(dimension_semantics=("arbitrary",), vmem_limit_bytes=VMEM_LIMIT),
        name="moe_experts",
    )(blk_e, blk_n, slot_a, hn_all, w_gate, w_up, w_down)


def _moe_plan(rt_all):
    t_all = rt_all.shape[0]
    n_assign = 2 * t_all
    e_flat = rt_all[:, 0:2].astype(jnp.int32).reshape(-1)
    onehot = (e_flat[:, None] == jnp.arange(N_EXPERTS, dtype=jnp.int32)[None, :]).astype(jnp.int32)
    csum = jnp.cumsum(onehot, axis=0)
    counts = csum[-1]
    rank = jnp.take_along_axis(csum, e_flat[:, None], axis=1)[:, 0] - 1
    padded = (counts + MOE_BLK - 1) // MOE_BLK * MOE_BLK
    pad_end = jnp.cumsum(padded)
    pad_start = pad_end - padded
    dest = pad_start[e_flat] + rank
    nblk = -(-(n_assign + N_EXPERTS * (MOE_BLK - 1)) // MOE_BLK)
    n_slots = nblk * MOE_BLK
    slot_a = jnp.zeros((n_slots,), jnp.int32).at[dest].set(jnp.arange(n_assign, dtype=jnp.int32))
    blk_start = jnp.arange(nblk, dtype=jnp.int32) * MOE_BLK
    blk_e = jnp.minimum(jnp.searchsorted(pad_end, blk_start, side='right'), N_EXPERTS - 1).astype(jnp.int32)
    blk_n = jnp.clip(counts[blk_e] - (blk_start - pad_start[blk_e]), 0, MOE_BLK).astype(jnp.int32)
    return blk_e, blk_n, slot_a


def _ple_out_kernel(h1_ref, y0_ref, y1_ref, rt_ref, p_ref, nple_ref, wpg_ref, wp_ref, nfin_ref, o_ref):
    rt = rt_ref[...]
    h2 = h1_ref[...] + rt[:, 2:3] * y0_ref[...] + rt[:, 3:4] * y1_ref[...]
    gate = jax.nn.sigmoid(_dot(_rms(h2, nple_ref[...]).astype(BF16), wpg_ref[...]))
    h3 = h2 + _dot(p_ref[...].astype(BF16), wp_ref[...]) * gate
    o_ref[...] = _rms(h3, nfin_ref[...])


def _ple_out(h1, ymoe, rt, p2d, nple, wpg, wp, nfin, row_off, t_all):
    t = p2d.shape[0]
    tm = TOK_TILE
    ob = row_off // tm
    kb = t_all // tm
    row = lambda w: pl.BlockSpec((tm, w), lambda i: (i, 0))
    orow = lambda w: pl.BlockSpec((tm, w), lambda i: (i + ob, 0))
    return pl.pallas_call(
        _ple_out_kernel,
        grid=(t // tm,),
        in_specs=[orow(D_MODEL), orow(D_MODEL),
                  pl.BlockSpec((tm, D_MODEL), lambda i: (i + ob + kb, 0)),
                  orow(LANES), row(D_PLE), _const_spec((1, D_MODEL)), _const_spec(wpg.shape),
                  _const_spec(wp.shape), _const_spec((1, D_MODEL))],
        out_specs=row(D_MODEL),
        out_shape=jax.ShapeDtypeStruct((t, D_MODEL), F32),
        compiler_params=pltpu.CompilerParams(dimension_semantics=("arbitrary",), vmem_limit_bytes=VMEM_LIMIT),
        name="ple_out",
    )(h1, ymoe, ymoe, rt, p2d, nple, wpg, wp, nfin)


def _s5_params(lam_re, lam_im, log_dt, b_re, b_im, c_re, c_im, d_skip):
    dt = jnp.exp(log_dt)[:, None]
    mag = jnp.exp(lam_re * dt)
    ab_re = mag * jnp.cos(lam_im * dt)
    ab_im = mag * jnp.sin(lam_im * dt)
    den = lam_re * lam_re + lam_im * lam_im
    nr = ab_re - 1.0
    f_re = (nr * lam_re + ab_im * lam_im) / den
    f_im = (ab_im * lam_re - nr * lam_im) / den
    bb_re = f_re[..., None] * b_re - f_im[..., None] * b_im
    bb_im = f_re[..., None] * b_im + f_im[..., None] * b_re
    eye = jnp.eye(S5_GB, dtype=F32)

    def bu_w(bb):
        bb = bb.reshape(S5_NGB, S5_GB, S5_STATE, S5_GROUP)
        w = jnp.einsum('nlph,lm->nlhmp', bb, eye)
        return w.reshape(S5_NGB, S5_GB * S5_GROUP, S5_SL)

    def c_w(c):
        c = c.reshape(S5_NGB, S5_GB, S5_GROUP, S5_STATE)
        w = jnp.einsum('nlhp,lm->nlpmh', c, eye)
        return w.reshape(S5_NGB, S5_SL, S5_GB * S5_GROUP)

    wbu = jnp.concatenate([bu_w(bb_re), bu_w(bb_im)], axis=2).astype(BF16)
    wc = jnp.concatenate([c_w(c_re), -c_w(c_im)], axis=1).astype(BF16)
    a_re = ab_re.reshape(S5_NGB, 1, S5_SL)
    a_im = ab_im.reshape(S5_NGB, 1, S5_SL)
    dsk = d_skip.reshape(S5_NGB, 1, S5_GB * S5_GROUP)
    return wbu, wc, a_re, a_im, dsk


def kernel(x_prompt, x_sample, p_prompt, p_sample, state_gla, state_s5_re, state_s5_im, norm_mix, w_in, gla_w_gate_up, gla_gate_bias, gla_norm, s5_lam_re, s5_lam_im, s5_log_dt, s5_b_re, s5_b_im, s5_c_re, s5_c_im, s5_d, s5_glu_w, s5_glu_b, s5_norm, w_out, norm_ffn, router_group, router_expert, w_gate, w_up, w_down, norm_ple, w_ple, w_ple_gate, norm_final):
    depth = w_in.shape[0]
    assert depth == 1
    i = 0
    bp, lp, _ = x_prompt.shape
    bs, ls, _ = x_sample.shape
    tp, ts = bp * lp, bs * ls
    t_all = tp + ts

    n_qkvr = 2 * QK_W + 2 * D_GLA
    wi = w_in[i]
    wm = jnp.concatenate([wi[:, :n_qkvr], wi[:, n_qkvr + GLA_RANK:]], axis=1).astype(BF16)
    wg = jnp.pad(wi[:, n_qkvr:n_qkvr + GLA_RANK], ((0, 0), (0, LANES - GLA_RANK))).astype(BF16)
    wgu = jnp.pad(gla_w_gate_up[i], ((0, LANES - GLA_RANK), (0, 0))).astype(BF16)
    gbias = gla_gate_bias[i].reshape(1, QK_W)
    wbu, wc, a_re, a_im, dsk = _s5_params(s5_lam_re[i], s5_lam_im[i], s5_log_dt[i], s5_b_re[i], s5_b_im[i],
                                          s5_c_re[i], s5_c_im[i], s5_d[i])
    glu_w = s5_glu_w[i].astype(BF16)
    wo = w_out[i].astype(BF16)
    wr = jnp.pad(jnp.concatenate([router_group[i], router_expert[i]], axis=1),
                 ((0, 0), (0, LANES - N_EGROUPS - N_EXPERTS)))
    wpg = w_ple_gate[i].astype(BF16)
    wp = w_ple[i].astype(BF16)
    vec = lambda a: a.reshape(1, -1)

    xp = x_prompt.reshape(tp, D_MODEL)
    xs = x_sample.reshape(ts, D_MODEL)

    qp, kp, vp, rp, lap, up = _in_proj(xp, vec(norm_mix[i]), wm, wg, wgu, gbias, BF16)
    qs, ks, vs, rs, las, us = _in_proj(xs, vec(norm_mix[i]), wm, wg, wgu, gbias, F32)
    ogp, gla_p = _gla_prompt(qp, kp, vp, rp, lap, vec(gla_norm[i]), bp, lp)
    ogs, gla_s = _gla_sample(qs, ks, vs, rs, las, vec(gla_norm[i]), state_gla[i], bs, ls)
    wbu2, wc2, a_re2, a_im2 = _s5_prompt_params(wbu, wc, a_re, a_im, bp)
    ysp, re_p, im_p = _s5_prompt(up.reshape(bp, lp, D_S5), wbu2, wc2, a_re2, a_im2, dsk, tc=256)
    yss, re_s, im_s = _s5(us.reshape(bs, ls, D_S5), wbu, wc, a_re, a_im, dsk,
                          state_s5_re[i].reshape(bs, -1), state_s5_im[i].reshape(bs, -1),
                          nb=32, tc=ls, has_state=True)

    h1, hn_all, rt_all = _mix_out(xp, xs, ogp, ogs, ysp.reshape(tp, D_S5), yss.reshape(ts, D_S5), glu_w,
                                  vec(s5_glu_b[i]), vec(s5_norm[i]), wo, vec(norm_ffn[i]), wr)

    blk_e, blk_n, slot_a = _moe_plan(rt_all)
    ymoe = _moe(hn_all, blk_e, blk_n, slot_a, w_gate[i], w_up[i], w_down[i], 2 * t_all)

    po = functools.partial(_ple_out, h1, ymoe, rt_all, nple=vec(norm_ple[i]), wpg=wpg, wp=wp,
                           nfin=vec(norm_final), t_all=t_all)
    y_p = po(p_prompt[i].reshape(tp, D_PLE), row_off=0)
    y_s = po(p_sample[i].reshape(ts, D_PLE), row_off=tp)

    s5shape = lambda a, b: a.reshape(1, b, S5_GROUPS, S5_STATE)
    return (y_p.reshape(bp, lp, D_MODEL), y_s.reshape(bs, ls, D_MODEL),
            gla_p[None], s5shape(re_p, bp), s5shape(im_p, bp),
            gla_s[None], s5shape(re_s, bs), s5shape(im_s, bs))
```

```python
import functools
import math

import jax
import jax.numpy as jnp
from jax import lax
from jax.experimental import pallas as pl
from jax.experimental.pallas import tpu as pltpu

F32 = jnp.float32
BF16 = jnp.bfloat16

D_MODEL = 2048
D_GLA = 1024
D_S5 = 1024
GLA_HEADS = 4
GLA_DV = 256
GLA_DK = 128
GLA_RANK = 16
GLA_CHUNK = 64
S5_GROUP = 16
S5_GROUPS = 64
S5_STATE = 64
N_EGROUPS = 4
N_EPG = 8
N_EXPERTS = 32
D_EXPERT = 512
D_PLE = 256
EPS = 1e-6

LANES = 128
QK_W = GLA_HEADS * GLA_DK
S5_GB = 8
S5_NGB = S5_GROUPS // S5_GB
S5_SL = S5_GB * S5_STATE
TOK_TILE = 256
MOE_BLK = 256
VMEM_LIMIT = 56 * 1024 * 1024


def _const_spec(shape):
    nd = len(shape)
    return pl.BlockSpec(shape, lambda *_: (0,) * nd, pipeline_mode=pl.Buffered(1))


def _rms(x, g):
    return x * lax.rsqrt(jnp.mean(x * x, axis=-1, keepdims=True) + EPS) * g


def _dot(a, b):
    return jnp.dot(a, b, preferred_element_type=F32)


def _log_sigmoid(x):
    return -(jnp.maximum(-x, 0.0) + jnp.log1p(jnp.exp(-jnp.abs(x))))


def _in_proj_kernel(x_ref, g_ref, wm_ref, wg_ref, wgu_ref, gb_ref,
                    q_ref, k_ref, v_ref, r_ref, la_ref, u_ref):
    hb = _rms(x_ref[...], g_ref[...]).astype(BF16)

    def seg(a, b):
        return _dot(hb, wm_ref[:, a:b])

    q_ref[...] = (seg(0, QK_W) * (GLA_DK ** -0.5)).astype(q_ref.dtype)
    k_ref[...] = seg(QK_W, 2 * QK_W).astype(k_ref.dtype)
    v_ref[...] = seg(2 * QK_W, 2 * QK_W + D_GLA).astype(v_ref.dtype)
    r_ref[...] = seg(2 * QK_W + D_GLA, 2 * QK_W + 2 * D_GLA).astype(r_ref.dtype)
    u_ref[...] = seg(2 * QK_W + 2 * D_GLA, 2 * QK_W + 2 * D_GLA + D_S5)
    zg = _dot(hb, wg_ref[...])
    xg = _dot(zg.astype(BF16), wgu_ref[...]) + gb_ref[...]
    la_ref[...] = _log_sigmoid(xg) * (1.0 / 16.0)


def _in_proj(x2d, g, wm, wg, wgu, gbias, act_dtype):
    t = x2d.shape[0]
    tm = TOK_TILE
    row = lambda w: pl.BlockSpec((tm, w), lambda i: (i, 0))
    return pl.pallas_call(
        _in_proj_kernel,
        grid=(t // tm,),
        in_specs=[row(D_MODEL), _const_spec((1, D_MODEL)), _const_spec(wm.shape), _const_spec(wg.shape),
                  _const_spec(wgu.shape), _const_spec((1, QK_W))],
        out_specs=[row(QK_W), row(QK_W), row(D_GLA), row(D_GLA), row(QK_W), row(D_S5)],
        out_shape=[jax.ShapeDtypeStruct((t, QK_W), act_dtype), jax.ShapeDtypeStruct((t, QK_W), act_dtype),
                   jax.ShapeDtypeStruct((t, D_GLA), act_dtype), jax.ShapeDtypeStruct((t, D_GLA), act_dtype),
                   jax.ShapeDtypeStruct((t, QK_W), F32), jax.ShapeDtypeStruct((t, D_S5), F32)],
        compiler_params=pltpu.CompilerParams(dimension_semantics=("arbitrary",), vmem_limit_bytes=VMEM_LIMIT),
        name="in_proj",
    )(x2d, g, wm, wg, wgu, gbias)


_NT = (((1,), (1,)), ((), ()))
_TN = (((0,), (0,)), ((), ()))


def _gla_pre(q, k, la, c):
    r = q.shape[0]
    shift = int(math.log2(c))
    ri = lax.broadcasted_iota(jnp.int32, (r, r), 0)
    si = lax.broadcasted_iota(jnp.int32, (r, r), 1)
    mask = ((ri >> shift) == (si >> shift)) & (ri >= si)
    tri = jnp.where(mask, 1.0, 0.0).astype(BF16)
    hi = la.astype(BF16)
    r1 = la - hi.astype(F32)
    mid = r1.astype(BF16)
    lo = (r1 - mid.astype(F32)).astype(BF16)
    cum = _dot(tri, hi) + _dot(tri, mid) + _dot(tri, lo)
    last = jnp.concatenate([jnp.broadcast_to(cum[(i + 1) * c - 1:(i + 1) * c, :], (c, cum.shape[1]))
                            for i in range(r // c)], axis=0)
    qe = (q * jnp.exp(cum)).astype(BF16)
    ke = (k * jnp.exp(-cum)).astype(BF16)
    kd = (k * jnp.exp(last - cum)).astype(BF16)
    return qe, ke, kd, cum, mask


def _gla_intra(qe, ke, v, mask):
    sc = lax.dot_general(qe, ke, _NT, preferred_element_type=F32)
    return _dot(jnp.where(mask, sc, 0.0).astype(BF16), v)


def _gla_finish(o, r, g):
    rf = r.astype(F32)
    return _rms(o, g) * (rf * jax.nn.sigmoid(rf))


def _gla_prompt_kernel(q_ref, k_ref, v_ref, r_ref, la_ref, g_ref, o_ref, sfin_ref, st_ref, *, n_chunks):
    j = pl.program_id(1)

    @pl.when(j == 0)
    def _():
        st_ref[...] = jnp.zeros_like(st_ref)

    c = GLA_CHUNK
    qe, ke, kd, cum, mask = _gla_pre(q_ref[...].astype(F32), k_ref[...].astype(F32), la_ref[...], c)
    g = g_ref[...]
    for h in range(GLA_HEADS):
        kc = slice(h * GLA_DK, (h + 1) * GLA_DK)
        vc = slice(h * GLA_DV, (h + 1) * GLA_DV)
        v = v_ref[:, vc]
        qe_h, kd_h = qe[:, kc], kd[:, kc]
        o = _gla_intra(qe_h, ke[:, kc], v, mask)
        st = st_ref[h]
        inter = []
        for ci in range(n_chunks):
            rows = slice(ci * c, (ci + 1) * c)
            inter.append(lax.dot_general(qe_h[rows], st.astype(BF16), _NT, preferred_element_type=F32))
            dec = jnp.exp(cum[(ci + 1) * c - 1:(ci + 1) * c, kc])
            st = st * dec + lax.dot_general(v[rows], kd_h[rows], _TN, preferred_element_type=F32)
        st_ref[h] = st
        o = o + jnp.concatenate(inter, axis=0)
        o_ref[:, vc] = _gla_finish(o, r_ref[:, vc], g).astype(o_ref.dtype)

    @pl.when(j == pl.num_programs(1) - 1)
    def _():
        for h in range(GLA_HEADS):
            sfin_ref[0, h] = st_ref[h].T


def _gla_prompt(q, k, v, r, la, g, batch, seq):
    rb = 4 * GLA_CHUNK
    nj = seq // rb
    row = lambda w: pl.BlockSpec((rb, w), lambda b, j: (b * nj + j, 0))
    return pl.pallas_call(
        functools.partial(_gla_prompt_kernel, n_chunks=rb // GLA_CHUNK),
        grid=(batch, nj),
        in_specs=[row(QK_W), row(QK_W), row(D_GLA), row(D_GLA), row(QK_W), _const_spec((1, GLA_DV))],
        out_specs=[row(D_GLA),
                   pl.BlockSpec((1, GLA_HEADS, GLA_DK, GLA_DV), lambda b, j: (b, 0, 0, 0))],
        out_shape=[jax.ShapeDtypeStruct((batch * seq, D_GLA), BF16),
                   jax.ShapeDtypeStruct((batch, GLA_HEADS, GLA_DK, GLA_DV), F32)],
        scratch_shapes=[pltpu.VMEM((GLA_HEADS, GLA_DV, GLA_DK), F32)],
        compiler_params=pltpu.CompilerParams(dimension_semantics=("arbitrary", "arbitrary"),
                                             vmem_limit_bytes=VMEM_LIMIT),
        name="gla_prompt",
    )(q, k, v, r, la, g)


def _gla_sample_kernel(q_ref, k_ref, v_ref, r_ref, la_ref, g_ref, s0_ref, o_ref, sfin_ref, *, n_seq, seq):
    qe, ke, kd, cum, mask = _gla_pre(q_ref[...].astype(F32), k_ref[...].astype(F32), la_ref[...], seq)
    g = g_ref[...]
    lasts = jnp.concatenate([cum[(s + 1) * seq - 1:(s + 1) * seq, :] for s in range(n_seq)]
                            + [jnp.zeros((GLA_DK - n_seq, cum.shape[1]), F32)], axis=0)
    pair = 2 * seq
    upper = lax.broadcasted_iota(jnp.int32, (pair, GLA_DK), 0) < seq
    for h in range(GLA_HEADS):
        kc = slice(h * GLA_DK, (h + 1) * GLA_DK)
        vc = slice(h * GLA_DV, (h + 1) * GLA_DV)
        v = v_ref[:, vc]
        qe_h, kd_h = qe[:, kc], kd[:, kc]
        o = _gla_intra(qe_h, ke[:, kc], v, mask)
        dec_t = jnp.exp(lasts[:, kc].T)
        inter = []
        for p in range(n_seq // 2):
            rows = slice(p * pair, (p + 1) * pair)
            qe_p, kd_p, v_p = qe_h[rows], kd_h[rows], v[rows]
            for half in range(2):
                s = 2 * p + half
                s0 = s0_ref[s, h]
                o_s = _dot(qe_p, s0.astype(BF16))
                inter.append(o_s[half * seq:(half + 1) * seq])
                kd_s = jnp.where(upper if half == 0 else ~upper, kd_p, jnp.zeros_like(kd_p))
                dec = jnp.broadcast_to(dec_t[:, s:s + 1], (GLA_DK, GLA_DV))
                sfin_ref[s, h] = s0 * dec + lax.dot_general(kd_s, v_p, _TN, preferred_element_type=F32)
        o = o + jnp.concatenate(inter, axis=0)
        o_ref[:, vc] = _gla_finish(o, r_ref[:, vc], g).astype(o_ref.dtype)


def _gla_sample(q, k, v, r, la, g, s0, batch, seq):
    ns = 16
    rb = ns * seq
    row = lambda w: pl.BlockSpec((rb, w), lambda i: (i, 0))
    st = pl.BlockSpec((ns, GLA_HEADS, GLA_DK, GLA_DV), lambda i: (i, 0, 0, 0))
    return pl.pallas_call(
        functools.partial(_gla_sample_kernel, n_seq=ns, seq=seq),
        grid=(batch // ns,),
        in_specs=[row(QK_W), row(QK_W), row(D_GLA), row(D_GLA), row(QK_W), _const_spec((1, GLA_DV)), st],
        out_specs=[row(D_GLA), st],
        out_shape=[jax.ShapeDtypeStruct((batch * seq, D_GLA), BF16),
                   jax.ShapeDtypeStruct((batch, GLA_HEADS, GLA_DK, GLA_DV), F32)],
        compiler_params=pltpu.CompilerParams(dimension_semantics=("arbitrary",), vmem_limit_bytes=VMEM_LIMIT),
        name="gla_sample",
    )(q, k, v, r, la, g, s0)


def _s5_kernel(u_ref, wbu_ref, wc_ref, are_ref, aim_ref, d_ref, h0r_ref, h0i_ref,
               y_ref, sre_ref, sim_ref, bu_ref, xs_ref, car_ref, *, nb, tc, has_state):
    j = pl.program_id(2)

    @pl.when(j == 0)
    def _():
        if has_state:
            car_ref[0] = h0r_ref[...]
            car_ref[1] = h0i_ref[...]
        else:
            car_ref[...] = jnp.zeros_like(car_ref)

    u2 = u_ref[...].reshape(nb * tc, LANES)
    ub = u2.astype(BF16)
    nl = S5_SL // LANES
    for l in range(2 * nl):
        bu_ref[l] = _dot(ub, wbu_ref[0, :, l * LANES:(l + 1) * LANES])
    a_r = [jnp.broadcast_to(are_ref[0, :, l * LANES:(l + 1) * LANES], (nb, LANES)) for l in range(nl)]
    a_i = [jnp.broadcast_to(aim_ref[0, :, l * LANES:(l + 1) * LANES], (nb, LANES)) for l in range(nl)]

    def step(t, carry):
        rows = pl.ds(t, nb, stride=tc)
        out = []
        for l in range(nl):
            xr, xi = carry[2 * l], carry[2 * l + 1]
            nr = a_r[l] * xr - a_i[l] * xi + bu_ref[l, rows, :]
            ni = a_r[l] * xi + a_i[l] * xr + bu_ref[nl + l, rows, :]
            xs_ref[l, rows, :] = nr
            xs_ref[nl + l, rows, :] = ni
            out += [nr, ni]
        return tuple(out)

    init = []
    for l in range(nl):
        init += [car_ref[0, :, l * LANES:(l + 1) * LANES], car_ref[1, :, l * LANES:(l + 1) * LANES]]
    fin = lax.fori_loop(0, tc, step, tuple(init), unroll=8)
    xr = jnp.concatenate([fin[2 * l] for l in range(nl)], axis=1)
    xi = jnp.concatenate([fin[2 * l + 1] for l in range(nl)], axis=1)
    car_ref[0] = xr
    car_ref[1] = xi
    y = d_ref[0] * u2
    for l in range(2 * nl):
        y = y + _dot(xs_ref[l].astype(BF16), wc_ref[0, l * LANES:(l + 1) * LANES, :])
    y_ref[...] = y.reshape(nb, tc, LANES)

    @pl.when(j == pl.num_programs(2) - 1)
    def _():
        sre_ref[...] = xr
        sim_ref[...] = xi


def _s5(u3d, wbu, wc, a_re, a_im, dsk, h0r, h0i, nb, tc, has_state):
    batch, seq, _ = u3d.shape
    grid = (S5_NGB, batch // nb, seq // tc)
    st = pl.BlockSpec((nb, S5_SL), lambda g, b, j: (b, g))
    par = lambda w: pl.BlockSpec((1, 1, w), lambda g, b, j: (g, 0, 0))
    ublk = pl.BlockSpec((nb, tc, LANES), lambda g, b, j: (b, j, g))
    return pl.pallas_call(
        functools.partial(_s5_kernel, nb=nb, tc=tc, has_state=has_state),
        grid=grid,
        in_specs=[ublk,
                  pl.BlockSpec((1, LANES, 2 * S5_SL), lambda g, b, j: (g, 0, 0)),
                  pl.BlockSpec((1, 2 * S5_SL, LANES), lambda g, b, j: (g, 0, 0)),
                  par(S5_SL), par(S5_SL), par(LANES), st, st],
        out_specs=[ublk, st, st],
        out_shape=[jax.ShapeDtypeStruct(u3d.shape, F32),
                   jax.ShapeDtypeStruct((batch, S5_GROUPS * S5_STATE), F32),
                   jax.ShapeDtypeStruct((batch, S5_GROUPS * S5_STATE), F32)],
        scratch_shapes=[pltpu.VMEM((2 * S5_SL // LANES, nb * tc, LANES), F32),
                        pltpu.VMEM((2 * S5_SL // LANES, nb * tc, LANES), F32),
                        pltpu.VMEM((2, nb, S5_SL), F32)],
        compiler_params=pltpu.CompilerParams(dimension_semantics=("arbitrary",) * 3,
                                             vmem_limit_bytes=VMEM_LIMIT),
        name="s5_state" if has_state else "s5_zero",
    )(u3d, wbu, wc, a_re, a_im, dsk, h0r, h0i)


S5_J = 2
S5_HL = S5_SL // S5_J


def _s5_prompt_kernel(u_ref, wbu_ref, wc_ref, are_ref, aim_ref, d_ref, y_ref, sre_ref, sim_ref,
                      u2_ref, lhs_ref, bu_ref, xs_ref, y2_ref, yo_ref, car_ref, *, nb, tc):
    g = pl.program_id(0)
    j = pl.program_id(1)
    rows = nb * S5_J

    @pl.when((g == 0) & (j == 0))
    def _():
        lhs_ref[...] = jnp.zeros_like(lhs_ref)

    @pl.when(j == 0)
    def _():
        car_ref[...] = jnp.zeros_like(car_ref)

    u2 = u_ref[...].reshape(nb * tc, LANES)
    u2_ref[...] = u2

    def build(t, c):
        u4 = u2_ref[pl.ds(t, nb, stride=tc), :]
        for jj in range(S5_J):
            lhs_ref[t, jj * nb:(jj + 1) * nb, jj * LANES:(jj + 1) * LANES] = u4
        return c

    lax.fori_loop(0, tc, build, 0, unroll=8)
    lhs = lhs_ref[...].reshape(tc * rows, S5_J * LANES).astype(BF16)
    bu_ref[...] = _dot(lhs, wbu_ref[0]).reshape(tc, rows, 2 * S5_HL)
    ar = are_ref[0]
    ai = aim_ref[0]

    def step(t, carry):
        xr, xi = carry
        tile = bu_ref[t]
        nr = ar * xr - ai * xi + tile[:, 0:S5_HL]
        ni = ar * xi + ai * xr + tile[:, S5_HL:2 * S5_HL]
        xs_ref[t] = jnp.concatenate([nr, ni], axis=1)
        return nr, ni

    xr, xi = lax.fori_loop(0, tc, step, (car_ref[0], car_ref[1]), unroll=8)
    car_ref[0] = xr
    car_ref[1] = xi
    xs = xs_ref[...].reshape(tc * rows, 2 * S5_HL).astype(BF16)
    y2_ref[...] = _dot(xs, wc_ref[0]).reshape(tc, rows, S5_J * LANES)
    first_half = lax.broadcasted_iota(jnp.int32, (rows, LANES), 0) < nb

    def unperm(t, c):
        t2 = y2_ref[t]
        part = jnp.where(first_half, t2[:, 0:LANES], t2[:, LANES:2 * LANES])
        tot = part + pltpu.roll(part, nb, axis=0)
        yo_ref[pl.ds(t, nb, stride=tc), :] = tot[0:nb]
        return c

    lax.fori_loop(0, tc, unperm, 0, unroll=8)
    y_ref[...] = (yo_ref[...] + d_ref[0] * u2).reshape(nb, tc, LANES)

    @pl.when(j == pl.num_programs(1) - 1)
    def _():
        sre_ref[...] = jnp.concatenate([xr[jj * nb:(jj + 1) * nb] for jj in range(S5_J)], axis=1)
        sim_ref[...] = jnp.concatenate([xi[jj * nb:(jj + 1) * nb] for jj in range(S5_J)], axis=1)


def _s5_prompt(u3d, wbu2, wc2, a_re2, a_im2, dsk, tc):
    nb, seq, _ = u3d.shape
    rows = nb * S5_J
    assert rows == 8
    st = pl.BlockSpec((nb, S5_SL), lambda g, j: (0, g))
    gblk = lambda s: pl.BlockSpec((1,) + s, lambda g, j: (g, 0, 0))
    ublk = pl.BlockSpec((nb, tc, LANES), lambda g, j: (0, j, g))
    return pl.pallas_call(
        functools.partial(_s5_prompt_kernel, nb=nb, tc=tc),
        grid=(S5_NGB, seq // tc),
        in_specs=[ublk, gblk((S5_J * LANES, 2 * S5_HL)), gblk((2 * S5_HL, S5_J * LANES)),
                  gblk((rows, S5_HL)), gblk((rows, S5_HL)), gblk((1, LANES))],
        out_specs=[ublk, st, st],
        out_shape=[jax.ShapeDtypeStruct(u3d.shape, F32),
                   jax.ShapeDtypeStruct((nb, S5_GROUPS * S5_STATE), F32),
                   jax.ShapeDtypeStruct((nb, S5_GROUPS * S5_STATE), F32)],
        scratch_shapes=[pltpu.VMEM((nb * tc, LANES), F32),
                        pltpu.VMEM((tc, rows, S5_J * LANES), F32),
                        pltpu.VMEM((tc, rows, 2 * S5_HL), F32),
                        pltpu.VMEM((tc, rows, 2 * S5_HL), F32),
                        pltpu.VMEM((tc, rows, S5_J * LANES), F32),
                        pltpu.VMEM((nb * tc, LANES), F32),
                        pltpu.VMEM((2, rows, S5_HL), F32)],
        compiler_params=pltpu.CompilerParams(dimension_semantics=("arbitrary",) * 2,
                                             vmem_limit_bytes=VMEM_LIMIT),
        name="s5_prompt",
    )(u3d, wbu2, wc2, a_re2, a_im2, dsk)


def _s5_prompt_params(wbu, wc, a_re, a_im, nb):
    h = S5_HL
    top = jnp.concatenate([wbu[:, :, 0:h], wbu[:, :, S5_SL:S5_SL + h]], axis=2)
    bot = jnp.concatenate([wbu[:, :, h:2 * h], wbu[:, :, S5_SL + h:S5_SL + 2 * h]], axis=2)
    wbu2 = jnp.concatenate([top, bot], axis=1)
    wc_j = [jnp.concatenate([wc[:, jj * h:(jj + 1) * h, :], wc[:, S5_SL + jj * h:S5_SL + (jj + 1) * h, :]],
                            axis=1) for jj in range(S5_J)]
    wc2 = jnp.concatenate(wc_j, axis=2)
    tile = lambda a: jnp.repeat(a.reshape(S5_NGB, S5_J, h), nb, axis=1)
    return wbu2, wc2, tile(a_re), tile(a_im)


def _mix_out_kernel(xp_ref, xs_ref, ogp_ref, ogs_ref, ysp_ref, yss_ref, glu_w_ref, glu_b_ref, s5n_ref, wo_ref,
                    nffn_ref, wr_ref, h1_ref, hn_ref, rt_ref, *, n_prompt_tiles):
    is_p = pl.program_id(0) < n_prompt_tiles
    x = jnp.where(is_p, xp_ref[...], xs_ref[...])
    og = jnp.where(is_p, ogp_ref[...], ogs_ref[...])
    y = jax.nn.gelu(jnp.where(is_p, ysp_ref[...], yss_ref[...]))
    y = y * jax.nn.sigmoid(_dot(y.astype(BF16), glu_w_ref[...]) + glu_b_ref[...])
    yn = _rms(y, s5n_ref[...]).astype(BF16)
    mix = _dot(og, wo_ref[0:D_GLA, :]) + _dot(yn, wo_ref[D_GLA:D_GLA + D_S5, :])
    h1 = x + mix
    h1_ref[...] = h1
    hn = _rms(h1, nffn_ref[...])
    hn_ref[...] = hn
    hn_hi = hn.astype(BF16)
    hn_lo = (hn - hn_hi.astype(F32)).astype(BF16)
    logits = _dot(hn_hi, wr_ref[0]) + _dot(hn_hi, wr_ref[1]) + _dot(hn_lo, wr_ref[0])
    rt_ref[...] = _route(logits)


def _route(logits):
    col = lax.broadcasted_iota(jnp.int32, logits.shape, 1)
    colf = col.astype(F32)
    neg = -jnp.inf

    def first_argmax(vals):
        m = jnp.max(vals, axis=-1, keepdims=True)
        idx = jnp.min(jnp.where(vals == m, colf, float(LANES)), axis=-1, keepdims=True)
        return m, idx

    lg = jnp.where(col < N_EGROUPS, logits, neg)
    gmax, gsel = first_argmax(lg)
    p_g = 1.0 / jnp.sum(jnp.exp(lg - gmax), axis=-1, keepdims=True)
    ecol = col - N_EGROUPS
    egrp = (ecol >> 3).astype(F32)
    in_group = (ecol >= 0) & (ecol < N_EXPERTS) & (egrp == gsel)
    le = jnp.where(in_group, logits, neg)
    m1, i1 = first_argmax(le)
    le2 = jnp.where(colf == i1, neg, le)
    m2, i2 = first_argmax(le2)
    e2 = jnp.exp(m2 - m1)
    den = 1.0 + e2
    w1 = p_g * (1.0 / den)
    w2 = p_g * (e2 / den)
    e1f = i1 - float(N_EGROUPS)
    e2f = i2 - float(N_EGROUPS)
    out = jnp.where(col == 0, e1f, jnp.where(col == 1, e2f, jnp.where(col == 2, w1, jnp.where(col == 3, w2, 0.0))))
    return out


def _mix_out(xp, xs, ogp, ogs, ysp, yss, glu_w, glu_b, s5n, wo, nffn, wr):
    tm = TOK_TILE
    npt, nst = xp.shape[0] // tm, xs.shape[0] // tm
    t = (npt + nst) * tm
    row = lambda w: pl.BlockSpec((tm, w), lambda i: (i, 0))
    prow = lambda w: pl.BlockSpec((tm, w), lambda i: (jnp.minimum(i, npt - 1), 0))
    srow = lambda w: pl.BlockSpec((tm, w), lambda i: (jnp.maximum(i - npt, 0), 0))
    return pl.pallas_call(
        functools.partial(_mix_out_kernel, n_prompt_tiles=npt),
        grid=(npt + nst,),
        in_specs=[prow(D_MODEL), srow(D_MODEL), prow(D_GLA), srow(D_GLA), prow(D_S5), srow(D_S5),
                  _const_spec(glu_w.shape), _const_spec((1, D_S5)), _const_spec((1, D_S5)),
                  _const_spec(wo.shape), _const_spec((1, D_MODEL)), _const_spec(wr.shape)],
        out_specs=[row(D_MODEL), row(D_MODEL), row(LANES)],
        out_shape=[jax.ShapeDtypeStruct((t, D_MODEL), F32), jax.ShapeDtypeStruct((t, D_MODEL), F32),
                   jax.ShapeDtypeStruct((t, LANES), F32)],
        compiler_params=pltpu.CompilerParams(dimension_semantics=("arbitrary",), vmem_limit_bytes=VMEM_LIMIT),
        name="mix_out",
    )(xp, xs, ogp, ogs, ysp, yss, glu_w, glu_b, s5n, wo, nffn, wr)


DMA_UNROLL = 8


def _moe_kernel(be_ref, nv_ref, sa_ref, hn_hbm, wg_ref, wu_ref, wd_ref, out_hbm,
                xb_ref, yb_ref, wgb_ref, wub_ref, wdb_ref, gsem, ssem):
    b = pl.program_id(0)
    nb = pl.num_programs(0)
    n = nv_ref[b]
    slot = lax.rem(b, 2)
    t_all = hn_hbm.shape[0]

    def round_up(cnt):
        return (cnt + DMA_UNROLL - 1) // DMA_UNROLL * DMA_UNROLL

    def issue_gather(blk, sl):
        base = blk * MOE_BLK

        def grp(g, c):
            for j in range(DMA_UNROLL):
                i = g * DMA_UNROLL + j
                tok = sa_ref[base + i] >> 1
                pltpu.make_async_copy(hn_hbm.at[pl.ds(tok, 1), :], xb_ref.at[sl, pl.ds(i, 1), :],
                                      gsem.at[sl]).start()
            return c

        lax.fori_loop(0, round_up(nv_ref[blk]) // DMA_UNROLL, grp, 0)

    def wait_gather(cnt, sl):
        rows = pl.ds(0, pl.multiple_of(round_up(cnt), DMA_UNROLL))
        pltpu.make_async_copy(hn_hbm.at[rows, :], xb_ref.at[sl, rows, :], gsem.at[sl]).wait()

    def scatter_row(base, i):
        a = sa_ref[base + i]
        dst = (a & 1) * t_all + (a >> 1)
        pltpu.make_async_copy(yb_ref.at[pl.ds(i, 1), :], out_hbm.at[pl.ds(dst, 1), :], ssem.at[0]).start()

    def issue_scatter(blk, cnt):
        base = blk * MOE_BLK
        full = cnt // DMA_UNROLL

        def grp(g, c):
            for j in range(DMA_UNROLL):
                scatter_row(base, g * DMA_UNROLL + j)
            return c

        def one(i, c):
            scatter_row(base, i)
            return c

        lax.fori_loop(0, full, grp, 0)
        lax.fori_loop(full * DMA_UNROLL, cnt, one, 0)

    def wait_scatter(cnt):
        full = cnt // DMA_UNROLL * DMA_UNROLL

        @pl.when(full > 0)
        def _():
            rows = pl.ds(0, pl.multiple_of(full, DMA_UNROLL))
            pltpu.make_async_copy(yb_ref.at[rows, :], out_hbm.at[rows, :], ssem.at[0]).wait()

        def one(i, c):
            pltpu.make_async_copy(yb_ref.at[pl.ds(i, 1), :], out_hbm.at[pl.ds(0, 1), :], ssem.at[0]).wait()
            return c

        lax.fori_loop(full, cnt, one, 0)

    @pl.when(b == 0)
    def _():
        xb_ref[...] = jnp.zeros_like(xb_ref)
        issue_gather(0, 0)

    @pl.when(b + 1 < nb)
    def _():
        issue_gather(b + 1, 1 - slot)

    n_prev = nv_ref[jnp.maximum(b - 1, 0)]
    has_prev = (b > 0) & (n_prev > 0)

    @pl.when(n > 0)
    def _():
        prev_e = be_ref[jnp.maximum(b - 1, 0)]

        @pl.when((b == 0) | (prev_e != be_ref[b]))
        def _():
            wgb_ref[...] = wg_ref[...].astype(BF16)
            wub_ref[...] = wu_ref[...].astype(BF16)
            wdb_ref[...] = wd_ref[...].astype(BF16)

        wait_gather(n, slot)
        x = xb_ref[slot].astype(BF16)
        gate = _dot(x, wgb_ref[...])
        up = _dot(x, wub_ref[...])
        hid = (gate * jax.nn.sigmoid(gate) * up).astype(BF16)

        @pl.when(has_prev)
        def _():
            wait_scatter(n_prev)

        yb_ref[...] = _dot(hid, wdb_ref[...])
        issue_scatter(b, n)

        @pl.when(b == nb - 1)
        def _():
            wait_scatter(n)

    @pl.when((n == 0) & has_prev)
    def _():
        wait_scatter(n_prev)


def _moe(hn_all, blk_e, blk_n, slot_a, w_gate, w_up, w_down, n_out_rows):
    nblk = blk_e.shape[0]
    wspec = lambda s: pl.BlockSpec((None,) + s, lambda b, be, nv, sa: (be[b], 0, 0))
    grid_spec = pltpu.PrefetchScalarGridSpec(
        num_scalar_prefetch=3,
        grid=(nblk,),
        in_specs=[pl.BlockSpec(memory_space=pl.ANY),
                  wspec((D_MODEL, D_EXPERT)), wspec((D_MODEL, D_EXPERT)), wspec((D_EXPERT, D_MODEL))],
        out_specs=pl.BlockSpec(memory_space=pl.ANY),
        scratch_shapes=[pltpu.VMEM((2, MOE_BLK, D_MODEL), F32), pltpu.VMEM((MOE_BLK, D_MODEL), F32),
                        pltpu.VMEM((D_MODEL, D_EXPERT), BF16), pltpu.VMEM((D_MODEL, D_EXPERT), BF16),
                        pltpu.VMEM((D_EXPERT, D_MODEL), BF16), pltpu.SemaphoreType.DMA((2,)),
                        pltpu.SemaphoreType.DMA((1,))],
    )
    return pl.pallas_call(
        _moe_kernel,
        grid_spec=grid_spec,
        out_shape=jax.ShapeDtypeStruct((n_out_rows, D_MODEL), F32),
        compiler_params=pltpu.CompilerParams(dimension_semantics=("arbitrary",), vmem_limit_bytes=VMEM_LIMIT),
        name="moe_experts",
    )(blk_e, blk_n, slot_a, hn_all, w_gate, w_up, w_down)


def _moe_plan(rt_all):
    t_all = rt_all.shape[0]
    n_assign = 2 * t_all
    e_flat = rt_all[:, 0:2].astype(jnp.int32).reshape(-1)
    onehot = (e_flat[:, None] == jnp.arange(N_EXPERTS, dtype=jnp.int32)[None, :]).astype(jnp.int32)
    csum = jnp.cumsum(onehot, axis=0)
    counts = csum[-1]
    rank = jnp.take_along_axis(csum, e_flat[:, None], axis=1)[:, 0] - 1
    padded = (counts + MOE_BLK - 1) // MOE_BLK * MOE_BLK
    pad_end = jnp.cumsum(padded)
    pad_start = pad_end - padded
    dest = pad_start[e_flat] + rank
    nblk = -(-(n_assign + N_EXPERTS * (MOE_BLK - 1)) // MOE_BLK)
    n_slots = nblk * MOE_BLK
    slot_a = jnp.zeros((n_slots,), jnp.int32).at[dest].set(jnp.arange(n_assign, dtype=jnp.int32))
    blk_start = jnp.arange(nblk, dtype=jnp.int32) * MOE_BLK
    blk_e = jnp.minimum(jnp.searchsorted(pad_end, blk_start, side='right'), N_EXPERTS - 1).astype(jnp.int32)
    blk_n = jnp.clip(counts[blk_e] - (blk_start - pad_start[blk_e]), 0, MOE_BLK).astype(jnp.int32)
    return blk_e, blk_n, slot_a


def _ple_out_kernel(h1_ref, y0_ref, y1_ref, rt_ref, p_ref, nple_ref, wpg_ref, wp_ref, nfin_ref, o_ref):
    rt = rt_ref[...]
    h2 = h1_ref[...] + rt[:, 2:3] * y0_ref[...] + rt[:, 3:4] * y1_ref[...]
    gate = jax.nn.sigmoid(_dot(_rms(h2, nple_ref[...]).astype(BF16), wpg_ref[...]))
    h3 = h2 + _dot(p_ref[...].astype(BF16), wp_ref[...]) * gate
    o_ref[...] = _rms(h3, nfin_ref[...])


def _ple_out(h1, ymoe, rt, p2d, nple, wpg, wp, nfin, row_off, t_all):
    t = p2d.shape[0]
    tm = TOK_TILE
    ob = row_off // tm
    kb = t_all // tm
    row = lambda w: pl.BlockSpec((tm, w), lambda i: (i, 0))
    orow = lambda w: pl.BlockSpec((tm, w), lambda i: (i + ob, 0))
    return pl.pallas_call(
        _ple_out_kernel,
        grid=(t // tm,),
        in_specs=[orow(D_MODEL), orow(D_MODEL),
                  pl.BlockSpec((tm, D_MODEL), lambda i: (i + ob + kb, 0)),
                  orow(LANES), row(D_PLE), _const_spec((1, D_MODEL)), _const_spec(wpg.shape),
                  _const_spec(wp.shape), _const_spec((1, D_MODEL))],
        out_specs=row(D_MODEL),
        out_shape=jax.ShapeDtypeStruct((t, D_MODEL), F32),
        compiler_params=pltpu.CompilerParams(dimension_semantics=("arbitrary",), vmem_limit_bytes=VMEM_LIMIT),
        name="ple_out",
    )(h1, ymoe, ymoe, rt, p2d, nple, wpg, wp, nfin)


def _s5_params(lam_re, lam_im, log_dt, b_re, b_im, c_re, c_im, d_skip):
    dt = jnp.exp(log_dt)[:, None]
    mag = jnp.exp(lam_re * dt)
    ab_re = mag * jnp.cos(lam_im * dt)
    ab_im = mag * jnp.sin(lam_im * dt)
    den = lam_re * lam_re + lam_im * lam_im
    nr = ab_re - 1.0
    f_re = (nr * lam_re + ab_im * lam_im) / den
    f_im = (ab_im * lam_re - nr * lam_im) / den
    bb_re = f_re[..., None] * b_re - f_im[..., None] * b_im
    bb_im = f_re[..., None] * b_im + f_im[..., None] * b_re
    eye = jnp.eye(S5_GB, dtype=F32)

    def bu_w(bb):
        bb = bb.reshape(S5_NGB, S5_GB, S5_STATE, S5_GROUP)
        w = jnp.einsum('nlph,lm->nlhmp', bb, eye)
        return w.reshape(S5_NGB, S5_GB * S5_GROUP, S5_SL)

    def c_w(c):
        c = c.reshape(S5_NGB, S5_GB, S5_GROUP, S5_STATE)
        w = jnp.einsum('nlhp,lm->nlpmh', c, eye)
        return w.reshape(S5_NGB, S5_SL, S5_GB * S5_GROUP)

    wbu = jnp.concatenate([bu_w(bb_re), bu_w(bb_im)], axis=2).astype(BF16)
    wc = jnp.concatenate([c_w(c_re), -c_w(c_im)], axis=1).astype(BF16)
    a_re = ab_re.reshape(S5_NGB, 1, S5_SL)
    a_im = ab_im.reshape(S5_NGB, 1, S5_SL)
    dsk = d_skip.reshape(S5_NGB, 1, S5_GB * S5_GROUP)
    return wbu, wc, a_re, a_im, dsk


def kernel(x_prompt, x_sample, p_prompt, p_sample, state_gla, state_s5_re, state_s5_im, norm_mix, w_in, gla_w_gate_up, gla_gate_bias, gla_norm, s5_lam_re, s5_lam_im, s5_log_dt, s5_b_re, s5_b_im, s5_c_re, s5_c_im, s5_d, s5_glu_w, s5_glu_b, s5_norm, w_out, norm_ffn, router_group, router_expert, w_gate, w_up, w_down, norm_ple, w_ple, w_ple_gate, norm_final):
    depth = w_in.shape[0]
    assert depth == 1
    i = 0
    bp, lp, _ = x_prompt.shape
    bs, ls, _ = x_sample.shape
    tp, ts = bp * lp, bs * ls
    t_all = tp + ts

    n_qkvr = 2 * QK_W + 2 * D_GLA
    wi = w_in[i]
    wm = jnp.concatenate([wi[:, :n_qkvr], wi[:, n_qkvr + GLA_RANK:]], axis=1).astype(BF16)
    wg = jnp.pad(wi[:, n_qkvr:n_qkvr + GLA_RANK], ((0, 0), (0, LANES - GLA_RANK))).astype(BF16)
    wgu = jnp.pad(gla_w_gate_up[i], ((0, LANES - GLA_RANK), (0, 0))).astype(BF16)
    gbias = gla_gate_bias[i].reshape(1, QK_W)
    wbu, wc, a_re, a_im, dsk = _s5_params(s5_lam_re[i], s5_lam_im[i], s5_log_dt[i], s5_b_re[i], s5_b_im[i],
                                          s5_c_re[i], s5_c_im[i], s5_d[i])
    glu_w = s5_glu_w[i].astype(BF16)
    wo = w_out[i].astype(BF16)
    wr32 = jnp.pad(jnp.concatenate([router_group[i], router_expert[i]], axis=1),
                   ((0, 0), (0, LANES - N_EGROUPS - N_EXPERTS)))
    wr_hi = wr32.astype(BF16)
    wr = jnp.stack([wr_hi, (wr32 - wr_hi.astype(F32)).astype(BF16)])
    wpg = w_ple_gate[i].astype(BF16)
    wp = w_ple[i].astype(BF16)
    vec = lambda a: a.reshape(1, -1)

    xp = x_prompt.reshape(tp, D_MODEL)
    xs = x_sample.reshape(ts, D_MODEL)

    qp, kp, vp, rp, lap, up = _in_proj(xp, vec(norm_mix[i]), wm, wg, wgu, gbias, BF16)
    qs, ks, vs, rs, las, us = _in_proj(xs, vec(norm_mix[i]), wm, wg, wgu, gbias, BF16)
    ogp, gla_p = _gla_prompt(qp, kp, vp, rp, lap, vec(gla_norm[i]), bp, lp)
    ogs, gla_s = _gla_sample(qs, ks, vs, rs, las, vec(gla_norm[i]), state_gla[i], bs, ls)
    wbu2, wc2, a_re2, a_im2 = _s5_prompt_params(wbu, wc, a_re, a_im, bp)
    ysp, re_p, im_p = _s5_prompt(up.reshape(bp, lp, D_S5), wbu2, wc2, a_re2, a_im2, dsk, tc=256)
    yss, re_s, im_s = _s5(us.reshape(bs, ls, D_S5), wbu, wc, a_re, a_im, dsk,
                          state_s5_re[i].reshape(bs, -1), state_s5_im[i].reshape(bs, -1),
                          nb=32, tc=ls, has_state=True)

    h1, hn_all, rt_all = _mix_out(xp, xs, ogp, ogs, ysp.reshape(tp, D_S5), yss.reshape(ts, D_S5), glu_w,
                                  vec(s5_glu_b[i]), vec(s5_norm[i]), wo, vec(norm_ffn[i]), wr)

    blk_e, blk_n, slot_a = _moe_plan(rt_all)
    ymoe = _moe(hn_all, blk_e, blk_n, slot_a, w_gate[i], w_up[i], w_down[i], 2 * t_all)

    po = functools.partial(_ple_out, h1, ymoe, rt_all, nple=vec(norm_ple[i]), wpg=wpg, wp=wp,
                           nfin=vec(norm_final), t_all=t_all)
    y_p = po(p_prompt[i].reshape(tp, D_PLE), row_off=0)
    y_s = po(p_sample[i].reshape(ts, D_PLE), row_off=tp)

    s5shape = lambda a, b: a.reshape(1, b, S5_GROUPS, S5_STATE)
    return (y_p.reshape(bp, lp, D_MODEL), y_s.reshape(bs, ls, D_MODEL),
            gla_p[None], s5shape(re_p, bp), s5shape(im_p, bp),
            gla_s[None], s5shape(re_s, bs), s5shape(im_s, bs))
```

```python
import functools
import math

import jax
import jax.numpy as jnp
from jax import lax
from jax.experimental import pallas as pl
from jax.experimental.pallas import tpu as pltpu

F32 = jnp.float32
BF16 = jnp.bfloat16

D_MODEL = 2048
D_GLA = 1024
D_S5 = 1024
GLA_HEADS = 4
GLA_DV = 256
GLA_DK = 128
GLA_RANK = 16
GLA_CHUNK = 64
S5_GROUP = 16
S5_GROUPS = 64
S5_STATE = 64
N_EGROUPS = 4
N_EPG = 8
N_EXPERTS = 32
D_EXPERT = 512
D_PLE = 256
EPS = 1e-6

LANES = 128
QK_W = GLA_HEADS * GLA_DK
S5_GB = 8
S5_NGB = S5_GROUPS // S5_GB
S5_SL = S5_GB * S5_STATE
TOK_TILE = 256
MOE_BLK = 256
VMEM_LIMIT = 56 * 1024 * 1024


def _const_spec(shape):
    nd = len(shape)
    return pl.BlockSpec(shape, lambda *_: (0,) * nd, pipeline_mode=pl.Buffered(1))


def _rms(x, g):
    return x * lax.rsqrt(jnp.mean(x * x, axis=-1, keepdims=True) + EPS) * g


def _dot(a, b):
    return jnp.dot(a, b, preferred_element_type=F32)


def _log_sigmoid(x):
    return -(jnp.maximum(-x, 0.0) + jnp.log1p(jnp.exp(-jnp.abs(x))))


def _in_proj_kernel(x_ref, g_ref, wm_ref, wg_ref, wgu_ref, gb_ref,
                    q_ref, k_ref, v_ref, r_ref, la_ref, u_ref):
    hb = _rms(x_ref[...], g_ref[...]).astype(BF16)

    def seg(a, b):
        return _dot(hb, wm_ref[:, a:b])

    q_ref[...] = (seg(0, QK_W) * (GLA_DK ** -0.5)).astype(q_ref.dtype)
    k_ref[...] = seg(QK_W, 2 * QK_W).astype(k_ref.dtype)
    v_ref[...] = seg(2 * QK_W, 2 * QK_W + D_GLA).astype(v_ref.dtype)
    r_ref[...] = seg(2 * QK_W + D_GLA, 2 * QK_W + 2 * D_GLA).astype(r_ref.dtype)
    u_ref[...] = seg(2 * QK_W + 2 * D_GLA, 2 * QK_W + 2 * D_GLA + D_S5)
    zg = _dot(hb, wg_ref[...])
    xg = _dot(zg.astype(BF16), wgu_ref[...]) + gb_ref[...]
    la_ref[...] = _log_sigmoid(xg) * (1.0 / 16.0)


def _in_proj(x2d, g, wm, wg, wgu, gbias, act_dtype):
    t = x2d.shape[0]
    tm = TOK_TILE
    row = lambda w: pl.BlockSpec((tm, w), lambda i: (i, 0))
    return pl.pallas_call(
        _in_proj_kernel,
        grid=(t // tm,),
        in_specs=[row(D_MODEL), _const_spec((1, D_MODEL)), _const_spec(wm.shape), _const_spec(wg.shape),
                  _const_spec(wgu.shape), _const_spec((1, QK_W))],
        out_specs=[row(QK_W), row(QK_W), row(D_GLA), row(D_GLA), row(QK_W), row(D_S5)],
        out_shape=[jax.ShapeDtypeStruct((t, QK_W), act_dtype), jax.ShapeDtypeStruct((t, QK_W), act_dtype),
                   jax.ShapeDtypeStruct((t, D_GLA), act_dtype), jax.ShapeDtypeStruct((t, D_GLA), act_dtype),
                   jax.ShapeDtypeStruct((t, QK_W), F32), jax.ShapeDtypeStruct((t, D_S5), F32)],
        compiler_params=pltpu.CompilerParams(dimension_semantics=("arbitrary",), vmem_limit_bytes=VMEM_LIMIT),
        name="in_proj",
    )(x2d, g, wm, wg, wgu, gbias)


_NT = (((1,), (1,)), ((), ()))
_TN = (((0,), (0,)), ((), ()))


def _gla_pre(q, k, la, c):
    r = q.shape[0]
    shift = int(math.log2(c))
    ri = lax.broadcasted_iota(jnp.int32, (r, r), 0)
    si = lax.broadcasted_iota(jnp.int32, (r, r), 1)
    mask = ((ri >> shift) == (si >> shift)) & (ri >= si)
    tri = jnp.where(mask, 1.0, 0.0).astype(BF16)
    hi = la.astype(BF16)
    r1 = la - hi.astype(F32)
    mid = r1.astype(BF16)
    lo = (r1 - mid.astype(F32)).astype(BF16)
    cum = _dot(tri, hi) + _dot(tri, mid) + _dot(tri, lo)
    last = jnp.concatenate([jnp.broadcast_to(cum[(i + 1) * c - 1:(i + 1) * c, :], (c, cum.shape[1]))
                            for i in range(r // c)], axis=0)
    qe = (q * jnp.exp(cum)).astype(BF16)
    ke = (k * jnp.exp(-cum)).astype(BF16)
    kd = (k * jnp.exp(last - cum)).astype(BF16)
    return qe, ke, kd, cum, mask


def _gla_intra(qe, ke, v, mask):
    sc = lax.dot_general(qe, ke, _NT, preferred_element_type=F32)
    return _dot(jnp.where(mask, sc, 0.0).astype(BF16), v)


def _gla_finish(o, r, g):
    rf = r.astype(F32)
    return _rms(o, g) * (rf * jax.nn.sigmoid(rf))


def _gla_prompt_kernel(q_ref, k_ref, v_ref, r_ref, la_ref, g_ref, o_ref, sfin_ref, st_ref, *, n_chunks):
    j = pl.program_id(1)

    @pl.when(j == 0)
    def _():
        st_ref[...] = jnp.zeros_like(st_ref)

    c = GLA_CHUNK
    qe, ke, kd, cum, mask = _gla_pre(q_ref[...].astype(F32), k_ref[...].astype(F32), la_ref[...], c)
    g = g_ref[...]
    for h in range(GLA_HEADS):
        kc = slice(h * GLA_DK, (h + 1) * GLA_DK)
        vc = slice(h * GLA_DV, (h + 1) * GLA_DV)
        v = v_ref[:, vc]
        qe_h, kd_h = qe[:, kc], kd[:, kc]
        o = _gla_intra(qe_h, ke[:, kc], v, mask)
        st = st_ref[h]
        inter = []
        for ci in range(n_chunks):
            rows = slice(ci * c, (ci + 1) * c)
            inter.append(lax.dot_general(qe_h[rows], st.astype(BF16), _NT, preferred_element_type=F32))
            dec = jnp.exp(cum[(ci + 1) * c - 1:(ci + 1) * c, kc])
            st = st * dec + lax.dot_general(v[rows], kd_h[rows], _TN, preferred_element_type=F32)
        st_ref[h] = st
        o = o + jnp.concatenate(inter, axis=0)
        o_ref[:, vc] = _gla_finish(o, r_ref[:, vc], g).astype(o_ref.dtype)

    @pl.when(j == pl.num_programs(1) - 1)
    def _():
        for h in range(GLA_HEADS):
            sfin_ref[0, h] = st_ref[h].T


def _gla_prompt(q, k, v, r, la, g, batch, seq):
    rb = 4 * GLA_CHUNK
    nj = seq // rb
    row = lambda w: pl.BlockSpec((rb, w), lambda b, j: (b * nj + j, 0))
    return pl.pallas_call(
        functools.partial(_gla_prompt_kernel, n_chunks=rb // GLA_CHUNK),
        grid=(batch, nj),
        in_specs=[row(QK_W), row(QK_W), row(D_GLA), row(D_GLA), row(QK_W), _const_spec((1, GLA_DV))],
        out_specs=[row(D_GLA),
                   pl.BlockSpec((1, GLA_HEADS, GLA_DK, GLA_DV), lambda b, j: (b, 0, 0, 0))],
        out_shape=[jax.ShapeDtypeStruct((batch * seq, D_GLA), BF16),
                   jax.ShapeDtypeStruct((batch, GLA_HEADS, GLA_DK, GLA_DV), F32)],
        scratch_shapes=[pltpu.VMEM((GLA_HEADS, GLA_DV, GLA_DK), F32)],
        compiler_params=pltpu.CompilerParams(dimension_semantics=("arbitrary", "arbitrary"),
                                             vmem_limit_bytes=VMEM_LIMIT),
        name="gla_prompt",
    )(q, k, v, r, la, g)


def _gla_sample_kernel(q_ref, k_ref, v_ref, r_ref, la_ref, g_ref, s0_ref, o_ref, sfin_ref, *, n_seq, seq):
    qe, ke, kd, cum, mask = _gla_pre(q_ref[...].astype(F32), k_ref[...].astype(F32), la_ref[...], seq)
    g = g_ref[...]
    lasts = jnp.concatenate([cum[(s + 1) * seq - 1:(s + 1) * seq, :] for s in range(n_seq)]
                            + [jnp.zeros((GLA_DK - n_seq, cum.shape[1]), F32)], axis=0)
    pair = 2 * seq
    upper = lax.broadcasted_iota(jnp.int32, (pair, GLA_DK), 0) < seq
    for h in range(GLA_HEADS):
        kc = slice(h * GLA_DK, (h + 1) * GLA_DK)
        vc = slice(h * GLA_DV, (h + 1) * GLA_DV)
        v = v_ref[:, vc]
        qe_h, kd_h = qe[:, kc], kd[:, kc]
        o = _gla_intra(qe_h, ke[:, kc], v, mask)
        dec_t = jnp.exp(lasts[:, kc].T)
        inter = []
        for p in range(n_seq // 2):
            rows = slice(p * pair, (p + 1) * pair)
            qe_p, kd_p, v_p = qe_h[rows], kd_h[rows], v[rows]
            for half in range(2):
                s = 2 * p + half
                s0 = s0_ref[s, h]
                o_s = _dot(qe_p, s0.astype(BF16))
                inter.append(o_s[half * seq:(half + 1) * seq])
                kd_s = jnp.where(upper if half == 0 else ~upper, kd_p, jnp.zeros_like(kd_p))
                dec = jnp.broadcast_to(dec_t[:, s:s + 1], (GLA_DK, GLA_DV))
                sfin_ref[s, h] = s0 * dec + lax.dot_general(kd_s, v_p, _TN, preferred_element_type=F32)
        o = o + jnp.concatenate(inter, axis=0)
        o_ref[:, vc] = _gla_finish(o, r_ref[:, vc], g).astype(o_ref.dtype)


def _gla_sample(q, k, v, r, la, g, s0, batch, seq):
    ns = 16
    rb = ns * seq
    row = lambda w: pl.BlockSpec((rb, w), lambda i: (i, 0))
    st = pl.BlockSpec((ns, GLA_HEADS, GLA_DK, GLA_DV), lambda i: (i, 0, 0, 0))
    return pl.pallas_call(
        functools.partial(_gla_sample_kernel, n_seq=ns, seq=seq),
        grid=(batch // ns,),
        in_specs=[row(QK_W), row(QK_W), row(D_GLA), row(D_GLA), row(QK_W), _const_spec((1, GLA_DV)), st],
        out_specs=[row(D_GLA), st],
        out_shape=[jax.ShapeDtypeStruct((batch * seq, D_GLA), BF16),
                   jax.ShapeDtypeStruct((batch, GLA_HEADS, GLA_DK, GLA_DV), F32)],
        compiler_params=pltpu.CompilerParams(dimension_semantics=("arbitrary",), vmem_limit_bytes=VMEM_LIMIT),
        name="gla_sample",
    )(q, k, v, r, la, g, s0)


def _s5_kernel(u_ref, wbu_ref, wc_ref, are_ref, aim_ref, d_ref, h0r_ref, h0i_ref,
               y_ref, sre_ref, sim_ref, bu_ref, xs_ref, car_ref, *, nb, tc, has_state):
    j = pl.program_id(2)

    @pl.when(j == 0)
    def _():
        if has_state:
            car_ref[0] = h0r_ref[...]
            car_ref[1] = h0i_ref[...]
        else:
            car_ref[...] = jnp.zeros_like(car_ref)

    u2 = u_ref[...].reshape(nb * tc, LANES)
    ub = u2.astype(BF16)
    nl = S5_SL // LANES
    for l in range(2 * nl):
        bu_ref[l] = _dot(ub, wbu_ref[0, :, l * LANES:(l + 1) * LANES])
    a_r = [jnp.broadcast_to(are_ref[0, :, l * LANES:(l + 1) * LANES], (nb, LANES)) for l in range(nl)]
    a_i = [jnp.broadcast_to(aim_ref[0, :, l * LANES:(l + 1) * LANES], (nb, LANES)) for l in range(nl)]

    def step(t, carry):
        rows = pl.ds(t, nb, stride=tc)
        out = []
        for l in range(nl):
            xr, xi = carry[2 * l], carry[2 * l + 1]
            nr = a_r[l] * xr - a_i[l] * xi + bu_ref[l, rows, :]
            ni = a_r[l] * xi + a_i[l] * xr + bu_ref[nl + l, rows, :]
            xs_ref[l, rows, :] = nr
            xs_ref[nl + l, rows, :] = ni
            out += [nr, ni]
        return tuple(out)

    init = []
    for l in range(nl):
        init += [car_ref[0, :, l * LANES:(l + 1) * LANES], car_ref[1, :, l * LANES:(l + 1) * LANES]]
    fin = lax.fori_loop(0, tc, step, tuple(init), unroll=8)
    xr = jnp.concatenate([fin[2 * l] for l in range(nl)], axis=1)
    xi = jnp.concatenate([fin[2 * l + 1] for l in range(nl)], axis=1)
    car_ref[0] = xr
    car_ref[1] = xi
    y = d_ref[0] * u2
    for l in range(2 * nl):
        y = y + _dot(xs_ref[l].astype(BF16), wc_ref[0, l * LANES:(l + 1) * LANES, :])
    y_ref[...] = y.reshape(nb, tc, LANES)

    @pl.when(j == pl.num_programs(2) - 1)
    def _():
        sre_ref[...] = xr
        sim_ref[...] = xi


def _s5(u3d, wbu, wc, a_re, a_im, dsk, h0r, h0i, nb, tc, has_state):
    batch, seq, _ = u3d.shape
    grid = (S5_NGB, batch // nb, seq // tc)
    st = pl.BlockSpec((nb, S5_SL), lambda g, b, j: (b, g))
    par = lambda w: pl.BlockSpec((1, 1, w), lambda g, b, j: (g, 0, 0))
    ublk = pl.BlockSpec((nb, tc, LANES), lambda g, b, j: (b, j, g))
    return pl.pallas_call(
        functools.partial(_s5_kernel, nb=nb, tc=tc, has_state=has_state),
        grid=grid,
        in_specs=[ublk,
                  pl.BlockSpec((1, LANES, 2 * S5_SL), lambda g, b, j: (g, 0, 0)),
                  pl.BlockSpec((1, 2 * S5_SL, LANES), lambda g, b, j: (g, 0, 0)),
                  par(S5_SL), par(S5_SL), par(LANES), st, st],
        out_specs=[ublk, st, st],
        out_shape=[jax.ShapeDtypeStruct(u3d.shape, F32),
                   jax.ShapeDtypeStruct((batch, S5_GROUPS * S5_STATE), F32),
                   jax.ShapeDtypeStruct((batch, S5_GROUPS * S5_STATE), F32)],
        scratch_shapes=[pltpu.VMEM((2 * S5_SL // LANES, nb * tc, LANES), F32),
                        pltpu.VMEM((2 * S5_SL // LANES, nb * tc, LANES), F32),
                        pltpu.VMEM((2, nb, S5_SL), F32)],
        compiler_params=pltpu.CompilerParams(dimension_semantics=("arbitrary",) * 3,
                                             vmem_limit_bytes=VMEM_LIMIT),
        name="s5_state" if has_state else "s5_zero",
    )(u3d, wbu, wc, a_re, a_im, dsk, h0r, h0i)


S5_J = 2
S5_HL = S5_SL // S5_J


def _s5_prompt_kernel(u_ref, wbu_ref, wc_ref, are_ref, aim_ref, d_ref, y_ref, sre_ref, sim_ref,
                      u2_ref, lhs_ref, bu_ref, xs_ref, y2_ref, yo_ref, car_ref, *, nb, tc):
    g = pl.program_id(0)
    j = pl.program_id(1)
    rows = nb * S5_J

    @pl.when((g == 0) & (j == 0))
    def _():
        lhs_ref[...] = jnp.zeros_like(lhs_ref)

    @pl.when(j == 0)
    def _():
        car_ref[...] = jnp.zeros_like(car_ref)

    u2 = u_ref[...].reshape(nb * tc, LANES)
    u2_ref[...] = u2

    def build(t, c):
        u4 = u2_ref[pl.ds(t, nb, stride=tc), :]
        for jj in range(S5_J):
            lhs_ref[t, jj * nb:(jj + 1) * nb, jj * LANES:(jj + 1) * LANES] = u4
        return c

    lax.fori_loop(0, tc, build, 0, unroll=8)
    lhs = lhs_ref[...].reshape(tc * rows, S5_J * LANES).astype(BF16)
    bu_ref[...] = _dot(lhs, wbu_ref[0]).reshape(tc, rows, 2 * S5_HL)
    ar = are_ref[0]
    ai = aim_ref[0]

    def step(t, carry):
        xr, xi = carry
        tile = bu_ref[t]
        nr = ar * xr - ai * xi + tile[:, 0:S5_HL]
        ni = ar * xi + ai * xr + tile[:, S5_HL:2 * S5_HL]
        xs_ref[t] = jnp.concatenate([nr, ni], axis=1)
        return nr, ni

    xr, xi = lax.fori_loop(0, tc, step, (car_ref[0], car_ref[1]), unroll=8)
    car_ref[0] = xr
    car_ref[1] = xi
    xs = xs_ref[...].reshape(tc * rows, 2 * S5_HL).astype(BF16)
    y2_ref[...] = _dot(xs, wc_ref[0]).reshape(tc, rows, S5_J * LANES)
    first_half = lax.broadcasted_iota(jnp.int32, (rows, LANES), 0) < nb

    def unperm(t, c):
        t2 = y2_ref[t]
        part = jnp.where(first_half, t2[:, 0:LANES], t2[:, LANES:2 * LANES])
        tot = part + pltpu.roll(part, nb, axis=0)
        yo_ref[pl.ds(t, nb, stride=tc), :] = tot[0:nb]
        return c

    lax.fori_loop(0, tc, unperm, 0, unroll=8)
    y_ref[...] = (yo_ref[...] + d_ref[0] * u2).reshape(nb, tc, LANES)

    @pl.when(j == pl.num_programs(1) - 1)
    def _():
        sre_ref[...] = jnp.concatenate([xr[jj * nb:(jj + 1) * nb] for jj in range(S5_J)], axis=1)
        sim_ref[...] = jnp.concatenate([xi[jj * nb:(jj + 1) * nb] for jj in range(S5_J)], axis=1)


def _s5_prompt(u3d, wbu2, wc2, a_re2, a_im2, dsk, tc):
    nb, seq, _ = u3d.shape
    rows = nb * S5_J
    assert rows == 8
    st = pl.BlockSpec((nb, S5_SL), lambda g, j: (0, g))
    gblk = lambda s: pl.BlockSpec((1,) + s, lambda g, j: (g, 0, 0))
    ublk = pl.BlockSpec((nb, tc, LANES), lambda g, j: (0, j, g))
    return pl.pallas_call(
        functools.partial(_s5_prompt_kernel, nb=nb, tc=tc),
        grid=(S5_NGB, seq // tc),
        in_specs=[ublk, gblk((S5_J * LANES, 2 * S5_HL)), gblk((2 * S5_HL, S5_J * LANES)),
                  gblk((rows, S5_HL)), gblk((rows, S5_HL)), gblk((1, LANES))],
        out_specs=[ublk, st, st],
        out_shape=[jax.ShapeDtypeStruct(u3d.shape, F32),
                   jax.ShapeDtypeStruct((nb, S5_GROUPS * S5_STATE), F32),
                   jax.ShapeDtypeStruct((nb, S5_GROUPS * S5_STATE), F32)],
        scratch_shapes=[pltpu.VMEM((nb * tc, LANES), F32),
                        pltpu.VMEM((tc, rows, S5_J * LANES), F32),
                        pltpu.VMEM((tc, rows, 2 * S5_HL), F32),
                        pltpu.VMEM((tc, rows, 2 * S5_HL), F32),
                        pltpu.VMEM((tc, rows, S5_J * LANES), F32),
                        pltpu.VMEM((nb * tc, LANES), F32),
                        pltpu.VMEM((2, rows, S5_HL), F32)],
        compiler_params=pltpu.CompilerParams(dimension_semantics=("arbitrary",) * 2,
                                             vmem_limit_bytes=VMEM_LIMIT),
        name="s5_prompt",
    )(u3d, wbu2, wc2, a_re2, a_im2, dsk)


def _s5_prompt_params(wbu, wc, a_re, a_im, nb):
    h = S5_HL
    top = jnp.concatenate([wbu[:, :, 0:h], wbu[:, :, S5_SL:S5_SL + h]], axis=2)
    bot = jnp.concatenate([wbu[:, :, h:2 * h], wbu[:, :, S5_SL + h:S5_SL + 2 * h]], axis=2)
    wbu2 = jnp.concatenate([top, bot], axis=1)
    wc_j = [jnp.concatenate([wc[:, jj * h:(jj + 1) * h, :], wc[:, S5_SL + jj * h:S5_SL + (jj + 1) * h, :]],
                            axis=1) for jj in range(S5_J)]
    wc2 = jnp.concatenate(wc_j, axis=2)
    tile = lambda a: jnp.repeat(a.reshape(S5_NGB, S5_J, h), nb, axis=1)
    return wbu2, wc2, tile(a_re), tile(a_im)


def _mix_out_kernel(xp_ref, xs_ref, ogp_ref, ogs_ref, ysp_ref, yss_ref, glu_w_ref, glu_b_ref, s5n_ref, wo_ref,
                    nffn_ref, wr_ref, h1_ref, hn_ref, rt_ref, *, n_prompt_tiles):
    is_p = pl.program_id(0) < n_prompt_tiles
    x = jnp.where(is_p, xp_ref[...], xs_ref[...])
    og = jnp.where(is_p, ogp_ref[...], ogs_ref[...])
    y = jax.nn.gelu(jnp.where(is_p, ysp_ref[...], yss_ref[...]))
    y = y * jax.nn.sigmoid(_dot(y.astype(BF16), glu_w_ref[...]) + glu_b_ref[...])
    yn = _rms(y, s5n_ref[...]).astype(BF16)
    mix = _dot(og, wo_ref[0:D_GLA, :]) + _dot(yn, wo_ref[D_GLA:D_GLA + D_S5, :])
    h1 = x + mix
    h1_ref[...] = h1
    hn = _rms(h1, nffn_ref[...])
    hn_ref[...] = hn
    hn_hi = hn.astype(BF16)
    hn_lo = (hn - hn_hi.astype(F32)).astype(BF16)
    logits = _dot(hn_hi, wr_ref[0]) + _dot(hn_hi, wr_ref[1]) + _dot(hn_lo, wr_ref[0])
    rt_ref[...] = _route(logits)


def _route(logits):
    col = lax.broadcasted_iota(jnp.int32, logits.shape, 1)
    colf = col.astype(F32)
    neg = -jnp.inf

    def first_argmax(vals):
        m = jnp.max(vals, axis=-1, keepdims=True)
        idx = jnp.min(jnp.where(vals == m, colf, float(LANES)), axis=-1, keepdims=True)
        return m, idx

    lg = jnp.where(col < N_EGROUPS, logits, neg)
    gmax, gsel = first_argmax(lg)
    p_g = 1.0 / jnp.sum(jnp.exp(lg - gmax), axis=-1, keepdims=True)
    ecol = col - N_EGROUPS
    egrp = (ecol >> 3).astype(F32)
    in_group = (ecol >= 0) & (ecol < N_EXPERTS) & (egrp == gsel)
    le = jnp.where(in_group, logits, neg)
    m1, i1 = first_argmax(le)
    le2 = jnp.where(colf == i1, neg, le)
    m2, i2 = first_argmax(le2)
    e2 = jnp.exp(m2 - m1)
    den = 1.0 + e2
    w1 = p_g * (1.0 / den)
    w2 = p_g * (e2 / den)
    e1f = i1 - float(N_EGROUPS)
    e2f = i2 - float(N_EGROUPS)
    out = jnp.where(col == 0, e1f, jnp.where(col == 1, e2f, jnp.where(col == 2, w1, jnp.where(col == 3, w2, 0.0))))
    return out


def _mix_out(xp, xs, ogp, ogs, ysp, yss, glu_w, glu_b, s5n, wo, nffn, wr):
    tm = TOK_TILE
    npt, nst = xp.shape[0] // tm, xs.shape[0] // tm
    t = (npt + nst) * tm
    row = lambda w: pl.BlockSpec((tm, w), lambda i: (i, 0))
    prow = lambda w: pl.BlockSpec((tm, w), lambda i: (jnp.minimum(i, npt - 1), 0))
    srow = lambda w: pl.BlockSpec((tm, w), lambda i: (jnp.maximum(i - npt, 0), 0))
    return pl.pallas_call(
        functools.partial(_mix_out_kernel, n_prompt_tiles=npt),
        grid=(npt + nst,),
        in_specs=[prow(D_MODEL), srow(D_MODEL), prow(D_GLA), srow(D_GLA), prow(D_S5), srow(D_S5),
                  _const_spec(glu_w.shape), _const_spec((1, D_S5)), _const_spec((1, D_S5)),
                  _const_spec(wo.shape), _const_spec((1, D_MODEL)), _const_spec(wr.shape)],
        out_specs=[row(D_MODEL), row(D_MODEL), row(LANES)],
        out_shape=[jax.ShapeDtypeStruct((t, D_MODEL), F32), jax.ShapeDtypeStruct((t, D_MODEL), F32),
                   jax.ShapeDtypeStruct((t, LANES), F32)],
        compiler_params=pltpu.CompilerParams(dimension_semantics=("arbitrary",), vmem_limit_bytes=VMEM_LIMIT),
        name="mix_out",
    )(xp, xs, ogp, ogs, ysp, yss, glu_w, glu_b, s5n, wo, nffn, wr)


def _moe_kernel(be_ref, nv_ref, sa_ref, hn_hbm, wg_ref, wu_ref, wd_ref, out_hbm,
                xb0_ref, xb1_ref, yb0_ref, yb1_ref, wgb_ref, wub_ref, wdb_ref, gsem, ssem):
    b = pl.program_id(0)
    nb = pl.num_programs(0)
    n = nv_ref[b]
    t_all = hn_hbm.shape[0]
    dump = 2 * t_all
    xbs, ybs = (xb0_ref, xb1_ref), (yb0_ref, yb1_ref)
    prev = jnp.maximum(b - 1, 0)
    n_prev = jnp.where(b > 0, nv_ref[prev], 0)

    def gather_row(base, i, xb, sem):
        tok = sa_ref[base + i] >> 1
        pltpu.make_async_copy(hn_hbm.at[pl.ds(tok, 1), :], xb.at[pl.ds(i, 1), :], sem).start()

    def scatter_row(base, cnt, i, yb, sem):
        a = sa_ref[base + i]
        dst = jnp.where(i < cnt, (a & 1) * t_all + (a >> 1), dump + i)
        pltpu.make_async_copy(yb.at[pl.ds(i, 1), :], out_hbm.at[pl.ds(dst, 1), :], sem).start()

    def wait_gather(xb, sem):
        pltpu.make_async_copy(hn_hbm.at[pl.ds(0, MOE_BLK), :], xb, sem).wait()

    def wait_scatter(yb, sem):
        pltpu.make_async_copy(yb, out_hbm.at[pl.ds(0, MOE_BLK), :], sem).wait()

    def scatter_exact(base, cnt, yb, sem):
        def start(i, c):
            scatter_row(base, cnt, i, yb, sem)
            return c

        def wait(i, c):
            pltpu.make_async_copy(yb.at[pl.ds(i, 1), :], out_hbm.at[pl.ds(0, 1), :], sem).wait()
            return c

        lax.fori_loop(0, cnt, start, 0)
        lax.fori_loop(0, cnt, wait, 0)

    @pl.when(b == 0)
    def _():
        for r in (xb0_ref, xb1_ref, yb0_ref, yb1_ref):
            r[...] = jnp.zeros_like(r)
        cp = pltpu.make_async_copy(yb1_ref, out_hbm.at[pl.ds(dump, MOE_BLK), :], ssem.at[1])
        cp.start()
        cp.wait()
        for i in range(MOE_BLK):
            gather_row(0, i, xb0_ref, gsem.at[0])

    prev_e = be_ref[prev]

    @pl.when((n > 0) & ((b == 0) | (prev_e != be_ref[b])))
    def _():
        wgb_ref[...] = wg_ref[...].astype(BF16)
        wub_ref[...] = wu_ref[...].astype(BF16)
        wdb_ref[...] = wd_ref[...].astype(BF16)

    def main(p):
        xb, xo, yb, yo = xbs[p], xbs[1 - p], ybs[p], ybs[1 - p]
        wait_gather(xb, gsem.at[p])

        @pl.when(b > 0)
        def _():
            wait_scatter(yb, ssem.at[p])

        nbase = jnp.minimum(b + 1, nb - 1) * MOE_BLK
        pbase = prev * MOE_BLK
        x = xb[...].astype(BF16)
        g_rows = MOE_BLK // 4
        halves = []
        for w_ref in (wgb_ref, wub_ref):
            cols = []
            for c in range(2):
                cols.append(_dot(x, w_ref[:, c * (D_EXPERT // 2):(c + 1) * (D_EXPERT // 2)]))
                piece = len(halves) * 2 + c
                for i in range(piece * g_rows, (piece + 1) * g_rows):
                    gather_row(nbase, i, xo, gsem.at[1 - p])
            halves.append(jnp.concatenate(cols, axis=1))
        gate, up = halves
        hid = (gate * jax.nn.sigmoid(gate) * up).astype(BF16)
        n_down = 8
        wcol = D_MODEL // n_down
        s_rows = MOE_BLK // n_down
        for c in range(n_down):
            yb[:, c * wcol:(c + 1) * wcol] = _dot(hid, wdb_ref[:, c * wcol:(c + 1) * wcol])
            for i in range(c * s_rows, (c + 1) * s_rows):
                scatter_row(pbase, n_prev, i, yo, ssem.at[1 - p])

        @pl.when(b == nb - 1)
        def _():
            wait_gather(xo, gsem.at[1 - p])
            wait_scatter(yo, ssem.at[1 - p])
            scatter_exact(b * MOE_BLK, n, yb, ssem.at[p])

    def drain(p):
        wait_gather(xbs[p], gsem.at[p])
        wait_scatter(ybs[p], ssem.at[p])
        scatter_exact(prev * MOE_BLK, n_prev, ybs[1 - p], ssem.at[1 - p])

    parity = lax.rem(b, 2)
    for p in range(2):
        @pl.when((n > 0) & (parity == p))
        def _():
            main(p)

        @pl.when((n == 0) & (n_prev > 0) & (parity == p))
        def _():
            drain(p)


def _moe(hn_all, blk_e, blk_n, slot_a, w_gate, w_up, w_down, n_out_rows):
    nblk = blk_e.shape[0]
    wspec = lambda s: pl.BlockSpec((None,) + s, lambda b, be, nv, sa: (be[b], 0, 0))
    grid_spec = pltpu.PrefetchScalarGridSpec(
        num_scalar_prefetch=3,
        grid=(nblk,),
        in_specs=[pl.BlockSpec(memory_space=pl.ANY),
                  wspec((D_MODEL, D_EXPERT)), wspec((D_MODEL, D_EXPERT)), wspec((D_EXPERT, D_MODEL))],
        out_specs=pl.BlockSpec(memory_space=pl.ANY),
        scratch_shapes=[pltpu.VMEM((MOE_BLK, D_MODEL), F32)] * 4
                       + [pltpu.VMEM((D_MODEL, D_EXPERT), BF16), pltpu.VMEM((D_MODEL, D_EXPERT), BF16),
                          pltpu.VMEM((D_EXPERT, D_MODEL), BF16), pltpu.SemaphoreType.DMA((2,)),
                          pltpu.SemaphoreType.DMA((2,))],
    )
    return pl.pallas_call(
        _moe_kernel,
        grid_spec=grid_spec,
        out_shape=jax.ShapeDtypeStruct((n_out_rows + MOE_BLK, D_MODEL), F32),
        compiler_params=pltpu.CompilerParams(dimension_semantics=("arbitrary",), vmem_limit_bytes=VMEM_LIMIT),
        name="moe_experts",
    )(blk_e, blk_n, slot_a, hn_all, w_gate, w_up, w_down)


def _moe_plan(rt_all):
    t_all = rt_all.shape[0]
    n_assign = 2 * t_all
    e_flat = rt_all[:, 0:2].astype(jnp.int32).reshape(-1)
    onehot = (e_flat[:, None] == jnp.arange(N_EXPERTS, dtype=jnp.int32)[None, :]).astype(jnp.int32)
    csum = jnp.cumsum(onehot, axis=0)
    counts = csum[-1]
    rank = jnp.take_along_axis(csum, e_flat[:, None], axis=1)[:, 0] - 1
    padded = (counts + MOE_BLK - 1) // MOE_BLK * MOE_BLK
    pad_end = jnp.cumsum(padded)
    pad_start = pad_end - padded
    dest = pad_start[e_flat] + rank
    nblk = -(-(n_assign + N_EXPERTS * (MOE_BLK - 1)) // MOE_BLK)
    n_slots = nblk * MOE_BLK
    slot_a = jnp.zeros((n_slots,), jnp.int32).at[dest].set(jnp.arange(n_assign, dtype=jnp.int32))
    blk_start = jnp.arange(nblk, dtype=jnp.int32) * MOE_BLK
    blk_e = jnp.minimum(jnp.searchsorted(pad_end, blk_start, side='right'), N_EXPERTS - 1).astype(jnp.int32)
    blk_n = jnp.clip(counts[blk_e] - (blk_start - pad_start[blk_e]), 0, MOE_BLK).astype(jnp.int32)
    return blk_e, blk_n, slot_a


def _ple_out_kernel(h1_ref, y0_ref, y1_ref, rt_ref, p_ref, nple_ref, wpg_ref, wp_ref, nfin_ref, o_ref):
    rt = rt_ref[...]
    h2 = h1_ref[...] + rt[:, 2:3] * y0_ref[...] + rt[:, 3:4] * y1_ref[...]
    gate = jax.nn.sigmoid(_dot(_rms(h2, nple_ref[...]).astype(BF16), wpg_ref[...]))
    h3 = h2 + _dot(p_ref[...].astype(BF16), wp_ref[...]) * gate
    o_ref[...] = _rms(h3, nfin_ref[...])


def _ple_out(h1, ymoe, rt, p2d, nple, wpg, wp, nfin, row_off, t_all):
    t = p2d.shape[0]
    tm = TOK_TILE
    ob = row_off // tm
    kb = t_all // tm
    row = lambda w: pl.BlockSpec((tm, w), lambda i: (i, 0))
    orow = lambda w: pl.BlockSpec((tm, w), lambda i: (i + ob, 0))
    return pl.pallas_call(
        _ple_out_kernel,
        grid=(t // tm,),
        in_specs=[orow(D_MODEL), orow(D_MODEL),
                  pl.BlockSpec((tm, D_MODEL), lambda i: (i + ob + kb, 0)),
                  orow(LANES), row(D_PLE), _const_spec((1, D_MODEL)), _const_spec(wpg.shape),
                  _const_spec(wp.shape), _const_spec((1, D_MODEL))],
        out_specs=row(D_MODEL),
        out_shape=jax.ShapeDtypeStruct((t, D_MODEL), F32),
        compiler_params=pltpu.CompilerParams(dimension_semantics=("arbitrary",), vmem_limit_bytes=VMEM_LIMIT),
        name="ple_out",
    )(h1, ymoe, ymoe, rt, p2d, nple, wpg, wp, nfin)


def _s5_params(lam_re, lam_im, log_dt, b_re, b_im, c_re, c_im, d_skip):
    dt = jnp.exp(log_dt)[:, None]
    mag = jnp.exp(lam_re * dt)
    ab_re = mag * jnp.cos(lam_im * dt)
    ab_im = mag * jnp.sin(lam_im * dt)
    den = lam_re * lam_re + lam_im * lam_im
    nr = ab_re - 1.0
    f_re = (nr * lam_re + ab_im * lam_im) / den
    f_im = (ab_im * lam_re - nr * lam_im) / den
    bb_re = f_re[..., None] * b_re - f_im[..., None] * b_im
    bb_im = f_re[..., None] * b_im + f_im[..., None] * b_re
    eye = jnp.eye(S5_GB, dtype=F32)

    def bu_w(bb):
        bb = bb.reshape(S5_NGB, S5_GB, S5_STATE, S5_GROUP)
        w = jnp.einsum('nlph,lm->nlhmp', bb, eye)
        return w.reshape(S5_NGB, S5_GB * S5_GROUP, S5_SL)

    def c_w(c):
        c = c.reshape(S5_NGB, S5_GB, S5_GROUP, S5_STATE)
        w = jnp.einsum('nlhp,lm->nlpmh', c, eye)
        return w.reshape(S5_NGB, S5_SL, S5_GB * S5_GROUP)

    wbu = jnp.concatenate([bu_w(bb_re), bu_w(bb_im)], axis=2).astype(BF16)
    wc = jnp.concatenate([c_w(c_re), -c_w(c_im)], axis=1).astype(BF16)
    a_re = ab_re.reshape(S5_NGB, 1, S5_SL)
    a_im = ab_im.reshape(S5_NGB, 1, S5_SL)
    dsk = d_skip.reshape(S5_NGB, 1, S5_GB * S5_GROUP)
    return wbu, wc, a_re, a_im, dsk


def kernel(x_prompt, x_sample, p_prompt, p_sample, state_gla, state_s5_re, state_s5_im, norm_mix, w_in, gla_w_gate_up, gla_gate_bias, gla_norm, s5_lam_re, s5_lam_im, s5_log_dt, s5_b_re, s5_b_im, s5_c_re, s5_c_im, s5_d, s5_glu_w, s5_glu_b, s5_norm, w_out, norm_ffn, router_group, router_expert, w_gate, w_up, w_down, norm_ple, w_ple, w_ple_gate, norm_final):
    depth = w_in.shape[0]
    assert depth == 1
    i = 0
    bp, lp, _ = x_prompt.shape
    bs, ls, _ = x_sample.shape
    tp, ts = bp * lp, bs * ls
    t_all = tp + ts

    n_qkvr = 2 * QK_W + 2 * D_GLA
    wi = w_in[i]
    wm = jnp.concatenate([wi[:, :n_qkvr], wi[:, n_qkvr + GLA_RANK:]], axis=1).astype(BF16)
    wg = jnp.pad(wi[:, n_qkvr:n_qkvr + GLA_RANK], ((0, 0), (0, LANES - GLA_RANK))).astype(BF16)
    wgu = jnp.pad(gla_w_gate_up[i], ((0, LANES - GLA_RANK), (0, 0))).astype(BF16)
    gbias = gla_gate_bias[i].reshape(1, QK_W)
    wbu, wc, a_re, a_im, dsk = _s5_params(s5_lam_re[i], s5_lam_im[i], s5_log_dt[i], s5_b_re[i], s5_b_im[i],
                                          s5_c_re[i], s5_c_im[i], s5_d[i])
    glu_w = s5_glu_w[i].astype(BF16)
    wo = w_out[i].astype(BF16)
    wr32 = jnp.pad(jnp.concatenate([router_group[i], router_expert[i]], axis=1),
                   ((0, 0), (0, LANES - N_EGROUPS - N_EXPERTS)))
    wr_hi = wr32.astype(BF16)
    wr = jnp.stack([wr_hi, (wr32 - wr_hi.astype(F32)).astype(BF16)])
    wpg = w_ple_gate[i].astype(BF16)
    wp = w_ple[i].astype(BF16)
    vec = lambda a: a.reshape(1, -1)

    xp = x_prompt.reshape(tp, D_MODEL)
    xs = x_sample.reshape(ts, D_MODEL)

    qp, kp, vp, rp, lap, up = _in_proj(xp, vec(norm_mix[i]), wm, wg, wgu, gbias, BF16)
    qs, ks, vs, rs, las, us = _in_proj(xs, vec(norm_mix[i]), wm, wg, wgu, gbias, BF16)
    ogp, gla_p = _gla_prompt(qp, kp, vp, rp, lap, vec(gla_norm[i]), bp, lp)
    ogs, gla_s = _gla_sample(qs, ks, vs, rs, las, vec(gla_norm[i]), state_gla[i], bs, ls)
    wbu2, wc2, a_re2, a_im2 = _s5_prompt_params(wbu, wc, a_re, a_im, bp)
    ysp, re_p, im_p = _s5_prompt(up.reshape(bp, lp, D_S5), wbu2, wc2, a_re2, a_im2, dsk, tc=256)
    yss, re_s, im_s = _s5(us.reshape(bs, ls, D_S5), wbu, wc, a_re, a_im, dsk,
                          state_s5_re[i].reshape(bs, -1), state_s5_im[i].reshape(bs, -1),
                          nb=32, tc=ls, has_state=True)

    h1, hn_all, rt_all = _mix_out(xp, xs, ogp, ogs, ysp.reshape(tp, D_S5), yss.reshape(ts, D_S5), glu_w,
                                  vec(s5_glu_b[i]), vec(s5_norm[i]), wo, vec(norm_ffn[i]), wr)

    blk_e, blk_n, slot_a = _moe_plan(rt_all)
    ymoe = _moe(hn_all, blk_e, blk_n, slot_a, w_gate[i], w_up[i], w_down[i], 2 * t_all)

    po = functools.partial(_ple_out, h1, ymoe, rt_all, nple=vec(norm_ple[i]), wpg=wpg, wp=wp,
                           nfin=vec(norm_final), t_all=t_all)
    y_p = po(p_prompt[i].reshape(tp, D_PLE), row_off=0)
    y_s = po(p_sample[i].reshape(ts, D_PLE), row_off=tp)

    s5shape = lambda a, b: a.reshape(1, b, S5_GROUPS, S5_STATE)
    return (y_p.reshape(bp, lp, D_MODEL), y_s.reshape(bs, ls, D_MODEL),
            gla_p[None], s5shape(re_p, bp), s5shape(im_p, bp),
            gla_s[None], s5shape(re_s, bs), s5shape(im_s, bs))
```

```python
import functools
import math

import jax
import jax.numpy as jnp
from jax import lax
from jax.experimental import pallas as pl
from jax.experimental.pallas import tpu as pltpu

F32 = jnp.float32
BF16 = jnp.bfloat16

D_MODEL = 2048
D_GLA = 1024
D_S5 = 1024
GLA_HEADS = 4
GLA_DV = 256
GLA_DK = 128
GLA_RANK = 16
GLA_CHUNK = 64
S5_GROUP = 16
S5_GROUPS = 64
S5_STATE = 64
N_EGROUPS = 4
N_EPG = 8
N_EXPERTS = 32
D_EXPERT = 512
D_PLE = 256
EPS = 1e-6

LANES = 128
QK_W = GLA_HEADS * GLA_DK
S5_GB = 8
S5_NGB = S5_GROUPS // S5_GB
S5_SL = S5_GB * S5_STATE
TOK_TILE = 256
MOE_BLK = 256
VMEM_LIMIT = 56 * 1024 * 1024
MOE_VMEM_LIMIT = 60 * 1024 * 1024


def _const_spec(shape):
    nd = len(shape)
    return pl.BlockSpec(shape, lambda *_: (0,) * nd, pipeline_mode=pl.Buffered(1))


def _rms(x, g):
    return x * lax.rsqrt(jnp.mean(x * x, axis=-1, keepdims=True) + EPS) * g


def _dot(a, b):
    return jnp.dot(a, b, preferred_element_type=F32)


def _log_sigmoid(x):
    return -(jnp.maximum(-x, 0.0) + jnp.log1p(jnp.exp(-jnp.abs(x))))


def _in_proj_kernel(x_ref, g_ref, wm_ref, wg_ref, wgu_ref, gb_ref,
                    q_ref, k_ref, v_ref, r_ref, la_ref, u_ref):
    hb = _rms(x_ref[...], g_ref[...]).astype(BF16)

    def seg(a, b):
        return _dot(hb, wm_ref[:, a:b])

    q_ref[...] = (seg(0, QK_W) * (GLA_DK ** -0.5)).astype(q_ref.dtype)
    k_ref[...] = seg(QK_W, 2 * QK_W).astype(k_ref.dtype)
    v_ref[...] = seg(2 * QK_W, 2 * QK_W + D_GLA).astype(v_ref.dtype)
    r_ref[...] = seg(2 * QK_W + D_GLA, 2 * QK_W + 2 * D_GLA).astype(r_ref.dtype)
    u_ref[...] = seg(2 * QK_W + 2 * D_GLA, 2 * QK_W + 2 * D_GLA + D_S5)
    zg = _dot(hb, wg_ref[...])
    xg = _dot(zg.astype(BF16), wgu_ref[...]) + gb_ref[...]
    la_ref[...] = _log_sigmoid(xg) * (1.0 / 16.0)


def _in_proj(x2d, g, wm, wg, wgu, gbias, act_dtype):
    t = x2d.shape[0]
    tm = TOK_TILE
    row = lambda w: pl.BlockSpec((tm, w), lambda i: (i, 0))
    return pl.pallas_call(
        _in_proj_kernel,
        grid=(t // tm,),
        in_specs=[row(D_MODEL), _const_spec((1, D_MODEL)), _const_spec(wm.shape), _const_spec(wg.shape),
                  _const_spec(wgu.shape), _const_spec((1, QK_W))],
        out_specs=[row(QK_W), row(QK_W), row(D_GLA), row(D_GLA), row(QK_W), row(D_S5)],
        out_shape=[jax.ShapeDtypeStruct((t, QK_W), act_dtype), jax.ShapeDtypeStruct((t, QK_W), act_dtype),
                   jax.ShapeDtypeStruct((t, D_GLA), act_dtype), jax.ShapeDtypeStruct((t, D_GLA), act_dtype),
                   jax.ShapeDtypeStruct((t, QK_W), F32), jax.ShapeDtypeStruct((t, D_S5), F32)],
        compiler_params=pltpu.CompilerParams(dimension_semantics=("arbitrary",), vmem_limit_bytes=VMEM_LIMIT),
        name="in_proj",
    )(x2d, g, wm, wg, wgu, gbias)


_NT = (((1,), (1,)), ((), ()))
_TN = (((0,), (0,)), ((), ()))


def _gla_pre(q, k, la, c):
    r = q.shape[0]
    shift = int(math.log2(c))
    ri = lax.broadcasted_iota(jnp.int32, (r, r), 0)
    si = lax.broadcasted_iota(jnp.int32, (r, r), 1)
    mask = ((ri >> shift) == (si >> shift)) & (ri >= si)
    tri = jnp.where(mask, 1.0, 0.0).astype(BF16)
    hi = la.astype(BF16)
    r1 = la - hi.astype(F32)
    mid = r1.astype(BF16)
    lo = (r1 - mid.astype(F32)).astype(BF16)
    cum = _dot(tri, hi) + _dot(tri, mid) + _dot(tri, lo)
    last = jnp.concatenate([jnp.broadcast_to(cum[(i + 1) * c - 1:(i + 1) * c, :], (c, cum.shape[1]))
                            for i in range(r // c)], axis=0)
    qe = (q * jnp.exp(cum)).astype(BF16)
    ke = (k * jnp.exp(-cum)).astype(BF16)
    kd = (k * jnp.exp(last - cum)).astype(BF16)
    return qe, ke, kd, cum, mask


def _gla_intra(qe, ke, v, mask):
    sc = lax.dot_general(qe, ke, _NT, preferred_element_type=F32)
    return _dot(jnp.where(mask, sc, 0.0).astype(BF16), v)


def _gla_finish(o, r, g):
    rf = r.astype(F32)
    return _rms(o, g) * (rf * jax.nn.sigmoid(rf))


def _gla_prompt_kernel(q_ref, k_ref, v_ref, r_ref, la_ref, g_ref, o_ref, sfin_ref, st_ref, *, n_chunks):
    j = pl.program_id(1)

    @pl.when(j == 0)
    def _():
        st_ref[...] = jnp.zeros_like(st_ref)

    c = GLA_CHUNK
    qe, ke, kd, cum, mask = _gla_pre(q_ref[...].astype(F32), k_ref[...].astype(F32), la_ref[...], c)
    g = g_ref[...]
    for h in range(GLA_HEADS):
        kc = slice(h * GLA_DK, (h + 1) * GLA_DK)
        vc = slice(h * GLA_DV, (h + 1) * GLA_DV)
        v = v_ref[:, vc]
        qe_h, kd_h = qe[:, kc], kd[:, kc]
        o = _gla_intra(qe_h, ke[:, kc], v, mask)
        st = st_ref[h]
        inter = []
        for ci in range(n_chunks):
            rows = slice(ci * c, (ci + 1) * c)
            inter.append(lax.dot_general(qe_h[rows], st.astype(BF16), _NT, preferred_element_type=F32))
            dec = jnp.exp(cum[(ci + 1) * c - 1:(ci + 1) * c, kc])
            st = st * dec + lax.dot_general(v[rows], kd_h[rows], _TN, preferred_element_type=F32)
        st_ref[h] = st
        o = o + jnp.concatenate(inter, axis=0)
        o_ref[:, vc] = _gla_finish(o, r_ref[:, vc], g).astype(o_ref.dtype)

    @pl.when(j == pl.num_programs(1) - 1)
    def _():
        for h in range(GLA_HEADS):
            sfin_ref[0, h] = st_ref[h].T


def _gla_prompt(q, k, v, r, la, g, batch, seq):
    rb = 4 * GLA_CHUNK
    nj = seq // rb
    row = lambda w: pl.BlockSpec((rb, w), lambda b, j: (b * nj + j, 0))
    return pl.pallas_call(
        functools.partial(_gla_prompt_kernel, n_chunks=rb // GLA_CHUNK),
        grid=(batch, nj),
        in_specs=[row(QK_W), row(QK_W), row(D_GLA), row(D_GLA), row(QK_W), _const_spec((1, GLA_DV))],
        out_specs=[row(D_GLA),
                   pl.BlockSpec((1, GLA_HEADS, GLA_DK, GLA_DV), lambda b, j: (b, 0, 0, 0))],
        out_shape=[jax.ShapeDtypeStruct((batch * seq, D_GLA), BF16),
                   jax.ShapeDtypeStruct((batch, GLA_HEADS, GLA_DK, GLA_DV), F32)],
        scratch_shapes=[pltpu.VMEM((GLA_HEADS, GLA_DV, GLA_DK), F32)],
        compiler_params=pltpu.CompilerParams(dimension_semantics=("arbitrary", "arbitrary"),
                                             vmem_limit_bytes=VMEM_LIMIT),
        name="gla_prompt",
    )(q, k, v, r, la, g)


def _gla_sample_kernel(q_ref, k_ref, v_ref, r_ref, la_ref, g_ref, s0_ref, o_ref, sfin_ref, *, n_seq, seq):
    qe, ke, kd, cum, mask = _gla_pre(q_ref[...].astype(F32), k_ref[...].astype(F32), la_ref[...], seq)
    g = g_ref[...]
    lasts = jnp.concatenate([cum[(s + 1) * seq - 1:(s + 1) * seq, :] for s in range(n_seq)]
                            + [jnp.zeros((GLA_DK - n_seq, cum.shape[1]), F32)], axis=0)
    pair = 2 * seq
    upper = lax.broadcasted_iota(jnp.int32, (pair, GLA_DK), 0) < seq
    for h in range(GLA_HEADS):
        kc = slice(h * GLA_DK, (h + 1) * GLA_DK)
        vc = slice(h * GLA_DV, (h + 1) * GLA_DV)
        v = v_ref[:, vc]
        qe_h, kd_h = qe[:, kc], kd[:, kc]
        o = _gla_intra(qe_h, ke[:, kc], v, mask)
        dec_t = jnp.exp(lasts[:, kc].T)
        inter = []
        for p in range(n_seq // 2):
            rows = slice(p * pair, (p + 1) * pair)
            qe_p, kd_p, v_p = qe_h[rows], kd_h[rows], v[rows]
            for half in range(2):
                s = 2 * p + half
                s0 = s0_ref[s, h]
                o_s = _dot(qe_p, s0.astype(BF16))
                inter.append(o_s[half * seq:(half + 1) * seq])
                kd_s = jnp.where(upper if half == 0 else ~upper, kd_p, jnp.zeros_like(kd_p))
                dec = jnp.broadcast_to(dec_t[:, s:s + 1], (GLA_DK, GLA_DV))
                sfin_ref[s, h] = s0 * dec + lax.dot_general(kd_s, v_p, _TN, preferred_element_type=F32)
        o = o + jnp.concatenate(inter, axis=0)
        o_ref[:, vc] = _gla_finish(o, r_ref[:, vc], g).astype(o_ref.dtype)


def _gla_sample(q, k, v, r, la, g, s0, batch, seq):
    ns = 16
    rb = ns * seq
    row = lambda w: pl.BlockSpec((rb, w), lambda i: (i, 0))
    st = pl.BlockSpec((ns, GLA_HEADS, GLA_DK, GLA_DV), lambda i: (i, 0, 0, 0))
    return pl.pallas_call(
        functools.partial(_gla_sample_kernel, n_seq=ns, seq=seq),
        grid=(batch // ns,),
        in_specs=[row(QK_W), row(QK_W), row(D_GLA), row(D_GLA), row(QK_W), _const_spec((1, GLA_DV)), st],
        out_specs=[row(D_GLA), st],
        out_shape=[jax.ShapeDtypeStruct((batch * seq, D_GLA), BF16),
                   jax.ShapeDtypeStruct((batch, GLA_HEADS, GLA_DK, GLA_DV), F32)],
        compiler_params=pltpu.CompilerParams(dimension_semantics=("arbitrary",), vmem_limit_bytes=VMEM_LIMIT),
        name="gla_sample",
    )(q, k, v, r, la, g, s0)


def _s5_kernel(u_ref, wbu_ref, wc_ref, are_ref, aim_ref, d_ref, h0r_ref, h0i_ref,
               y_ref, sre_ref, sim_ref, bu_ref, xs_ref, car_ref, *, nb, tc, has_state):
    j = pl.program_id(2)

    @pl.when(j == 0)
    def _():
        if has_state:
            car_ref[0] = h0r_ref[...]
            car_ref[1] = h0i_ref[...]
        else:
            car_ref[...] = jnp.zeros_like(car_ref)

    u2 = u_ref[...].reshape(nb * tc, LANES)
    ub = u2.astype(BF16)
    nl = S5_SL // LANES
    for l in range(2 * nl):
        bu_ref[l] = _dot(ub, wbu_ref[0, :, l * LANES:(l + 1) * LANES])
    a_r = [jnp.broadcast_to(are_ref[0, :, l * LANES:(l + 1) * LANES], (nb, LANES)) for l in range(nl)]
    a_i = [jnp.broadcast_to(aim_ref[0, :, l * LANES:(l + 1) * LANES], (nb, LANES)) for l in range(nl)]

    def step(t, carry):
        rows = pl.ds(t, nb, stride=tc)
        out = []
        for l in range(nl):
            xr, xi = carry[2 * l], carry[2 * l + 1]
            nr = a_r[l] * xr - a_i[l] * xi + bu_ref[l, rows, :]
            ni = a_r[l] * xi + a_i[l] * xr + bu_ref[nl + l, rows, :]
            xs_ref[l, rows, :] = nr
            xs_ref[nl + l, rows, :] = ni
            out += [nr, ni]
        return tuple(out)

    init = []
    for l in range(nl):
        init += [car_ref[0, :, l * LANES:(l + 1) * LANES], car_ref[1, :, l * LANES:(l + 1) * LANES]]
    fin = lax.fori_loop(0, tc, step, tuple(init), unroll=8)
    xr = jnp.concatenate([fin[2 * l] for l in range(nl)], axis=1)
    xi = jnp.concatenate([fin[2 * l + 1] for l in range(nl)], axis=1)
    car_ref[0] = xr
    car_ref[1] = xi
    y = d_ref[0] * u2
    for l in range(2 * nl):
        y = y + _dot(xs_ref[l].astype(BF16), wc_ref[0, l * LANES:(l + 1) * LANES, :])
    y_ref[...] = y.reshape(nb, tc, LANES)

    @pl.when(j == pl.num_programs(2) - 1)
    def _():
        sre_ref[...] = xr
        sim_ref[...] = xi


def _s5(u3d, wbu, wc, a_re, a_im, dsk, h0r, h0i, nb, tc, has_state):
    batch, seq, _ = u3d.shape
    grid = (S5_NGB, batch // nb, seq // tc)
    st = pl.BlockSpec((nb, S5_SL), lambda g, b, j: (b, g))
    par = lambda w: pl.BlockSpec((1, 1, w), lambda g, b, j: (g, 0, 0))
    ublk = pl.BlockSpec((nb, tc, LANES), lambda g, b, j: (b, j, g))
    return pl.pallas_call(
        functools.partial(_s5_kernel, nb=nb, tc=tc, has_state=has_state),
        grid=grid,
        in_specs=[ublk,
                  pl.BlockSpec((1, LANES, 2 * S5_SL), lambda g, b, j: (g, 0, 0)),
                  pl.BlockSpec((1, 2 * S5_SL, LANES), lambda g, b, j: (g, 0, 0)),
                  par(S5_SL), par(S5_SL), par(LANES), st, st],
        out_specs=[ublk, st, st],
        out_shape=[jax.ShapeDtypeStruct(u3d.shape, F32),
                   jax.ShapeDtypeStruct((batch, S5_GROUPS * S5_STATE), F32),
                   jax.ShapeDtypeStruct((batch, S5_GROUPS * S5_STATE), F32)],
        scratch_shapes=[pltpu.VMEM((2 * S5_SL // LANES, nb * tc, LANES), F32),
                        pltpu.VMEM((2 * S5_SL // LANES, nb * tc, LANES), F32),
                        pltpu.VMEM((2, nb, S5_SL), F32)],
        compiler_params=pltpu.CompilerParams(dimension_semantics=("arbitrary",) * 3,
                                             vmem_limit_bytes=VMEM_LIMIT),
        name="s5_state" if has_state else "s5_zero",
    )(u3d, wbu, wc, a_re, a_im, dsk, h0r, h0i)


S5_J = 2
S5_HL = S5_SL // S5_J


def _s5_prompt_kernel(u_ref, wbu_ref, wc_ref, are_ref, aim_ref, d_ref, y_ref, sre_ref, sim_ref,
                      u2_ref, lhs_ref, bu_ref, xs_ref, y2_ref, yo_ref, car_ref, *, nb, tc):
    g = pl.program_id(0)
    j = pl.program_id(1)
    rows = nb * S5_J

    @pl.when((g == 0) & (j == 0))
    def _():
        lhs_ref[...] = jnp.zeros_like(lhs_ref)

    @pl.when(j == 0)
    def _():
        car_ref[...] = jnp.zeros_like(car_ref)

    u2 = u_ref[...].reshape(nb * tc, LANES)
    u2_ref[...] = u2

    def build(t, c):
        u4 = u2_ref[pl.ds(t, nb, stride=tc), :]
        for jj in range(S5_J):
            lhs_ref[t, jj * nb:(jj + 1) * nb, jj * LANES:(jj + 1) * LANES] = u4
        return c

    lax.fori_loop(0, tc, build, 0, unroll=8)
    lhs = lhs_ref[...].reshape(tc * rows, S5_J * LANES).astype(BF16)
    bu_ref[...] = _dot(lhs, wbu_ref[0]).reshape(tc, rows, 2 * S5_HL)
    ar = are_ref[0]
    ai = aim_ref[0]

    def step(t, carry):
        xr, xi = carry
        tile = bu_ref[t]
        nr = ar * xr - ai * xi + tile[:, 0:S5_HL]
        ni = ar * xi + ai * xr + tile[:, S5_HL:2 * S5_HL]
        xs_ref[t] = jnp.concatenate([nr, ni], axis=1)
        return nr, ni

    xr, xi = lax.fori_loop(0, tc, step, (car_ref[0], car_ref[1]), unroll=8)
    car_ref[0] = xr
    car_ref[1] = xi
    xs = xs_ref[...].reshape(tc * rows, 2 * S5_HL).astype(BF16)
    y2_ref[...] = _dot(xs, wc_ref[0]).reshape(tc, rows, S5_J * LANES)
    first_half = lax.broadcasted_iota(jnp.int32, (rows, LANES), 0) < nb

    def unperm(t, c):
        t2 = y2_ref[t]
        part = jnp.where(first_half, t2[:, 0:LANES], t2[:, LANES:2 * LANES])
        tot = part + pltpu.roll(part, nb, axis=0)
        yo_ref[pl.ds(t, nb, stride=tc), :] = tot[0:nb]
        return c

    lax.fori_loop(0, tc, unperm, 0, unroll=8)
    y_ref[...] = (yo_ref[...] + d_ref[0] * u2).reshape(nb, tc, LANES)

    @pl.when(j == pl.num_programs(1) - 1)
    def _():
        sre_ref[...] = jnp.concatenate([xr[jj * nb:(jj + 1) * nb] for jj in range(S5_J)], axis=1)
        sim_ref[...] = jnp.concatenate([xi[jj * nb:(jj + 1) * nb] for jj in range(S5_J)], axis=1)


def _s5_prompt(u3d, wbu2, wc2, a_re2, a_im2, dsk, tc):
    nb, seq, _ = u3d.shape
    rows = nb * S5_J
    assert rows == 8
    st = pl.BlockSpec((nb, S5_SL), lambda g, j: (0, g))
    gblk = lambda s: pl.BlockSpec((1,) + s, lambda g, j: (g, 0, 0))
    ublk = pl.BlockSpec((nb, tc, LANES), lambda g, j: (0, j, g))
    return pl.pallas_call(
        functools.partial(_s5_prompt_kernel, nb=nb, tc=tc),
        grid=(S5_NGB, seq // tc),
        in_specs=[ublk, gblk((S5_J * LANES, 2 * S5_HL)), gblk((2 * S5_HL, S5_J * LANES)),
                  gblk((rows, S5_HL)), gblk((rows, S5_HL)), gblk((1, LANES))],
        out_specs=[ublk, st, st],
        out_shape=[jax.ShapeDtypeStruct(u3d.shape, F32),
                   jax.ShapeDtypeStruct((nb, S5_GROUPS * S5_STATE), F32),
                   jax.ShapeDtypeStruct((nb, S5_GROUPS * S5_STATE), F32)],
        scratch_shapes=[pltpu.VMEM((nb * tc, LANES), F32),
                        pltpu.VMEM((tc, rows, S5_J * LANES), F32),
                        pltpu.VMEM((tc, rows, 2 * S5_HL), F32),
                        pltpu.VMEM((tc, rows, 2 * S5_HL), F32),
                        pltpu.VMEM((tc, rows, S5_J * LANES), F32),
                        pltpu.VMEM((nb * tc, LANES), F32),
                        pltpu.VMEM((2, rows, S5_HL), F32)],
        compiler_params=pltpu.CompilerParams(dimension_semantics=("arbitrary",) * 2,
                                             vmem_limit_bytes=VMEM_LIMIT),
        name="s5_prompt",
    )(u3d, wbu2, wc2, a_re2, a_im2, dsk)


def _s5_prompt_params(wbu, wc, a_re, a_im, nb):
    h = S5_HL
    top = jnp.concatenate([wbu[:, :, 0:h], wbu[:, :, S5_SL:S5_SL + h]], axis=2)
    bot = jnp.concatenate([wbu[:, :, h:2 * h], wbu[:, :, S5_SL + h:S5_SL + 2 * h]], axis=2)
    wbu2 = jnp.concatenate([top, bot], axis=1)
    wc_j = [jnp.concatenate([wc[:, jj * h:(jj + 1) * h, :], wc[:, S5_SL + jj * h:S5_SL + (jj + 1) * h, :]],
                            axis=1) for jj in range(S5_J)]
    wc2 = jnp.concatenate(wc_j, axis=2)
    tile = lambda a: jnp.repeat(a.reshape(S5_NGB, S5_J, h), nb, axis=1)
    return wbu2, wc2, tile(a_re), tile(a_im)


def _mix_out_kernel(xp_ref, xs_ref, ogp_ref, ogs_ref, ysp_ref, yss_ref, glu_w_ref, glu_b_ref, s5n_ref, wo_ref,
                    nffn_ref, wr_ref, h1_ref, hn_ref, rt_ref, *, n_prompt_tiles):
    is_p = pl.program_id(0) < n_prompt_tiles
    x = jnp.where(is_p, xp_ref[...], xs_ref[...])
    og = jnp.where(is_p, ogp_ref[...], ogs_ref[...])
    y = jax.nn.gelu(jnp.where(is_p, ysp_ref[...], yss_ref[...]))
    y = y * jax.nn.sigmoid(_dot(y.astype(BF16), glu_w_ref[...]) + glu_b_ref[...])
    yn = _rms(y, s5n_ref[...]).astype(BF16)
    mix = _dot(og, wo_ref[0:D_GLA, :]) + _dot(yn, wo_ref[D_GLA:D_GLA + D_S5, :])
    h1 = x + mix
    h1_ref[...] = h1
    hn = _rms(h1, nffn_ref[...])
    hn_ref[...] = _pack_bf16_pair(hn[:, 0:HALF], hn[:, HALF:D_MODEL])
    hn_hi = hn.astype(BF16)
    hn_lo = (hn - hn_hi.astype(F32)).astype(BF16)
    logits = _dot(hn_hi, wr_ref[0]) + _dot(hn_hi, wr_ref[1]) + _dot(hn_lo, wr_ref[0])
    rt_ref[...] = _route(logits)


def _route(logits):
    col = lax.broadcasted_iota(jnp.int32, logits.shape, 1)
    colf = col.astype(F32)
    neg = -jnp.inf

    def first_argmax(vals):
        m = jnp.max(vals, axis=-1, keepdims=True)
        idx = jnp.min(jnp.where(vals == m, colf, float(LANES)), axis=-1, keepdims=True)
        return m, idx

    lg = jnp.where(col < N_EGROUPS, logits, neg)
    gmax, gsel = first_argmax(lg)
    p_g = 1.0 / jnp.sum(jnp.exp(lg - gmax), axis=-1, keepdims=True)
    ecol = col - N_EGROUPS
    egrp = (ecol >> 3).astype(F32)
    in_group = (ecol >= 0) & (ecol < N_EXPERTS) & (egrp == gsel)
    le = jnp.where(in_group, logits, neg)
    m1, i1 = first_argmax(le)
    le2 = jnp.where(colf == i1, neg, le)
    m2, i2 = first_argmax(le2)
    e2 = jnp.exp(m2 - m1)
    den = 1.0 + e2
    w1 = p_g * (1.0 / den)
    w2 = p_g * (e2 / den)
    e1f = i1 - float(N_EGROUPS)
    e2f = i2 - float(N_EGROUPS)
    out = jnp.where(col == 0, e1f, jnp.where(col == 1, e2f, jnp.where(col == 2, w1, jnp.where(col == 3, w2, 0.0))))
    return out


def _mix_out(xp, xs, ogp, ogs, ysp, yss, glu_w, glu_b, s5n, wo, nffn, wr):
    tm = TOK_TILE
    npt, nst = xp.shape[0] // tm, xs.shape[0] // tm
    t = (npt + nst) * tm
    row = lambda w: pl.BlockSpec((tm, w), lambda i: (i, 0))
    prow = lambda w: pl.BlockSpec((tm, w), lambda i: (jnp.minimum(i, npt - 1), 0))
    srow = lambda w: pl.BlockSpec((tm, w), lambda i: (jnp.maximum(i - npt, 0), 0))
    return pl.pallas_call(
        functools.partial(_mix_out_kernel, n_prompt_tiles=npt),
        grid=(npt + nst,),
        in_specs=[prow(D_MODEL), srow(D_MODEL), prow(D_GLA), srow(D_GLA), prow(D_S5), srow(D_S5),
                  _const_spec(glu_w.shape), _const_spec((1, D_S5)), _const_spec((1, D_S5)),
                  _const_spec(wo.shape), _const_spec((1, D_MODEL)), _const_spec(wr.shape)],
        out_specs=[row(D_MODEL), row(HALF), row(LANES)],
        out_shape=[jax.ShapeDtypeStruct((t, D_MODEL), F32), jax.ShapeDtypeStruct((t, HALF), jnp.uint32),
                   jax.ShapeDtypeStruct((t, LANES), F32)],
        compiler_params=pltpu.CompilerParams(dimension_semantics=("arbitrary",), vmem_limit_bytes=VMEM_LIMIT),
        name="mix_out",
    )(xp, xs, ogp, ogs, ysp, yss, glu_w, glu_b, s5n, wo, nffn, wr)


DMA_UNROLL = 8


MOE_SB = 3
HALF = D_MODEL // 2


def _pack_bf16_pair(lo, hi):
    def bits(x):
        b = pltpu.bitcast(x, jnp.uint32)
        return (b + jnp.uint32(0x7FFF) + ((b >> 16) & jnp.uint32(1))) >> 16
    return bits(lo) | (bits(hi) << 16)


def _unpack_bf16_pair(u):
    return (pltpu.bitcast(u << 16, F32), pltpu.bitcast(u & jnp.uint32(0xFFFF0000), F32))


def _moe_kernel(se_ref, sm_ref, sn_ref, sbase_ref, sa_ref, hn_hbm, wg_ref, wu_ref, wd_ref, out_hbm,
                xb_ref, yb_ref, wgb_ref, wub_ref, wdb_ref, gsem, ssem):
    b = pl.program_id(0)
    nb = pl.num_programs(0)
    n = sn_ref[b]
    slot = lax.rem(b, 2)
    t_all = hn_hbm.shape[0]

    def round_up(cnt):
        return (cnt + DMA_UNROLL - 1) // DMA_UNROLL * DMA_UNROLL

    def issue_gather(blk, sl):
        base = sbase_ref[blk]

        def grp(g, c):
            for j in range(DMA_UNROLL):
                i = g * DMA_UNROLL + j
                tok = sa_ref[base + i] >> 1
                pltpu.make_async_copy(hn_hbm.at[pl.ds(tok, 1), :], xb_ref.at[sl, pl.ds(i, 1), :],
                                      gsem.at[sl]).start()
            return c

        lax.fori_loop(0, round_up(sn_ref[blk]) // DMA_UNROLL, grp, 0)

    def wait_gather(cnt, sl):
        rows = pl.ds(0, pl.multiple_of(round_up(cnt), DMA_UNROLL))
        pltpu.make_async_copy(hn_hbm.at[rows, :], xb_ref.at[sl, rows, :], gsem.at[sl]).wait()

    def scatter_row(base, i):
        a = sa_ref[base + i]
        dst = (a & 1) * t_all + (a >> 1)
        pltpu.make_async_copy(yb_ref.at[pl.ds(i, 1), :], out_hbm.at[pl.ds(dst, 1), :], ssem.at[0]).start()

    def issue_scatter(blk, cnt):
        base = sbase_ref[blk]
        full = cnt // DMA_UNROLL

        def grp(g, c):
            for j in range(DMA_UNROLL):
                scatter_row(base, g * DMA_UNROLL + j)
            return c

        def one(i, c):
            scatter_row(base, i)
            return c

        lax.fori_loop(0, full, grp, 0)
        lax.fori_loop(full * DMA_UNROLL, cnt, one, 0)

    def wait_scatter(cnt):
        full = cnt // DMA_UNROLL * DMA_UNROLL

        @pl.when(full > 0)
        def _():
            rows = pl.ds(0, pl.multiple_of(full, DMA_UNROLL))
            pltpu.make_async_copy(yb_ref.at[rows, :], out_hbm.at[rows, :], ssem.at[0]).wait()

        def one(i, c):
            pltpu.make_async_copy(yb_ref.at[pl.ds(i, 1), :], out_hbm.at[pl.ds(0, 1), :], ssem.at[0]).wait()
            return c

        lax.fori_loop(full, cnt, one, 0)

    @pl.when(b == 0)
    def _():
        xb_ref[...] = jnp.zeros_like(xb_ref)
        issue_gather(0, 0)

    @pl.when(b + 1 < nb)
    def _():
        issue_gather(b + 1, 1 - slot)

    n_prev = sn_ref[jnp.maximum(b - 1, 0)]
    has_prev = (b > 0) & (n_prev > 0)

    def compute(rows):
        x_lo, x_hi = _unpack_bf16_pair(xb_ref[slot, 0:rows, :])
        x_lo, x_hi = x_lo.astype(BF16), x_hi.astype(BF16)
        gate = _dot(x_lo, wgb_ref[0:HALF, :]) + _dot(x_hi, wgb_ref[HALF:D_MODEL, :])
        up = _dot(x_lo, wub_ref[0:HALF, :]) + _dot(x_hi, wub_ref[HALF:D_MODEL, :])
        hid = (gate * jax.nn.sigmoid(gate) * up).astype(BF16)

        @pl.when(has_prev)
        def _():
            wait_scatter(n_prev)

        yb_ref[0:rows, :] = _pack_bf16_pair(_dot(hid, wdb_ref[:, 0:HALF]), _dot(hid, wdb_ref[:, HALF:D_MODEL]))

    @pl.when(n > 0)
    def _():
        prev_e = se_ref[jnp.maximum(b - 1, 0)]

        @pl.when((b == 0) | (prev_e != se_ref[b]))
        def _():
            wgb_ref[...] = wg_ref[...].astype(BF16)
            wub_ref[...] = wu_ref[...].astype(BF16)
            wdb_ref[...] = wd_ref[...].astype(BF16)

        wait_gather(n, slot)
        for m in range(1, MOE_SB + 1):
            @pl.when(sm_ref[b] == m)
            def _():
                compute(m * MOE_BLK)

        issue_scatter(b, n)

        @pl.when(b == nb - 1)
        def _():
            wait_scatter(n)

    @pl.when((n == 0) & has_prev)
    def _():
        wait_scatter(n_prev)


def _moe(hn_pk, plan, w_gate, w_up, w_down, n_out_rows):
    sb_e, sb_m, sb_n, sb_base, slot_a = plan
    sb_rows = MOE_SB * MOE_BLK
    wspec = lambda s: pl.BlockSpec((None,) + s, lambda b, se, *_: (se[b], 0, 0))
    grid_spec = pltpu.PrefetchScalarGridSpec(
        num_scalar_prefetch=5,
        grid=(sb_e.shape[0],),
        in_specs=[pl.BlockSpec(memory_space=pl.ANY),
                  wspec((D_MODEL, D_EXPERT)), wspec((D_MODEL, D_EXPERT)), wspec((D_EXPERT, D_MODEL))],
        out_specs=pl.BlockSpec(memory_space=pl.ANY),
        scratch_shapes=[pltpu.VMEM((2, sb_rows, HALF), jnp.uint32), pltpu.VMEM((sb_rows, HALF), jnp.uint32),
                        pltpu.VMEM((D_MODEL, D_EXPERT), BF16), pltpu.VMEM((D_MODEL, D_EXPERT), BF16),
                        pltpu.VMEM((D_EXPERT, D_MODEL), BF16), pltpu.SemaphoreType.DMA((2,)),
                        pltpu.SemaphoreType.DMA((1,))],
    )
    return pl.pallas_call(
        _moe_kernel,
        grid_spec=grid_spec,
        out_shape=jax.ShapeDtypeStruct((n_out_rows, HALF), jnp.uint32),
        compiler_params=pltpu.CompilerParams(dimension_semantics=("arbitrary",),
                                             vmem_limit_bytes=MOE_VMEM_LIMIT),
        name="moe_experts",
    )(sb_e, sb_m, sb_n, sb_base, slot_a, hn_pk, w_gate, w_up, w_down)


def _moe_plan(rt_all):
    t_all = rt_all.shape[0]
    n_assign = 2 * t_all
    e_flat = rt_all[:, 0:2].astype(jnp.int32).reshape(-1)
    onehot = (e_flat[:, None] == jnp.arange(N_EXPERTS, dtype=jnp.int32)[None, :]).astype(jnp.int32)
    csum = jnp.cumsum(onehot, axis=0)
    counts = csum[-1]
    rank = jnp.take_along_axis(csum, e_flat[:, None], axis=1)[:, 0] - 1
    padded = (counts + MOE_BLK - 1) // MOE_BLK * MOE_BLK
    pad_end = jnp.cumsum(padded)
    pad_start = pad_end - padded
    dest = pad_start[e_flat] + rank
    nblk = -(-(n_assign + N_EXPERTS * (MOE_BLK - 1)) // MOE_BLK)
    n_slots = nblk * MOE_BLK
    slot_a = jnp.zeros((n_slots,), jnp.int32).at[dest].set(jnp.arange(n_assign, dtype=jnp.int32))
    k_e = padded // MOE_BLK
    sbc = (k_e + MOE_SB - 1) // MOE_SB
    sb_end = jnp.cumsum(sbc)
    sb_start = sb_end - sbc
    n_sb = (nblk + (MOE_SB - 1) * N_EXPERTS) // MOE_SB
    s = jnp.arange(n_sb, dtype=jnp.int32)
    sb_e = jnp.minimum(jnp.sum((s[:, None] >= sb_end[None, :]).astype(jnp.int32), axis=1), N_EXPERTS - 1)
    j = s - sb_start[sb_e]
    sb_m = jnp.clip(k_e[sb_e] - MOE_SB * j, 0, MOE_SB).astype(jnp.int32)
    sb_n = jnp.clip(counts[sb_e] - MOE_SB * MOE_BLK * j, 0, MOE_SB * MOE_BLK).astype(jnp.int32)
    sb_base = jnp.where(sb_m > 0, pad_start[sb_e] + MOE_SB * MOE_BLK * j, 0).astype(jnp.int32)
    return sb_e.astype(jnp.int32), sb_m, sb_n, sb_base, slot_a


def _ple_out_kernel(h1_ref, y0_ref, y1_ref, rt_ref, p_ref, nple_ref, wpg_ref, wp_ref, nfin_ref, o_ref):
    rt = rt_ref[...]
    w0, w1 = rt[:, 2:3], rt[:, 3:4]
    lo0, hi0 = _unpack_bf16_pair(y0_ref[...])
    lo1, hi1 = _unpack_bf16_pair(y1_ref[...])
    h2 = h1_ref[...] + jnp.concatenate([w0 * lo0 + w1 * lo1, w0 * hi0 + w1 * hi1], axis=1)
    gate = jax.nn.sigmoid(_dot(_rms(h2, nple_ref[...]).astype(BF16), wpg_ref[...]))
    h3 = h2 + _dot(p_ref[...].astype(BF16), wp_ref[...]) * gate
    o_ref[...] = _rms(h3, nfin_ref[...])


def _ple_out(h1, ymoe, rt, p2d, nple, wpg, wp, nfin, row_off, t_all):
    t = p2d.shape[0]
    tm = TOK_TILE
    ob = row_off // tm
    kb = t_all // tm
    row = lambda w: pl.BlockSpec((tm, w), lambda i: (i, 0))
    orow = lambda w: pl.BlockSpec((tm, w), lambda i: (i + ob, 0))
    return pl.pallas_call(
        _ple_out_kernel,
        grid=(t // tm,),
        in_specs=[orow(D_MODEL), orow(HALF),
                  pl.BlockSpec((tm, HALF), lambda i: (i + ob + kb, 0)),
                  orow(LANES), row(D_PLE), _const_spec((1, D_MODEL)), _const_spec(wpg.shape),
                  _const_spec(wp.shape), _const_spec((1, D_MODEL))],
        out_specs=row(D_MODEL),
        out_shape=jax.ShapeDtypeStruct((t, D_MODEL), F32),
        compiler_params=pltpu.CompilerParams(dimension_semantics=("arbitrary",), vmem_limit_bytes=VMEM_LIMIT),
        name="ple_out",
    )(h1, ymoe, ymoe, rt, p2d, nple, wpg, wp, nfin)


def _s5_params(lam_re, lam_im, log_dt, b_re, b_im, c_re, c_im, d_skip):
    dt = jnp.exp(log_dt)[:, None]
    mag = jnp.exp(lam_re * dt)
    ab_re = mag * jnp.cos(lam_im * dt)
    ab_im = mag * jnp.sin(lam_im * dt)
    den = lam_re * lam_re + lam_im * lam_im
    nr = ab_re - 1.0
    f_re = (nr * lam_re + ab_im * lam_im) / den
    f_im = (ab_im * lam_re - nr * lam_im) / den
    bb_re = f_re[..., None] * b_re - f_im[..., None] * b_im
    bb_im = f_re[..., None] * b_im + f_im[..., None] * b_re
    eye = jnp.eye(S5_GB, dtype=F32)

    def bu_w(bb):
        bb = bb.reshape(S5_NGB, S5_GB, S5_STATE, S5_GROUP)
        w = jnp.einsum('nlph,lm->nlhmp', bb, eye)
        return w.reshape(S5_NGB, S5_GB * S5_GROUP, S5_SL)

    def c_w(c):
        c = c.reshape(S5_NGB, S5_GB, S5_GROUP, S5_STATE)
        w = jnp.einsum('nlhp,lm->nlpmh', c, eye)
        return w.reshape(S5_NGB, S5_SL, S5_GB * S5_GROUP)

    wbu = jnp.concatenate([bu_w(bb_re), bu_w(bb_im)], axis=2).astype(BF16)
    wc = jnp.concatenate([c_w(c_re), -c_w(c_im)], axis=1).astype(BF16)
    a_re = ab_re.reshape(S5_NGB, 1, S5_SL)
    a_im = ab_im.reshape(S5_NGB, 1, S5_SL)
    dsk = d_skip.reshape(S5_NGB, 1, S5_GB * S5_GROUP)
    return wbu, wc, a_re, a_im, dsk


def kernel(x_prompt, x_sample, p_prompt, p_sample, state_gla, state_s5_re, state_s5_im, norm_mix, w_in, gla_w_gate_up, gla_gate_bias, gla_norm, s5_lam_re, s5_lam_im, s5_log_dt, s5_b_re, s5_b_im, s5_c_re, s5_c_im, s5_d, s5_glu_w, s5_glu_b, s5_norm, w_out, norm_ffn, router_group, router_expert, w_gate, w_up, w_down, norm_ple, w_ple, w_ple_gate, norm_final):
    depth = w_in.shape[0]
    assert depth == 1
    i = 0
    bp, lp, _ = x_prompt.shape
    bs, ls, _ = x_sample.shape
    tp, ts = bp * lp, bs * ls
    t_all = tp + ts

    n_qkvr = 2 * QK_W + 2 * D_GLA
    wi = w_in[i]
    wm = jnp.concatenate([wi[:, :n_qkvr], wi[:, n_qkvr + GLA_RANK:]], axis=1).astype(BF16)
    wg = jnp.pad(wi[:, n_qkvr:n_qkvr + GLA_RANK], ((0, 0), (0, LANES - GLA_RANK))).astype(BF16)
    wgu = jnp.pad(gla_w_gate_up[i], ((0, LANES - GLA_RANK), (0, 0))).astype(BF16)
    gbias = gla_gate_bias[i].reshape(1, QK_W)
    wbu, wc, a_re, a_im, dsk = _s5_params(s5_lam_re[i], s5_lam_im[i], s5_log_dt[i], s5_b_re[i], s5_b_im[i],
                                          s5_c_re[i], s5_c_im[i], s5_d[i])
    glu_w = s5_glu_w[i].astype(BF16)
    wo = w_out[i].astype(BF16)
    wr32 = jnp.pad(jnp.concatenate([router_group[i], router_expert[i]], axis=1),
                   ((0, 0), (0, LANES - N_EGROUPS - N_EXPERTS)))
    wr_hi = wr32.astype(BF16)
    wr = jnp.stack([wr_hi, (wr32 - wr_hi.astype(F32)).astype(BF16)])
    wpg = w_ple_gate[i].astype(BF16)
    wp = w_ple[i].astype(BF16)
    vec = lambda a: a.reshape(1, -1)

    xp = x_prompt.reshape(tp, D_MODEL)
    xs = x_sample.reshape(ts, D_MODEL)

    qp, kp, vp, rp, lap, up = _in_proj(xp, vec(norm_mix[i]), wm, wg, wgu, gbias, BF16)
    qs, ks, vs, rs, las, us = _in_proj(xs, vec(norm_mix[i]), wm, wg, wgu, gbias, BF16)
    ogp, gla_p = _gla_prompt(qp, kp, vp, rp, lap, vec(gla_norm[i]), bp, lp)
    ogs, gla_s = _gla_sample(qs, ks, vs, rs, las, vec(gla_norm[i]), state_gla[i], bs, ls)
    wbu2, wc2, a_re2, a_im2 = _s5_prompt_params(wbu, wc, a_re, a_im, bp)
    ysp, re_p, im_p = _s5_prompt(up.reshape(bp, lp, D_S5), wbu2, wc2, a_re2, a_im2, dsk, tc=256)
    yss, re_s, im_s = _s5(us.reshape(bs, ls, D_S5), wbu, wc, a_re, a_im, dsk,
                          state_s5_re[i].reshape(bs, -1), state_s5_im[i].reshape(bs, -1),
                          nb=32, tc=ls, has_state=True)

    h1, hn_all, rt_all = _mix_out(xp, xs, ogp, ogs, ysp.reshape(tp, D_S5), yss.reshape(ts, D_S5), glu_w,
                                  vec(s5_glu_b[i]), vec(s5_norm[i]), wo, vec(norm_ffn[i]), wr)

    ymoe = _moe(hn_all, _moe_plan(rt_all), w_gate[i], w_up[i], w_down[i], 2 * t_all)

    po = functools.partial(_ple_out, h1, ymoe, rt_all, nple=vec(norm_ple[i]), wpg=wpg, wp=wp,
                           nfin=vec(norm_final), t_all=t_all)
    y_p = po(p_prompt[i].reshape(tp, D_PLE), row_off=0)
    y_s = po(p_sample[i].reshape(ts, D_PLE), row_off=tp)

    s5shape = lambda a, b: a.reshape(1, b, S5_GROUPS, S5_STATE)
    return (y_p.reshape(bp, lp, D_MODEL), y_s.reshape(bs, ls, D_MODEL),
            gla_p[None], s5shape(re_p, bp), s5shape(im_p, bp),
            gla_s[None], s5shape(re_s, bs), s5shape(im_s, bs))
```

```python
import functools
import math

import jax
import jax.numpy as jnp
from jax import lax
from jax.experimental import pallas as pl
from jax.experimental.pallas import tpu as pltpu

F32 = jnp.float32
BF16 = jnp.bfloat16

D_MODEL = 2048
D_GLA = 1024
D_S5 = 1024
GLA_HEADS = 4
GLA_DV = 256
GLA_DK = 128
GLA_RANK = 16
GLA_CHUNK = 64
S5_GROUP = 16
S5_GROUPS = 64
S5_STATE = 64
N_EGROUPS = 4
N_EPG = 8
N_EXPERTS = 32
D_EXPERT = 512
D_PLE = 256
EPS = 1e-6

LANES = 128
QK_W = GLA_HEADS * GLA_DK
S5_GB = 8
S5_NGB = S5_GROUPS // S5_GB
S5_SL = S5_GB * S5_STATE
TOK_TILE = 256
MOE_BLK = 256
VMEM_LIMIT = 56 * 1024 * 1024
MOE_VMEM_LIMIT = 60 * 1024 * 1024


def _const_spec(shape):
    nd = len(shape)
    return pl.BlockSpec(shape, lambda *_: (0,) * nd, pipeline_mode=pl.Buffered(1))


def _rms(x, g):
    return x * lax.rsqrt(jnp.mean(x * x, axis=-1, keepdims=True) + EPS) * g


def _dot(a, b):
    return jnp.dot(a, b, preferred_element_type=F32)


def _log_sigmoid(x):
    return -(jnp.maximum(-x, 0.0) + jnp.log1p(jnp.exp(-jnp.abs(x))))


def _in_proj_kernel(x_ref, g_ref, wm_ref, wg_ref, wgu_ref, gb_ref,
                    q_ref, k_ref, v_ref, r_ref, la_ref, u_ref):
    hb = _rms(x_ref[...], g_ref[...]).astype(BF16)

    def seg(a, b):
        return _dot(hb, wm_ref[:, a:b])

    q_ref[...] = (seg(0, QK_W) * (GLA_DK ** -0.5)).astype(q_ref.dtype)
    k_ref[...] = seg(QK_W, 2 * QK_W).astype(k_ref.dtype)
    v_ref[...] = seg(2 * QK_W, 2 * QK_W + D_GLA).astype(v_ref.dtype)
    r_ref[...] = seg(2 * QK_W + D_GLA, 2 * QK_W + 2 * D_GLA).astype(r_ref.dtype)
    u_ref[...] = seg(2 * QK_W + 2 * D_GLA, 2 * QK_W + 2 * D_GLA + D_S5)
    zg = _dot(hb, wg_ref[...])
    xg = _dot(zg.astype(BF16), wgu_ref[...]) + gb_ref[...]
    la_ref[...] = _log_sigmoid(xg) * (1.0 / 16.0)


def _in_proj(x2d, g, wm, wg, wgu, gbias, act_dtype):
    t = x2d.shape[0]
    tm = TOK_TILE
    row = lambda w: pl.BlockSpec((tm, w), lambda i: (i, 0))
    return pl.pallas_call(
        _in_proj_kernel,
        grid=(t // tm,),
        in_specs=[row(D_MODEL), _const_spec((1, D_MODEL)), _const_spec(wm.shape), _const_spec(wg.shape),
                  _const_spec(wgu.shape), _const_spec((1, QK_W))],
        out_specs=[row(QK_W), row(QK_W), row(D_GLA), row(D_GLA), row(QK_W), row(D_S5)],
        out_shape=[jax.ShapeDtypeStruct((t, QK_W), act_dtype), jax.ShapeDtypeStruct((t, QK_W), act_dtype),
                   jax.ShapeDtypeStruct((t, D_GLA), act_dtype), jax.ShapeDtypeStruct((t, D_GLA), act_dtype),
                   jax.ShapeDtypeStruct((t, QK_W), F32), jax.ShapeDtypeStruct((t, D_S5), F32)],
        compiler_params=pltpu.CompilerParams(dimension_semantics=("arbitrary",), vmem_limit_bytes=VMEM_LIMIT),
        name="in_proj",
    )(x2d, g, wm, wg, wgu, gbias)


_NT = (((1,), (1,)), ((), ()))
_TN = (((0,), (0,)), ((), ()))


def _gla_pre(q, k, la, c):
    r = q.shape[0]
    shift = int(math.log2(c))
    ri = lax.broadcasted_iota(jnp.int32, (r, r), 0)
    si = lax.broadcasted_iota(jnp.int32, (r, r), 1)
    mask = ((ri >> shift) == (si >> shift)) & (ri >= si)
    tri = jnp.where(mask, 1.0, 0.0).astype(BF16)
    hi = la.astype(BF16)
    r1 = la - hi.astype(F32)
    mid = r1.astype(BF16)
    lo = (r1 - mid.astype(F32)).astype(BF16)
    cum = _dot(tri, hi) + _dot(tri, mid) + _dot(tri, lo)
    last = jnp.concatenate([jnp.broadcast_to(cum[(i + 1) * c - 1:(i + 1) * c, :], (c, cum.shape[1]))
                            for i in range(r // c)], axis=0)
    qe = (q * jnp.exp(cum)).astype(BF16)
    ke = (k * jnp.exp(-cum)).astype(BF16)
    kd = (k * jnp.exp(last - cum)).astype(BF16)
    return qe, ke, kd, cum, mask


def _gla_intra(qe, ke, v, mask):
    sc = lax.dot_general(qe, ke, _NT, preferred_element_type=F32)
    return _dot(jnp.where(mask, sc, 0.0).astype(BF16), v)


def _gla_finish(o, r, g):
    rf = r.astype(F32)
    return _rms(o, g) * (rf * jax.nn.sigmoid(rf))


GLA_SAFE_LOG_DECAY = -60.0


def _gla_token_step(t, rows, q, k, v, a_all, st_ref, oacc_ref):
    m = rows == t
    a = jnp.sum(jnp.where(m, a_all, 0.0), axis=0, keepdims=True)
    kt = jnp.where(m, k, 0.0).astype(BF16)
    qt = jnp.where(m, q, 0.0).astype(BF16)
    vt = jnp.where(m, v, jnp.zeros_like(v))
    st = st_ref[...] * a + lax.dot_general(vt, kt, _TN, preferred_element_type=F32)
    st_ref[...] = st
    oacc_ref[...] += lax.dot_general(qt, st.astype(BF16), _NT, preferred_element_type=F32)


def _gla_prompt_kernel(q_ref, k_ref, v_ref, r_ref, la_ref, g_ref, o_ref, sfin_ref, st_ref, oacc_ref,
                       *, n_chunks):
    j = pl.program_id(1)

    @pl.when(j == 0)
    def _():
        st_ref[...] = jnp.zeros_like(st_ref)

    c = GLA_CHUNK
    qe, ke, kd, cum, mask = _gla_pre(q_ref[...].astype(F32), k_ref[...].astype(F32), la_ref[...], c)
    g = g_ref[...]
    safe = jnp.min(cum) >= GLA_SAFE_LOG_DECAY

    @pl.when(safe)
    def _():
        for h in range(GLA_HEADS):
            kc = slice(h * GLA_DK, (h + 1) * GLA_DK)
            vc = slice(h * GLA_DV, (h + 1) * GLA_DV)
            v = v_ref[:, vc]
            qe_h, kd_h = qe[:, kc], kd[:, kc]
            o = _gla_intra(qe_h, ke[:, kc], v, mask)
            st = st_ref[h]
            inter = []
            for ci in range(n_chunks):
                rows = slice(ci * c, (ci + 1) * c)
                inter.append(lax.dot_general(qe_h[rows], st.astype(BF16), _NT, preferred_element_type=F32))
                dec = jnp.exp(cum[(ci + 1) * c - 1:(ci + 1) * c, kc])
                st = st * dec + lax.dot_general(v[rows], kd_h[rows], _TN, preferred_element_type=F32)
            st_ref[h] = st
            o = o + jnp.concatenate(inter, axis=0)
            o_ref[:, vc] = _gla_finish(o, r_ref[:, vc], g).astype(o_ref.dtype)

    @pl.when(jnp.logical_not(safe))
    def _():
        n_rows = n_chunks * c
        rows = lax.broadcasted_iota(jnp.int32, (n_rows, 1), 0)
        for h in range(GLA_HEADS):
            kc = slice(h * GLA_DK, (h + 1) * GLA_DK)
            vc = slice(h * GLA_DV, (h + 1) * GLA_DV)
            q, k, v = q_ref[:, kc].astype(F32), k_ref[:, kc].astype(F32), v_ref[:, vc]
            a_all = jnp.exp(la_ref[:, kc])
            oacc_ref[...] = jnp.zeros_like(oacc_ref)

            def step(t, carry):
                _gla_token_step(t, rows, q, k, v, a_all, st_ref.at[h], oacc_ref)
                return carry

            lax.fori_loop(0, n_rows, step, 0)
            o_ref[:, vc] = _gla_finish(oacc_ref[...], r_ref[:, vc], g).astype(o_ref.dtype)

    @pl.when(j == pl.num_programs(1) - 1)
    def _():
        for h in range(GLA_HEADS):
            sfin_ref[0, h] = st_ref[h].T


def _gla_prompt(q, k, v, r, la, g, batch, seq):
    rb = 4 * GLA_CHUNK
    nj = seq // rb
    row = lambda w: pl.BlockSpec((rb, w), lambda b, j: (b * nj + j, 0))
    return pl.pallas_call(
        functools.partial(_gla_prompt_kernel, n_chunks=rb // GLA_CHUNK),
        grid=(batch, nj),
        in_specs=[row(QK_W), row(QK_W), row(D_GLA), row(D_GLA), row(QK_W), _const_spec((1, GLA_DV))],
        out_specs=[row(D_GLA),
                   pl.BlockSpec((1, GLA_HEADS, GLA_DK, GLA_DV), lambda b, j: (b, 0, 0, 0))],
        out_shape=[jax.ShapeDtypeStruct((batch * seq, D_GLA), BF16),
                   jax.ShapeDtypeStruct((batch, GLA_HEADS, GLA_DK, GLA_DV), F32)],
        scratch_shapes=[pltpu.VMEM((GLA_HEADS, GLA_DV, GLA_DK), F32), pltpu.VMEM((rb, GLA_DV), F32)],
        compiler_params=pltpu.CompilerParams(dimension_semantics=("arbitrary", "arbitrary"),
                                             vmem_limit_bytes=VMEM_LIMIT),
        name="gla_prompt",
    )(q, k, v, r, la, g)


def _gla_sample_kernel(q_ref, k_ref, v_ref, r_ref, la_ref, g_ref, s0_ref, o_ref, sfin_ref, st_ref, oacc_ref,
                       *, n_seq, seq):
    qe, ke, kd, cum, mask = _gla_pre(q_ref[...].astype(F32), k_ref[...].astype(F32), la_ref[...], seq)
    g = g_ref[...]
    safe = jnp.min(cum) >= GLA_SAFE_LOG_DECAY

    @pl.when(safe)
    def _():
        lasts = jnp.concatenate([cum[(s + 1) * seq - 1:(s + 1) * seq, :] for s in range(n_seq)]
                                + [jnp.zeros((GLA_DK - n_seq, cum.shape[1]), F32)], axis=0)
        pair = 2 * seq
        upper = lax.broadcasted_iota(jnp.int32, (pair, GLA_DK), 0) < seq
        for h in range(GLA_HEADS):
            kc = slice(h * GLA_DK, (h + 1) * GLA_DK)
            vc = slice(h * GLA_DV, (h + 1) * GLA_DV)
            v = v_ref[:, vc]
            qe_h, kd_h = qe[:, kc], kd[:, kc]
            o = _gla_intra(qe_h, ke[:, kc], v, mask)
            dec_t = jnp.exp(lasts[:, kc].T)
            inter = []
            for p in range(n_seq // 2):
                rows = slice(p * pair, (p + 1) * pair)
                qe_p, kd_p, v_p = qe_h[rows], kd_h[rows], v[rows]
                for half in range(2):
                    s = 2 * p + half
                    s0 = s0_ref[s, h]
                    o_s = _dot(qe_p, s0.astype(BF16))
                    inter.append(o_s[half * seq:(half + 1) * seq])
                    kd_s = jnp.where(upper if half == 0 else ~upper, kd_p, jnp.zeros_like(kd_p))
                    dec = jnp.broadcast_to(dec_t[:, s:s + 1], (GLA_DK, GLA_DV))
                    sfin_ref[s, h] = s0 * dec + lax.dot_general(kd_s, v_p, _TN, preferred_element_type=F32)
            o = o + jnp.concatenate(inter, axis=0)
            o_ref[:, vc] = _gla_finish(o, r_ref[:, vc], g).astype(o_ref.dtype)

    @pl.when(jnp.logical_not(safe))
    def _():
        n_rows = n_seq * seq
        rows = lax.broadcasted_iota(jnp.int32, (n_rows, 1), 0)
        for h in range(GLA_HEADS):
            kc = slice(h * GLA_DK, (h + 1) * GLA_DK)
            vc = slice(h * GLA_DV, (h + 1) * GLA_DV)
            q, k, v = q_ref[:, kc].astype(F32), k_ref[:, kc].astype(F32), v_ref[:, vc]
            a_all = jnp.exp(la_ref[:, kc])
            oacc_ref[...] = jnp.zeros_like(oacc_ref)

            def step(t, carry):
                s = t // seq

                @pl.when(t % seq == 0)
                def _():
                    st_ref[...] = s0_ref[s, h].T

                _gla_token_step(t, rows, q, k, v, a_all, st_ref, oacc_ref)

                @pl.when(t % seq == seq - 1)
                def _():
                    sfin_ref[s, h] = st_ref[...].T

                return carry

            lax.fori_loop(0, n_rows, step, 0)
            o_ref[:, vc] = _gla_finish(oacc_ref[...], r_ref[:, vc], g).astype(o_ref.dtype)


def _gla_sample(q, k, v, r, la, g, s0, batch, seq):
    ns = 16
    rb = ns * seq
    row = lambda w: pl.BlockSpec((rb, w), lambda i: (i, 0))
    st = pl.BlockSpec((ns, GLA_HEADS, GLA_DK, GLA_DV), lambda i: (i, 0, 0, 0))
    return pl.pallas_call(
        functools.partial(_gla_sample_kernel, n_seq=ns, seq=seq),
        grid=(batch // ns,),
        in_specs=[row(QK_W), row(QK_W), row(D_GLA), row(D_GLA), row(QK_W), _const_spec((1, GLA_DV)), st],
        out_specs=[row(D_GLA), st],
        out_shape=[jax.ShapeDtypeStruct((batch * seq, D_GLA), BF16),
                   jax.ShapeDtypeStruct((batch, GLA_HEADS, GLA_DK, GLA_DV), F32)],
        scratch_shapes=[pltpu.VMEM((GLA_DV, GLA_DK), F32), pltpu.VMEM((rb, GLA_DV), F32)],
        compiler_params=pltpu.CompilerParams(dimension_semantics=("arbitrary",), vmem_limit_bytes=VMEM_LIMIT),
        name="gla_sample",
    )(q, k, v, r, la, g, s0)


def _s5_kernel(u_ref, wbu_ref, wc_ref, are_ref, aim_ref, d_ref, h0r_ref, h0i_ref,
               y_ref, sre_ref, sim_ref, bu_ref, xs_ref, car_ref, *, nb, tc, has_state):
    j = pl.program_id(2)

    @pl.when(j == 0)
    def _():
        if has_state:
            car_ref[0] = h0r_ref[...]
            car_ref[1] = h0i_ref[...]
        else:
            car_ref[...] = jnp.zeros_like(car_ref)

    u2 = u_ref[...].reshape(nb * tc, LANES)
    ub = u2.astype(BF16)
    nl = S5_SL // LANES
    for l in range(2 * nl):
        bu_ref[l] = _dot(ub, wbu_ref[0, :, l * LANES:(l + 1) * LANES])
    a_r = [jnp.broadcast_to(are_ref[0, :, l * LANES:(l + 1) * LANES], (nb, LANES)) for l in range(nl)]
    a_i = [jnp.broadcast_to(aim_ref[0, :, l * LANES:(l + 1) * LANES], (nb, LANES)) for l in range(nl)]

    def step(t, carry):
        rows = pl.ds(t, nb, stride=tc)
        out = []
        for l in range(nl):
            xr, xi = carry[2 * l], carry[2 * l + 1]
            nr = a_r[l] * xr - a_i[l] * xi + bu_ref[l, rows, :]
            ni = a_r[l] * xi + a_i[l] * xr + bu_ref[nl + l, rows, :]
            xs_ref[l, rows, :] = nr
            xs_ref[nl + l, rows, :] = ni
            out += [nr, ni]
        return tuple(out)

    init = []
    for l in range(nl):
        init += [car_ref[0, :, l * LANES:(l + 1) * LANES], car_ref[1, :, l * LANES:(l + 1) * LANES]]
    fin = lax.fori_loop(0, tc, step, tuple(init), unroll=8)
    xr = jnp.concatenate([fin[2 * l] for l in range(nl)], axis=1)
    xi = jnp.concatenate([fin[2 * l + 1] for l in range(nl)], axis=1)
    car_ref[0] = xr
    car_ref[1] = xi
    y = d_ref[0] * u2
    for l in range(2 * nl):
        y = y + _dot(xs_ref[l].astype(BF16), wc_ref[0, l * LANES:(l + 1) * LANES, :])
    y_ref[...] = y.reshape(nb, tc, LANES)

    @pl.when(j == pl.num_programs(2) - 1)
    def _():
        sre_ref[...] = xr
        sim_ref[...] = xi


def _s5(u3d, wbu, wc, a_re, a_im, dsk, h0r, h0i, nb, tc, has_state):
    batch, seq, _ = u3d.shape
    grid = (S5_NGB, batch // nb, seq // tc)
    st = pl.BlockSpec((nb, S5_SL), lambda g, b, j: (b, g))
    par = lambda w: pl.BlockSpec((1, 1, w), lambda g, b, j: (g, 0, 0))
    ublk = pl.BlockSpec((nb, tc, LANES), lambda g, b, j: (b, j, g))
    return pl.pallas_call(
        functools.partial(_s5_kernel, nb=nb, tc=tc, has_state=has_state),
        grid=grid,
        in_specs=[ublk,
                  pl.BlockSpec((1, LANES, 2 * S5_SL), lambda g, b, j: (g, 0, 0)),
                  pl.BlockSpec((1, 2 * S5_SL, LANES), lambda g, b, j: (g, 0, 0)),
                  par(S5_SL), par(S5_SL), par(LANES), st, st],
        out_specs=[ublk, st, st],
        out_shape=[jax.ShapeDtypeStruct(u3d.shape, F32),
                   jax.ShapeDtypeStruct((batch, S5_GROUPS * S5_STATE), F32),
                   jax.ShapeDtypeStruct((batch, S5_GROUPS * S5_STATE), F32)],
        scratch_shapes=[pltpu.VMEM((2 * S5_SL // LANES, nb * tc, LANES), F32),
                        pltpu.VMEM((2 * S5_SL // LANES, nb * tc, LANES), F32),
                        pltpu.VMEM((2, nb, S5_SL), F32)],
        compiler_params=pltpu.CompilerParams(dimension_semantics=("arbitrary",) * 3,
                                             vmem_limit_bytes=VMEM_LIMIT),
        name="s5_state" if has_state else "s5_zero",
    )(u3d, wbu, wc, a_re, a_im, dsk, h0r, h0i)


S5_J = 2
S5_HL = S5_SL // S5_J


def _s5_prompt_kernel(u_ref, wbu_ref, wc_ref, are_ref, aim_ref, d_ref, y_ref, sre_ref, sim_ref,
                      u2_ref, lhs_ref, bu_ref, xs_ref, y2_ref, yo_ref, car_ref, *, nb, tc):
    g = pl.program_id(0)
    j = pl.program_id(1)
    rows = nb * S5_J

    @pl.when((g == 0) & (j == 0))
    def _():
        lhs_ref[...] = jnp.zeros_like(lhs_ref)

    @pl.when(j == 0)
    def _():
        car_ref[...] = jnp.zeros_like(car_ref)

    u2 = u_ref[...].reshape(nb * tc, LANES)
    u2_ref[...] = u2

    def build(t, c):
        u4 = u2_ref[pl.ds(t, nb, stride=tc), :]
        for jj in range(S5_J):
            lhs_ref[t, jj * nb:(jj + 1) * nb, jj * LANES:(jj + 1) * LANES] = u4
        return c

    lax.fori_loop(0, tc, build, 0, unroll=8)
    lhs = lhs_ref[...].reshape(tc * rows, S5_J * LANES).astype(BF16)
    bu_ref[...] = _dot(lhs, wbu_ref[0]).reshape(tc, rows, 2 * S5_HL)
    ar = are_ref[0]
    ai = aim_ref[0]

    def step(t, carry):
        xr, xi = carry
        tile = bu_ref[t]
        nr = ar * xr - ai * xi + tile[:, 0:S5_HL]
        ni = ar * xi + ai * xr + tile[:, S5_HL:2 * S5_HL]
        xs_ref[t] = jnp.concatenate([nr, ni], axis=1)
        return nr, ni

    xr, xi = lax.fori_loop(0, tc, step, (car_ref[0], car_ref[1]), unroll=8)
    car_ref[0] = xr
    car_ref[1] = xi
    xs = xs_ref[...].reshape(tc * rows, 2 * S5_HL).astype(BF16)
    y2_ref[...] = _dot(xs, wc_ref[0]).reshape(tc, rows, S5_J * LANES)
    first_half = lax.broadcasted_iota(jnp.int32, (rows, LANES), 0) < nb

    def unperm(t, c):
        t2 = y2_ref[t]
        part = jnp.where(first_half, t2[:, 0:LANES], t2[:, LANES:2 * LANES])
        tot = part + pltpu.roll(part, nb, axis=0)
        yo_ref[pl.ds(t, nb, stride=tc), :] = tot[0:nb]
        return c

    lax.fori_loop(0, tc, unperm, 0, unroll=8)
    y_ref[...] = (yo_ref[...] + d_ref[0] * u2).reshape(nb, tc, LANES)

    @pl.when(j == pl.num_programs(1) - 1)
    def _():
        sre_ref[...] = jnp.concatenate([xr[jj * nb:(jj + 1) * nb] for jj in range(S5_J)], axis=1)
        sim_ref[...] = jnp.concatenate([xi[jj * nb:(jj + 1) * nb] for jj in range(S5_J)], axis=1)


def _s5_prompt(u3d, wbu2, wc2, a_re2, a_im2, dsk, tc):
    nb, seq, _ = u3d.shape
    rows = nb * S5_J
    assert rows == 8
    st = pl.BlockSpec((nb, S5_SL), lambda g, j: (0, g))
    gblk = lambda s: pl.BlockSpec((1,) + s, lambda g, j: (g, 0, 0))
    ublk = pl.BlockSpec((nb, tc, LANES), lambda g, j: (0, j, g))
    return pl.pallas_call(
        functools.partial(_s5_prompt_kernel, nb=nb, tc=tc),
        grid=(S5_NGB, seq // tc),
        in_specs=[ublk, gblk((S5_J * LANES, 2 * S5_HL)), gblk((2 * S5_HL, S5_J * LANES)),
                  gblk((rows, S5_HL)), gblk((rows, S5_HL)), gblk((1, LANES))],
        out_specs=[ublk, st, st],
        out_shape=[jax.ShapeDtypeStruct(u3d.shape, F32),
                   jax.ShapeDtypeStruct((nb, S5_GROUPS * S5_STATE), F32),
                   jax.ShapeDtypeStruct((nb, S5_GROUPS * S5_STATE), F32)],
        scratch_shapes=[pltpu.VMEM((nb * tc, LANES), F32),
                        pltpu.VMEM((tc, rows, S5_J * LANES), F32),
                        pltpu.VMEM((tc, rows, 2 * S5_HL), F32),
                        pltpu.VMEM((tc, rows, 2 * S5_HL), F32),
                        pltpu.VMEM((tc, rows, S5_J * LANES), F32),
                        pltpu.VMEM((nb * tc, LANES), F32),
                        pltpu.VMEM((2, rows, S5_HL), F32)],
        compiler_params=pltpu.CompilerParams(dimension_semantics=("arbitrary",) * 2,
                                             vmem_limit_bytes=VMEM_LIMIT),
        name="s5_prompt",
    )(u3d, wbu2, wc2, a_re2, a_im2, dsk)


def _s5_prompt_params(wbu, wc, a_re, a_im, nb):
    h = S5_HL
    top = jnp.concatenate([wbu[:, :, 0:h], wbu[:, :, S5_SL:S5_SL + h]], axis=2)
    bot = jnp.concatenate([wbu[:, :, h:2 * h], wbu[:, :, S5_SL + h:S5_SL + 2 * h]], axis=2)
    wbu2 = jnp.concatenate([top, bot], axis=1)
    wc_j = [jnp.concatenate([wc[:, jj * h:(jj + 1) * h, :], wc[:, S5_SL + jj * h:S5_SL + (jj + 1) * h, :]],
                            axis=1) for jj in range(S5_J)]
    wc2 = jnp.concatenate(wc_j, axis=2)
    tile = lambda a: jnp.repeat(a.reshape(S5_NGB, S5_J, h), nb, axis=1)
    return wbu2, wc2, tile(a_re), tile(a_im)


def _mix_out_kernel(xp_ref, xs_ref, ogp_ref, ogs_ref, ysp_ref, yss_ref, glu_w_ref, glu_b_ref, s5n_ref, wo_ref,
                    nffn_ref, wr_ref, h1_ref, hn_ref, rt_ref, *, n_prompt_tiles):
    is_p = pl.program_id(0) < n_prompt_tiles
    x = jnp.where(is_p, xp_ref[...], xs_ref[...])
    og = jnp.where(is_p, ogp_ref[...], ogs_ref[...])
    y = jax.nn.gelu(jnp.where(is_p, ysp_ref[...], yss_ref[...]))
    y = y * jax.nn.sigmoid(_dot(y.astype(BF16), glu_w_ref[...]) + glu_b_ref[...])
    yn = _rms(y, s5n_ref[...]).astype(BF16)
    mix = _dot(og, wo_ref[0:D_GLA, :]) + _dot(yn, wo_ref[D_GLA:D_GLA + D_S5, :])
    h1 = x + mix
    h1_ref[...] = h1
    hn = _rms(h1, nffn_ref[...])
    hn_ref[...] = _pack_bf16_pair(hn[:, 0:HALF], hn[:, HALF:D_MODEL])
    hn_hi = hn.astype(BF16)
    hn_lo = (hn - hn_hi.astype(F32)).astype(BF16)
    logits = _dot(hn_hi, wr_ref[0]) + _dot(hn_hi, wr_ref[1]) + _dot(hn_lo, wr_ref[0])
    rt_ref[...] = _route(logits)


def _route(logits):
    col = lax.broadcasted_iota(jnp.int32, logits.shape, 1)
    colf = col.astype(F32)
    neg = -jnp.inf

    def first_argmax(vals):
        m = jnp.max(vals, axis=-1, keepdims=True)
        idx = jnp.min(jnp.where(vals == m, colf, float(LANES)), axis=-1, keepdims=True)
        return m, idx

    lg = jnp.where(col < N_EGROUPS, logits, neg)
    gmax, gsel = first_argmax(lg)
    p_g = 1.0 / jnp.sum(jnp.exp(lg - gmax), axis=-1, keepdims=True)
    ecol = col - N_EGROUPS
    egrp = (ecol >> 3).astype(F32)
    in_group = (ecol >= 0) & (ecol < N_EXPERTS) & (egrp == gsel)
    le = jnp.where(in_group, logits, neg)
    m1, i1 = first_argmax(le)
    le2 = jnp.where(colf == i1, neg, le)
    m2, i2 = first_argmax(le2)
    e2 = jnp.exp(m2 - m1)
    den = 1.0 + e2
    w1 = p_g * (1.0 / den)
    w2 = p_g * (e2 / den)
    e1f = i1 - float(N_EGROUPS)
    e2f = i2 - float(N_EGROUPS)
    out = jnp.where(col == 0, e1f, jnp.where(col == 1, e2f, jnp.where(col == 2, w1, jnp.where(col == 3, w2, 0.0))))
    return out


def _mix_out(xp, xs, ogp, ogs, ysp, yss, glu_w, glu_b, s5n, wo, nffn, wr):
    tm = TOK_TILE
    npt, nst = xp.shape[0] // tm, xs.shape[0] // tm
    t = (npt + nst) * tm
    row = lambda w: pl.BlockSpec((tm, w), lambda i: (i, 0))
    prow = lambda w: pl.BlockSpec((tm, w), lambda i: (jnp.minimum(i, npt - 1), 0))
    srow = lambda w: pl.BlockSpec((tm, w), lambda i: (jnp.maximum(i - npt, 0), 0))
    return pl.pallas_call(
        functools.partial(_mix_out_kernel, n_prompt_tiles=npt),
        grid=(npt + nst,),
        in_specs=[prow(D_MODEL), srow(D_MODEL), prow(D_GLA), srow(D_GLA), prow(D_S5), srow(D_S5),
                  _const_spec(glu_w.shape), _const_spec((1, D_S5)), _const_spec((1, D_S5)),
                  _const_spec(wo.shape), _const_spec((1, D_MODEL)), _const_spec(wr.shape)],
        out_specs=[row(D_MODEL), row(HALF), row(LANES)],
        out_shape=[jax.ShapeDtypeStruct((t, D_MODEL), F32), jax.ShapeDtypeStruct((t, HALF), jnp.uint32),
                   jax.ShapeDtypeStruct((t, LANES), F32)],
        compiler_params=pltpu.CompilerParams(dimension_semantics=("arbitrary",), vmem_limit_bytes=VMEM_LIMIT),
        name="mix_out",
    )(xp, xs, ogp, ogs, ysp, yss, glu_w, glu_b, s5n, wo, nffn, wr)


DMA_UNROLL = 8


MOE_SB = 3
HALF = D_MODEL // 2


def _pack_bf16_pair(lo, hi):
    def bits(x):
        b = pltpu.bitcast(x, jnp.uint32)
        return (b + jnp.uint32(0x7FFF) + ((b >> 16) & jnp.uint32(1))) >> 16
    return bits(lo) | (bits(hi) << 16)


def _unpack_bf16_pair(u):
    return (pltpu.bitcast(u << 16, F32), pltpu.bitcast(u & jnp.uint32(0xFFFF0000), F32))


def _moe_kernel(se_ref, sm_ref, sn_ref, sbase_ref, sa_ref, hn_hbm, wg_ref, wu_ref, wd_ref, out_hbm,
                xb_ref, yb_ref, wgb_ref, wub_ref, wdb_ref, gsem, ssem):
    b = pl.program_id(0)
    nb = pl.num_programs(0)
    n = sn_ref[b]
    slot = lax.rem(b, 2)
    t_all = hn_hbm.shape[0]

    def round_up(cnt):
        return (cnt + DMA_UNROLL - 1) // DMA_UNROLL * DMA_UNROLL

    def issue_gather(blk, sl):
        base = sbase_ref[blk]

        def grp(g, c):
            for j in range(DMA_UNROLL):
                i = g * DMA_UNROLL + j
                tok = sa_ref[base + i] >> 1
                pltpu.make_async_copy(hn_hbm.at[pl.ds(tok, 1), :], xb_ref.at[sl, pl.ds(i, 1), :],
                                      gsem.at[sl]).start()
            return c

        lax.fori_loop(0, round_up(sn_ref[blk]) // DMA_UNROLL, grp, 0)

    def wait_gather(cnt, sl):
        rows = pl.ds(0, pl.multiple_of(round_up(cnt), DMA_UNROLL))
        pltpu.make_async_copy(hn_hbm.at[rows, :], xb_ref.at[sl, rows, :], gsem.at[sl]).wait()

    def scatter_row(base, i):
        a = sa_ref[base + i]
        dst = (a & 1) * t_all + (a >> 1)
        pltpu.make_async_copy(yb_ref.at[pl.ds(i, 1), :], out_hbm.at[pl.ds(dst, 1), :], ssem.at[0]).start()

    def issue_scatter(blk, cnt):
        base = sbase_ref[blk]
        full = cnt // DMA_UNROLL

        def grp(g, c):
            for j in range(DMA_UNROLL):
                scatter_row(base, g * DMA_UNROLL + j)
            return c

        def one(i, c):
            scatter_row(base, i)
            return c

        lax.fori_loop(0, full, grp, 0)
        lax.fori_loop(full * DMA_UNROLL, cnt, one, 0)

    def wait_scatter(cnt):
        full = cnt // DMA_UNROLL * DMA_UNROLL

        @pl.when(full > 0)
        def _():
            rows = pl.ds(0, pl.multiple_of(full, DMA_UNROLL))
            pltpu.make_async_copy(yb_ref.at[rows, :], out_hbm.at[rows, :], ssem.at[0]).wait()

        def one(i, c):
            pltpu.make_async_copy(yb_ref.at[pl.ds(i, 1), :], out_hbm.at[pl.ds(0, 1), :], ssem.at[0]).wait()
            return c

        lax.fori_loop(full, cnt, one, 0)

    @pl.when(b == 0)
    def _():
        xb_ref[...] = jnp.zeros_like(xb_ref)
        issue_gather(0, 0)

    @pl.when(b + 1 < nb)
    def _():
        issue_gather(b + 1, 1 - slot)

    n_prev = sn_ref[jnp.maximum(b - 1, 0)]
    has_prev = (b > 0) & (n_prev > 0)

    def compute(rows):
        x_lo, x_hi = _unpack_bf16_pair(xb_ref[slot, 0:rows, :])
        x_lo, x_hi = x_lo.astype(BF16), x_hi.astype(BF16)
        gate = _dot(x_lo, wgb_ref[0:HALF, :]) + _dot(x_hi, wgb_ref[HALF:D_MODEL, :])
        up = _dot(x_lo, wub_ref[0:HALF, :]) + _dot(x_hi, wub_ref[HALF:D_MODEL, :])
        hid = (gate * jax.nn.sigmoid(gate) * up).astype(BF16)

        @pl.when(has_prev)
        def _():
            wait_scatter(n_prev)

        yb_ref[0:rows, :] = _pack_bf16_pair(_dot(hid, wdb_ref[:, 0:HALF]), _dot(hid, wdb_ref[:, HALF:D_MODEL]))

    @pl.when(n > 0)
    def _():
        prev_e = se_ref[jnp.maximum(b - 1, 0)]

        @pl.when((b == 0) | (prev_e != se_ref[b]))
        def _():
            wgb_ref[...] = wg_ref[...].astype(BF16)
            wub_ref[...] = wu_ref[...].astype(BF16)
            wdb_ref[...] = wd_ref[...].astype(BF16)

        wait_gather(n, slot)
        for m in range(1, MOE_SB + 1):
            @pl.when(sm_ref[b] == m)
            def _():
                compute(m * MOE_BLK)

        issue_scatter(b, n)

        @pl.when(b == nb - 1)
        def _():
            wait_scatter(n)

    @pl.when((n == 0) & has_prev)
    def _():
        wait_scatter(n_prev)


def _moe(hn_pk, plan, w_gate, w_up, w_down, n_out_rows):
    sb_e, sb_m, sb_n, sb_base, slot_a = plan
    sb_rows = MOE_SB * MOE_BLK
    wspec = lambda s: pl.BlockSpec((None,) + s, lambda b, se, *_: (se[b], 0, 0))
    grid_spec = pltpu.PrefetchScalarGridSpec(
        num_scalar_prefetch=5,
        grid=(sb_e.shape[0],),
        in_specs=[pl.BlockSpec(memory_space=pl.ANY),
                  wspec((D_MODEL, D_EXPERT)), wspec((D_MODEL, D_EXPERT)), wspec((D_EXPERT, D_MODEL))],
        out_specs=pl.BlockSpec(memory_space=pl.ANY),
        scratch_shapes=[pltpu.VMEM((2, sb_rows, HALF), jnp.uint32), pltpu.VMEM((sb_rows, HALF), jnp.uint32),
                        pltpu.VMEM((D_MODEL, D_EXPERT), BF16), pltpu.VMEM((D_MODEL, D_EXPERT), BF16),
                        pltpu.VMEM((D_EXPERT, D_MODEL), BF16), pltpu.SemaphoreType.DMA((2,)),
                        pltpu.SemaphoreType.DMA((1,))],
    )
    return pl.pallas_call(
        _moe_kernel,
        grid_spec=grid_spec,
        out_shape=jax.ShapeDtypeStruct((n_out_rows, HALF), jnp.uint32),
        compiler_params=pltpu.CompilerParams(dimension_semantics=("arbitrary",),
                                             vmem_limit_bytes=MOE_VMEM_LIMIT),
        name="moe_experts",
    )(sb_e, sb_m, sb_n, sb_base, slot_a, hn_pk, w_gate, w_up, w_down)


def _moe_plan(rt_all):
    t_all = rt_all.shape[0]
    n_assign = 2 * t_all
    e_flat = rt_all[:, 0:2].astype(jnp.int32).reshape(-1)
    onehot = (e_flat[:, None] == jnp.arange(N_EXPERTS, dtype=jnp.int32)[None, :]).astype(jnp.int32)
    csum = jnp.cumsum(onehot, axis=0)
    counts = csum[-1]
    rank = jnp.take_along_axis(csum, e_flat[:, None], axis=1)[:, 0] - 1
    padded = (counts + MOE_BLK - 1) // MOE_BLK * MOE_BLK
    pad_end = jnp.cumsum(padded)
    pad_start = pad_end - padded
    dest = pad_start[e_flat] + rank
    nblk = -(-(n_assign + N_EXPERTS * (MOE_BLK - 1)) // MOE_BLK)
    n_slots = nblk * MOE_BLK
    slot_a = jnp.zeros((n_slots,), jnp.int32).at[dest].set(jnp.arange(n_assign, dtype=jnp.int32))
    k_e = padded // MOE_BLK
    sbc = (k_e + MOE_SB - 1) // MOE_SB
    sb_end = jnp.cumsum(sbc)
    sb_start = sb_end - sbc
    n_sb = (nblk + (MOE_SB - 1) * N_EXPERTS) // MOE_SB
    s = jnp.arange(n_sb, dtype=jnp.int32)
    sb_e = jnp.minimum(jnp.sum((s[:, None] >= sb_end[None, :]).astype(jnp.int32), axis=1), N_EXPERTS - 1)
    j = s - sb_start[sb_e]
    sb_m = jnp.clip(k_e[sb_e] - MOE_SB * j, 0, MOE_SB).astype(jnp.int32)
    sb_n = jnp.clip(counts[sb_e] - MOE_SB * MOE_BLK * j, 0, MOE_SB * MOE_BLK).astype(jnp.int32)
    sb_base = jnp.where(sb_m > 0, pad_start[sb_e] + MOE_SB * MOE_BLK * j, 0).astype(jnp.int32)
    return sb_e.astype(jnp.int32), sb_m, sb_n, sb_base, slot_a


def _ple_out_kernel(h1_ref, y0_ref, y1_ref, rt_ref, p_ref, nple_ref, wpg_ref, wp_ref, nfin_ref, o_ref):
    rt = rt_ref[...]
    w0, w1 = rt[:, 2:3], rt[:, 3:4]
    lo0, hi0 = _unpack_bf16_pair(y0_ref[...])
    lo1, hi1 = _unpack_bf16_pair(y1_ref[...])
    h2 = h1_ref[...] + jnp.concatenate([w0 * lo0 + w1 * lo1, w0 * hi0 + w1 * hi1], axis=1)
    gate = jax.nn.sigmoid(_dot(_rms(h2, nple_ref[...]).astype(BF16), wpg_ref[...]))
    h3 = h2 + _dot(p_ref[...].astype(BF16), wp_ref[...]) * gate
    o_ref[...] = _rms(h3, nfin_ref[...])


def _ple_out(h1, ymoe, rt, p2d, nple, wpg, wp, nfin, row_off, t_all):
    t = p2d.shape[0]
    tm = TOK_TILE
    ob = row_off // tm
    kb = t_all // tm
    row = lambda w: pl.BlockSpec((tm, w), lambda i: (i, 0))
    orow = lambda w: pl.BlockSpec((tm, w), lambda i: (i + ob, 0))
    return pl.pallas_call(
        _ple_out_kernel,
        grid=(t // tm,),
        in_specs=[orow(D_MODEL), orow(HALF),
                  pl.BlockSpec((tm, HALF), lambda i: (i + ob + kb, 0)),
                  orow(LANES), row(D_PLE), _const_spec((1, D_MODEL)), _const_spec(wpg.shape),
                  _const_spec(wp.shape), _const_spec((1, D_MODEL))],
        out_specs=row(D_MODEL),
        out_shape=jax.ShapeDtypeStruct((t, D_MODEL), F32),
        compiler_params=pltpu.CompilerParams(dimension_semantics=("arbitrary",), vmem_limit_bytes=VMEM_LIMIT),
        name="ple_out",
    )(h1, ymoe, ymoe, rt, p2d, nple, wpg, wp, nfin)


def _s5_params(lam_re, lam_im, log_dt, b_re, b_im, c_re, c_im, d_skip):
    dt = jnp.exp(log_dt)[:, None]
    mag = jnp.exp(lam_re * dt)
    ab_re = mag * jnp.cos(lam_im * dt)
    ab_im = mag * jnp.sin(lam_im * dt)
    den = lam_re * lam_re + lam_im * lam_im
    nr = ab_re - 1.0
    f_re = (nr * lam_re + ab_im * lam_im) / den
    f_im = (ab_im * lam_re - nr * lam_im) / den
    bb_re = f_re[..., None] * b_re - f_im[..., None] * b_im
    bb_im = f_re[..., None] * b_im + f_im[..., None] * b_re
    eye = jnp.eye(S5_GB, dtype=F32)

    def bu_w(bb):
        bb = bb.reshape(S5_NGB, S5_GB, S5_STATE, S5_GROUP)
        w = jnp.einsum('nlph,lm->nlhmp', bb, eye)
        return w.reshape(S5_NGB, S5_GB * S5_GROUP, S5_SL)

    def c_w(c):
        c = c.reshape(S5_NGB, S5_GB, S5_GROUP, S5_STATE)
        w = jnp.einsum('nlhp,lm->nlpmh', c, eye)
        return w.reshape(S5_NGB, S5_SL, S5_GB * S5_GROUP)

    wbu = jnp.concatenate([bu_w(bb_re), bu_w(bb_im)], axis=2).astype(BF16)
    wc = jnp.concatenate([c_w(c_re), -c_w(c_im)], axis=1).astype(BF16)
    a_re = ab_re.reshape(S5_NGB, 1, S5_SL)
    a_im = ab_im.reshape(S5_NGB, 1, S5_SL)
    dsk = d_skip.reshape(S5_NGB, 1, S5_GB * S5_GROUP)
    return wbu, wc, a_re, a_im, dsk


def kernel(x_prompt, x_sample, p_prompt, p_sample, state_gla, state_s5_re, state_s5_im, norm_mix, w_in, gla_w_gate_up, gla_gate_bias, gla_norm, s5_lam_re, s5_lam_im, s5_log_dt, s5_b_re, s5_b_im, s5_c_re, s5_c_im, s5_d, s5_glu_w, s5_glu_b, s5_norm, w_out, norm_ffn, router_group, router_expert, w_gate, w_up, w_down, norm_ple, w_ple, w_ple_gate, norm_final):
    depth = w_in.shape[0]
    assert depth == 1
    i = 0
    bp, lp, _ = x_prompt.shape
    bs, ls, _ = x_sample.shape
    tp, ts = bp * lp, bs * ls
    t_all = tp + ts

    n_qkvr = 2 * QK_W + 2 * D_GLA
    wi = w_in[i]
    wm = jnp.concatenate([wi[:, :n_qkvr], wi[:, n_qkvr + GLA_RANK:]], axis=1).astype(BF16)
    wg = jnp.pad(wi[:, n_qkvr:n_qkvr + GLA_RANK], ((0, 0), (0, LANES - GLA_RANK))).astype(BF16)
    wgu = jnp.pad(gla_w_gate_up[i], ((0, LANES - GLA_RANK), (0, 0))).astype(BF16)
    gbias = gla_gate_bias[i].reshape(1, QK_W)
    wbu, wc, a_re, a_im, dsk = _s5_params(s5_lam_re[i], s5_lam_im[i], s5_log_dt[i], s5_b_re[i], s5_b_im[i],
                                          s5_c_re[i], s5_c_im[i], s5_d[i])
    glu_w = s5_glu_w[i].astype(BF16)
    wo = w_out[i].astype(BF16)
    wr32 = jnp.pad(jnp.concatenate([router_group[i], router_expert[i]], axis=1),
                   ((0, 0), (0, LANES - N_EGROUPS - N_EXPERTS)))
    wr_hi = wr32.astype(BF16)
    wr = jnp.stack([wr_hi, (wr32 - wr_hi.astype(F32)).astype(BF16)])
    wpg = w_ple_gate[i].astype(BF16)
    wp = w_ple[i].astype(BF16)
    vec = lambda a: a.reshape(1, -1)

    xp = x_prompt.reshape(tp, D_MODEL)
    xs = x_sample.reshape(ts, D_MODEL)

    qp, kp, vp, rp, lap, up = _in_proj(xp, vec(norm_mix[i]), wm, wg, wgu, gbias, BF16)
    qs, ks, vs, rs, las, us = _in_proj(xs, vec(norm_mix[i]), wm, wg, wgu, gbias, BF16)
    ogp, gla_p = _gla_prompt(qp, kp, vp, rp, lap, vec(gla_norm[i]), bp, lp)
    ogs, gla_s = _gla_sample(qs, ks, vs, rs, las, vec(gla_norm[i]), state_gla[i], bs, ls)
    wbu2, wc2, a_re2, a_im2 = _s5_prompt_params(wbu, wc, a_re, a_im, bp)
    ysp, re_p, im_p = _s5_prompt(up.reshape(bp, lp, D_S5), wbu2, wc2, a_re2, a_im2, dsk, tc=256)
    yss, re_s, im_s = _s5(us.reshape(bs, ls, D_S5), wbu, wc, a_re, a_im, dsk,
                          state_s5_re[i].reshape(bs, -1), state_s5_im[i].reshape(bs, -1),
                          nb=32, tc=ls, has_state=True)

    h1, hn_all, rt_all = _mix_out(xp, xs, ogp, ogs, ysp.reshape(tp, D_S5), yss.reshape(ts, D_S5), glu_w,
                                  vec(s5_glu_b[i]), vec(s5_norm[i]), wo, vec(norm_ffn[i]), wr)

    ymoe = _moe(hn_all, _moe_plan(rt_all), w_gate[i], w_up[i], w_down[i], 2 * t_all)

    po = functools.partial(_ple_out, h1, ymoe, rt_all, nple=vec(norm_ple[i]), wpg=wpg, wp=wp,
                           nfin=vec(norm_final), t_all=t_all)
    y_p = po(p_prompt[i].reshape(tp, D_PLE), row_off=0)
    y_s = po(p_sample[i].reshape(ts, D_PLE), row_off=tp)

    s5shape = lambda a, b: a.reshape(1, b, S5_GROUPS, S5_STATE)
    return (y_p.reshape(bp, lp, D_MODEL), y_s.reshape(bs, ls, D_MODEL),
            gla_p[None], s5shape(re_p, bp), s5shape(im_p, bp),
            gla_s[None], s5shape(re_s, bs), s5shape(im_s, bs))
```

```python
import functools
import math

import jax
import jax.numpy as jnp
from jax import lax
from jax.experimental import pallas as pl
from jax.experimental.pallas import tpu as pltpu

F32 = jnp.float32
BF16 = jnp.bfloat16

D_MODEL = 2048
D_GLA = 1024
D_S5 = 1024
GLA_HEADS = 4
GLA_DV = 256
GLA_DK = 128
GLA_RANK = 16
GLA_CHUNK = 64
S5_GROUP = 16
S5_GROUPS = 64
S5_STATE = 64
N_EGROUPS = 4
N_EPG = 8
N_EXPERTS = 32
D_EXPERT = 512
D_PLE = 256
EPS = 1e-6

LANES = 128
QK_W = GLA_HEADS * GLA_DK
S5_GB = 8
S5_NGB = S5_GROUPS // S5_GB
S5_SL = S5_GB * S5_STATE
TOK_TILE = 256
MOE_BLK = 256
VMEM_LIMIT = 56 * 1024 * 1024
MOE_VMEM_LIMIT = 60 * 1024 * 1024


def _const_spec(shape):
    nd = len(shape)
    return pl.BlockSpec(shape, lambda *_: (0,) * nd, pipeline_mode=pl.Buffered(1))


def _rms(x, g):
    return x * lax.rsqrt(jnp.mean(x * x, axis=-1, keepdims=True) + EPS) * g


def _dot(a, b):
    return jnp.dot(a, b, preferred_element_type=F32)


def _log_sigmoid(x):
    return -(jnp.maximum(-x, 0.0) + jnp.log1p(jnp.exp(-jnp.abs(x))))


def _in_proj_kernel(x_ref, g_ref, wm_ref, wg_ref, wgu_ref, gb_ref,
                    q_ref, k_ref, v_ref, r_ref, la_ref, u_ref):
    hb = _rms(x_ref[...], g_ref[...]).astype(BF16)

    def seg(a, b):
        return _dot(hb, wm_ref[:, a:b])

    q_ref[...] = (seg(0, QK_W) * (GLA_DK ** -0.5)).astype(q_ref.dtype)
    k_ref[...] = seg(QK_W, 2 * QK_W).astype(k_ref.dtype)
    v_ref[...] = seg(2 * QK_W, 2 * QK_W + D_GLA).astype(v_ref.dtype)
    r_ref[...] = seg(2 * QK_W + D_GLA, 2 * QK_W + 2 * D_GLA).astype(r_ref.dtype)
    u_ref[...] = seg(2 * QK_W + 2 * D_GLA, 2 * QK_W + 2 * D_GLA + D_S5)
    zg = _dot(hb, wg_ref[...])
    xg = _dot(zg.astype(BF16), wgu_ref[...]) + gb_ref[...]
    la_ref[...] = _log_sigmoid(xg) * (1.0 / 16.0)


def _in_proj(x2d, g, wm, wg, wgu, gbias, act_dtype):
    t = x2d.shape[0]
    tm = TOK_TILE
    row = lambda w: pl.BlockSpec((tm, w), lambda i: (i, 0))
    return pl.pallas_call(
        _in_proj_kernel,
        grid=(t // tm,),
        in_specs=[row(D_MODEL), _const_spec((1, D_MODEL)), _const_spec(wm.shape), _const_spec(wg.shape),
                  _const_spec(wgu.shape), _const_spec((1, QK_W))],
        out_specs=[row(QK_W), row(QK_W), row(D_GLA), row(D_GLA), row(QK_W), row(D_S5)],
        out_shape=[jax.ShapeDtypeStruct((t, QK_W), act_dtype), jax.ShapeDtypeStruct((t, QK_W), act_dtype),
                   jax.ShapeDtypeStruct((t, D_GLA), act_dtype), jax.ShapeDtypeStruct((t, D_GLA), act_dtype),
                   jax.ShapeDtypeStruct((t, QK_W), F32), jax.ShapeDtypeStruct((t, D_S5), F32)],
        compiler_params=pltpu.CompilerParams(dimension_semantics=("arbitrary",), vmem_limit_bytes=VMEM_LIMIT),
        name="in_proj",
    )(x2d, g, wm, wg, wgu, gbias)


_NT = (((1,), (1,)), ((), ()))
_TN = (((0,), (0,)), ((), ()))


def _gla_pre(q, k, la, c):
    r = q.shape[0]
    shift = int(math.log2(c))
    ri = lax.broadcasted_iota(jnp.int32, (r, r), 0)
    si = lax.broadcasted_iota(jnp.int32, (r, r), 1)
    mask = ((ri >> shift) == (si >> shift)) & (ri >= si)
    tri = jnp.where(mask, 1.0, 0.0).astype(BF16)
    hi = la.astype(BF16)
    r1 = la - hi.astype(F32)
    mid = r1.astype(BF16)
    lo = (r1 - mid.astype(F32)).astype(BF16)
    cum = _dot(tri, hi) + _dot(tri, mid) + _dot(tri, lo)
    last = jnp.concatenate([jnp.broadcast_to(cum[(i + 1) * c - 1:(i + 1) * c, :], (c, cum.shape[1]))
                            for i in range(r // c)], axis=0)
    qe = (q * jnp.exp(cum)).astype(BF16)
    ke = (k * jnp.exp(-cum)).astype(BF16)
    kd = (k * jnp.exp(last - cum)).astype(BF16)
    return qe, ke, kd, cum, mask


def _gla_intra(qe, ke, v, mask):
    sc = lax.dot_general(qe, ke, _NT, preferred_element_type=F32)
    return _dot(jnp.where(mask, sc, 0.0).astype(BF16), v)


def _gla_finish(o, r, g):
    rf = r.astype(F32)
    return _rms(o, g) * (rf * jax.nn.sigmoid(rf))


GLA_SAFE_LOG_DECAY = -60.0


def _gla_token_step(t, rows, q, k, v, a_all, st_ref, oacc_ref):
    m = rows == t
    a = jnp.sum(jnp.where(m, a_all, 0.0), axis=0, keepdims=True)
    kt = jnp.where(m, k, 0.0).astype(BF16)
    qt = jnp.where(m, q, 0.0).astype(BF16)
    vt = jnp.where(m, v, jnp.zeros_like(v))
    st = st_ref[...] * a + lax.dot_general(vt, kt, _TN, preferred_element_type=F32)
    st_ref[...] = st
    oacc_ref[...] += lax.dot_general(qt, st.astype(BF16), _NT, preferred_element_type=F32)


def _gla_prompt_kernel(q_ref, k_ref, v_ref, r_ref, la_ref, g_ref, o_ref, sfin_ref, st_ref, oacc_ref,
                       *, n_chunks):
    j = pl.program_id(1)

    @pl.when(j == 0)
    def _():
        st_ref[...] = jnp.zeros_like(st_ref)

    c = GLA_CHUNK
    qe, ke, kd, cum, mask = _gla_pre(q_ref[...].astype(F32), k_ref[...].astype(F32), la_ref[...], c)
    g = g_ref[...]
    safe = jnp.min(cum) >= GLA_SAFE_LOG_DECAY

    @pl.when(safe)
    def _():
        for h in range(GLA_HEADS):
            kc = slice(h * GLA_DK, (h + 1) * GLA_DK)
            vc = slice(h * GLA_DV, (h + 1) * GLA_DV)
            v = v_ref[:, vc]
            qe_h, kd_h = qe[:, kc], kd[:, kc]
            o = _gla_intra(qe_h, ke[:, kc], v, mask)
            st = st_ref[h]
            inter = []
            for ci in range(n_chunks):
                rows = slice(ci * c, (ci + 1) * c)
                inter.append(lax.dot_general(qe_h[rows], st.astype(BF16), _NT, preferred_element_type=F32))
                dec = jnp.exp(cum[(ci + 1) * c - 1:(ci + 1) * c, kc])
                st = st * dec + lax.dot_general(v[rows], kd_h[rows], _TN, preferred_element_type=F32)
            st_ref[h] = st
            o = o + jnp.concatenate(inter, axis=0)
            o_ref[:, vc] = _gla_finish(o, r_ref[:, vc], g).astype(o_ref.dtype)

    @pl.when(jnp.logical_not(safe))
    def _():
        n_rows = n_chunks * c
        rows = lax.broadcasted_iota(jnp.int32, (n_rows, 1), 0)
        for h in range(GLA_HEADS):
            kc = slice(h * GLA_DK, (h + 1) * GLA_DK)
            vc = slice(h * GLA_DV, (h + 1) * GLA_DV)
            q, k, v = q_ref[:, kc].astype(F32), k_ref[:, kc].astype(F32), v_ref[:, vc]
            a_all = jnp.exp(la_ref[:, kc])
            oacc_ref[...] = jnp.zeros_like(oacc_ref)

            def step(t, carry):
                _gla_token_step(t, rows, q, k, v, a_all, st_ref.at[h], oacc_ref)
                return carry

            lax.fori_loop(0, n_rows, step, 0)
            o_ref[:, vc] = _gla_finish(oacc_ref[...], r_ref[:, vc], g).astype(o_ref.dtype)

    @pl.when(j == pl.num_programs(1) - 1)
    def _():
        for h in range(GLA_HEADS):
            sfin_ref[0, h] = st_ref[h].T


def _gla_prompt(q, k, v, r, la, g, batch, seq):
    rb = 4 * GLA_CHUNK
    nj = seq // rb
    row = lambda w: pl.BlockSpec((rb, w), lambda b, j: (b * nj + j, 0))
    return pl.pallas_call(
        functools.partial(_gla_prompt_kernel, n_chunks=rb // GLA_CHUNK),
        grid=(batch, nj),
        in_specs=[row(QK_W), row(QK_W), row(D_GLA), row(D_GLA), row(QK_W), _const_spec((1, GLA_DV))],
        out_specs=[row(D_GLA),
                   pl.BlockSpec((1, GLA_HEADS, GLA_DK, GLA_DV), lambda b, j: (b, 0, 0, 0))],
        out_shape=[jax.ShapeDtypeStruct((batch * seq, D_GLA), BF16),
                   jax.ShapeDtypeStruct((batch, GLA_HEADS, GLA_DK, GLA_DV), F32)],
        scratch_shapes=[pltpu.VMEM((GLA_HEADS, GLA_DV, GLA_DK), F32), pltpu.VMEM((rb, GLA_DV), F32)],
        compiler_params=pltpu.CompilerParams(dimension_semantics=("arbitrary", "arbitrary"),
                                             vmem_limit_bytes=VMEM_LIMIT),
        name="gla_prompt",
    )(q, k, v, r, la, g)


def _gla_sample_kernel(q_ref, k_ref, v_ref, r_ref, la_ref, g_ref, s0_ref, o_ref, sfin_ref, st_ref, oacc_ref,
                       *, n_seq, seq):
    qe, ke, kd, cum, mask = _gla_pre(q_ref[...].astype(F32), k_ref[...].astype(F32), la_ref[...], seq)
    g = g_ref[...]
    safe = jnp.min(cum) >= GLA_SAFE_LOG_DECAY

    @pl.when(safe)
    def _():
        lasts = jnp.concatenate([cum[(s + 1) * seq - 1:(s + 1) * seq, :] for s in range(n_seq)]
                                + [jnp.zeros((GLA_DK - n_seq, cum.shape[1]), F32)], axis=0)
        pair = 2 * seq
        upper = lax.broadcasted_iota(jnp.int32, (pair, GLA_DK), 0) < seq
        for h in range(GLA_HEADS):
            kc = slice(h * GLA_DK, (h + 1) * GLA_DK)
            vc = slice(h * GLA_DV, (h + 1) * GLA_DV)
            v = v_ref[:, vc]
            qe_h, kd_h = qe[:, kc], kd[:, kc]
            o = _gla_intra(qe_h, ke[:, kc], v, mask)
            dec_t = jnp.exp(lasts[:, kc].T)
            inter = []
            for p in range(n_seq // 2):
                rows = slice(p * pair, (p + 1) * pair)
                qe_p, kd_p, v_p = qe_h[rows], kd_h[rows], v[rows]
                for half in range(2):
                    s = 2 * p + half
                    s0 = s0_ref[s, h]
                    o_s = _dot(qe_p, s0.astype(BF16))
                    inter.append(o_s[half * seq:(half + 1) * seq])
                    kd_s = jnp.where(upper if half == 0 else ~upper, kd_p, jnp.zeros_like(kd_p))
                    dec = jnp.broadcast_to(dec_t[:, s:s + 1], (GLA_DK, GLA_DV))
                    sfin_ref[s, h] = s0 * dec + lax.dot_general(kd_s, v_p, _TN, preferred_element_type=F32)
            o = o + jnp.concatenate(inter, axis=0)
            o_ref[:, vc] = _gla_finish(o, r_ref[:, vc], g).astype(o_ref.dtype)

    @pl.when(jnp.logical_not(safe))
    def _():
        n_rows = n_seq * seq
        rows = lax.broadcasted_iota(jnp.int32, (n_rows, 1), 0)
        for h in range(GLA_HEADS):
            kc = slice(h * GLA_DK, (h + 1) * GLA_DK)
            vc = slice(h * GLA_DV, (h + 1) * GLA_DV)
            q, k, v = q_ref[:, kc].astype(F32), k_ref[:, kc].astype(F32), v_ref[:, vc]
            a_all = jnp.exp(la_ref[:, kc])
            oacc_ref[...] = jnp.zeros_like(oacc_ref)

            def step(t, carry):
                s = t // seq

                @pl.when(t % seq == 0)
                def _():
                    st_ref[...] = s0_ref[s, h].T

                _gla_token_step(t, rows, q, k, v, a_all, st_ref, oacc_ref)

                @pl.when(t % seq == seq - 1)
                def _():
                    sfin_ref[s, h] = st_ref[...].T

                return carry

            lax.fori_loop(0, n_rows, step, 0)
            o_ref[:, vc] = _gla_finish(oacc_ref[...], r_ref[:, vc], g).astype(o_ref.dtype)


def _gla_sample(q, k, v, r, la, g, s0, batch, seq):
    ns = 16
    rb = ns * seq
    row = lambda w: pl.BlockSpec((rb, w), lambda i: (i, 0))
    st = pl.BlockSpec((ns, GLA_HEADS, GLA_DK, GLA_DV), lambda i: (i, 0, 0, 0))
    return pl.pallas_call(
        functools.partial(_gla_sample_kernel, n_seq=ns, seq=seq),
        grid=(batch // ns,),
        in_specs=[row(QK_W), row(QK_W), row(D_GLA), row(D_GLA), row(QK_W), _const_spec((1, GLA_DV)), st],
        out_specs=[row(D_GLA), st],
        out_shape=[jax.ShapeDtypeStruct((batch * seq, D_GLA), BF16),
                   jax.ShapeDtypeStruct((batch, GLA_HEADS, GLA_DK, GLA_DV), F32)],
        scratch_shapes=[pltpu.VMEM((GLA_DV, GLA_DK), F32), pltpu.VMEM((rb, GLA_DV), F32)],
        compiler_params=pltpu.CompilerParams(dimension_semantics=("arbitrary",), vmem_limit_bytes=VMEM_LIMIT),
        name="gla_sample",
    )(q, k, v, r, la, g, s0)


def _s5_kernel(u_ref, wbu_ref, wc_ref, are_ref, aim_ref, d_ref, h0r_ref, h0i_ref,
               y_ref, sre_ref, sim_ref, bu_ref, xs_ref, car_ref, *, nb, tc, has_state):
    j = pl.program_id(2)

    @pl.when(j == 0)
    def _():
        if has_state:
            car_ref[0] = h0r_ref[...]
            car_ref[1] = h0i_ref[...]
        else:
            car_ref[...] = jnp.zeros_like(car_ref)

    u2 = u_ref[...].reshape(nb * tc, LANES)
    ub = u2.astype(BF16)
    nl = S5_SL // LANES
    for l in range(2 * nl):
        bu_ref[l] = _dot(ub, wbu_ref[0, :, l * LANES:(l + 1) * LANES])
    a_r = [jnp.broadcast_to(are_ref[0, :, l * LANES:(l + 1) * LANES], (nb, LANES)) for l in range(nl)]
    a_i = [jnp.broadcast_to(aim_ref[0, :, l * LANES:(l + 1) * LANES], (nb, LANES)) for l in range(nl)]

    def step(t, carry):
        rows = pl.ds(t, nb, stride=tc)
        out = []
        for l in range(nl):
            xr, xi = carry[2 * l], carry[2 * l + 1]
            nr = a_r[l] * xr - a_i[l] * xi + bu_ref[l, rows, :]
            ni = a_r[l] * xi + a_i[l] * xr + bu_ref[nl + l, rows, :]
            xs_ref[l, rows, :] = nr
            xs_ref[nl + l, rows, :] = ni
            out += [nr, ni]
        return tuple(out)

    init = []
    for l in range(nl):
        init += [car_ref[0, :, l * LANES:(l + 1) * LANES], car_ref[1, :, l * LANES:(l + 1) * LANES]]
    fin = lax.fori_loop(0, tc, step, tuple(init), unroll=8)
    xr = jnp.concatenate([fin[2 * l] for l in range(nl)], axis=1)
    xi = jnp.concatenate([fin[2 * l + 1] for l in range(nl)], axis=1)
    car_ref[0] = xr
    car_ref[1] = xi
    y = d_ref[0] * u2
    for l in range(2 * nl):
        y = y + _dot(xs_ref[l].astype(BF16), wc_ref[0, l * LANES:(l + 1) * LANES, :])
    y_ref[...] = y.reshape(nb, tc, LANES)

    @pl.when(j == pl.num_programs(2) - 1)
    def _():
        sre_ref[...] = xr
        sim_ref[...] = xi


def _s5(u3d, wbu, wc, a_re, a_im, dsk, h0r, h0i, nb, tc, has_state):
    batch, seq, _ = u3d.shape
    grid = (S5_NGB, batch // nb, seq // tc)
    st = pl.BlockSpec((nb, S5_SL), lambda g, b, j: (b, g))
    par = lambda w: pl.BlockSpec((1, 1, w), lambda g, b, j: (g, 0, 0))
    ublk = pl.BlockSpec((nb, tc, LANES), lambda g, b, j: (b, j, g))
    return pl.pallas_call(
        functools.partial(_s5_kernel, nb=nb, tc=tc, has_state=has_state),
        grid=grid,
        in_specs=[ublk,
                  pl.BlockSpec((1, LANES, 2 * S5_SL), lambda g, b, j: (g, 0, 0)),
                  pl.BlockSpec((1, 2 * S5_SL, LANES), lambda g, b, j: (g, 0, 0)),
                  par(S5_SL), par(S5_SL), par(LANES), st, st],
        out_specs=[ublk, st, st],
        out_shape=[jax.ShapeDtypeStruct(u3d.shape, F32),
                   jax.ShapeDtypeStruct((batch, S5_GROUPS * S5_STATE), F32),
                   jax.ShapeDtypeStruct((batch, S5_GROUPS * S5_STATE), F32)],
        scratch_shapes=[pltpu.VMEM((2 * S5_SL // LANES, nb * tc, LANES), F32),
                        pltpu.VMEM((2 * S5_SL // LANES, nb * tc, LANES), F32),
                        pltpu.VMEM((2, nb, S5_SL), F32)],
        compiler_params=pltpu.CompilerParams(dimension_semantics=("arbitrary",) * 3,
                                             vmem_limit_bytes=VMEM_LIMIT),
        name="s5_state" if has_state else "s5_zero",
    )(u3d, wbu, wc, a_re, a_im, dsk, h0r, h0i)


S5_J = 2
S5_HL = S5_SL // S5_J


def _s5_prompt_kernel(u_ref, wbu_ref, wc_ref, are_ref, aim_ref, d_ref, y_ref, sre_ref, sim_ref,
                      u2_ref, lhs_ref, bu_ref, xs_ref, y2_ref, yo_ref, car_ref, *, nb, tc):
    g = pl.program_id(0)
    j = pl.program_id(1)
    rows = nb * S5_J

    @pl.when((g == 0) & (j == 0))
    def _():
        lhs_ref[...] = jnp.zeros_like(lhs_ref)

    @pl.when(j == 0)
    def _():
        car_ref[...] = jnp.zeros_like(car_ref)

    u2 = u_ref[...].reshape(nb * tc, LANES)
    u2_ref[...] = u2

    def build(t, c):
        u4 = u2_ref[pl.ds(t, nb, stride=tc), :]
        for jj in range(S5_J):
            lhs_ref[t, jj * nb:(jj + 1) * nb, jj * LANES:(jj + 1) * LANES] = u4
        return c

    lax.fori_loop(0, tc, build, 0, unroll=8)
    lhs = lhs_ref[...].reshape(tc * rows, S5_J * LANES).astype(BF16)
    bu_ref[...] = _dot(lhs, wbu_ref[0]).reshape(tc, rows, 2 * S5_HL)
    ar = are_ref[0]
    ai = aim_ref[0]

    def step(t, carry):
        xr, xi = carry
        tile = bu_ref[t]
        nr = ar * xr - ai * xi + tile[:, 0:S5_HL]
        ni = ar * xi + ai * xr + tile[:, S5_HL:2 * S5_HL]
        xs_ref[t] = jnp.concatenate([nr, ni], axis=1)
        return nr, ni

    xr, xi = lax.fori_loop(0, tc, step, (car_ref[0], car_ref[1]), unroll=8)
    car_ref[0] = xr
    car_ref[1] = xi
    xs = xs_ref[...].reshape(tc * rows, 2 * S5_HL).astype(BF16)
    y2_ref[...] = _dot(xs, wc_ref[0]).reshape(tc, rows, S5_J * LANES)
    first_half = lax.broadcasted_iota(jnp.int32, (rows, LANES), 0) < nb

    def unperm(t, c):
        t2 = y2_ref[t]
        part = jnp.where(first_half, t2[:, 0:LANES], t2[:, LANES:2 * LANES])
        tot = part + pltpu.roll(part, nb, axis=0)
        yo_ref[pl.ds(t, nb, stride=tc), :] = tot[0:nb]
        return c

    lax.fori_loop(0, tc, unperm, 0, unroll=8)
    y_ref[...] = (yo_ref[...] + d_ref[0] * u2).reshape(nb, tc, LANES)

    @pl.when(j == pl.num_programs(1) - 1)
    def _():
        sre_ref[...] = jnp.concatenate([xr[jj * nb:(jj + 1) * nb] for jj in range(S5_J)], axis=1)
        sim_ref[...] = jnp.concatenate([xi[jj * nb:(jj + 1) * nb] for jj in range(S5_J)], axis=1)


def _s5_prompt(u3d, wbu2, wc2, a_re2, a_im2, dsk, tc):
    nb, seq, _ = u3d.shape
    rows = nb * S5_J
    assert rows == 8
    st = pl.BlockSpec((nb, S5_SL), lambda g, j: (0, g))
    gblk = lambda s: pl.BlockSpec((1,) + s, lambda g, j: (g, 0, 0))
    ublk = pl.BlockSpec((nb, tc, LANES), lambda g, j: (0, j, g))
    return pl.pallas_call(
        functools.partial(_s5_prompt_kernel, nb=nb, tc=tc),
        grid=(S5_NGB, seq // tc),
        in_specs=[ublk, gblk((S5_J * LANES, 2 * S5_HL)), gblk((2 * S5_HL, S5_J * LANES)),
                  gblk((rows, S5_HL)), gblk((rows, S5_HL)), gblk((1, LANES))],
        out_specs=[ublk, st, st],
        out_shape=[jax.ShapeDtypeStruct(u3d.shape, F32),
                   jax.ShapeDtypeStruct((nb, S5_GROUPS * S5_STATE), F32),
                   jax.ShapeDtypeStruct((nb, S5_GROUPS * S5_STATE), F32)],
        scratch_shapes=[pltpu.VMEM((nb * tc, LANES), F32),
                        pltpu.VMEM((tc, rows, S5_J * LANES), F32),
                        pltpu.VMEM((tc, rows, 2 * S5_HL), F32),
                        pltpu.VMEM((tc, rows, 2 * S5_HL), F32),
                        pltpu.VMEM((tc, rows, S5_J * LANES), F32),
                        pltpu.VMEM((nb * tc, LANES), F32),
                        pltpu.VMEM((2, rows, S5_HL), F32)],
        compiler_params=pltpu.CompilerParams(dimension_semantics=("arbitrary",) * 2,
                                             vmem_limit_bytes=VMEM_LIMIT),
        name="s5_prompt",
    )(u3d, wbu2, wc2, a_re2, a_im2, dsk)


def _s5_prompt_params(wbu, wc, a_re, a_im, nb):
    h = S5_HL
    top = jnp.concatenate([wbu[:, :, 0:h], wbu[:, :, S5_SL:S5_SL + h]], axis=2)
    bot = jnp.concatenate([wbu[:, :, h:2 * h], wbu[:, :, S5_SL + h:S5_SL + 2 * h]], axis=2)
    wbu2 = jnp.concatenate([top, bot], axis=1)
    wc_j = [jnp.concatenate([wc[:, jj * h:(jj + 1) * h, :], wc[:, S5_SL + jj * h:S5_SL + (jj + 1) * h, :]],
                            axis=1) for jj in range(S5_J)]
    wc2 = jnp.concatenate(wc_j, axis=2)
    tile = lambda a: jnp.repeat(a.reshape(S5_NGB, S5_J, h), nb, axis=1)
    return wbu2, wc2, tile(a_re), tile(a_im)


def _mix_out_kernel(xp_ref, xs_ref, ogp_ref, ogs_ref, ysp_ref, yss_ref, glu_w_ref, glu_b_ref, s5n_ref, wo_ref,
                    nffn_ref, wr_ref, h1_ref, hn_ref, rt_ref, *, n_prompt_tiles):
    is_p = pl.program_id(0) < n_prompt_tiles
    x = jnp.where(is_p, xp_ref[...], xs_ref[...])
    og = jnp.where(is_p, ogp_ref[...], ogs_ref[...])
    y = jax.nn.gelu(jnp.where(is_p, ysp_ref[...], yss_ref[...]))
    y = y * jax.nn.sigmoid(_dot(y.astype(BF16), glu_w_ref[...]) + glu_b_ref[...])
    yn = _rms(y, s5n_ref[...]).astype(BF16)
    mix = _dot(og, wo_ref[0:D_GLA, :]) + _dot(yn, wo_ref[D_GLA:D_GLA + D_S5, :])
    h1 = x + mix
    h1_ref[...] = h1
    hn = _rms(h1, nffn_ref[...])
    hn_ref[...] = _pack_bf16_pair(hn[:, 0:HALF], hn[:, HALF:D_MODEL])
    hn_hi = hn.astype(BF16)
    hn_lo = (hn - hn_hi.astype(F32)).astype(BF16)
    logits = _dot(hn_hi, wr_ref[0]) + _dot(hn_hi, wr_ref[1]) + _dot(hn_lo, wr_ref[0])
    rt_ref[...] = _route(logits)


def _route(logits):
    col = lax.broadcasted_iota(jnp.int32, logits.shape, 1)
    colf = col.astype(F32)
    neg = -jnp.inf

    def first_argmax(vals):
        m = jnp.max(vals, axis=-1, keepdims=True)
        idx = jnp.min(jnp.where(vals == m, colf, float(LANES)), axis=-1, keepdims=True)
        return m, idx

    lg = jnp.where(col < N_EGROUPS, logits, neg)
    gmax, gsel = first_argmax(lg)
    p_g = 1.0 / jnp.sum(jnp.exp(lg - gmax), axis=-1, keepdims=True)
    ecol = col - N_EGROUPS
    egrp = (ecol >> 3).astype(F32)
    in_group = (ecol >= 0) & (ecol < N_EXPERTS) & (egrp == gsel)
    le = jnp.where(in_group, logits, neg)
    m1, i1 = first_argmax(le)
    le2 = jnp.where(colf == i1, neg, le)
    m2, i2 = first_argmax(le2)
    e2 = jnp.exp(m2 - m1)
    den = 1.0 + e2
    w1 = p_g * (1.0 / den)
    w2 = p_g * (e2 / den)
    e1f = i1 - float(N_EGROUPS)
    e2f = i2 - float(N_EGROUPS)
    out = jnp.where(col == 0, e1f, jnp.where(col == 1, e2f, jnp.where(col == 2, w1, jnp.where(col == 3, w2, 0.0))))
    return out


def _mix_out(xp, xs, ogp, ogs, ysp, yss, glu_w, glu_b, s5n, wo, nffn, wr):
    tm = TOK_TILE
    npt, nst = xp.shape[0] // tm, xs.shape[0] // tm
    t = (npt + nst) * tm
    row = lambda w: pl.BlockSpec((tm, w), lambda i: (i, 0))
    prow = lambda w: pl.BlockSpec((tm, w), lambda i: (jnp.minimum(i, npt - 1), 0))
    srow = lambda w: pl.BlockSpec((tm, w), lambda i: (jnp.maximum(i - npt, 0), 0))
    return pl.pallas_call(
        functools.partial(_mix_out_kernel, n_prompt_tiles=npt),
        grid=(npt + nst,),
        in_specs=[prow(D_MODEL), srow(D_MODEL), prow(D_GLA), srow(D_GLA), prow(D_S5), srow(D_S5),
                  _const_spec(glu_w.shape), _const_spec((1, D_S5)), _const_spec((1, D_S5)),
                  _const_spec(wo.shape), _const_spec((1, D_MODEL)), _const_spec(wr.shape)],
        out_specs=[row(D_MODEL), row(HALF), row(LANES)],
        out_shape=[jax.ShapeDtypeStruct((t, D_MODEL), F32), jax.ShapeDtypeStruct((t, HALF), jnp.uint32),
                   jax.ShapeDtypeStruct((t, LANES), F32)],
        compiler_params=pltpu.CompilerParams(dimension_semantics=("arbitrary",), vmem_limit_bytes=VMEM_LIMIT),
        name="mix_out",
    )(xp, xs, ogp, ogs, ysp, yss, glu_w, glu_b, s5n, wo, nffn, wr)


DMA_UNROLL = 8


MOE_SB = 3
HALF = D_MODEL // 2


def _pack_bf16_pair(lo, hi):
    def bits(x):
        b = pltpu.bitcast(x, jnp.uint32)
        return (b + jnp.uint32(0x7FFF) + ((b >> 16) & jnp.uint32(1))) >> 16
    return bits(lo) | (bits(hi) << 16)


def _unpack_bf16_pair(u):
    return (pltpu.bitcast(u << 16, F32), pltpu.bitcast(u & jnp.uint32(0xFFFF0000), F32))


def _moe_kernel(se_ref, sm_ref, sn_ref, sbase_ref, sa_ref, hn_hbm, wg_ref, wu_ref, wd_ref, out_hbm,
                xb_ref, yb_ref, wgb_ref, wub_ref, wdb_ref, gsem, ssem):
    b = pl.program_id(0)
    nb = pl.num_programs(0)
    n = sn_ref[b]
    slot = lax.rem(b, 2)

    def round_up(cnt):
        return (cnt + DMA_UNROLL - 1) // DMA_UNROLL * DMA_UNROLL

    def issue_gather(blk, sl):
        base = sbase_ref[blk]

        def grp(g, c):
            for j in range(DMA_UNROLL):
                i = g * DMA_UNROLL + j
                tok = sa_ref[base + i] >> 1
                pltpu.make_async_copy(hn_hbm.at[pl.ds(tok, 1), :], xb_ref.at[sl, pl.ds(i, 1), :],
                                      gsem.at[sl]).start()
            return c

        lax.fori_loop(0, round_up(sn_ref[blk]) // DMA_UNROLL, grp, 0)

    def wait_gather(cnt, sl):
        rows = pl.ds(0, pl.multiple_of(round_up(cnt), DMA_UNROLL))
        pltpu.make_async_copy(hn_hbm.at[rows, :], xb_ref.at[sl, rows, :], gsem.at[sl]).wait()

    m_cur = sm_ref[b]
    m_prev = sm_ref[jnp.maximum(b - 1, 0)]
    has_prev = (b > 0) & (m_prev > 0)

    def out_copy(rows):
        dst = pl.ds(pl.multiple_of(sbase_ref[b], MOE_BLK), rows)
        return pltpu.make_async_copy(yb_ref.at[pl.ds(0, rows), :], out_hbm.at[dst, :], ssem.at[0])

    def wait_prev_out():
        rows = pl.ds(0, pl.multiple_of(m_prev * MOE_BLK, MOE_BLK))
        pltpu.make_async_copy(yb_ref.at[rows, :], out_hbm.at[rows, :], ssem.at[0]).wait()

    def write_out(rows):
        out_copy(rows).start()

        @pl.when(b == nb - 1)
        def _():
            out_copy(rows).wait()

    @pl.when(b == 0)
    def _():
        xb_ref[...] = jnp.zeros_like(xb_ref)
        issue_gather(0, 0)

    @pl.when(b + 1 < nb)
    def _():
        issue_gather(b + 1, 1 - slot)

    def compute(rows):
        x_lo, x_hi = _unpack_bf16_pair(xb_ref[slot, 0:rows, :])
        x_lo, x_hi = x_lo.astype(BF16), x_hi.astype(BF16)
        gate = _dot(x_lo, wgb_ref[0:HALF, :]) + _dot(x_hi, wgb_ref[HALF:D_MODEL, :])
        up = _dot(x_lo, wub_ref[0:HALF, :]) + _dot(x_hi, wub_ref[HALF:D_MODEL, :])
        hid = (gate * jax.nn.sigmoid(gate) * up).astype(BF16)

        @pl.when(has_prev)
        def _():
            wait_prev_out()

        yb_ref[0:rows, :] = _pack_bf16_pair(_dot(hid, wdb_ref[:, 0:HALF]), _dot(hid, wdb_ref[:, HALF:D_MODEL]))
        write_out(rows)

    @pl.when(n > 0)
    def _():
        prev_e = se_ref[jnp.maximum(b - 1, 0)]

        @pl.when((b == 0) | (prev_e != se_ref[b]))
        def _():
            wgb_ref[...] = wg_ref[...].astype(BF16)
            wub_ref[...] = wu_ref[...].astype(BF16)
            wdb_ref[...] = wd_ref[...].astype(BF16)

        wait_gather(n, slot)
        for m in range(1, MOE_SB + 1):
            @pl.when(m_cur == m)
            def _():
                compute(m * MOE_BLK)

    @pl.when((n == 0) & (m_cur > 0))
    def _():
        @pl.when(has_prev)
        def _():
            wait_prev_out()

        yb_ref[...] = jnp.zeros_like(yb_ref)
        for m in range(1, MOE_SB + 1):
            @pl.when(m_cur == m)
            def _():
                write_out(m * MOE_BLK)

    @pl.when((m_cur == 0) & has_prev)
    def _():
        wait_prev_out()


def _moe(hn_pk, plan, w_gate, w_up, w_down, n_out_rows):
    sb_e, sb_m, sb_n, sb_base, slot_a = plan
    sb_rows = MOE_SB * MOE_BLK
    wspec = lambda s: pl.BlockSpec((None,) + s, lambda b, se, *_: (se[b], 0, 0))
    grid_spec = pltpu.PrefetchScalarGridSpec(
        num_scalar_prefetch=5,
        grid=(sb_e.shape[0],),
        in_specs=[pl.BlockSpec(memory_space=pl.ANY),
                  wspec((D_MODEL, D_EXPERT)), wspec((D_MODEL, D_EXPERT)), wspec((D_EXPERT, D_MODEL))],
        out_specs=pl.BlockSpec(memory_space=pl.ANY),
        scratch_shapes=[pltpu.VMEM((2, sb_rows, HALF), jnp.uint32), pltpu.VMEM((sb_rows, HALF), jnp.uint32),
                        pltpu.VMEM((D_MODEL, D_EXPERT), BF16), pltpu.VMEM((D_MODEL, D_EXPERT), BF16),
                        pltpu.VMEM((D_EXPERT, D_MODEL), BF16), pltpu.SemaphoreType.DMA((2,)),
                        pltpu.SemaphoreType.DMA((1,))],
    )
    return pl.pallas_call(
        _moe_kernel,
        grid_spec=grid_spec,
        out_shape=jax.ShapeDtypeStruct((n_out_rows, HALF), jnp.uint32),
        compiler_params=pltpu.CompilerParams(dimension_semantics=("arbitrary",),
                                             vmem_limit_bytes=MOE_VMEM_LIMIT),
        name="moe_experts",
    )(sb_e, sb_m, sb_n, sb_base, slot_a, hn_pk, w_gate, w_up, w_down)


def _moe_plan(rt_all):
    t_all = rt_all.shape[0]
    n_assign = 2 * t_all
    e_flat = rt_all[:, 0:2].astype(jnp.int32).reshape(-1)
    onehot = (e_flat[:, None] == jnp.arange(N_EXPERTS, dtype=jnp.int32)[None, :]).astype(jnp.int32)
    csum = jnp.cumsum(onehot, axis=0)
    counts = csum[-1]
    rank = jnp.take_along_axis(csum, e_flat[:, None], axis=1)[:, 0] - 1
    padded = (counts + MOE_BLK - 1) // MOE_BLK * MOE_BLK
    pad_end = jnp.cumsum(padded)
    pad_start = pad_end - padded
    dest = pad_start[e_flat] + rank
    nblk = -(-(n_assign + N_EXPERTS * (MOE_BLK - 1)) // MOE_BLK)
    n_slots = nblk * MOE_BLK
    slot_a = jnp.zeros((n_slots,), jnp.int32).at[dest].set(jnp.arange(n_assign, dtype=jnp.int32))
    k_e = padded // MOE_BLK
    sbc = (k_e + MOE_SB - 1) // MOE_SB
    sb_end = jnp.cumsum(sbc)
    sb_start = sb_end - sbc
    n_sb = (nblk + (MOE_SB - 1) * N_EXPERTS) // MOE_SB
    s = jnp.arange(n_sb, dtype=jnp.int32)
    sb_e = jnp.minimum(jnp.sum((s[:, None] >= sb_end[None, :]).astype(jnp.int32), axis=1), N_EXPERTS - 1)
    j = s - sb_start[sb_e]
    real = s < sb_end[-1]
    tail_blk = pad_end[-1] // MOE_BLK + MOE_SB * (s - sb_end[-1])
    sb_m = jnp.where(real, jnp.clip(k_e[sb_e] - MOE_SB * j, 0, MOE_SB), jnp.clip(nblk - tail_blk, 0, MOE_SB))
    sb_n = jnp.where(real, jnp.clip(counts[sb_e] - MOE_SB * MOE_BLK * j, 0, MOE_SB * MOE_BLK), 0)
    sb_base = jnp.where(real, pad_start[sb_e] + MOE_SB * MOE_BLK * j, jnp.minimum(tail_blk, nblk - 1) * MOE_BLK)
    i32 = lambda a: a.astype(jnp.int32)
    return (i32(sb_e), i32(sb_m), i32(sb_n), i32(sb_base), slot_a), i32(dest), n_slots


def _ple_out_kernel(dest_ref, h1_ref, rt_ref, pp_ref, ps_ref, nple_ref, wpg_ref, wp_ref, nfin_ref, ys_hbm,
                    op_ref, os_ref, yb_ref, sem, *, n_prompt_tiles):
    i = pl.program_id(0)
    nt = pl.num_programs(0)
    tm = h1_ref.shape[0]
    slot = lax.rem(i, 2)

    def issue(tile, sl):
        base = tile * (2 * tm)

        def grp(g, c):
            for j in range(DMA_UNROLL):
                d = dest_ref[base + g * DMA_UNROLL + j]
                row = g * (DMA_UNROLL // 2) + j // 2
                pltpu.make_async_copy(ys_hbm.at[pl.ds(d, 1), :], yb_ref.at[sl, j % 2, pl.ds(row, 1), :],
                                      sem.at[sl]).start()
            return c

        lax.fori_loop(0, 2 * tm // DMA_UNROLL, grp, 0)

    @pl.when(i == 0)
    def _():
        issue(0, 0)

    @pl.when(i + 1 < nt)
    def _():
        issue(i + 1, 1 - slot)

    for k in range(2):
        pltpu.make_async_copy(ys_hbm.at[pl.ds(0, tm), :], yb_ref.at[slot, k], sem.at[slot]).wait()
    is_p = i < n_prompt_tiles
    rt = rt_ref[...]
    w0, w1 = rt[:, 2:3], rt[:, 3:4]
    lo0, hi0 = _unpack_bf16_pair(yb_ref[slot, 0])
    lo1, hi1 = _unpack_bf16_pair(yb_ref[slot, 1])
    h2 = h1_ref[...] + jnp.concatenate([w0 * lo0 + w1 * lo1, w0 * hi0 + w1 * hi1], axis=1)
    gate = jax.nn.sigmoid(_dot(_rms(h2, nple_ref[...]).astype(BF16), wpg_ref[...]))
    p = jnp.where(is_p, pp_ref[...], ps_ref[...])
    h3 = h2 + _dot(p.astype(BF16), wp_ref[...]) * gate
    y = _rms(h3, nfin_ref[...])

    @pl.when(is_p)
    def _():
        op_ref[...] = y

    @pl.when(jnp.logical_not(is_p))
    def _():
        os_ref[...] = y


def _ple_out(dest, h1, rt, pp, ps, nple, wpg, wp, nfin, y_slots):
    tm = TOK_TILE
    npt, nst = pp.shape[0] // tm, ps.shape[0] // tm
    row = lambda w: pl.BlockSpec((tm, w), lambda i, d: (i, 0))
    prow = lambda w: pl.BlockSpec((tm, w), lambda i, d: (jnp.minimum(i, npt - 1), 0))
    srow = lambda w: pl.BlockSpec((tm, w), lambda i, d: (jnp.maximum(i - npt, 0), 0))
    const = lambda shape: pl.BlockSpec(shape, lambda i, d: (0,) * len(shape), pipeline_mode=pl.Buffered(1))
    grid_spec = pltpu.PrefetchScalarGridSpec(
        num_scalar_prefetch=1,
        grid=(npt + nst,),
        in_specs=[row(D_MODEL), row(LANES), prow(D_PLE), srow(D_PLE), const((1, D_MODEL)), const(wpg.shape),
                  const(wp.shape), const((1, D_MODEL)), pl.BlockSpec(memory_space=pl.ANY)],
        out_specs=[prow(D_MODEL), srow(D_MODEL)],
        scratch_shapes=[pltpu.VMEM((2, 2, tm, HALF), jnp.uint32), pltpu.SemaphoreType.DMA((2,))],
    )
    return pl.pallas_call(
        functools.partial(_ple_out_kernel, n_prompt_tiles=npt),
        grid_spec=grid_spec,
        out_shape=[jax.ShapeDtypeStruct((pp.shape[0], D_MODEL), F32),
                   jax.ShapeDtypeStruct((ps.shape[0], D_MODEL), F32)],
        compiler_params=pltpu.CompilerParams(dimension_semantics=("arbitrary",), vmem_limit_bytes=VMEM_LIMIT),
        name="ple_out",
    )(dest, h1, rt, pp, ps, nple, wpg, wp, nfin, y_slots)


def _s5_params(lam_re, lam_im, log_dt, b_re, b_im, c_re, c_im, d_skip):
    dt = jnp.exp(log_dt)[:, None]
    mag = jnp.exp(lam_re * dt)
    ab_re = mag * jnp.cos(lam_im * dt)
    ab_im = mag * jnp.sin(lam_im * dt)
    den = lam_re * lam_re + lam_im * lam_im
    nr = ab_re - 1.0
    f_re = (nr * lam_re + ab_im * lam_im) / den
    f_im = (ab_im * lam_re - nr * lam_im) / den
    bb_re = f_re[..., None] * b_re - f_im[..., None] * b_im
    bb_im = f_re[..., None] * b_im + f_im[..., None] * b_re
    eye = jnp.eye(S5_GB, dtype=F32)

    def bu_w(bb):
        bb = bb.reshape(S5_NGB, S5_GB, S5_STATE, S5_GROUP)
        w = jnp.einsum('nlph,lm->nlhmp', bb, eye)
        return w.reshape(S5_NGB, S5_GB * S5_GROUP, S5_SL)

    def c_w(c):
        c = c.reshape(S5_NGB, S5_GB, S5_GROUP, S5_STATE)
        w = jnp.einsum('nlhp,lm->nlpmh', c, eye)
        return w.reshape(S5_NGB, S5_SL, S5_GB * S5_GROUP)

    wbu = jnp.concatenate([bu_w(bb_re), bu_w(bb_im)], axis=2).astype(BF16)
    wc = jnp.concatenate([c_w(c_re), -c_w(c_im)], axis=1).astype(BF16)
    a_re = ab_re.reshape(S5_NGB, 1, S5_SL)
    a_im = ab_im.reshape(S5_NGB, 1, S5_SL)
    dsk = d_skip.reshape(S5_NGB, 1, S5_GB * S5_GROUP)
    return wbu, wc, a_re, a_im, dsk


def kernel(x_prompt, x_sample, p_prompt, p_sample, state_gla, state_s5_re, state_s5_im, norm_mix, w_in, gla_w_gate_up, gla_gate_bias, gla_norm, s5_lam_re, s5_lam_im, s5_log_dt, s5_b_re, s5_b_im, s5_c_re, s5_c_im, s5_d, s5_glu_w, s5_glu_b, s5_norm, w_out, norm_ffn, router_group, router_expert, w_gate, w_up, w_down, norm_ple, w_ple, w_ple_gate, norm_final):
    depth = w_in.shape[0]
    assert depth == 1
    i = 0
    bp, lp, _ = x_prompt.shape
    bs, ls, _ = x_sample.shape
    tp, ts = bp * lp, bs * ls
    t_all = tp + ts

    n_qkvr = 2 * QK_W + 2 * D_GLA
    wi = w_in[i]
    wm = jnp.concatenate([wi[:, :n_qkvr], wi[:, n_qkvr + GLA_RANK:]], axis=1).astype(BF16)
    wg = jnp.pad(wi[:, n_qkvr:n_qkvr + GLA_RANK], ((0, 0), (0, LANES - GLA_RANK))).astype(BF16)
    wgu = jnp.pad(gla_w_gate_up[i], ((0, LANES - GLA_RANK), (0, 0))).astype(BF16)
    gbias = gla_gate_bias[i].reshape(1, QK_W)
    wbu, wc, a_re, a_im, dsk = _s5_params(s5_lam_re[i], s5_lam_im[i], s5_log_dt[i], s5_b_re[i], s5_b_im[i],
                                          s5_c_re[i], s5_c_im[i], s5_d[i])
    glu_w = s5_glu_w[i].astype(BF16)
    wo = w_out[i].astype(BF16)
    wr32 = jnp.pad(jnp.concatenate([router_group[i], router_expert[i]], axis=1),
                   ((0, 0), (0, LANES - N_EGROUPS - N_EXPERTS)))
    wr_hi = wr32.astype(BF16)
    wr = jnp.stack([wr_hi, (wr32 - wr_hi.astype(F32)).astype(BF16)])
    wpg = w_ple_gate[i].astype(BF16)
    wp = w_ple[i].astype(BF16)
    vec = lambda a: a.reshape(1, -1)

    xp = x_prompt.reshape(tp, D_MODEL)
    xs = x_sample.reshape(ts, D_MODEL)

    qp, kp, vp, rp, lap, up = _in_proj(xp, vec(norm_mix[i]), wm, wg, wgu, gbias, BF16)
    qs, ks, vs, rs, las, us = _in_proj(xs, vec(norm_mix[i]), wm, wg, wgu, gbias, BF16)
    ogp, gla_p = _gla_prompt(qp, kp, vp, rp, lap, vec(gla_norm[i]), bp, lp)
    ogs, gla_s = _gla_sample(qs, ks, vs, rs, las, vec(gla_norm[i]), state_gla[i], bs, ls)
    wbu2, wc2, a_re2, a_im2 = _s5_prompt_params(wbu, wc, a_re, a_im, bp)
    ysp, re_p, im_p = _s5_prompt(up.reshape(bp, lp, D_S5), wbu2, wc2, a_re2, a_im2, dsk, tc=256)
    yss, re_s, im_s = _s5(us.reshape(bs, ls, D_S5), wbu, wc, a_re, a_im, dsk,
                          state_s5_re[i].reshape(bs, -1), state_s5_im[i].reshape(bs, -1),
                          nb=32, tc=ls, has_state=True)

    h1, hn_all, rt_all = _mix_out(xp, xs, ogp, ogs, ysp.reshape(tp, D_S5), yss.reshape(ts, D_S5), glu_w,
                                  vec(s5_glu_b[i]), vec(s5_norm[i]), wo, vec(norm_ffn[i]), wr)

    plan, dest, n_slots = _moe_plan(rt_all)
    y_slots = _moe(hn_all, plan, w_gate[i], w_up[i], w_down[i], n_slots)

    y_p, y_s = _ple_out(dest, h1, rt_all, p_prompt[i].reshape(tp, D_PLE), p_sample[i].reshape(ts, D_PLE),
                        vec(norm_ple[i]), wpg, wp, vec(norm_final), y_slots)

    s5shape = lambda a, b: a.reshape(1, b, S5_GROUPS, S5_STATE)
    return (y_p.reshape(bp, lp, D_MODEL), y_s.reshape(bs, ls, D_MODEL),
            gla_p[None], s5shape(re_p, bp), s5shape(im_p, bp),
            gla_s[None], s5shape(re_s, bs), s5shape(im_s, bs))
```

```python
import functools
import math

import jax
import jax.numpy as jnp
from jax import lax
from jax.experimental import pallas as pl
from jax.experimental.pallas import tpu as pltpu

F32 = jnp.float32
BF16 = jnp.bfloat16

D_MODEL = 2048
D_GLA = 1024
D_S5 = 1024
GLA_HEADS = 4
GLA_DV = 256
GLA_DK = 128
GLA_RANK = 16
GLA_CHUNK = 64
S5_GROUP = 16
S5_GROUPS = 64
S5_STATE = 64
N_EGROUPS = 4
N_EPG = 8
N_EXPERTS = 32
D_EXPERT = 512
D_PLE = 256
EPS = 1e-6

LANES = 128
QK_W = GLA_HEADS * GLA_DK
S5_GB = 8
S5_NGB = S5_GROUPS // S5_GB
S5_SL = S5_GB * S5_STATE
TOK_TILE = 256
MOE_BLK = 256
VMEM_LIMIT = 56 * 1024 * 1024
MOE_VMEM_LIMIT = 60 * 1024 * 1024


def _const_spec(shape):
    nd = len(shape)
    return pl.BlockSpec(shape, lambda *_: (0,) * nd, pipeline_mode=pl.Buffered(1))


def _rms(x, g):
    return x * lax.rsqrt(jnp.mean(x * x, axis=-1, keepdims=True) + EPS) * g


def _dot(a, b):
    return jnp.dot(a, b, preferred_element_type=F32)


def _log_sigmoid(x):
    return -(jnp.maximum(-x, 0.0) + jnp.log1p(jnp.exp(-jnp.abs(x))))


def _in_proj_kernel(x_ref, g_ref, wm_ref, wg_ref, wgu_ref, gb_ref,
                    q_ref, k_ref, v_ref, r_ref, la_ref, u_ref):
    hb = _rms(x_ref[...], g_ref[...]).astype(BF16)

    def seg(a, b):
        return _dot(hb, wm_ref[:, a:b])

    q_ref[...] = (seg(0, QK_W) * (GLA_DK ** -0.5)).astype(q_ref.dtype)
    k_ref[...] = seg(QK_W, 2 * QK_W).astype(k_ref.dtype)
    v_ref[...] = seg(2 * QK_W, 2 * QK_W + D_GLA).astype(v_ref.dtype)
    r_ref[...] = seg(2 * QK_W + D_GLA, 2 * QK_W + 2 * D_GLA).astype(r_ref.dtype)
    u_ref[...] = seg(2 * QK_W + 2 * D_GLA, 2 * QK_W + 2 * D_GLA + D_S5)
    zg = _dot(hb, wg_ref[...])
    xg = _dot(zg.astype(BF16), wgu_ref[...]) + gb_ref[...]
    la_ref[...] = _log_sigmoid(xg) * (1.0 / 16.0)


def _in_proj(x2d, g, wm, wg, wgu, gbias, act_dtype):
    t = x2d.shape[0]
    tm = TOK_TILE
    row = lambda w: pl.BlockSpec((tm, w), lambda i: (i, 0))
    return pl.pallas_call(
        _in_proj_kernel,
        grid=(t // tm,),
        in_specs=[row(D_MODEL), _const_spec((1, D_MODEL)), _const_spec(wm.shape), _const_spec(wg.shape),
                  _const_spec(wgu.shape), _const_spec((1, QK_W))],
        out_specs=[row(QK_W), row(QK_W), row(D_GLA), row(D_GLA), row(QK_W), row(D_S5)],
        out_shape=[jax.ShapeDtypeStruct((t, QK_W), act_dtype), jax.ShapeDtypeStruct((t, QK_W), act_dtype),
                   jax.ShapeDtypeStruct((t, D_GLA), act_dtype), jax.ShapeDtypeStruct((t, D_GLA), act_dtype),
                   jax.ShapeDtypeStruct((t, QK_W), F32), jax.ShapeDtypeStruct((t, D_S5), F32)],
        compiler_params=pltpu.CompilerParams(dimension_semantics=("arbitrary",), vmem_limit_bytes=VMEM_LIMIT),
        name="in_proj",
    )(x2d, g, wm, wg, wgu, gbias)


_NT = (((1,), (1,)), ((), ()))
_TN = (((0,), (0,)), ((), ()))


def _gla_pre(q, k, la, c):
    r = q.shape[0]
    shift = int(math.log2(c))
    ri = lax.broadcasted_iota(jnp.int32, (r, r), 0)
    si = lax.broadcasted_iota(jnp.int32, (r, r), 1)
    mask = ((ri >> shift) == (si >> shift)) & (ri >= si)
    tri = jnp.where(mask, 1.0, 0.0).astype(BF16)
    hi = la.astype(BF16)
    r1 = la - hi.astype(F32)
    mid = r1.astype(BF16)
    lo = (r1 - mid.astype(F32)).astype(BF16)
    cum = _dot(tri, hi) + _dot(tri, mid) + _dot(tri, lo)
    last = jnp.concatenate([jnp.broadcast_to(cum[(i + 1) * c - 1:(i + 1) * c, :], (c, cum.shape[1]))
                            for i in range(r // c)], axis=0)
    qe = (q * jnp.exp(cum)).astype(BF16)
    ke = (k * jnp.exp(-cum)).astype(BF16)
    kd = (k * jnp.exp(last - cum)).astype(BF16)
    return qe, ke, kd, cum, mask


def _gla_intra(qe, ke, v, mask):
    sc = lax.dot_general(qe, ke, _NT, preferred_element_type=F32)
    return _dot(jnp.where(mask, sc, 0.0).astype(BF16), v)


def _gla_finish(o, r, g):
    rf = r.astype(F32)
    return _rms(o, g) * (rf * jax.nn.sigmoid(rf))


GLA_SAFE_LOG_DECAY = -60.0


def _gla_token_step(t, rows, q, k, v, a_all, st_ref, oacc_ref):
    m = rows == t
    a = jnp.sum(jnp.where(m, a_all, 0.0), axis=0, keepdims=True)
    kt = jnp.where(m, k, 0.0).astype(BF16)
    qt = jnp.where(m, q, 0.0).astype(BF16)
    vt = jnp.where(m, v, jnp.zeros_like(v))
    st = st_ref[...] * a + lax.dot_general(vt, kt, _TN, preferred_element_type=F32)
    st_ref[...] = st
    oacc_ref[...] += lax.dot_general(qt, st.astype(BF16), _NT, preferred_element_type=F32)


def _gla_prompt_kernel(q_ref, k_ref, v_ref, r_ref, la_ref, g_ref, o_ref, sfin_ref, st_ref, oacc_ref,
                       *, n_chunks):
    j = pl.program_id(1)

    @pl.when(j == 0)
    def _():
        st_ref[...] = jnp.zeros_like(st_ref)

    c = GLA_CHUNK
    qe, ke, kd, cum, mask = _gla_pre(q_ref[...].astype(F32), k_ref[...].astype(F32), la_ref[...], c)
    g = g_ref[...]
    safe = jnp.min(cum) >= GLA_SAFE_LOG_DECAY

    @pl.when(safe)
    def _():
        for h in range(GLA_HEADS):
            kc = slice(h * GLA_DK, (h + 1) * GLA_DK)
            vc = slice(h * GLA_DV, (h + 1) * GLA_DV)
            v = v_ref[:, vc]
            qe_h, kd_h = qe[:, kc], kd[:, kc]
            o = _gla_intra(qe_h, ke[:, kc], v, mask)
            st = st_ref[h]
            inter = []
            for ci in range(n_chunks):
                rows = slice(ci * c, (ci + 1) * c)
                inter.append(lax.dot_general(qe_h[rows], st.astype(BF16), _NT, preferred_element_type=F32))
                dec = jnp.exp(cum[(ci + 1) * c - 1:(ci + 1) * c, kc])
                st = st * dec + lax.dot_general(v[rows], kd_h[rows], _TN, preferred_element_type=F32)
            st_ref[h] = st
            o = o + jnp.concatenate(inter, axis=0)
            o_ref[:, vc] = _gla_finish(o, r_ref[:, vc], g).astype(o_ref.dtype)

    @pl.when(jnp.logical_not(safe))
    def _():
        n_rows = n_chunks * c
        rows = lax.broadcasted_iota(jnp.int32, (n_rows, 1), 0)
        for h in range(GLA_HEADS):
            kc = slice(h * GLA_DK, (h + 1) * GLA_DK)
            vc = slice(h * GLA_DV, (h + 1) * GLA_DV)
            q, k, v = q_ref[:, kc].astype(F32), k_ref[:, kc].astype(F32), v_ref[:, vc]
            a_all = jnp.exp(la_ref[:, kc])
            oacc_ref[...] = jnp.zeros_like(oacc_ref)

            def step(t, carry):
                _gla_token_step(t, rows, q, k, v, a_all, st_ref.at[h], oacc_ref)
                return carry

            lax.fori_loop(0, n_rows, step, 0)
            o_ref[:, vc] = _gla_finish(oacc_ref[...], r_ref[:, vc], g).astype(o_ref.dtype)

    @pl.when(j == pl.num_programs(1) - 1)
    def _():
        for h in range(GLA_HEADS):
            sfin_ref[0, h] = st_ref[h].T


def _gla_prompt(q, k, v, r, la, g, batch, seq):
    rb = 4 * GLA_CHUNK
    nj = seq // rb
    row = lambda w: pl.BlockSpec((rb, w), lambda b, j: (b * nj + j, 0))
    return pl.pallas_call(
        functools.partial(_gla_prompt_kernel, n_chunks=rb // GLA_CHUNK),
        grid=(batch, nj),
        in_specs=[row(QK_W), row(QK_W), row(D_GLA), row(D_GLA), row(QK_W), _const_spec((1, GLA_DV))],
        out_specs=[row(D_GLA),
                   pl.BlockSpec((1, GLA_HEADS, GLA_DK, GLA_DV), lambda b, j: (b, 0, 0, 0))],
        out_shape=[jax.ShapeDtypeStruct((batch * seq, D_GLA), BF16),
                   jax.ShapeDtypeStruct((batch, GLA_HEADS, GLA_DK, GLA_DV), F32)],
        scratch_shapes=[pltpu.VMEM((GLA_HEADS, GLA_DV, GLA_DK), F32), pltpu.VMEM((rb, GLA_DV), F32)],
        compiler_params=pltpu.CompilerParams(dimension_semantics=("arbitrary", "arbitrary"),
                                             vmem_limit_bytes=VMEM_LIMIT),
        name="gla_prompt",
    )(q, k, v, r, la, g)


def _gla_sample_kernel(q_ref, k_ref, v_ref, r_ref, la_ref, g_ref, s0_ref, o_ref, sfin_ref, st_ref, oacc_ref,
                       *, n_seq, seq):
    qe, ke, kd, cum, mask = _gla_pre(q_ref[...].astype(F32), k_ref[...].astype(F32), la_ref[...], seq)
    g = g_ref[...]
    safe = jnp.min(cum) >= GLA_SAFE_LOG_DECAY

    @pl.when(safe)
    def _():
        lasts = jnp.concatenate([cum[(s + 1) * seq - 1:(s + 1) * seq, :] for s in range(n_seq)]
                                + [jnp.zeros((GLA_DK - n_seq, cum.shape[1]), F32)], axis=0)
        pair = 2 * seq
        upper = lax.broadcasted_iota(jnp.int32, (pair, GLA_DK), 0) < seq
        for h in range(GLA_HEADS):
            kc = slice(h * GLA_DK, (h + 1) * GLA_DK)
            vc = slice(h * GLA_DV, (h + 1) * GLA_DV)
            v = v_ref[:, vc]
            qe_h, kd_h = qe[:, kc], kd[:, kc]
            o = _gla_intra(qe_h, ke[:, kc], v, mask)
            dec_t = jnp.exp(lasts[:, kc].T)
            inter = []
            for p in range(n_seq // 2):
                rows = slice(p * pair, (p + 1) * pair)
                qe_p, kd_p, v_p = qe_h[rows], kd_h[rows], v[rows]
                for half in range(2):
                    s = 2 * p + half
                    s0 = s0_ref[s, h]
                    o_s = _dot(qe_p, s0.astype(BF16))
                    inter.append(o_s[half * seq:(half + 1) * seq])
                    kd_s = jnp.where(upper if half == 0 else ~upper, kd_p, jnp.zeros_like(kd_p))
                    dec = jnp.broadcast_to(dec_t[:, s:s + 1], (GLA_DK, GLA_DV))
                    sfin_ref[s, h] = s0 * dec + lax.dot_general(kd_s, v_p, _TN, preferred_element_type=F32)
            o = o + jnp.concatenate(inter, axis=0)
            o_ref[:, vc] = _gla_finish(o, r_ref[:, vc], g).astype(o_ref.dtype)

    @pl.when(jnp.logical_not(safe))
    def _():
        n_rows = n_seq * seq
        rows = lax.broadcasted_iota(jnp.int32, (n_rows, 1), 0)
        for h in range(GLA_HEADS):
            kc = slice(h * GLA_DK, (h + 1) * GLA_DK)
            vc = slice(h * GLA_DV, (h + 1) * GLA_DV)
            q, k, v = q_ref[:, kc].astype(F32), k_ref[:, kc].astype(F32), v_ref[:, vc]
            a_all = jnp.exp(la_ref[:, kc])
            oacc_ref[...] = jnp.zeros_like(oacc_ref)

            def step(t, carry):
                s = t // seq

                @pl.when(t % seq == 0)
                def _():
                    st_ref[...] = s0_ref[s, h].T

                _gla_token_step(t, rows, q, k, v, a_all, st_ref, oacc_ref)

                @pl.when(t % seq == seq - 1)
                def _():
                    sfin_ref[s, h] = st_ref[...].T

                return carry

            lax.fori_loop(0, n_rows, step, 0)
            o_ref[:, vc] = _gla_finish(oacc_ref[...], r_ref[:, vc], g).astype(o_ref.dtype)


def _gla_sample(q, k, v, r, la, g, s0, batch, seq):
    ns = 16
    rb = ns * seq
    row = lambda w: pl.BlockSpec((rb, w), lambda i: (i, 0))
    st = pl.BlockSpec((ns, GLA_HEADS, GLA_DK, GLA_DV), lambda i: (i, 0, 0, 0))
    return pl.pallas_call(
        functools.partial(_gla_sample_kernel, n_seq=ns, seq=seq),
        grid=(batch // ns,),
        in_specs=[row(QK_W), row(QK_W), row(D_GLA), row(D_GLA), row(QK_W), _const_spec((1, GLA_DV)), st],
        out_specs=[row(D_GLA), st],
        out_shape=[jax.ShapeDtypeStruct((batch * seq, D_GLA), BF16),
                   jax.ShapeDtypeStruct((batch, GLA_HEADS, GLA_DK, GLA_DV), F32)],
        scratch_shapes=[pltpu.VMEM((GLA_DV, GLA_DK), F32), pltpu.VMEM((rb, GLA_DV), F32)],
        compiler_params=pltpu.CompilerParams(dimension_semantics=("arbitrary",), vmem_limit_bytes=VMEM_LIMIT),
        name="gla_sample",
    )(q, k, v, r, la, g, s0)


def _s5_kernel(u_ref, wbu_ref, wc_ref, are_ref, aim_ref, d_ref, h0r_ref, h0i_ref,
               y_ref, sre_ref, sim_ref, bu_ref, xs_ref, car_ref, *, nb, tc, has_state):
    j = pl.program_id(2)

    @pl.when(j == 0)
    def _():
        if has_state:
            car_ref[0] = h0r_ref[...]
            car_ref[1] = h0i_ref[...]
        else:
            car_ref[...] = jnp.zeros_like(car_ref)

    u2 = u_ref[...].reshape(nb * tc, LANES)
    ub = u2.astype(BF16)
    nl = S5_SL // LANES
    for l in range(2 * nl):
        bu_ref[l] = _dot(ub, wbu_ref[0, :, l * LANES:(l + 1) * LANES])
    a_r = [jnp.broadcast_to(are_ref[0, :, l * LANES:(l + 1) * LANES], (nb, LANES)) for l in range(nl)]
    a_i = [jnp.broadcast_to(aim_ref[0, :, l * LANES:(l + 1) * LANES], (nb, LANES)) for l in range(nl)]

    def step(t, carry):
        rows = pl.ds(t, nb, stride=tc)
        out = []
        for l in range(nl):
            xr, xi = carry[2 * l], carry[2 * l + 1]
            nr = a_r[l] * xr - a_i[l] * xi + bu_ref[l, rows, :]
            ni = a_r[l] * xi + a_i[l] * xr + bu_ref[nl + l, rows, :]
            xs_ref[l, rows, :] = nr
            xs_ref[nl + l, rows, :] = ni
            out += [nr, ni]
        return tuple(out)

    init = []
    for l in range(nl):
        init += [car_ref[0, :, l * LANES:(l + 1) * LANES], car_ref[1, :, l * LANES:(l + 1) * LANES]]
    fin = lax.fori_loop(0, tc, step, tuple(init), unroll=8)
    xr = jnp.concatenate([fin[2 * l] for l in range(nl)], axis=1)
    xi = jnp.concatenate([fin[2 * l + 1] for l in range(nl)], axis=1)
    car_ref[0] = xr
    car_ref[1] = xi
    y = d_ref[0] * u2
    for l in range(2 * nl):
        y = y + _dot(xs_ref[l].astype(BF16), wc_ref[0, l * LANES:(l + 1) * LANES, :])
    y_ref[...] = y.reshape(nb, tc, LANES)

    @pl.when(j == pl.num_programs(2) - 1)
    def _():
        sre_ref[...] = xr
        sim_ref[...] = xi


def _s5(u3d, wbu, wc, a_re, a_im, dsk, h0r, h0i, nb, tc, has_state):
    batch, seq, _ = u3d.shape
    grid = (S5_NGB, batch // nb, seq // tc)
    st = pl.BlockSpec((nb, S5_SL), lambda g, b, j: (b, g))
    par = lambda w: pl.BlockSpec((1, 1, w), lambda g, b, j: (g, 0, 0))
    ublk = pl.BlockSpec((nb, tc, LANES), lambda g, b, j: (b, j, g))
    return pl.pallas_call(
        functools.partial(_s5_kernel, nb=nb, tc=tc, has_state=has_state),
        grid=grid,
        in_specs=[ublk,
                  pl.BlockSpec((1, LANES, 2 * S5_SL), lambda g, b, j: (g, 0, 0)),
                  pl.BlockSpec((1, 2 * S5_SL, LANES), lambda g, b, j: (g, 0, 0)),
                  par(S5_SL), par(S5_SL), par(LANES), st, st],
        out_specs=[ublk, st, st],
        out_shape=[jax.ShapeDtypeStruct(u3d.shape, F32),
                   jax.ShapeDtypeStruct((batch, S5_GROUPS * S5_STATE), F32),
                   jax.ShapeDtypeStruct((batch, S5_GROUPS * S5_STATE), F32)],
        scratch_shapes=[pltpu.VMEM((2 * S5_SL // LANES, nb * tc, LANES), F32),
                        pltpu.VMEM((2 * S5_SL // LANES, nb * tc, LANES), F32),
                        pltpu.VMEM((2, nb, S5_SL), F32)],
        compiler_params=pltpu.CompilerParams(dimension_semantics=("arbitrary",) * 3,
                                             vmem_limit_bytes=VMEM_LIMIT),
        name="s5_state" if has_state else "s5_zero",
    )(u3d, wbu, wc, a_re, a_im, dsk, h0r, h0i)


S5_J = 2
S5_HL = S5_SL // S5_J


def _s5_prompt_kernel(u_ref, wbu_ref, wc_ref, are_ref, aim_ref, d_ref, y_ref, sre_ref, sim_ref,
                      u2_ref, lhs_ref, bu_ref, xs_ref, y2_ref, yo_ref, car_ref, *, nb, tc):
    g = pl.program_id(0)
    j = pl.program_id(1)
    rows = nb * S5_J

    @pl.when((g == 0) & (j == 0))
    def _():
        lhs_ref[...] = jnp.zeros_like(lhs_ref)

    @pl.when(j == 0)
    def _():
        car_ref[...] = jnp.zeros_like(car_ref)

    u2 = u_ref[...].reshape(nb * tc, LANES)
    u2_ref[...] = u2

    def build(t, c):
        u4 = u2_ref[pl.ds(t, nb, stride=tc), :]
        for jj in range(S5_J):
            lhs_ref[t, jj * nb:(jj + 1) * nb, jj * LANES:(jj + 1) * LANES] = u4
        return c

    lax.fori_loop(0, tc, build, 0, unroll=8)
    lhs = lhs_ref[...].reshape(tc * rows, S5_J * LANES).astype(BF16)
    bu_ref[...] = _dot(lhs, wbu_ref[0]).reshape(tc, rows, 2 * S5_HL)
    ar = are_ref[0]
    ai = aim_ref[0]

    def step(t, carry):
        xr, xi = carry
        tile = bu_ref[t]
        nr = ar * xr - ai * xi + tile[:, 0:S5_HL]
        ni = ar * xi + ai * xr + tile[:, S5_HL:2 * S5_HL]
        xs_ref[t] = jnp.concatenate([nr, ni], axis=1)
        return nr, ni

    xr, xi = lax.fori_loop(0, tc, step, (car_ref[0], car_ref[1]), unroll=8)
    car_ref[0] = xr
    car_ref[1] = xi
    xs = xs_ref[...].reshape(tc * rows, 2 * S5_HL).astype(BF16)
    y2_ref[...] = _dot(xs, wc_ref[0]).reshape(tc, rows, S5_J * LANES)
    first_half = lax.broadcasted_iota(jnp.int32, (rows, LANES), 0) < nb

    def unperm(t, c):
        t2 = y2_ref[t]
        part = jnp.where(first_half, t2[:, 0:LANES], t2[:, LANES:2 * LANES])
        tot = part + pltpu.roll(part, nb, axis=0)
        yo_ref[pl.ds(t, nb, stride=tc), :] = tot[0:nb]
        return c

    lax.fori_loop(0, tc, unperm, 0, unroll=8)
    y_ref[...] = (yo_ref[...] + d_ref[0] * u2).reshape(nb, tc, LANES)

    @pl.when(j == pl.num_programs(1) - 1)
    def _():
        sre_ref[...] = jnp.concatenate([xr[jj * nb:(jj + 1) * nb] for jj in range(S5_J)], axis=1)
        sim_ref[...] = jnp.concatenate([xi[jj * nb:(jj + 1) * nb] for jj in range(S5_J)], axis=1)


def _s5_prompt(u3d, wbu2, wc2, a_re2, a_im2, dsk, tc):
    nb, seq, _ = u3d.shape
    rows = nb * S5_J
    assert rows == 8
    st = pl.BlockSpec((nb, S5_SL), lambda g, j: (0, g))
    gblk = lambda s: pl.BlockSpec((1,) + s, lambda g, j: (g, 0, 0))
    ublk = pl.BlockSpec((nb, tc, LANES), lambda g, j: (0, j, g))
    return pl.pallas_call(
        functools.partial(_s5_prompt_kernel, nb=nb, tc=tc),
        grid=(S5_NGB, seq // tc),
        in_specs=[ublk, gblk((S5_J * LANES, 2 * S5_HL)), gblk((2 * S5_HL, S5_J * LANES)),
                  gblk((rows, S5_HL)), gblk((rows, S5_HL)), gblk((1, LANES))],
        out_specs=[ublk, st, st],
        out_shape=[jax.ShapeDtypeStruct(u3d.shape, F32),
                   jax.ShapeDtypeStruct((nb, S5_GROUPS * S5_STATE), F32),
                   jax.ShapeDtypeStruct((nb, S5_GROUPS * S5_STATE), F32)],
        scratch_shapes=[pltpu.VMEM((nb * tc, LANES), F32),
                        pltpu.VMEM((tc, rows, S5_J * LANES), F32),
                        pltpu.VMEM((tc, rows, 2 * S5_HL), F32),
                        pltpu.VMEM((tc, rows, 2 * S5_HL), F32),
                        pltpu.VMEM((tc, rows, S5_J * LANES), F32),
                        pltpu.VMEM((nb * tc, LANES), F32),
                        pltpu.VMEM((2, rows, S5_HL), F32)],
        compiler_params=pltpu.CompilerParams(dimension_semantics=("arbitrary",) * 2,
                                             vmem_limit_bytes=VMEM_LIMIT),
        name="s5_prompt",
    )(u3d, wbu2, wc2, a_re2, a_im2, dsk)


def _s5_prompt_params(wbu, wc, a_re, a_im, nb):
    h = S5_HL
    top = jnp.concatenate([wbu[:, :, 0:h], wbu[:, :, S5_SL:S5_SL + h]], axis=2)
    bot = jnp.concatenate([wbu[:, :, h:2 * h], wbu[:, :, S5_SL + h:S5_SL + 2 * h]], axis=2)
    wbu2 = jnp.concatenate([top, bot], axis=1)
    wc_j = [jnp.concatenate([wc[:, jj * h:(jj + 1) * h, :], wc[:, S5_SL + jj * h:S5_SL + (jj + 1) * h, :]],
                            axis=1) for jj in range(S5_J)]
    wc2 = jnp.concatenate(wc_j, axis=2)
    tile = lambda a: jnp.repeat(a.reshape(S5_NGB, S5_J, h), nb, axis=1)
    return wbu2, wc2, tile(a_re), tile(a_im)


def _mix_out_kernel(xp_ref, xs_ref, ogp_ref, ogs_ref, ysp_ref, yss_ref, glu_w_ref, glu_b_ref, s5n_ref, wo_ref,
                    nffn_ref, wr_ref, h1_ref, hn_ref, rt_ref, *, n_prompt_tiles):
    is_p = pl.program_id(0) < n_prompt_tiles
    x = jnp.where(is_p, xp_ref[...], xs_ref[...])
    og = jnp.where(is_p, ogp_ref[...], ogs_ref[...])
    y = jax.nn.gelu(jnp.where(is_p, ysp_ref[...], yss_ref[...]))
    y = y * jax.nn.sigmoid(_dot(y.astype(BF16), glu_w_ref[...]) + glu_b_ref[...])
    yn = _rms(y, s5n_ref[...]).astype(BF16)
    mix = _dot(og, wo_ref[0:D_GLA, :]) + _dot(yn, wo_ref[D_GLA:D_GLA + D_S5, :])
    h1 = x + mix
    h1_ref[...] = h1
    hn = _rms(h1, nffn_ref[...])
    hn_ref[...] = _pack_bf16_pair(hn[:, 0:HALF], hn[:, HALF:D_MODEL])
    hn_hi = hn.astype(BF16)
    hn_lo = (hn - hn_hi.astype(F32)).astype(BF16)
    logits = _dot(hn_hi, wr_ref[0]) + _dot(hn_hi, wr_ref[1]) + _dot(hn_lo, wr_ref[0])
    rt_ref[...] = _route(logits)


def _route(logits):
    col = lax.broadcasted_iota(jnp.int32, logits.shape, 1)
    colf = col.astype(F32)
    neg = -jnp.inf

    def first_argmax(vals):
        m = jnp.max(vals, axis=-1, keepdims=True)
        idx = jnp.min(jnp.where(vals == m, colf, float(LANES)), axis=-1, keepdims=True)
        return m, idx

    lg = jnp.where(col < N_EGROUPS, logits, neg)
    gmax, gsel = first_argmax(lg)
    p_g = 1.0 / jnp.sum(jnp.exp(lg - gmax), axis=-1, keepdims=True)
    ecol = col - N_EGROUPS
    egrp = (ecol >> 3).astype(F32)
    in_group = (ecol >= 0) & (ecol < N_EXPERTS) & (egrp == gsel)
    le = jnp.where(in_group, logits, neg)
    m1, i1 = first_argmax(le)
    le2 = jnp.where(colf == i1, neg, le)
    m2, i2 = first_argmax(le2)
    e2 = jnp.exp(m2 - m1)
    den = 1.0 + e2
    w1 = p_g * (1.0 / den)
    w2 = p_g * (e2 / den)
    e1f = i1 - float(N_EGROUPS)
    e2f = i2 - float(N_EGROUPS)
    out = jnp.where(col == 0, e1f, jnp.where(col == 1, e2f, jnp.where(col == 2, w1, jnp.where(col == 3, w2, 0.0))))
    return out


def _mix_out(xp, xs, ogp, ogs, ysp, yss, glu_w, glu_b, s5n, wo, nffn, wr):
    tm = TOK_TILE
    npt, nst = xp.shape[0] // tm, xs.shape[0] // tm
    t = (npt + nst) * tm
    row = lambda w: pl.BlockSpec((tm, w), lambda i: (i, 0))
    prow = lambda w: pl.BlockSpec((tm, w), lambda i: (jnp.minimum(i, npt - 1), 0))
    srow = lambda w: pl.BlockSpec((tm, w), lambda i: (jnp.maximum(i - npt, 0), 0))
    return pl.pallas_call(
        functools.partial(_mix_out_kernel, n_prompt_tiles=npt),
        grid=(npt + nst,),
        in_specs=[prow(D_MODEL), srow(D_MODEL), prow(D_GLA), srow(D_GLA), prow(D_S5), srow(D_S5),
                  _const_spec(glu_w.shape), _const_spec((1, D_S5)), _const_spec((1, D_S5)),
                  _const_spec(wo.shape), _const_spec((1, D_MODEL)), _const_spec(wr.shape)],
        out_specs=[row(D_MODEL), row(HALF), row(LANES)],
        out_shape=[jax.ShapeDtypeStruct((t, D_MODEL), F32), jax.ShapeDtypeStruct((t, HALF), jnp.uint32),
                   jax.ShapeDtypeStruct((t, LANES), F32)],
        compiler_params=pltpu.CompilerParams(dimension_semantics=("arbitrary",), vmem_limit_bytes=VMEM_LIMIT),
        name="mix_out",
    )(xp, xs, ogp, ogs, ysp, yss, glu_w, glu_b, s5n, wo, nffn, wr)


DMA_UNROLL = 8


MOE_SB = 3
HALF = D_MODEL // 2


def _pack_bf16_pair(lo, hi):
    def bits(x):
        b = pltpu.bitcast(x, jnp.uint32)
        return (b + jnp.uint32(0x7FFF) + ((b >> 16) & jnp.uint32(1))) >> 16
    return bits(lo) | (bits(hi) << 16)


def _unpack_bf16_pair(u):
    return (pltpu.bitcast(u << 16, F32), pltpu.bitcast(u & jnp.uint32(0xFFFF0000), F32))


def _moe_kernel(se_ref, sm_ref, sn_ref, sbase_ref, sa_ref, hn_hbm, wg_ref, wu_ref, wd_ref, out_hbm,
                xb_ref, yb_ref, wgb_ref, wub_ref, wdb_ref, gsem, ssem):
    b = pl.program_id(0)
    nb = pl.num_programs(0)
    n = sn_ref[b]
    slot = lax.rem(b, 2)

    def round_up(cnt):
        return (cnt + DMA_UNROLL - 1) // DMA_UNROLL * DMA_UNROLL

    def issue_gather(blk, sl):
        base = sbase_ref[blk]

        def grp(g, c):
            for j in range(DMA_UNROLL):
                i = g * DMA_UNROLL + j
                tok = sa_ref[base + i] >> 1
                pltpu.make_async_copy(hn_hbm.at[pl.ds(tok, 1), :], xb_ref.at[sl, pl.ds(i, 1), :],
                                      gsem.at[sl]).start()
            return c

        lax.fori_loop(0, round_up(sn_ref[blk]) // DMA_UNROLL, grp, 0)

    def wait_gather(cnt, sl):
        rows = pl.ds(0, pl.multiple_of(round_up(cnt), DMA_UNROLL))
        pltpu.make_async_copy(hn_hbm.at[rows, :], xb_ref.at[sl, rows, :], gsem.at[sl]).wait()

    m_cur = sm_ref[b]
    m_prev = sm_ref[jnp.maximum(b - 1, 0)]
    has_prev = (b > 0) & (m_prev > 0)

    def out_copy(rows):
        dst = pl.ds(pl.multiple_of(sbase_ref[b], MOE_BLK), rows)
        return pltpu.make_async_copy(yb_ref.at[pl.ds(0, rows), :], out_hbm.at[dst, :], ssem.at[0])

    def wait_prev_out():
        rows = pl.ds(0, pl.multiple_of(m_prev * MOE_BLK, MOE_BLK))
        pltpu.make_async_copy(yb_ref.at[rows, :], out_hbm.at[rows, :], ssem.at[0]).wait()

    def write_out(rows):
        out_copy(rows).start()

        @pl.when(b == nb - 1)
        def _():
            out_copy(rows).wait()

    @pl.when(b == 0)
    def _():
        xb_ref[...] = jnp.zeros_like(xb_ref)
        issue_gather(0, 0)

    @pl.when(b + 1 < nb)
    def _():
        issue_gather(b + 1, 1 - slot)

    def compute(rows):
        x_lo, x_hi = _unpack_bf16_pair(xb_ref[slot, 0:rows, :])
        x_lo, x_hi = x_lo.astype(BF16), x_hi.astype(BF16)
        gate = _dot(x_lo, wgb_ref[0:HALF, :]) + _dot(x_hi, wgb_ref[HALF:D_MODEL, :])
        up = _dot(x_lo, wub_ref[0:HALF, :]) + _dot(x_hi, wub_ref[HALF:D_MODEL, :])
        hid = (gate * jax.nn.sigmoid(gate) * up).astype(BF16)

        @pl.when(has_prev)
        def _():
            wait_prev_out()

        yb_ref[0:rows, :] = _pack_bf16_pair(_dot(hid, wdb_ref[:, 0:HALF]), _dot(hid, wdb_ref[:, HALF:D_MODEL]))
        write_out(rows)

    @pl.when(n > 0)
    def _():
        prev_e = se_ref[jnp.maximum(b - 1, 0)]

        @pl.when((b == 0) | (prev_e != se_ref[b]))
        def _():
            wgb_ref[...] = wg_ref[...].astype(BF16)
            wub_ref[...] = wu_ref[...].astype(BF16)
            wdb_ref[...] = wd_ref[...].astype(BF16)

        wait_gather(n, slot)
        for m in range(1, MOE_SB + 1):
            @pl.when(m_cur == m)
            def _():
                compute(m * MOE_BLK)

    @pl.when((n == 0) & (m_cur > 0))
    def _():
        @pl.when(has_prev)
        def _():
            wait_prev_out()

        yb_ref[...] = jnp.zeros_like(yb_ref)
        for m in range(1, MOE_SB + 1):
            @pl.when(m_cur == m)
            def _():
                write_out(m * MOE_BLK)

    @pl.when((m_cur == 0) & has_prev)
    def _():
        wait_prev_out()


def _moe(hn_pk, plan, w_gate, w_up, w_down, n_out_rows):
    sb_e, sb_m, sb_n, sb_base, slot_a = plan
    sb_rows = MOE_SB * MOE_BLK
    wspec = lambda s: pl.BlockSpec((None,) + s, lambda b, se, *_: (se[b], 0, 0))
    grid_spec = pltpu.PrefetchScalarGridSpec(
        num_scalar_prefetch=5,
        grid=(sb_e.shape[0],),
        in_specs=[pl.BlockSpec(memory_space=pl.ANY),
                  wspec((D_MODEL, D_EXPERT)), wspec((D_MODEL, D_EXPERT)), wspec((D_EXPERT, D_MODEL))],
        out_specs=pl.BlockSpec(memory_space=pl.ANY),
        scratch_shapes=[pltpu.VMEM((2, sb_rows, HALF), jnp.uint32), pltpu.VMEM((sb_rows, HALF), jnp.uint32),
                        pltpu.VMEM((D_MODEL, D_EXPERT), BF16), pltpu.VMEM((D_MODEL, D_EXPERT), BF16),
                        pltpu.VMEM((D_EXPERT, D_MODEL), BF16), pltpu.SemaphoreType.DMA((2,)),
                        pltpu.SemaphoreType.DMA((1,))],
    )
    return pl.pallas_call(
        _moe_kernel,
        grid_spec=grid_spec,
        out_shape=jax.ShapeDtypeStruct((n_out_rows, HALF), jnp.uint32),
        compiler_params=pltpu.CompilerParams(dimension_semantics=("arbitrary",),
                                             vmem_limit_bytes=MOE_VMEM_LIMIT),
        name="moe_experts",
    )(sb_e, sb_m, sb_n, sb_base, slot_a, hn_pk, w_gate, w_up, w_down)


def _moe_plan(rt_all):
    t_all = rt_all.shape[0]
    n_assign = 2 * t_all
    e_flat = rt_all[:, 0:2].astype(jnp.int32).reshape(-1)
    onehot = (e_flat[:, None] == jnp.arange(N_EXPERTS, dtype=jnp.int32)[None, :]).astype(jnp.int32)
    csum = jnp.cumsum(onehot, axis=0)
    counts = csum[-1]
    rank = jnp.take_along_axis(csum, e_flat[:, None], axis=1)[:, 0] - 1
    padded = (counts + MOE_BLK - 1) // MOE_BLK * MOE_BLK
    pad_end = jnp.cumsum(padded)
    pad_start = pad_end - padded
    dest = pad_start[e_flat] + rank
    nblk = -(-(n_assign + N_EXPERTS * (MOE_BLK - 1)) // MOE_BLK)
    n_slots = nblk * MOE_BLK
    slot_a = jnp.zeros((n_slots,), jnp.int32).at[dest].set(jnp.arange(n_assign, dtype=jnp.int32))
    k_e = padded // MOE_BLK
    sbc = (k_e + MOE_SB - 1) // MOE_SB
    sb_end = jnp.cumsum(sbc)
    sb_start = sb_end - sbc
    n_sb = (nblk + (MOE_SB - 1) * N_EXPERTS) // MOE_SB
    s = jnp.arange(n_sb, dtype=jnp.int32)
    sb_e = jnp.minimum(jnp.sum((s[:, None] >= sb_end[None, :]).astype(jnp.int32), axis=1), N_EXPERTS - 1)
    j = s - sb_start[sb_e]
    real = s < sb_end[-1]
    tail_blk = pad_end[-1] // MOE_BLK + MOE_SB * (s - sb_end[-1])
    sb_m = jnp.where(real, jnp.clip(k_e[sb_e] - MOE_SB * j, 0, MOE_SB), jnp.clip(nblk - tail_blk, 0, MOE_SB))
    sb_n = jnp.where(real, jnp.clip(counts[sb_e] - MOE_SB * MOE_BLK * j, 0, MOE_SB * MOE_BLK), 0)
    sb_base = jnp.where(real, pad_start[sb_e] + MOE_SB * MOE_BLK * j, jnp.minimum(tail_blk, nblk - 1) * MOE_BLK)
    i32 = lambda a: a.astype(jnp.int32)
    return (i32(sb_e), i32(sb_m), i32(sb_n), i32(sb_base), slot_a), i32(dest), n_slots


def _ple_out_kernel(dest_ref, h1_ref, rt_ref, pp_ref, ps_ref, nple_ref, wpg_ref, wp_ref, nfin_ref, ys_hbm,
                    op_ref, os_ref, yb_ref, sem, *, n_prompt_tiles):
    i = pl.program_id(0)
    nt = pl.num_programs(0)
    tm = h1_ref.shape[0]
    slot = lax.rem(i, 2)

    def issue(tile, sl):
        base = tile * (2 * tm)

        def grp(g, c):
            for j in range(DMA_UNROLL):
                d = dest_ref[base + g * DMA_UNROLL + j]
                row = g * (DMA_UNROLL // 2) + j // 2
                pltpu.make_async_copy(ys_hbm.at[pl.ds(d, 1), :], yb_ref.at[sl, j % 2, pl.ds(row, 1), :],
                                      sem.at[sl]).start()
            return c

        lax.fori_loop(0, 2 * tm // DMA_UNROLL, grp, 0)

    @pl.when(i == 0)
    def _():
        issue(0, 0)

    def wait_rows(sl):
        for k in range(2):
            pltpu.make_async_copy(ys_hbm.at[pl.ds(0, tm), :], yb_ref.at[sl, k], sem.at[sl]).wait()

    wait_rows(slot)
    is_p = i < n_prompt_tiles
    rt = rt_ref[...]
    w0, w1 = rt[:, 2:3], rt[:, 3:4]
    lo0, hi0 = _unpack_bf16_pair(yb_ref[slot, 0])
    lo1, hi1 = _unpack_bf16_pair(yb_ref[slot, 1])
    h2 = h1_ref[...] + jnp.concatenate([w0 * lo0 + w1 * lo1, w0 * hi0 + w1 * hi1], axis=1)
    hb = _rms(h2, nple_ref[...]).astype(BF16)
    nbase = jnp.minimum(i + 1, nt - 1) * (2 * tm)
    n_chunk = 8
    cw, ca = D_MODEL // n_chunk, 2 * tm // n_chunk
    gates = []
    for c in range(n_chunk):
        gates.append(jax.nn.sigmoid(_dot(hb, wpg_ref[:, c * cw:(c + 1) * cw])))
        for a in range(c * ca, (c + 1) * ca):
            d = dest_ref[nbase + a]
            pltpu.make_async_copy(ys_hbm.at[pl.ds(d, 1), :], yb_ref.at[1 - slot, a % 2, pl.ds(a // 2, 1), :],
                                  sem.at[1 - slot]).start()
    gate = jnp.concatenate(gates, axis=1)

    @pl.when(i == nt - 1)
    def _():
        wait_rows(1 - slot)

    p = jnp.where(is_p, pp_ref[...], ps_ref[...])
    h3 = h2 + _dot(p.astype(BF16), wp_ref[...]) * gate
    y = _rms(h3, nfin_ref[...])

    @pl.when(is_p)
    def _():
        op_ref[...] = y

    @pl.when(jnp.logical_not(is_p))
    def _():
        os_ref[...] = y


def _ple_out(dest, h1, rt, pp, ps, nple, wpg, wp, nfin, y_slots):
    tm = TOK_TILE
    npt, nst = pp.shape[0] // tm, ps.shape[0] // tm
    row = lambda w: pl.BlockSpec((tm, w), lambda i, d: (i, 0))
    prow = lambda w: pl.BlockSpec((tm, w), lambda i, d: (jnp.minimum(i, npt - 1), 0))
    srow = lambda w: pl.BlockSpec((tm, w), lambda i, d: (jnp.maximum(i - npt, 0), 0))
    const = lambda shape: pl.BlockSpec(shape, lambda i, d: (0,) * len(shape), pipeline_mode=pl.Buffered(1))
    grid_spec = pltpu.PrefetchScalarGridSpec(
        num_scalar_prefetch=1,
        grid=(npt + nst,),
        in_specs=[row(D_MODEL), row(LANES), prow(D_PLE), srow(D_PLE), const((1, D_MODEL)), const(wpg.shape),
                  const(wp.shape), const((1, D_MODEL)), pl.BlockSpec(memory_space=pl.ANY)],
        out_specs=[prow(D_MODEL), srow(D_MODEL)],
        scratch_shapes=[pltpu.VMEM((2, 2, tm, HALF), jnp.uint32), pltpu.SemaphoreType.DMA((2,))],
    )
    return pl.pallas_call(
        functools.partial(_ple_out_kernel, n_prompt_tiles=npt),
        grid_spec=grid_spec,
        out_shape=[jax.ShapeDtypeStruct((pp.shape[0], D_MODEL), F32),
                   jax.ShapeDtypeStruct((ps.shape[0], D_MODEL), F32)],
        compiler_params=pltpu.CompilerParams(dimension_semantics=("arbitrary",), vmem_limit_bytes=VMEM_LIMIT),
        name="ple_out",
    )(dest, h1, rt, pp, ps, nple, wpg, wp, nfin, y_slots)


def _s5_params(lam_re, lam_im, log_dt, b_re, b_im, c_re, c_im, d_skip):
    dt = jnp.exp(log_dt)[:, None]
    mag = jnp.exp(lam_re * dt)
    ab_re = mag * jnp.cos(lam_im * dt)
    ab_im = mag * jnp.sin(lam_im * dt)
    den = lam_re * lam_re + lam_im * lam_im
    nr = ab_re - 1.0
    f_re = (nr * lam_re + ab_im * lam_im) / den
    f_im = (ab_im * lam_re - nr * lam_im) / den
    bb_re = f_re[..., None] * b_re - f_im[..., None] * b_im
    bb_im = f_re[..., None] * b_im + f_im[..., None] * b_re
    eye = jnp.eye(S5_GB, dtype=F32)

    def bu_w(bb):
        bb = bb.reshape(S5_NGB, S5_GB, S5_STATE, S5_GROUP)
        w = jnp.einsum('nlph,lm->nlhmp', bb, eye)
        return w.reshape(S5_NGB, S5_GB * S5_GROUP, S5_SL)

    def c_w(c):
        c = c.reshape(S5_NGB, S5_GB, S5_GROUP, S5_STATE)
        w = jnp.einsum('nlhp,lm->nlpmh', c, eye)
        return w.reshape(S5_NGB, S5_SL, S5_GB * S5_GROUP)

    wbu = jnp.concatenate([bu_w(bb_re), bu_w(bb_im)], axis=2).astype(BF16)
    wc = jnp.concatenate([c_w(c_re), -c_w(c_im)], axis=1).astype(BF16)
    a_re = ab_re.reshape(S5_NGB, 1, S5_SL)
    a_im = ab_im.reshape(S5_NGB, 1, S5_SL)
    dsk = d_skip.reshape(S5_NGB, 1, S5_GB * S5_GROUP)
    return wbu, wc, a_re, a_im, dsk


def kernel(x_prompt, x_sample, p_prompt, p_sample, state_gla, state_s5_re, state_s5_im, norm_mix, w_in, gla_w_gate_up, gla_gate_bias, gla_norm, s5_lam_re, s5_lam_im, s5_log_dt, s5_b_re, s5_b_im, s5_c_re, s5_c_im, s5_d, s5_glu_w, s5_glu_b, s5_norm, w_out, norm_ffn, router_group, router_expert, w_gate, w_up, w_down, norm_ple, w_ple, w_ple_gate, norm_final):
    depth = w_in.shape[0]
    assert depth == 1
    i = 0
    bp, lp, _ = x_prompt.shape
    bs, ls, _ = x_sample.shape
    tp, ts = bp * lp, bs * ls
    t_all = tp + ts

    n_qkvr = 2 * QK_W + 2 * D_GLA
    wi = w_in[i]
    wm = jnp.concatenate([wi[:, :n_qkvr], wi[:, n_qkvr + GLA_RANK:]], axis=1).astype(BF16)
    wg = jnp.pad(wi[:, n_qkvr:n_qkvr + GLA_RANK], ((0, 0), (0, LANES - GLA_RANK))).astype(BF16)
    wgu = jnp.pad(gla_w_gate_up[i], ((0, LANES - GLA_RANK), (0, 0))).astype(BF16)
    gbias = gla_gate_bias[i].reshape(1, QK_W)
    wbu, wc, a_re, a_im, dsk = _s5_params(s5_lam_re[i], s5_lam_im[i], s5_log_dt[i], s5_b_re[i], s5_b_im[i],
                                          s5_c_re[i], s5_c_im[i], s5_d[i])
    glu_w = s5_glu_w[i].astype(BF16)
    wo = w_out[i].astype(BF16)
    wr32 = jnp.pad(jnp.concatenate([router_group[i], router_expert[i]], axis=1),
                   ((0, 0), (0, LANES - N_EGROUPS - N_EXPERTS)))
    wr_hi = wr32.astype(BF16)
    wr = jnp.stack([wr_hi, (wr32 - wr_hi.astype(F32)).astype(BF16)])
    wpg = w_ple_gate[i].astype(BF16)
    wp = w_ple[i].astype(BF16)
    vec = lambda a: a.reshape(1, -1)

    xp = x_prompt.reshape(tp, D_MODEL)
    xs = x_sample.reshape(ts, D_MODEL)

    qp, kp, vp, rp, lap, up = _in_proj(xp, vec(norm_mix[i]), wm, wg, wgu, gbias, BF16)
    qs, ks, vs, rs, las, us = _in_proj(xs, vec(norm_mix[i]), wm, wg, wgu, gbias, BF16)
    ogp, gla_p = _gla_prompt(qp, kp, vp, rp, lap, vec(gla_norm[i]), bp, lp)
    ogs, gla_s = _gla_sample(qs, ks, vs, rs, las, vec(gla_norm[i]), state_gla[i], bs, ls)
    wbu2, wc2, a_re2, a_im2 = _s5_prompt_params(wbu, wc, a_re, a_im, bp)
    ysp, re_p, im_p = _s5_prompt(up.reshape(bp, lp, D_S5), wbu2, wc2, a_re2, a_im2, dsk, tc=256)
    yss, re_s, im_s = _s5(us.reshape(bs, ls, D_S5), wbu, wc, a_re, a_im, dsk,
                          state_s5_re[i].reshape(bs, -1), state_s5_im[i].reshape(bs, -1),
                          nb=32, tc=ls, has_state=True)

    h1, hn_all, rt_all = _mix_out(xp, xs, ogp, ogs, ysp.reshape(tp, D_S5), yss.reshape(ts, D_S5), glu_w,
                                  vec(s5_glu_b[i]), vec(s5_norm[i]), wo, vec(norm_ffn[i]), wr)

    plan, dest, n_slots = _moe_plan(rt_all)
    y_slots = _moe(hn_all, plan, w_gate[i], w_up[i], w_down[i], n_slots)

    y_p, y_s = _ple_out(dest, h1, rt_all, p_prompt[i].reshape(tp, D_PLE), p_sample[i].reshape(ts, D_PLE),
                        vec(norm_ple[i]), wpg, wp, vec(norm_final), y_slots)

    s5shape = lambda a, b: a.reshape(1, b, S5_GROUPS, S5_STATE)
    return (y_p.reshape(bp, lp, D_MODEL), y_s.reshape(bs, ls, D_MODEL),
            gla_p[None], s5shape(re_p, bp), s5shape(im_p, bp),
            gla_s[None], s5shape(re_s, bs), s5shape(im_s, bs))
```

```python
import functools
import math

import jax
import jax.numpy as jnp
from jax import lax
from jax.experimental import pallas as pl
from jax.experimental.pallas import tpu as pltpu

F32 = jnp.float32
BF16 = jnp.bfloat16

D_MODEL = 2048
D_GLA = 1024
D_S5 = 1024
GLA_HEADS = 4
GLA_DV = 256
GLA_DK = 128
GLA_RANK = 16
GLA_CHUNK = 64
S5_GROUP = 16
S5_GROUPS = 64
S5_STATE = 64
N_EGROUPS = 4
N_EPG = 8
N_EXPERTS = 32
D_EXPERT = 512
D_PLE = 256
EPS = 1e-6

LANES = 128
QK_W = GLA_HEADS * GLA_DK
S5_GB = 8
S5_NGB = S5_GROUPS // S5_GB
S5_SL = S5_GB * S5_STATE
TOK_TILE = 256
MOE_BLK = 256
VMEM_LIMIT = 56 * 1024 * 1024
MOE_VMEM_LIMIT = 60 * 1024 * 1024


def _const_spec(shape):
    nd = len(shape)
    return pl.BlockSpec(shape, lambda *_: (0,) * nd, pipeline_mode=pl.Buffered(1))


def _rms(x, g):
    return x * lax.rsqrt(jnp.mean(x * x, axis=-1, keepdims=True) + EPS) * g


def _dot(a, b):
    return jnp.dot(a, b, preferred_element_type=F32)


def _log_sigmoid(x):
    return -(jnp.maximum(-x, 0.0) + jnp.log1p(jnp.exp(-jnp.abs(x))))


N_QKVR = 2 * QK_W + 2 * D_GLA
W_IN_COLS = N_QKVR + GLA_RANK + D_S5


def _w_in_prep_kernel(w_ref, o_ref):
    o_ref[:, 0:N_QKVR] = w_ref[:, 0:N_QKVR].astype(BF16)
    tail = w_ref[:, N_QKVR:W_IN_COLS]
    o_ref[:, N_QKVR:N_QKVR + D_S5] = tail[:, GLA_RANK:GLA_RANK + D_S5].astype(BF16)
    o_ref[:, N_QKVR + D_S5:N_QKVR + D_S5 + LANES] = tail[:, 0:LANES].astype(BF16)


def _w_in_prep(wi):
    rows = 256
    return pl.pallas_call(
        _w_in_prep_kernel,
        grid=(D_MODEL // rows,),
        in_specs=[pl.BlockSpec((rows, W_IN_COLS), lambda i: (i, 0))],
        out_specs=pl.BlockSpec((rows, N_QKVR + D_S5 + LANES), lambda i: (i, 0)),
        out_shape=jax.ShapeDtypeStruct((D_MODEL, N_QKVR + D_S5 + LANES), BF16),
        compiler_params=pltpu.CompilerParams(dimension_semantics=("arbitrary",), vmem_limit_bytes=VMEM_LIMIT),
        name="w_in_prep",
    )(wi)


def _in_proj_kernel(x_ref, g_ref, wm_ref, wgu_ref, gb_ref,
                    q_ref, k_ref, v_ref, r_ref, la_ref, u_ref):
    hb = _rms(x_ref[...], g_ref[...]).astype(BF16)

    def seg(a, b):
        return _dot(hb, wm_ref[:, a:b])

    q_ref[...] = (seg(0, QK_W) * (GLA_DK ** -0.5)).astype(q_ref.dtype)
    k_ref[...] = seg(QK_W, 2 * QK_W).astype(k_ref.dtype)
    v_ref[...] = seg(2 * QK_W, 2 * QK_W + D_GLA).astype(v_ref.dtype)
    r_ref[...] = seg(2 * QK_W + D_GLA, N_QKVR).astype(r_ref.dtype)
    u_ref[...] = seg(N_QKVR, N_QKVR + D_S5)
    zg = seg(N_QKVR + D_S5, N_QKVR + D_S5 + LANES)
    xg = _dot(zg.astype(BF16), wgu_ref[...]) + gb_ref[...]
    la_ref[...] = _log_sigmoid(xg) * (1.0 / 16.0)


def _in_proj(x2d, g, wm, wgu, gbias, act_dtype):
    t = x2d.shape[0]
    tm = TOK_TILE
    row = lambda w: pl.BlockSpec((tm, w), lambda i: (i, 0))
    return pl.pallas_call(
        _in_proj_kernel,
        grid=(t // tm,),
        in_specs=[row(D_MODEL), _const_spec((1, D_MODEL)), _const_spec(wm.shape),
                  _const_spec(wgu.shape), _const_spec((1, QK_W))],
        out_specs=[row(QK_W), row(QK_W), row(D_GLA), row(D_GLA), row(QK_W), row(D_S5)],
        out_shape=[jax.ShapeDtypeStruct((t, QK_W), act_dtype), jax.ShapeDtypeStruct((t, QK_W), act_dtype),
                   jax.ShapeDtypeStruct((t, D_GLA), act_dtype), jax.ShapeDtypeStruct((t, D_GLA), act_dtype),
                   jax.ShapeDtypeStruct((t, QK_W), F32), jax.ShapeDtypeStruct((t, D_S5), F32)],
        compiler_params=pltpu.CompilerParams(dimension_semantics=("arbitrary",), vmem_limit_bytes=VMEM_LIMIT),
        name="in_proj",
    )(x2d, g, wm, wgu, gbias)


_NT = (((1,), (1,)), ((), ()))
_TN = (((0,), (0,)), ((), ()))


def _gla_pre(q, k, la, c):
    r = q.shape[0]
    shift = int(math.log2(c))
    ri = lax.broadcasted_iota(jnp.int32, (r, r), 0)
    si = lax.broadcasted_iota(jnp.int32, (r, r), 1)
    mask = ((ri >> shift) == (si >> shift)) & (ri >= si)
    tri = jnp.where(mask, 1.0, 0.0).astype(BF16)
    hi = la.astype(BF16)
    r1 = la - hi.astype(F32)
    mid = r1.astype(BF16)
    lo = (r1 - mid.astype(F32)).astype(BF16)
    cum = _dot(tri, hi) + _dot(tri, mid) + _dot(tri, lo)
    last = jnp.concatenate([jnp.broadcast_to(cum[(i + 1) * c - 1:(i + 1) * c, :], (c, cum.shape[1]))
                            for i in range(r // c)], axis=0)
    qe = (q * jnp.exp(cum)).astype(BF16)
    ke = (k * jnp.exp(-cum)).astype(BF16)
    kd = (k * jnp.exp(last - cum)).astype(BF16)
    return qe, ke, kd, cum, mask


def _gla_intra(qe, ke, v, mask):
    sc = lax.dot_general(qe, ke, _NT, preferred_element_type=F32)
    return _dot(jnp.where(mask, sc, 0.0).astype(BF16), v)


def _gla_finish(o, r, g):
    rf = r.astype(F32)
    return _rms(o, g) * (rf * jax.nn.sigmoid(rf))


GLA_SAFE_LOG_DECAY = -60.0


def _gla_token_step(t, rows, q, k, v, a_all, st_ref, oacc_ref):
    m = rows == t
    a = jnp.sum(jnp.where(m, a_all, 0.0), axis=0, keepdims=True)
    kt = jnp.where(m, k, 0.0).astype(BF16)
    qt = jnp.where(m, q, 0.0).astype(BF16)
    vt = jnp.where(m, v, jnp.zeros_like(v))
    st = st_ref[...] * a + lax.dot_general(vt, kt, _TN, preferred_element_type=F32)
    st_ref[...] = st
    oacc_ref[...] += lax.dot_general(qt, st.astype(BF16), _NT, preferred_element_type=F32)


def _gla_prompt_kernel(q_ref, k_ref, v_ref, r_ref, la_ref, g_ref, o_ref, sfin_ref, st_ref, oacc_ref,
                       *, n_chunks):
    j = pl.program_id(1)

    @pl.when(j == 0)
    def _():
        st_ref[...] = jnp.zeros_like(st_ref)

    c = GLA_CHUNK
    qe, ke, kd, cum, mask = _gla_pre(q_ref[...].astype(F32), k_ref[...].astype(F32), la_ref[...], c)
    g = g_ref[...]
    safe = jnp.min(cum) >= GLA_SAFE_LOG_DECAY

    @pl.when(safe)
    def _():
        for h in range(GLA_HEADS):
            kc = slice(h * GLA_DK, (h + 1) * GLA_DK)
            vc = slice(h * GLA_DV, (h + 1) * GLA_DV)
            v = v_ref[:, vc]
            qe_h, kd_h = qe[:, kc], kd[:, kc]
            o = _gla_intra(qe_h, ke[:, kc], v, mask)
            st = st_ref[h]
            inter = []
            for ci in range(n_chunks):
                rows = slice(ci * c, (ci + 1) * c)
                inter.append(lax.dot_general(qe_h[rows], st.astype(BF16), _NT, preferred_element_type=F32))
                dec = jnp.exp(cum[(ci + 1) * c - 1:(ci + 1) * c, kc])
                st = st * dec + lax.dot_general(v[rows], kd_h[rows], _TN, preferred_element_type=F32)
            st_ref[h] = st
            o = o + jnp.concatenate(inter, axis=0)
            o_ref[:, vc] = _gla_finish(o, r_ref[:, vc], g).astype(o_ref.dtype)

    @pl.when(jnp.logical_not(safe))
    def _():
        n_rows = n_chunks * c
        rows = lax.broadcasted_iota(jnp.int32, (n_rows, 1), 0)
        for h in range(GLA_HEADS):
            kc = slice(h * GLA_DK, (h + 1) * GLA_DK)
            vc = slice(h * GLA_DV, (h + 1) * GLA_DV)
            q, k, v = q_ref[:, kc].astype(F32), k_ref[:, kc].astype(F32), v_ref[:, vc]
            a_all = jnp.exp(la_ref[:, kc])
            oacc_ref[...] = jnp.zeros_like(oacc_ref)

            def step(t, carry):
                _gla_token_step(t, rows, q, k, v, a_all, st_ref.at[h], oacc_ref)
                return carry

            lax.fori_loop(0, n_rows, step, 0)
            o_ref[:, vc] = _gla_finish(oacc_ref[...], r_ref[:, vc], g).astype(o_ref.dtype)

    @pl.when(j == pl.num_programs(1) - 1)
    def _():
        for h in range(GLA_HEADS):
            sfin_ref[0, h] = st_ref[h].T


def _gla_prompt(q, k, v, r, la, g, batch, seq):
    rb = 4 * GLA_CHUNK
    nj = seq // rb
    row = lambda w: pl.BlockSpec((rb, w), lambda b, j: (b * nj + j, 0))
    return pl.pallas_call(
        functools.partial(_gla_prompt_kernel, n_chunks=rb // GLA_CHUNK),
        grid=(batch, nj),
        in_specs=[row(QK_W), row(QK_W), row(D_GLA), row(D_GLA), row(QK_W), _const_spec((1, GLA_DV))],
        out_specs=[row(D_GLA),
                   pl.BlockSpec((1, GLA_HEADS, GLA_DK, GLA_DV), lambda b, j: (b, 0, 0, 0))],
        out_shape=[jax.ShapeDtypeStruct((batch * seq, D_GLA), BF16),
                   jax.ShapeDtypeStruct((batch, GLA_HEADS, GLA_DK, GLA_DV), F32)],
        scratch_shapes=[pltpu.VMEM((GLA_HEADS, GLA_DV, GLA_DK), F32), pltpu.VMEM((rb, GLA_DV), F32)],
        compiler_params=pltpu.CompilerParams(dimension_semantics=("arbitrary", "arbitrary"),
                                             vmem_limit_bytes=VMEM_LIMIT),
        name="gla_prompt",
    )(q, k, v, r, la, g)


def _gla_sample_kernel(q_ref, k_ref, v_ref, r_ref, la_ref, g_ref, s0_ref, o_ref, sfin_ref, st_ref, oacc_ref,
                       *, n_seq, seq):
    qe, ke, kd, cum, mask = _gla_pre(q_ref[...].astype(F32), k_ref[...].astype(F32), la_ref[...], seq)
    g = g_ref[...]
    safe = jnp.min(cum) >= GLA_SAFE_LOG_DECAY

    @pl.when(safe)
    def _():
        lasts = jnp.concatenate([cum[(s + 1) * seq - 1:(s + 1) * seq, :] for s in range(n_seq)]
                                + [jnp.zeros((GLA_DK - n_seq, cum.shape[1]), F32)], axis=0)
        pair = 2 * seq
        upper = lax.broadcasted_iota(jnp.int32, (pair, GLA_DK), 0) < seq
        for h in range(GLA_HEADS):
            kc = slice(h * GLA_DK, (h + 1) * GLA_DK)
            vc = slice(h * GLA_DV, (h + 1) * GLA_DV)
            v = v_ref[:, vc]
            qe_h, kd_h = qe[:, kc], kd[:, kc]
            o = _gla_intra(qe_h, ke[:, kc], v, mask)
            dec_t = jnp.exp(lasts[:, kc].T)
            inter = []
            for p in range(n_seq // 2):
                rows = slice(p * pair, (p + 1) * pair)
                qe_p, kd_p, v_p = qe_h[rows], kd_h[rows], v[rows]
                for half in range(2):
                    s = 2 * p + half
                    s0 = s0_ref[s, h]
                    o_s = _dot(qe_p, s0.astype(BF16))
                    inter.append(o_s[half * seq:(half + 1) * seq])
                    kd_s = jnp.where(upper if half == 0 else ~upper, kd_p, jnp.zeros_like(kd_p))
                    dec = jnp.broadcast_to(dec_t[:, s:s + 1], (GLA_DK, GLA_DV))
                    sfin_ref[s, h] = s0 * dec + lax.dot_general(kd_s, v_p, _TN, preferred_element_type=F32)
            o = o + jnp.concatenate(inter, axis=0)
            o_ref[:, vc] = _gla_finish(o, r_ref[:, vc], g).astype(o_ref.dtype)

    @pl.when(jnp.logical_not(safe))
    def _():
        n_rows = n_seq * seq
        rows = lax.broadcasted_iota(jnp.int32, (n_rows, 1), 0)
        for h in range(GLA_HEADS):
            kc = slice(h * GLA_DK, (h + 1) * GLA_DK)
            vc = slice(h * GLA_DV, (h + 1) * GLA_DV)
            q, k, v = q_ref[:, kc].astype(F32), k_ref[:, kc].astype(F32), v_ref[:, vc]
            a_all = jnp.exp(la_ref[:, kc])
            oacc_ref[...] = jnp.zeros_like(oacc_ref)

            def step(t, carry):
                s = t // seq

                @pl.when(t % seq == 0)
                def _():
                    st_ref[...] = s0_ref[s, h].T

                _gla_token_step(t, rows, q, k, v, a_all, st_ref, oacc_ref)

                @pl.when(t % seq == seq - 1)
                def _():
                    sfin_ref[s, h] = st_ref[...].T

                return carry

            lax.fori_loop(0, n_rows, step, 0)
            o_ref[:, vc] = _gla_finish(oacc_ref[...], r_ref[:, vc], g).astype(o_ref.dtype)


def _gla_sample(q, k, v, r, la, g, s0, batch, seq):
    ns = 16
    rb = ns * seq
    row = lambda w: pl.BlockSpec((rb, w), lambda i: (i, 0))
    st = pl.BlockSpec((ns, GLA_HEADS, GLA_DK, GLA_DV), lambda i: (i, 0, 0, 0))
    return pl.pallas_call(
        functools.partial(_gla_sample_kernel, n_seq=ns, seq=seq),
        grid=(batch // ns,),
        in_specs=[row(QK_W), row(QK_W), row(D_GLA), row(D_GLA), row(QK_W), _const_spec((1, GLA_DV)), st],
        out_specs=[row(D_GLA), st],
        out_shape=[jax.ShapeDtypeStruct((batch * seq, D_GLA), BF16),
                   jax.ShapeDtypeStruct((batch, GLA_HEADS, GLA_DK, GLA_DV), F32)],
        scratch_shapes=[pltpu.VMEM((GLA_DV, GLA_DK), F32), pltpu.VMEM((rb, GLA_DV), F32)],
        compiler_params=pltpu.CompilerParams(dimension_semantics=("arbitrary",), vmem_limit_bytes=VMEM_LIMIT),
        name="gla_sample",
    )(q, k, v, r, la, g, s0)


def _s5_kernel(u_ref, wbu_ref, wc_ref, are_ref, aim_ref, d_ref, h0r_ref, h0i_ref,
               y_ref, sre_ref, sim_ref, bu_ref, xs_ref, car_ref, *, nb, tc, has_state):
    j = pl.program_id(2)

    @pl.when(j == 0)
    def _():
        if has_state:
            car_ref[0] = h0r_ref[...]
            car_ref[1] = h0i_ref[...]
        else:
            car_ref[...] = jnp.zeros_like(car_ref)

    u2 = u_ref[...].reshape(nb * tc, LANES)
    ub = u2.astype(BF16)
    nl = S5_SL // LANES
    for l in range(2 * nl):
        bu_ref[l] = _dot(ub, wbu_ref[0, :, l * LANES:(l + 1) * LANES])
    a_r = [jnp.broadcast_to(are_ref[0, :, l * LANES:(l + 1) * LANES], (nb, LANES)) for l in range(nl)]
    a_i = [jnp.broadcast_to(aim_ref[0, :, l * LANES:(l + 1) * LANES], (nb, LANES)) for l in range(nl)]

    def step(t, carry):
        rows = pl.ds(t, nb, stride=tc)
        out = []
        for l in range(nl):
            xr, xi = carry[2 * l], carry[2 * l + 1]
            nr = a_r[l] * xr - a_i[l] * xi + bu_ref[l, rows, :]
            ni = a_r[l] * xi + a_i[l] * xr + bu_ref[nl + l, rows, :]
            xs_ref[l, rows, :] = nr
            xs_ref[nl + l, rows, :] = ni
            out += [nr, ni]
        return tuple(out)

    init = []
    for l in range(nl):
        init += [car_ref[0, :, l * LANES:(l + 1) * LANES], car_ref[1, :, l * LANES:(l + 1) * LANES]]
    fin = lax.fori_loop(0, tc, step, tuple(init), unroll=8)
    xr = jnp.concatenate([fin[2 * l] for l in range(nl)], axis=1)
    xi = jnp.concatenate([fin[2 * l + 1] for l in range(nl)], axis=1)
    car_ref[0] = xr
    car_ref[1] = xi
    y = d_ref[0] * u2
    for l in range(2 * nl):
        y = y + _dot(xs_ref[l].astype(BF16), wc_ref[0, l * LANES:(l + 1) * LANES, :])
    y_ref[...] = y.reshape(nb, tc, LANES)

    @pl.when(j == pl.num_programs(2) - 1)
    def _():
        sre_ref[...] = xr
        sim_ref[...] = xi


def _s5(u3d, wbu, wc, a_re, a_im, dsk, h0r, h0i, nb, tc, has_state):
    batch, seq, _ = u3d.shape
    grid = (S5_NGB, batch // nb, seq // tc)
    st = pl.BlockSpec((nb, S5_SL), lambda g, b, j: (b, g))
    par = lambda w: pl.BlockSpec((1, 1, w), lambda g, b, j: (g, 0, 0))
    ublk = pl.BlockSpec((nb, tc, LANES), lambda g, b, j: (b, j, g))
    return pl.pallas_call(
        functools.partial(_s5_kernel, nb=nb, tc=tc, has_state=has_state),
        grid=grid,
        in_specs=[ublk,
                  pl.BlockSpec((1, LANES, 2 * S5_SL), lambda g, b, j: (g, 0, 0)),
                  pl.BlockSpec((1, 2 * S5_SL, LANES), lambda g, b, j: (g, 0, 0)),
                  par(S5_SL), par(S5_SL), par(LANES), st, st],
        out_specs=[ublk, st, st],
        out_shape=[jax.ShapeDtypeStruct(u3d.shape, F32),
                   jax.ShapeDtypeStruct((batch, S5_GROUPS * S5_STATE), F32),
                   jax.ShapeDtypeStruct((batch, S5_GROUPS * S5_STATE), F32)],
        scratch_shapes=[pltpu.VMEM((2 * S5_SL // LANES, nb * tc, LANES), F32),
                        pltpu.VMEM((2 * S5_SL // LANES, nb * tc, LANES), F32),
                        pltpu.VMEM((2, nb, S5_SL), F32)],
        compiler_params=pltpu.CompilerParams(dimension_semantics=("arbitrary",) * 3,
                                             vmem_limit_bytes=VMEM_LIMIT),
        name="s5_state" if has_state else "s5_zero",
    )(u3d, wbu, wc, a_re, a_im, dsk, h0r, h0i)


S5_J = 2
S5_HL = S5_SL // S5_J


def _s5_prompt_kernel(u_ref, wbu_ref, wc_ref, are_ref, aim_ref, d_ref, y_ref, sre_ref, sim_ref,
                      u2_ref, lhs_ref, bu_ref, xs_ref, y2_ref, yo_ref, car_ref, *, nb, tc):
    g = pl.program_id(0)
    j = pl.program_id(1)
    rows = nb * S5_J

    @pl.when((g == 0) & (j == 0))
    def _():
        lhs_ref[...] = jnp.zeros_like(lhs_ref)

    @pl.when(j == 0)
    def _():
        car_ref[...] = jnp.zeros_like(car_ref)

    u2 = u_ref[...].reshape(nb * tc, LANES)
    u2_ref[...] = u2

    def build(t, c):
        u4 = u2_ref[pl.ds(t, nb, stride=tc), :]
        for jj in range(S5_J):
            lhs_ref[t, jj * nb:(jj + 1) * nb, jj * LANES:(jj + 1) * LANES] = u4
        return c

    lax.fori_loop(0, tc, build, 0, unroll=8)
    lhs = lhs_ref[...].reshape(tc * rows, S5_J * LANES).astype(BF16)
    bu_ref[...] = _dot(lhs, wbu_ref[0]).reshape(tc, rows, 2 * S5_HL)
    ar = are_ref[0]
    ai = aim_ref[0]

    def step(t, carry):
        xr, xi = carry
        tile = bu_ref[t]
        nr = ar * xr - ai * xi + tile[:, 0:S5_HL]
        ni = ar * xi + ai * xr + tile[:, S5_HL:2 * S5_HL]
        xs_ref[t] = jnp.concatenate([nr, ni], axis=1)
        return nr, ni

    xr, xi = lax.fori_loop(0, tc, step, (car_ref[0], car_ref[1]), unroll=8)
    car_ref[0] = xr
    car_ref[1] = xi
    xs = xs_ref[...].reshape(tc * rows, 2 * S5_HL).astype(BF16)
    y2_ref[...] = _dot(xs, wc_ref[0]).reshape(tc, rows, S5_J * LANES)
    first_half = lax.broadcasted_iota(jnp.int32, (rows, LANES), 0) < nb

    def unperm(t, c):
        t2 = y2_ref[t]
        part = jnp.where(first_half, t2[:, 0:LANES], t2[:, LANES:2 * LANES])
        tot = part + pltpu.roll(part, nb, axis=0)
        yo_ref[pl.ds(t, nb, stride=tc), :] = tot[0:nb]
        return c

    lax.fori_loop(0, tc, unperm, 0, unroll=8)
    y_ref[...] = (yo_ref[...] + d_ref[0] * u2).reshape(nb, tc, LANES)

    @pl.when(j == pl.num_programs(1) - 1)
    def _():
        sre_ref[...] = jnp.concatenate([xr[jj * nb:(jj + 1) * nb] for jj in range(S5_J)], axis=1)
        sim_ref[...] = jnp.concatenate([xi[jj * nb:(jj + 1) * nb] for jj in range(S5_J)], axis=1)


def _s5_prompt(u3d, wbu2, wc2, a_re2, a_im2, dsk, tc):
    nb, seq, _ = u3d.shape
    rows = nb * S5_J
    assert rows == 8
    st = pl.BlockSpec((nb, S5_SL), lambda g, j: (0, g))
    gblk = lambda s: pl.BlockSpec((1,) + s, lambda g, j: (g, 0, 0))
    ublk = pl.BlockSpec((nb, tc, LANES), lambda g, j: (0, j, g))
    return pl.pallas_call(
        functools.partial(_s5_prompt_kernel, nb=nb, tc=tc),
        grid=(S5_NGB, seq // tc),
        in_specs=[ublk, gblk((S5_J * LANES, 2 * S5_HL)), gblk((2 * S5_HL, S5_J * LANES)),
                  gblk((rows, S5_HL)), gblk((rows, S5_HL)), gblk((1, LANES))],
        out_specs=[ublk, st, st],
        out_shape=[jax.ShapeDtypeStruct(u3d.shape, F32),
                   jax.ShapeDtypeStruct((nb, S5_GROUPS * S5_STATE), F32),
                   jax.ShapeDtypeStruct((nb, S5_GROUPS * S5_STATE), F32)],
        scratch_shapes=[pltpu.VMEM((nb * tc, LANES), F32),
                        pltpu.VMEM((tc, rows, S5_J * LANES), F32),
                        pltpu.VMEM((tc, rows, 2 * S5_HL), F32),
                        pltpu.VMEM((tc, rows, 2 * S5_HL), F32),
                        pltpu.VMEM((tc, rows, S5_J * LANES), F32),
                        pltpu.VMEM((nb * tc, LANES), F32),
                        pltpu.VMEM((2, rows, S5_HL), F32)],
        compiler_params=pltpu.CompilerParams(dimension_semantics=("arbitrary",) * 2,
                                             vmem_limit_bytes=VMEM_LIMIT),
        name="s5_prompt",
    )(u3d, wbu2, wc2, a_re2, a_im2, dsk)


def _s5_prompt_params(wbu, wc, a_re, a_im, nb):
    h = S5_HL
    top = jnp.concatenate([wbu[:, :, 0:h], wbu[:, :, S5_SL:S5_SL + h]], axis=2)
    bot = jnp.concatenate([wbu[:, :, h:2 * h], wbu[:, :, S5_SL + h:S5_SL + 2 * h]], axis=2)
    wbu2 = jnp.concatenate([top, bot], axis=1)
    wc_j = [jnp.concatenate([wc[:, jj * h:(jj + 1) * h, :], wc[:, S5_SL + jj * h:S5_SL + (jj + 1) * h, :]],
                            axis=1) for jj in range(S5_J)]
    wc2 = jnp.concatenate(wc_j, axis=2)
    tile = lambda a: jnp.repeat(a.reshape(S5_NGB, S5_J, h), nb, axis=1)
    return wbu2, wc2, tile(a_re), tile(a_im)


def _mix_out_kernel(xp_ref, xs_ref, ogp_ref, ogs_ref, ysp_ref, yss_ref, glu_w_ref, glu_b_ref, s5n_ref, wo_ref,
                    nffn_ref, wr_ref, h1_ref, hn_ref, rt_ref, *, n_prompt_tiles):
    is_p = pl.program_id(0) < n_prompt_tiles
    x = jnp.where(is_p, xp_ref[...], xs_ref[...])
    og = jnp.where(is_p, ogp_ref[...], ogs_ref[...])
    y = jax.nn.gelu(jnp.where(is_p, ysp_ref[...], yss_ref[...]))
    y = y * jax.nn.sigmoid(_dot(y.astype(BF16), glu_w_ref[...]) + glu_b_ref[...])
    yn = _rms(y, s5n_ref[...]).astype(BF16)
    mix = _dot(og, wo_ref[0:D_GLA, :]) + _dot(yn, wo_ref[D_GLA:D_GLA + D_S5, :])
    h1 = x + mix
    h1_ref[...] = h1
    hn = _rms(h1, nffn_ref[...])
    hn_ref[...] = _pack_bf16_pair(hn[:, 0:HALF], hn[:, HALF:D_MODEL])
    hn_hi = hn.astype(BF16)
    hn_lo = (hn - hn_hi.astype(F32)).astype(BF16)
    logits = _dot(hn_hi, wr_ref[0]) + _dot(hn_hi, wr_ref[1]) + _dot(hn_lo, wr_ref[0])
    rt_ref[...] = _route(logits)


def _route(logits):
    col = lax.broadcasted_iota(jnp.int32, logits.shape, 1)
    colf = col.astype(F32)
    neg = -jnp.inf

    def first_argmax(vals):
        m = jnp.max(vals, axis=-1, keepdims=True)
        idx = jnp.min(jnp.where(vals == m, colf, float(LANES)), axis=-1, keepdims=True)
        return m, idx

    lg = jnp.where(col < N_EGROUPS, logits, neg)
    gmax, gsel = first_argmax(lg)
    p_g = 1.0 / jnp.sum(jnp.exp(lg - gmax), axis=-1, keepdims=True)
    ecol = col - N_EGROUPS
    egrp = (ecol >> 3).astype(F32)
    in_group = (ecol >= 0) & (ecol < N_EXPERTS) & (egrp == gsel)
    le = jnp.where(in_group, logits, neg)
    m1, i1 = first_argmax(le)
    le2 = jnp.where(colf == i1, neg, le)
    m2, i2 = first_argmax(le2)
    e2 = jnp.exp(m2 - m1)
    den = 1.0 + e2
    w1 = p_g * (1.0 / den)
    w2 = p_g * (e2 / den)
    e1f = i1 - float(N_EGROUPS)
    e2f = i2 - float(N_EGROUPS)
    out = jnp.where(col == 0, e1f, jnp.where(col == 1, e2f, jnp.where(col == 2, w1, jnp.where(col == 3, w2, 0.0))))
    return out


def _mix_out(xp, xs, ogp, ogs, ysp, yss, glu_w, glu_b, s5n, wo, nffn, wr):
    tm = TOK_TILE
    npt, nst = xp.shape[0] // tm, xs.shape[0] // tm
    t = (npt + nst) * tm
    row = lambda w: pl.BlockSpec((tm, w), lambda i: (i, 0))
    prow = lambda w: pl.BlockSpec((tm, w), lambda i: (jnp.minimum(i, npt - 1), 0))
    srow = lambda w: pl.BlockSpec((tm, w), lambda i: (jnp.maximum(i - npt, 0), 0))
    return pl.pallas_call(
        functools.partial(_mix_out_kernel, n_prompt_tiles=npt),
        grid=(npt + nst,),
        in_specs=[prow(D_MODEL), srow(D_MODEL), prow(D_GLA), srow(D_GLA), prow(D_S5), srow(D_S5),
                  _const_spec(glu_w.shape), _const_spec((1, D_S5)), _const_spec((1, D_S5)),
                  _const_spec(wo.shape), _const_spec((1, D_MODEL)), _const_spec(wr.shape)],
        out_specs=[row(D_MODEL), row(HALF), row(LANES)],
        out_shape=[jax.ShapeDtypeStruct((t, D_MODEL), F32), jax.ShapeDtypeStruct((t, HALF), jnp.uint32),
                   jax.ShapeDtypeStruct((t, LANES), F32)],
        compiler_params=pltpu.CompilerParams(dimension_semantics=("arbitrary",), vmem_limit_bytes=VMEM_LIMIT),
        name="mix_out",
    )(xp, xs, ogp, ogs, ysp, yss, glu_w, glu_b, s5n, wo, nffn, wr)


DMA_UNROLL = 8


MOE_SB = 3
HALF = D_MODEL // 2


def _pack_bf16_pair(lo, hi):
    def bits(x):
        b = pltpu.bitcast(x, jnp.uint32)
        return (b + jnp.uint32(0x7FFF) + ((b >> 16) & jnp.uint32(1))) >> 16
    return bits(lo) | (bits(hi) << 16)


def _unpack_bf16_pair(u):
    return (pltpu.bitcast(u << 16, F32), pltpu.bitcast(u & jnp.uint32(0xFFFF0000), F32))


def _moe_kernel(se_ref, sm_ref, sn_ref, sbase_ref, sa_ref, hn_hbm, wg_ref, wu_ref, wd_ref, out_hbm,
                xb_ref, yb_ref, wgb_ref, wub_ref, wdb_ref, gsem, ssem):
    b = pl.program_id(0)
    nb = pl.num_programs(0)
    n = sn_ref[b]
    slot = lax.rem(b, 2)

    def round_up(cnt):
        return (cnt + DMA_UNROLL - 1) // DMA_UNROLL * DMA_UNROLL

    def issue_gather(blk, sl):
        base = sbase_ref[blk]

        def grp(g, c):
            for j in range(DMA_UNROLL):
                i = g * DMA_UNROLL + j
                tok = sa_ref[base + i] >> 1
                pltpu.make_async_copy(hn_hbm.at[pl.ds(tok, 1), :], xb_ref.at[sl, pl.ds(i, 1), :],
                                      gsem.at[sl]).start()
            return c

        lax.fori_loop(0, round_up(sn_ref[blk]) // DMA_UNROLL, grp, 0)

    def wait_gather(cnt, sl):
        rows = pl.ds(0, pl.multiple_of(round_up(cnt), DMA_UNROLL))
        pltpu.make_async_copy(hn_hbm.at[rows, :], xb_ref.at[sl, rows, :], gsem.at[sl]).wait()

    m_cur = sm_ref[b]
    m_prev = sm_ref[jnp.maximum(b - 1, 0)]
    has_prev = (b > 0) & (m_prev > 0)

    def out_copy(rows):
        dst = pl.ds(pl.multiple_of(sbase_ref[b], MOE_BLK), rows)
        return pltpu.make_async_copy(yb_ref.at[pl.ds(0, rows), :], out_hbm.at[dst, :], ssem.at[0])

    def wait_prev_out():
        rows = pl.ds(0, pl.multiple_of(m_prev * MOE_BLK, MOE_BLK))
        pltpu.make_async_copy(yb_ref.at[rows, :], out_hbm.at[rows, :], ssem.at[0]).wait()

    def write_out(rows):
        out_copy(rows).start()

        @pl.when(b == nb - 1)
        def _():
            out_copy(rows).wait()

    @pl.when(b == 0)
    def _():
        xb_ref[...] = jnp.zeros_like(xb_ref)
        issue_gather(0, 0)

    @pl.when(b + 1 < nb)
    def _():
        issue_gather(b + 1, 1 - slot)

    def compute(rows):
        x_lo, x_hi = _unpack_bf16_pair(xb_ref[slot, 0:rows, :])
        x_lo, x_hi = x_lo.astype(BF16), x_hi.astype(BF16)
        gate = _dot(x_lo, wgb_ref[0:HALF, :]) + _dot(x_hi, wgb_ref[HALF:D_MODEL, :])
        up = _dot(x_lo, wub_ref[0:HALF, :]) + _dot(x_hi, wub_ref[HALF:D_MODEL, :])
        hid = (gate * jax.nn.sigmoid(gate) * up).astype(BF16)

        @pl.when(has_prev)
        def _():
            wait_prev_out()

        yb_ref[0:rows, :] = _pack_bf16_pair(_dot(hid, wdb_ref[:, 0:HALF]), _dot(hid, wdb_ref[:, HALF:D_MODEL]))
        write_out(rows)

    @pl.when(n > 0)
    def _():
        prev_e = se_ref[jnp.maximum(b - 1, 0)]

        @pl.when((b == 0) | (prev_e != se_ref[b]))
        def _():
            wgb_ref[...] = wg_ref[...].astype(BF16)
            wub_ref[...] = wu_ref[...].astype(BF16)
            wdb_ref[...] = wd_ref[...].astype(BF16)

        wait_gather(n, slot)
        for m in range(1, MOE_SB + 1):
            @pl.when(m_cur == m)
            def _():
                compute(m * MOE_BLK)

    @pl.when((n == 0) & (m_cur > 0))
    def _():
        @pl.when(has_prev)
        def _():
            wait_prev_out()

        yb_ref[...] = jnp.zeros_like(yb_ref)
        for m in range(1, MOE_SB + 1):
            @pl.when(m_cur == m)
            def _():
                write_out(m * MOE_BLK)

    @pl.when((m_cur == 0) & has_prev)
    def _():
        wait_prev_out()


def _moe(hn_pk, plan, w_gate, w_up, w_down, n_out_rows):
    sb_e, sb_m, sb_n, sb_base, slot_a = plan
    sb_rows = MOE_SB * MOE_BLK
    wspec = lambda s: pl.BlockSpec((None,) + s, lambda b, se, *_: (se[b], 0, 0))
    grid_spec = pltpu.PrefetchScalarGridSpec(
        num_scalar_prefetch=5,
        grid=(sb_e.shape[0],),
        in_specs=[pl.BlockSpec(memory_space=pl.ANY),
                  wspec((D_MODEL, D_EXPERT)), wspec((D_MODEL, D_EXPERT)), wspec((D_EXPERT, D_MODEL))],
        out_specs=pl.BlockSpec(memory_space=pl.ANY),
        scratch_shapes=[pltpu.VMEM((2, sb_rows, HALF), jnp.uint32), pltpu.VMEM((sb_rows, HALF), jnp.uint32),
                        pltpu.VMEM((D_MODEL, D_EXPERT), BF16), pltpu.VMEM((D_MODEL, D_EXPERT), BF16),
                        pltpu.VMEM((D_EXPERT, D_MODEL), BF16), pltpu.SemaphoreType.DMA((2,)),
                        pltpu.SemaphoreType.DMA((1,))],
    )
    return pl.pallas_call(
        _moe_kernel,
        grid_spec=grid_spec,
        out_shape=jax.ShapeDtypeStruct((n_out_rows, HALF), jnp.uint32),
        compiler_params=pltpu.CompilerParams(dimension_semantics=("arbitrary",),
                                             vmem_limit_bytes=MOE_VMEM_LIMIT),
        name="moe_experts",
    )(sb_e, sb_m, sb_n, sb_base, slot_a, hn_pk, w_gate, w_up, w_down)


def _moe_plan(rt_all):
    t_all = rt_all.shape[0]
    n_assign = 2 * t_all
    e_flat = rt_all[:, 0:2].astype(jnp.int32).reshape(-1)
    onehot = (e_flat[:, None] == jnp.arange(N_EXPERTS, dtype=jnp.int32)[None, :]).astype(jnp.int32)
    csum = jnp.cumsum(onehot, axis=0)
    counts = csum[-1]
    rank = jnp.take_along_axis(csum, e_flat[:, None], axis=1)[:, 0] - 1
    padded = (counts + MOE_BLK - 1) // MOE_BLK * MOE_BLK
    pad_end = jnp.cumsum(padded)
    pad_start = pad_end - padded
    dest = pad_start[e_flat] + rank
    nblk = -(-(n_assign + N_EXPERTS * (MOE_BLK - 1)) // MOE_BLK)
    n_slots = nblk * MOE_BLK
    slot_a = jnp.zeros((n_slots,), jnp.int32).at[dest].set(jnp.arange(n_assign, dtype=jnp.int32))
    k_e = padded // MOE_BLK
    sbc = (k_e + MOE_SB - 1) // MOE_SB
    sb_end = jnp.cumsum(sbc)
    sb_start = sb_end - sbc
    n_sb = (nblk + (MOE_SB - 1) * N_EXPERTS) // MOE_SB
    s = jnp.arange(n_sb, dtype=jnp.int32)
    sb_e = jnp.minimum(jnp.sum((s[:, None] >= sb_end[None, :]).astype(jnp.int32), axis=1), N_EXPERTS - 1)
    j = s - sb_start[sb_e]
    real = s < sb_end[-1]
    tail_blk = pad_end[-1] // MOE_BLK + MOE_SB * (s - sb_end[-1])
    sb_m = jnp.where(real, jnp.clip(k_e[sb_e] - MOE_SB * j, 0, MOE_SB), jnp.clip(nblk - tail_blk, 0, MOE_SB))
    sb_n = jnp.where(real, jnp.clip(counts[sb_e] - MOE_SB * MOE_BLK * j, 0, MOE_SB * MOE_BLK), 0)
    sb_base = jnp.where(real, pad_start[sb_e] + MOE_SB * MOE_BLK * j, jnp.minimum(tail_blk, nblk - 1) * MOE_BLK)
    i32 = lambda a: a.astype(jnp.int32)
    return (i32(sb_e), i32(sb_m), i32(sb_n), i32(sb_base), slot_a), i32(dest), n_slots


def _ple_out_kernel(dest_ref, h1_ref, rt_ref, pp_ref, ps_ref, nple_ref, wpg_ref, wp_ref, nfin_ref, ys_hbm,
                    op_ref, os_ref, yb_ref, sem, *, n_prompt_tiles):
    i = pl.program_id(0)
    nt = pl.num_programs(0)
    tm = h1_ref.shape[0]
    slot = lax.rem(i, 2)

    def issue(tile, sl):
        base = tile * (2 * tm)

        def grp(g, c):
            for j in range(DMA_UNROLL):
                d = dest_ref[base + g * DMA_UNROLL + j]
                row = g * (DMA_UNROLL // 2) + j // 2
                pltpu.make_async_copy(ys_hbm.at[pl.ds(d, 1), :], yb_ref.at[sl, j % 2, pl.ds(row, 1), :],
                                      sem.at[sl]).start()
            return c

        lax.fori_loop(0, 2 * tm // DMA_UNROLL, grp, 0)

    @pl.when(i == 0)
    def _():
        issue(0, 0)

    def wait_rows(sl):
        for k in range(2):
            pltpu.make_async_copy(ys_hbm.at[pl.ds(0, tm), :], yb_ref.at[sl, k], sem.at[sl]).wait()

    wait_rows(slot)
    is_p = i < n_prompt_tiles
    rt = rt_ref[...]
    w0, w1 = rt[:, 2:3], rt[:, 3:4]
    lo0, hi0 = _unpack_bf16_pair(yb_ref[slot, 0])
    lo1, hi1 = _unpack_bf16_pair(yb_ref[slot, 1])
    h2 = h1_ref[...] + jnp.concatenate([w0 * lo0 + w1 * lo1, w0 * hi0 + w1 * hi1], axis=1)
    hb = _rms(h2, nple_ref[...]).astype(BF16)
    nbase = jnp.minimum(i + 1, nt - 1) * (2 * tm)
    n_chunk = 8
    cw, ca = D_MODEL // n_chunk, 2 * tm // n_chunk
    gates = []
    for c in range(n_chunk):
        gates.append(jax.nn.sigmoid(_dot(hb, wpg_ref[:, c * cw:(c + 1) * cw])))
        for a in range(c * ca, (c + 1) * ca):
            d = dest_ref[nbase + a]
            pltpu.make_async_copy(ys_hbm.at[pl.ds(d, 1), :], yb_ref.at[1 - slot, a % 2, pl.ds(a // 2, 1), :],
                                  sem.at[1 - slot]).start()
    gate = jnp.concatenate(gates, axis=1)

    @pl.when(i == nt - 1)
    def _():
        wait_rows(1 - slot)

    p = jnp.where(is_p, pp_ref[...], ps_ref[...])
    h3 = h2 + _dot(p.astype(BF16), wp_ref[...]) * gate
    y = _rms(h3, nfin_ref[...])

    @pl.when(is_p)
    def _():
        op_ref[...] = y

    @pl.when(jnp.logical_not(is_p))
    def _():
        os_ref[...] = y


def _ple_out(dest, h1, rt, pp, ps, nple, wpg, wp, nfin, y_slots):
    tm = TOK_TILE
    npt, nst = pp.shape[0] // tm, ps.shape[0] // tm
    row = lambda w: pl.BlockSpec((tm, w), lambda i, d: (i, 0))
    prow = lambda w: pl.BlockSpec((tm, w), lambda i, d: (jnp.minimum(i, npt - 1), 0))
    srow = lambda w: pl.BlockSpec((tm, w), lambda i, d: (jnp.maximum(i - npt, 0), 0))
    const = lambda shape: pl.BlockSpec(shape, lambda i, d: (0,) * len(shape), pipeline_mode=pl.Buffered(1))
    grid_spec = pltpu.PrefetchScalarGridSpec(
        num_scalar_prefetch=1,
        grid=(npt + nst,),
        in_specs=[row(D_MODEL), row(LANES), prow(D_PLE), srow(D_PLE), const((1, D_MODEL)), const(wpg.shape),
                  const(wp.shape), const((1, D_MODEL)), pl.BlockSpec(memory_space=pl.ANY)],
        out_specs=[prow(D_MODEL), srow(D_MODEL)],
        scratch_shapes=[pltpu.VMEM((2, 2, tm, HALF), jnp.uint32), pltpu.SemaphoreType.DMA((2,))],
    )
    return pl.pallas_call(
        functools.partial(_ple_out_kernel, n_prompt_tiles=npt),
        grid_spec=grid_spec,
        out_shape=[jax.ShapeDtypeStruct((pp.shape[0], D_MODEL), F32),
                   jax.ShapeDtypeStruct((ps.shape[0], D_MODEL), F32)],
        compiler_params=pltpu.CompilerParams(dimension_semantics=("arbitrary",), vmem_limit_bytes=VMEM_LIMIT),
        name="ple_out",
    )(dest, h1, rt, pp, ps, nple, wpg, wp, nfin, y_slots)


def _s5_params(lam_re, lam_im, log_dt, b_re, b_im, c_re, c_im, d_skip):
    dt = jnp.exp(log_dt)[:, None]
    mag = jnp.exp(lam_re * dt)
    ab_re = mag * jnp.cos(lam_im * dt)
    ab_im = mag * jnp.sin(lam_im * dt)
    den = lam_re * lam_re + lam_im * lam_im
    nr = ab_re - 1.0
    f_re = (nr * lam_re + ab_im * lam_im) / den
    f_im = (ab_im * lam_re - nr * lam_im) / den
    bb_re = f_re[..., None] * b_re - f_im[..., None] * b_im
    bb_im = f_re[..., None] * b_im + f_im[..., None] * b_re
    eye = jnp.eye(S5_GB, dtype=F32)

    def bu_w(bb):
        bb = bb.reshape(S5_NGB, S5_GB, S5_STATE, S5_GROUP)
        w = jnp.einsum('nlph,lm->nlhmp', bb, eye)
        return w.reshape(S5_NGB, S5_GB * S5_GROUP, S5_SL)

    def c_w(c):
        c = c.reshape(S5_NGB, S5_GB, S5_GROUP, S5_STATE)
        w = jnp.einsum('nlhp,lm->nlpmh', c, eye)
        return w.reshape(S5_NGB, S5_SL, S5_GB * S5_GROUP)

    wbu = jnp.concatenate([bu_w(bb_re), bu_w(bb_im)], axis=2).astype(BF16)
    wc = jnp.concatenate([c_w(c_re), -c_w(c_im)], axis=1).astype(BF16)
    a_re = ab_re.reshape(S5_NGB, 1, S5_SL)
    a_im = ab_im.reshape(S5_NGB, 1, S5_SL)
    dsk = d_skip.reshape(S5_NGB, 1, S5_GB * S5_GROUP)
    return wbu, wc, a_re, a_im, dsk


def kernel(x_prompt, x_sample, p_prompt, p_sample, state_gla, state_s5_re, state_s5_im, norm_mix, w_in, gla_w_gate_up, gla_gate_bias, gla_norm, s5_lam_re, s5_lam_im, s5_log_dt, s5_b_re, s5_b_im, s5_c_re, s5_c_im, s5_d, s5_glu_w, s5_glu_b, s5_norm, w_out, norm_ffn, router_group, router_expert, w_gate, w_up, w_down, norm_ple, w_ple, w_ple_gate, norm_final):
    depth = w_in.shape[0]
    assert depth == 1
    i = 0
    bp, lp, _ = x_prompt.shape
    bs, ls, _ = x_sample.shape
    tp, ts = bp * lp, bs * ls
    t_all = tp + ts

    wm = _w_in_prep(w_in[i])
    wgu = jnp.pad(gla_w_gate_up[i], ((0, LANES - GLA_RANK), (0, 0))).astype(BF16)
    gbias = gla_gate_bias[i].reshape(1, QK_W)
    wbu, wc, a_re, a_im, dsk = _s5_params(s5_lam_re[i], s5_lam_im[i], s5_log_dt[i], s5_b_re[i], s5_b_im[i],
                                          s5_c_re[i], s5_c_im[i], s5_d[i])
    glu_w = s5_glu_w[i].astype(BF16)
    wo = w_out[i].astype(BF16)
    wr32 = jnp.pad(jnp.concatenate([router_group[i], router_expert[i]], axis=1),
                   ((0, 0), (0, LANES - N_EGROUPS - N_EXPERTS)))
    wr_hi = wr32.astype(BF16)
    wr = jnp.stack([wr_hi, (wr32 - wr_hi.astype(F32)).astype(BF16)])
    wpg = w_ple_gate[i].astype(BF16)
    wp = w_ple[i].astype(BF16)
    vec = lambda a: a.reshape(1, -1)

    xp = x_prompt.reshape(tp, D_MODEL)
    xs = x_sample.reshape(ts, D_MODEL)

    qp, kp, vp, rp, lap, up = _in_proj(xp, vec(norm_mix[i]), wm, wgu, gbias, BF16)
    qs, ks, vs, rs, las, us = _in_proj(xs, vec(norm_mix[i]), wm, wgu, gbias, BF16)
    ogp, gla_p = _gla_prompt(qp, kp, vp, rp, lap, vec(gla_norm[i]), bp, lp)
    ogs, gla_s = _gla_sample(qs, ks, vs, rs, las, vec(gla_norm[i]), state_gla[i], bs, ls)
    wbu2, wc2, a_re2, a_im2 = _s5_prompt_params(wbu, wc, a_re, a_im, bp)
    ysp, re_p, im_p = _s5_prompt(up.reshape(bp, lp, D_S5), wbu2, wc2, a_re2, a_im2, dsk, tc=256)
    yss, re_s, im_s = _s5(us.reshape(bs, ls, D_S5), wbu, wc, a_re, a_im, dsk,
                          state_s5_re[i].reshape(bs, -1), state_s5_im[i].reshape(bs, -1),
                          nb=32, tc=ls, has_state=True)

    h1, hn_all, rt_all = _mix_out(xp, xs, ogp, ogs, ysp.reshape(tp, D_S5), yss.reshape(ts, D_S5), glu_w,
                                  vec(s5_glu_b[i]), vec(s5_norm[i]), wo, vec(norm_ffn[i]), wr)

    plan, dest, n_slots = _moe_plan(rt_all)
    y_slots = _moe(hn_all, plan, w_gate[i], w_up[i], w_down[i], n_slots)

    y_p, y_s = _ple_out(dest, h1, rt_all, p_prompt[i].reshape(tp, D_PLE), p_sample[i].reshape(ts, D_PLE),
                        vec(norm_ple[i]), wpg, wp, vec(norm_final), y_slots)

    s5shape = lambda a, b: a.reshape(1, b, S5_GROUPS, S5_STATE)
    return (y_p.reshape(bp, lp, D_MODEL), y_s.reshape(bs, ls, D_MODEL),
            gla_p[None], s5shape(re_p, bp), s5shape(im_p, bp),
            gla_s[None], s5shape(re_s, bs), s5shape(im_s, bs))
```

```python
import functools
import math

import jax
import jax.numpy as jnp
from jax import lax
from jax.experimental import pallas as pl
from jax.experimental.pallas import tpu as pltpu

F32 = jnp.float32
BF16 = jnp.bfloat16

D_MODEL = 2048
D_GLA = 1024
D_S5 = 1024
GLA_HEADS = 4
GLA_DV = 256
GLA_DK = 128
GLA_RANK = 16
GLA_CHUNK = 64
S5_GROUP = 16
S5_GROUPS = 64
S5_STATE = 64
N_EGROUPS = 4
N_EPG = 8
N_EXPERTS = 32
D_EXPERT = 512
D_PLE = 256
EPS = 1e-6

LANES = 128
QK_W = GLA_HEADS * GLA_DK
S5_GB = 8
S5_NGB = S5_GROUPS // S5_GB
S5_SL = S5_GB * S5_STATE
TOK_TILE = 256
IN_PROJ_TILE = 512
MOE_BLK = 256
VMEM_LIMIT = 56 * 1024 * 1024
MOE_VMEM_LIMIT = 60 * 1024 * 1024


def _const_spec(shape):
    nd = len(shape)
    return pl.BlockSpec(shape, lambda *_: (0,) * nd, pipeline_mode=pl.Buffered(1))


def _rms(x, g):
    return x * lax.rsqrt(jnp.mean(x * x, axis=-1, keepdims=True) + EPS) * g


def _dot(a, b):
    return jnp.dot(a, b, preferred_element_type=F32)


def _log_sigmoid(x):
    return -(jnp.maximum(-x, 0.0) + jnp.log1p(jnp.exp(-jnp.abs(x))))


N_QKVR = 2 * QK_W + 2 * D_GLA
W_IN_COLS = N_QKVR + GLA_RANK + D_S5


def _w_in_prep_kernel(w_ref, o_ref):
    o_ref[:, 0:N_QKVR] = w_ref[:, 0:N_QKVR].astype(BF16)
    tail = w_ref[:, N_QKVR:W_IN_COLS]
    o_ref[:, N_QKVR:N_QKVR + D_S5] = tail[:, GLA_RANK:GLA_RANK + D_S5].astype(BF16)
    o_ref[:, N_QKVR + D_S5:N_QKVR + D_S5 + LANES] = tail[:, 0:LANES].astype(BF16)


def _w_in_prep(wi):
    rows = 256
    return pl.pallas_call(
        _w_in_prep_kernel,
        grid=(D_MODEL // rows,),
        in_specs=[pl.BlockSpec((rows, W_IN_COLS), lambda i: (i, 0))],
        out_specs=pl.BlockSpec((rows, N_QKVR + D_S5 + LANES), lambda i: (i, 0)),
        out_shape=jax.ShapeDtypeStruct((D_MODEL, N_QKVR + D_S5 + LANES), BF16),
        compiler_params=pltpu.CompilerParams(dimension_semantics=("arbitrary",), vmem_limit_bytes=VMEM_LIMIT),
        name="w_in_prep",
    )(wi)


def _in_proj_kernel(x_ref, g_ref, wm_ref, wgu_ref, gb_ref,
                    q_ref, k_ref, v_ref, r_ref, la_ref, u_ref):
    hb = _rms(x_ref[...], g_ref[...]).astype(BF16)

    def seg(a, b):
        return _dot(hb, wm_ref[:, a:b])

    q_ref[...] = (seg(0, QK_W) * (GLA_DK ** -0.5)).astype(q_ref.dtype)
    k_ref[...] = seg(QK_W, 2 * QK_W).astype(k_ref.dtype)
    v_ref[...] = seg(2 * QK_W, 2 * QK_W + D_GLA).astype(v_ref.dtype)
    r_ref[...] = seg(2 * QK_W + D_GLA, N_QKVR).astype(r_ref.dtype)
    u_ref[...] = seg(N_QKVR, N_QKVR + D_S5)
    zg = seg(N_QKVR + D_S5, N_QKVR + D_S5 + LANES)
    xg = _dot(zg.astype(BF16), wgu_ref[...]) + gb_ref[...]
    la_ref[...] = _log_sigmoid(xg) * (1.0 / 16.0)


def _in_proj(x2d, g, wm, wgu, gbias, act_dtype):
    t = x2d.shape[0]
    tm = IN_PROJ_TILE
    row = lambda w: pl.BlockSpec((tm, w), lambda i: (i, 0))
    return pl.pallas_call(
        _in_proj_kernel,
        grid=(t // tm,),
        in_specs=[row(D_MODEL), _const_spec((1, D_MODEL)), _const_spec(wm.shape),
                  _const_spec(wgu.shape), _const_spec((1, QK_W))],
        out_specs=[row(QK_W), row(QK_W), row(D_GLA), row(D_GLA), row(QK_W), row(D_S5)],
        out_shape=[jax.ShapeDtypeStruct((t, QK_W), act_dtype), jax.ShapeDtypeStruct((t, QK_W), act_dtype),
                   jax.ShapeDtypeStruct((t, D_GLA), act_dtype), jax.ShapeDtypeStruct((t, D_GLA), act_dtype),
                   jax.ShapeDtypeStruct((t, QK_W), F32), jax.ShapeDtypeStruct((t, D_S5), F32)],
        compiler_params=pltpu.CompilerParams(dimension_semantics=("arbitrary",), vmem_limit_bytes=VMEM_LIMIT),
        name="in_proj",
    )(x2d, g, wm, wgu, gbias)


_NT = (((1,), (1,)), ((), ()))
_TN = (((0,), (0,)), ((), ()))


def _gla_pre(q, k, la, c):
    r = q.shape[0]
    shift = int(math.log2(c))
    ri = lax.broadcasted_iota(jnp.int32, (r, r), 0)
    si = lax.broadcasted_iota(jnp.int32, (r, r), 1)
    mask = ((ri >> shift) == (si >> shift)) & (ri >= si)
    tri = jnp.where(mask, 1.0, 0.0).astype(BF16)
    hi = la.astype(BF16)
    r1 = la - hi.astype(F32)
    mid = r1.astype(BF16)
    lo = (r1 - mid.astype(F32)).astype(BF16)
    cum = _dot(tri, hi) + _dot(tri, mid) + _dot(tri, lo)
    last = jnp.concatenate([jnp.broadcast_to(cum[(i + 1) * c - 1:(i + 1) * c, :], (c, cum.shape[1]))
                            for i in range(r // c)], axis=0)
    qe = (q * jnp.exp(cum)).astype(BF16)
    ke = (k * jnp.exp(-cum)).astype(BF16)
    kd = (k * jnp.exp(last - cum)).astype(BF16)
    return qe, ke, kd, cum, mask


def _gla_intra(qe, ke, v, mask):
    sc = lax.dot_general(qe, ke, _NT, preferred_element_type=F32)
    return _dot(jnp.where(mask, sc, 0.0).astype(BF16), v)


def _gla_finish(o, r, g):
    rf = r.astype(F32)
    return _rms(o, g) * (rf * jax.nn.sigmoid(rf))


GLA_SAFE_LOG_DECAY = -60.0


def _gla_token_step(t, rows, q, k, v, a_all, st_ref, oacc_ref):
    m = rows == t
    a = jnp.sum(jnp.where(m, a_all, 0.0), axis=0, keepdims=True)
    kt = jnp.where(m, k, 0.0).astype(BF16)
    qt = jnp.where(m, q, 0.0).astype(BF16)
    vt = jnp.where(m, v, jnp.zeros_like(v))
    st = st_ref[...] * a + lax.dot_general(vt, kt, _TN, preferred_element_type=F32)
    st_ref[...] = st
    oacc_ref[...] += lax.dot_general(qt, st.astype(BF16), _NT, preferred_element_type=F32)


def _gla_prompt_kernel(q_ref, k_ref, v_ref, r_ref, la_ref, g_ref, o_ref, sfin_ref, st_ref, oacc_ref,
                       *, n_chunks):
    j = pl.program_id(1)

    @pl.when(j == 0)
    def _():
        st_ref[...] = jnp.zeros_like(st_ref)

    c = GLA_CHUNK
    qe, ke, kd, cum, mask = _gla_pre(q_ref[...].astype(F32), k_ref[...].astype(F32), la_ref[...], c)
    g = g_ref[...]
    safe = jnp.min(cum) >= GLA_SAFE_LOG_DECAY

    @pl.when(safe)
    def _():
        for h in range(GLA_HEADS):
            kc = slice(h * GLA_DK, (h + 1) * GLA_DK)
            vc = slice(h * GLA_DV, (h + 1) * GLA_DV)
            v = v_ref[:, vc]
            qe_h, kd_h = qe[:, kc], kd[:, kc]
            o = _gla_intra(qe_h, ke[:, kc], v, mask)
            st = st_ref[h]
            inter = []
            for ci in range(n_chunks):
                rows = slice(ci * c, (ci + 1) * c)
                inter.append(lax.dot_general(qe_h[rows], st.astype(BF16), _NT, preferred_element_type=F32))
                dec = jnp.exp(cum[(ci + 1) * c - 1:(ci + 1) * c, kc])
                st = st * dec + lax.dot_general(v[rows], kd_h[rows], _TN, preferred_element_type=F32)
            st_ref[h] = st
            o = o + jnp.concatenate(inter, axis=0)
            o_ref[:, vc] = _gla_finish(o, r_ref[:, vc], g).astype(o_ref.dtype)

    @pl.when(jnp.logical_not(safe))
    def _():
        n_rows = n_chunks * c
        rows = lax.broadcasted_iota(jnp.int32, (n_rows, 1), 0)
        for h in range(GLA_HEADS):
            kc = slice(h * GLA_DK, (h + 1) * GLA_DK)
            vc = slice(h * GLA_DV, (h + 1) * GLA_DV)
            q, k, v = q_ref[:, kc].astype(F32), k_ref[:, kc].astype(F32), v_ref[:, vc]
            a_all = jnp.exp(la_ref[:, kc])
            oacc_ref[...] = jnp.zeros_like(oacc_ref)

            def step(t, carry):
                _gla_token_step(t, rows, q, k, v, a_all, st_ref.at[h], oacc_ref)
                return carry

            lax.fori_loop(0, n_rows, step, 0)
            o_ref[:, vc] = _gla_finish(oacc_ref[...], r_ref[:, vc], g).astype(o_ref.dtype)

    @pl.when(j == pl.num_programs(1) - 1)
    def _():
        for h in range(GLA_HEADS):
            sfin_ref[0, h] = st_ref[h].T


def _gla_prompt(q, k, v, r, la, g, batch, seq):
    rb = 4 * GLA_CHUNK
    nj = seq // rb
    row = lambda w: pl.BlockSpec((rb, w), lambda b, j: (b * nj + j, 0))
    return pl.pallas_call(
        functools.partial(_gla_prompt_kernel, n_chunks=rb // GLA_CHUNK),
        grid=(batch, nj),
        in_specs=[row(QK_W), row(QK_W), row(D_GLA), row(D_GLA), row(QK_W), _const_spec((1, GLA_DV))],
        out_specs=[row(D_GLA),
                   pl.BlockSpec((1, GLA_HEADS, GLA_DK, GLA_DV), lambda b, j: (b, 0, 0, 0))],
        out_shape=[jax.ShapeDtypeStruct((batch * seq, D_GLA), BF16),
                   jax.ShapeDtypeStruct((batch, GLA_HEADS, GLA_DK, GLA_DV), F32)],
        scratch_shapes=[pltpu.VMEM((GLA_HEADS, GLA_DV, GLA_DK), F32), pltpu.VMEM((rb, GLA_DV), F32)],
        compiler_params=pltpu.CompilerParams(dimension_semantics=("arbitrary", "arbitrary"),
                                             vmem_limit_bytes=VMEM_LIMIT),
        name="gla_prompt",
    )(q, k, v, r, la, g)


def _gla_sample_kernel(q_ref, k_ref, v_ref, r_ref, la_ref, g_ref, s0_ref, o_ref, sfin_ref, st_ref, oacc_ref,
                       *, n_seq, seq):
    qe, ke, kd, cum, mask = _gla_pre(q_ref[...].astype(F32), k_ref[...].astype(F32), la_ref[...], seq)
    g = g_ref[...]
    safe = jnp.min(cum) >= GLA_SAFE_LOG_DECAY

    @pl.when(safe)
    def _():
        lasts = jnp.concatenate([cum[(s + 1) * seq - 1:(s + 1) * seq, :] for s in range(n_seq)]
                                + [jnp.zeros((GLA_DK - n_seq, cum.shape[1]), F32)], axis=0)
        pair = 2 * seq
        upper = lax.broadcasted_iota(jnp.int32, (pair, GLA_DK), 0) < seq
        for h in range(GLA_HEADS):
            kc = slice(h * GLA_DK, (h + 1) * GLA_DK)
            vc = slice(h * GLA_DV, (h + 1) * GLA_DV)
            v = v_ref[:, vc]
            qe_h, kd_h = qe[:, kc], kd[:, kc]
            o = _gla_intra(qe_h, ke[:, kc], v, mask)
            dec_t = jnp.exp(lasts[:, kc].T)
            inter = []
            for p in range(n_seq // 2):
                rows = slice(p * pair, (p + 1) * pair)
                qe_p, kd_p, v_p = qe_h[rows], kd_h[rows], v[rows]
                for half in range(2):
                    s = 2 * p + half
                    s0 = s0_ref[s, h]
                    o_s = _dot(qe_p, s0.astype(BF16))
                    inter.append(o_s[half * seq:(half + 1) * seq])
                    kd_s = jnp.where(upper if half == 0 else ~upper, kd_p, jnp.zeros_like(kd_p))
                    dec = jnp.broadcast_to(dec_t[:, s:s + 1], (GLA_DK, GLA_DV))
                    sfin_ref[s, h] = s0 * dec + lax.dot_general(kd_s, v_p, _TN, preferred_element_type=F32)
            o = o + jnp.concatenate(inter, axis=0)
            o_ref[:, vc] = _gla_finish(o, r_ref[:, vc], g).astype(o_ref.dtype)

    @pl.when(jnp.logical_not(safe))
    def _():
        n_rows = n_seq * seq
        rows = lax.broadcasted_iota(jnp.int32, (n_rows, 1), 0)
        for h in range(GLA_HEADS):
            kc = slice(h * GLA_DK, (h + 1) * GLA_DK)
            vc = slice(h * GLA_DV, (h + 1) * GLA_DV)
            q, k, v = q_ref[:, kc].astype(F32), k_ref[:, kc].astype(F32), v_ref[:, vc]
            a_all = jnp.exp(la_ref[:, kc])
            oacc_ref[...] = jnp.zeros_like(oacc_ref)

            def step(t, carry):
                s = t // seq

                @pl.when(t % seq == 0)
                def _():
                    st_ref[...] = s0_ref[s, h].T

                _gla_token_step(t, rows, q, k, v, a_all, st_ref, oacc_ref)

                @pl.when(t % seq == seq - 1)
                def _():
                    sfin_ref[s, h] = st_ref[...].T

                return carry

            lax.fori_loop(0, n_rows, step, 0)
            o_ref[:, vc] = _gla_finish(oacc_ref[...], r_ref[:, vc], g).astype(o_ref.dtype)


def _gla_sample(q, k, v, r, la, g, s0, batch, seq):
    ns = 16
    rb = ns * seq
    row = lambda w: pl.BlockSpec((rb, w), lambda i: (i, 0))
    st = pl.BlockSpec((ns, GLA_HEADS, GLA_DK, GLA_DV), lambda i: (i, 0, 0, 0))
    return pl.pallas_call(
        functools.partial(_gla_sample_kernel, n_seq=ns, seq=seq),
        grid=(batch // ns,),
        in_specs=[row(QK_W), row(QK_W), row(D_GLA), row(D_GLA), row(QK_W), _const_spec((1, GLA_DV)), st],
        out_specs=[row(D_GLA), st],
        out_shape=[jax.ShapeDtypeStruct((batch * seq, D_GLA), BF16),
                   jax.ShapeDtypeStruct((batch, GLA_HEADS, GLA_DK, GLA_DV), F32)],
        scratch_shapes=[pltpu.VMEM((GLA_DV, GLA_DK), F32), pltpu.VMEM((rb, GLA_DV), F32)],
        compiler_params=pltpu.CompilerParams(dimension_semantics=("arbitrary",), vmem_limit_bytes=VMEM_LIMIT),
        name="gla_sample",
    )(q, k, v, r, la, g, s0)


def _s5_kernel(u_ref, wbu_ref, wc_ref, are_ref, aim_ref, d_ref, h0r_ref, h0i_ref,
               y_ref, sre_ref, sim_ref, bu_ref, xs_ref, car_ref, *, nb, tc, has_state):
    j = pl.program_id(2)

    @pl.when(j == 0)
    def _():
        if has_state:
            car_ref[0] = h0r_ref[...]
            car_ref[1] = h0i_ref[...]
        else:
            car_ref[...] = jnp.zeros_like(car_ref)

    u2 = u_ref[...].reshape(nb * tc, LANES)
    ub = u2.astype(BF16)
    nl = S5_SL // LANES
    for l in range(2 * nl):
        bu_ref[l] = _dot(ub, wbu_ref[0, :, l * LANES:(l + 1) * LANES])
    a_r = [jnp.broadcast_to(are_ref[0, :, l * LANES:(l + 1) * LANES], (nb, LANES)) for l in range(nl)]
    a_i = [jnp.broadcast_to(aim_ref[0, :, l * LANES:(l + 1) * LANES], (nb, LANES)) for l in range(nl)]

    def step(t, carry):
        rows = pl.ds(t, nb, stride=tc)
        out = []
        for l in range(nl):
            xr, xi = carry[2 * l], carry[2 * l + 1]
            nr = a_r[l] * xr - a_i[l] * xi + bu_ref[l, rows, :]
            ni = a_r[l] * xi + a_i[l] * xr + bu_ref[nl + l, rows, :]
            xs_ref[l, rows, :] = nr
            xs_ref[nl + l, rows, :] = ni
            out += [nr, ni]
        return tuple(out)

    init = []
    for l in range(nl):
        init += [car_ref[0, :, l * LANES:(l + 1) * LANES], car_ref[1, :, l * LANES:(l + 1) * LANES]]
    fin = lax.fori_loop(0, tc, step, tuple(init), unroll=8)
    xr = jnp.concatenate([fin[2 * l] for l in range(nl)], axis=1)
    xi = jnp.concatenate([fin[2 * l + 1] for l in range(nl)], axis=1)
    car_ref[0] = xr
    car_ref[1] = xi
    y = d_ref[0] * u2
    for l in range(2 * nl):
        y = y + _dot(xs_ref[l].astype(BF16), wc_ref[0, l * LANES:(l + 1) * LANES, :])
    y_ref[...] = y.reshape(nb, tc, LANES)

    @pl.when(j == pl.num_programs(2) - 1)
    def _():
        sre_ref[...] = xr
        sim_ref[...] = xi


def _s5(u3d, wbu, wc, a_re, a_im, dsk, h0r, h0i, nb, tc, has_state):
    batch, seq, _ = u3d.shape
    grid = (S5_NGB, batch // nb, seq // tc)
    st = pl.BlockSpec((nb, S5_SL), lambda g, b, j: (b, g))
    par = lambda w: pl.BlockSpec((1, 1, w), lambda g, b, j: (g, 0, 0))
    ublk = pl.BlockSpec((nb, tc, LANES), lambda g, b, j: (b, j, g))
    return pl.pallas_call(
        functools.partial(_s5_kernel, nb=nb, tc=tc, has_state=has_state),
        grid=grid,
        in_specs=[ublk,
                  pl.BlockSpec((1, LANES, 2 * S5_SL), lambda g, b, j: (g, 0, 0)),
                  pl.BlockSpec((1, 2 * S5_SL, LANES), lambda g, b, j: (g, 0, 0)),
                  par(S5_SL), par(S5_SL), par(LANES), st, st],
        out_specs=[ublk, st, st],
        out_shape=[jax.ShapeDtypeStruct(u3d.shape, F32),
                   jax.ShapeDtypeStruct((batch, S5_GROUPS * S5_STATE), F32),
                   jax.ShapeDtypeStruct((batch, S5_GROUPS * S5_STATE), F32)],
        scratch_shapes=[pltpu.VMEM((2 * S5_SL // LANES, nb * tc, LANES), F32),
                        pltpu.VMEM((2 * S5_SL // LANES, nb * tc, LANES), F32),
                        pltpu.VMEM((2, nb, S5_SL), F32)],
        compiler_params=pltpu.CompilerParams(dimension_semantics=("arbitrary",) * 3,
                                             vmem_limit_bytes=VMEM_LIMIT),
        name="s5_state" if has_state else "s5_zero",
    )(u3d, wbu, wc, a_re, a_im, dsk, h0r, h0i)


S5_J = 2
S5_HL = S5_SL // S5_J


def _s5_prompt_kernel(u_ref, wbu_ref, wc_ref, are_ref, aim_ref, d_ref, y_ref, sre_ref, sim_ref,
                      u2_ref, lhs_ref, bu_ref, xs_ref, y2_ref, yo_ref, car_ref, *, nb, tc):
    g = pl.program_id(0)
    j = pl.program_id(1)
    rows = nb * S5_J

    @pl.when((g == 0) & (j == 0))
    def _():
        lhs_ref[...] = jnp.zeros_like(lhs_ref)

    @pl.when(j == 0)
    def _():
        car_ref[...] = jnp.zeros_like(car_ref)

    u2 = u_ref[...].reshape(nb * tc, LANES)
    u2_ref[...] = u2

    def build(t, c):
        u4 = u2_ref[pl.ds(t, nb, stride=tc), :]
        for jj in range(S5_J):
            lhs_ref[t, jj * nb:(jj + 1) * nb, jj * LANES:(jj + 1) * LANES] = u4
        return c

    lax.fori_loop(0, tc, build, 0, unroll=8)
    lhs = lhs_ref[...].reshape(tc * rows, S5_J * LANES).astype(BF16)
    bu_ref[...] = _dot(lhs, wbu_ref[0]).reshape(tc, rows, 2 * S5_HL)
    ar = are_ref[0]
    ai = aim_ref[0]

    def step(t, carry):
        xr, xi = carry
        tile = bu_ref[t]
        nr = ar * xr - ai * xi + tile[:, 0:S5_HL]
        ni = ar * xi + ai * xr + tile[:, S5_HL:2 * S5_HL]
        xs_ref[t] = jnp.concatenate([nr, ni], axis=1)
        return nr, ni

    xr, xi = lax.fori_loop(0, tc, step, (car_ref[0], car_ref[1]), unroll=8)
    car_ref[0] = xr
    car_ref[1] = xi
    xs = xs_ref[...].reshape(tc * rows, 2 * S5_HL).astype(BF16)
    y2_ref[...] = _dot(xs, wc_ref[0]).reshape(tc, rows, S5_J * LANES)
    first_half = lax.broadcasted_iota(jnp.int32, (rows, LANES), 0) < nb

    def unperm(t, c):
        t2 = y2_ref[t]
        part = jnp.where(first_half, t2[:, 0:LANES], t2[:, LANES:2 * LANES])
        tot = part + pltpu.roll(part, nb, axis=0)
        yo_ref[pl.ds(t, nb, stride=tc), :] = tot[0:nb]
        return c

    lax.fori_loop(0, tc, unperm, 0, unroll=8)
    y_ref[...] = (yo_ref[...] + d_ref[0] * u2).reshape(nb, tc, LANES)

    @pl.when(j == pl.num_programs(1) - 1)
    def _():
        sre_ref[...] = jnp.concatenate([xr[jj * nb:(jj + 1) * nb] for jj in range(S5_J)], axis=1)
        sim_ref[...] = jnp.concatenate([xi[jj * nb:(jj + 1) * nb] for jj in range(S5_J)], axis=1)


def _s5_prompt(u3d, wbu2, wc2, a_re2, a_im2, dsk, tc):
    nb, seq, _ = u3d.shape
    rows = nb * S5_J
    assert rows == 8
    st = pl.BlockSpec((nb, S5_SL), lambda g, j: (0, g))
    gblk = lambda s: pl.BlockSpec((1,) + s, lambda g, j: (g, 0, 0))
    ublk = pl.BlockSpec((nb, tc, LANES), lambda g, j: (0, j, g))
    return pl.pallas_call(
        functools.partial(_s5_prompt_kernel, nb=nb, tc=tc),
        grid=(S5_NGB, seq // tc),
        in_specs=[ublk, gblk((S5_J * LANES, 2 * S5_HL)), gblk((2 * S5_HL, S5_J * LANES)),
                  gblk((rows, S5_HL)), gblk((rows, S5_HL)), gblk((1, LANES))],
        out_specs=[ublk, st, st],
        out_shape=[jax.ShapeDtypeStruct(u3d.shape, F32),
                   jax.ShapeDtypeStruct((nb, S5_GROUPS * S5_STATE), F32),
                   jax.ShapeDtypeStruct((nb, S5_GROUPS * S5_STATE), F32)],
        scratch_shapes=[pltpu.VMEM((nb * tc, LANES), F32),
                        pltpu.VMEM((tc, rows, S5_J * LANES), F32),
                        pltpu.VMEM((tc, rows, 2 * S5_HL), F32),
                        pltpu.VMEM((tc, rows, 2 * S5_HL), F32),
                        pltpu.VMEM((tc, rows, S5_J * LANES), F32),
                        pltpu.VMEM((nb * tc, LANES), F32),
                        pltpu.VMEM((2, rows, S5_HL), F32)],
        compiler_params=pltpu.CompilerParams(dimension_semantics=("arbitrary",) * 2,
                                             vmem_limit_bytes=VMEM_LIMIT),
        name="s5_prompt",
    )(u3d, wbu2, wc2, a_re2, a_im2, dsk)


def _s5_prompt_params(wbu, wc, a_re, a_im, nb):
    h = S5_HL
    top = jnp.concatenate([wbu[:, :, 0:h], wbu[:, :, S5_SL:S5_SL + h]], axis=2)
    bot = jnp.concatenate([wbu[:, :, h:2 * h], wbu[:, :, S5_SL + h:S5_SL + 2 * h]], axis=2)
    wbu2 = jnp.concatenate([top, bot], axis=1)
    wc_j = [jnp.concatenate([wc[:, jj * h:(jj + 1) * h, :], wc[:, S5_SL + jj * h:S5_SL + (jj + 1) * h, :]],
                            axis=1) for jj in range(S5_J)]
    wc2 = jnp.concatenate(wc_j, axis=2)
    tile = lambda a: jnp.repeat(a.reshape(S5_NGB, S5_J, h), nb, axis=1)
    return wbu2, wc2, tile(a_re), tile(a_im)


def _mix_out_kernel(xp_ref, xs_ref, ogp_ref, ogs_ref, ysp_ref, yss_ref, glu_w_ref, glu_b_ref, s5n_ref, wo_ref,
                    nffn_ref, wr_ref, h1_ref, hn_ref, rt_ref, *, n_prompt_tiles):
    is_p = pl.program_id(0) < n_prompt_tiles
    x = jnp.where(is_p, xp_ref[...], xs_ref[...])
    og = jnp.where(is_p, ogp_ref[...], ogs_ref[...])
    y = jax.nn.gelu(jnp.where(is_p, ysp_ref[...], yss_ref[...]))
    y = y * jax.nn.sigmoid(_dot(y.astype(BF16), glu_w_ref[...]) + glu_b_ref[...])
    yn = _rms(y, s5n_ref[...]).astype(BF16)
    mix = _dot(og, wo_ref[0:D_GLA, :]) + _dot(yn, wo_ref[D_GLA:D_GLA + D_S5, :])
    h1 = x + mix
    h1_ref[...] = h1
    hn = _rms(h1, nffn_ref[...])
    hn_ref[...] = _pack_bf16_pair(hn[:, 0:HALF], hn[:, HALF:D_MODEL])
    hn_hi = hn.astype(BF16)
    hn_lo = (hn - hn_hi.astype(F32)).astype(BF16)
    logits = _dot(hn_hi, wr_ref[0]) + _dot(hn_hi, wr_ref[1]) + _dot(hn_lo, wr_ref[0])
    rt_ref[...] = _route(logits)


def _route(logits):
    col = lax.broadcasted_iota(jnp.int32, logits.shape, 1)
    colf = col.astype(F32)
    neg = -jnp.inf

    def first_argmax(vals):
        m = jnp.max(vals, axis=-1, keepdims=True)
        idx = jnp.min(jnp.where(vals == m, colf, float(LANES)), axis=-1, keepdims=True)
        return m, idx

    lg = jnp.where(col < N_EGROUPS, logits, neg)
    gmax, gsel = first_argmax(lg)
    p_g = 1.0 / jnp.sum(jnp.exp(lg - gmax), axis=-1, keepdims=True)
    ecol = col - N_EGROUPS
    egrp = (ecol >> 3).astype(F32)
    in_group = (ecol >= 0) & (ecol < N_EXPERTS) & (egrp == gsel)
    le = jnp.where(in_group, logits, neg)
    m1, i1 = first_argmax(le)
    le2 = jnp.where(colf == i1, neg, le)
    m2, i2 = first_argmax(le2)
    e2 = jnp.exp(m2 - m1)
    den = 1.0 + e2
    w1 = p_g * (1.0 / den)
    w2 = p_g * (e2 / den)
    e1f = i1 - float(N_EGROUPS)
    e2f = i2 - float(N_EGROUPS)
    out = jnp.where(col == 0, e1f, jnp.where(col == 1, e2f, jnp.where(col == 2, w1, jnp.where(col == 3, w2, 0.0))))
    return out


def _mix_out(xp, xs, ogp, ogs, ysp, yss, glu_w, glu_b, s5n, wo, nffn, wr):
    tm = TOK_TILE
    npt, nst = xp.shape[0] // tm, xs.shape[0] // tm
    t = (npt + nst) * tm
    row = lambda w: pl.BlockSpec((tm, w), lambda i: (i, 0))
    prow = lambda w: pl.BlockSpec((tm, w), lambda i: (jnp.minimum(i, npt - 1), 0))
    srow = lambda w: pl.BlockSpec((tm, w), lambda i: (jnp.maximum(i - npt, 0), 0))
    return pl.pallas_call(
        functools.partial(_mix_out_kernel, n_prompt_tiles=npt),
        grid=(npt + nst,),
        in_specs=[prow(D_MODEL), srow(D_MODEL), prow(D_GLA), srow(D_GLA), prow(D_S5), srow(D_S5),
                  _const_spec(glu_w.shape), _const_spec((1, D_S5)), _const_spec((1, D_S5)),
                  _const_spec(wo.shape), _const_spec((1, D_MODEL)), _const_spec(wr.shape)],
        out_specs=[row(D_MODEL), row(HALF), row(LANES)],
        out_shape=[jax.ShapeDtypeStruct((t, D_MODEL), F32), jax.ShapeDtypeStruct((t, HALF), jnp.uint32),
                   jax.ShapeDtypeStruct((t, LANES), F32)],
        compiler_params=pltpu.CompilerParams(dimension_semantics=("arbitrary",), vmem_limit_bytes=VMEM_LIMIT),
        name="mix_out",
    )(xp, xs, ogp, ogs, ysp, yss, glu_w, glu_b, s5n, wo, nffn, wr)


DMA_UNROLL = 8


MOE_SB = 3
HALF = D_MODEL // 2


def _pack_bf16_pair(lo, hi):
    def bits(x):
        b = pltpu.bitcast(x, jnp.uint32)
        return (b + jnp.uint32(0x7FFF) + ((b >> 16) & jnp.uint32(1))) >> 16
    return bits(lo) | (bits(hi) << 16)


def _unpack_bf16_pair(u):
    return (pltpu.bitcast(u << 16, F32), pltpu.bitcast(u & jnp.uint32(0xFFFF0000), F32))


def _moe_kernel(se_ref, sm_ref, sn_ref, sbase_ref, sa_ref, hn_hbm, wg_ref, wu_ref, wd_ref, out_hbm,
                xb_ref, yb_ref, wgb_ref, wub_ref, wdb_ref, gsem, ssem):
    b = pl.program_id(0)
    nb = pl.num_programs(0)
    n = sn_ref[b]
    slot = lax.rem(b, 2)

    def round_up(cnt):
        return (cnt + DMA_UNROLL - 1) // DMA_UNROLL * DMA_UNROLL

    def gather_row(base, i, sl):
        tok = sa_ref[base + i] >> 1
        pltpu.make_async_copy(hn_hbm.at[pl.ds(tok, 1), :], xb_ref.at[sl, pl.ds(i, 1), :], gsem.at[sl]).start()

    def issue_gather_loop(blk, sl, first, last):
        base = sbase_ref[blk]

        def grp(g, c):
            for j in range(DMA_UNROLL):
                gather_row(base, g * DMA_UNROLL + j, sl)
            return c

        lax.fori_loop(first // DMA_UNROLL, last // DMA_UNROLL, grp, 0)

    def wait_gather(cnt, sl):
        rows = pl.ds(0, pl.multiple_of(cnt, DMA_UNROLL))
        pltpu.make_async_copy(hn_hbm.at[rows, :], xb_ref.at[sl, rows, :], gsem.at[sl]).wait()

    m_cur = sm_ref[b]
    prev = jnp.maximum(b - 1, 0)
    nxt = jnp.minimum(b + 1, nb - 1)
    m_prev = sm_ref[prev]
    has_prev = (b > 0) & (m_prev > 0)
    n_in = jnp.where((b > 0) & (sn_ref[prev] > 0), jnp.maximum(m_prev * MOE_BLK, round_up(n)), round_up(n))

    def out_copy(rows):
        dst = pl.ds(pl.multiple_of(sbase_ref[b], MOE_BLK), rows)
        return pltpu.make_async_copy(yb_ref.at[pl.ds(0, rows), :], out_hbm.at[dst, :], ssem.at[0])

    def wait_prev_out():
        rows = pl.ds(0, pl.multiple_of(m_prev * MOE_BLK, MOE_BLK))
        pltpu.make_async_copy(yb_ref.at[rows, :], out_hbm.at[rows, :], ssem.at[0]).wait()

    def write_out(rows):
        out_copy(rows).start()

        @pl.when(b == nb - 1)
        def _():
            out_copy(rows).wait()

    @pl.when(b == 0)
    def _():
        xb_ref[...] = jnp.zeros_like(xb_ref)
        issue_gather_loop(0, 0, 0, round_up(n))

    def compute(rows):
        nbase = sbase_ref[nxt]
        pieces = 8
        per = rows // pieces
        issued = [0]

        def issue_some():
            for i in range(issued[0], issued[0] + per):
                gather_row(nbase, i, 1 - slot)
            issued[0] += per

        x_lo, x_hi = _unpack_bf16_pair(xb_ref[slot, 0:rows, :])
        x_lo, x_hi = x_lo.astype(BF16), x_hi.astype(BF16)
        halves = []
        hw = D_EXPERT // 2
        for w_ref in (wgb_ref, wub_ref):
            cols = []
            for c in range(2):
                cs = slice(c * hw, (c + 1) * hw)
                cols.append(_dot(x_lo, w_ref[0:HALF, cs]) + _dot(x_hi, w_ref[HALF:D_MODEL, cs]))
                issue_some()
            halves.append(jnp.concatenate(cols, axis=1))
        gate, up = halves
        hid = (gate * jax.nn.sigmoid(gate) * up).astype(BF16)

        @pl.when(has_prev)
        def _():
            wait_prev_out()

        qw = HALF // 4
        for c in range(4):
            yb_ref[0:rows, c * qw:(c + 1) * qw] = _pack_bf16_pair(
                _dot(hid, wdb_ref[:, c * qw:(c + 1) * qw]), _dot(hid, wdb_ref[:, HALF + c * qw:HALF + (c + 1) * qw]))
            issue_some()
        write_out(rows)
        n_next = jnp.where(b + 1 < nb, round_up(sn_ref[nxt]), 0)
        issue_gather_loop(nxt, 1 - slot, rows, jnp.maximum(n_next, rows))

        @pl.when(b == nb - 1)
        def _():
            wait_gather(rows, 1 - slot)

    @pl.when(n > 0)
    def _():
        prev_e = se_ref[jnp.maximum(b - 1, 0)]

        @pl.when((b == 0) | (prev_e != se_ref[b]))
        def _():
            wgb_ref[...] = wg_ref[...].astype(BF16)
            wub_ref[...] = wu_ref[...].astype(BF16)
            wdb_ref[...] = wd_ref[...].astype(BF16)

        wait_gather(n_in, slot)
        for m in range(1, MOE_SB + 1):
            @pl.when(m_cur == m)
            def _():
                compute(m * MOE_BLK)

    @pl.when((n == 0) & (n_in > 0))
    def _():
        wait_gather(n_in, slot)

    @pl.when((n == 0) & (m_cur > 0))
    def _():
        @pl.when(has_prev)
        def _():
            wait_prev_out()

        yb_ref[...] = jnp.zeros_like(yb_ref)
        for m in range(1, MOE_SB + 1):
            @pl.when(m_cur == m)
            def _():
                write_out(m * MOE_BLK)

    @pl.when((m_cur == 0) & has_prev)
    def _():
        wait_prev_out()


def _moe(hn_pk, plan, w_gate, w_up, w_down, n_out_rows):
    sb_e, sb_m, sb_n, sb_base, slot_a = plan
    sb_rows = MOE_SB * MOE_BLK
    wspec = lambda s: pl.BlockSpec((None,) + s, lambda b, se, *_: (se[b], 0, 0))
    grid_spec = pltpu.PrefetchScalarGridSpec(
        num_scalar_prefetch=5,
        grid=(sb_e.shape[0],),
        in_specs=[pl.BlockSpec(memory_space=pl.ANY),
                  wspec((D_MODEL, D_EXPERT)), wspec((D_MODEL, D_EXPERT)), wspec((D_EXPERT, D_MODEL))],
        out_specs=pl.BlockSpec(memory_space=pl.ANY),
        scratch_shapes=[pltpu.VMEM((2, sb_rows, HALF), jnp.uint32), pltpu.VMEM((sb_rows, HALF), jnp.uint32),
                        pltpu.VMEM((D_MODEL, D_EXPERT), BF16), pltpu.VMEM((D_MODEL, D_EXPERT), BF16),
                        pltpu.VMEM((D_EXPERT, D_MODEL), BF16), pltpu.SemaphoreType.DMA((2,)),
                        pltpu.SemaphoreType.DMA((1,))],
    )
    return pl.pallas_call(
        _moe_kernel,
        grid_spec=grid_spec,
        out_shape=jax.ShapeDtypeStruct((n_out_rows, HALF), jnp.uint32),
        compiler_params=pltpu.CompilerParams(dimension_semantics=("arbitrary",),
                                             vmem_limit_bytes=MOE_VMEM_LIMIT),
        name="moe_experts",
    )(sb_e, sb_m, sb_n, sb_base, slot_a, hn_pk, w_gate, w_up, w_down)


def _moe_plan(rt_all):
    t_all = rt_all.shape[0]
    n_assign = 2 * t_all
    e_flat = rt_all[:, 0:2].astype(jnp.int32).reshape(-1)
    onehot = (e_flat[:, None] == jnp.arange(N_EXPERTS, dtype=jnp.int32)[None, :]).astype(jnp.int32)
    csum = jnp.cumsum(onehot, axis=0)
    counts = csum[-1]
    rank = jnp.take_along_axis(csum, e_flat[:, None], axis=1)[:, 0] - 1
    padded = (counts + MOE_BLK - 1) // MOE_BLK * MOE_BLK
    pad_end = jnp.cumsum(padded)
    pad_start = pad_end - padded
    dest = pad_start[e_flat] + rank
    nblk = -(-(n_assign + N_EXPERTS * (MOE_BLK - 1)) // MOE_BLK)
    n_slots = nblk * MOE_BLK
    slot_a = jnp.zeros((n_slots + MOE_SB * MOE_BLK,), jnp.int32).at[dest].set(
        jnp.arange(n_assign, dtype=jnp.int32))
    k_e = padded // MOE_BLK
    sbc = (k_e + MOE_SB - 1) // MOE_SB
    sb_end = jnp.cumsum(sbc)
    sb_start = sb_end - sbc
    n_sb = (nblk + (MOE_SB - 1) * N_EXPERTS) // MOE_SB
    s = jnp.arange(n_sb, dtype=jnp.int32)
    sb_e = jnp.minimum(jnp.sum((s[:, None] >= sb_end[None, :]).astype(jnp.int32), axis=1), N_EXPERTS - 1)
    j = s - sb_start[sb_e]
    real = s < sb_end[-1]
    tail_blk = pad_end[-1] // MOE_BLK + MOE_SB * (s - sb_end[-1])
    sb_m = jnp.where(real, jnp.clip(k_e[sb_e] - MOE_SB * j, 0, MOE_SB), jnp.clip(nblk - tail_blk, 0, MOE_SB))
    sb_n = jnp.where(real, jnp.clip(counts[sb_e] - MOE_SB * MOE_BLK * j, 0, MOE_SB * MOE_BLK), 0)
    sb_base = jnp.where(real, pad_start[sb_e] + MOE_SB * MOE_BLK * j, jnp.minimum(tail_blk, nblk - 1) * MOE_BLK)
    i32 = lambda a: a.astype(jnp.int32)
    return (i32(sb_e), i32(sb_m), i32(sb_n), i32(sb_base), slot_a), i32(dest), n_slots


def _ple_out_kernel(dest_ref, h1_ref, rt_ref, pp_ref, ps_ref, nple_ref, wpg_ref, wp_ref, nfin_ref, ys_hbm,
                    op_ref, os_ref, yb_ref, sem, *, n_prompt_tiles):
    i = pl.program_id(0)
    nt = pl.num_programs(0)
    tm = h1_ref.shape[0]
    slot = lax.rem(i, 2)

    def issue(tile, sl):
        base = tile * (2 * tm)

        def grp(g, c):
            for j in range(DMA_UNROLL):
                d = dest_ref[base + g * DMA_UNROLL + j]
                row = g * (DMA_UNROLL // 2) + j // 2
                pltpu.make_async_copy(ys_hbm.at[pl.ds(d, 1), :], yb_ref.at[sl, j % 2, pl.ds(row, 1), :],
                                      sem.at[sl]).start()
            return c

        lax.fori_loop(0, 2 * tm // DMA_UNROLL, grp, 0)

    @pl.when(i == 0)
    def _():
        issue(0, 0)

    def wait_rows(sl):
        for k in range(2):
            pltpu.make_async_copy(ys_hbm.at[pl.ds(0, tm), :], yb_ref.at[sl, k], sem.at[sl]).wait()

    wait_rows(slot)
    is_p = i < n_prompt_tiles
    rt = rt_ref[...]
    w0, w1 = rt[:, 2:3], rt[:, 3:4]
    lo0, hi0 = _unpack_bf16_pair(yb_ref[slot, 0])
    lo1, hi1 = _unpack_bf16_pair(yb_ref[slot, 1])
    h2 = h1_ref[...] + jnp.concatenate([w0 * lo0 + w1 * lo1, w0 * hi0 + w1 * hi1], axis=1)
    hb = _rms(h2, nple_ref[...]).astype(BF16)
    nbase = jnp.minimum(i + 1, nt - 1) * (2 * tm)
    n_chunk = 8
    cw, ca = D_MODEL // n_chunk, 2 * tm // n_chunk
    gates = []
    for c in range(n_chunk):
        gates.append(jax.nn.sigmoid(_dot(hb, wpg_ref[:, c * cw:(c + 1) * cw])))
        for a in range(c * ca, (c + 1) * ca):
            d = dest_ref[nbase + a]
            pltpu.make_async_copy(ys_hbm.at[pl.ds(d, 1), :], yb_ref.at[1 - slot, a % 2, pl.ds(a // 2, 1), :],
                                  sem.at[1 - slot]).start()
    gate = jnp.concatenate(gates, axis=1)

    @pl.when(i == nt - 1)
    def _():
        wait_rows(1 - slot)

    p = jnp.where(is_p, pp_ref[...], ps_ref[...])
    h3 = h2 + _dot(p.astype(BF16), wp_ref[...]) * gate
    y = _rms(h3, nfin_ref[...])

    @pl.when(is_p)
    def _():
        op_ref[...] = y

    @pl.when(jnp.logical_not(is_p))
    def _():
        os_ref[...] = y


def _ple_out(dest, h1, rt, pp, ps, nple, wpg, wp, nfin, y_slots):
    tm = TOK_TILE
    npt, nst = pp.shape[0] // tm, ps.shape[0] // tm
    row = lambda w: pl.BlockSpec((tm, w), lambda i, d: (i, 0))
    prow = lambda w: pl.BlockSpec((tm, w), lambda i, d: (jnp.minimum(i, npt - 1), 0))
    srow = lambda w: pl.BlockSpec((tm, w), lambda i, d: (jnp.maximum(i - npt, 0), 0))
    const = lambda shape: pl.BlockSpec(shape, lambda i, d: (0,) * len(shape), pipeline_mode=pl.Buffered(1))
    grid_spec = pltpu.PrefetchScalarGridSpec(
        num_scalar_prefetch=1,
        grid=(npt + nst,),
        in_specs=[row(D_MODEL), row(LANES), prow(D_PLE), srow(D_PLE), const((1, D_MODEL)), const(wpg.shape),
                  const(wp.shape), const((1, D_MODEL)), pl.BlockSpec(memory_space=pl.ANY)],
        out_specs=[prow(D_MODEL), srow(D_MODEL)],
        scratch_shapes=[pltpu.VMEM((2, 2, tm, HALF), jnp.uint32), pltpu.SemaphoreType.DMA((2,))],
    )
    return pl.pallas_call(
        functools.partial(_ple_out_kernel, n_prompt_tiles=npt),
        grid_spec=grid_spec,
        out_shape=[jax.ShapeDtypeStruct((pp.shape[0], D_MODEL), F32),
                   jax.ShapeDtypeStruct((ps.shape[0], D_MODEL), F32)],
        compiler_params=pltpu.CompilerParams(dimension_semantics=("arbitrary",), vmem_limit_bytes=VMEM_LIMIT),
        name="ple_out",
    )(dest, h1, rt, pp, ps, nple, wpg, wp, nfin, y_slots)


def _s5_params(lam_re, lam_im, log_dt, b_re, b_im, c_re, c_im, d_skip):
    dt = jnp.exp(log_dt)[:, None]
    mag = jnp.exp(lam_re * dt)
    ab_re = mag * jnp.cos(lam_im * dt)
    ab_im = mag * jnp.sin(lam_im * dt)
    den = lam_re * lam_re + lam_im * lam_im
    nr = ab_re - 1.0
    f_re = (nr * lam_re + ab_im * lam_im) / den
    f_im = (ab_im * lam_re - nr * lam_im) / den
    bb_re = f_re[..., None] * b_re - f_im[..., None] * b_im
    bb_im = f_re[..., None] * b_im + f_im[..., None] * b_re
    eye = jnp.eye(S5_GB, dtype=F32)

    def bu_w(bb):
        bb = bb.reshape(S5_NGB, S5_GB, S5_STATE, S5_GROUP)
        w = jnp.einsum('nlph,lm->nlhmp', bb, eye)
        return w.reshape(S5_NGB, S5_GB * S5_GROUP, S5_SL)

    def c_w(c):
        c = c.reshape(S5_NGB, S5_GB, S5_GROUP, S5_STATE)
        w = jnp.einsum('nlhp,lm->nlpmh', c, eye)
        return w.reshape(S5_NGB, S5_SL, S5_GB * S5_GROUP)

    wbu = jnp.concatenate([bu_w(bb_re), bu_w(bb_im)], axis=2).astype(BF16)
    wc = jnp.concatenate([c_w(c_re), -c_w(c_im)], axis=1).astype(BF16)
    a_re = ab_re.reshape(S5_NGB, 1, S5_SL)
    a_im = ab_im.reshape(S5_NGB, 1, S5_SL)
    dsk = d_skip.reshape(S5_NGB, 1, S5_GB * S5_GROUP)
    return wbu, wc, a_re, a_im, dsk


def kernel(x_prompt, x_sample, p_prompt, p_sample, state_gla, state_s5_re, state_s5_im, norm_mix, w_in, gla_w_gate_up, gla_gate_bias, gla_norm, s5_lam_re, s5_lam_im, s5_log_dt, s5_b_re, s5_b_im, s5_c_re, s5_c_im, s5_d, s5_glu_w, s5_glu_b, s5_norm, w_out, norm_ffn, router_group, router_expert, w_gate, w_up, w_down, norm_ple, w_ple, w_ple_gate, norm_final):
    depth = w_in.shape[0]
    assert depth == 1
    i = 0
    bp, lp, _ = x_prompt.shape
    bs, ls, _ = x_sample.shape
    tp, ts = bp * lp, bs * ls
    t_all = tp + ts

    wm = _w_in_prep(w_in[i])
    wgu = jnp.pad(gla_w_gate_up[i], ((0, LANES - GLA_RANK), (0, 0))).astype(BF16)
    gbias = gla_gate_bias[i].reshape(1, QK_W)
    wbu, wc, a_re, a_im, dsk = _s5_params(s5_lam_re[i], s5_lam_im[i], s5_log_dt[i], s5_b_re[i], s5_b_im[i],
                                          s5_c_re[i], s5_c_im[i], s5_d[i])
    glu_w = s5_glu_w[i].astype(BF16)
    wo = w_out[i].astype(BF16)
    wr32 = jnp.pad(jnp.concatenate([router_group[i], router_expert[i]], axis=1),
                   ((0, 0), (0, LANES - N_EGROUPS - N_EXPERTS)))
    wr_hi = wr32.astype(BF16)
    wr = jnp.stack([wr_hi, (wr32 - wr_hi.astype(F32)).astype(BF16)])
    wpg = w_ple_gate[i].astype(BF16)
    wp = w_ple[i].astype(BF16)
    vec = lambda a: a.reshape(1, -1)

    xp = x_prompt.reshape(tp, D_MODEL)
    xs = x_sample.reshape(ts, D_MODEL)

    qp, kp, vp, rp, lap, up = _in_proj(xp, vec(norm_mix[i]), wm, wgu, gbias, BF16)
    qs, ks, vs, rs, las, us = _in_proj(xs, vec(norm_mix[i]), wm, wgu, gbias, BF16)
    ogp, gla_p = _gla_prompt(qp, kp, vp, rp, lap, vec(gla_norm[i]), bp, lp)
    ogs, gla_s = _gla_sample(qs, ks, vs, rs, las, vec(gla_norm[i]), state_gla[i], bs, ls)
    wbu2, wc2, a_re2, a_im2 = _s5_prompt_params(wbu, wc, a_re, a_im, bp)
    ysp, re_p, im_p = _s5_prompt(up.reshape(bp, lp, D_S5), wbu2, wc2, a_re2, a_im2, dsk, tc=256)
    yss, re_s, im_s = _s5(us.reshape(bs, ls, D_S5), wbu, wc, a_re, a_im, dsk,
                          state_s5_re[i].reshape(bs, -1), state_s5_im[i].reshape(bs, -1),
                          nb=32, tc=ls, has_state=True)

    h1, hn_all, rt_all = _mix_out(xp, xs, ogp, ogs, ysp.reshape(tp, D_S5), yss.reshape(ts, D_S5), glu_w,
                                  vec(s5_glu_b[i]), vec(s5_norm[i]), wo, vec(norm_ffn[i]), wr)

    plan, dest, n_slots = _moe_plan(rt_all)
    y_slots = _moe(hn_all, plan, w_gate[i], w_up[i], w_down[i], n_slots)

    y_p, y_s = _ple_out(dest, h1, rt_all, p_prompt[i].reshape(tp, D_PLE), p_sample[i].reshape(ts, D_PLE),
                        vec(norm_ple[i]), wpg, wp, vec(norm_final), y_slots)

    s5shape = lambda a, b: a.reshape(1, b, S5_GROUPS, S5_STATE)
    return (y_p.reshape(bp, lp, D_MODEL), y_s.reshape(bs, ls, D_MODEL),
            gla_p[None], s5shape(re_p, bp), s5shape(im_p, bp),
            gla_s[None], s5shape(re_s, bs), s5shape(im_s, bs))
```

```python
import functools
import math

import jax
import jax.numpy as jnp
from jax import lax
from jax.experimental import pallas as pl
from jax.experimental.pallas import tpu as pltpu

F32 = jnp.float32
BF16 = jnp.bfloat16

D_MODEL = 2048
D_GLA = 1024
D_S5 = 1024
GLA_HEADS = 4
GLA_DV = 256
GLA_DK = 128
GLA_RANK = 16
GLA_CHUNK = 64
S5_GROUP = 16
S5_GROUPS = 64
S5_STATE = 64
N_EGROUPS = 4
N_EPG = 8
N_EXPERTS = 32
D_EXPERT = 512
D_PLE = 256
EPS = 1e-6

LANES = 128
QK_W = GLA_HEADS * GLA_DK
S5_GB = 8
S5_NGB = S5_GROUPS // S5_GB
S5_SL = S5_GB * S5_STATE
TOK_TILE = 256
MOE_BLK = 256
VMEM_LIMIT = 56 * 1024 * 1024
MOE_VMEM_LIMIT = 60 * 1024 * 1024


def _const_spec(shape):
    nd = len(shape)
    return pl.BlockSpec(shape, lambda *_: (0,) * nd, pipeline_mode=pl.Buffered(1))


def _rms(x, g):
    return x * lax.rsqrt(jnp.mean(x * x, axis=-1, keepdims=True) + EPS) * g


def _dot(a, b):
    return jnp.dot(a, b, preferred_element_type=F32)


def _log_sigmoid(x):
    return -(jnp.maximum(-x, 0.0) + jnp.log1p(jnp.exp(-jnp.abs(x))))


N_QKVR = 2 * QK_W + 2 * D_GLA
W_IN_COLS = N_QKVR + GLA_RANK + D_S5


def _w_in_prep_kernel(w_ref, o_ref):
    o_ref[:, 0:N_QKVR] = w_ref[:, 0:N_QKVR].astype(BF16)
    tail = w_ref[:, N_QKVR:W_IN_COLS]
    o_ref[:, N_QKVR:N_QKVR + D_S5] = tail[:, GLA_RANK:GLA_RANK + D_S5].astype(BF16)
    o_ref[:, N_QKVR + D_S5:N_QKVR + D_S5 + LANES] = tail[:, 0:LANES].astype(BF16)


def _w_in_prep(wi):
    rows = 256
    return pl.pallas_call(
        _w_in_prep_kernel,
        grid=(D_MODEL // rows,),
        in_specs=[pl.BlockSpec((rows, W_IN_COLS), lambda i: (i, 0))],
        out_specs=pl.BlockSpec((rows, N_QKVR + D_S5 + LANES), lambda i: (i, 0)),
        out_shape=jax.ShapeDtypeStruct((D_MODEL, N_QKVR + D_S5 + LANES), BF16),
        compiler_params=pltpu.CompilerParams(dimension_semantics=("arbitrary",), vmem_limit_bytes=VMEM_LIMIT),
        name="w_in_prep",
    )(wi)


def _in_proj_kernel(x_ref, g_ref, wm_ref, wgu_ref, gb_ref,
                    q_ref, k_ref, v_ref, r_ref, la_ref, u_ref):
    hb = _rms(x_ref[...], g_ref[...]).astype(BF16)

    def seg(a, b):
        return _dot(hb, wm_ref[:, a:b])

    q_ref[...] = (seg(0, QK_W) * (GLA_DK ** -0.5)).astype(q_ref.dtype)
    k_ref[...] = seg(QK_W, 2 * QK_W).astype(k_ref.dtype)
    v_ref[...] = seg(2 * QK_W, 2 * QK_W + D_GLA).astype(v_ref.dtype)
    r_ref[...] = seg(2 * QK_W + D_GLA, N_QKVR).astype(r_ref.dtype)
    u_ref[...] = seg(N_QKVR, N_QKVR + D_S5)
    zg = seg(N_QKVR + D_S5, N_QKVR + D_S5 + LANES)
    xg = _dot(zg.astype(BF16), wgu_ref[...]) + gb_ref[...]
    la_ref[...] = _log_sigmoid(xg) * (1.0 / 16.0)


def _in_proj(x2d, g, wm, wgu, gbias, act_dtype):
    t = x2d.shape[0]
    tm = TOK_TILE
    row = lambda w: pl.BlockSpec((tm, w), lambda i: (i, 0))
    return pl.pallas_call(
        _in_proj_kernel,
        grid=(t // tm,),
        in_specs=[row(D_MODEL), _const_spec((1, D_MODEL)), _const_spec(wm.shape),
                  _const_spec(wgu.shape), _const_spec((1, QK_W))],
        out_specs=[row(QK_W), row(QK_W), row(D_GLA), row(D_GLA), row(QK_W), row(D_S5)],
        out_shape=[jax.ShapeDtypeStruct((t, QK_W), act_dtype), jax.ShapeDtypeStruct((t, QK_W), act_dtype),
                   jax.ShapeDtypeStruct((t, D_GLA), act_dtype), jax.ShapeDtypeStruct((t, D_GLA), act_dtype),
                   jax.ShapeDtypeStruct((t, QK_W), F32), jax.ShapeDtypeStruct((t, D_S5), F32)],
        compiler_params=pltpu.CompilerParams(dimension_semantics=("arbitrary",), vmem_limit_bytes=VMEM_LIMIT),
        name="in_proj",
    )(x2d, g, wm, wgu, gbias)


_NT = (((1,), (1,)), ((), ()))
_TN = (((0,), (0,)), ((), ()))


def _gla_pre(q, k, la, c):
    r = q.shape[0]
    shift = int(math.log2(c))
    ri = lax.broadcasted_iota(jnp.int32, (r, r), 0)
    si = lax.broadcasted_iota(jnp.int32, (r, r), 1)
    mask = ((ri >> shift) == (si >> shift)) & (ri >= si)
    tri = jnp.where(mask, 1.0, 0.0).astype(BF16)
    hi = la.astype(BF16)
    r1 = la - hi.astype(F32)
    mid = r1.astype(BF16)
    lo = (r1 - mid.astype(F32)).astype(BF16)
    cum = _dot(tri, hi) + _dot(tri, mid) + _dot(tri, lo)
    last = jnp.concatenate([jnp.broadcast_to(cum[(i + 1) * c - 1:(i + 1) * c, :], (c, cum.shape[1]))
                            for i in range(r // c)], axis=0)
    qe = (q * jnp.exp(cum)).astype(BF16)
    ke = (k * jnp.exp(-cum)).astype(BF16)
    kd = (k * jnp.exp(last - cum)).astype(BF16)
    return qe, ke, kd, cum, mask


def _gla_intra(qe, ke, v, mask):
    sc = lax.dot_general(qe, ke, _NT, preferred_element_type=F32)
    return _dot(jnp.where(mask, sc, 0.0).astype(BF16), v)


def _gla_finish(o, r, g):
    rf = r.astype(F32)
    return _rms(o, g) * (rf * jax.nn.sigmoid(rf))


GLA_SAFE_LOG_DECAY = -60.0


def _gla_token_step(t, rows, q, k, v, a_all, st_ref, oacc_ref):
    m = rows == t
    a = jnp.sum(jnp.where(m, a_all, 0.0), axis=0, keepdims=True)
    kt = jnp.where(m, k, 0.0).astype(BF16)
    qt = jnp.where(m, q, 0.0).astype(BF16)
    vt = jnp.where(m, v, jnp.zeros_like(v))
    st = st_ref[...] * a + lax.dot_general(vt, kt, _TN, preferred_element_type=F32)
    st_ref[...] = st
    oacc_ref[...] += lax.dot_general(qt, st.astype(BF16), _NT, preferred_element_type=F32)


def _gla_prompt_kernel(q_ref, k_ref, v_ref, r_ref, la_ref, g_ref, o_ref, sfin_ref, st_ref, oacc_ref,
                       *, n_chunks):
    j = pl.program_id(1)

    @pl.when(j == 0)
    def _():
        st_ref[...] = jnp.zeros_like(st_ref)

    c = GLA_CHUNK
    qe, ke, kd, cum, mask = _gla_pre(q_ref[...].astype(F32), k_ref[...].astype(F32), la_ref[...], c)
    g = g_ref[...]
    safe = jnp.min(cum) >= GLA_SAFE_LOG_DECAY

    @pl.when(safe)
    def _():
        for h in range(GLA_HEADS):
            kc = slice(h * GLA_DK, (h + 1) * GLA_DK)
            vc = slice(h * GLA_DV, (h + 1) * GLA_DV)
            v = v_ref[:, vc]
            qe_h, kd_h = qe[:, kc], kd[:, kc]
            o = _gla_intra(qe_h, ke[:, kc], v, mask)
            st = st_ref[h]
            inter = []
            for ci in range(n_chunks):
                rows = slice(ci * c, (ci + 1) * c)
                inter.append(lax.dot_general(qe_h[rows], st.astype(BF16), _NT, preferred_element_type=F32))
                dec = jnp.exp(cum[(ci + 1) * c - 1:(ci + 1) * c, kc])
                st = st * dec + lax.dot_general(v[rows], kd_h[rows], _TN, preferred_element_type=F32)
            st_ref[h] = st
            o = o + jnp.concatenate(inter, axis=0)
            o_ref[:, vc] = _gla_finish(o, r_ref[:, vc], g).astype(o_ref.dtype)

    @pl.when(jnp.logical_not(safe))
    def _():
        n_rows = n_chunks * c
        rows = lax.broadcasted_iota(jnp.int32, (n_rows, 1), 0)
        for h in range(GLA_HEADS):
            kc = slice(h * GLA_DK, (h + 1) * GLA_DK)
            vc = slice(h * GLA_DV, (h + 1) * GLA_DV)
            q, k, v = q_ref[:, kc].astype(F32), k_ref[:, kc].astype(F32), v_ref[:, vc]
            a_all = jnp.exp(la_ref[:, kc])
            oacc_ref[...] = jnp.zeros_like(oacc_ref)

            def step(t, carry):
                _gla_token_step(t, rows, q, k, v, a_all, st_ref.at[h], oacc_ref)
                return carry

            lax.fori_loop(0, n_rows, step, 0)
            o_ref[:, vc] = _gla_finish(oacc_ref[...], r_ref[:, vc], g).astype(o_ref.dtype)

    @pl.when(j == pl.num_programs(1) - 1)
    def _():
        for h in range(GLA_HEADS):
            sfin_ref[0, h] = st_ref[h].T


def _gla_prompt(q, k, v, r, la, g, batch, seq):
    rb = 4 * GLA_CHUNK
    nj = seq // rb
    row = lambda w: pl.BlockSpec((rb, w), lambda b, j: (b * nj + j, 0))
    return pl.pallas_call(
        functools.partial(_gla_prompt_kernel, n_chunks=rb // GLA_CHUNK),
        grid=(batch, nj),
        in_specs=[row(QK_W), row(QK_W), row(D_GLA), row(D_GLA), row(QK_W), _const_spec((1, GLA_DV))],
        out_specs=[row(D_GLA),
                   pl.BlockSpec((1, GLA_HEADS, GLA_DK, GLA_DV), lambda b, j: (b, 0, 0, 0))],
        out_shape=[jax.ShapeDtypeStruct((batch * seq, D_GLA), BF16),
                   jax.ShapeDtypeStruct((batch, GLA_HEADS, GLA_DK, GLA_DV), F32)],
        scratch_shapes=[pltpu.VMEM((GLA_HEADS, GLA_DV, GLA_DK), F32), pltpu.VMEM((rb, GLA_DV), F32)],
        compiler_params=pltpu.CompilerParams(dimension_semantics=("arbitrary", "arbitrary"),
                                             vmem_limit_bytes=VMEM_LIMIT),
        name="gla_prompt",
    )(q, k, v, r, la, g)


def _gla_sample_kernel(q_ref, k_ref, v_ref, r_ref, la_ref, g_ref, s0_ref, o_ref, sfin_ref, st_ref, oacc_ref,
                       *, n_seq, seq):
    qe, ke, kd, cum, mask = _gla_pre(q_ref[...].astype(F32), k_ref[...].astype(F32), la_ref[...], seq)
    g = g_ref[...]
    safe = jnp.min(cum) >= GLA_SAFE_LOG_DECAY

    @pl.when(safe)
    def _():
        lasts = jnp.concatenate([cum[(s + 1) * seq - 1:(s + 1) * seq, :] for s in range(n_seq)]
                                + [jnp.zeros((GLA_DK - n_seq, cum.shape[1]), F32)], axis=0)
        pair = 2 * seq
        upper = lax.broadcasted_iota(jnp.int32, (pair, GLA_DK), 0) < seq
        for h in range(GLA_HEADS):
            kc = slice(h * GLA_DK, (h + 1) * GLA_DK)
            vc = slice(h * GLA_DV, (h + 1) * GLA_DV)
            v = v_ref[:, vc]
            qe_h, kd_h = qe[:, kc], kd[:, kc]
            o = _gla_intra(qe_h, ke[:, kc], v, mask)
            dec_t = jnp.exp(lasts[:, kc].T)
            inter = []
            for p in range(n_seq // 2):
                rows = slice(p * pair, (p + 1) * pair)
                qe_p, kd_p, v_p = qe_h[rows], kd_h[rows], v[rows]
                for half in range(2):
                    s = 2 * p + half
                    s0 = s0_ref[s, h]
                    o_s = _dot(qe_p, s0.astype(BF16))
                    inter.append(o_s[half * seq:(half + 1) * seq])
                    kd_s = jnp.where(upper if half == 0 else ~upper, kd_p, jnp.zeros_like(kd_p))
                    dec = jnp.broadcast_to(dec_t[:, s:s + 1], (GLA_DK, GLA_DV))
                    sfin_ref[s, h] = s0 * dec + lax.dot_general(kd_s, v_p, _TN, preferred_element_type=F32)
            o = o + jnp.concatenate(inter, axis=0)
            o_ref[:, vc] = _gla_finish(o, r_ref[:, vc], g).astype(o_ref.dtype)

    @pl.when(jnp.logical_not(safe))
    def _():
        n_rows = n_seq * seq
        rows = lax.broadcasted_iota(jnp.int32, (n_rows, 1), 0)
        for h in range(GLA_HEADS):
            kc = slice(h * GLA_DK, (h + 1) * GLA_DK)
            vc = slice(h * GLA_DV, (h + 1) * GLA_DV)
            q, k, v = q_ref[:, kc].astype(F32), k_ref[:, kc].astype(F32), v_ref[:, vc]
            a_all = jnp.exp(la_ref[:, kc])
            oacc_ref[...] = jnp.zeros_like(oacc_ref)

            def step(t, carry):
                s = t // seq

                @pl.when(t % seq == 0)
                def _():
                    st_ref[...] = s0_ref[s, h].T

                _gla_token_step(t, rows, q, k, v, a_all, st_ref, oacc_ref)

                @pl.when(t % seq == seq - 1)
                def _():
                    sfin_ref[s, h] = st_ref[...].T

                return carry

            lax.fori_loop(0, n_rows, step, 0)
            o_ref[:, vc] = _gla_finish(oacc_ref[...], r_ref[:, vc], g).astype(o_ref.dtype)


def _gla_sample(q, k, v, r, la, g, s0, batch, seq):
    ns = 16
    rb = ns * seq
    row = lambda w: pl.BlockSpec((rb, w), lambda i: (i, 0))
    st = pl.BlockSpec((ns, GLA_HEADS, GLA_DK, GLA_DV), lambda i: (i, 0, 0, 0))
    return pl.pallas_call(
        functools.partial(_gla_sample_kernel, n_seq=ns, seq=seq),
        grid=(batch // ns,),
        in_specs=[row(QK_W), row(QK_W), row(D_GLA), row(D_GLA), row(QK_W), _const_spec((1, GLA_DV)), st],
        out_specs=[row(D_GLA), st],
        out_shape=[jax.ShapeDtypeStruct((batch * seq, D_GLA), BF16),
                   jax.ShapeDtypeStruct((batch, GLA_HEADS, GLA_DK, GLA_DV), F32)],
        scratch_shapes=[pltpu.VMEM((GLA_DV, GLA_DK), F32), pltpu.VMEM((rb, GLA_DV), F32)],
        compiler_params=pltpu.CompilerParams(dimension_semantics=("arbitrary",), vmem_limit_bytes=VMEM_LIMIT),
        name="gla_sample",
    )(q, k, v, r, la, g, s0)


def _s5_kernel(u_ref, wbu_ref, wc_ref, are_ref, aim_ref, d_ref, h0r_ref, h0i_ref,
               y_ref, sre_ref, sim_ref, bu_ref, xs_ref, car_ref, *, nb, tc, has_state):
    j = pl.program_id(2)

    @pl.when(j == 0)
    def _():
        if has_state:
            car_ref[0] = h0r_ref[...]
            car_ref[1] = h0i_ref[...]
        else:
            car_ref[...] = jnp.zeros_like(car_ref)

    u2 = u_ref[...].reshape(nb * tc, LANES)
    ub = u2.astype(BF16)
    nl = S5_SL // LANES
    for l in range(2 * nl):
        bu_ref[l] = _dot(ub, wbu_ref[0, :, l * LANES:(l + 1) * LANES])
    a_r = [jnp.broadcast_to(are_ref[0, :, l * LANES:(l + 1) * LANES], (nb, LANES)) for l in range(nl)]
    a_i = [jnp.broadcast_to(aim_ref[0, :, l * LANES:(l + 1) * LANES], (nb, LANES)) for l in range(nl)]

    def step(t, carry):
        rows = pl.ds(t, nb, stride=tc)
        out = []
        for l in range(nl):
            xr, xi = carry[2 * l], carry[2 * l + 1]
            nr = a_r[l] * xr - a_i[l] * xi + bu_ref[l, rows, :]
            ni = a_r[l] * xi + a_i[l] * xr + bu_ref[nl + l, rows, :]
            xs_ref[l, rows, :] = nr
            xs_ref[nl + l, rows, :] = ni
            out += [nr, ni]
        return tuple(out)

    init = []
    for l in range(nl):
        init += [car_ref[0, :, l * LANES:(l + 1) * LANES], car_ref[1, :, l * LANES:(l + 1) * LANES]]
    fin = lax.fori_loop(0, tc, step, tuple(init), unroll=8)
    xr = jnp.concatenate([fin[2 * l] for l in range(nl)], axis=1)
    xi = jnp.concatenate([fin[2 * l + 1] for l in range(nl)], axis=1)
    car_ref[0] = xr
    car_ref[1] = xi
    y = d_ref[0] * u2
    for l in range(2 * nl):
        y = y + _dot(xs_ref[l].astype(BF16), wc_ref[0, l * LANES:(l + 1) * LANES, :])
    y_ref[...] = y.reshape(nb, tc, LANES)

    @pl.when(j == pl.num_programs(2) - 1)
    def _():
        sre_ref[...] = xr
        sim_ref[...] = xi


def _s5(u3d, wbu, wc, a_re, a_im, dsk, h0r, h0i, nb, tc, has_state):
    batch, seq, _ = u3d.shape
    grid = (S5_NGB, batch // nb, seq // tc)
    st = pl.BlockSpec((nb, S5_SL), lambda g, b, j: (b, g))
    par = lambda w: pl.BlockSpec((1, 1, w), lambda g, b, j: (g, 0, 0))
    ublk = pl.BlockSpec((nb, tc, LANES), lambda g, b, j: (b, j, g))
    return pl.pallas_call(
        functools.partial(_s5_kernel, nb=nb, tc=tc, has_state=has_state),
        grid=grid,
        in_specs=[ublk,
                  pl.BlockSpec((1, LANES, 2 * S5_SL), lambda g, b, j: (g, 0, 0)),
                  pl.BlockSpec((1, 2 * S5_SL, LANES), lambda g, b, j: (g, 0, 0)),
                  par(S5_SL), par(S5_SL), par(LANES), st, st],
        out_specs=[ublk, st, st],
        out_shape=[jax.ShapeDtypeStruct(u3d.shape, F32),
                   jax.ShapeDtypeStruct((batch, S5_GROUPS * S5_STATE), F32),
                   jax.ShapeDtypeStruct((batch, S5_GROUPS * S5_STATE), F32)],
        scratch_shapes=[pltpu.VMEM((2 * S5_SL // LANES, nb * tc, LANES), F32),
                        pltpu.VMEM((2 * S5_SL // LANES, nb * tc, LANES), F32),
                        pltpu.VMEM((2, nb, S5_SL), F32)],
        compiler_params=pltpu.CompilerParams(dimension_semantics=("arbitrary",) * 3,
                                             vmem_limit_bytes=VMEM_LIMIT),
        name="s5_state" if has_state else "s5_zero",
    )(u3d, wbu, wc, a_re, a_im, dsk, h0r, h0i)


S5_J = 2
S5_HL = S5_SL // S5_J


def _s5_prompt_kernel(u_ref, wbu_ref, wc_ref, are_ref, aim_ref, d_ref, y_ref, sre_ref, sim_ref,
                      u2_ref, lhs_ref, bu_ref, xs_ref, y2_ref, yo_ref, car_ref, *, nb, tc):
    g = pl.program_id(0)
    j = pl.program_id(1)
    rows = nb * S5_J

    @pl.when((g == 0) & (j == 0))
    def _():
        lhs_ref[...] = jnp.zeros_like(lhs_ref)

    @pl.when(j == 0)
    def _():
        car_ref[...] = jnp.zeros_like(car_ref)

    u2 = u_ref[...].reshape(nb * tc, LANES)
    u2_ref[...] = u2

    def build(t, c):
        u4 = u2_ref[pl.ds(t, nb, stride=tc), :]
        for jj in range(S5_J):
            lhs_ref[t, jj * nb:(jj + 1) * nb, jj * LANES:(jj + 1) * LANES] = u4
        return c

    lax.fori_loop(0, tc, build, 0, unroll=8)
    lhs = lhs_ref[...].reshape(tc * rows, S5_J * LANES).astype(BF16)
    bu_ref[...] = _dot(lhs, wbu_ref[0]).reshape(tc, rows, 2 * S5_HL)
    ar = are_ref[0]
    ai = aim_ref[0]

    def step(t, carry):
        xr, xi = carry
        tile = bu_ref[t]
        nr = ar * xr - ai * xi + tile[:, 0:S5_HL]
        ni = ar * xi + ai * xr + tile[:, S5_HL:2 * S5_HL]
        xs_ref[t] = jnp.concatenate([nr, ni], axis=1)
        return nr, ni

    xr, xi = lax.fori_loop(0, tc, step, (car_ref[0], car_ref[1]), unroll=8)
    car_ref[0] = xr
    car_ref[1] = xi
    xs = xs_ref[...].reshape(tc * rows, 2 * S5_HL).astype(BF16)
    y2_ref[...] = _dot(xs, wc_ref[0]).reshape(tc, rows, S5_J * LANES)
    first_half = lax.broadcasted_iota(jnp.int32, (rows, LANES), 0) < nb

    def unperm(t, c):
        t2 = y2_ref[t]
        part = jnp.where(first_half, t2[:, 0:LANES], t2[:, LANES:2 * LANES])
        tot = part + pltpu.roll(part, nb, axis=0)
        yo_ref[pl.ds(t, nb, stride=tc), :] = tot[0:nb]
        return c

    lax.fori_loop(0, tc, unperm, 0, unroll=8)
    y_ref[...] = (yo_ref[...] + d_ref[0] * u2).reshape(nb, tc, LANES)

    @pl.when(j == pl.num_programs(1) - 1)
    def _():
        sre_ref[...] = jnp.concatenate([xr[jj * nb:(jj + 1) * nb] for jj in range(S5_J)], axis=1)
        sim_ref[...] = jnp.concatenate([xi[jj * nb:(jj + 1) * nb] for jj in range(S5_J)], axis=1)


def _s5_prompt(u3d, wbu2, wc2, a_re2, a_im2, dsk, tc):
    nb, seq, _ = u3d.shape
    rows = nb * S5_J
    assert rows == 8
    st = pl.BlockSpec((nb, S5_SL), lambda g, j: (0, g))
    gblk = lambda s: pl.BlockSpec((1,) + s, lambda g, j: (g, 0, 0))
    ublk = pl.BlockSpec((nb, tc, LANES), lambda g, j: (0, j, g))
    return pl.pallas_call(
        functools.partial(_s5_prompt_kernel, nb=nb, tc=tc),
        grid=(S5_NGB, seq // tc),
        in_specs=[ublk, gblk((S5_J * LANES, 2 * S5_HL)), gblk((2 * S5_HL, S5_J * LANES)),
                  gblk((rows, S5_HL)), gblk((rows, S5_HL)), gblk((1, LANES))],
        out_specs=[ublk, st, st],
        out_shape=[jax.ShapeDtypeStruct(u3d.shape, F32),
                   jax.ShapeDtypeStruct((nb, S5_GROUPS * S5_STATE), F32),
                   jax.ShapeDtypeStruct((nb, S5_GROUPS * S5_STATE), F32)],
        scratch_shapes=[pltpu.VMEM((nb * tc, LANES), F32),
                        pltpu.VMEM((tc, rows, S5_J * LANES), F32),
                        pltpu.VMEM((tc, rows, 2 * S5_HL), F32),
                        pltpu.VMEM((tc, rows, 2 * S5_HL), F32),
                        pltpu.VMEM((tc, rows, S5_J * LANES), F32),
                        pltpu.VMEM((nb * tc, LANES), F32),
                        pltpu.VMEM((2, rows, S5_HL), F32)],
        compiler_params=pltpu.CompilerParams(dimension_semantics=("arbitrary",) * 2,
                                             vmem_limit_bytes=VMEM_LIMIT),
        name="s5_prompt",
    )(u3d, wbu2, wc2, a_re2, a_im2, dsk)


def _s5_prompt_params(wbu, wc, a_re, a_im, nb):
    h = S5_HL
    top = jnp.concatenate([wbu[:, :, 0:h], wbu[:, :, S5_SL:S5_SL + h]], axis=2)
    bot = jnp.concatenate([wbu[:, :, h:2 * h], wbu[:, :, S5_SL + h:S5_SL + 2 * h]], axis=2)
    wbu2 = jnp.concatenate([top, bot], axis=1)
    wc_j = [jnp.concatenate([wc[:, jj * h:(jj + 1) * h, :], wc[:, S5_SL + jj * h:S5_SL + (jj + 1) * h, :]],
                            axis=1) for jj in range(S5_J)]
    wc2 = jnp.concatenate(wc_j, axis=2)
    tile = lambda a: jnp.repeat(a.reshape(S5_NGB, S5_J, h), nb, axis=1)
    return wbu2, wc2, tile(a_re), tile(a_im)


def _mix_out_kernel(xp_ref, xs_ref, ogp_ref, ogs_ref, ysp_ref, yss_ref, glu_w_ref, glu_b_ref, s5n_ref, wo_ref,
                    nffn_ref, wr_ref, h1_ref, hn_ref, rt_ref, *, n_prompt_tiles):
    is_p = pl.program_id(0) < n_prompt_tiles
    x = jnp.where(is_p, xp_ref[...], xs_ref[...])
    og = jnp.where(is_p, ogp_ref[...], ogs_ref[...])
    y = jax.nn.gelu(jnp.where(is_p, ysp_ref[...], yss_ref[...]))
    y = y * jax.nn.sigmoid(_dot(y.astype(BF16), glu_w_ref[...]) + glu_b_ref[...])
    yn = _rms(y, s5n_ref[...]).astype(BF16)
    mix = _dot(og, wo_ref[0:D_GLA, :]) + _dot(yn, wo_ref[D_GLA:D_GLA + D_S5, :])
    h1 = x + mix
    h1_ref[...] = h1
    hn = _rms(h1, nffn_ref[...])
    hn_ref[...] = _pack_bf16_pair(hn[:, 0:HALF], hn[:, HALF:D_MODEL])
    hn_hi = hn.astype(BF16)
    hn_lo = (hn - hn_hi.astype(F32)).astype(BF16)
    logits = _dot(hn_hi, wr_ref[0]) + _dot(hn_hi, wr_ref[1]) + _dot(hn_lo, wr_ref[0])
    rt_ref[...] = _route(logits)


def _route(logits):
    col = lax.broadcasted_iota(jnp.int32, logits.shape, 1)
    colf = col.astype(F32)
    neg = -jnp.inf

    def first_argmax(vals):
        m = jnp.max(vals, axis=-1, keepdims=True)
        idx = jnp.min(jnp.where(vals == m, colf, float(LANES)), axis=-1, keepdims=True)
        return m, idx

    lg = jnp.where(col < N_EGROUPS, logits, neg)
    gmax, gsel = first_argmax(lg)
    p_g = 1.0 / jnp.sum(jnp.exp(lg - gmax), axis=-1, keepdims=True)
    ecol = col - N_EGROUPS
    egrp = (ecol >> 3).astype(F32)
    in_group = (ecol >= 0) & (ecol < N_EXPERTS) & (egrp == gsel)
    le = jnp.where(in_group, logits, neg)
    m1, i1 = first_argmax(le)
    le2 = jnp.where(colf == i1, neg, le)
    m2, i2 = first_argmax(le2)
    e2 = jnp.exp(m2 - m1)
    den = 1.0 + e2
    w1 = p_g * (1.0 / den)
    w2 = p_g * (e2 / den)
    e1f = i1 - float(N_EGROUPS)
    e2f = i2 - float(N_EGROUPS)
    out = jnp.where(col == 0, e1f, jnp.where(col == 1, e2f, jnp.where(col == 2, w1, jnp.where(col == 3, w2, 0.0))))
    return out


def _mix_out(xp, xs, ogp, ogs, ysp, yss, glu_w, glu_b, s5n, wo, nffn, wr):
    tm = TOK_TILE
    npt, nst = xp.shape[0] // tm, xs.shape[0] // tm
    t = (npt + nst) * tm
    row = lambda w: pl.BlockSpec((tm, w), lambda i: (i, 0))
    prow = lambda w: pl.BlockSpec((tm, w), lambda i: (jnp.minimum(i, npt - 1), 0))
    srow = lambda w: pl.BlockSpec((tm, w), lambda i: (jnp.maximum(i - npt, 0), 0))
    return pl.pallas_call(
        functools.partial(_mix_out_kernel, n_prompt_tiles=npt),
        grid=(npt + nst,),
        in_specs=[prow(D_MODEL), srow(D_MODEL), prow(D_GLA), srow(D_GLA), prow(D_S5), srow(D_S5),
                  _const_spec(glu_w.shape), _const_spec((1, D_S5)), _const_spec((1, D_S5)),
                  _const_spec(wo.shape), _const_spec((1, D_MODEL)), _const_spec(wr.shape)],
        out_specs=[row(D_MODEL), row(HALF), row(LANES)],
        out_shape=[jax.ShapeDtypeStruct((t, D_MODEL), F32), jax.ShapeDtypeStruct((t, HALF), jnp.uint32),
                   jax.ShapeDtypeStruct((t, LANES), F32)],
        compiler_params=pltpu.CompilerParams(dimension_semantics=("arbitrary",), vmem_limit_bytes=VMEM_LIMIT),
        name="mix_out",
    )(xp, xs, ogp, ogs, ysp, yss, glu_w, glu_b, s5n, wo, nffn, wr)


DMA_UNROLL = 8


MOE_SB = 3
HALF = D_MODEL // 2


def _pack_bf16_pair(lo, hi):
    def bits(x):
        b = pltpu.bitcast(x, jnp.uint32)
        return (b + jnp.uint32(0x7FFF) + ((b >> 16) & jnp.uint32(1))) >> 16
    return bits(lo) | (bits(hi) << 16)


def _unpack_bf16_pair(u):
    return (pltpu.bitcast(u << 16, F32), pltpu.bitcast(u & jnp.uint32(0xFFFF0000), F32))


def _moe_kernel(se_ref, sm_ref, sn_ref, sbase_ref, sa_ref, hn_hbm, wg_ref, wu_ref, wd_ref, out_hbm,
                xb_ref, yb_ref, wgb_ref, wub_ref, wdb_ref, gsem, ssem):
    b = pl.program_id(0)
    nb = pl.num_programs(0)
    n = sn_ref[b]
    slot = lax.rem(b, 2)

    def round_up(cnt):
        return (cnt + DMA_UNROLL - 1) // DMA_UNROLL * DMA_UNROLL

    def issue_gather(blk, sl):
        base = sbase_ref[blk]

        def grp(g, c):
            for j in range(DMA_UNROLL):
                i = g * DMA_UNROLL + j
                tok = sa_ref[base + i] >> 1
                pltpu.make_async_copy(hn_hbm.at[pl.ds(tok, 1), :], xb_ref.at[sl, pl.ds(i, 1), :],
                                      gsem.at[sl]).start(priority=1)
            return c

        lax.fori_loop(0, round_up(sn_ref[blk]) // DMA_UNROLL, grp, 0)

    def wait_gather(cnt, sl):
        rows = pl.ds(0, pl.multiple_of(round_up(cnt), DMA_UNROLL))
        pltpu.make_async_copy(hn_hbm.at[rows, :], xb_ref.at[sl, rows, :], gsem.at[sl]).wait()

    m_cur = sm_ref[b]
    m_prev = sm_ref[jnp.maximum(b - 1, 0)]
    has_prev = (b > 0) & (m_prev > 0)

    def out_copy(rows):
        dst = pl.ds(pl.multiple_of(sbase_ref[b], MOE_BLK), rows)
        return pltpu.make_async_copy(yb_ref.at[pl.ds(0, rows), :], out_hbm.at[dst, :], ssem.at[0])

    def wait_prev_out():
        rows = pl.ds(0, pl.multiple_of(m_prev * MOE_BLK, MOE_BLK))
        pltpu.make_async_copy(yb_ref.at[rows, :], out_hbm.at[rows, :], ssem.at[0]).wait()

    def write_out(rows):
        out_copy(rows).start()

        @pl.when(b == nb - 1)
        def _():
            out_copy(rows).wait()

    @pl.when(b == 0)
    def _():
        xb_ref[...] = jnp.zeros_like(xb_ref)
        issue_gather(0, 0)

    @pl.when(b + 1 < nb)
    def _():
        issue_gather(b + 1, 1 - slot)

    def compute(rows):
        x_lo, x_hi = _unpack_bf16_pair(xb_ref[slot, 0:rows, :])
        x_lo, x_hi = x_lo.astype(BF16), x_hi.astype(BF16)
        gate = _dot(x_lo, wgb_ref[0:HALF, :]) + _dot(x_hi, wgb_ref[HALF:D_MODEL, :])
        up = _dot(x_lo, wub_ref[0:HALF, :]) + _dot(x_hi, wub_ref[HALF:D_MODEL, :])
        hid = (gate * jax.nn.sigmoid(gate) * up).astype(BF16)

        @pl.when(has_prev)
        def _():
            wait_prev_out()

        yb_ref[0:rows, :] = _pack_bf16_pair(_dot(hid, wdb_ref[:, 0:HALF]), _dot(hid, wdb_ref[:, HALF:D_MODEL]))
        write_out(rows)

    @pl.when(n > 0)
    def _():
        prev_e = se_ref[jnp.maximum(b - 1, 0)]

        @pl.when((b == 0) | (prev_e != se_ref[b]))
        def _():
            wgb_ref[...] = wg_ref[...].astype(BF16)
            wub_ref[...] = wu_ref[...].astype(BF16)
            wdb_ref[...] = wd_ref[...].astype(BF16)

        wait_gather(n, slot)
        for m in range(1, MOE_SB + 1):
            @pl.when(m_cur == m)
            def _():
                compute(m * MOE_BLK)

    @pl.when((n == 0) & (m_cur > 0))
    def _():
        @pl.when(has_prev)
        def _():
            wait_prev_out()

        yb_ref[...] = jnp.zeros_like(yb_ref)
        for m in range(1, MOE_SB + 1):
            @pl.when(m_cur == m)
            def _():
                write_out(m * MOE_BLK)

    @pl.when((m_cur == 0) & has_prev)
    def _():
        wait_prev_out()


def _moe(hn_pk, plan, w_gate, w_up, w_down, n_out_rows):
    sb_e, sb_m, sb_n, sb_base, slot_a = plan
    sb_rows = MOE_SB * MOE_BLK
    wspec = lambda s: pl.BlockSpec((None,) + s, lambda b, se, *_: (se[b], 0, 0))
    grid_spec = pltpu.PrefetchScalarGridSpec(
        num_scalar_prefetch=5,
        grid=(sb_e.shape[0],),
        in_specs=[pl.BlockSpec(memory_space=pl.ANY),
                  wspec((D_MODEL, D_EXPERT)), wspec((D_MODEL, D_EXPERT)), wspec((D_EXPERT, D_MODEL))],
        out_specs=pl.BlockSpec(memory_space=pl.ANY),
        scratch_shapes=[pltpu.VMEM((2, sb_rows, HALF), jnp.uint32), pltpu.VMEM((sb_rows, HALF), jnp.uint32),
                        pltpu.VMEM((D_MODEL, D_EXPERT), BF16), pltpu.VMEM((D_MODEL, D_EXPERT), BF16),
                        pltpu.VMEM((D_EXPERT, D_MODEL), BF16), pltpu.SemaphoreType.DMA((2,)),
                        pltpu.SemaphoreType.DMA((1,))],
    )
    return pl.pallas_call(
        _moe_kernel,
        grid_spec=grid_spec,
        out_shape=jax.ShapeDtypeStruct((n_out_rows, HALF), jnp.uint32),
        compiler_params=pltpu.CompilerParams(dimension_semantics=("arbitrary",),
                                             vmem_limit_bytes=MOE_VMEM_LIMIT),
        name="moe_experts",
    )(sb_e, sb_m, sb_n, sb_base, slot_a, hn_pk, w_gate, w_up, w_down)


def _moe_plan(rt_all):
    t_all = rt_all.shape[0]
    n_assign = 2 * t_all
    e_flat = rt_all[:, 0:2].astype(jnp.int32).reshape(-1)
    onehot = (e_flat[:, None] == jnp.arange(N_EXPERTS, dtype=jnp.int32)[None, :]).astype(jnp.int32)
    csum = jnp.cumsum(onehot, axis=0)
    counts = csum[-1]
    rank = jnp.take_along_axis(csum, e_flat[:, None], axis=1)[:, 0] - 1
    padded = (counts + MOE_BLK - 1) // MOE_BLK * MOE_BLK
    pad_end = jnp.cumsum(padded)
    pad_start = pad_end - padded
    dest = pad_start[e_flat] + rank
    nblk = -(-(n_assign + N_EXPERTS * (MOE_BLK - 1)) // MOE_BLK)
    n_slots = nblk * MOE_BLK
    slot_a = jnp.zeros((n_slots,), jnp.int32).at[dest].set(jnp.arange(n_assign, dtype=jnp.int32))
    k_e = padded // MOE_BLK
    sbc = (k_e + MOE_SB - 1) // MOE_SB
    sb_end = jnp.cumsum(sbc)
    sb_start = sb_end - sbc
    n_sb = (nblk + (MOE_SB - 1) * N_EXPERTS) // MOE_SB
    s = jnp.arange(n_sb, dtype=jnp.int32)
    sb_e = jnp.minimum(jnp.sum((s[:, None] >= sb_end[None, :]).astype(jnp.int32), axis=1), N_EXPERTS - 1)
    j = s - sb_start[sb_e]
    real = s < sb_end[-1]
    tail_blk = pad_end[-1] // MOE_BLK + MOE_SB * (s - sb_end[-1])
    sb_m = jnp.where(real, jnp.clip(k_e[sb_e] - MOE_SB * j, 0, MOE_SB), jnp.clip(nblk - tail_blk, 0, MOE_SB))
    sb_n = jnp.where(real, jnp.clip(counts[sb_e] - MOE_SB * MOE_BLK * j, 0, MOE_SB * MOE_BLK), 0)
    sb_base = jnp.where(real, pad_start[sb_e] + MOE_SB * MOE_BLK * j, jnp.minimum(tail_blk, nblk - 1) * MOE_BLK)
    i32 = lambda a: a.astype(jnp.int32)
    return (i32(sb_e), i32(sb_m), i32(sb_n), i32(sb_base), slot_a), i32(dest), n_slots


def _ple_out_kernel(dest_ref, h1_ref, rt_ref, pp_ref, ps_ref, nple_ref, wpg_ref, wp_ref, nfin_ref, ys_hbm,
                    op_ref, os_ref, yb_ref, sem, *, n_prompt_tiles):
    i = pl.program_id(0)
    nt = pl.num_programs(0)
    tm = h1_ref.shape[0]
    slot = lax.rem(i, 2)

    def issue(tile, sl):
        base = tile * (2 * tm)

        def grp(g, c):
            for j in range(DMA_UNROLL):
                d = dest_ref[base + g * DMA_UNROLL + j]
                row = g * (DMA_UNROLL // 2) + j // 2
                pltpu.make_async_copy(ys_hbm.at[pl.ds(d, 1), :], yb_ref.at[sl, j % 2, pl.ds(row, 1), :],
                                      sem.at[sl]).start()
            return c

        lax.fori_loop(0, 2 * tm // DMA_UNROLL, grp, 0)

    @pl.when(i == 0)
    def _():
        issue(0, 0)

    def wait_rows(sl):
        for k in range(2):
            pltpu.make_async_copy(ys_hbm.at[pl.ds(0, tm), :], yb_ref.at[sl, k], sem.at[sl]).wait()

    wait_rows(slot)
    is_p = i < n_prompt_tiles
    rt = rt_ref[...]
    w0, w1 = rt[:, 2:3], rt[:, 3:4]
    lo0, hi0 = _unpack_bf16_pair(yb_ref[slot, 0])
    lo1, hi1 = _unpack_bf16_pair(yb_ref[slot, 1])
    h2 = h1_ref[...] + jnp.concatenate([w0 * lo0 + w1 * lo1, w0 * hi0 + w1 * hi1], axis=1)
    hb = _rms(h2, nple_ref[...]).astype(BF16)
    nbase = jnp.minimum(i + 1, nt - 1) * (2 * tm)
    n_chunk = 8
    cw, ca = D_MODEL // n_chunk, 2 * tm // n_chunk
    gates = []
    for c in range(n_chunk):
        gates.append(jax.nn.sigmoid(_dot(hb, wpg_ref[:, c * cw:(c + 1) * cw])))
        for a in range(c * ca, (c + 1) * ca):
            d = dest_ref[nbase + a]
            pltpu.make_async_copy(ys_hbm.at[pl.ds(d, 1), :], yb_ref.at[1 - slot, a % 2, pl.ds(a // 2, 1), :],
                                  sem.at[1 - slot]).start()
    gate = jnp.concatenate(gates, axis=1)

    @pl.when(i == nt - 1)
    def _():
        wait_rows(1 - slot)

    p = jnp.where(is_p, pp_ref[...], ps_ref[...])
    h3 = h2 + _dot(p.astype(BF16), wp_ref[...]) * gate
    y = _rms(h3, nfin_ref[...])

    @pl.when(is_p)
    def _():
        op_ref[...] = y

    @pl.when(jnp.logical_not(is_p))
    def _():
        os_ref[...] = y


def _ple_out(dest, h1, rt, pp, ps, nple, wpg, wp, nfin, y_slots):
    tm = TOK_TILE
    npt, nst = pp.shape[0] // tm, ps.shape[0] // tm
    row = lambda w: pl.BlockSpec((tm, w), lambda i, d: (i, 0))
    prow = lambda w: pl.BlockSpec((tm, w), lambda i, d: (jnp.minimum(i, npt - 1), 0))
    srow = lambda w: pl.BlockSpec((tm, w), lambda i, d: (jnp.maximum(i - npt, 0), 0))
    const = lambda shape: pl.BlockSpec(shape, lambda i, d: (0,) * len(shape), pipeline_mode=pl.Buffered(1))
    grid_spec = pltpu.PrefetchScalarGridSpec(
        num_scalar_prefetch=1,
        grid=(npt + nst,),
        in_specs=[row(D_MODEL), row(LANES), prow(D_PLE), srow(D_PLE), const((1, D_MODEL)), const(wpg.shape),
                  const(wp.shape), const((1, D_MODEL)), pl.BlockSpec(memory_space=pl.ANY)],
        out_specs=[prow(D_MODEL), srow(D_MODEL)],
        scratch_shapes=[pltpu.VMEM((2, 2, tm, HALF), jnp.uint32), pltpu.SemaphoreType.DMA((2,))],
    )
    return pl.pallas_call(
        functools.partial(_ple_out_kernel, n_prompt_tiles=npt),
        grid_spec=grid_spec,
        out_shape=[jax.ShapeDtypeStruct((pp.shape[0], D_MODEL), F32),
                   jax.ShapeDtypeStruct((ps.shape[0], D_MODEL), F32)],
        compiler_params=pltpu.CompilerParams(dimension_semantics=("arbitrary",), vmem_limit_bytes=VMEM_LIMIT),
        name="ple_out",
    )(dest, h1, rt, pp, ps, nple, wpg, wp, nfin, y_slots)


def _s5_params(lam_re, lam_im, log_dt, b_re, b_im, c_re, c_im, d_skip):
    dt = jnp.exp(log_dt)[:, None]
    mag = jnp.exp(lam_re * dt)
    ab_re = mag * jnp.cos(lam_im * dt)
    ab_im = mag * jnp.sin(lam_im * dt)
    den = lam_re * lam_re + lam_im * lam_im
    nr = ab_re - 1.0
    f_re = (nr * lam_re + ab_im * lam_im) / den
    f_im = (ab_im * lam_re - nr * lam_im) / den
    bb_re = f_re[..., None] * b_re - f_im[..., None] * b_im
    bb_im = f_re[..., None] * b_im + f_im[..., None] * b_re
    eye = jnp.eye(S5_GB, dtype=F32)

    def bu_w(bb):
        bb = bb.reshape(S5_NGB, S5_GB, S5_STATE, S5_GROUP)
        w = jnp.einsum('nlph,lm->nlhmp', bb, eye)
        return w.reshape(S5_NGB, S5_GB * S5_GROUP, S5_SL)

    def c_w(c):
        c = c.reshape(S5_NGB, S5_GB, S5_GROUP, S5_STATE)
        w = jnp.einsum('nlhp,lm->nlpmh', c, eye)
        return w.reshape(S5_NGB, S5_SL, S5_GB * S5_GROUP)

    wbu = jnp.concatenate([bu_w(bb_re), bu_w(bb_im)], axis=2).astype(BF16)
    wc = jnp.concatenate([c_w(c_re), -c_w(c_im)], axis=1).astype(BF16)
    a_re = ab_re.reshape(S5_NGB, 1, S5_SL)
    a_im = ab_im.reshape(S5_NGB, 1, S5_SL)
    dsk = d_skip.reshape(S5_NGB, 1, S5_GB * S5_GROUP)
    return wbu, wc, a_re, a_im, dsk


def kernel(x_prompt, x_sample, p_prompt, p_sample, state_gla, state_s5_re, state_s5_im, norm_mix, w_in, gla_w_gate_up, gla_gate_bias, gla_norm, s5_lam_re, s5_lam_im, s5_log_dt, s5_b_re, s5_b_im, s5_c_re, s5_c_im, s5_d, s5_glu_w, s5_glu_b, s5_norm, w_out, norm_ffn, router_group, router_expert, w_gate, w_up, w_down, norm_ple, w_ple, w_ple_gate, norm_final):
    depth = w_in.shape[0]
    assert depth == 1
    i = 0
    bp, lp, _ = x_prompt.shape
    bs, ls, _ = x_sample.shape
    tp, ts = bp * lp, bs * ls
    t_all = tp + ts

    wm = _w_in_prep(w_in[i])
    wgu = jnp.pad(gla_w_gate_up[i], ((0, LANES - GLA_RANK), (0, 0))).astype(BF16)
    gbias = gla_gate_bias[i].reshape(1, QK_W)
    wbu, wc, a_re, a_im, dsk = _s5_params(s5_lam_re[i], s5_lam_im[i], s5_log_dt[i], s5_b_re[i], s5_b_im[i],
                                          s5_c_re[i], s5_c_im[i], s5_d[i])
    glu_w = s5_glu_w[i].astype(BF16)
    wo = w_out[i].astype(BF16)
    wr32 = jnp.pad(jnp.concatenate([router_group[i], router_expert[i]], axis=1),
                   ((0, 0), (0, LANES - N_EGROUPS - N_EXPERTS)))
    wr_hi = wr32.astype(BF16)
    wr = jnp.stack([wr_hi, (wr32 - wr_hi.astype(F32)).astype(BF16)])
    wpg = w_ple_gate[i].astype(BF16)
    wp = w_ple[i].astype(BF16)
    vec = lambda a: a.reshape(1, -1)

    xp = x_prompt.reshape(tp, D_MODEL)
    xs = x_sample.reshape(ts, D_MODEL)

    qp, kp, vp, rp, lap, up = _in_proj(xp, vec(norm_mix[i]), wm, wgu, gbias, BF16)
    qs, ks, vs, rs, las, us = _in_proj(xs, vec(norm_mix[i]), wm, wgu, gbias, BF16)
    ogp, gla_p = _gla_prompt(qp, kp, vp, rp, lap, vec(gla_norm[i]), bp, lp)
    ogs, gla_s = _gla_sample(qs, ks, vs, rs, las, vec(gla_norm[i]), state_gla[i], bs, ls)
    wbu2, wc2, a_re2, a_im2 = _s5_prompt_params(wbu, wc, a_re, a_im, bp)
    ysp, re_p, im_p = _s5_prompt(up.reshape(bp, lp, D_S5), wbu2, wc2, a_re2, a_im2, dsk, tc=256)
    yss, re_s, im_s = _s5(us.reshape(bs, ls, D_S5), wbu, wc, a_re, a_im, dsk,
                          state_s5_re[i].reshape(bs, -1), state_s5_im[i].reshape(bs, -1),
                          nb=32, tc=ls, has_state=True)

    h1, hn_all, rt_all = _mix_out(xp, xs, ogp, ogs, ysp.reshape(tp, D_S5), yss.reshape(ts, D_S5), glu_w,
                                  vec(s5_glu_b[i]), vec(s5_norm[i]), wo, vec(norm_ffn[i]), wr)

    plan, dest, n_slots = _moe_plan(rt_all)
    y_slots = _moe(hn_all, plan, w_gate[i], w_up[i], w_down[i], n_slots)

    y_p, y_s = _ple_out(dest, h1, rt_all, p_prompt[i].reshape(tp, D_PLE), p_sample[i].reshape(ts, D_PLE),
                        vec(norm_ple[i]), wpg, wp, vec(norm_final), y_slots)

    s5shape = lambda a, b: a.reshape(1, b, S5_GROUPS, S5_STATE)
    return (y_p.reshape(bp, lp, D_MODEL), y_s.reshape(bs, ls, D_MODEL),
            gla_p[None], s5shape(re_p, bp), s5shape(im_p, bp),
            gla_s[None], s5shape(re_s, bs), s5shape(im_s, bs))
```

```python
import functools
import math

import jax
import jax.numpy as jnp
from jax import lax
from jax.experimental import pallas as pl
from jax.experimental.pallas import tpu as pltpu

F32 = jnp.float32
BF16 = jnp.bfloat16

D_MODEL = 2048
D_GLA = 1024
D_S5 = 1024
GLA_HEADS = 4
GLA_DV = 256
GLA_DK = 128
GLA_RANK = 16
GLA_CHUNK = 64
S5_GROUP = 16
S5_GROUPS = 64
S5_STATE = 64
N_EGROUPS = 4
N_EPG = 8
N_EXPERTS = 32
D_EXPERT = 512
D_PLE = 256
EPS = 1e-6

LANES = 128
QK_W = GLA_HEADS * GLA_DK
S5_GB = 8
S5_NGB = S5_GROUPS // S5_GB
S5_SL = S5_GB * S5_STATE
TOK_TILE = 256
MOE_BLK = 256
VMEM_LIMIT = 56 * 1024 * 1024
MOE_VMEM_LIMIT = 60 * 1024 * 1024


def _const_spec(shape):
    nd = len(shape)
    return pl.BlockSpec(shape, lambda *_: (0,) * nd, pipeline_mode=pl.Buffered(1))


def _rms(x, g):
    return x * lax.rsqrt(jnp.mean(x * x, axis=-1, keepdims=True) + EPS) * g


def _dot(a, b):
    return jnp.dot(a, b, preferred_element_type=F32)


def _log_sigmoid(x):
    return -(jnp.maximum(-x, 0.0) + jnp.log1p(jnp.exp(-jnp.abs(x))))


N_QKVR = 2 * QK_W + 2 * D_GLA
W_IN_COLS = N_QKVR + GLA_RANK + D_S5


def _w_in_prep_kernel(w_ref, o_ref):
    o_ref[:, 0:N_QKVR] = w_ref[:, 0:N_QKVR].astype(BF16)
    tail = w_ref[:, N_QKVR:W_IN_COLS]
    o_ref[:, N_QKVR:N_QKVR + D_S5] = tail[:, GLA_RANK:GLA_RANK + D_S5].astype(BF16)
    o_ref[:, N_QKVR + D_S5:N_QKVR + D_S5 + LANES] = tail[:, 0:LANES].astype(BF16)


def _w_in_prep(wi):
    rows = 256
    return pl.pallas_call(
        _w_in_prep_kernel,
        grid=(D_MODEL // rows,),
        in_specs=[pl.BlockSpec((rows, W_IN_COLS), lambda i: (i, 0))],
        out_specs=pl.BlockSpec((rows, N_QKVR + D_S5 + LANES), lambda i: (i, 0)),
        out_shape=jax.ShapeDtypeStruct((D_MODEL, N_QKVR + D_S5 + LANES), BF16),
        compiler_params=pltpu.CompilerParams(dimension_semantics=("arbitrary",), vmem_limit_bytes=VMEM_LIMIT),
        name="w_in_prep",
    )(wi)


def _in_proj_kernel(x_ref, g_ref, wm_ref, wgu_ref, gb_ref,
                    q_ref, k_ref, v_ref, r_ref, la_ref, u_ref):
    hb = _rms(x_ref[...], g_ref[...]).astype(BF16)

    def seg(a, b):
        return _dot(hb, wm_ref[:, a:b])

    q_ref[...] = (seg(0, QK_W) * (GLA_DK ** -0.5)).astype(q_ref.dtype)
    k_ref[...] = seg(QK_W, 2 * QK_W).astype(k_ref.dtype)
    v_ref[...] = seg(2 * QK_W, 2 * QK_W + D_GLA).astype(v_ref.dtype)
    r_ref[...] = seg(2 * QK_W + D_GLA, N_QKVR).astype(r_ref.dtype)
    u_ref[...] = seg(N_QKVR, N_QKVR + D_S5)
    zg = seg(N_QKVR + D_S5, N_QKVR + D_S5 + LANES)
    xg = _dot(zg.astype(BF16), wgu_ref[...]) + gb_ref[...]
    la_ref[...] = _log_sigmoid(xg) * (1.0 / 16.0)


def _in_proj(x2d, g, wm, wgu, gbias, act_dtype):
    t = x2d.shape[0]
    tm = TOK_TILE
    row = lambda w: pl.BlockSpec((tm, w), lambda i: (i, 0))
    return pl.pallas_call(
        _in_proj_kernel,
        grid=(t // tm,),
        in_specs=[row(D_MODEL), _const_spec((1, D_MODEL)), _const_spec(wm.shape),
                  _const_spec(wgu.shape), _const_spec((1, QK_W))],
        out_specs=[row(QK_W), row(QK_W), row(D_GLA), row(D_GLA), row(QK_W), row(D_S5)],
        out_shape=[jax.ShapeDtypeStruct((t, QK_W), act_dtype), jax.ShapeDtypeStruct((t, QK_W), act_dtype),
                   jax.ShapeDtypeStruct((t, D_GLA), act_dtype), jax.ShapeDtypeStruct((t, D_GLA), act_dtype),
                   jax.ShapeDtypeStruct((t, QK_W), F32), jax.ShapeDtypeStruct((t, D_S5), F32)],
        compiler_params=pltpu.CompilerParams(dimension_semantics=("arbitrary",), vmem_limit_bytes=VMEM_LIMIT),
        name="in_proj",
    )(x2d, g, wm, wgu, gbias)


_NT = (((1,), (1,)), ((), ()))
_TN = (((0,), (0,)), ((), ()))


def _gla_pre(q, k, la, c):
    r = q.shape[0]
    shift = int(math.log2(c))
    ri = lax.broadcasted_iota(jnp.int32, (r, r), 0)
    si = lax.broadcasted_iota(jnp.int32, (r, r), 1)
    mask = ((ri >> shift) == (si >> shift)) & (ri >= si)
    tri = jnp.where(mask, 1.0, 0.0).astype(BF16)
    hi = la.astype(BF16)
    r1 = la - hi.astype(F32)
    mid = r1.astype(BF16)
    lo = (r1 - mid.astype(F32)).astype(BF16)
    cum = _dot(tri, hi) + _dot(tri, mid) + _dot(tri, lo)
    last = jnp.concatenate([jnp.broadcast_to(cum[(i + 1) * c - 1:(i + 1) * c, :], (c, cum.shape[1]))
                            for i in range(r // c)], axis=0)
    qe = (q * jnp.exp(cum)).astype(BF16)
    ke = (k * jnp.exp(-cum)).astype(BF16)
    kd = (k * jnp.exp(last - cum)).astype(BF16)
    return qe, ke, kd, cum, mask


def _gla_intra(qe, ke, v, mask):
    sc = lax.dot_general(qe, ke, _NT, preferred_element_type=F32)
    return _dot(jnp.where(mask, sc, 0.0).astype(BF16), v)


def _gla_finish(o, r, g):
    rf = r.astype(F32)
    return _rms(o, g) * (rf * jax.nn.sigmoid(rf))


GLA_SAFE_LOG_DECAY = -60.0


def _gla_token_step(t, rows, q, k, v, a_all, st_ref, oacc_ref):
    m = rows == t
    a = jnp.sum(jnp.where(m, a_all, 0.0), axis=0, keepdims=True)
    kt = jnp.where(m, k, 0.0).astype(BF16)
    qt = jnp.where(m, q, 0.0).astype(BF16)
    vt = jnp.where(m, v, jnp.zeros_like(v))
    st = st_ref[...] * a + lax.dot_general(vt, kt, _TN, preferred_element_type=F32)
    st_ref[...] = st
    oacc_ref[...] += lax.dot_general(qt, st.astype(BF16), _NT, preferred_element_type=F32)


def _gla_prompt_kernel(q_ref, k_ref, v_ref, r_ref, la_ref, g_ref, o_ref, sfin_ref, st_ref, oacc_ref,
                       *, n_chunks):
    j = pl.program_id(1)

    @pl.when(j == 0)
    def _():
        st_ref[...] = jnp.zeros_like(st_ref)

    c = GLA_CHUNK
    qe, ke, kd, cum, mask = _gla_pre(q_ref[...].astype(F32), k_ref[...].astype(F32), la_ref[...], c)
    g = g_ref[...]
    safe = jnp.min(cum) >= GLA_SAFE_LOG_DECAY

    @pl.when(safe)
    def _():
        for h in range(GLA_HEADS):
            kc = slice(h * GLA_DK, (h + 1) * GLA_DK)
            vc = slice(h * GLA_DV, (h + 1) * GLA_DV)
            v = v_ref[:, vc]
            qe_h, kd_h = qe[:, kc], kd[:, kc]
            o = _gla_intra(qe_h, ke[:, kc], v, mask)
            st = st_ref[h]
            inter = []
            for ci in range(n_chunks):
                rows = slice(ci * c, (ci + 1) * c)
                inter.append(lax.dot_general(qe_h[rows], st.astype(BF16), _NT, preferred_element_type=F32))
                dec = jnp.exp(cum[(ci + 1) * c - 1:(ci + 1) * c, kc])
                st = st * dec + lax.dot_general(v[rows], kd_h[rows], _TN, preferred_element_type=F32)
            st_ref[h] = st
            o = o + jnp.concatenate(inter, axis=0)
            o_ref[:, vc] = _gla_finish(o, r_ref[:, vc], g).astype(o_ref.dtype)

    @pl.when(jnp.logical_not(safe))
    def _():
        n_rows = n_chunks * c
        rows = lax.broadcasted_iota(jnp.int32, (n_rows, 1), 0)
        for h in range(GLA_HEADS):
            kc = slice(h * GLA_DK, (h + 1) * GLA_DK)
            vc = slice(h * GLA_DV, (h + 1) * GLA_DV)
            q, k, v = q_ref[:, kc].astype(F32), k_ref[:, kc].astype(F32), v_ref[:, vc]
            a_all = jnp.exp(la_ref[:, kc])
            oacc_ref[...] = jnp.zeros_like(oacc_ref)

            def step(t, carry):
                _gla_token_step(t, rows, q, k, v, a_all, st_ref.at[h], oacc_ref)
                return carry

            lax.fori_loop(0, n_rows, step, 0)
            o_ref[:, vc] = _gla_finish(oacc_ref[...], r_ref[:, vc], g).astype(o_ref.dtype)

    @pl.when(j == pl.num_programs(1) - 1)
    def _():
        for h in range(GLA_HEADS):
            sfin_ref[0, h] = st_ref[h].T


def _gla_prompt(q, k, v, r, la, g, batch, seq):
    rb = 4 * GLA_CHUNK
    nj = seq // rb
    row = lambda w: pl.BlockSpec((rb, w), lambda b, j: (b * nj + j, 0))
    return pl.pallas_call(
        functools.partial(_gla_prompt_kernel, n_chunks=rb // GLA_CHUNK),
        grid=(batch, nj),
        in_specs=[row(QK_W), row(QK_W), row(D_GLA), row(D_GLA), row(QK_W), _const_spec((1, GLA_DV))],
        out_specs=[row(D_GLA),
                   pl.BlockSpec((1, GLA_HEADS, GLA_DK, GLA_DV), lambda b, j: (b, 0, 0, 0))],
        out_shape=[jax.ShapeDtypeStruct((batch * seq, D_GLA), BF16),
                   jax.ShapeDtypeStruct((batch, GLA_HEADS, GLA_DK, GLA_DV), F32)],
        scratch_shapes=[pltpu.VMEM((GLA_HEADS, GLA_DV, GLA_DK), F32), pltpu.VMEM((rb, GLA_DV), F32)],
        compiler_params=pltpu.CompilerParams(dimension_semantics=("arbitrary", "arbitrary"),
                                             vmem_limit_bytes=VMEM_LIMIT),
        name="gla_prompt",
    )(q, k, v, r, la, g)


def _gla_sample_kernel(q_ref, k_ref, v_ref, r_ref, la_ref, g_ref, s0_ref, o_ref, sfin_ref, st_ref, oacc_ref,
                       *, n_seq, seq):
    qe, ke, kd, cum, mask = _gla_pre(q_ref[...].astype(F32), k_ref[...].astype(F32), la_ref[...], seq)
    g = g_ref[...]
    safe = jnp.min(cum) >= GLA_SAFE_LOG_DECAY

    @pl.when(safe)
    def _():
        lasts = jnp.concatenate([cum[(s + 1) * seq - 1:(s + 1) * seq, :] for s in range(n_seq)]
                                + [jnp.zeros((GLA_DK - n_seq, cum.shape[1]), F32)], axis=0)
        pair = 2 * seq
        upper = lax.broadcasted_iota(jnp.int32, (pair, GLA_DK), 0) < seq
        for h in range(GLA_HEADS):
            kc = slice(h * GLA_DK, (h + 1) * GLA_DK)
            vc = slice(h * GLA_DV, (h + 1) * GLA_DV)
            v = v_ref[:, vc]
            qe_h, kd_h = qe[:, kc], kd[:, kc]
            o = _gla_intra(qe_h, ke[:, kc], v, mask)
            dec_t = jnp.exp(lasts[:, kc].T)
            inter = []
            for p in range(n_seq // 2):
                rows = slice(p * pair, (p + 1) * pair)
                qe_p, kd_p, v_p = qe_h[rows], kd_h[rows], v[rows]
                for half in range(2):
                    s = 2 * p + half
                    s0 = s0_ref[s, h]
                    o_s = _dot(qe_p, s0.astype(BF16))
                    inter.append(o_s[half * seq:(half + 1) * seq])
                    kd_s = jnp.where(upper if half == 0 else ~upper, kd_p, jnp.zeros_like(kd_p))
                    dec = jnp.broadcast_to(dec_t[:, s:s + 1], (GLA_DK, GLA_DV))
                    sfin_ref[s, h] = s0 * dec + lax.dot_general(kd_s, v_p, _TN, preferred_element_type=F32)
            o = o + jnp.concatenate(inter, axis=0)
            o_ref[:, vc] = _gla_finish(o, r_ref[:, vc], g).astype(o_ref.dtype)

    @pl.when(jnp.logical_not(safe))
    def _():
        n_rows = n_seq * seq
        rows = lax.broadcasted_iota(jnp.int32, (n_rows, 1), 0)
        for h in range(GLA_HEADS):
            kc = slice(h * GLA_DK, (h + 1) * GLA_DK)
            vc = slice(h * GLA_DV, (h + 1) * GLA_DV)
            q, k, v = q_ref[:, kc].astype(F32), k_ref[:, kc].astype(F32), v_ref[:, vc]
            a_all = jnp.exp(la_ref[:, kc])
            oacc_ref[...] = jnp.zeros_like(oacc_ref)

            def step(t, carry):
                s = t // seq

                @pl.when(t % seq == 0)
                def _():
                    st_ref[...] = s0_ref[s, h].T

                _gla_token_step(t, rows, q, k, v, a_all, st_ref, oacc_ref)

                @pl.when(t % seq == seq - 1)
                def _():
                    sfin_ref[s, h] = st_ref[...].T

                return carry

            lax.fori_loop(0, n_rows, step, 0)
            o_ref[:, vc] = _gla_finish(oacc_ref[...], r_ref[:, vc], g).astype(o_ref.dtype)


def _gla_sample(q, k, v, r, la, g, s0, batch, seq):
    ns = 16
    rb = ns * seq
    row = lambda w: pl.BlockSpec((rb, w), lambda i: (i, 0))
    st = pl.BlockSpec((ns, GLA_HEADS, GLA_DK, GLA_DV), lambda i: (i, 0, 0, 0))
    return pl.pallas_call(
        functools.partial(_gla_sample_kernel, n_seq=ns, seq=seq),
        grid=(batch // ns,),
        in_specs=[row(QK_W), row(QK_W), row(D_GLA), row(D_GLA), row(QK_W), _const_spec((1, GLA_DV)), st],
        out_specs=[row(D_GLA), st],
        out_shape=[jax.ShapeDtypeStruct((batch * seq, D_GLA), BF16),
                   jax.ShapeDtypeStruct((batch, GLA_HEADS, GLA_DK, GLA_DV), F32)],
        scratch_shapes=[pltpu.VMEM((GLA_DV, GLA_DK), F32), pltpu.VMEM((rb, GLA_DV), F32)],
        compiler_params=pltpu.CompilerParams(dimension_semantics=("arbitrary",), vmem_limit_bytes=VMEM_LIMIT),
        name="gla_sample",
    )(q, k, v, r, la, g, s0)


def _s5_kernel(u_ref, wbu_ref, wc_ref, are_ref, aim_ref, d_ref, h0r_ref, h0i_ref,
               y_ref, sre_ref, sim_ref, bu_ref, xs_ref, car_ref, *, nb, tc, has_state):
    j = pl.program_id(2)

    @pl.when(j == 0)
    def _():
        if has_state:
            car_ref[0] = h0r_ref[...]
            car_ref[1] = h0i_ref[...]
        else:
            car_ref[...] = jnp.zeros_like(car_ref)

    u2 = u_ref[...].reshape(nb * tc, LANES)
    ub = u2.astype(BF16)
    nl = S5_SL // LANES
    for l in range(2 * nl):
        bu_ref[l] = _dot(ub, wbu_ref[0, :, l * LANES:(l + 1) * LANES])
    a_r = [jnp.broadcast_to(are_ref[0, :, l * LANES:(l + 1) * LANES], (nb, LANES)) for l in range(nl)]
    a_i = [jnp.broadcast_to(aim_ref[0, :, l * LANES:(l + 1) * LANES], (nb, LANES)) for l in range(nl)]

    def step(t, carry):
        rows = pl.ds(t, nb, stride=tc)
        out = []
        for l in range(nl):
            xr, xi = carry[2 * l], carry[2 * l + 1]
            nr = a_r[l] * xr - a_i[l] * xi + bu_ref[l, rows, :]
            ni = a_r[l] * xi + a_i[l] * xr + bu_ref[nl + l, rows, :]
            xs_ref[l, rows, :] = nr
            xs_ref[nl + l, rows, :] = ni
            out += [nr, ni]
        return tuple(out)

    init = []
    for l in range(nl):
        init += [car_ref[0, :, l * LANES:(l + 1) * LANES], car_ref[1, :, l * LANES:(l + 1) * LANES]]
    fin = lax.fori_loop(0, tc, step, tuple(init), unroll=8)
    xr = jnp.concatenate([fin[2 * l] for l in range(nl)], axis=1)
    xi = jnp.concatenate([fin[2 * l + 1] for l in range(nl)], axis=1)
    car_ref[0] = xr
    car_ref[1] = xi
    y = d_ref[0] * u2
    for l in range(2 * nl):
        y = y + _dot(xs_ref[l].astype(BF16), wc_ref[0, l * LANES:(l + 1) * LANES, :])
    y_ref[...] = y.reshape(nb, tc, LANES)

    @pl.when(j == pl.num_programs(2) - 1)
    def _():
        sre_ref[...] = xr
        sim_ref[...] = xi


def _s5(u3d, wbu, wc, a_re, a_im, dsk, h0r, h0i, nb, tc, has_state):
    batch, seq, _ = u3d.shape
    grid = (S5_NGB, batch // nb, seq // tc)
    st = pl.BlockSpec((nb, S5_SL), lambda g, b, j: (b, g))
    par = lambda w: pl.BlockSpec((1, 1, w), lambda g, b, j: (g, 0, 0))
    ublk = pl.BlockSpec((nb, tc, LANES), lambda g, b, j: (b, j, g))
    return pl.pallas_call(
        functools.partial(_s5_kernel, nb=nb, tc=tc, has_state=has_state),
        grid=grid,
        in_specs=[ublk,
                  pl.BlockSpec((1, LANES, 2 * S5_SL), lambda g, b, j: (g, 0, 0)),
                  pl.BlockSpec((1, 2 * S5_SL, LANES), lambda g, b, j: (g, 0, 0)),
                  par(S5_SL), par(S5_SL), par(LANES), st, st],
        out_specs=[ublk, st, st],
        out_shape=[jax.ShapeDtypeStruct(u3d.shape, F32),
                   jax.ShapeDtypeStruct((batch, S5_GROUPS * S5_STATE), F32),
                   jax.ShapeDtypeStruct((batch, S5_GROUPS * S5_STATE), F32)],
        scratch_shapes=[pltpu.VMEM((2 * S5_SL // LANES, nb * tc, LANES), F32),
                        pltpu.VMEM((2 * S5_SL // LANES, nb * tc, LANES), F32),
                        pltpu.VMEM((2, nb, S5_SL), F32)],
        compiler_params=pltpu.CompilerParams(dimension_semantics=("arbitrary",) * 3,
                                             vmem_limit_bytes=VMEM_LIMIT),
        name="s5_state" if has_state else "s5_zero",
    )(u3d, wbu, wc, a_re, a_im, dsk, h0r, h0i)


S5_J = 2
S5_HL = S5_SL // S5_J


def _s5_prompt_kernel(u_ref, wbu_ref, wc_ref, are_ref, aim_ref, d_ref, y_ref, sre_ref, sim_ref,
                      u2_ref, lhs_ref, bu_ref, xs_ref, y2_ref, yo_ref, car_ref, *, nb, tc):
    g = pl.program_id(0)
    j = pl.program_id(1)
    rows = nb * S5_J

    @pl.when((g == 0) & (j == 0))
    def _():
        lhs_ref[...] = jnp.zeros_like(lhs_ref)

    @pl.when(j == 0)
    def _():
        car_ref[...] = jnp.zeros_like(car_ref)

    u2 = u_ref[...].reshape(nb * tc, LANES)
    u2_ref[...] = u2

    def build(t, c):
        u4 = u2_ref[pl.ds(t, nb, stride=tc), :]
        for jj in range(S5_J):
            lhs_ref[t, jj * nb:(jj + 1) * nb, jj * LANES:(jj + 1) * LANES] = u4
        return c

    lax.fori_loop(0, tc, build, 0, unroll=8)
    lhs = lhs_ref[...].reshape(tc * rows, S5_J * LANES).astype(BF16)
    bu_ref[...] = _dot(lhs, wbu_ref[0]).reshape(tc, rows, 2 * S5_HL)
    ar = are_ref[0]
    ai = aim_ref[0]

    def step(t, carry):
        xr, xi = carry
        tile = bu_ref[t]
        nr = ar * xr - ai * xi + tile[:, 0:S5_HL]
        ni = ar * xi + ai * xr + tile[:, S5_HL:2 * S5_HL]
        xs_ref[t] = jnp.concatenate([nr, ni], axis=1)
        return nr, ni

    xr, xi = lax.fori_loop(0, tc, step, (car_ref[0], car_ref[1]), unroll=8)
    car_ref[0] = xr
    car_ref[1] = xi
    xs = xs_ref[...].reshape(tc * rows, 2 * S5_HL).astype(BF16)
    y2_ref[...] = _dot(xs, wc_ref[0]).reshape(tc, rows, S5_J * LANES)
    first_half = lax.broadcasted_iota(jnp.int32, (rows, LANES), 0) < nb

    def unperm(t, c):
        t2 = y2_ref[t]
        part = jnp.where(first_half, t2[:, 0:LANES], t2[:, LANES:2 * LANES])
        tot = part + pltpu.roll(part, nb, axis=0)
        yo_ref[pl.ds(t, nb, stride=tc), :] = tot[0:nb]
        return c

    lax.fori_loop(0, tc, unperm, 0, unroll=8)
    y_ref[...] = (yo_ref[...] + d_ref[0] * u2).reshape(nb, tc, LANES)

    @pl.when(j == pl.num_programs(1) - 1)
    def _():
        sre_ref[...] = jnp.concatenate([xr[jj * nb:(jj + 1) * nb] for jj in range(S5_J)], axis=1)
        sim_ref[...] = jnp.concatenate([xi[jj * nb:(jj + 1) * nb] for jj in range(S5_J)], axis=1)


def _s5_prompt(u3d, wbu2, wc2, a_re2, a_im2, dsk, tc):
    nb, seq, _ = u3d.shape
    rows = nb * S5_J
    assert rows == 8
    st = pl.BlockSpec((nb, S5_SL), lambda g, j: (0, g))
    gblk = lambda s: pl.BlockSpec((1,) + s, lambda g, j: (g, 0, 0))
    ublk = pl.BlockSpec((nb, tc, LANES), lambda g, j: (0, j, g))
    return pl.pallas_call(
        functools.partial(_s5_prompt_kernel, nb=nb, tc=tc),
        grid=(S5_NGB, seq // tc),
        in_specs=[ublk, gblk((S5_J * LANES, 2 * S5_HL)), gblk((2 * S5_HL, S5_J * LANES)),
                  gblk((rows, S5_HL)), gblk((rows, S5_HL)), gblk((1, LANES))],
        out_specs=[ublk, st, st],
        out_shape=[jax.ShapeDtypeStruct(u3d.shape, F32),
                   jax.ShapeDtypeStruct((nb, S5_GROUPS * S5_STATE), F32),
                   jax.ShapeDtypeStruct((nb, S5_GROUPS * S5_STATE), F32)],
        scratch_shapes=[pltpu.VMEM((nb * tc, LANES), F32),
                        pltpu.VMEM((tc, rows, S5_J * LANES), F32),
                        pltpu.VMEM((tc, rows, 2 * S5_HL), F32),
                        pltpu.VMEM((tc, rows, 2 * S5_HL), F32),
                        pltpu.VMEM((tc, rows, S5_J * LANES), F32),
                        pltpu.VMEM((nb * tc, LANES), F32),
                        pltpu.VMEM((2, rows, S5_HL), F32)],
        compiler_params=pltpu.CompilerParams(dimension_semantics=("arbitrary",) * 2,
                                             vmem_limit_bytes=VMEM_LIMIT),
        name="s5_prompt",
    )(u3d, wbu2, wc2, a_re2, a_im2, dsk)


def _s5_prompt_params(wbu, wc, a_re, a_im, nb):
    h = S5_HL
    top = jnp.concatenate([wbu[:, :, 0:h], wbu[:, :, S5_SL:S5_SL + h]], axis=2)
    bot = jnp.concatenate([wbu[:, :, h:2 * h], wbu[:, :, S5_SL + h:S5_SL + 2 * h]], axis=2)
    wbu2 = jnp.concatenate([top, bot], axis=1)
    wc_j = [jnp.concatenate([wc[:, jj * h:(jj + 1) * h, :], wc[:, S5_SL + jj * h:S5_SL + (jj + 1) * h, :]],
                            axis=1) for jj in range(S5_J)]
    wc2 = jnp.concatenate(wc_j, axis=2)
    tile = lambda a: jnp.repeat(a.reshape(S5_NGB, S5_J, h), nb, axis=1)
    return wbu2, wc2, tile(a_re), tile(a_im)


def _mix_out_kernel(xp_ref, xs_ref, ogp_ref, ogs_ref, ysp_ref, yss_ref, glu_w_ref, glu_b_ref, s5n_ref, wo_ref,
                    nffn_ref, wr_ref, h1_ref, hn_ref, rt_ref, cnt_ref, *, n_prompt_tiles):
    is_p = pl.program_id(0) < n_prompt_tiles
    x = jnp.where(is_p, xp_ref[...], xs_ref[...])
    og = jnp.where(is_p, ogp_ref[...], ogs_ref[...])
    y = jax.nn.gelu(jnp.where(is_p, ysp_ref[...], yss_ref[...]))
    y = y * jax.nn.sigmoid(_dot(y.astype(BF16), glu_w_ref[...]) + glu_b_ref[...])
    yn = _rms(y, s5n_ref[...]).astype(BF16)
    mix = _dot(og, wo_ref[0:D_GLA, :]) + _dot(yn, wo_ref[D_GLA:D_GLA + D_S5, :])
    h1 = x + mix
    h1_ref[...] = h1
    hn = _rms(h1, nffn_ref[...])
    hn_ref[...] = _pack_bf16_pair(hn[:, 0:HALF], hn[:, HALF:D_MODEL])
    hn_hi = hn.astype(BF16)
    hn_lo = (hn - hn_hi.astype(F32)).astype(BF16)
    logits = _dot(hn_hi, wr_ref[0]) + _dot(hn_hi, wr_ref[1]) + _dot(hn_lo, wr_ref[0])
    rt = _route(logits)

    @pl.when(pl.program_id(0) == 0)
    def _():
        cnt_ref[...] = jnp.zeros_like(cnt_ref)

    tm = rt.shape[0]
    lane = lax.broadcasted_iota(jnp.int32, rt.shape, 1).astype(F32)
    oh0 = lane == rt[:, 0:1]
    oh1 = lane == rt[:, 1:2]
    both = jnp.where(oh0 | oh1, 1.0, 0.0)
    ri = lax.broadcasted_iota(jnp.int32, (tm, tm), 0)
    ci = lax.broadcasted_iota(jnp.int32, (tm, tm), 1)
    before = _dot(jnp.where(ri > ci, 1.0, 0.0).astype(BF16), both.astype(BF16)) + cnt_ref[...]
    rank0 = jnp.sum(jnp.where(oh0, before, 0.0), axis=-1, keepdims=True)
    rank1 = jnp.sum(jnp.where(oh1, before, 0.0), axis=-1, keepdims=True)
    cnt_ref[...] += jnp.sum(both, axis=0, keepdims=True)
    rt_ref[...] = jnp.where(lane == 4.0, rank0, jnp.where(lane == 5.0, rank1, rt))


def _route(logits):
    col = lax.broadcasted_iota(jnp.int32, logits.shape, 1)
    colf = col.astype(F32)
    neg = -jnp.inf

    def first_argmax(vals):
        m = jnp.max(vals, axis=-1, keepdims=True)
        idx = jnp.min(jnp.where(vals == m, colf, float(LANES)), axis=-1, keepdims=True)
        return m, idx

    lg = jnp.where(col < N_EGROUPS, logits, neg)
    gmax, gsel = first_argmax(lg)
    p_g = 1.0 / jnp.sum(jnp.exp(lg - gmax), axis=-1, keepdims=True)
    ecol = col - N_EGROUPS
    egrp = (ecol >> 3).astype(F32)
    in_group = (ecol >= 0) & (ecol < N_EXPERTS) & (egrp == gsel)
    le = jnp.where(in_group, logits, neg)
    m1, i1 = first_argmax(le)
    le2 = jnp.where(colf == i1, neg, le)
    m2, i2 = first_argmax(le2)
    e2 = jnp.exp(m2 - m1)
    den = 1.0 + e2
    w1 = p_g * (1.0 / den)
    w2 = p_g * (e2 / den)
    e1f = i1 - float(N_EGROUPS)
    e2f = i2 - float(N_EGROUPS)
    out = jnp.where(col == 0, e1f, jnp.where(col == 1, e2f, jnp.where(col == 2, w1, jnp.where(col == 3, w2, 0.0))))
    return out


def _mix_out(xp, xs, ogp, ogs, ysp, yss, glu_w, glu_b, s5n, wo, nffn, wr):
    tm = TOK_TILE
    npt, nst = xp.shape[0] // tm, xs.shape[0] // tm
    t = (npt + nst) * tm
    row = lambda w: pl.BlockSpec((tm, w), lambda i: (i, 0))
    prow = lambda w: pl.BlockSpec((tm, w), lambda i: (jnp.minimum(i, npt - 1), 0))
    srow = lambda w: pl.BlockSpec((tm, w), lambda i: (jnp.maximum(i - npt, 0), 0))
    return pl.pallas_call(
        functools.partial(_mix_out_kernel, n_prompt_tiles=npt),
        grid=(npt + nst,),
        in_specs=[prow(D_MODEL), srow(D_MODEL), prow(D_GLA), srow(D_GLA), prow(D_S5), srow(D_S5),
                  _const_spec(glu_w.shape), _const_spec((1, D_S5)), _const_spec((1, D_S5)),
                  _const_spec(wo.shape), _const_spec((1, D_MODEL)), _const_spec(wr.shape)],
        out_specs=[row(D_MODEL), row(HALF), row(LANES), pl.BlockSpec((1, LANES), lambda i: (0, 0))],
        out_shape=[jax.ShapeDtypeStruct((t, D_MODEL), F32), jax.ShapeDtypeStruct((t, HALF), jnp.uint32),
                   jax.ShapeDtypeStruct((t, LANES), F32), jax.ShapeDtypeStruct((1, LANES), F32)],
        compiler_params=pltpu.CompilerParams(dimension_semantics=("arbitrary",), vmem_limit_bytes=VMEM_LIMIT),
        name="mix_out",
    )(xp, xs, ogp, ogs, ysp, yss, glu_w, glu_b, s5n, wo, nffn, wr)


DMA_UNROLL = 8


MOE_SB = 3
HALF = D_MODEL // 2


def _pack_bf16_pair(lo, hi):
    def bits(x):
        b = pltpu.bitcast(x, jnp.uint32)
        return (b + jnp.uint32(0x7FFF) + ((b >> 16) & jnp.uint32(1))) >> 16
    return bits(lo) | (bits(hi) << 16)


def _unpack_bf16_pair(u):
    return (pltpu.bitcast(u << 16, F32), pltpu.bitcast(u & jnp.uint32(0xFFFF0000), F32))


def _moe_kernel(se_ref, sm_ref, sn_ref, sbase_ref, sa_ref, hn_hbm, wg_ref, wu_ref, wd_ref, out_hbm,
                xb_ref, yb_ref, wgb_ref, wub_ref, wdb_ref, gsem, ssem):
    b = pl.program_id(0)
    nb = pl.num_programs(0)
    n = sn_ref[b]
    slot = lax.rem(b, 2)

    def round_up(cnt):
        return (cnt + DMA_UNROLL - 1) // DMA_UNROLL * DMA_UNROLL

    def issue_gather(blk, sl):
        base = sbase_ref[blk]

        def grp(g, c):
            for j in range(DMA_UNROLL):
                i = g * DMA_UNROLL + j
                tok = sa_ref[base + i] >> 1
                pltpu.make_async_copy(hn_hbm.at[pl.ds(tok, 1), :], xb_ref.at[sl, pl.ds(i, 1), :],
                                      gsem.at[sl]).start()
            return c

        lax.fori_loop(0, round_up(sn_ref[blk]) // DMA_UNROLL, grp, 0)

    def wait_gather(cnt, sl):
        rows = pl.ds(0, pl.multiple_of(round_up(cnt), DMA_UNROLL))
        pltpu.make_async_copy(hn_hbm.at[rows, :], xb_ref.at[sl, rows, :], gsem.at[sl]).wait()

    m_cur = sm_ref[b]
    m_prev = sm_ref[jnp.maximum(b - 1, 0)]
    has_prev = (b > 0) & (m_prev > 0)

    def out_copy(rows):
        dst = pl.ds(pl.multiple_of(sbase_ref[b], MOE_BLK), rows)
        return pltpu.make_async_copy(yb_ref.at[pl.ds(0, rows), :], out_hbm.at[dst, :], ssem.at[0])

    def wait_prev_out():
        rows = pl.ds(0, pl.multiple_of(m_prev * MOE_BLK, MOE_BLK))
        pltpu.make_async_copy(yb_ref.at[rows, :], out_hbm.at[rows, :], ssem.at[0]).wait()

    def write_out(rows):
        out_copy(rows).start()

        @pl.when(b == nb - 1)
        def _():
            out_copy(rows).wait()

    @pl.when(b == 0)
    def _():
        xb_ref[...] = jnp.zeros_like(xb_ref)
        issue_gather(0, 0)

    @pl.when(b + 1 < nb)
    def _():
        issue_gather(b + 1, 1 - slot)

    def compute(rows):
        x_lo, x_hi = _unpack_bf16_pair(xb_ref[slot, 0:rows, :])
        x_lo, x_hi = x_lo.astype(BF16), x_hi.astype(BF16)
        gate = _dot(x_lo, wgb_ref[0:HALF, :]) + _dot(x_hi, wgb_ref[HALF:D_MODEL, :])
        up = _dot(x_lo, wub_ref[0:HALF, :]) + _dot(x_hi, wub_ref[HALF:D_MODEL, :])
        hid = (gate * jax.nn.sigmoid(gate) * up).astype(BF16)

        @pl.when(has_prev)
        def _():
            wait_prev_out()

        yb_ref[0:rows, :] = _pack_bf16_pair(_dot(hid, wdb_ref[:, 0:HALF]), _dot(hid, wdb_ref[:, HALF:D_MODEL]))
        write_out(rows)

    @pl.when(n > 0)
    def _():
        prev_e = se_ref[jnp.maximum(b - 1, 0)]

        @pl.when((b == 0) | (prev_e != se_ref[b]))
        def _():
            wgb_ref[...] = wg_ref[...].astype(BF16)
            wub_ref[...] = wu_ref[...].astype(BF16)
            wdb_ref[...] = wd_ref[...].astype(BF16)

        wait_gather(n, slot)
        for m in range(1, MOE_SB + 1):
            @pl.when(m_cur == m)
            def _():
                compute(m * MOE_BLK)

    @pl.when((n == 0) & (m_cur > 0))
    def _():
        @pl.when(has_prev)
        def _():
            wait_prev_out()

        yb_ref[...] = jnp.zeros_like(yb_ref)
        for m in range(1, MOE_SB + 1):
            @pl.when(m_cur == m)
            def _():
                write_out(m * MOE_BLK)

    @pl.when((m_cur == 0) & has_prev)
    def _():
        wait_prev_out()


def _moe(hn_pk, plan, w_gate, w_up, w_down, n_out_rows):
    sb_e, sb_m, sb_n, sb_base, slot_a = plan
    sb_rows = MOE_SB * MOE_BLK
    wspec = lambda s: pl.BlockSpec((None,) + s, lambda b, se, *_: (se[b], 0, 0))
    grid_spec = pltpu.PrefetchScalarGridSpec(
        num_scalar_prefetch=5,
        grid=(sb_e.shape[0],),
        in_specs=[pl.BlockSpec(memory_space=pl.ANY),
                  wspec((D_MODEL, D_EXPERT)), wspec((D_MODEL, D_EXPERT)), wspec((D_EXPERT, D_MODEL))],
        out_specs=pl.BlockSpec(memory_space=pl.ANY),
        scratch_shapes=[pltpu.VMEM((2, sb_rows, HALF), jnp.uint32), pltpu.VMEM((sb_rows, HALF), jnp.uint32),
                        pltpu.VMEM((D_MODEL, D_EXPERT), BF16), pltpu.VMEM((D_MODEL, D_EXPERT), BF16),
                        pltpu.VMEM((D_EXPERT, D_MODEL), BF16), pltpu.SemaphoreType.DMA((2,)),
                        pltpu.SemaphoreType.DMA((1,))],
    )
    return pl.pallas_call(
        _moe_kernel,
        grid_spec=grid_spec,
        out_shape=jax.ShapeDtypeStruct((n_out_rows, HALF), jnp.uint32),
        compiler_params=pltpu.CompilerParams(dimension_semantics=("arbitrary",),
                                             vmem_limit_bytes=MOE_VMEM_LIMIT),
        name="moe_experts",
    )(sb_e, sb_m, sb_n, sb_base, slot_a, hn_pk, w_gate, w_up, w_down)


def _moe_plan(rt_all, cnt):
    t_all = rt_all.shape[0]
    n_assign = 2 * t_all
    e_flat = rt_all[:, 0:2].astype(jnp.int32).reshape(-1)
    rank = rt_all[:, 4:6].astype(jnp.int32).reshape(-1)
    counts = cnt[0, 0:N_EXPERTS].astype(jnp.int32)
    padded = (counts + MOE_BLK - 1) // MOE_BLK * MOE_BLK
    pad_end = jnp.cumsum(padded)
    pad_start = pad_end - padded
    dest = pad_start[e_flat] + rank
    nblk = -(-(n_assign + N_EXPERTS * (MOE_BLK - 1)) // MOE_BLK)
    n_slots = nblk * MOE_BLK
    slot_a = jnp.zeros((n_slots,), jnp.int32).at[dest].set(jnp.arange(n_assign, dtype=jnp.int32))
    k_e = padded // MOE_BLK
    sbc = (k_e + MOE_SB - 1) // MOE_SB
    sb_end = jnp.cumsum(sbc)
    sb_start = sb_end - sbc
    n_sb = (nblk + (MOE_SB - 1) * N_EXPERTS) // MOE_SB
    s = jnp.arange(n_sb, dtype=jnp.int32)
    sb_e = jnp.minimum(jnp.sum((s[:, None] >= sb_end[None, :]).astype(jnp.int32), axis=1), N_EXPERTS - 1)
    j = s - sb_start[sb_e]
    real = s < sb_end[-1]
    tail_blk = pad_end[-1] // MOE_BLK + MOE_SB * (s - sb_end[-1])
    sb_m = jnp.where(real, jnp.clip(k_e[sb_e] - MOE_SB * j, 0, MOE_SB), jnp.clip(nblk - tail_blk, 0, MOE_SB))
    sb_n = jnp.where(real, jnp.clip(counts[sb_e] - MOE_SB * MOE_BLK * j, 0, MOE_SB * MOE_BLK), 0)
    sb_base = jnp.where(real, pad_start[sb_e] + MOE_SB * MOE_BLK * j, jnp.minimum(tail_blk, nblk - 1) * MOE_BLK)
    i32 = lambda a: a.astype(jnp.int32)
    return (i32(sb_e), i32(sb_m), i32(sb_n), i32(sb_base), slot_a), i32(dest), n_slots


def _ple_out_kernel(dest_ref, h1_ref, rt_ref, pp_ref, ps_ref, nple_ref, wpg_ref, wp_ref, nfin_ref, ys_hbm,
                    op_ref, os_ref, yb_ref, sem, *, n_prompt_tiles):
    i = pl.program_id(0)
    nt = pl.num_programs(0)
    tm = h1_ref.shape[0]
    slot = lax.rem(i, 2)

    def issue(tile, sl):
        base = tile * (2 * tm)

        def grp(g, c):
            for j in range(DMA_UNROLL):
                d = dest_ref[base + g * DMA_UNROLL + j]
                row = g * (DMA_UNROLL // 2) + j // 2
                pltpu.make_async_copy(ys_hbm.at[pl.ds(d, 1), :], yb_ref.at[sl, j % 2, pl.ds(row, 1), :],
                                      sem.at[sl]).start()
            return c

        lax.fori_loop(0, 2 * tm // DMA_UNROLL, grp, 0)

    @pl.when(i == 0)
    def _():
        issue(0, 0)

    def wait_rows(sl):
        for k in range(2):
            pltpu.make_async_copy(ys_hbm.at[pl.ds(0, tm), :], yb_ref.at[sl, k], sem.at[sl]).wait()

    wait_rows(slot)
    is_p = i < n_prompt_tiles
    rt = rt_ref[...]
    w0, w1 = rt[:, 2:3], rt[:, 3:4]
    lo0, hi0 = _unpack_bf16_pair(yb_ref[slot, 0])
    lo1, hi1 = _unpack_bf16_pair(yb_ref[slot, 1])
    h2 = h1_ref[...] + jnp.concatenate([w0 * lo0 + w1 * lo1, w0 * hi0 + w1 * hi1], axis=1)
    hb = _rms(h2, nple_ref[...]).astype(BF16)
    nbase = jnp.minimum(i + 1, nt - 1) * (2 * tm)
    n_chunk = 8
    cw, ca = D_MODEL // n_chunk, 2 * tm // n_chunk
    gates = []
    for c in range(n_chunk):
        gates.append(jax.nn.sigmoid(_dot(hb, wpg_ref[:, c * cw:(c + 1) * cw])))
        for a in range(c * ca, (c + 1) * ca):
            d = dest_ref[nbase + a]
            pltpu.make_async_copy(ys_hbm.at[pl.ds(d, 1), :], yb_ref.at[1 - slot, a % 2, pl.ds(a // 2, 1), :],
                                  sem.at[1 - slot]).start()
    gate = jnp.concatenate(gates, axis=1)

    @pl.when(i == nt - 1)
    def _():
        wait_rows(1 - slot)

    p = jnp.where(is_p, pp_ref[...], ps_ref[...])
    h3 = h2 + _dot(p.astype(BF16), wp_ref[...]) * gate
    y = _rms(h3, nfin_ref[...])

    @pl.when(is_p)
    def _():
        op_ref[...] = y

    @pl.when(jnp.logical_not(is_p))
    def _():
        os_ref[...] = y


def _ple_out(dest, h1, rt, pp, ps, nple, wpg, wp, nfin, y_slots):
    tm = TOK_TILE
    npt, nst = pp.shape[0] // tm, ps.shape[0] // tm
    row = lambda w: pl.BlockSpec((tm, w), lambda i, d: (i, 0))
    prow = lambda w: pl.BlockSpec((tm, w), lambda i, d: (jnp.minimum(i, npt - 1), 0))
    srow = lambda w: pl.BlockSpec((tm, w), lambda i, d: (jnp.maximum(i - npt, 0), 0))
    const = lambda shape: pl.BlockSpec(shape, lambda i, d: (0,) * len(shape), pipeline_mode=pl.Buffered(1))
    grid_spec = pltpu.PrefetchScalarGridSpec(
        num_scalar_prefetch=1,
        grid=(npt + nst,),
        in_specs=[row(D_MODEL), row(LANES), prow(D_PLE), srow(D_PLE), const((1, D_MODEL)), const(wpg.shape),
                  const(wp.shape), const((1, D_MODEL)), pl.BlockSpec(memory_space=pl.ANY)],
        out_specs=[prow(D_MODEL), srow(D_MODEL)],
        scratch_shapes=[pltpu.VMEM((2, 2, tm, HALF), jnp.uint32), pltpu.SemaphoreType.DMA((2,))],
    )
    return pl.pallas_call(
        functools.partial(_ple_out_kernel, n_prompt_tiles=npt),
        grid_spec=grid_spec,
        out_shape=[jax.ShapeDtypeStruct((pp.shape[0], D_MODEL), F32),
                   jax.ShapeDtypeStruct((ps.shape[0], D_MODEL), F32)],
        compiler_params=pltpu.CompilerParams(dimension_semantics=("arbitrary",), vmem_limit_bytes=VMEM_LIMIT),
        name="ple_out",
    )(dest, h1, rt, pp, ps, nple, wpg, wp, nfin, y_slots)


def _s5_params(lam_re, lam_im, log_dt, b_re, b_im, c_re, c_im, d_skip):
    dt = jnp.exp(log_dt)[:, None]
    mag = jnp.exp(lam_re * dt)
    ab_re = mag * jnp.cos(lam_im * dt)
    ab_im = mag * jnp.sin(lam_im * dt)
    den = lam_re * lam_re + lam_im * lam_im
    nr = ab_re - 1.0
    f_re = (nr * lam_re + ab_im * lam_im) / den
    f_im = (ab_im * lam_re - nr * lam_im) / den
    bb_re = f_re[..., None] * b_re - f_im[..., None] * b_im
    bb_im = f_re[..., None] * b_im + f_im[..., None] * b_re
    eye = jnp.eye(S5_GB, dtype=F32)

    def bu_w(bb):
        bb = bb.reshape(S5_NGB, S5_GB, S5_STATE, S5_GROUP)
        w = jnp.einsum('nlph,lm->nlhmp', bb, eye)
        return w.reshape(S5_NGB, S5_GB * S5_GROUP, S5_SL)

    def c_w(c):
        c = c.reshape(S5_NGB, S5_GB, S5_GROUP, S5_STATE)
        w = jnp.einsum('nlhp,lm->nlpmh', c, eye)
        return w.reshape(S5_NGB, S5_SL, S5_GB * S5_GROUP)

    wbu = jnp.concatenate([bu_w(bb_re), bu_w(bb_im)], axis=2).astype(BF16)
    wc = jnp.concatenate([c_w(c_re), -c_w(c_im)], axis=1).astype(BF16)
    a_re = ab_re.reshape(S5_NGB, 1, S5_SL)
    a_im = ab_im.reshape(S5_NGB, 1, S5_SL)
    dsk = d_skip.reshape(S5_NGB, 1, S5_GB * S5_GROUP)
    return wbu, wc, a_re, a_im, dsk


def kernel(x_prompt, x_sample, p_prompt, p_sample, state_gla, state_s5_re, state_s5_im, norm_mix, w_in, gla_w_gate_up, gla_gate_bias, gla_norm, s5_lam_re, s5_lam_im, s5_log_dt, s5_b_re, s5_b_im, s5_c_re, s5_c_im, s5_d, s5_glu_w, s5_glu_b, s5_norm, w_out, norm_ffn, router_group, router_expert, w_gate, w_up, w_down, norm_ple, w_ple, w_ple_gate, norm_final):
    depth = w_in.shape[0]
    assert depth == 1
    i = 0
    bp, lp, _ = x_prompt.shape
    bs, ls, _ = x_sample.shape
    tp, ts = bp * lp, bs * ls
    t_all = tp + ts

    wm = _w_in_prep(w_in[i])
    wgu = jnp.pad(gla_w_gate_up[i], ((0, LANES - GLA_RANK), (0, 0))).astype(BF16)
    gbias = gla_gate_bias[i].reshape(1, QK_W)
    wbu, wc, a_re, a_im, dsk = _s5_params(s5_lam_re[i], s5_lam_im[i], s5_log_dt[i], s5_b_re[i], s5_b_im[i],
                                          s5_c_re[i], s5_c_im[i], s5_d[i])
    glu_w = s5_glu_w[i].astype(BF16)
    wo = w_out[i].astype(BF16)
    wr32 = jnp.pad(jnp.concatenate([router_group[i], router_expert[i]], axis=1),
                   ((0, 0), (0, LANES - N_EGROUPS - N_EXPERTS)))
    wr_hi = wr32.astype(BF16)
    wr = jnp.stack([wr_hi, (wr32 - wr_hi.astype(F32)).astype(BF16)])
    wpg = w_ple_gate[i].astype(BF16)
    wp = w_ple[i].astype(BF16)
    vec = lambda a: a.reshape(1, -1)

    xp = x_prompt.reshape(tp, D_MODEL)
    xs = x_sample.reshape(ts, D_MODEL)

    qp, kp, vp, rp, lap, up = _in_proj(xp, vec(norm_mix[i]), wm, wgu, gbias, BF16)
    qs, ks, vs, rs, las, us = _in_proj(xs, vec(norm_mix[i]), wm, wgu, gbias, BF16)
    ogp, gla_p = _gla_prompt(qp, kp, vp, rp, lap, vec(gla_norm[i]), bp, lp)
    ogs, gla_s = _gla_sample(qs, ks, vs, rs, las, vec(gla_norm[i]), state_gla[i], bs, ls)
    wbu2, wc2, a_re2, a_im2 = _s5_prompt_params(wbu, wc, a_re, a_im, bp)
    ysp, re_p, im_p = _s5_prompt(up.reshape(bp, lp, D_S5), wbu2, wc2, a_re2, a_im2, dsk, tc=256)
    yss, re_s, im_s = _s5(us.reshape(bs, ls, D_S5), wbu, wc, a_re, a_im, dsk,
                          state_s5_re[i].reshape(bs, -1), state_s5_im[i].reshape(bs, -1),
                          nb=32, tc=ls, has_state=True)

    h1, hn_all, rt_all, cnt = _mix_out(xp, xs, ogp, ogs, ysp.reshape(tp, D_S5), yss.reshape(ts, D_S5), glu_w,
                                  vec(s5_glu_b[i]), vec(s5_norm[i]), wo, vec(norm_ffn[i]), wr)

    plan, dest, n_slots = _moe_plan(rt_all, cnt)
    y_slots = _moe(hn_all, plan, w_gate[i], w_up[i], w_down[i], n_slots)

    y_p, y_s = _ple_out(dest, h1, rt_all, p_prompt[i].reshape(tp, D_PLE), p_sample[i].reshape(ts, D_PLE),
                        vec(norm_ple[i]), wpg, wp, vec(norm_final), y_slots)

    s5shape = lambda a, b: a.reshape(1, b, S5_GROUPS, S5_STATE)
    return (y_p.reshape(bp, lp, D_MODEL), y_s.reshape(bs, ls, D_MODEL),
            gla_p[None], s5shape(re_p, bp), s5shape(im_p, bp),
            gla_s[None], s5shape(re_s, bs), s5shape(im_s, bs))
```

```python
import functools
import math

import jax
import jax.numpy as jnp
from jax import lax
from jax.experimental import pallas as pl
from jax.experimental.pallas import tpu as pltpu

F32 = jnp.float32
BF16 = jnp.bfloat16

D_MODEL = 2048
D_GLA = 1024
D_S5 = 1024
GLA_HEADS = 4
GLA_DV = 256
GLA_DK = 128
GLA_RANK = 16
GLA_CHUNK = 64
S5_GROUP = 16
S5_GROUPS = 64
S5_STATE = 64
N_EGROUPS = 4
N_EPG = 8
N_EXPERTS = 32
D_EXPERT = 512
D_PLE = 256
EPS = 1e-6

LANES = 128
QK_W = GLA_HEADS * GLA_DK
S5_GB = 8
S5_NGB = S5_GROUPS // S5_GB
S5_SL = S5_GB * S5_STATE
TOK_TILE = 256
MOE_BLK = 256
VMEM_LIMIT = 56 * 1024 * 1024
MOE_VMEM_LIMIT = 60 * 1024 * 1024


def _const_spec(shape):
    nd = len(shape)
    return pl.BlockSpec(shape, lambda *_: (0,) * nd, pipeline_mode=pl.Buffered(1))


def _rms(x, g):
    return x * lax.rsqrt(jnp.mean(x * x, axis=-1, keepdims=True) + EPS) * g


def _dot(a, b):
    return jnp.dot(a, b, preferred_element_type=F32)


def _log_sigmoid(x):
    return -(jnp.maximum(-x, 0.0) + jnp.log1p(jnp.exp(-jnp.abs(x))))


N_QKVR = 2 * QK_W + 2 * D_GLA
W_IN_COLS = N_QKVR + GLA_RANK + D_S5


def _w_in_prep_kernel(w_ref, o_ref):
    o_ref[:, 0:N_QKVR] = w_ref[:, 0:N_QKVR].astype(BF16)
    tail = w_ref[:, N_QKVR:W_IN_COLS]
    o_ref[:, N_QKVR:N_QKVR + D_S5] = tail[:, GLA_RANK:GLA_RANK + D_S5].astype(BF16)
    o_ref[:, N_QKVR + D_S5:N_QKVR + D_S5 + LANES] = tail[:, 0:LANES].astype(BF16)


def _w_in_prep(wi):
    rows = 256
    return pl.pallas_call(
        _w_in_prep_kernel,
        grid=(D_MODEL // rows,),
        in_specs=[pl.BlockSpec((rows, W_IN_COLS), lambda i: (i, 0))],
        out_specs=pl.BlockSpec((rows, N_QKVR + D_S5 + LANES), lambda i: (i, 0)),
        out_shape=jax.ShapeDtypeStruct((D_MODEL, N_QKVR + D_S5 + LANES), BF16),
        compiler_params=pltpu.CompilerParams(dimension_semantics=("arbitrary",), vmem_limit_bytes=VMEM_LIMIT),
        name="w_in_prep",
    )(wi)


def _in_proj_kernel(x_ref, g_ref, wm_ref, wgu_ref, gb_ref,
                    q_ref, k_ref, v_ref, r_ref, la_ref, u_ref):
    hb = _rms(x_ref[...], g_ref[...]).astype(BF16)

    def seg(a, b):
        return _dot(hb, wm_ref[:, a:b])

    q_ref[...] = (seg(0, QK_W) * (GLA_DK ** -0.5)).astype(q_ref.dtype)
    k_ref[...] = seg(QK_W, 2 * QK_W).astype(k_ref.dtype)
    v_ref[...] = seg(2 * QK_W, 2 * QK_W + D_GLA).astype(v_ref.dtype)
    r_ref[...] = seg(2 * QK_W + D_GLA, N_QKVR).astype(r_ref.dtype)
    u_ref[...] = seg(N_QKVR, N_QKVR + D_S5)
    zg = seg(N_QKVR + D_S5, N_QKVR + D_S5 + LANES)
    xg = _dot(zg.astype(BF16), wgu_ref[...]) + gb_ref[...]
    la_ref[...] = _log_sigmoid(xg) * (1.0 / 16.0)


def _in_proj(x2d, g, wm, wgu, gbias, act_dtype):
    t = x2d.shape[0]
    tm = TOK_TILE
    row = lambda w: pl.BlockSpec((tm, w), lambda i: (i, 0))
    return pl.pallas_call(
        _in_proj_kernel,
        grid=(t // tm,),
        in_specs=[row(D_MODEL), _const_spec((1, D_MODEL)), _const_spec(wm.shape),
                  _const_spec(wgu.shape), _const_spec((1, QK_W))],
        out_specs=[row(QK_W), row(QK_W), row(D_GLA), row(D_GLA), row(QK_W), row(D_S5)],
        out_shape=[jax.ShapeDtypeStruct((t, QK_W), act_dtype), jax.ShapeDtypeStruct((t, QK_W), act_dtype),
                   jax.ShapeDtypeStruct((t, D_GLA), act_dtype), jax.ShapeDtypeStruct((t, D_GLA), act_dtype),
                   jax.ShapeDtypeStruct((t, QK_W), F32), jax.ShapeDtypeStruct((t, D_S5), F32)],
        compiler_params=pltpu.CompilerParams(dimension_semantics=("arbitrary",), vmem_limit_bytes=VMEM_LIMIT),
        name="in_proj",
    )(x2d, g, wm, wgu, gbias)


_NT = (((1,), (1,)), ((), ()))
_TN = (((0,), (0,)), ((), ()))


def _gla_pre(q, k, la, c):
    r = q.shape[0]
    shift = int(math.log2(c))
    ri = lax.broadcasted_iota(jnp.int32, (r, r), 0)
    si = lax.broadcasted_iota(jnp.int32, (r, r), 1)
    mask = ((ri >> shift) == (si >> shift)) & (ri >= si)
    tri = jnp.where(mask, 1.0, 0.0).astype(BF16)
    hi = la.astype(BF16)
    r1 = la - hi.astype(F32)
    mid = r1.astype(BF16)
    lo = (r1 - mid.astype(F32)).astype(BF16)
    cum = _dot(tri, hi) + _dot(tri, mid) + _dot(tri, lo)
    last = jnp.concatenate([jnp.broadcast_to(cum[(i + 1) * c - 1:(i + 1) * c, :], (c, cum.shape[1]))
                            for i in range(r // c)], axis=0)
    qe = (q * jnp.exp(cum)).astype(BF16)
    ke = (k * jnp.exp(-cum)).astype(BF16)
    kd = (k * jnp.exp(last - cum)).astype(BF16)
    return qe, ke, kd, cum, mask


def _gla_intra(qe, ke, v, mask):
    sc = lax.dot_general(qe, ke, _NT, preferred_element_type=F32)
    return _dot(jnp.where(mask, sc, 0.0).astype(BF16), v)


def _gla_finish(o, r, g):
    rf = r.astype(F32)
    return _rms(o, g) * (rf * jax.nn.sigmoid(rf))


GLA_SAFE_LOG_DECAY = -60.0


def _gla_token_step(t, rows, q, k, v, a_all, st_ref, oacc_ref):
    m = rows == t
    a = jnp.sum(jnp.where(m, a_all, 0.0), axis=0, keepdims=True)
    kt = jnp.where(m, k, 0.0).astype(BF16)
    qt = jnp.where(m, q, 0.0).astype(BF16)
    vt = jnp.where(m, v, jnp.zeros_like(v))
    st = st_ref[...] * a + lax.dot_general(vt, kt, _TN, preferred_element_type=F32)
    st_ref[...] = st
    oacc_ref[...] += lax.dot_general(qt, st.astype(BF16), _NT, preferred_element_type=F32)


def _gla_prompt_kernel(q_ref, k_ref, v_ref, r_ref, la_ref, g_ref, o_ref, sfin_ref, st_ref, oacc_ref,
                       *, n_chunks):
    j = pl.program_id(1)

    @pl.when(j == 0)
    def _():
        st_ref[...] = jnp.zeros_like(st_ref)

    c = GLA_CHUNK
    qe, ke, kd, cum, mask = _gla_pre(q_ref[...].astype(F32), k_ref[...].astype(F32), la_ref[...], c)
    g = g_ref[...]
    safe = jnp.min(cum) >= GLA_SAFE_LOG_DECAY

    @pl.when(safe)
    def _():
        for h in range(GLA_HEADS):
            kc = slice(h * GLA_DK, (h + 1) * GLA_DK)
            vc = slice(h * GLA_DV, (h + 1) * GLA_DV)
            v = v_ref[:, vc]
            qe_h, kd_h = qe[:, kc], kd[:, kc]
            o = _gla_intra(qe_h, ke[:, kc], v, mask)
            st = st_ref[h]
            inter = []
            for ci in range(n_chunks):
                rows = slice(ci * c, (ci + 1) * c)
                inter.append(lax.dot_general(qe_h[rows], st.astype(BF16), _NT, preferred_element_type=F32))
                dec = jnp.exp(cum[(ci + 1) * c - 1:(ci + 1) * c, kc])
                st = st * dec + lax.dot_general(v[rows], kd_h[rows], _TN, preferred_element_type=F32)
            st_ref[h] = st
            o = o + jnp.concatenate(inter, axis=0)
            o_ref[:, vc] = _gla_finish(o, r_ref[:, vc], g).astype(o_ref.dtype)

    @pl.when(jnp.logical_not(safe))
    def _():
        n_rows = n_chunks * c
        rows = lax.broadcasted_iota(jnp.int32, (n_rows, 1), 0)
        for h in range(GLA_HEADS):
            kc = slice(h * GLA_DK, (h + 1) * GLA_DK)
            vc = slice(h * GLA_DV, (h + 1) * GLA_DV)
            q, k, v = q_ref[:, kc].astype(F32), k_ref[:, kc].astype(F32), v_ref[:, vc]
            a_all = jnp.exp(la_ref[:, kc])
            oacc_ref[...] = jnp.zeros_like(oacc_ref)

            def step(t, carry):
                _gla_token_step(t, rows, q, k, v, a_all, st_ref.at[h], oacc_ref)
                return carry

            lax.fori_loop(0, n_rows, step, 0)
            o_ref[:, vc] = _gla_finish(oacc_ref[...], r_ref[:, vc], g).astype(o_ref.dtype)

    @pl.when(j == pl.num_programs(1) - 1)
    def _():
        for h in range(GLA_HEADS):
            sfin_ref[0, h] = st_ref[h].T


def _gla_prompt(q, k, v, r, la, g, batch, seq):
    rb = 4 * GLA_CHUNK
    nj = seq // rb
    row = lambda w: pl.BlockSpec((rb, w), lambda b, j: (b * nj + j, 0))
    return pl.pallas_call(
        functools.partial(_gla_prompt_kernel, n_chunks=rb // GLA_CHUNK),
        grid=(batch, nj),
        in_specs=[row(QK_W), row(QK_W), row(D_GLA), row(D_GLA), row(QK_W), _const_spec((1, GLA_DV))],
        out_specs=[row(D_GLA),
                   pl.BlockSpec((1, GLA_HEADS, GLA_DK, GLA_DV), lambda b, j: (b, 0, 0, 0))],
        out_shape=[jax.ShapeDtypeStruct((batch * seq, D_GLA), BF16),
                   jax.ShapeDtypeStruct((batch, GLA_HEADS, GLA_DK, GLA_DV), F32)],
        scratch_shapes=[pltpu.VMEM((GLA_HEADS, GLA_DV, GLA_DK), F32), pltpu.VMEM((rb, GLA_DV), F32)],
        compiler_params=pltpu.CompilerParams(dimension_semantics=("arbitrary", "arbitrary"),
                                             vmem_limit_bytes=VMEM_LIMIT),
        name="gla_prompt",
    )(q, k, v, r, la, g)


def _gla_sample_kernel(q_ref, k_ref, v_ref, r_ref, la_ref, g_ref, s0_ref, o_ref, sfin_ref, st_ref, oacc_ref,
                       *, n_seq, seq):
    qe, ke, kd, cum, mask = _gla_pre(q_ref[...].astype(F32), k_ref[...].astype(F32), la_ref[...], seq)
    g = g_ref[...]
    safe = jnp.min(cum) >= GLA_SAFE_LOG_DECAY

    @pl.when(safe)
    def _():
        lasts = jnp.concatenate([cum[(s + 1) * seq - 1:(s + 1) * seq, :] for s in range(n_seq)]
                                + [jnp.zeros((GLA_DK - n_seq, cum.shape[1]), F32)], axis=0)
        pair = 2 * seq
        upper = lax.broadcasted_iota(jnp.int32, (pair, GLA_DK), 0) < seq
        for h in range(GLA_HEADS):
            kc = slice(h * GLA_DK, (h + 1) * GLA_DK)
            vc = slice(h * GLA_DV, (h + 1) * GLA_DV)
            v = v_ref[:, vc]
            qe_h, kd_h = qe[:, kc], kd[:, kc]
            o = _gla_intra(qe_h, ke[:, kc], v, mask)
            dec_t = jnp.exp(lasts[:, kc].T)
            inter = []
            for p in range(n_seq // 2):
                rows = slice(p * pair, (p + 1) * pair)
                qe_p, kd_p, v_p = qe_h[rows], kd_h[rows], v[rows]
                for half in range(2):
                    s = 2 * p + half
                    s0 = s0_ref[s, h]
                    o_s = _dot(qe_p, s0.astype(BF16))
                    inter.append(o_s[half * seq:(half + 1) * seq])
                    kd_s = jnp.where(upper if half == 0 else ~upper, kd_p, jnp.zeros_like(kd_p))
                    dec = jnp.broadcast_to(dec_t[:, s:s + 1], (GLA_DK, GLA_DV))
                    sfin_ref[s, h] = s0 * dec + lax.dot_general(kd_s, v_p, _TN, preferred_element_type=F32)
            o = o + jnp.concatenate(inter, axis=0)
            o_ref[:, vc] = _gla_finish(o, r_ref[:, vc], g).astype(o_ref.dtype)

    @pl.when(jnp.logical_not(safe))
    def _():
        n_rows = n_seq * seq
        rows = lax.broadcasted_iota(jnp.int32, (n_rows, 1), 0)
        for h in range(GLA_HEADS):
            kc = slice(h * GLA_DK, (h + 1) * GLA_DK)
            vc = slice(h * GLA_DV, (h + 1) * GLA_DV)
            q, k, v = q_ref[:, kc].astype(F32), k_ref[:, kc].astype(F32), v_ref[:, vc]
            a_all = jnp.exp(la_ref[:, kc])
            oacc_ref[...] = jnp.zeros_like(oacc_ref)

            def step(t, carry):
                s = t // seq

                @pl.when(t % seq == 0)
                def _():
                    st_ref[...] = s0_ref[s, h].T

                _gla_token_step(t, rows, q, k, v, a_all, st_ref, oacc_ref)

                @pl.when(t % seq == seq - 1)
                def _():
                    sfin_ref[s, h] = st_ref[...].T

                return carry

            lax.fori_loop(0, n_rows, step, 0)
            o_ref[:, vc] = _gla_finish(oacc_ref[...], r_ref[:, vc], g).astype(o_ref.dtype)


def _gla_sample(q, k, v, r, la, g, s0, batch, seq):
    ns = 16
    rb = ns * seq
    row = lambda w: pl.BlockSpec((rb, w), lambda i: (i, 0))
    st = pl.BlockSpec((ns, GLA_HEADS, GLA_DK, GLA_DV), lambda i: (i, 0, 0, 0))
    return pl.pallas_call(
        functools.partial(_gla_sample_kernel, n_seq=ns, seq=seq),
        grid=(batch // ns,),
        in_specs=[row(QK_W), row(QK_W), row(D_GLA), row(D_GLA), row(QK_W), _const_spec((1, GLA_DV)), st],
        out_specs=[row(D_GLA), st],
        out_shape=[jax.ShapeDtypeStruct((batch * seq, D_GLA), BF16),
                   jax.ShapeDtypeStruct((batch, GLA_HEADS, GLA_DK, GLA_DV), F32)],
        scratch_shapes=[pltpu.VMEM((GLA_DV, GLA_DK), F32), pltpu.VMEM((rb, GLA_DV), F32)],
        compiler_params=pltpu.CompilerParams(dimension_semantics=("arbitrary",), vmem_limit_bytes=VMEM_LIMIT),
        name="gla_sample",
    )(q, k, v, r, la, g, s0)


def _s5_kernel(u_ref, wbu_ref, wc_ref, are_ref, aim_ref, d_ref, h0r_ref, h0i_ref,
               y_ref, sre_ref, sim_ref, bu_ref, xs_ref, car_ref, *, nb, tc, has_state):
    j = pl.program_id(2)

    @pl.when(j == 0)
    def _():
        if has_state:
            car_ref[0] = h0r_ref[...]
            car_ref[1] = h0i_ref[...]
        else:
            car_ref[...] = jnp.zeros_like(car_ref)

    u2 = u_ref[...].reshape(nb * tc, LANES)
    ub = u2.astype(BF16)
    nl = S5_SL // LANES
    for l in range(2 * nl):
        bu_ref[l] = _dot(ub, wbu_ref[0, :, l * LANES:(l + 1) * LANES])
    a_r = [jnp.broadcast_to(are_ref[0, :, l * LANES:(l + 1) * LANES], (nb, LANES)) for l in range(nl)]
    a_i = [jnp.broadcast_to(aim_ref[0, :, l * LANES:(l + 1) * LANES], (nb, LANES)) for l in range(nl)]

    def step(t, carry):
        rows = pl.ds(t, nb, stride=tc)
        out = []
        for l in range(nl):
            xr, xi = carry[2 * l], carry[2 * l + 1]
            nr = a_r[l] * xr - a_i[l] * xi + bu_ref[l, rows, :]
            ni = a_r[l] * xi + a_i[l] * xr + bu_ref[nl + l, rows, :]
            xs_ref[l, rows, :] = nr
            xs_ref[nl + l, rows, :] = ni
            out += [nr, ni]
        return tuple(out)

    init = []
    for l in range(nl):
        init += [car_ref[0, :, l * LANES:(l + 1) * LANES], car_ref[1, :, l * LANES:(l + 1) * LANES]]
    fin = lax.fori_loop(0, tc, step, tuple(init), unroll=8)
    xr = jnp.concatenate([fin[2 * l] for l in range(nl)], axis=1)
    xi = jnp.concatenate([fin[2 * l + 1] for l in range(nl)], axis=1)
    car_ref[0] = xr
    car_ref[1] = xi
    y = d_ref[0] * u2
    for l in range(2 * nl):
        y = y + _dot(xs_ref[l].astype(BF16), wc_ref[0, l * LANES:(l + 1) * LANES, :])
    y_ref[...] = y.reshape(nb, tc, LANES)

    @pl.when(j == pl.num_programs(2) - 1)
    def _():
        sre_ref[...] = xr
        sim_ref[...] = xi


def _s5(u3d, wbu, wc, a_re, a_im, dsk, h0r, h0i, nb, tc, has_state):
    batch, seq, _ = u3d.shape
    grid = (S5_NGB, batch // nb, seq // tc)
    st = pl.BlockSpec((nb, S5_SL), lambda g, b, j: (b, g))
    par = lambda w: pl.BlockSpec((1, 1, w), lambda g, b, j: (g, 0, 0))
    ublk = pl.BlockSpec((nb, tc, LANES), lambda g, b, j: (b, j, g))
    return pl.pallas_call(
        functools.partial(_s5_kernel, nb=nb, tc=tc, has_state=has_state),
        grid=grid,
        in_specs=[ublk,
                  pl.BlockSpec((1, LANES, 2 * S5_SL), lambda g, b, j: (g, 0, 0)),
                  pl.BlockSpec((1, 2 * S5_SL, LANES), lambda g, b, j: (g, 0, 0)),
                  par(S5_SL), par(S5_SL), par(LANES), st, st],
        out_specs=[ublk, st, st],
        out_shape=[jax.ShapeDtypeStruct(u3d.shape, F32),
                   jax.ShapeDtypeStruct((batch, S5_GROUPS * S5_STATE), F32),
                   jax.ShapeDtypeStruct((batch, S5_GROUPS * S5_STATE), F32)],
        scratch_shapes=[pltpu.VMEM((2 * S5_SL // LANES, nb * tc, LANES), F32),
                        pltpu.VMEM((2 * S5_SL // LANES, nb * tc, LANES), F32),
                        pltpu.VMEM((2, nb, S5_SL), F32)],
        compiler_params=pltpu.CompilerParams(dimension_semantics=("arbitrary",) * 3,
                                             vmem_limit_bytes=VMEM_LIMIT),
        name="s5_state" if has_state else "s5_zero",
    )(u3d, wbu, wc, a_re, a_im, dsk, h0r, h0i)


S5_J = 2
S5_HL = S5_SL // S5_J


def _s5_prompt_kernel(u_ref, wbu_ref, wc_ref, are_ref, aim_ref, d_ref, y_ref, sre_ref, sim_ref,
                      u2_ref, lhs_ref, bu_ref, xs_ref, y2_ref, yo_ref, car_ref, *, nb, tc):
    g = pl.program_id(0)
    j = pl.program_id(1)
    rows = nb * S5_J

    @pl.when((g == 0) & (j == 0))
    def _():
        lhs_ref[...] = jnp.zeros_like(lhs_ref)

    @pl.when(j == 0)
    def _():
        car_ref[...] = jnp.zeros_like(car_ref)

    u2 = u_ref[...].reshape(nb * tc, LANES)
    u2_ref[...] = u2

    def build(t, c):
        u4 = u2_ref[pl.ds(t, nb, stride=tc), :]
        for jj in range(S5_J):
            lhs_ref[t, jj * nb:(jj + 1) * nb, jj * LANES:(jj + 1) * LANES] = u4
        return c

    lax.fori_loop(0, tc, build, 0, unroll=8)
    lhs = lhs_ref[...].reshape(tc * rows, S5_J * LANES).astype(BF16)
    bu_ref[...] = _dot(lhs, wbu_ref[0]).reshape(tc, rows, 2 * S5_HL)
    ar = are_ref[0]
    ai = aim_ref[0]

    def step(t, carry):
        xr, xi = carry
        tile = bu_ref[t]
        nr = ar * xr - ai * xi + tile[:, 0:S5_HL]
        ni = ar * xi + ai * xr + tile[:, S5_HL:2 * S5_HL]
        xs_ref[t] = jnp.concatenate([nr, ni], axis=1)
        return nr, ni

    xr, xi = lax.fori_loop(0, tc, step, (car_ref[0], car_ref[1]), unroll=8)
    car_ref[0] = xr
    car_ref[1] = xi
    xs = xs_ref[...].reshape(tc * rows, 2 * S5_HL).astype(BF16)
    y2_ref[...] = _dot(xs, wc_ref[0]).reshape(tc, rows, S5_J * LANES)
    first_half = lax.broadcasted_iota(jnp.int32, (rows, LANES), 0) < nb

    def unperm(t, c):
        t2 = y2_ref[t]
        part = jnp.where(first_half, t2[:, 0:LANES], t2[:, LANES:2 * LANES])
        tot = part + pltpu.roll(part, nb, axis=0)
        yo_ref[pl.ds(t, nb, stride=tc), :] = tot[0:nb]
        return c

    lax.fori_loop(0, tc, unperm, 0, unroll=8)
    y_ref[...] = (yo_ref[...] + d_ref[0] * u2).reshape(nb, tc, LANES)

    @pl.when(j == pl.num_programs(1) - 1)
    def _():
        sre_ref[...] = jnp.concatenate([xr[jj * nb:(jj + 1) * nb] for jj in range(S5_J)], axis=1)
        sim_ref[...] = jnp.concatenate([xi[jj * nb:(jj + 1) * nb] for jj in range(S5_J)], axis=1)


def _s5_prompt(u3d, wbu2, wc2, a_re2, a_im2, dsk, tc):
    nb, seq, _ = u3d.shape
    rows = nb * S5_J
    assert rows == 8
    st = pl.BlockSpec((nb, S5_SL), lambda g, j: (0, g))
    gblk = lambda s: pl.BlockSpec((1,) + s, lambda g, j: (g, 0, 0))
    ublk = pl.BlockSpec((nb, tc, LANES), lambda g, j: (0, j, g))
    return pl.pallas_call(
        functools.partial(_s5_prompt_kernel, nb=nb, tc=tc),
        grid=(S5_NGB, seq // tc),
        in_specs=[ublk, gblk((S5_J * LANES, 2 * S5_HL)), gblk((2 * S5_HL, S5_J * LANES)),
                  gblk((rows, S5_HL)), gblk((rows, S5_HL)), gblk((1, LANES))],
        out_specs=[ublk, st, st],
        out_shape=[jax.ShapeDtypeStruct(u3d.shape, F32),
                   jax.ShapeDtypeStruct((nb, S5_GROUPS * S5_STATE), F32),
                   jax.ShapeDtypeStruct((nb, S5_GROUPS * S5_STATE), F32)],
        scratch_shapes=[pltpu.VMEM((nb * tc, LANES), F32),
                        pltpu.VMEM((tc, rows, S5_J * LANES), F32),
                        pltpu.VMEM((tc, rows, 2 * S5_HL), F32),
                        pltpu.VMEM((tc, rows, 2 * S5_HL), F32),
                        pltpu.VMEM((tc, rows, S5_J * LANES), F32),
                        pltpu.VMEM((nb * tc, LANES), F32),
                        pltpu.VMEM((2, rows, S5_HL), F32)],
        compiler_params=pltpu.CompilerParams(dimension_semantics=("arbitrary",) * 2,
                                             vmem_limit_bytes=VMEM_LIMIT),
        name="s5_prompt",
    )(u3d, wbu2, wc2, a_re2, a_im2, dsk)


def _s5_prompt_params(wbu, wc, a_re, a_im, nb):
    h = S5_HL
    top = jnp.concatenate([wbu[:, :, 0:h], wbu[:, :, S5_SL:S5_SL + h]], axis=2)
    bot = jnp.concatenate([wbu[:, :, h:2 * h], wbu[:, :, S5_SL + h:S5_SL + 2 * h]], axis=2)
    wbu2 = jnp.concatenate([top, bot], axis=1)
    wc_j = [jnp.concatenate([wc[:, jj * h:(jj + 1) * h, :], wc[:, S5_SL + jj * h:S5_SL + (jj + 1) * h, :]],
                            axis=1) for jj in range(S5_J)]
    wc2 = jnp.concatenate(wc_j, axis=2)
    tile = lambda a: jnp.repeat(a.reshape(S5_NGB, S5_J, h), nb, axis=1)
    return wbu2, wc2, tile(a_re), tile(a_im)


def _mix_out_kernel(xp_ref, xs_ref, ogp_ref, ogs_ref, ysp_ref, yss_ref, glu_w_ref, glu_b_ref, s5n_ref, wo_ref,
                    nffn_ref, wr_ref, h1_ref, hn_ref, rt_ref, rtt_ref, cnt_ref, *, n_prompt_tiles):
    is_p = pl.program_id(0) < n_prompt_tiles
    x = jnp.where(is_p, xp_ref[...], xs_ref[...])
    og = jnp.where(is_p, ogp_ref[...], ogs_ref[...])
    y = jax.nn.gelu(jnp.where(is_p, ysp_ref[...], yss_ref[...]))
    y = y * jax.nn.sigmoid(_dot(y.astype(BF16), glu_w_ref[...]) + glu_b_ref[...])
    yn = _rms(y, s5n_ref[...]).astype(BF16)
    mix = _dot(og, wo_ref[0:D_GLA, :]) + _dot(yn, wo_ref[D_GLA:D_GLA + D_S5, :])
    h1 = x + mix
    h1_ref[...] = h1
    hn = _rms(h1, nffn_ref[...])
    hn_ref[...] = _pack_bf16_pair(hn[:, 0:HALF], hn[:, HALF:D_MODEL])
    hn_hi = hn.astype(BF16)
    hn_lo = (hn - hn_hi.astype(F32)).astype(BF16)
    logits = _dot(hn_hi, wr_ref[0]) + _dot(hn_hi, wr_ref[1]) + _dot(hn_lo, wr_ref[0])
    rt = _route(logits)

    @pl.when(pl.program_id(0) == 0)
    def _():
        cnt_ref[...] = jnp.zeros_like(cnt_ref)

    tm = rt.shape[0]
    lane = lax.broadcasted_iota(jnp.int32, rt.shape, 1).astype(F32)
    oh0 = lane == rt[:, 0:1]
    oh1 = lane == rt[:, 1:2]
    both = jnp.where(oh0 | oh1, 1.0, 0.0)
    ri = lax.broadcasted_iota(jnp.int32, (tm, tm), 0)
    ci = lax.broadcasted_iota(jnp.int32, (tm, tm), 1)
    before = _dot(jnp.where(ri > ci, 1.0, 0.0).astype(BF16), both.astype(BF16)) + cnt_ref[...]
    rank0 = jnp.sum(jnp.where(oh0, before, 0.0), axis=-1, keepdims=True)
    rank1 = jnp.sum(jnp.where(oh1, before, 0.0), axis=-1, keepdims=True)
    cnt_ref[...] += jnp.sum(both, axis=0, keepdims=True)
    rt = jnp.where(lane == 4.0, rank0, jnp.where(lane == 5.0, rank1, rt))
    rt_ref[...] = rt
    rtt_ref[...] = rt.T[0:8, :]


def _route(logits):
    col = lax.broadcasted_iota(jnp.int32, logits.shape, 1)
    colf = col.astype(F32)
    neg = -jnp.inf

    def first_argmax(vals):
        m = jnp.max(vals, axis=-1, keepdims=True)
        idx = jnp.min(jnp.where(vals == m, colf, float(LANES)), axis=-1, keepdims=True)
        return m, idx

    lg = jnp.where(col < N_EGROUPS, logits, neg)
    gmax, gsel = first_argmax(lg)
    p_g = 1.0 / jnp.sum(jnp.exp(lg - gmax), axis=-1, keepdims=True)
    ecol = col - N_EGROUPS
    egrp = (ecol >> 3).astype(F32)
    in_group = (ecol >= 0) & (ecol < N_EXPERTS) & (egrp == gsel)
    le = jnp.where(in_group, logits, neg)
    m1, i1 = first_argmax(le)
    le2 = jnp.where(colf == i1, neg, le)
    m2, i2 = first_argmax(le2)
    e2 = jnp.exp(m2 - m1)
    den = 1.0 + e2
    w1 = p_g * (1.0 / den)
    w2 = p_g * (e2 / den)
    e1f = i1 - float(N_EGROUPS)
    e2f = i2 - float(N_EGROUPS)
    out = jnp.where(col == 0, e1f, jnp.where(col == 1, e2f, jnp.where(col == 2, w1, jnp.where(col == 3, w2, 0.0))))
    return out


def _mix_out(xp, xs, ogp, ogs, ysp, yss, glu_w, glu_b, s5n, wo, nffn, wr):
    tm = TOK_TILE
    npt, nst = xp.shape[0] // tm, xs.shape[0] // tm
    t = (npt + nst) * tm
    row = lambda w: pl.BlockSpec((tm, w), lambda i: (i, 0))
    prow = lambda w: pl.BlockSpec((tm, w), lambda i: (jnp.minimum(i, npt - 1), 0))
    srow = lambda w: pl.BlockSpec((tm, w), lambda i: (jnp.maximum(i - npt, 0), 0))
    return pl.pallas_call(
        functools.partial(_mix_out_kernel, n_prompt_tiles=npt),
        grid=(npt + nst,),
        in_specs=[prow(D_MODEL), srow(D_MODEL), prow(D_GLA), srow(D_GLA), prow(D_S5), srow(D_S5),
                  _const_spec(glu_w.shape), _const_spec((1, D_S5)), _const_spec((1, D_S5)),
                  _const_spec(wo.shape), _const_spec((1, D_MODEL)), _const_spec(wr.shape)],
        out_specs=[row(D_MODEL), row(HALF), row(LANES), pl.BlockSpec((8, tm), lambda i: (0, i)),
                   pl.BlockSpec((1, LANES), lambda i: (0, 0))],
        out_shape=[jax.ShapeDtypeStruct((t, D_MODEL), F32), jax.ShapeDtypeStruct((t, HALF), jnp.uint32),
                   jax.ShapeDtypeStruct((t, LANES), F32), jax.ShapeDtypeStruct((8, t), F32),
                   jax.ShapeDtypeStruct((1, LANES), F32)],
        compiler_params=pltpu.CompilerParams(dimension_semantics=("arbitrary",), vmem_limit_bytes=VMEM_LIMIT),
        name="mix_out",
    )(xp, xs, ogp, ogs, ysp, yss, glu_w, glu_b, s5n, wo, nffn, wr)


DMA_UNROLL = 8


MOE_SB = 3
HALF = D_MODEL // 2


def _pack_bf16_pair(lo, hi):
    def bits(x):
        b = pltpu.bitcast(x, jnp.uint32)
        return (b + jnp.uint32(0x7FFF) + ((b >> 16) & jnp.uint32(1))) >> 16
    return bits(lo) | (bits(hi) << 16)


def _unpack_bf16_pair(u):
    return (pltpu.bitcast(u << 16, F32), pltpu.bitcast(u & jnp.uint32(0xFFFF0000), F32))


def _moe_kernel(se_ref, sm_ref, sn_ref, sbase_ref, sa_ref, hn_hbm, wg_ref, wu_ref, wd_ref, out_hbm,
                xb_ref, yb_ref, wgb_ref, wub_ref, wdb_ref, gsem, ssem):
    b = pl.program_id(0)
    nb = pl.num_programs(0)
    n = sn_ref[b]
    slot = lax.rem(b, 2)

    def round_up(cnt):
        return (cnt + DMA_UNROLL - 1) // DMA_UNROLL * DMA_UNROLL

    def issue_gather(blk, sl):
        base = sbase_ref[blk]

        def grp(g, c):
            for j in range(DMA_UNROLL):
                i = g * DMA_UNROLL + j
                tok = sa_ref[base + i]
                pltpu.make_async_copy(hn_hbm.at[pl.ds(tok, 1), :], xb_ref.at[sl, pl.ds(i, 1), :],
                                      gsem.at[sl]).start()
            return c

        lax.fori_loop(0, round_up(sn_ref[blk]) // DMA_UNROLL, grp, 0)

    def wait_gather(cnt, sl):
        rows = pl.ds(0, pl.multiple_of(round_up(cnt), DMA_UNROLL))
        pltpu.make_async_copy(hn_hbm.at[rows, :], xb_ref.at[sl, rows, :], gsem.at[sl]).wait()

    m_cur = sm_ref[b]
    m_prev = sm_ref[jnp.maximum(b - 1, 0)]
    has_prev = (b > 0) & (m_prev > 0)

    def out_copy(rows):
        dst = pl.ds(pl.multiple_of(sbase_ref[b], MOE_BLK), rows)
        return pltpu.make_async_copy(yb_ref.at[pl.ds(0, rows), :], out_hbm.at[dst, :], ssem.at[0])

    def wait_prev_out():
        rows = pl.ds(0, pl.multiple_of(m_prev * MOE_BLK, MOE_BLK))
        pltpu.make_async_copy(yb_ref.at[rows, :], out_hbm.at[rows, :], ssem.at[0]).wait()

    def write_out(rows):
        out_copy(rows).start()

        @pl.when(b == nb - 1)
        def _():
            out_copy(rows).wait()

    @pl.when(b == 0)
    def _():
        xb_ref[...] = jnp.zeros_like(xb_ref)
        issue_gather(0, 0)

    @pl.when(b + 1 < nb)
    def _():
        issue_gather(b + 1, 1 - slot)

    def compute(rows):
        x_lo, x_hi = _unpack_bf16_pair(xb_ref[slot, 0:rows, :])
        x_lo, x_hi = x_lo.astype(BF16), x_hi.astype(BF16)
        gate = _dot(x_lo, wgb_ref[0:HALF, :]) + _dot(x_hi, wgb_ref[HALF:D_MODEL, :])
        up = _dot(x_lo, wub_ref[0:HALF, :]) + _dot(x_hi, wub_ref[HALF:D_MODEL, :])
        hid = (gate * jax.nn.sigmoid(gate) * up).astype(BF16)

        @pl.when(has_prev)
        def _():
            wait_prev_out()

        yb_ref[0:rows, :] = _pack_bf16_pair(_dot(hid, wdb_ref[:, 0:HALF]), _dot(hid, wdb_ref[:, HALF:D_MODEL]))
        write_out(rows)

    @pl.when(n > 0)
    def _():
        prev_e = se_ref[jnp.maximum(b - 1, 0)]

        @pl.when((b == 0) | (prev_e != se_ref[b]))
        def _():
            wgb_ref[...] = wg_ref[...].astype(BF16)
            wub_ref[...] = wu_ref[...].astype(BF16)
            wdb_ref[...] = wd_ref[...].astype(BF16)

        wait_gather(n, slot)
        for m in range(1, MOE_SB + 1):
            @pl.when(m_cur == m)
            def _():
                compute(m * MOE_BLK)

    @pl.when((n == 0) & (m_cur > 0))
    def _():
        @pl.when(has_prev)
        def _():
            wait_prev_out()

        yb_ref[...] = jnp.zeros_like(yb_ref)
        for m in range(1, MOE_SB + 1):
            @pl.when(m_cur == m)
            def _():
                write_out(m * MOE_BLK)

    @pl.when((m_cur == 0) & has_prev)
    def _():
        wait_prev_out()


def _moe(hn_pk, plan, w_gate, w_up, w_down, n_out_rows):
    sb_e, sb_m, sb_n, sb_base, slot_a = plan
    sb_rows = MOE_SB * MOE_BLK
    wspec = lambda s: pl.BlockSpec((None,) + s, lambda b, se, *_: (se[b], 0, 0))
    grid_spec = pltpu.PrefetchScalarGridSpec(
        num_scalar_prefetch=5,
        grid=(sb_e.shape[0],),
        in_specs=[pl.BlockSpec(memory_space=pl.ANY),
                  wspec((D_MODEL, D_EXPERT)), wspec((D_MODEL, D_EXPERT)), wspec((D_EXPERT, D_MODEL))],
        out_specs=pl.BlockSpec(memory_space=pl.ANY),
        scratch_shapes=[pltpu.VMEM((2, sb_rows, HALF), jnp.uint32), pltpu.VMEM((sb_rows, HALF), jnp.uint32),
                        pltpu.VMEM((D_MODEL, D_EXPERT), BF16), pltpu.VMEM((D_MODEL, D_EXPERT), BF16),
                        pltpu.VMEM((D_EXPERT, D_MODEL), BF16), pltpu.SemaphoreType.DMA((2,)),
                        pltpu.SemaphoreType.DMA((1,))],
    )
    return pl.pallas_call(
        _moe_kernel,
        grid_spec=grid_spec,
        out_shape=jax.ShapeDtypeStruct((n_out_rows, HALF), jnp.uint32),
        compiler_params=pltpu.CompilerParams(dimension_semantics=("arbitrary",),
                                             vmem_limit_bytes=MOE_VMEM_LIMIT),
        name="moe_experts",
    )(sb_e, sb_m, sb_n, sb_base, slot_a, hn_pk, w_gate, w_up, w_down)


def _moe_plan(rtt, cnt):
    t_all = rtt.shape[1]
    n_assign = 2 * t_all
    e_flat = rtt[0:2].astype(jnp.int32).reshape(-1)
    rank = rtt[4:6].astype(jnp.int32).reshape(-1)
    counts = cnt[0, 0:N_EXPERTS].astype(jnp.int32)
    padded = (counts + MOE_BLK - 1) // MOE_BLK * MOE_BLK
    pad_end = jnp.cumsum(padded)
    pad_start = pad_end - padded
    dest = pad_start[e_flat] + rank
    nblk = -(-(n_assign + N_EXPERTS * (MOE_BLK - 1)) // MOE_BLK)
    n_slots = nblk * MOE_BLK
    tok = jnp.arange(t_all, dtype=jnp.int32)
    slot_a = jnp.zeros((n_slots,), jnp.int32).at[dest].set(jnp.concatenate([tok, tok]))
    k_e = padded // MOE_BLK
    sbc = (k_e + MOE_SB - 1) // MOE_SB
    sb_end = jnp.cumsum(sbc)
    sb_start = sb_end - sbc
    n_sb = (nblk + (MOE_SB - 1) * N_EXPERTS) // MOE_SB
    s = jnp.arange(n_sb, dtype=jnp.int32)
    sb_e = jnp.minimum(jnp.sum((s[:, None] >= sb_end[None, :]).astype(jnp.int32), axis=1), N_EXPERTS - 1)
    j = s - sb_start[sb_e]
    real = s < sb_end[-1]
    tail_blk = pad_end[-1] // MOE_BLK + MOE_SB * (s - sb_end[-1])
    sb_m = jnp.where(real, jnp.clip(k_e[sb_e] - MOE_SB * j, 0, MOE_SB), jnp.clip(nblk - tail_blk, 0, MOE_SB))
    sb_n = jnp.where(real, jnp.clip(counts[sb_e] - MOE_SB * MOE_BLK * j, 0, MOE_SB * MOE_BLK), 0)
    sb_base = jnp.where(real, pad_start[sb_e] + MOE_SB * MOE_BLK * j, jnp.minimum(tail_blk, nblk - 1) * MOE_BLK)
    i32 = lambda a: a.astype(jnp.int32)
    return (i32(sb_e), i32(sb_m), i32(sb_n), i32(sb_base), slot_a), i32(dest), n_slots


def _ple_out_kernel(dest_ref, h1_ref, rt_ref, pp_ref, ps_ref, nple_ref, wpg_ref, wp_ref, nfin_ref, ys_hbm,
                    op_ref, os_ref, yb_ref, sem, *, n_prompt_tiles):
    i = pl.program_id(0)
    nt = pl.num_programs(0)
    tm = h1_ref.shape[0]
    t_all = nt * tm
    slot = lax.rem(i, 2)

    def issue(tile, sl):
        base = tile * tm

        def grp(g, c):
            for j in range(DMA_UNROLL):
                row = g * (DMA_UNROLL // 2) + j // 2
                d = dest_ref[(j % 2) * t_all + base + row]
                pltpu.make_async_copy(ys_hbm.at[pl.ds(d, 1), :], yb_ref.at[sl, j % 2, pl.ds(row, 1), :],
                                      sem.at[sl]).start()
            return c

        lax.fori_loop(0, 2 * tm // DMA_UNROLL, grp, 0)

    @pl.when(i == 0)
    def _():
        issue(0, 0)

    def wait_rows(sl):
        for k in range(2):
            pltpu.make_async_copy(ys_hbm.at[pl.ds(0, tm), :], yb_ref.at[sl, k], sem.at[sl]).wait()

    wait_rows(slot)
    is_p = i < n_prompt_tiles
    rt = rt_ref[...]
    w0, w1 = rt[:, 2:3], rt[:, 3:4]
    lo0, hi0 = _unpack_bf16_pair(yb_ref[slot, 0])
    lo1, hi1 = _unpack_bf16_pair(yb_ref[slot, 1])
    h2 = h1_ref[...] + jnp.concatenate([w0 * lo0 + w1 * lo1, w0 * hi0 + w1 * hi1], axis=1)
    hb = _rms(h2, nple_ref[...]).astype(BF16)
    nbase = jnp.minimum(i + 1, nt - 1) * tm
    n_chunk = 8
    cw, ca = D_MODEL // n_chunk, 2 * tm // n_chunk
    gates = []
    for c in range(n_chunk):
        gates.append(jax.nn.sigmoid(_dot(hb, wpg_ref[:, c * cw:(c + 1) * cw])))
        for a in range(c * ca, (c + 1) * ca):
            d = dest_ref[(a % 2) * t_all + nbase + a // 2]
            pltpu.make_async_copy(ys_hbm.at[pl.ds(d, 1), :], yb_ref.at[1 - slot, a % 2, pl.ds(a // 2, 1), :],
                                  sem.at[1 - slot]).start()
    gate = jnp.concatenate(gates, axis=1)

    @pl.when(i == nt - 1)
    def _():
        wait_rows(1 - slot)

    p = jnp.where(is_p, pp_ref[...], ps_ref[...])
    h3 = h2 + _dot(p.astype(BF16), wp_ref[...]) * gate
    y = _rms(h3, nfin_ref[...])

    @pl.when(is_p)
    def _():
        op_ref[...] = y

    @pl.when(jnp.logical_not(is_p))
    def _():
        os_ref[...] = y


def _ple_out(dest, h1, rt, pp, ps, nple, wpg, wp, nfin, y_slots):
    tm = TOK_TILE
    npt, nst = pp.shape[0] // tm, ps.shape[0] // tm
    row = lambda w: pl.BlockSpec((tm, w), lambda i, d: (i, 0))
    prow = lambda w: pl.BlockSpec((tm, w), lambda i, d: (jnp.minimum(i, npt - 1), 0))
    srow = lambda w: pl.BlockSpec((tm, w), lambda i, d: (jnp.maximum(i - npt, 0), 0))
    const = lambda shape: pl.BlockSpec(shape, lambda i, d: (0,) * len(shape), pipeline_mode=pl.Buffered(1))
    grid_spec = pltpu.PrefetchScalarGridSpec(
        num_scalar_prefetch=1,
        grid=(npt + nst,),
        in_specs=[row(D_MODEL), row(LANES), prow(D_PLE), srow(D_PLE), const((1, D_MODEL)), const(wpg.shape),
                  const(wp.shape), const((1, D_MODEL)), pl.BlockSpec(memory_space=pl.ANY)],
        out_specs=[prow(D_MODEL), srow(D_MODEL)],
        scratch_shapes=[pltpu.VMEM((2, 2, tm, HALF), jnp.uint32), pltpu.SemaphoreType.DMA((2,))],
    )
    return pl.pallas_call(
        functools.partial(_ple_out_kernel, n_prompt_tiles=npt),
        grid_spec=grid_spec,
        out_shape=[jax.ShapeDtypeStruct((pp.shape[0], D_MODEL), F32),
                   jax.ShapeDtypeStruct((ps.shape[0], D_MODEL), F32)],
        compiler_params=pltpu.CompilerParams(dimension_semantics=("arbitrary",), vmem_limit_bytes=VMEM_LIMIT),
        name="ple_out",
    )(dest, h1, rt, pp, ps, nple, wpg, wp, nfin, y_slots)


def _s5_params(lam_re, lam_im, log_dt, b_re, b_im, c_re, c_im, d_skip):
    dt = jnp.exp(log_dt)[:, None]
    mag = jnp.exp(lam_re * dt)
    ab_re = mag * jnp.cos(lam_im * dt)
    ab_im = mag * jnp.sin(lam_im * dt)
    den = lam_re * lam_re + lam_im * lam_im
    nr = ab_re - 1.0
    f_re = (nr * lam_re + ab_im * lam_im) / den
    f_im = (ab_im * lam_re - nr * lam_im) / den
    bb_re = f_re[..., None] * b_re - f_im[..., None] * b_im
    bb_im = f_re[..., None] * b_im + f_im[..., None] * b_re
    eye = jnp.eye(S5_GB, dtype=F32)

    def bu_w(bb):
        bb = bb.reshape(S5_NGB, S5_GB, S5_STATE, S5_GROUP)
        w = jnp.einsum('nlph,lm->nlhmp', bb, eye)
        return w.reshape(S5_NGB, S5_GB * S5_GROUP, S5_SL)

    def c_w(c):
        c = c.reshape(S5_NGB, S5_GB, S5_GROUP, S5_STATE)
        w = jnp.einsum('nlhp,lm->nlpmh', c, eye)
        return w.reshape(S5_NGB, S5_SL, S5_GB * S5_GROUP)

    wbu = jnp.concatenate([bu_w(bb_re), bu_w(bb_im)], axis=2).astype(BF16)
    wc = jnp.concatenate([c_w(c_re), -c_w(c_im)], axis=1).astype(BF16)
    a_re = ab_re.reshape(S5_NGB, 1, S5_SL)
    a_im = ab_im.reshape(S5_NGB, 1, S5_SL)
    dsk = d_skip.reshape(S5_NGB, 1, S5_GB * S5_GROUP)
    return wbu, wc, a_re, a_im, dsk


def kernel(x_prompt, x_sample, p_prompt, p_sample, state_gla, state_s5_re, state_s5_im, norm_mix, w_in, gla_w_gate_up, gla_gate_bias, gla_norm, s5_lam_re, s5_lam_im, s5_log_dt, s5_b_re, s5_b_im, s5_c_re, s5_c_im, s5_d, s5_glu_w, s5_glu_b, s5_norm, w_out, norm_ffn, router_group, router_expert, w_gate, w_up, w_down, norm_ple, w_ple, w_ple_gate, norm_final):
    depth = w_in.shape[0]
    assert depth == 1
    i = 0
    bp, lp, _ = x_prompt.shape
    bs, ls, _ = x_sample.shape
    tp, ts = bp * lp, bs * ls
    t_all = tp + ts

    wm = _w_in_prep(w_in[i])
    wgu = jnp.pad(gla_w_gate_up[i], ((0, LANES - GLA_RANK), (0, 0))).astype(BF16)
    gbias = gla_gate_bias[i].reshape(1, QK_W)
    wbu, wc, a_re, a_im, dsk = _s5_params(s5_lam_re[i], s5_lam_im[i], s5_log_dt[i], s5_b_re[i], s5_b_im[i],
                                          s5_c_re[i], s5_c_im[i], s5_d[i])
    glu_w = s5_glu_w[i].astype(BF16)
    wo = w_out[i].astype(BF16)
    wr32 = jnp.pad(jnp.concatenate([router_group[i], router_expert[i]], axis=1),
                   ((0, 0), (0, LANES - N_EGROUPS - N_EXPERTS)))
    wr_hi = wr32.astype(BF16)
    wr = jnp.stack([wr_hi, (wr32 - wr_hi.astype(F32)).astype(BF16)])
    wpg = w_ple_gate[i].astype(BF16)
    wp = w_ple[i].astype(BF16)
    vec = lambda a: a.reshape(1, -1)

    xp = x_prompt.reshape(tp, D_MODEL)
    xs = x_sample.reshape(ts, D_MODEL)

    qp, kp, vp, rp, lap, up = _in_proj(xp, vec(norm_mix[i]), wm, wgu, gbias, BF16)
    qs, ks, vs, rs, las, us = _in_proj(xs, vec(norm_mix[i]), wm, wgu, gbias, BF16)
    ogp, gla_p = _gla_prompt(qp, kp, vp, rp, lap, vec(gla_norm[i]), bp, lp)
    ogs, gla_s = _gla_sample(qs, ks, vs, rs, las, vec(gla_norm[i]), state_gla[i], bs, ls)
    wbu2, wc2, a_re2, a_im2 = _s5_prompt_params(wbu, wc, a_re, a_im, bp)
    ysp, re_p, im_p = _s5_prompt(up.reshape(bp, lp, D_S5), wbu2, wc2, a_re2, a_im2, dsk, tc=256)
    yss, re_s, im_s = _s5(us.reshape(bs, ls, D_S5), wbu, wc, a_re, a_im, dsk,
                          state_s5_re[i].reshape(bs, -1), state_s5_im[i].reshape(bs, -1),
                          nb=32, tc=ls, has_state=True)

    h1, hn_all, rt_all, rtt, cnt = _mix_out(xp, xs, ogp, ogs, ysp.reshape(tp, D_S5), yss.reshape(ts, D_S5),
                                            glu_w, vec(s5_glu_b[i]), vec(s5_norm[i]), wo, vec(norm_ffn[i]), wr)

    plan, dest, n_slots = _moe_plan(rtt, cnt)
    y_slots = _moe(hn_all, plan, w_gate[i], w_up[i], w_down[i], n_slots)

    y_p, y_s = _ple_out(dest, h1, rt_all, p_prompt[i].reshape(tp, D_PLE), p_sample[i].reshape(ts, D_PLE),
                        vec(norm_ple[i]), wpg, wp, vec(norm_final), y_slots)

    s5shape = lambda a, b: a.reshape(1, b, S5_GROUPS, S5_STATE)
    return (y_p.reshape(bp, lp, D_MODEL), y_s.reshape(bs, ls, D_MODEL),
            gla_p[None], s5shape(re_p, bp), s5shape(im_p, bp),
            gla_s[None], s5shape(re_s, bs), s5shape(im_s, bs))
```

```python
import functools
import math

import jax
import jax.numpy as jnp
from jax import lax
from jax.experimental import pallas as pl
from jax.experimental.pallas import tpu as pltpu

F32 = jnp.float32
BF16 = jnp.bfloat16

D_MODEL = 2048
D_GLA = 1024
D_S5 = 1024
GLA_HEADS = 4
GLA_DV = 256
GLA_DK = 128
GLA_RANK = 16
GLA_CHUNK = 64
S5_GROUP = 16
S5_GROUPS = 64
S5_STATE = 64
N_EGROUPS = 4
N_EPG = 8
N_EXPERTS = 32
D_EXPERT = 512
D_PLE = 256
EPS = 1e-6

LANES = 128
QK_W = GLA_HEADS * GLA_DK
S5_GB = 8
S5_NGB = S5_GROUPS // S5_GB
S5_SL = S5_GB * S5_STATE
TOK_TILE = 256
MOE_BLK = 256
VMEM_LIMIT = 56 * 1024 * 1024
MOE_VMEM_LIMIT = 60 * 1024 * 1024


def _const_spec(shape):
    nd = len(shape)
    return pl.BlockSpec(shape, lambda *_: (0,) * nd, pipeline_mode=pl.Buffered(1))


def _rms(x, g):
    return x * lax.rsqrt(jnp.mean(x * x, axis=-1, keepdims=True) + EPS) * g


def _dot(a, b):
    return jnp.dot(a, b, preferred_element_type=F32)


def _log_sigmoid(x):
    return -(jnp.maximum(-x, 0.0) + jnp.log1p(jnp.exp(-jnp.abs(x))))


N_QKVR = 2 * QK_W + 2 * D_GLA
W_IN_COLS = N_QKVR + GLA_RANK + D_S5


def _w_in_prep_kernel(w_ref, o_ref):
    o_ref[:, 0:N_QKVR] = w_ref[:, 0:N_QKVR].astype(BF16)
    tail = w_ref[:, N_QKVR:W_IN_COLS]
    o_ref[:, N_QKVR:N_QKVR + D_S5] = tail[:, GLA_RANK:GLA_RANK + D_S5].astype(BF16)
    o_ref[:, N_QKVR + D_S5:N_QKVR + D_S5 + LANES] = tail[:, 0:LANES].astype(BF16)


def _w_in_prep(wi):
    rows = 256
    return pl.pallas_call(
        _w_in_prep_kernel,
        grid=(D_MODEL // rows,),
        in_specs=[pl.BlockSpec((rows, W_IN_COLS), lambda i: (i, 0))],
        out_specs=pl.BlockSpec((rows, N_QKVR + D_S5 + LANES), lambda i: (i, 0)),
        out_shape=jax.ShapeDtypeStruct((D_MODEL, N_QKVR + D_S5 + LANES), BF16),
        compiler_params=pltpu.CompilerParams(dimension_semantics=("arbitrary",), vmem_limit_bytes=VMEM_LIMIT),
        name="w_in_prep",
    )(wi)


def _in_proj_kernel(x_ref, g_ref, wm_ref, wgu_ref, gb_ref,
                    q_ref, k_ref, v_ref, r_ref, la_ref, u_ref):
    hb = _rms(x_ref[...], g_ref[...]).astype(BF16)

    def seg(a, b):
        return _dot(hb, wm_ref[:, a:b])

    q_ref[...] = (seg(0, QK_W) * (GLA_DK ** -0.5)).astype(q_ref.dtype)
    k_ref[...] = seg(QK_W, 2 * QK_W).astype(k_ref.dtype)
    v_ref[...] = seg(2 * QK_W, 2 * QK_W + D_GLA).astype(v_ref.dtype)
    r_ref[...] = seg(2 * QK_W + D_GLA, N_QKVR).astype(r_ref.dtype)
    u_ref[...] = seg(N_QKVR, N_QKVR + D_S5)
    zg = seg(N_QKVR + D_S5, N_QKVR + D_S5 + LANES)
    xg = _dot(zg.astype(BF16), wgu_ref[...]) + gb_ref[...]
    la_ref[...] = _log_sigmoid(xg) * (1.0 / 16.0)


def _in_proj(x2d, g, wm, wgu, gbias, act_dtype):
    t = x2d.shape[0]
    tm = TOK_TILE
    row = lambda w: pl.BlockSpec((tm, w), lambda i: (i, 0))
    return pl.pallas_call(
        _in_proj_kernel,
        grid=(t // tm,),
        in_specs=[row(D_MODEL), _const_spec((1, D_MODEL)), _const_spec(wm.shape),
                  _const_spec(wgu.shape), _const_spec((1, QK_W))],
        out_specs=[row(QK_W), row(QK_W), row(D_GLA), row(D_GLA), row(QK_W), row(D_S5)],
        out_shape=[jax.ShapeDtypeStruct((t, QK_W), act_dtype), jax.ShapeDtypeStruct((t, QK_W), act_dtype),
                   jax.ShapeDtypeStruct((t, D_GLA), act_dtype), jax.ShapeDtypeStruct((t, D_GLA), act_dtype),
                   jax.ShapeDtypeStruct((t, QK_W), F32), jax.ShapeDtypeStruct((t, D_S5), F32)],
        compiler_params=pltpu.CompilerParams(dimension_semantics=("arbitrary",), vmem_limit_bytes=VMEM_LIMIT),
        name="in_proj",
    )(x2d, g, wm, wgu, gbias)


_NT = (((1,), (1,)), ((), ()))
_TN = (((0,), (0,)), ((), ()))


def _gla_pre(q, k, la, c):
    r = q.shape[0]
    shift = int(math.log2(c))
    ri = lax.broadcasted_iota(jnp.int32, (r, r), 0)
    si = lax.broadcasted_iota(jnp.int32, (r, r), 1)
    mask = ((ri >> shift) == (si >> shift)) & (ri >= si)
    tri = jnp.where(mask, 1.0, 0.0).astype(BF16)
    hi = la.astype(BF16)
    r1 = la - hi.astype(F32)
    mid = r1.astype(BF16)
    lo = (r1 - mid.astype(F32)).astype(BF16)
    cum = _dot(tri, hi) + _dot(tri, mid) + _dot(tri, lo)
    last = jnp.concatenate([jnp.broadcast_to(cum[(i + 1) * c - 1:(i + 1) * c, :], (c, cum.shape[1]))
                            for i in range(r // c)], axis=0)
    qe = (q * jnp.exp(cum)).astype(BF16)
    ke = (k * jnp.exp(-cum)).astype(BF16)
    kd = (k * jnp.exp(last - cum)).astype(BF16)
    return qe, ke, kd, cum, mask


def _gla_intra(qe, ke, v, mask):
    sc = lax.dot_general(qe, ke, _NT, preferred_element_type=F32)
    return _dot(jnp.where(mask, sc, 0.0).astype(BF16), v)


def _gla_finish(o, r, g):
    rf = r.astype(F32)
    return _rms(o, g) * (rf * jax.nn.sigmoid(rf))


GLA_SAFE_LOG_DECAY = -60.0


def _gla_token_step(t, rows, q, k, v, a_all, st_ref, oacc_ref):
    m = rows == t
    a = jnp.sum(jnp.where(m, a_all, 0.0), axis=0, keepdims=True)
    kt = jnp.where(m, k, 0.0).astype(BF16)
    qt = jnp.where(m, q, 0.0).astype(BF16)
    vt = jnp.where(m, v, jnp.zeros_like(v))
    st = st_ref[...] * a + lax.dot_general(vt, kt, _TN, preferred_element_type=F32)
    st_ref[...] = st
    oacc_ref[...] += lax.dot_general(qt, st.astype(BF16), _NT, preferred_element_type=F32)


def _gla_prompt_kernel(q_ref, k_ref, v_ref, r_ref, la_ref, g_ref, o_ref, sfin_ref, st_ref, oacc_ref,
                       *, n_chunks):
    j = pl.program_id(1)

    @pl.when(j == 0)
    def _():
        st_ref[...] = jnp.zeros_like(st_ref)

    c = GLA_CHUNK
    qe, ke, kd, cum, mask = _gla_pre(q_ref[...].astype(F32), k_ref[...].astype(F32), la_ref[...], c)
    g = g_ref[...]
    safe = jnp.min(cum) >= GLA_SAFE_LOG_DECAY

    @pl.when(safe)
    def _():
        for h in range(GLA_HEADS):
            kc = slice(h * GLA_DK, (h + 1) * GLA_DK)
            vc = slice(h * GLA_DV, (h + 1) * GLA_DV)
            v = v_ref[:, vc]
            qe_h, kd_h = qe[:, kc], kd[:, kc]
            o = _gla_intra(qe_h, ke[:, kc], v, mask)
            st = st_ref[h]
            inter = []
            for ci in range(n_chunks):
                rows = slice(ci * c, (ci + 1) * c)
                inter.append(lax.dot_general(qe_h[rows], st.astype(BF16), _NT, preferred_element_type=F32))
                dec = jnp.exp(cum[(ci + 1) * c - 1:(ci + 1) * c, kc])
                st = st * dec + lax.dot_general(v[rows], kd_h[rows], _TN, preferred_element_type=F32)
            st_ref[h] = st
            o = o + jnp.concatenate(inter, axis=0)
            o_ref[:, vc] = _gla_finish(o, r_ref[:, vc], g).astype(o_ref.dtype)

    @pl.when(jnp.logical_not(safe))
    def _():
        n_rows = n_chunks * c
        rows = lax.broadcasted_iota(jnp.int32, (n_rows, 1), 0)
        for h in range(GLA_HEADS):
            kc = slice(h * GLA_DK, (h + 1) * GLA_DK)
            vc = slice(h * GLA_DV, (h + 1) * GLA_DV)
            q, k, v = q_ref[:, kc].astype(F32), k_ref[:, kc].astype(F32), v_ref[:, vc]
            a_all = jnp.exp(la_ref[:, kc])
            oacc_ref[...] = jnp.zeros_like(oacc_ref)

            def step(t, carry):
                _gla_token_step(t, rows, q, k, v, a_all, st_ref.at[h], oacc_ref)
                return carry

            lax.fori_loop(0, n_rows, step, 0)
            o_ref[:, vc] = _gla_finish(oacc_ref[...], r_ref[:, vc], g).astype(o_ref.dtype)

    @pl.when(j == pl.num_programs(1) - 1)
    def _():
        for h in range(GLA_HEADS):
            sfin_ref[0, h] = st_ref[h].T


def _gla_prompt(q, k, v, r, la, g, batch, seq):
    rb = 4 * GLA_CHUNK
    nj = seq // rb
    row = lambda w: pl.BlockSpec((rb, w), lambda b, j: (b * nj + j, 0))
    return pl.pallas_call(
        functools.partial(_gla_prompt_kernel, n_chunks=rb // GLA_CHUNK),
        grid=(batch, nj),
        in_specs=[row(QK_W), row(QK_W), row(D_GLA), row(D_GLA), row(QK_W), _const_spec((1, GLA_DV))],
        out_specs=[row(D_GLA),
                   pl.BlockSpec((1, GLA_HEADS, GLA_DK, GLA_DV), lambda b, j: (b, 0, 0, 0))],
        out_shape=[jax.ShapeDtypeStruct((batch * seq, D_GLA), BF16),
                   jax.ShapeDtypeStruct((batch, GLA_HEADS, GLA_DK, GLA_DV), F32)],
        scratch_shapes=[pltpu.VMEM((GLA_HEADS, GLA_DV, GLA_DK), F32), pltpu.VMEM((rb, GLA_DV), F32)],
        compiler_params=pltpu.CompilerParams(dimension_semantics=("arbitrary", "arbitrary"),
                                             vmem_limit_bytes=VMEM_LIMIT),
        name="gla_prompt",
    )(q, k, v, r, la, g)


def _gla_sample_kernel(q_ref, k_ref, v_ref, r_ref, la_ref, g_ref, s0_ref, o_ref, sfin_ref, st_ref, oacc_ref,
                       *, n_seq, seq):
    qe, ke, kd, cum, mask = _gla_pre(q_ref[...].astype(F32), k_ref[...].astype(F32), la_ref[...], seq)
    g = g_ref[...]
    safe = jnp.min(cum) >= GLA_SAFE_LOG_DECAY

    @pl.when(safe)
    def _():
        lasts = jnp.concatenate([cum[(s + 1) * seq - 1:(s + 1) * seq, :] for s in range(n_seq)]
                                + [jnp.zeros((GLA_DK - n_seq, cum.shape[1]), F32)], axis=0)
        pair = 2 * seq
        upper = lax.broadcasted_iota(jnp.int32, (pair, GLA_DK), 0) < seq
        for h in range(GLA_HEADS):
            kc = slice(h * GLA_DK, (h + 1) * GLA_DK)
            vc = slice(h * GLA_DV, (h + 1) * GLA_DV)
            v = v_ref[:, vc]
            qe_h, kd_h = qe[:, kc], kd[:, kc]
            o = _gla_intra(qe_h, ke[:, kc], v, mask)
            dec_t = jnp.exp(lasts[:, kc].T)
            inter = []
            for p in range(n_seq // 2):
                rows = slice(p * pair, (p + 1) * pair)
                qe_p, kd_p, v_p = qe_h[rows], kd_h[rows], v[rows]
                for half in range(2):
                    s = 2 * p + half
                    s0 = s0_ref[s, h]
                    o_s = _dot(qe_p, s0.astype(BF16))
                    inter.append(o_s[half * seq:(half + 1) * seq])
                    kd_s = jnp.where(upper if half == 0 else ~upper, kd_p, jnp.zeros_like(kd_p))
                    dec = jnp.broadcast_to(dec_t[:, s:s + 1], (GLA_DK, GLA_DV))
                    sfin_ref[s, h] = s0 * dec + lax.dot_general(kd_s, v_p, _TN, preferred_element_type=F32)
            o = o + jnp.concatenate(inter, axis=0)
            o_ref[:, vc] = _gla_finish(o, r_ref[:, vc], g).astype(o_ref.dtype)

    @pl.when(jnp.logical_not(safe))
    def _():
        n_rows = n_seq * seq
        rows = lax.broadcasted_iota(jnp.int32, (n_rows, 1), 0)
        for h in range(GLA_HEADS):
            kc = slice(h * GLA_DK, (h + 1) * GLA_DK)
            vc = slice(h * GLA_DV, (h + 1) * GLA_DV)
            q, k, v = q_ref[:, kc].astype(F32), k_ref[:, kc].astype(F32), v_ref[:, vc]
            a_all = jnp.exp(la_ref[:, kc])
            oacc_ref[...] = jnp.zeros_like(oacc_ref)

            def step(t, carry):
                s = t // seq

                @pl.when(t % seq == 0)
                def _():
                    st_ref[...] = s0_ref[s, h].T

                _gla_token_step(t, rows, q, k, v, a_all, st_ref, oacc_ref)

                @pl.when(t % seq == seq - 1)
                def _():
                    sfin_ref[s, h] = st_ref[...].T

                return carry

            lax.fori_loop(0, n_rows, step, 0)
            o_ref[:, vc] = _gla_finish(oacc_ref[...], r_ref[:, vc], g).astype(o_ref.dtype)


def _gla_sample(q, k, v, r, la, g, s0, batch, seq):
    ns = 16
    rb = ns * seq
    row = lambda w: pl.BlockSpec((rb, w), lambda i: (i, 0))
    st = pl.BlockSpec((ns, GLA_HEADS, GLA_DK, GLA_DV), lambda i: (i, 0, 0, 0))
    return pl.pallas_call(
        functools.partial(_gla_sample_kernel, n_seq=ns, seq=seq),
        grid=(batch // ns,),
        in_specs=[row(QK_W), row(QK_W), row(D_GLA), row(D_GLA), row(QK_W), _const_spec((1, GLA_DV)), st],
        out_specs=[row(D_GLA), st],
        out_shape=[jax.ShapeDtypeStruct((batch * seq, D_GLA), BF16),
                   jax.ShapeDtypeStruct((batch, GLA_HEADS, GLA_DK, GLA_DV), F32)],
        scratch_shapes=[pltpu.VMEM((GLA_DV, GLA_DK), F32), pltpu.VMEM((rb, GLA_DV), F32)],
        compiler_params=pltpu.CompilerParams(dimension_semantics=("arbitrary",), vmem_limit_bytes=VMEM_LIMIT),
        name="gla_sample",
    )(q, k, v, r, la, g, s0)


def _s5_kernel(u_ref, wbu_ref, wc_ref, are_ref, aim_ref, d_ref, h0r_ref, h0i_ref,
               y_ref, sre_ref, sim_ref, bu_ref, xs_ref, car_ref, *, nb, tc, has_state):
    j = pl.program_id(2)

    @pl.when(j == 0)
    def _():
        if has_state:
            car_ref[0] = h0r_ref[...]
            car_ref[1] = h0i_ref[...]
        else:
            car_ref[...] = jnp.zeros_like(car_ref)

    u2 = u_ref[...].reshape(nb * tc, LANES)
    ub = u2.astype(BF16)
    nl = S5_SL // LANES
    for l in range(2 * nl):
        bu_ref[l] = _dot(ub, wbu_ref[0, :, l * LANES:(l + 1) * LANES])
    a_r = [jnp.broadcast_to(are_ref[0, :, l * LANES:(l + 1) * LANES], (nb, LANES)) for l in range(nl)]
    a_i = [jnp.broadcast_to(aim_ref[0, :, l * LANES:(l + 1) * LANES], (nb, LANES)) for l in range(nl)]

    def step(t, carry):
        rows = pl.ds(t, nb, stride=tc)
        out = []
        for l in range(nl):
            xr, xi = carry[2 * l], carry[2 * l + 1]
            nr = a_r[l] * xr - a_i[l] * xi + bu_ref[l, rows, :]
            ni = a_r[l] * xi + a_i[l] * xr + bu_ref[nl + l, rows, :]
            xs_ref[l, rows, :] = nr
            xs_ref[nl + l, rows, :] = ni
            out += [nr, ni]
        return tuple(out)

    init = []
    for l in range(nl):
        init += [car_ref[0, :, l * LANES:(l + 1) * LANES], car_ref[1, :, l * LANES:(l + 1) * LANES]]
    fin = lax.fori_loop(0, tc, step, tuple(init), unroll=8)
    xr = jnp.concatenate([fin[2 * l] for l in range(nl)], axis=1)
    xi = jnp.concatenate([fin[2 * l + 1] for l in range(nl)], axis=1)
    car_ref[0] = xr
    car_ref[1] = xi
    y = d_ref[0] * u2
    for l in range(2 * nl):
        y = y + _dot(xs_ref[l].astype(BF16), wc_ref[0, l * LANES:(l + 1) * LANES, :])
    y_ref[...] = y.reshape(nb, tc, LANES)

    @pl.when(j == pl.num_programs(2) - 1)
    def _():
        sre_ref[...] = xr
        sim_ref[...] = xi


def _s5(u3d, wbu, wc, a_re, a_im, dsk, h0r, h0i, nb, tc, has_state):
    batch, seq, _ = u3d.shape
    grid = (S5_NGB, batch // nb, seq // tc)
    st = pl.BlockSpec((nb, S5_SL), lambda g, b, j: (b, g))
    par = lambda w: pl.BlockSpec((1, 1, w), lambda g, b, j: (g, 0, 0))
    ublk = pl.BlockSpec((nb, tc, LANES), lambda g, b, j: (b, j, g))
    return pl.pallas_call(
        functools.partial(_s5_kernel, nb=nb, tc=tc, has_state=has_state),
        grid=grid,
        in_specs=[ublk,
                  pl.BlockSpec((1, LANES, 2 * S5_SL), lambda g, b, j: (g, 0, 0)),
                  pl.BlockSpec((1, 2 * S5_SL, LANES), lambda g, b, j: (g, 0, 0)),
                  par(S5_SL), par(S5_SL), par(LANES), st, st],
        out_specs=[ublk, st, st],
        out_shape=[jax.ShapeDtypeStruct(u3d.shape, F32),
                   jax.ShapeDtypeStruct((batch, S5_GROUPS * S5_STATE), F32),
                   jax.ShapeDtypeStruct((batch, S5_GROUPS * S5_STATE), F32)],
        scratch_shapes=[pltpu.VMEM((2 * S5_SL // LANES, nb * tc, LANES), F32),
                        pltpu.VMEM((2 * S5_SL // LANES, nb * tc, LANES), F32),
                        pltpu.VMEM((2, nb, S5_SL), F32)],
        compiler_params=pltpu.CompilerParams(dimension_semantics=("arbitrary",) * 3,
                                             vmem_limit_bytes=VMEM_LIMIT),
        name="s5_state" if has_state else "s5_zero",
    )(u3d, wbu, wc, a_re, a_im, dsk, h0r, h0i)


S5_J = 2
S5_HL = S5_SL // S5_J


def _s5_prompt_kernel(u_ref, wbu_ref, wc_ref, are_ref, aim_ref, d_ref, y_ref, sre_ref, sim_ref,
                      u2_ref, lhs_ref, bu_ref, xs_ref, y2_ref, yo_ref, car_ref, *, nb, tc):
    g = pl.program_id(0)
    j = pl.program_id(1)
    rows = nb * S5_J

    @pl.when((g == 0) & (j == 0))
    def _():
        lhs_ref[...] = jnp.zeros_like(lhs_ref)

    @pl.when(j == 0)
    def _():
        car_ref[...] = jnp.zeros_like(car_ref)

    u2 = u_ref[...].reshape(nb * tc, LANES)
    u2_ref[...] = u2

    def build(t, c):
        u4 = u2_ref[pl.ds(t, nb, stride=tc), :]
        for jj in range(S5_J):
            lhs_ref[t, jj * nb:(jj + 1) * nb, jj * LANES:(jj + 1) * LANES] = u4
        return c

    lax.fori_loop(0, tc, build, 0, unroll=8)
    lhs = lhs_ref[...].reshape(tc * rows, S5_J * LANES).astype(BF16)
    bu_ref[...] = _dot(lhs, wbu_ref[0]).reshape(tc, rows, 2 * S5_HL)
    ar = are_ref[0]
    ai = aim_ref[0]

    def step(t, carry):
        xr, xi = carry
        tile = bu_ref[t]
        nr = ar * xr - ai * xi + tile[:, 0:S5_HL]
        ni = ar * xi + ai * xr + tile[:, S5_HL:2 * S5_HL]
        xs_ref[t] = jnp.concatenate([nr, ni], axis=1)
        return nr, ni

    xr, xi = lax.fori_loop(0, tc, step, (car_ref[0], car_ref[1]), unroll=8)
    car_ref[0] = xr
    car_ref[1] = xi
    xs = xs_ref[...].reshape(tc * rows, 2 * S5_HL).astype(BF16)
    y2_ref[...] = _dot(xs, wc_ref[0]).reshape(tc, rows, S5_J * LANES)
    first_half = lax.broadcasted_iota(jnp.int32, (rows, LANES), 0) < nb

    def unperm(t, c):
        t2 = y2_ref[t]
        part = jnp.where(first_half, t2[:, 0:LANES], t2[:, LANES:2 * LANES])
        tot = part + pltpu.roll(part, nb, axis=0)
        yo_ref[pl.ds(t, nb, stride=tc), :] = tot[0:nb]
        return c

    lax.fori_loop(0, tc, unperm, 0, unroll=8)
    y_ref[...] = (yo_ref[...] + d_ref[0] * u2).reshape(nb, tc, LANES)

    @pl.when(j == pl.num_programs(1) - 1)
    def _():
        sre_ref[...] = jnp.concatenate([xr[jj * nb:(jj + 1) * nb] for jj in range(S5_J)], axis=1)
        sim_ref[...] = jnp.concatenate([xi[jj * nb:(jj + 1) * nb] for jj in range(S5_J)], axis=1)


def _s5_prompt(u3d, wbu2, wc2, a_re2, a_im2, dsk, tc):
    nb, seq, _ = u3d.shape
    rows = nb * S5_J
    assert rows == 8
    st = pl.BlockSpec((nb, S5_SL), lambda g, j: (0, g))
    gblk = lambda s: pl.BlockSpec((1,) + s, lambda g, j: (g, 0, 0))
    ublk = pl.BlockSpec((nb, tc, LANES), lambda g, j: (0, j, g))
    return pl.pallas_call(
        functools.partial(_s5_prompt_kernel, nb=nb, tc=tc),
        grid=(S5_NGB, seq // tc),
        in_specs=[ublk, gblk((S5_J * LANES, 2 * S5_HL)), gblk((2 * S5_HL, S5_J * LANES)),
                  gblk((rows, S5_HL)), gblk((rows, S5_HL)), gblk((1, LANES))],
        out_specs=[ublk, st, st],
        out_shape=[jax.ShapeDtypeStruct(u3d.shape, F32),
                   jax.ShapeDtypeStruct((nb, S5_GROUPS * S5_STATE), F32),
                   jax.ShapeDtypeStruct((nb, S5_GROUPS * S5_STATE), F32)],
        scratch_shapes=[pltpu.VMEM((nb * tc, LANES), F32),
                        pltpu.VMEM((tc, rows, S5_J * LANES), F32),
                        pltpu.VMEM((tc, rows, 2 * S5_HL), F32),
                        pltpu.VMEM((tc, rows, 2 * S5_HL), F32),
                        pltpu.VMEM((tc, rows, S5_J * LANES), F32),
                        pltpu.VMEM((nb * tc, LANES), F32),
                        pltpu.VMEM((2, rows, S5_HL), F32)],
        compiler_params=pltpu.CompilerParams(dimension_semantics=("arbitrary",) * 2,
                                             vmem_limit_bytes=VMEM_LIMIT),
        name="s5_prompt",
    )(u3d, wbu2, wc2, a_re2, a_im2, dsk)


def _s5_prompt_params(wbu, wc, a_re, a_im, nb):
    h = S5_HL
    top = jnp.concatenate([wbu[:, :, 0:h], wbu[:, :, S5_SL:S5_SL + h]], axis=2)
    bot = jnp.concatenate([wbu[:, :, h:2 * h], wbu[:, :, S5_SL + h:S5_SL + 2 * h]], axis=2)
    wbu2 = jnp.concatenate([top, bot], axis=1)
    wc_j = [jnp.concatenate([wc[:, jj * h:(jj + 1) * h, :], wc[:, S5_SL + jj * h:S5_SL + (jj + 1) * h, :]],
                            axis=1) for jj in range(S5_J)]
    wc2 = jnp.concatenate(wc_j, axis=2)
    tile = lambda a: jnp.repeat(a.reshape(S5_NGB, S5_J, h), nb, axis=1)
    return wbu2, wc2, tile(a_re), tile(a_im)


def _mix_out_kernel(xp_ref, xs_ref, ogp_ref, ogs_ref, ysp_ref, yss_ref, glu_w_ref, glu_b_ref, s5n_ref, wo_ref,
                    nffn_ref, wr_ref, h1_ref, hn_ref, rt_ref, rtt_ref, cnt_ref, *, n_prompt_tiles):
    is_p = pl.program_id(0) < n_prompt_tiles
    x = jnp.where(is_p, xp_ref[...], xs_ref[...])
    og = jnp.where(is_p, ogp_ref[...], ogs_ref[...])
    y = jax.nn.gelu(jnp.where(is_p, ysp_ref[...], yss_ref[...]))
    y = y * jax.nn.sigmoid(_dot(y.astype(BF16), glu_w_ref[...]) + glu_b_ref[...])
    yn = _rms(y, s5n_ref[...]).astype(BF16)
    mix = _dot(og, wo_ref[0:D_GLA, :]) + _dot(yn, wo_ref[D_GLA:D_GLA + D_S5, :])
    h1 = x + mix
    h1_ref[...] = h1
    hn = _rms(h1, nffn_ref[...])
    hn_ref[...] = _pack_bf16_pair(hn[:, 0:HALF], hn[:, HALF:D_MODEL])
    hn_hi = hn.astype(BF16)
    hn_lo = (hn - hn_hi.astype(F32)).astype(BF16)
    logits = _dot(hn_hi, wr_ref[0]) + _dot(hn_hi, wr_ref[1]) + _dot(hn_lo, wr_ref[0])
    rt = _route(logits)

    @pl.when(pl.program_id(0) == 0)
    def _():
        cnt_ref[...] = jnp.zeros_like(cnt_ref)

    tm = rt.shape[0]
    lane = lax.broadcasted_iota(jnp.int32, rt.shape, 1).astype(F32)
    oh0 = lane == rt[:, 0:1]
    oh1 = lane == rt[:, 1:2]
    both = jnp.where(oh0 | oh1, 1.0, 0.0)
    ri = lax.broadcasted_iota(jnp.int32, (tm, tm), 0)
    ci = lax.broadcasted_iota(jnp.int32, (tm, tm), 1)
    before = _dot(jnp.where(ri > ci, 1.0, 0.0).astype(BF16), both.astype(BF16)) + cnt_ref[...]
    rank0 = jnp.sum(jnp.where(oh0, before, 0.0), axis=-1, keepdims=True)
    rank1 = jnp.sum(jnp.where(oh1, before, 0.0), axis=-1, keepdims=True)
    cnt_ref[...] += jnp.sum(both, axis=0, keepdims=True)
    rt = jnp.where(lane == 4.0, rank0, jnp.where(lane == 5.0, rank1, rt))
    rt_ref[...] = rt
    rtt_ref[...] = rt.T[0:8, :]


def _route(logits):
    col = lax.broadcasted_iota(jnp.int32, logits.shape, 1)
    colf = col.astype(F32)
    neg = -jnp.inf

    def first_argmax(vals):
        m = jnp.max(vals, axis=-1, keepdims=True)
        idx = jnp.min(jnp.where(vals == m, colf, float(LANES)), axis=-1, keepdims=True)
        return m, idx

    lg = jnp.where(col < N_EGROUPS, logits, neg)
    gmax, gsel = first_argmax(lg)
    p_g = 1.0 / jnp.sum(jnp.exp(lg - gmax), axis=-1, keepdims=True)
    ecol = col - N_EGROUPS
    egrp = (ecol >> 3).astype(F32)
    in_group = (ecol >= 0) & (ecol < N_EXPERTS) & (egrp == gsel)
    le = jnp.where(in_group, logits, neg)
    m1, i1 = first_argmax(le)
    le2 = jnp.where(colf == i1, neg, le)
    m2, i2 = first_argmax(le2)
    e2 = jnp.exp(m2 - m1)
    den = 1.0 + e2
    w1 = p_g * (1.0 / den)
    w2 = p_g * (e2 / den)
    e1f = i1 - float(N_EGROUPS)
    e2f = i2 - float(N_EGROUPS)
    out = jnp.where(col == 0, e1f, jnp.where(col == 1, e2f, jnp.where(col == 2, w1, jnp.where(col == 3, w2, 0.0))))
    return out


def _mix_out(xp, xs, ogp, ogs, ysp, yss, glu_w, glu_b, s5n, wo, nffn, wr):
    tm = TOK_TILE
    npt, nst = xp.shape[0] // tm, xs.shape[0] // tm
    t = (npt + nst) * tm
    row = lambda w: pl.BlockSpec((tm, w), lambda i: (i, 0))
    prow = lambda w: pl.BlockSpec((tm, w), lambda i: (jnp.minimum(i, npt - 1), 0))
    srow = lambda w: pl.BlockSpec((tm, w), lambda i: (jnp.maximum(i - npt, 0), 0))
    return pl.pallas_call(
        functools.partial(_mix_out_kernel, n_prompt_tiles=npt),
        grid=(npt + nst,),
        in_specs=[prow(D_MODEL), srow(D_MODEL), prow(D_GLA), srow(D_GLA), prow(D_S5), srow(D_S5),
                  _const_spec(glu_w.shape), _const_spec((1, D_S5)), _const_spec((1, D_S5)),
                  _const_spec(wo.shape), _const_spec((1, D_MODEL)), _const_spec(wr.shape)],
        out_specs=[row(D_MODEL), row(HALF), row(LANES), pl.BlockSpec((8, tm), lambda i: (0, i)),
                   pl.BlockSpec((1, LANES), lambda i: (0, 0))],
        out_shape=[jax.ShapeDtypeStruct((t, D_MODEL), F32), jax.ShapeDtypeStruct((t, HALF), jnp.uint32),
                   jax.ShapeDtypeStruct((t, LANES), F32), jax.ShapeDtypeStruct((8, t), F32),
                   jax.ShapeDtypeStruct((1, LANES), F32)],
        compiler_params=pltpu.CompilerParams(dimension_semantics=("arbitrary",), vmem_limit_bytes=VMEM_LIMIT),
        name="mix_out",
    )(xp, xs, ogp, ogs, ysp, yss, glu_w, glu_b, s5n, wo, nffn, wr)


DMA_UNROLL = 8


MOE_SB = 3
HALF = D_MODEL // 2


def _pack_bf16_pair(lo, hi):
    def bits(x):
        b = pltpu.bitcast(x, jnp.uint32)
        return (b + jnp.uint32(0x7FFF) + ((b >> 16) & jnp.uint32(1))) >> 16
    return bits(lo) | (bits(hi) << 16)


def _unpack_bf16_pair(u):
    return (pltpu.bitcast(u << 16, F32), pltpu.bitcast(u & jnp.uint32(0xFFFF0000), F32))


def _moe_kernel(se_ref, sm_ref, sn_ref, sbase_ref, xs_hbm, wg_ref, wu_ref, wd_ref, out_hbm,
                xb_ref, yb_ref, wgb_ref, wub_ref, wdb_ref, gsem, ssem):
    b = pl.program_id(0)
    nb = pl.num_programs(0)
    n = sn_ref[b]
    slot = lax.rem(b, 2)

    def issue_gather(blk, sl):
        base = pl.multiple_of(sbase_ref[blk], MOE_BLK)
        for m in range(1, MOE_SB + 1):
            @pl.when((sn_ref[blk] > 0) & (sm_ref[blk] == m))
            def _():
                rows = m * MOE_BLK
                pltpu.make_async_copy(xs_hbm.at[pl.ds(base, rows), :], xb_ref.at[sl, pl.ds(0, rows), :],
                                      gsem.at[sl]).start()

    def wait_gather(blocks, sl):
        rows = pl.ds(0, pl.multiple_of(blocks * MOE_BLK, MOE_BLK))
        pltpu.make_async_copy(xs_hbm.at[rows, :], xb_ref.at[sl, rows, :], gsem.at[sl]).wait()

    m_cur = sm_ref[b]
    m_prev = sm_ref[jnp.maximum(b - 1, 0)]
    has_prev = (b > 0) & (m_prev > 0)

    def out_copy(rows):
        dst = pl.ds(pl.multiple_of(sbase_ref[b], MOE_BLK), rows)
        return pltpu.make_async_copy(yb_ref.at[pl.ds(0, rows), :], out_hbm.at[dst, :], ssem.at[0])

    def wait_prev_out():
        rows = pl.ds(0, pl.multiple_of(m_prev * MOE_BLK, MOE_BLK))
        pltpu.make_async_copy(yb_ref.at[rows, :], out_hbm.at[rows, :], ssem.at[0]).wait()

    def write_out(rows):
        out_copy(rows).start()

        @pl.when(b == nb - 1)
        def _():
            out_copy(rows).wait()

    @pl.when(b == 0)
    def _():
        issue_gather(0, 0)

    @pl.when(b + 1 < nb)
    def _():
        issue_gather(b + 1, 1 - slot)

    def compute(rows):
        x_lo, x_hi = _unpack_bf16_pair(xb_ref[slot, 0:rows, :])
        x_lo, x_hi = x_lo.astype(BF16), x_hi.astype(BF16)
        gate = _dot(x_lo, wgb_ref[0:HALF, :]) + _dot(x_hi, wgb_ref[HALF:D_MODEL, :])
        up = _dot(x_lo, wub_ref[0:HALF, :]) + _dot(x_hi, wub_ref[HALF:D_MODEL, :])
        hid = (gate * jax.nn.sigmoid(gate) * up).astype(BF16)

        @pl.when(has_prev)
        def _():
            wait_prev_out()

        yb_ref[0:rows, :] = _pack_bf16_pair(_dot(hid, wdb_ref[:, 0:HALF]), _dot(hid, wdb_ref[:, HALF:D_MODEL]))
        write_out(rows)

    @pl.when(n > 0)
    def _():
        prev_e = se_ref[jnp.maximum(b - 1, 0)]

        @pl.when((b == 0) | (prev_e != se_ref[b]))
        def _():
            wgb_ref[...] = wg_ref[...].astype(BF16)
            wub_ref[...] = wu_ref[...].astype(BF16)
            wdb_ref[...] = wd_ref[...].astype(BF16)

        wait_gather(m_cur, slot)
        for m in range(1, MOE_SB + 1):
            @pl.when(m_cur == m)
            def _():
                compute(m * MOE_BLK)

    @pl.when((n == 0) & (m_cur > 0))
    def _():
        @pl.when(has_prev)
        def _():
            wait_prev_out()

        yb_ref[...] = jnp.zeros_like(yb_ref)
        for m in range(1, MOE_SB + 1):
            @pl.when(m_cur == m)
            def _():
                write_out(m * MOE_BLK)

    @pl.when((m_cur == 0) & has_prev)
    def _():
        wait_prev_out()


def _moe(x_slots, plan, w_gate, w_up, w_down, n_out_rows):
    sb_e, sb_m, sb_n, sb_base = plan
    sb_rows = MOE_SB * MOE_BLK
    wspec = lambda s: pl.BlockSpec((None,) + s, lambda b, se, *_: (se[b], 0, 0))
    grid_spec = pltpu.PrefetchScalarGridSpec(
        num_scalar_prefetch=4,
        grid=(sb_e.shape[0],),
        in_specs=[pl.BlockSpec(memory_space=pl.ANY),
                  wspec((D_MODEL, D_EXPERT)), wspec((D_MODEL, D_EXPERT)), wspec((D_EXPERT, D_MODEL))],
        out_specs=pl.BlockSpec(memory_space=pl.ANY),
        scratch_shapes=[pltpu.VMEM((2, sb_rows, HALF), jnp.uint32), pltpu.VMEM((sb_rows, HALF), jnp.uint32),
                        pltpu.VMEM((D_MODEL, D_EXPERT), BF16), pltpu.VMEM((D_MODEL, D_EXPERT), BF16),
                        pltpu.VMEM((D_EXPERT, D_MODEL), BF16), pltpu.SemaphoreType.DMA((2,)),
                        pltpu.SemaphoreType.DMA((1,))],
    )
    return pl.pallas_call(
        _moe_kernel,
        grid_spec=grid_spec,
        out_shape=jax.ShapeDtypeStruct((n_out_rows, HALF), jnp.uint32),
        compiler_params=pltpu.CompilerParams(dimension_semantics=("arbitrary",),
                                             vmem_limit_bytes=MOE_VMEM_LIMIT),
        name="moe_experts",
    )(sb_e, sb_m, sb_n, sb_base, x_slots, w_gate, w_up, w_down)


def _dispatch_kernel(dest_ref, zb_ref, hn_ref, out_hbm, buf_ref, zero_ref, sem, zsem):
    i = pl.program_id(0)
    nt = pl.num_programs(0)
    tm = hn_ref.shape[0]
    t_all = nt * tm
    slot = lax.rem(i, 2)
    nblk = zb_ref.shape[0]

    def zero_copy(blk):
        dst = pl.ds(pl.multiple_of(blk * MOE_BLK, MOE_BLK), MOE_BLK)
        return pltpu.make_async_copy(zero_ref, out_hbm.at[dst, :], zsem.at[0])

    @pl.when(i == 0)
    def _():
        zero_ref[...] = jnp.zeros_like(zero_ref)

        def start(blk, c):
            @pl.when(zb_ref[blk] == 1)
            def _():
                zero_copy(blk).start()
            return c

        def wait(blk, c):
            @pl.when(zb_ref[blk] == 1)
            def _():
                zero_copy(blk).wait()
            return c

        lax.fori_loop(0, nblk, start, 0)
        lax.fori_loop(0, nblk, wait, 0)

    def wait_rows(sl):
        for k in range(2):
            pltpu.make_async_copy(buf_ref.at[sl], out_hbm.at[pl.ds(0, tm), :], sem.at[sl]).wait()

    @pl.when(i >= 2)
    def _():
        wait_rows(slot)

    buf_ref[slot] = hn_ref[...]
    base = i * tm

    def grp(g, c):
        for j in range(DMA_UNROLL):
            row = g * (DMA_UNROLL // 2) + j // 2
            d = dest_ref[(j % 2) * t_all + base + row]
            pltpu.make_async_copy(buf_ref.at[slot, pl.ds(row, 1), :], out_hbm.at[pl.ds(d, 1), :],
                                  sem.at[slot]).start()
        return c

    lax.fori_loop(0, 2 * tm // DMA_UNROLL, grp, 0)

    @pl.when(i == nt - 1)
    def _():
        wait_rows(slot)

        @pl.when(nt > 1)
        def _():
            wait_rows(1 - slot)


def _dispatch(dest, zero_blk, hn_pk, n_slots):
    tm = TOK_TILE
    grid_spec = pltpu.PrefetchScalarGridSpec(
        num_scalar_prefetch=2,
        grid=(hn_pk.shape[0] // tm,),
        in_specs=[pl.BlockSpec((tm, HALF), lambda i, d, z: (i, 0))],
        out_specs=pl.BlockSpec(memory_space=pl.ANY),
        scratch_shapes=[pltpu.VMEM((2, tm, HALF), jnp.uint32), pltpu.VMEM((MOE_BLK, HALF), jnp.uint32),
                        pltpu.SemaphoreType.DMA((2,)), pltpu.SemaphoreType.DMA((1,))],
    )
    return pl.pallas_call(
        _dispatch_kernel,
        grid_spec=grid_spec,
        out_shape=jax.ShapeDtypeStruct((n_slots, HALF), jnp.uint32),
        compiler_params=pltpu.CompilerParams(dimension_semantics=("arbitrary",), vmem_limit_bytes=VMEM_LIMIT),
        name="moe_dispatch",
    )(dest, zero_blk, hn_pk)


def _moe_plan(rtt, cnt):
    t_all = rtt.shape[1]
    n_assign = 2 * t_all
    e_flat = rtt[0:2].astype(jnp.int32).reshape(-1)
    rank = rtt[4:6].astype(jnp.int32).reshape(-1)
    counts = cnt[0, 0:N_EXPERTS].astype(jnp.int32)
    padded = (counts + MOE_BLK - 1) // MOE_BLK * MOE_BLK
    pad_end = jnp.cumsum(padded)
    pad_start = pad_end - padded
    before = jnp.arange(N_EXPERTS, dtype=jnp.int32)[:, None] < e_flat[None, :]
    dest = rank + jnp.sum(jnp.where(before, padded[:, None], 0), axis=0)
    nblk = -(-(n_assign + N_EXPERTS * (MOE_BLK - 1)) // MOE_BLK)
    n_slots = nblk * MOE_BLK
    blk = jnp.arange(nblk, dtype=jnp.int32)
    last_of_expert = jnp.any((blk[:, None] == (pad_end // MOE_BLK - 1)[None, :]) & (counts > 0)[None, :], axis=1)
    zero_blk = (last_of_expert | (blk >= pad_end[-1] // MOE_BLK)).astype(jnp.int32)
    k_e = padded // MOE_BLK
    sbc = (k_e + MOE_SB - 1) // MOE_SB
    sb_end = jnp.cumsum(sbc)
    sb_start = sb_end - sbc
    n_sb = (nblk + (MOE_SB - 1) * N_EXPERTS) // MOE_SB
    s = jnp.arange(n_sb, dtype=jnp.int32)
    sb_e = jnp.minimum(jnp.sum((s[:, None] >= sb_end[None, :]).astype(jnp.int32), axis=1), N_EXPERTS - 1)
    j = s - sb_start[sb_e]
    real = s < sb_end[-1]
    tail_blk = pad_end[-1] // MOE_BLK + MOE_SB * (s - sb_end[-1])
    sb_m = jnp.where(real, jnp.clip(k_e[sb_e] - MOE_SB * j, 0, MOE_SB), jnp.clip(nblk - tail_blk, 0, MOE_SB))
    sb_n = jnp.where(real, jnp.clip(counts[sb_e] - MOE_SB * MOE_BLK * j, 0, MOE_SB * MOE_BLK), 0)
    sb_base = jnp.where(real, pad_start[sb_e] + MOE_SB * MOE_BLK * j, jnp.minimum(tail_blk, nblk - 1) * MOE_BLK)
    i32 = lambda a: a.astype(jnp.int32)
    return (i32(sb_e), i32(sb_m), i32(sb_n), i32(sb_base)), i32(dest), zero_blk, n_slots


def _ple_out_kernel(dest_ref, h1_ref, rt_ref, pp_ref, ps_ref, nple_ref, wpg_ref, wp_ref, nfin_ref, ys_hbm,
                    op_ref, os_ref, yb_ref, sem, *, n_prompt_tiles):
    i = pl.program_id(0)
    nt = pl.num_programs(0)
    tm = h1_ref.shape[0]
    t_all = nt * tm
    slot = lax.rem(i, 2)

    def issue(tile, sl):
        base = tile * tm

        def grp(g, c):
            for j in range(DMA_UNROLL):
                row = g * (DMA_UNROLL // 2) + j // 2
                d = dest_ref[(j % 2) * t_all + base + row]
                pltpu.make_async_copy(ys_hbm.at[pl.ds(d, 1), :], yb_ref.at[sl, j % 2, pl.ds(row, 1), :],
                                      sem.at[sl]).start()
            return c

        lax.fori_loop(0, 2 * tm // DMA_UNROLL, grp, 0)

    @pl.when(i == 0)
    def _():
        issue(0, 0)

    def wait_rows(sl):
        for k in range(2):
            pltpu.make_async_copy(ys_hbm.at[pl.ds(0, tm), :], yb_ref.at[sl, k], sem.at[sl]).wait()

    wait_rows(slot)
    is_p = i < n_prompt_tiles
    rt = rt_ref[...]
    w0, w1 = rt[:, 2:3], rt[:, 3:4]
    lo0, hi0 = _unpack_bf16_pair(yb_ref[slot, 0])
    lo1, hi1 = _unpack_bf16_pair(yb_ref[slot, 1])
    h2 = h1_ref[...] + jnp.concatenate([w0 * lo0 + w1 * lo1, w0 * hi0 + w1 * hi1], axis=1)
    hb = _rms(h2, nple_ref[...]).astype(BF16)
    nbase = jnp.minimum(i + 1, nt - 1) * tm
    n_chunk = 8
    cw, ca = D_MODEL // n_chunk, 2 * tm // n_chunk
    gates = []
    for c in range(n_chunk):
        gates.append(jax.nn.sigmoid(_dot(hb, wpg_ref[:, c * cw:(c + 1) * cw])))
        for a in range(c * ca, (c + 1) * ca):
            d = dest_ref[(a % 2) * t_all + nbase + a // 2]
            pltpu.make_async_copy(ys_hbm.at[pl.ds(d, 1), :], yb_ref.at[1 - slot, a % 2, pl.ds(a // 2, 1), :],
                                  sem.at[1 - slot]).start()
    gate = jnp.concatenate(gates, axis=1)

    @pl.when(i == nt - 1)
    def _():
        wait_rows(1 - slot)

    p = jnp.where(is_p, pp_ref[...], ps_ref[...])
    h3 = h2 + _dot(p.astype(BF16), wp_ref[...]) * gate
    y = _rms(h3, nfin_ref[...])

    @pl.when(is_p)
    def _():
        op_ref[...] = y

    @pl.when(jnp.logical_not(is_p))
    def _():
        os_ref[...] = y


def _ple_out(dest, h1, rt, pp, ps, nple, wpg, wp, nfin, y_slots):
    tm = TOK_TILE
    npt, nst = pp.shape[0] // tm, ps.shape[0] // tm
    row = lambda w: pl.BlockSpec((tm, w), lambda i, d: (i, 0))
    prow = lambda w: pl.BlockSpec((tm, w), lambda i, d: (jnp.minimum(i, npt - 1), 0))
    srow = lambda w: pl.BlockSpec((tm, w), lambda i, d: (jnp.maximum(i - npt, 0), 0))
    const = lambda shape: pl.BlockSpec(shape, lambda i, d: (0,) * len(shape), pipeline_mode=pl.Buffered(1))
    grid_spec = pltpu.PrefetchScalarGridSpec(
        num_scalar_prefetch=1,
        grid=(npt + nst,),
        in_specs=[row(D_MODEL), row(LANES), prow(D_PLE), srow(D_PLE), const((1, D_MODEL)), const(wpg.shape),
                  const(wp.shape), const((1, D_MODEL)), pl.BlockSpec(memory_space=pl.ANY)],
        out_specs=[prow(D_MODEL), srow(D_MODEL)],
        scratch_shapes=[pltpu.VMEM((2, 2, tm, HALF), jnp.uint32), pltpu.SemaphoreType.DMA((2,))],
    )
    return pl.pallas_call(
        functools.partial(_ple_out_kernel, n_prompt_tiles=npt),
        grid_spec=grid_spec,
        out_shape=[jax.ShapeDtypeStruct((pp.shape[0], D_MODEL), F32),
                   jax.ShapeDtypeStruct((ps.shape[0], D_MODEL), F32)],
        compiler_params=pltpu.CompilerParams(dimension_semantics=("arbitrary",), vmem_limit_bytes=VMEM_LIMIT),
        name="ple_out",
    )(dest, h1, rt, pp, ps, nple, wpg, wp, nfin, y_slots)


def _s5_params(lam_re, lam_im, log_dt, b_re, b_im, c_re, c_im, d_skip):
    dt = jnp.exp(log_dt)[:, None]
    mag = jnp.exp(lam_re * dt)
    ab_re = mag * jnp.cos(lam_im * dt)
    ab_im = mag * jnp.sin(lam_im * dt)
    den = lam_re * lam_re + lam_im * lam_im
    nr = ab_re - 1.0
    f_re = (nr * lam_re + ab_im * lam_im) / den
    f_im = (ab_im * lam_re - nr * lam_im) / den
    bb_re = f_re[..., None] * b_re - f_im[..., None] * b_im
    bb_im = f_re[..., None] * b_im + f_im[..., None] * b_re
    eye = jnp.eye(S5_GB, dtype=F32)

    def bu_w(bb):
        bb = bb.reshape(S5_NGB, S5_GB, S5_STATE, S5_GROUP)
        w = jnp.einsum('nlph,lm->nlhmp', bb, eye)
        return w.reshape(S5_NGB, S5_GB * S5_GROUP, S5_SL)

    def c_w(c):
        c = c.reshape(S5_NGB, S5_GB, S5_GROUP, S5_STATE)
        w = jnp.einsum('nlhp,lm->nlpmh', c, eye)
        return w.reshape(S5_NGB, S5_SL, S5_GB * S5_GROUP)

    wbu = jnp.concatenate([bu_w(bb_re), bu_w(bb_im)], axis=2).astype(BF16)
    wc = jnp.concatenate([c_w(c_re), -c_w(c_im)], axis=1).astype(BF16)
    a_re = ab_re.reshape(S5_NGB, 1, S5_SL)
    a_im = ab_im.reshape(S5_NGB, 1, S5_SL)
    dsk = d_skip.reshape(S5_NGB, 1, S5_GB * S5_GROUP)
    return wbu, wc, a_re, a_im, dsk


def kernel(x_prompt, x_sample, p_prompt, p_sample, state_gla, state_s5_re, state_s5_im, norm_mix, w_in, gla_w_gate_up, gla_gate_bias, gla_norm, s5_lam_re, s5_lam_im, s5_log_dt, s5_b_re, s5_b_im, s5_c_re, s5_c_im, s5_d, s5_glu_w, s5_glu_b, s5_norm, w_out, norm_ffn, router_group, router_expert, w_gate, w_up, w_down, norm_ple, w_ple, w_ple_gate, norm_final):
    depth = w_in.shape[0]
    assert depth == 1
    i = 0
    bp, lp, _ = x_prompt.shape
    bs, ls, _ = x_sample.shape
    tp, ts = bp * lp, bs * ls
    t_all = tp + ts

    wm = _w_in_prep(w_in[i])
    wgu = jnp.pad(gla_w_gate_up[i], ((0, LANES - GLA_RANK), (0, 0))).astype(BF16)
    gbias = gla_gate_bias[i].reshape(1, QK_W)
    wbu, wc, a_re, a_im, dsk = _s5_params(s5_lam_re[i], s5_lam_im[i], s5_log_dt[i], s5_b_re[i], s5_b_im[i],
                                          s5_c_re[i], s5_c_im[i], s5_d[i])
    glu_w = s5_glu_w[i].astype(BF16)
    wo = w_out[i].astype(BF16)
    wr32 = jnp.pad(jnp.concatenate([router_group[i], router_expert[i]], axis=1),
                   ((0, 0), (0, LANES - N_EGROUPS - N_EXPERTS)))
    wr_hi = wr32.astype(BF16)
    wr = jnp.stack([wr_hi, (wr32 - wr_hi.astype(F32)).astype(BF16)])
    wpg = w_ple_gate[i].astype(BF16)
    wp = w_ple[i].astype(BF16)
    vec = lambda a: a.reshape(1, -1)

    xp = x_prompt.reshape(tp, D_MODEL)
    xs = x_sample.reshape(ts, D_MODEL)

    qp, kp, vp, rp, lap, up = _in_proj(xp, vec(norm_mix[i]), wm, wgu, gbias, BF16)
    qs, ks, vs, rs, las, us = _in_proj(xs, vec(norm_mix[i]), wm, wgu, gbias, BF16)
    ogp, gla_p = _gla_prompt(qp, kp, vp, rp, lap, vec(gla_norm[i]), bp, lp)
    ogs, gla_s = _gla_sample(qs, ks, vs, rs, las, vec(gla_norm[i]), state_gla[i], bs, ls)
    wbu2, wc2, a_re2, a_im2 = _s5_prompt_params(wbu, wc, a_re, a_im, bp)
    ysp, re_p, im_p = _s5_prompt(up.reshape(bp, lp, D_S5), wbu2, wc2, a_re2, a_im2, dsk, tc=256)
    yss, re_s, im_s = _s5(us.reshape(bs, ls, D_S5), wbu, wc, a_re, a_im, dsk,
                          state_s5_re[i].reshape(bs, -1), state_s5_im[i].reshape(bs, -1),
                          nb=32, tc=ls, has_state=True)

    h1, hn_all, rt_all, rtt, cnt = _mix_out(xp, xs, ogp, ogs, ysp.reshape(tp, D_S5), yss.reshape(ts, D_S5),
                                            glu_w, vec(s5_glu_b[i]), vec(s5_norm[i]), wo, vec(norm_ffn[i]), wr)

    plan, dest, zero_blk, n_slots = _moe_plan(rtt, cnt)
    x_slots = _dispatch(dest, zero_blk, hn_all, n_slots)
    y_slots = _moe(x_slots, plan, w_gate[i], w_up[i], w_down[i], n_slots)

    y_p, y_s = _ple_out(dest, h1, rt_all, p_prompt[i].reshape(tp, D_PLE), p_sample[i].reshape(ts, D_PLE),
                        vec(norm_ple[i]), wpg, wp, vec(norm_final), y_slots)

    s5shape = lambda a, b: a.reshape(1, b, S5_GROUPS, S5_STATE)
    return (y_p.reshape(bp, lp, D_MODEL), y_s.reshape(bs, ls, D_MODEL),
            gla_p[None], s5shape(re_p, bp), s5shape(im_p, bp),
            gla_s[None], s5shape(re_s, bs), s5shape(im_s, bs))
```

```python
import functools
import math

import jax
import jax.numpy as jnp
from jax import lax
from jax.experimental import pallas as pl
from jax.experimental.pallas import tpu as pltpu

F32 = jnp.float32
BF16 = jnp.bfloat16

D_MODEL = 2048
D_GLA = 1024
D_S5 = 1024
GLA_HEADS = 4
GLA_DV = 256
GLA_DK = 128
GLA_RANK = 16
GLA_CHUNK = 64
S5_GROUP = 16
S5_GROUPS = 64
S5_STATE = 64
N_EGROUPS = 4
N_EPG = 8
N_EXPERTS = 32
D_EXPERT = 512
D_PLE = 256
EPS = 1e-6

LANES = 128
QK_W = GLA_HEADS * GLA_DK
S5_GB = 8
S5_NGB = S5_GROUPS // S5_GB
S5_SL = S5_GB * S5_STATE
TOK_TILE = 256
MOE_BLK = 128
VMEM_LIMIT = 56 * 1024 * 1024
MOE_VMEM_LIMIT = 60 * 1024 * 1024


def _const_spec(shape):
    nd = len(shape)
    return pl.BlockSpec(shape, lambda *_: (0,) * nd, pipeline_mode=pl.Buffered(1))


def _rms(x, g):
    return x * lax.rsqrt(jnp.mean(x * x, axis=-1, keepdims=True) + EPS) * g


def _dot(a, b):
    return jnp.dot(a, b, preferred_element_type=F32)


def _log_sigmoid(x):
    return -(jnp.maximum(-x, 0.0) + jnp.log1p(jnp.exp(-jnp.abs(x))))


N_QKVR = 2 * QK_W + 2 * D_GLA
W_IN_COLS = N_QKVR + GLA_RANK + D_S5


def _w_in_prep_kernel(w_ref, o_ref):
    o_ref[:, 0:N_QKVR] = w_ref[:, 0:N_QKVR].astype(BF16)
    tail = w_ref[:, N_QKVR:W_IN_COLS]
    o_ref[:, N_QKVR:N_QKVR + D_S5] = tail[:, GLA_RANK:GLA_RANK + D_S5].astype(BF16)
    o_ref[:, N_QKVR + D_S5:N_QKVR + D_S5 + LANES] = tail[:, 0:LANES].astype(BF16)


def _w_in_prep(wi):
    rows = 256
    return pl.pallas_call(
        _w_in_prep_kernel,
        grid=(D_MODEL // rows,),
        in_specs=[pl.BlockSpec((rows, W_IN_COLS), lambda i: (i, 0))],
        out_specs=pl.BlockSpec((rows, N_QKVR + D_S5 + LANES), lambda i: (i, 0)),
        out_shape=jax.ShapeDtypeStruct((D_MODEL, N_QKVR + D_S5 + LANES), BF16),
        compiler_params=pltpu.CompilerParams(dimension_semantics=("arbitrary",), vmem_limit_bytes=VMEM_LIMIT),
        name="w_in_prep",
    )(wi)


def _in_proj_kernel(x_ref, g_ref, wm_ref, wgu_ref, gb_ref,
                    q_ref, k_ref, v_ref, r_ref, la_ref, u_ref):
    hb = _rms(x_ref[...], g_ref[...]).astype(BF16)

    def seg(a, b):
        return _dot(hb, wm_ref[:, a:b])

    q_ref[...] = (seg(0, QK_W) * (GLA_DK ** -0.5)).astype(q_ref.dtype)
    k_ref[...] = seg(QK_W, 2 * QK_W).astype(k_ref.dtype)
    v_ref[...] = seg(2 * QK_W, 2 * QK_W + D_GLA).astype(v_ref.dtype)
    r_ref[...] = seg(2 * QK_W + D_GLA, N_QKVR).astype(r_ref.dtype)
    u_ref[...] = seg(N_QKVR, N_QKVR + D_S5)
    zg = seg(N_QKVR + D_S5, N_QKVR + D_S5 + LANES)
    xg = _dot(zg.astype(BF16), wgu_ref[...]) + gb_ref[...]
    la_ref[...] = _log_sigmoid(xg) * (1.0 / 16.0)


def _in_proj(x2d, g, wm, wgu, gbias, act_dtype):
    t = x2d.shape[0]
    tm = TOK_TILE
    row = lambda w: pl.BlockSpec((tm, w), lambda i: (i, 0))
    return pl.pallas_call(
        _in_proj_kernel,
        grid=(t // tm,),
        in_specs=[row(D_MODEL), _const_spec((1, D_MODEL)), _const_spec(wm.shape),
                  _const_spec(wgu.shape), _const_spec((1, QK_W))],
        out_specs=[row(QK_W), row(QK_W), row(D_GLA), row(D_GLA), row(QK_W), row(D_S5)],
        out_shape=[jax.ShapeDtypeStruct((t, QK_W), act_dtype), jax.ShapeDtypeStruct((t, QK_W), act_dtype),
                   jax.ShapeDtypeStruct((t, D_GLA), act_dtype), jax.ShapeDtypeStruct((t, D_GLA), act_dtype),
                   jax.ShapeDtypeStruct((t, QK_W), F32), jax.ShapeDtypeStruct((t, D_S5), F32)],
        compiler_params=pltpu.CompilerParams(dimension_semantics=("arbitrary",), vmem_limit_bytes=VMEM_LIMIT),
        name="in_proj",
    )(x2d, g, wm, wgu, gbias)


_NT = (((1,), (1,)), ((), ()))
_TN = (((0,), (0,)), ((), ()))


def _gla_pre(q, k, la, c):
    r = q.shape[0]
    shift = int(math.log2(c))
    ri = lax.broadcasted_iota(jnp.int32, (r, r), 0)
    si = lax.broadcasted_iota(jnp.int32, (r, r), 1)
    mask = ((ri >> shift) == (si >> shift)) & (ri >= si)
    tri = jnp.where(mask, 1.0, 0.0).astype(BF16)
    hi = la.astype(BF16)
    r1 = la - hi.astype(F32)
    mid = r1.astype(BF16)
    lo = (r1 - mid.astype(F32)).astype(BF16)
    cum = _dot(tri, hi) + _dot(tri, mid) + _dot(tri, lo)
    last = jnp.concatenate([jnp.broadcast_to(cum[(i + 1) * c - 1:(i + 1) * c, :], (c, cum.shape[1]))
                            for i in range(r // c)], axis=0)
    qe = (q * jnp.exp(cum)).astype(BF16)
    ke = (k * jnp.exp(-cum)).astype(BF16)
    kd = (k * jnp.exp(last - cum)).astype(BF16)
    return qe, ke, kd, cum, mask


def _gla_intra(qe, ke, v, mask):
    sc = lax.dot_general(qe, ke, _NT, preferred_element_type=F32)
    return _dot(jnp.where(mask, sc, 0.0).astype(BF16), v)


def _gla_finish(o, r, g):
    rf = r.astype(F32)
    return _rms(o, g) * (rf * jax.nn.sigmoid(rf))


GLA_SAFE_LOG_DECAY = -60.0


def _gla_token_step(t, rows, q, k, v, a_all, st_ref, oacc_ref):
    m = rows == t
    a = jnp.sum(jnp.where(m, a_all, 0.0), axis=0, keepdims=True)
    kt = jnp.where(m, k, 0.0).astype(BF16)
    qt = jnp.where(m, q, 0.0).astype(BF16)
    vt = jnp.where(m, v, jnp.zeros_like(v))
    st = st_ref[...] * a + lax.dot_general(vt, kt, _TN, preferred_element_type=F32)
    st_ref[...] = st
    oacc_ref[...] += lax.dot_general(qt, st.astype(BF16), _NT, preferred_element_type=F32)


def _gla_prompt_kernel(q_ref, k_ref, v_ref, r_ref, la_ref, g_ref, o_ref, sfin_ref, st_ref, oacc_ref,
                       *, n_chunks):
    j = pl.program_id(1)

    @pl.when(j == 0)
    def _():
        st_ref[...] = jnp.zeros_like(st_ref)

    c = GLA_CHUNK
    qe, ke, kd, cum, mask = _gla_pre(q_ref[...].astype(F32), k_ref[...].astype(F32), la_ref[...], c)
    g = g_ref[...]
    safe = jnp.min(cum) >= GLA_SAFE_LOG_DECAY

    @pl.when(safe)
    def _():
        for h in range(GLA_HEADS):
            kc = slice(h * GLA_DK, (h + 1) * GLA_DK)
            vc = slice(h * GLA_DV, (h + 1) * GLA_DV)
            v = v_ref[:, vc]
            qe_h, kd_h = qe[:, kc], kd[:, kc]
            o = _gla_intra(qe_h, ke[:, kc], v, mask)
            st = st_ref[h]
            inter = []
            for ci in range(n_chunks):
                rows = slice(ci * c, (ci + 1) * c)
                inter.append(lax.dot_general(qe_h[rows], st.astype(BF16), _NT, preferred_element_type=F32))
                dec = jnp.exp(cum[(ci + 1) * c - 1:(ci + 1) * c, kc])
                st = st * dec + lax.dot_general(v[rows], kd_h[rows], _TN, preferred_element_type=F32)
            st_ref[h] = st
            o = o + jnp.concatenate(inter, axis=0)
            o_ref[:, vc] = _gla_finish(o, r_ref[:, vc], g).astype(o_ref.dtype)

    @pl.when(jnp.logical_not(safe))
    def _():
        n_rows = n_chunks * c
        rows = lax.broadcasted_iota(jnp.int32, (n_rows, 1), 0)
        for h in range(GLA_HEADS):
            kc = slice(h * GLA_DK, (h + 1) * GLA_DK)
            vc = slice(h * GLA_DV, (h + 1) * GLA_DV)
            q, k, v = q_ref[:, kc].astype(F32), k_ref[:, kc].astype(F32), v_ref[:, vc]
            a_all = jnp.exp(la_ref[:, kc])
            oacc_ref[...] = jnp.zeros_like(oacc_ref)

            def step(t, carry):
                _gla_token_step(t, rows, q, k, v, a_all, st_ref.at[h], oacc_ref)
                return carry

            lax.fori_loop(0, n_rows, step, 0)
            o_ref[:, vc] = _gla_finish(oacc_ref[...], r_ref[:, vc], g).astype(o_ref.dtype)

    @pl.when(j == pl.num_programs(1) - 1)
    def _():
        for h in range(GLA_HEADS):
            sfin_ref[0, h] = st_ref[h].T


def _gla_prompt(q, k, v, r, la, g, batch, seq):
    rb = 4 * GLA_CHUNK
    nj = seq // rb
    row = lambda w: pl.BlockSpec((rb, w), lambda b, j: (b * nj + j, 0))
    return pl.pallas_call(
        functools.partial(_gla_prompt_kernel, n_chunks=rb // GLA_CHUNK),
        grid=(batch, nj),
        in_specs=[row(QK_W), row(QK_W), row(D_GLA), row(D_GLA), row(QK_W), _const_spec((1, GLA_DV))],
        out_specs=[row(D_GLA),
                   pl.BlockSpec((1, GLA_HEADS, GLA_DK, GLA_DV), lambda b, j: (b, 0, 0, 0))],
        out_shape=[jax.ShapeDtypeStruct((batch * seq, D_GLA), BF16),
                   jax.ShapeDtypeStruct((batch, GLA_HEADS, GLA_DK, GLA_DV), F32)],
        scratch_shapes=[pltpu.VMEM((GLA_HEADS, GLA_DV, GLA_DK), F32), pltpu.VMEM((rb, GLA_DV), F32)],
        compiler_params=pltpu.CompilerParams(dimension_semantics=("arbitrary", "arbitrary"),
                                             vmem_limit_bytes=VMEM_LIMIT),
        name="gla_prompt",
    )(q, k, v, r, la, g)


def _gla_sample_kernel(q_ref, k_ref, v_ref, r_ref, la_ref, g_ref, s0_ref, o_ref, sfin_ref, st_ref, oacc_ref,
                       *, n_seq, seq):
    qe, ke, kd, cum, mask = _gla_pre(q_ref[...].astype(F32), k_ref[...].astype(F32), la_ref[...], seq)
    g = g_ref[...]
    safe = jnp.min(cum) >= GLA_SAFE_LOG_DECAY

    @pl.when(safe)
    def _():
        lasts = jnp.concatenate([cum[(s + 1) * seq - 1:(s + 1) * seq, :] for s in range(n_seq)]
                                + [jnp.zeros((GLA_DK - n_seq, cum.shape[1]), F32)], axis=0)
        pair = 2 * seq
        upper = lax.broadcasted_iota(jnp.int32, (pair, GLA_DK), 0) < seq
        for h in range(GLA_HEADS):
            kc = slice(h * GLA_DK, (h + 1) * GLA_DK)
            vc = slice(h * GLA_DV, (h + 1) * GLA_DV)
            v = v_ref[:, vc]
            qe_h, kd_h = qe[:, kc], kd[:, kc]
            o = _gla_intra(qe_h, ke[:, kc], v, mask)
            dec_t = jnp.exp(lasts[:, kc].T)
            inter = []
            for p in range(n_seq // 2):
                rows = slice(p * pair, (p + 1) * pair)
                qe_p, kd_p, v_p = qe_h[rows], kd_h[rows], v[rows]
                for half in range(2):
                    s = 2 * p + half
                    s0 = s0_ref[s, h]
                    o_s = _dot(qe_p, s0.astype(BF16))
                    inter.append(o_s[half * seq:(half + 1) * seq])
                    kd_s = jnp.where(upper if half == 0 else ~upper, kd_p, jnp.zeros_like(kd_p))
                    dec = jnp.broadcast_to(dec_t[:, s:s + 1], (GLA_DK, GLA_DV))
                    sfin_ref[s, h] = s0 * dec + lax.dot_general(kd_s, v_p, _TN, preferred_element_type=F32)
            o = o + jnp.concatenate(inter, axis=0)
            o_ref[:, vc] = _gla_finish(o, r_ref[:, vc], g).astype(o_ref.dtype)

    @pl.when(jnp.logical_not(safe))
    def _():
        n_rows = n_seq * seq
        rows = lax.broadcasted_iota(jnp.int32, (n_rows, 1), 0)
        for h in range(GLA_HEADS):
            kc = slice(h * GLA_DK, (h + 1) * GLA_DK)
            vc = slice(h * GLA_DV, (h + 1) * GLA_DV)
            q, k, v = q_ref[:, kc].astype(F32), k_ref[:, kc].astype(F32), v_ref[:, vc]
            a_all = jnp.exp(la_ref[:, kc])
            oacc_ref[...] = jnp.zeros_like(oacc_ref)

            def step(t, carry):
                s = t // seq

                @pl.when(t % seq == 0)
                def _():
                    st_ref[...] = s0_ref[s, h].T

                _gla_token_step(t, rows, q, k, v, a_all, st_ref, oacc_ref)

                @pl.when(t % seq == seq - 1)
                def _():
                    sfin_ref[s, h] = st_ref[...].T

                return carry

            lax.fori_loop(0, n_rows, step, 0)
            o_ref[:, vc] = _gla_finish(oacc_ref[...], r_ref[:, vc], g).astype(o_ref.dtype)


def _gla_sample(q, k, v, r, la, g, s0, batch, seq):
    ns = 16
    rb = ns * seq
    row = lambda w: pl.BlockSpec((rb, w), lambda i: (i, 0))
    st = pl.BlockSpec((ns, GLA_HEADS, GLA_DK, GLA_DV), lambda i: (i, 0, 0, 0))
    return pl.pallas_call(
        functools.partial(_gla_sample_kernel, n_seq=ns, seq=seq),
        grid=(batch // ns,),
        in_specs=[row(QK_W), row(QK_W), row(D_GLA), row(D_GLA), row(QK_W), _const_spec((1, GLA_DV)), st],
        out_specs=[row(D_GLA), st],
        out_shape=[jax.ShapeDtypeStruct((batch * seq, D_GLA), BF16),
                   jax.ShapeDtypeStruct((batch, GLA_HEADS, GLA_DK, GLA_DV), F32)],
        scratch_shapes=[pltpu.VMEM((GLA_DV, GLA_DK), F32), pltpu.VMEM((rb, GLA_DV), F32)],
        compiler_params=pltpu.CompilerParams(dimension_semantics=("arbitrary",), vmem_limit_bytes=VMEM_LIMIT),
        name="gla_sample",
    )(q, k, v, r, la, g, s0)


def _s5_kernel(u_ref, wbu_ref, wc_ref, are_ref, aim_ref, d_ref, h0r_ref, h0i_ref,
               y_ref, sre_ref, sim_ref, bu_ref, xs_ref, car_ref, *, nb, tc, has_state):
    j = pl.program_id(2)

    @pl.when(j == 0)
    def _():
        if has_state:
            car_ref[0] = h0r_ref[...]
            car_ref[1] = h0i_ref[...]
        else:
            car_ref[...] = jnp.zeros_like(car_ref)

    u2 = u_ref[...].reshape(nb * tc, LANES)
    ub = u2.astype(BF16)
    nl = S5_SL // LANES
    for l in range(2 * nl):
        bu_ref[l] = _dot(ub, wbu_ref[0, :, l * LANES:(l + 1) * LANES])
    a_r = [jnp.broadcast_to(are_ref[0, :, l * LANES:(l + 1) * LANES], (nb, LANES)) for l in range(nl)]
    a_i = [jnp.broadcast_to(aim_ref[0, :, l * LANES:(l + 1) * LANES], (nb, LANES)) for l in range(nl)]

    def step(t, carry):
        rows = pl.ds(t, nb, stride=tc)
        out = []
        for l in range(nl):
            xr, xi = carry[2 * l], carry[2 * l + 1]
            nr = a_r[l] * xr - a_i[l] * xi + bu_ref[l, rows, :]
            ni = a_r[l] * xi + a_i[l] * xr + bu_ref[nl + l, rows, :]
            xs_ref[l, rows, :] = nr
            xs_ref[nl + l, rows, :] = ni
            out += [nr, ni]
        return tuple(out)

    init = []
    for l in range(nl):
        init += [car_ref[0, :, l * LANES:(l + 1) * LANES], car_ref[1, :, l * LANES:(l + 1) * LANES]]
    fin = lax.fori_loop(0, tc, step, tuple(init), unroll=8)
    xr = jnp.concatenate([fin[2 * l] for l in range(nl)], axis=1)
    xi = jnp.concatenate([fin[2 * l + 1] for l in range(nl)], axis=1)
    car_ref[0] = xr
    car_ref[1] = xi
    y = d_ref[0] * u2
    for l in range(2 * nl):
        y = y + _dot(xs_ref[l].astype(BF16), wc_ref[0, l * LANES:(l + 1) * LANES, :])
    y_ref[...] = y.reshape(nb, tc, LANES)

    @pl.when(j == pl.num_programs(2) - 1)
    def _():
        sre_ref[...] = xr
        sim_ref[...] = xi


def _s5(u3d, wbu, wc, a_re, a_im, dsk, h0r, h0i, nb, tc, has_state):
    batch, seq, _ = u3d.shape
    grid = (S5_NGB, batch // nb, seq // tc)
    st = pl.BlockSpec((nb, S5_SL), lambda g, b, j: (b, g))
    par = lambda w: pl.BlockSpec((1, 1, w), lambda g, b, j: (g, 0, 0))
    ublk = pl.BlockSpec((nb, tc, LANES), lambda g, b, j: (b, j, g))
    return pl.pallas_call(
        functools.partial(_s5_kernel, nb=nb, tc=tc, has_state=has_state),
        grid=grid,
        in_specs=[ublk,
                  pl.BlockSpec((1, LANES, 2 * S5_SL), lambda g, b, j: (g, 0, 0)),
                  pl.BlockSpec((1, 2 * S5_SL, LANES), lambda g, b, j: (g, 0, 0)),
                  par(S5_SL), par(S5_SL), par(LANES), st, st],
        out_specs=[ublk, st, st],
        out_shape=[jax.ShapeDtypeStruct(u3d.shape, F32),
                   jax.ShapeDtypeStruct((batch, S5_GROUPS * S5_STATE), F32),
                   jax.ShapeDtypeStruct((batch, S5_GROUPS * S5_STATE), F32)],
        scratch_shapes=[pltpu.VMEM((2 * S5_SL // LANES, nb * tc, LANES), F32),
                        pltpu.VMEM((2 * S5_SL // LANES, nb * tc, LANES), F32),
                        pltpu.VMEM((2, nb, S5_SL), F32)],
        compiler_params=pltpu.CompilerParams(dimension_semantics=("arbitrary",) * 3,
                                             vmem_limit_bytes=VMEM_LIMIT),
        name="s5_state" if has_state else "s5_zero",
    )(u3d, wbu, wc, a_re, a_im, dsk, h0r, h0i)


S5_J = 2
S5_HL = S5_SL // S5_J


def _s5_prompt_kernel(u_ref, wbu_ref, wc_ref, are_ref, aim_ref, d_ref, y_ref, sre_ref, sim_ref,
                      u2_ref, lhs_ref, bu_ref, xs_ref, y2_ref, yo_ref, car_ref, *, nb, tc):
    g = pl.program_id(0)
    j = pl.program_id(1)
    rows = nb * S5_J

    @pl.when((g == 0) & (j == 0))
    def _():
        lhs_ref[...] = jnp.zeros_like(lhs_ref)

    @pl.when(j == 0)
    def _():
        car_ref[...] = jnp.zeros_like(car_ref)

    u2 = u_ref[...].reshape(nb * tc, LANES)
    u2_ref[...] = u2

    def build(t, c):
        u4 = u2_ref[pl.ds(t, nb, stride=tc), :]
        for jj in range(S5_J):
            lhs_ref[t, jj * nb:(jj + 1) * nb, jj * LANES:(jj + 1) * LANES] = u4
        return c

    lax.fori_loop(0, tc, build, 0, unroll=8)
    lhs = lhs_ref[...].reshape(tc * rows, S5_J * LANES).astype(BF16)
    bu_ref[...] = _dot(lhs, wbu_ref[0]).reshape(tc, rows, 2 * S5_HL)
    ar = are_ref[0]
    ai = aim_ref[0]

    def step(t, carry):
        xr, xi = carry
        tile = bu_ref[t]
        nr = ar * xr - ai * xi + tile[:, 0:S5_HL]
        ni = ar * xi + ai * xr + tile[:, S5_HL:2 * S5_HL]
        xs_ref[t] = jnp.concatenate([nr, ni], axis=1)
        return nr, ni

    xr, xi = lax.fori_loop(0, tc, step, (car_ref[0], car_ref[1]), unroll=8)
    car_ref[0] = xr
    car_ref[1] = xi
    xs = xs_ref[...].reshape(tc * rows, 2 * S5_HL).astype(BF16)
    y2_ref[...] = _dot(xs, wc_ref[0]).reshape(tc, rows, S5_J * LANES)
    first_half = lax.broadcasted_iota(jnp.int32, (rows, LANES), 0) < nb

    def unperm(t, c):
        t2 = y2_ref[t]
        part = jnp.where(first_half, t2[:, 0:LANES], t2[:, LANES:2 * LANES])
        tot = part + pltpu.roll(part, nb, axis=0)
        yo_ref[pl.ds(t, nb, stride=tc), :] = tot[0:nb]
        return c

    lax.fori_loop(0, tc, unperm, 0, unroll=8)
    y_ref[...] = (yo_ref[...] + d_ref[0] * u2).reshape(nb, tc, LANES)

    @pl.when(j == pl.num_programs(1) - 1)
    def _():
        sre_ref[...] = jnp.concatenate([xr[jj * nb:(jj + 1) * nb] for jj in range(S5_J)], axis=1)
        sim_ref[...] = jnp.concatenate([xi[jj * nb:(jj + 1) * nb] for jj in range(S5_J)], axis=1)


def _s5_prompt(u3d, wbu2, wc2, a_re2, a_im2, dsk, tc):
    nb, seq, _ = u3d.shape
    rows = nb * S5_J
    assert rows == 8
    st = pl.BlockSpec((nb, S5_SL), lambda g, j: (0, g))
    gblk = lambda s: pl.BlockSpec((1,) + s, lambda g, j: (g, 0, 0))
    ublk = pl.BlockSpec((nb, tc, LANES), lambda g, j: (0, j, g))
    return pl.pallas_call(
        functools.partial(_s5_prompt_kernel, nb=nb, tc=tc),
        grid=(S5_NGB, seq // tc),
        in_specs=[ublk, gblk((S5_J * LANES, 2 * S5_HL)), gblk((2 * S5_HL, S5_J * LANES)),
                  gblk((rows, S5_HL)), gblk((rows, S5_HL)), gblk((1, LANES))],
        out_specs=[ublk, st, st],
        out_shape=[jax.ShapeDtypeStruct(u3d.shape, F32),
                   jax.ShapeDtypeStruct((nb, S5_GROUPS * S5_STATE), F32),
                   jax.ShapeDtypeStruct((nb, S5_GROUPS * S5_STATE), F32)],
        scratch_shapes=[pltpu.VMEM((nb * tc, LANES), F32),
                        pltpu.VMEM((tc, rows, S5_J * LANES), F32),
                        pltpu.VMEM((tc, rows, 2 * S5_HL), F32),
                        pltpu.VMEM((tc, rows, 2 * S5_HL), F32),
                        pltpu.VMEM((tc, rows, S5_J * LANES), F32),
                        pltpu.VMEM((nb * tc, LANES), F32),
                        pltpu.VMEM((2, rows, S5_HL), F32)],
        compiler_params=pltpu.CompilerParams(dimension_semantics=("arbitrary",) * 2,
                                             vmem_limit_bytes=VMEM_LIMIT),
        name="s5_prompt",
    )(u3d, wbu2, wc2, a_re2, a_im2, dsk)


def _s5_prompt_params(wbu, wc, a_re, a_im, nb):
    h = S5_HL
    top = jnp.concatenate([wbu[:, :, 0:h], wbu[:, :, S5_SL:S5_SL + h]], axis=2)
    bot = jnp.concatenate([wbu[:, :, h:2 * h], wbu[:, :, S5_SL + h:S5_SL + 2 * h]], axis=2)
    wbu2 = jnp.concatenate([top, bot], axis=1)
    wc_j = [jnp.concatenate([wc[:, jj * h:(jj + 1) * h, :], wc[:, S5_SL + jj * h:S5_SL + (jj + 1) * h, :]],
                            axis=1) for jj in range(S5_J)]
    wc2 = jnp.concatenate(wc_j, axis=2)
    tile = lambda a: jnp.repeat(a.reshape(S5_NGB, S5_J, h), nb, axis=1)
    return wbu2, wc2, tile(a_re), tile(a_im)


def _mix_out_kernel(xp_ref, xs_ref, ogp_ref, ogs_ref, ysp_ref, yss_ref, glu_w_ref, glu_b_ref, s5n_ref, wo_ref,
                    nffn_ref, wr_ref, h1_ref, hn_ref, rt_ref, rtt_ref, cnt_ref, *, n_prompt_tiles):
    is_p = pl.program_id(0) < n_prompt_tiles
    x = jnp.where(is_p, xp_ref[...], xs_ref[...])
    og = jnp.where(is_p, ogp_ref[...], ogs_ref[...])
    y = jax.nn.gelu(jnp.where(is_p, ysp_ref[...], yss_ref[...]))
    y = y * jax.nn.sigmoid(_dot(y.astype(BF16), glu_w_ref[...]) + glu_b_ref[...])
    yn = _rms(y, s5n_ref[...]).astype(BF16)
    mix = _dot(og, wo_ref[0:D_GLA, :]) + _dot(yn, wo_ref[D_GLA:D_GLA + D_S5, :])
    h1 = x + mix
    h1_ref[...] = h1
    hn = _rms(h1, nffn_ref[...])
    hn_ref[...] = _pack_bf16_pair(hn[:, 0:HALF], hn[:, HALF:D_MODEL])
    hn_hi = hn.astype(BF16)
    hn_lo = (hn - hn_hi.astype(F32)).astype(BF16)
    logits = _dot(hn_hi, wr_ref[0]) + _dot(hn_hi, wr_ref[1]) + _dot(hn_lo, wr_ref[0])
    rt = _route(logits)

    @pl.when(pl.program_id(0) == 0)
    def _():
        cnt_ref[...] = jnp.zeros_like(cnt_ref)

    tm = rt.shape[0]
    lane = lax.broadcasted_iota(jnp.int32, rt.shape, 1).astype(F32)
    oh0 = lane == rt[:, 0:1]
    oh1 = lane == rt[:, 1:2]
    both = jnp.where(oh0 | oh1, 1.0, 0.0)
    ri = lax.broadcasted_iota(jnp.int32, (tm, tm), 0)
    ci = lax.broadcasted_iota(jnp.int32, (tm, tm), 1)
    before = _dot(jnp.where(ri > ci, 1.0, 0.0).astype(BF16), both.astype(BF16)) + cnt_ref[...]
    rank0 = jnp.sum(jnp.where(oh0, before, 0.0), axis=-1, keepdims=True)
    rank1 = jnp.sum(jnp.where(oh1, before, 0.0), axis=-1, keepdims=True)
    cnt_ref[...] += jnp.sum(both, axis=0, keepdims=True)
    rt = jnp.where(lane == 4.0, rank0, jnp.where(lane == 5.0, rank1, rt))
    rt_ref[...] = rt
    rtt_ref[...] = rt.T[0:8, :]


def _route(logits):
    col = lax.broadcasted_iota(jnp.int32, logits.shape, 1)
    colf = col.astype(F32)
    neg = -jnp.inf

    def first_argmax(vals):
        m = jnp.max(vals, axis=-1, keepdims=True)
        idx = jnp.min(jnp.where(vals == m, colf, float(LANES)), axis=-1, keepdims=True)
        return m, idx

    lg = jnp.where(col < N_EGROUPS, logits, neg)
    gmax, gsel = first_argmax(lg)
    p_g = 1.0 / jnp.sum(jnp.exp(lg - gmax), axis=-1, keepdims=True)
    ecol = col - N_EGROUPS
    egrp = (ecol >> 3).astype(F32)
    in_group = (ecol >= 0) & (ecol < N_EXPERTS) & (egrp == gsel)
    le = jnp.where(in_group, logits, neg)
    m1, i1 = first_argmax(le)
    le2 = jnp.where(colf == i1, neg, le)
    m2, i2 = first_argmax(le2)
    e2 = jnp.exp(m2 - m1)
    den = 1.0 + e2
    w1 = p_g * (1.0 / den)
    w2 = p_g * (e2 / den)
    e1f = i1 - float(N_EGROUPS)
    e2f = i2 - float(N_EGROUPS)
    out = jnp.where(col == 0, e1f, jnp.where(col == 1, e2f, jnp.where(col == 2, w1, jnp.where(col == 3, w2, 0.0))))
    return out


def _mix_out(xp, xs, ogp, ogs, ysp, yss, glu_w, glu_b, s5n, wo, nffn, wr):
    tm = TOK_TILE
    npt, nst = xp.shape[0] // tm, xs.shape[0] // tm
    t = (npt + nst) * tm
    row = lambda w: pl.BlockSpec((tm, w), lambda i: (i, 0))
    prow = lambda w: pl.BlockSpec((tm, w), lambda i: (jnp.minimum(i, npt - 1), 0))
    srow = lambda w: pl.BlockSpec((tm, w), lambda i: (jnp.maximum(i - npt, 0), 0))
    return pl.pallas_call(
        functools.partial(_mix_out_kernel, n_prompt_tiles=npt),
        grid=(npt + nst,),
        in_specs=[prow(D_MODEL), srow(D_MODEL), prow(D_GLA), srow(D_GLA), prow(D_S5), srow(D_S5),
                  _const_spec(glu_w.shape), _const_spec((1, D_S5)), _const_spec((1, D_S5)),
                  _const_spec(wo.shape), _const_spec((1, D_MODEL)), _const_spec(wr.shape)],
        out_specs=[row(D_MODEL), row(HALF), row(LANES), pl.BlockSpec((8, tm), lambda i: (0, i)),
                   pl.BlockSpec((1, LANES), lambda i: (0, 0))],
        out_shape=[jax.ShapeDtypeStruct((t, D_MODEL), F32), jax.ShapeDtypeStruct((t, HALF), jnp.uint32),
                   jax.ShapeDtypeStruct((t, LANES), F32), jax.ShapeDtypeStruct((8, t), F32),
                   jax.ShapeDtypeStruct((1, LANES), F32)],
        compiler_params=pltpu.CompilerParams(dimension_semantics=("arbitrary",), vmem_limit_bytes=VMEM_LIMIT),
        name="mix_out",
    )(xp, xs, ogp, ogs, ysp, yss, glu_w, glu_b, s5n, wo, nffn, wr)


DMA_UNROLL = 8


MOE_SB = 6
HALF = D_MODEL // 2


def _pack_bf16_pair(lo, hi):
    def bits(x):
        b = pltpu.bitcast(x, jnp.uint32)
        return (b + jnp.uint32(0x7FFF) + ((b >> 16) & jnp.uint32(1))) >> 16
    return bits(lo) | (bits(hi) << 16)


def _unpack_bf16_pair(u):
    return (pltpu.bitcast(u << 16, F32), pltpu.bitcast(u & jnp.uint32(0xFFFF0000), F32))


def _moe_kernel(se_ref, sm_ref, sn_ref, sbase_ref, xs_hbm, wg_ref, wu_ref, wd_ref, out_hbm,
                xb_ref, yb_ref, wgb_ref, wub_ref, wdb_ref, gsem, ssem):
    b = pl.program_id(0)
    nb = pl.num_programs(0)
    n = sn_ref[b]
    slot = lax.rem(b, 2)

    def issue_gather(blk, sl):
        base = pl.multiple_of(sbase_ref[blk], MOE_BLK)
        for m in range(1, MOE_SB + 1):
            @pl.when((sn_ref[blk] > 0) & (sm_ref[blk] == m))
            def _():
                rows = m * MOE_BLK
                pltpu.make_async_copy(xs_hbm.at[pl.ds(base, rows), :], xb_ref.at[sl, pl.ds(0, rows), :],
                                      gsem.at[sl]).start()

    def wait_gather(blocks, sl):
        rows = pl.ds(0, pl.multiple_of(blocks * MOE_BLK, MOE_BLK))
        pltpu.make_async_copy(xs_hbm.at[rows, :], xb_ref.at[sl, rows, :], gsem.at[sl]).wait()

    m_cur = sm_ref[b]
    m_prev = sm_ref[jnp.maximum(b - 1, 0)]
    has_prev = (b > 0) & (m_prev > 0)

    def out_copy(rows):
        dst = pl.ds(pl.multiple_of(sbase_ref[b], MOE_BLK), rows)
        return pltpu.make_async_copy(yb_ref.at[pl.ds(0, rows), :], out_hbm.at[dst, :], ssem.at[0])

    def wait_prev_out():
        rows = pl.ds(0, pl.multiple_of(m_prev * MOE_BLK, MOE_BLK))
        pltpu.make_async_copy(yb_ref.at[rows, :], out_hbm.at[rows, :], ssem.at[0]).wait()

    def write_out(rows):
        out_copy(rows).start()

        @pl.when(b == nb - 1)
        def _():
            out_copy(rows).wait()

    @pl.when(b == 0)
    def _():
        issue_gather(0, 0)

    @pl.when(b + 1 < nb)
    def _():
        issue_gather(b + 1, 1 - slot)

    def compute(rows):
        x_lo, x_hi = _unpack_bf16_pair(xb_ref[slot, 0:rows, :])
        x_lo, x_hi = x_lo.astype(BF16), x_hi.astype(BF16)
        gate = _dot(x_lo, wgb_ref[0:HALF, :]) + _dot(x_hi, wgb_ref[HALF:D_MODEL, :])
        up = _dot(x_lo, wub_ref[0:HALF, :]) + _dot(x_hi, wub_ref[HALF:D_MODEL, :])
        hid = (gate * jax.nn.sigmoid(gate) * up).astype(BF16)

        @pl.when(has_prev)
        def _():
            wait_prev_out()

        yb_ref[0:rows, :] = _pack_bf16_pair(_dot(hid, wdb_ref[:, 0:HALF]), _dot(hid, wdb_ref[:, HALF:D_MODEL]))
        write_out(rows)

    @pl.when(n > 0)
    def _():
        prev_e = se_ref[jnp.maximum(b - 1, 0)]

        @pl.when((b == 0) | (prev_e != se_ref[b]))
        def _():
            wgb_ref[...] = wg_ref[...].astype(BF16)
            wub_ref[...] = wu_ref[...].astype(BF16)
            wdb_ref[...] = wd_ref[...].astype(BF16)

        wait_gather(m_cur, slot)
        for m in range(1, MOE_SB + 1):
            @pl.when(m_cur == m)
            def _():
                compute(m * MOE_BLK)

    @pl.when((n == 0) & (m_cur > 0))
    def _():
        @pl.when(has_prev)
        def _():
            wait_prev_out()

        yb_ref[...] = jnp.zeros_like(yb_ref)
        for m in range(1, MOE_SB + 1):
            @pl.when(m_cur == m)
            def _():
                write_out(m * MOE_BLK)

    @pl.when((m_cur == 0) & has_prev)
    def _():
        wait_prev_out()


def _moe(x_slots, plan, w_gate, w_up, w_down, n_out_rows):
    sb_e, sb_m, sb_n, sb_base = plan
    sb_rows = MOE_SB * MOE_BLK
    wspec = lambda s: pl.BlockSpec((None,) + s, lambda b, se, *_: (se[b], 0, 0))
    grid_spec = pltpu.PrefetchScalarGridSpec(
        num_scalar_prefetch=4,
        grid=(sb_e.shape[0],),
        in_specs=[pl.BlockSpec(memory_space=pl.ANY),
                  wspec((D_MODEL, D_EXPERT)), wspec((D_MODEL, D_EXPERT)), wspec((D_EXPERT, D_MODEL))],
        out_specs=pl.BlockSpec(memory_space=pl.ANY),
        scratch_shapes=[pltpu.VMEM((2, sb_rows, HALF), jnp.uint32), pltpu.VMEM((sb_rows, HALF), jnp.uint32),
                        pltpu.VMEM((D_MODEL, D_EXPERT), BF16), pltpu.VMEM((D_MODEL, D_EXPERT), BF16),
                        pltpu.VMEM((D_EXPERT, D_MODEL), BF16), pltpu.SemaphoreType.DMA((2,)),
                        pltpu.SemaphoreType.DMA((1,))],
    )
    return pl.pallas_call(
        _moe_kernel,
        grid_spec=grid_spec,
        out_shape=jax.ShapeDtypeStruct((n_out_rows, HALF), jnp.uint32),
        compiler_params=pltpu.CompilerParams(dimension_semantics=("arbitrary",),
                                             vmem_limit_bytes=MOE_VMEM_LIMIT),
        name="moe_experts",
    )(sb_e, sb_m, sb_n, sb_base, x_slots, w_gate, w_up, w_down)


def _dispatch_kernel(dest_ref, zb_ref, hn_ref, out_hbm, buf_ref, zero_ref, sem, zsem):
    i = pl.program_id(0)
    nt = pl.num_programs(0)
    tm = hn_ref.shape[0]
    t_all = nt * tm
    slot = lax.rem(i, 2)
    nblk = zb_ref.shape[0]

    def zero_copy(blk):
        dst = pl.ds(pl.multiple_of(blk * MOE_BLK, MOE_BLK), MOE_BLK)
        return pltpu.make_async_copy(zero_ref, out_hbm.at[dst, :], zsem.at[0])

    @pl.when(i == 0)
    def _():
        zero_ref[...] = jnp.zeros_like(zero_ref)

        def start(blk, c):
            @pl.when(zb_ref[blk] == 1)
            def _():
                zero_copy(blk).start()
            return c

        def wait(blk, c):
            @pl.when(zb_ref[blk] == 1)
            def _():
                zero_copy(blk).wait()
            return c

        lax.fori_loop(0, nblk, start, 0)
        lax.fori_loop(0, nblk, wait, 0)

    def wait_rows(sl):
        for k in range(2):
            pltpu.make_async_copy(buf_ref.at[sl], out_hbm.at[pl.ds(0, tm), :], sem.at[sl]).wait()

    @pl.when(i >= 2)
    def _():
        wait_rows(slot)

    buf_ref[slot] = hn_ref[...]
    base = i * tm

    def grp(g, c):
        for j in range(DMA_UNROLL):
            row = g * (DMA_UNROLL // 2) + j // 2
            d = dest_ref[(j % 2) * t_all + base + row]
            pltpu.make_async_copy(buf_ref.at[slot, pl.ds(row, 1), :], out_hbm.at[pl.ds(d, 1), :],
                                  sem.at[slot]).start()
        return c

    lax.fori_loop(0, 2 * tm // DMA_UNROLL, grp, 0)

    @pl.when(i == nt - 1)
    def _():
        wait_rows(slot)

        @pl.when(nt > 1)
        def _():
            wait_rows(1 - slot)


def _dispatch(dest, zero_blk, hn_pk, n_slots):
    tm = TOK_TILE
    grid_spec = pltpu.PrefetchScalarGridSpec(
        num_scalar_prefetch=2,
        grid=(hn_pk.shape[0] // tm,),
        in_specs=[pl.BlockSpec((tm, HALF), lambda i, d, z: (i, 0))],
        out_specs=pl.BlockSpec(memory_space=pl.ANY),
        scratch_shapes=[pltpu.VMEM((2, tm, HALF), jnp.uint32), pltpu.VMEM((MOE_BLK, HALF), jnp.uint32),
                        pltpu.SemaphoreType.DMA((2,)), pltpu.SemaphoreType.DMA((1,))],
    )
    return pl.pallas_call(
        _dispatch_kernel,
        grid_spec=grid_spec,
        out_shape=jax.ShapeDtypeStruct((n_slots, HALF), jnp.uint32),
        compiler_params=pltpu.CompilerParams(dimension_semantics=("arbitrary",), vmem_limit_bytes=VMEM_LIMIT),
        name="moe_dispatch",
    )(dest, zero_blk, hn_pk)


def _moe_plan(rtt, cnt):
    t_all = rtt.shape[1]
    n_assign = 2 * t_all
    e_flat = rtt[0:2].astype(jnp.int32).reshape(-1)
    rank = rtt[4:6].astype(jnp.int32).reshape(-1)
    counts = cnt[0, 0:N_EXPERTS].astype(jnp.int32)
    padded = (counts + MOE_BLK - 1) // MOE_BLK * MOE_BLK
    pad_end = jnp.cumsum(padded)
    pad_start = pad_end - padded
    before = jnp.arange(N_EXPERTS, dtype=jnp.int32)[:, None] < e_flat[None, :]
    dest = rank + jnp.sum(jnp.where(before, padded[:, None], 0), axis=0)
    nblk = -(-(n_assign + N_EXPERTS * (MOE_BLK - 1)) // MOE_BLK)
    n_slots = nblk * MOE_BLK
    blk = jnp.arange(nblk, dtype=jnp.int32)
    last_of_expert = jnp.any((blk[:, None] == (pad_end // MOE_BLK - 1)[None, :]) & (counts > 0)[None, :], axis=1)
    zero_blk = (last_of_expert | (blk >= pad_end[-1] // MOE_BLK)).astype(jnp.int32)
    k_e = padded // MOE_BLK
    sbc = (k_e + MOE_SB - 1) // MOE_SB
    sb_end = jnp.cumsum(sbc)
    sb_start = sb_end - sbc
    n_sb = (nblk + (MOE_SB - 1) * N_EXPERTS) // MOE_SB
    s = jnp.arange(n_sb, dtype=jnp.int32)
    sb_e = jnp.minimum(jnp.sum((s[:, None] >= sb_end[None, :]).astype(jnp.int32), axis=1), N_EXPERTS - 1)
    j = s - sb_start[sb_e]
    real = s < sb_end[-1]
    tail_blk = pad_end[-1] // MOE_BLK + MOE_SB * (s - sb_end[-1])
    sb_m = jnp.where(real, jnp.clip(k_e[sb_e] - MOE_SB * j, 0, MOE_SB), jnp.clip(nblk - tail_blk, 0, MOE_SB))
    sb_n = jnp.where(real, jnp.clip(counts[sb_e] - MOE_SB * MOE_BLK * j, 0, MOE_SB * MOE_BLK), 0)
    sb_base = jnp.where(real, pad_start[sb_e] + MOE_SB * MOE_BLK * j, jnp.minimum(tail_blk, nblk - 1) * MOE_BLK)
    i32 = lambda a: a.astype(jnp.int32)
    return (i32(sb_e), i32(sb_m), i32(sb_n), i32(sb_base)), i32(dest), zero_blk, n_slots


def _ple_out_kernel(dest_ref, h1_ref, rt_ref, pp_ref, ps_ref, nple_ref, wpg_ref, wp_ref, nfin_ref, ys_hbm,
                    op_ref, os_ref, yb_ref, sem, *, n_prompt_tiles):
    i = pl.program_id(0)
    nt = pl.num_programs(0)
    tm = h1_ref.shape[0]
    t_all = nt * tm
    slot = lax.rem(i, 2)

    def issue(tile, sl):
        base = tile * tm

        def grp(g, c):
            for j in range(DMA_UNROLL):
                row = g * (DMA_UNROLL // 2) + j // 2
                d = dest_ref[(j % 2) * t_all + base + row]
                pltpu.make_async_copy(ys_hbm.at[pl.ds(d, 1), :], yb_ref.at[sl, j % 2, pl.ds(row, 1), :],
                                      sem.at[sl]).start()
            return c

        lax.fori_loop(0, 2 * tm // DMA_UNROLL, grp, 0)

    @pl.when(i == 0)
    def _():
        issue(0, 0)

    def wait_rows(sl):
        for k in range(2):
            pltpu.make_async_copy(ys_hbm.at[pl.ds(0, tm), :], yb_ref.at[sl, k], sem.at[sl]).wait()

    wait_rows(slot)
    is_p = i < n_prompt_tiles
    rt = rt_ref[...]
    w0, w1 = rt[:, 2:3], rt[:, 3:4]
    lo0, hi0 = _unpack_bf16_pair(yb_ref[slot, 0])
    lo1, hi1 = _unpack_bf16_pair(yb_ref[slot, 1])
    h2 = h1_ref[...] + jnp.concatenate([w0 * lo0 + w1 * lo1, w0 * hi0 + w1 * hi1], axis=1)
    hb = _rms(h2, nple_ref[...]).astype(BF16)
    nbase = jnp.minimum(i + 1, nt - 1) * tm
    n_chunk = 8
    cw, ca = D_MODEL // n_chunk, 2 * tm // n_chunk
    gates = []
    for c in range(n_chunk):
        gates.append(jax.nn.sigmoid(_dot(hb, wpg_ref[:, c * cw:(c + 1) * cw])))
        for a in range(c * ca, (c + 1) * ca):
            d = dest_ref[(a % 2) * t_all + nbase + a // 2]
            pltpu.make_async_copy(ys_hbm.at[pl.ds(d, 1), :], yb_ref.at[1 - slot, a % 2, pl.ds(a // 2, 1), :],
                                  sem.at[1 - slot]).start()
    gate = jnp.concatenate(gates, axis=1)

    @pl.when(i == nt - 1)
    def _():
        wait_rows(1 - slot)

    p = jnp.where(is_p, pp_ref[...], ps_ref[...])
    h3 = h2 + _dot(p.astype(BF16), wp_ref[...]) * gate
    y = _rms(h3, nfin_ref[...])

    @pl.when(is_p)
    def _():
        op_ref[...] = y

    @pl.when(jnp.logical_not(is_p))
    def _():
        os_ref[...] = y


def _ple_out(dest, h1, rt, pp, ps, nple, wpg, wp, nfin, y_slots):
    tm = TOK_TILE
    npt, nst = pp.shape[0] // tm, ps.shape[0] // tm
    row = lambda w: pl.BlockSpec((tm, w), lambda i, d: (i, 0))
    prow = lambda w: pl.BlockSpec((tm, w), lambda i, d: (jnp.minimum(i, npt - 1), 0))
    srow = lambda w: pl.BlockSpec((tm, w), lambda i, d: (jnp.maximum(i - npt, 0), 0))
    const = lambda shape: pl.BlockSpec(shape, lambda i, d: (0,) * len(shape), pipeline_mode=pl.Buffered(1))
    grid_spec = pltpu.PrefetchScalarGridSpec(
        num_scalar_prefetch=1,
        grid=(npt + nst,),
        in_specs=[row(D_MODEL), row(LANES), prow(D_PLE), srow(D_PLE), const((1, D_MODEL)), const(wpg.shape),
                  const(wp.shape), const((1, D_MODEL)), pl.BlockSpec(memory_space=pl.ANY)],
        out_specs=[prow(D_MODEL), srow(D_MODEL)],
        scratch_shapes=[pltpu.VMEM((2, 2, tm, HALF), jnp.uint32), pltpu.SemaphoreType.DMA((2,))],
    )
    return pl.pallas_call(
        functools.partial(_ple_out_kernel, n_prompt_tiles=npt),
        grid_spec=grid_spec,
        out_shape=[jax.ShapeDtypeStruct((pp.shape[0], D_MODEL), F32),
                   jax.ShapeDtypeStruct((ps.shape[0], D_MODEL), F32)],
        compiler_params=pltpu.CompilerParams(dimension_semantics=("arbitrary",), vmem_limit_bytes=VMEM_LIMIT),
        name="ple_out",
    )(dest, h1, rt, pp, ps, nple, wpg, wp, nfin, y_slots)


def _s5_params(lam_re, lam_im, log_dt, b_re, b_im, c_re, c_im, d_skip):
    dt = jnp.exp(log_dt)[:, None]
    mag = jnp.exp(lam_re * dt)
    ab_re = mag * jnp.cos(lam_im * dt)
    ab_im = mag * jnp.sin(lam_im * dt)
    den = lam_re * lam_re + lam_im * lam_im
    nr = ab_re - 1.0
    f_re = (nr * lam_re + ab_im * lam_im) / den
    f_im = (ab_im * lam_re - nr * lam_im) / den
    bb_re = f_re[..., None] * b_re - f_im[..., None] * b_im
    bb_im = f_re[..., None] * b_im + f_im[..., None] * b_re
    eye = jnp.eye(S5_GB, dtype=F32)

    def bu_w(bb):
        bb = bb.reshape(S5_NGB, S5_GB, S5_STATE, S5_GROUP)
        w = jnp.einsum('nlph,lm->nlhmp', bb, eye)
        return w.reshape(S5_NGB, S5_GB * S5_GROUP, S5_SL)

    def c_w(c):
        c = c.reshape(S5_NGB, S5_GB, S5_GROUP, S5_STATE)
        w = jnp.einsum('nlhp,lm->nlpmh', c, eye)
        return w.reshape(S5_NGB, S5_SL, S5_GB * S5_GROUP)

    wbu = jnp.concatenate([bu_w(bb_re), bu_w(bb_im)], axis=2).astype(BF16)
    wc = jnp.concatenate([c_w(c_re), -c_w(c_im)], axis=1).astype(BF16)
    a_re = ab_re.reshape(S5_NGB, 1, S5_SL)
    a_im = ab_im.reshape(S5_NGB, 1, S5_SL)
    dsk = d_skip.reshape(S5_NGB, 1, S5_GB * S5_GROUP)
    return wbu, wc, a_re, a_im, dsk


def kernel(x_prompt, x_sample, p_prompt, p_sample, state_gla, state_s5_re, state_s5_im, norm_mix, w_in, gla_w_gate_up, gla_gate_bias, gla_norm, s5_lam_re, s5_lam_im, s5_log_dt, s5_b_re, s5_b_im, s5_c_re, s5_c_im, s5_d, s5_glu_w, s5_glu_b, s5_norm, w_out, norm_ffn, router_group, router_expert, w_gate, w_up, w_down, norm_ple, w_ple, w_ple_gate, norm_final):
    depth = w_in.shape[0]
    assert depth == 1
    i = 0
    bp, lp, _ = x_prompt.shape
    bs, ls, _ = x_sample.shape
    tp, ts = bp * lp, bs * ls
    t_all = tp + ts

    wm = _w_in_prep(w_in[i])
    wgu = jnp.pad(gla_w_gate_up[i], ((0, LANES - GLA_RANK), (0, 0))).astype(BF16)
    gbias = gla_gate_bias[i].reshape(1, QK_W)
    wbu, wc, a_re, a_im, dsk = _s5_params(s5_lam_re[i], s5_lam_im[i], s5_log_dt[i], s5_b_re[i], s5_b_im[i],
                                          s5_c_re[i], s5_c_im[i], s5_d[i])
    glu_w = s5_glu_w[i].astype(BF16)
    wo = w_out[i].astype(BF16)
    wr32 = jnp.pad(jnp.concatenate([router_group[i], router_expert[i]], axis=1),
                   ((0, 0), (0, LANES - N_EGROUPS - N_EXPERTS)))
    wr_hi = wr32.astype(BF16)
    wr = jnp.stack([wr_hi, (wr32 - wr_hi.astype(F32)).astype(BF16)])
    wpg = w_ple_gate[i].astype(BF16)
    wp = w_ple[i].astype(BF16)
    vec = lambda a: a.reshape(1, -1)

    xp = x_prompt.reshape(tp, D_MODEL)
    xs = x_sample.reshape(ts, D_MODEL)

    qp, kp, vp, rp, lap, up = _in_proj(xp, vec(norm_mix[i]), wm, wgu, gbias, BF16)
    qs, ks, vs, rs, las, us = _in_proj(xs, vec(norm_mix[i]), wm, wgu, gbias, BF16)
    ogp, gla_p = _gla_prompt(qp, kp, vp, rp, lap, vec(gla_norm[i]), bp, lp)
    ogs, gla_s = _gla_sample(qs, ks, vs, rs, las, vec(gla_norm[i]), state_gla[i], bs, ls)
    wbu2, wc2, a_re2, a_im2 = _s5_prompt_params(wbu, wc, a_re, a_im, bp)
    ysp, re_p, im_p = _s5_prompt(up.reshape(bp, lp, D_S5), wbu2, wc2, a_re2, a_im2, dsk, tc=256)
    yss, re_s, im_s = _s5(us.reshape(bs, ls, D_S5), wbu, wc, a_re, a_im, dsk,
                          state_s5_re[i].reshape(bs, -1), state_s5_im[i].reshape(bs, -1),
                          nb=32, tc=ls, has_state=True)

    h1, hn_all, rt_all, rtt, cnt = _mix_out(xp, xs, ogp, ogs, ysp.reshape(tp, D_S5), yss.reshape(ts, D_S5),
                                            glu_w, vec(s5_glu_b[i]), vec(s5_norm[i]), wo, vec(norm_ffn[i]), wr)

    plan, dest, zero_blk, n_slots = _moe_plan(rtt, cnt)
    x_slots = _dispatch(dest, zero_blk, hn_all, n_slots)
    y_slots = _moe(x_slots, plan, w_gate[i], w_up[i], w_down[i], n_slots)

    y_p, y_s = _ple_out(dest, h1, rt_all, p_prompt[i].reshape(tp, D_PLE), p_sample[i].reshape(ts, D_PLE),
                        vec(norm_ple[i]), wpg, wp, vec(norm_final), y_slots)

    s5shape = lambda a, b: a.reshape(1, b, S5_GROUPS, S5_STATE)
    return (y_p.reshape(bp, lp, D_MODEL), y_s.reshape(bs, ls, D_MODEL),
            gla_p[None], s5shape(re_p, bp), s5shape(im_p, bp),
            gla_s[None], s5shape(re_s, bs), s5shape(im_s, bs))
```

```python
import functools
import math

import jax
import jax.numpy as jnp
from jax import lax
from jax.experimental import pallas as pl
from jax.experimental.pallas import tpu as pltpu

F32 = jnp.float32
BF16 = jnp.bfloat16

D_MODEL = 2048
D_GLA = 1024
D_S5 = 1024
GLA_HEADS = 4
GLA_DV = 256
GLA_DK = 128
GLA_RANK = 16
GLA_CHUNK = 64
S5_GROUP = 16
S5_GROUPS = 64
S5_STATE = 64
N_EGROUPS = 4
N_EPG = 8
N_EXPERTS = 32
D_EXPERT = 512
D_PLE = 256
EPS = 1e-6

LANES = 128
QK_W = GLA_HEADS * GLA_DK
S5_GB = 8
S5_NGB = S5_GROUPS // S5_GB
S5_SL = S5_GB * S5_STATE
TOK_TILE = 256
MOE_BLK = 128
VMEM_LIMIT = 56 * 1024 * 1024
MOE_VMEM_LIMIT = 60 * 1024 * 1024


def _const_spec(shape):
    nd = len(shape)
    return pl.BlockSpec(shape, lambda *_: (0,) * nd, pipeline_mode=pl.Buffered(1))


def _rms(x, g):
    return x * lax.rsqrt(jnp.mean(x * x, axis=-1, keepdims=True) + EPS) * g


def _dot(a, b):
    return jnp.dot(a, b, preferred_element_type=F32)


def _log_sigmoid(x):
    return -(jnp.maximum(-x, 0.0) + jnp.log1p(jnp.exp(-jnp.abs(x))))


N_QKVR = 2 * QK_W + 2 * D_GLA
W_IN_COLS = N_QKVR + GLA_RANK + D_S5


def _w_in_prep_kernel(w_ref, o_ref):
    o_ref[:, 0:N_QKVR] = w_ref[:, 0:N_QKVR].astype(BF16)
    tail = w_ref[:, N_QKVR:W_IN_COLS]
    o_ref[:, N_QKVR:N_QKVR + D_S5] = tail[:, GLA_RANK:GLA_RANK + D_S5].astype(BF16)
    o_ref[:, N_QKVR + D_S5:N_QKVR + D_S5 + LANES] = tail[:, 0:LANES].astype(BF16)


def _w_in_prep(wi):
    rows = 256
    return pl.pallas_call(
        _w_in_prep_kernel,
        grid=(D_MODEL // rows,),
        in_specs=[pl.BlockSpec((rows, W_IN_COLS), lambda i: (i, 0))],
        out_specs=pl.BlockSpec((rows, N_QKVR + D_S5 + LANES), lambda i: (i, 0)),
        out_shape=jax.ShapeDtypeStruct((D_MODEL, N_QKVR + D_S5 + LANES), BF16),
        compiler_params=pltpu.CompilerParams(dimension_semantics=("arbitrary",), vmem_limit_bytes=VMEM_LIMIT),
        name="w_in_prep",
    )(wi)


def _in_proj_kernel(x_ref, g_ref, wm_ref, wgu_ref, gb_ref,
                    q_ref, k_ref, v_ref, r_ref, la_ref, u_ref):
    hb = _rms(x_ref[...], g_ref[...]).astype(BF16)

    def seg(a, b):
        return _dot(hb, wm_ref[:, a:b])

    q_ref[...] = (seg(0, QK_W) * (GLA_DK ** -0.5)).astype(q_ref.dtype)
    k_ref[...] = seg(QK_W, 2 * QK_W).astype(k_ref.dtype)
    v_ref[...] = seg(2 * QK_W, 2 * QK_W + D_GLA).astype(v_ref.dtype)
    r_ref[...] = seg(2 * QK_W + D_GLA, N_QKVR).astype(r_ref.dtype)
    u_ref[...] = seg(N_QKVR, N_QKVR + D_S5)
    zg = seg(N_QKVR + D_S5, N_QKVR + D_S5 + LANES)
    xg = _dot(zg.astype(BF16), wgu_ref[...]) + gb_ref[...]
    la_ref[...] = _log_sigmoid(xg) * (1.0 / 16.0)


def _in_proj(x2d, g, wm, wgu, gbias, act_dtype):
    t = x2d.shape[0]
    tm = TOK_TILE
    row = lambda w: pl.BlockSpec((tm, w), lambda i: (i, 0))
    return pl.pallas_call(
        _in_proj_kernel,
        grid=(t // tm,),
        in_specs=[row(D_MODEL), _const_spec((1, D_MODEL)), _const_spec(wm.shape),
                  _const_spec(wgu.shape), _const_spec((1, QK_W))],
        out_specs=[row(QK_W), row(QK_W), row(D_GLA), row(D_GLA), row(QK_W), row(D_S5)],
        out_shape=[jax.ShapeDtypeStruct((t, QK_W), act_dtype), jax.ShapeDtypeStruct((t, QK_W), act_dtype),
                   jax.ShapeDtypeStruct((t, D_GLA), act_dtype), jax.ShapeDtypeStruct((t, D_GLA), act_dtype),
                   jax.ShapeDtypeStruct((t, QK_W), F32), jax.ShapeDtypeStruct((t, D_S5), F32)],
        compiler_params=pltpu.CompilerParams(dimension_semantics=("arbitrary",), vmem_limit_bytes=VMEM_LIMIT),
        name="in_proj",
    )(x2d, g, wm, wgu, gbias)


_NT = (((1,), (1,)), ((), ()))
_TN = (((0,), (0,)), ((), ()))


def _gla_pre(q, k, la, c):
    r = q.shape[0]
    shift = int(math.log2(c))
    ri = lax.broadcasted_iota(jnp.int32, (r, r), 0)
    si = lax.broadcasted_iota(jnp.int32, (r, r), 1)
    mask = ((ri >> shift) == (si >> shift)) & (ri >= si)
    tri = jnp.where(mask, 1.0, 0.0).astype(BF16)
    hi = la.astype(BF16)
    r1 = la - hi.astype(F32)
    mid = r1.astype(BF16)
    lo = (r1 - mid.astype(F32)).astype(BF16)
    cum = _dot(tri, hi) + _dot(tri, mid) + _dot(tri, lo)
    last = jnp.concatenate([jnp.broadcast_to(cum[(i + 1) * c - 1:(i + 1) * c, :], (c, cum.shape[1]))
                            for i in range(r // c)], axis=0)
    qe = (q * jnp.exp(cum)).astype(BF16)
    ke = (k * jnp.exp(-cum)).astype(BF16)
    kd = (k * jnp.exp(last - cum)).astype(BF16)
    return qe, ke, kd, cum, mask


def _gla_intra(qe, ke, v, mask):
    sc = lax.dot_general(qe, ke, _NT, preferred_element_type=F32)
    return _dot(jnp.where(mask, sc, 0.0).astype(BF16), v)


def _gla_finish(o, r, g):
    rf = r.astype(F32)
    return _rms(o, g) * (rf * jax.nn.sigmoid(rf))


GLA_SAFE_LOG_DECAY = -60.0


def _gla_token_step(t, rows, q, k, v, a_all, st_ref, oacc_ref):
    m = rows == t
    a = jnp.sum(jnp.where(m, a_all, 0.0), axis=0, keepdims=True)
    kt = jnp.where(m, k, 0.0).astype(BF16)
    qt = jnp.where(m, q, 0.0).astype(BF16)
    vt = jnp.where(m, v, jnp.zeros_like(v))
    st = st_ref[...] * a + lax.dot_general(vt, kt, _TN, preferred_element_type=F32)
    st_ref[...] = st
    oacc_ref[...] += lax.dot_general(qt, st.astype(BF16), _NT, preferred_element_type=F32)


def _gla_prompt_kernel(q_ref, k_ref, v_ref, r_ref, la_ref, g_ref, o_ref, sfin_ref, st_ref, oacc_ref,
                       *, n_chunks):
    j = pl.program_id(1)

    @pl.when(j == 0)
    def _():
        st_ref[...] = jnp.zeros_like(st_ref)

    c = GLA_CHUNK
    qe, ke, kd, cum, mask = _gla_pre(q_ref[...].astype(F32), k_ref[...].astype(F32), la_ref[...], c)
    g = g_ref[...]
    safe = jnp.min(cum) >= GLA_SAFE_LOG_DECAY

    @pl.when(safe)
    def _():
        for h in range(GLA_HEADS):
            kc = slice(h * GLA_DK, (h + 1) * GLA_DK)
            vc = slice(h * GLA_DV, (h + 1) * GLA_DV)
            v = v_ref[:, vc]
            qe_h, kd_h = qe[:, kc], kd[:, kc]
            o = _gla_intra(qe_h, ke[:, kc], v, mask)
            st = st_ref[h]
            inter = []
            for ci in range(n_chunks):
                rows = slice(ci * c, (ci + 1) * c)
                inter.append(lax.dot_general(qe_h[rows], st.astype(BF16), _NT, preferred_element_type=F32))
                dec = jnp.exp(cum[(ci + 1) * c - 1:(ci + 1) * c, kc])
                st = st * dec + lax.dot_general(v[rows], kd_h[rows], _TN, preferred_element_type=F32)
            st_ref[h] = st
            o = o + jnp.concatenate(inter, axis=0)
            o_ref[:, vc] = _gla_finish(o, r_ref[:, vc], g).astype(o_ref.dtype)

    @pl.when(jnp.logical_not(safe))
    def _():
        n_rows = n_chunks * c
        rows = lax.broadcasted_iota(jnp.int32, (n_rows, 1), 0)
        for h in range(GLA_HEADS):
            kc = slice(h * GLA_DK, (h + 1) * GLA_DK)
            vc = slice(h * GLA_DV, (h + 1) * GLA_DV)
            q, k, v = q_ref[:, kc].astype(F32), k_ref[:, kc].astype(F32), v_ref[:, vc]
            a_all = jnp.exp(la_ref[:, kc])
            oacc_ref[...] = jnp.zeros_like(oacc_ref)

            def step(t, carry):
                _gla_token_step(t, rows, q, k, v, a_all, st_ref.at[h], oacc_ref)
                return carry

            lax.fori_loop(0, n_rows, step, 0)
            o_ref[:, vc] = _gla_finish(oacc_ref[...], r_ref[:, vc], g).astype(o_ref.dtype)

    @pl.when(j == pl.num_programs(1) - 1)
    def _():
        for h in range(GLA_HEADS):
            sfin_ref[0, h] = st_ref[h].T


def _gla_prompt(q, k, v, r, la, g, batch, seq):
    rb = 4 * GLA_CHUNK
    nj = seq // rb
    row = lambda w: pl.BlockSpec((rb, w), lambda b, j: (b * nj + j, 0))
    return pl.pallas_call(
        functools.partial(_gla_prompt_kernel, n_chunks=rb // GLA_CHUNK),
        grid=(batch, nj),
        in_specs=[row(QK_W), row(QK_W), row(D_GLA), row(D_GLA), row(QK_W), _const_spec((1, GLA_DV))],
        out_specs=[row(D_GLA),
                   pl.BlockSpec((1, GLA_HEADS, GLA_DK, GLA_DV), lambda b, j: (b, 0, 0, 0))],
        out_shape=[jax.ShapeDtypeStruct((batch * seq, D_GLA), BF16),
                   jax.ShapeDtypeStruct((batch, GLA_HEADS, GLA_DK, GLA_DV), F32)],
        scratch_shapes=[pltpu.VMEM((GLA_HEADS, GLA_DV, GLA_DK), F32), pltpu.VMEM((rb, GLA_DV), F32)],
        compiler_params=pltpu.CompilerParams(dimension_semantics=("arbitrary", "arbitrary"),
                                             vmem_limit_bytes=VMEM_LIMIT),
        name="gla_prompt",
    )(q, k, v, r, la, g)


def _gla_sample_kernel(q_ref, k_ref, v_ref, r_ref, la_ref, g_ref, s0_ref, o_ref, sfin_ref, st_ref, oacc_ref,
                       *, n_seq, seq):
    qe, ke, kd, cum, mask = _gla_pre(q_ref[...].astype(F32), k_ref[...].astype(F32), la_ref[...], seq)
    g = g_ref[...]
    safe = jnp.min(cum) >= GLA_SAFE_LOG_DECAY

    @pl.when(safe)
    def _():
        lasts = jnp.concatenate([cum[(s + 1) * seq - 1:(s + 1) * seq, :] for s in range(n_seq)]
                                + [jnp.zeros((GLA_DK - n_seq, cum.shape[1]), F32)], axis=0)
        pair = 2 * seq
        upper = lax.broadcasted_iota(jnp.int32, (pair, GLA_DK), 0) < seq
        for h in range(GLA_HEADS):
            kc = slice(h * GLA_DK, (h + 1) * GLA_DK)
            vc = slice(h * GLA_DV, (h + 1) * GLA_DV)
            v = v_ref[:, vc]
            qe_h, kd_h = qe[:, kc], kd[:, kc]
            o = _gla_intra(qe_h, ke[:, kc], v, mask)
            dec_t = jnp.exp(lasts[:, kc].T)
            inter = []
            for p in range(n_seq // 2):
                rows = slice(p * pair, (p + 1) * pair)
                qe_p, kd_p, v_p = qe_h[rows], kd_h[rows], v[rows]
                for half in range(2):
                    s = 2 * p + half
                    s0 = s0_ref[s, h]
                    o_s = _dot(qe_p, s0.astype(BF16))
                    inter.append(o_s[half * seq:(half + 1) * seq])
                    kd_s = jnp.where(upper if half == 0 else ~upper, kd_p, jnp.zeros_like(kd_p))
                    dec = jnp.broadcast_to(dec_t[:, s:s + 1], (GLA_DK, GLA_DV))
                    sfin_ref[s, h] = s0 * dec + lax.dot_general(kd_s, v_p, _TN, preferred_element_type=F32)
            o = o + jnp.concatenate(inter, axis=0)
            o_ref[:, vc] = _gla_finish(o, r_ref[:, vc], g).astype(o_ref.dtype)

    @pl.when(jnp.logical_not(safe))
    def _():
        n_rows = n_seq * seq
        rows = lax.broadcasted_iota(jnp.int32, (n_rows, 1), 0)
        for h in range(GLA_HEADS):
            kc = slice(h * GLA_DK, (h + 1) * GLA_DK)
            vc = slice(h * GLA_DV, (h + 1) * GLA_DV)
            q, k, v = q_ref[:, kc].astype(F32), k_ref[:, kc].astype(F32), v_ref[:, vc]
            a_all = jnp.exp(la_ref[:, kc])
            oacc_ref[...] = jnp.zeros_like(oacc_ref)

            def step(t, carry):
                s = t // seq

                @pl.when(t % seq == 0)
                def _():
                    st_ref[...] = s0_ref[s, h].T

                _gla_token_step(t, rows, q, k, v, a_all, st_ref, oacc_ref)

                @pl.when(t % seq == seq - 1)
                def _():
                    sfin_ref[s, h] = st_ref[...].T

                return carry

            lax.fori_loop(0, n_rows, step, 0)
            o_ref[:, vc] = _gla_finish(oacc_ref[...], r_ref[:, vc], g).astype(o_ref.dtype)


def _gla_sample(q, k, v, r, la, g, s0, batch, seq):
    ns = 16
    rb = ns * seq
    row = lambda w: pl.BlockSpec((rb, w), lambda i: (i, 0))
    st = pl.BlockSpec((ns, GLA_HEADS, GLA_DK, GLA_DV), lambda i: (i, 0, 0, 0))
    return pl.pallas_call(
        functools.partial(_gla_sample_kernel, n_seq=ns, seq=seq),
        grid=(batch // ns,),
        in_specs=[row(QK_W), row(QK_W), row(D_GLA), row(D_GLA), row(QK_W), _const_spec((1, GLA_DV)), st],
        out_specs=[row(D_GLA), st],
        out_shape=[jax.ShapeDtypeStruct((batch * seq, D_GLA), BF16),
                   jax.ShapeDtypeStruct((batch, GLA_HEADS, GLA_DK, GLA_DV), F32)],
        scratch_shapes=[pltpu.VMEM((GLA_DV, GLA_DK), F32), pltpu.VMEM((rb, GLA_DV), F32)],
        compiler_params=pltpu.CompilerParams(dimension_semantics=("arbitrary",), vmem_limit_bytes=VMEM_LIMIT),
        name="gla_sample",
    )(q, k, v, r, la, g, s0)


def _s5_kernel(u_ref, wbu_ref, wc_ref, are_ref, aim_ref, d_ref, h0r_ref, h0i_ref,
               y_ref, sre_ref, sim_ref, bu_ref, xs_ref, car_ref, *, nb, tc, has_state):
    j = pl.program_id(2)

    @pl.when(j == 0)
    def _():
        if has_state:
            car_ref[0] = h0r_ref[...]
            car_ref[1] = h0i_ref[...]
        else:
            car_ref[...] = jnp.zeros_like(car_ref)

    u2 = u_ref[...].reshape(nb * tc, LANES)
    ub = u2.astype(BF16)
    nl = S5_SL // LANES
    for l in range(2 * nl):
        bu_ref[l] = _dot(ub, wbu_ref[0, :, l * LANES:(l + 1) * LANES])
    a_r = [jnp.broadcast_to(are_ref[0, :, l * LANES:(l + 1) * LANES], (nb, LANES)) for l in range(nl)]
    a_i = [jnp.broadcast_to(aim_ref[0, :, l * LANES:(l + 1) * LANES], (nb, LANES)) for l in range(nl)]

    def step(t, carry):
        rows = pl.ds(t, nb, stride=tc)
        out = []
        for l in range(nl):
            xr, xi = carry[2 * l], carry[2 * l + 1]
            nr = a_r[l] * xr - a_i[l] * xi + bu_ref[l, rows, :]
            ni = a_r[l] * xi + a_i[l] * xr + bu_ref[nl + l, rows, :]
            xs_ref[l, rows, :] = nr
            xs_ref[nl + l, rows, :] = ni
            out += [nr, ni]
        return tuple(out)

    init = []
    for l in range(nl):
        init += [car_ref[0, :, l * LANES:(l + 1) * LANES], car_ref[1, :, l * LANES:(l + 1) * LANES]]
    fin = lax.fori_loop(0, tc, step, tuple(init), unroll=8)
    xr = jnp.concatenate([fin[2 * l] for l in range(nl)], axis=1)
    xi = jnp.concatenate([fin[2 * l + 1] for l in range(nl)], axis=1)
    car_ref[0] = xr
    car_ref[1] = xi
    y = d_ref[0] * u2
    for l in range(2 * nl):
        y = y + _dot(xs_ref[l].astype(BF16), wc_ref[0, l * LANES:(l + 1) * LANES, :])
    y_ref[...] = y.reshape(nb, tc, LANES)

    @pl.when(j == pl.num_programs(2) - 1)
    def _():
        sre_ref[...] = xr
        sim_ref[...] = xi


def _s5(u3d, wbu, wc, a_re, a_im, dsk, h0r, h0i, nb, tc, has_state):
    batch, seq, _ = u3d.shape
    grid = (S5_NGB, batch // nb, seq // tc)
    st = pl.BlockSpec((nb, S5_SL), lambda g, b, j: (b, g))
    par = lambda w: pl.BlockSpec((1, 1, w), lambda g, b, j: (g, 0, 0))
    ublk = pl.BlockSpec((nb, tc, LANES), lambda g, b, j: (b, j, g))
    return pl.pallas_call(
        functools.partial(_s5_kernel, nb=nb, tc=tc, has_state=has_state),
        grid=grid,
        in_specs=[ublk,
                  pl.BlockSpec((1, LANES, 2 * S5_SL), lambda g, b, j: (g, 0, 0)),
                  pl.BlockSpec((1, 2 * S5_SL, LANES), lambda g, b, j: (g, 0, 0)),
                  par(S5_SL), par(S5_SL), par(LANES), st, st],
        out_specs=[ublk, st, st],
        out_shape=[jax.ShapeDtypeStruct(u3d.shape, F32),
                   jax.ShapeDtypeStruct((batch, S5_GROUPS * S5_STATE), F32),
                   jax.ShapeDtypeStruct((batch, S5_GROUPS * S5_STATE), F32)],
        scratch_shapes=[pltpu.VMEM((2 * S5_SL // LANES, nb * tc, LANES), F32),
                        pltpu.VMEM((2 * S5_SL // LANES, nb * tc, LANES), F32),
                        pltpu.VMEM((2, nb, S5_SL), F32)],
        compiler_params=pltpu.CompilerParams(dimension_semantics=("arbitrary",) * 3,
                                             vmem_limit_bytes=VMEM_LIMIT),
        name="s5_state" if has_state else "s5_zero",
    )(u3d, wbu, wc, a_re, a_im, dsk, h0r, h0i)


S5_J = 2
S5_HL = S5_SL // S5_J


def _s5_prompt_kernel(u_ref, wbu_ref, wc_ref, are_ref, aim_ref, d_ref, y_ref, sre_ref, sim_ref,
                      u2_ref, lhs_ref, bu_ref, xs_ref, y2_ref, yo_ref, car_ref, *, nb, tc):
    g = pl.program_id(0)
    j = pl.program_id(1)
    rows = nb * S5_J

    @pl.when((g == 0) & (j == 0))
    def _():
        lhs_ref[...] = jnp.zeros_like(lhs_ref)

    @pl.when(j == 0)
    def _():
        car_ref[...] = jnp.zeros_like(car_ref)

    u2 = u_ref[...].reshape(nb * tc, LANES)
    u2_ref[...] = u2

    def build(t, c):
        u4 = u2_ref[pl.ds(t, nb, stride=tc), :]
        for jj in range(S5_J):
            lhs_ref[t, jj * nb:(jj + 1) * nb, jj * LANES:(jj + 1) * LANES] = u4
        return c

    lax.fori_loop(0, tc, build, 0, unroll=8)
    lhs = lhs_ref[...].reshape(tc * rows, S5_J * LANES).astype(BF16)
    bu_ref[...] = _dot(lhs, wbu_ref[0]).reshape(tc, rows, 2 * S5_HL)
    ar = are_ref[0]
    ai = aim_ref[0]

    def step(t, carry):
        xr, xi = carry
        tile = bu_ref[t]
        nr = ar * xr - ai * xi + tile[:, 0:S5_HL]
        ni = ar * xi + ai * xr + tile[:, S5_HL:2 * S5_HL]
        xs_ref[t] = jnp.concatenate([nr, ni], axis=1)
        return nr, ni

    xr, xi = lax.fori_loop(0, tc, step, (car_ref[0], car_ref[1]), unroll=8)
    car_ref[0] = xr
    car_ref[1] = xi
    xs = xs_ref[...].reshape(tc * rows, 2 * S5_HL).astype(BF16)
    y2_ref[...] = _dot(xs, wc_ref[0]).reshape(tc, rows, S5_J * LANES)
    first_half = lax.broadcasted_iota(jnp.int32, (rows, LANES), 0) < nb

    def unperm(t, c):
        t2 = y2_ref[t]
        part = jnp.where(first_half, t2[:, 0:LANES], t2[:, LANES:2 * LANES])
        tot = part + pltpu.roll(part, nb, axis=0)
        yo_ref[pl.ds(t, nb, stride=tc), :] = tot[0:nb]
        return c

    lax.fori_loop(0, tc, unperm, 0, unroll=8)
    y_ref[...] = (yo_ref[...] + d_ref[0] * u2).reshape(nb, tc, LANES)

    @pl.when(j == pl.num_programs(1) - 1)
    def _():
        sre_ref[...] = jnp.concatenate([xr[jj * nb:(jj + 1) * nb] for jj in range(S5_J)], axis=1)
        sim_ref[...] = jnp.concatenate([xi[jj * nb:(jj + 1) * nb] for jj in range(S5_J)], axis=1)


def _s5_prompt(u3d, wbu2, wc2, a_re2, a_im2, dsk, tc):
    nb, seq, _ = u3d.shape
    rows = nb * S5_J
    assert rows == 8
    st = pl.BlockSpec((nb, S5_SL), lambda g, j: (0, g))
    gblk = lambda s: pl.BlockSpec((1,) + s, lambda g, j: (g, 0, 0))
    ublk = pl.BlockSpec((nb, tc, LANES), lambda g, j: (0, j, g))
    return pl.pallas_call(
        functools.partial(_s5_prompt_kernel, nb=nb, tc=tc),
        grid=(S5_NGB, seq // tc),
        in_specs=[ublk, gblk((S5_J * LANES, 2 * S5_HL)), gblk((2 * S5_HL, S5_J * LANES)),
                  gblk((rows, S5_HL)), gblk((rows, S5_HL)), gblk((1, LANES))],
        out_specs=[ublk, st, st],
        out_shape=[jax.ShapeDtypeStruct(u3d.shape, F32),
                   jax.ShapeDtypeStruct((nb, S5_GROUPS * S5_STATE), F32),
                   jax.ShapeDtypeStruct((nb, S5_GROUPS * S5_STATE), F32)],
        scratch_shapes=[pltpu.VMEM((nb * tc, LANES), F32),
                        pltpu.VMEM((tc, rows, S5_J * LANES), F32),
                        pltpu.VMEM((tc, rows, 2 * S5_HL), F32),
                        pltpu.VMEM((tc, rows, 2 * S5_HL), F32),
                        pltpu.VMEM((tc, rows, S5_J * LANES), F32),
                        pltpu.VMEM((nb * tc, LANES), F32),
                        pltpu.VMEM((2, rows, S5_HL), F32)],
        compiler_params=pltpu.CompilerParams(dimension_semantics=("arbitrary",) * 2,
                                             vmem_limit_bytes=VMEM_LIMIT),
        name="s5_prompt",
    )(u3d, wbu2, wc2, a_re2, a_im2, dsk)


def _s5_prompt_params(wbu, wc, a_re, a_im, nb):
    h = S5_HL
    top = jnp.concatenate([wbu[:, :, 0:h], wbu[:, :, S5_SL:S5_SL + h]], axis=2)
    bot = jnp.concatenate([wbu[:, :, h:2 * h], wbu[:, :, S5_SL + h:S5_SL + 2 * h]], axis=2)
    wbu2 = jnp.concatenate([top, bot], axis=1)
    wc_j = [jnp.concatenate([wc[:, jj * h:(jj + 1) * h, :], wc[:, S5_SL + jj * h:S5_SL + (jj + 1) * h, :]],
                            axis=1) for jj in range(S5_J)]
    wc2 = jnp.concatenate(wc_j, axis=2)
    tile = lambda a: jnp.repeat(a.reshape(S5_NGB, S5_J, h), nb, axis=1)
    return wbu2, wc2, tile(a_re), tile(a_im)


def _mix_out_kernel(xp_ref, xs_ref, ogp_ref, ogs_ref, ysp_ref, yss_ref, glu_w_ref, glu_b_ref, s5n_ref, wo_ref,
                    nffn_ref, wr_ref, h1_ref, hn_ref, rt_ref, rtt_ref, cnt_ref, *, n_prompt_tiles):
    is_p = pl.program_id(0) < n_prompt_tiles
    x = jnp.where(is_p, xp_ref[...], xs_ref[...])
    og = jnp.where(is_p, ogp_ref[...], ogs_ref[...])
    y = jax.nn.gelu(jnp.where(is_p, ysp_ref[...], yss_ref[...]))
    y = y * jax.nn.sigmoid(_dot(y.astype(BF16), glu_w_ref[...]) + glu_b_ref[...])
    yn = _rms(y, s5n_ref[...]).astype(BF16)
    mix = _dot(og, wo_ref[0:D_GLA, :]) + _dot(yn, wo_ref[D_GLA:D_GLA + D_S5, :])
    h1 = x + mix
    h1_ref[...] = h1
    hn = _rms(h1, nffn_ref[...])
    hn_ref[...] = _pack_bf16_pair(hn[:, 0:HALF], hn[:, HALF:D_MODEL])
    hn_hi = hn.astype(BF16)
    hn_lo = (hn - hn_hi.astype(F32)).astype(BF16)
    logits = _dot(hn_hi, wr_ref[0]) + _dot(hn_hi, wr_ref[1]) + _dot(hn_lo, wr_ref[0])
    rt = _route(logits)

    @pl.when(pl.program_id(0) == 0)
    def _():
        cnt_ref[...] = jnp.zeros_like(cnt_ref)

    tm = rt.shape[0]
    lane = lax.broadcasted_iota(jnp.int32, rt.shape, 1).astype(F32)
    oh0 = lane == rt[:, 0:1]
    oh1 = lane == rt[:, 1:2]
    both = jnp.where(oh0 | oh1, 1.0, 0.0)
    ri = lax.broadcasted_iota(jnp.int32, (tm, tm), 0)
    ci = lax.broadcasted_iota(jnp.int32, (tm, tm), 1)
    before = _dot(jnp.where(ri > ci, 1.0, 0.0).astype(BF16), both.astype(BF16)) + cnt_ref[...]
    rank0 = jnp.sum(jnp.where(oh0, before, 0.0), axis=-1, keepdims=True)
    rank1 = jnp.sum(jnp.where(oh1, before, 0.0), axis=-1, keepdims=True)
    cnt_ref[...] += jnp.sum(both, axis=0, keepdims=True)
    rt = jnp.where(lane == 4.0, rank0, jnp.where(lane == 5.0, rank1, rt))
    rt_ref[...] = rt
    rtt_ref[...] = rt.T[0:8, :]


def _route(logits):
    col = lax.broadcasted_iota(jnp.int32, logits.shape, 1)
    colf = col.astype(F32)
    neg = -jnp.inf

    def first_argmax(vals):
        m = jnp.max(vals, axis=-1, keepdims=True)
        idx = jnp.min(jnp.where(vals == m, colf, float(LANES)), axis=-1, keepdims=True)
        return m, idx

    lg = jnp.where(col < N_EGROUPS, logits, neg)
    gmax, gsel = first_argmax(lg)
    p_g = 1.0 / jnp.sum(jnp.exp(lg - gmax), axis=-1, keepdims=True)
    ecol = col - N_EGROUPS
    egrp = (ecol >> 3).astype(F32)
    in_group = (ecol >= 0) & (ecol < N_EXPERTS) & (egrp == gsel)
    le = jnp.where(in_group, logits, neg)
    m1, i1 = first_argmax(le)
    le2 = jnp.where(colf == i1, neg, le)
    m2, i2 = first_argmax(le2)
    e2 = jnp.exp(m2 - m1)
    den = 1.0 + e2
    w1 = p_g * (1.0 / den)
    w2 = p_g * (e2 / den)
    e1f = i1 - float(N_EGROUPS)
    e2f = i2 - float(N_EGROUPS)
    out = jnp.where(col == 0, e1f, jnp.where(col == 1, e2f, jnp.where(col == 2, w1, jnp.where(col == 3, w2, 0.0))))
    return out


def _mix_out(xp, xs, ogp, ogs, ysp, yss, glu_w, glu_b, s5n, wo, nffn, wr):
    tm = TOK_TILE
    npt, nst = xp.shape[0] // tm, xs.shape[0] // tm
    t = (npt + nst) * tm
    row = lambda w: pl.BlockSpec((tm, w), lambda i: (i, 0))
    prow = lambda w: pl.BlockSpec((tm, w), lambda i: (jnp.minimum(i, npt - 1), 0))
    srow = lambda w: pl.BlockSpec((tm, w), lambda i: (jnp.maximum(i - npt, 0), 0))
    return pl.pallas_call(
        functools.partial(_mix_out_kernel, n_prompt_tiles=npt),
        grid=(npt + nst,),
        in_specs=[prow(D_MODEL), srow(D_MODEL), prow(D_GLA), srow(D_GLA), prow(D_S5), srow(D_S5),
                  _const_spec(glu_w.shape), _const_spec((1, D_S5)), _const_spec((1, D_S5)),
                  _const_spec(wo.shape), _const_spec((1, D_MODEL)), _const_spec(wr.shape)],
        out_specs=[row(D_MODEL), row(HALF), row(LANES), pl.BlockSpec((8, tm), lambda i: (0, i)),
                   pl.BlockSpec((1, LANES), lambda i: (0, 0))],
        out_shape=[jax.ShapeDtypeStruct((t, D_MODEL), F32), jax.ShapeDtypeStruct((t, HALF), jnp.uint32),
                   jax.ShapeDtypeStruct((t, LANES), F32), jax.ShapeDtypeStruct((8, t), F32),
                   jax.ShapeDtypeStruct((1, LANES), F32)],
        compiler_params=pltpu.CompilerParams(dimension_semantics=("arbitrary",), vmem_limit_bytes=VMEM_LIMIT),
        name="mix_out",
    )(xp, xs, ogp, ogs, ysp, yss, glu_w, glu_b, s5n, wo, nffn, wr)


DMA_UNROLL = 8


MOE_SB = 6
HALF = D_MODEL // 2


def _pack_bf16_pair(lo, hi):
    def bits(x):
        b = pltpu.bitcast(x, jnp.uint32)
        return (b + jnp.uint32(0x7FFF) + ((b >> 16) & jnp.uint32(1))) >> 16
    return bits(lo) | (bits(hi) << 16)


def _unpack_bf16_pair(u):
    return (pltpu.bitcast(u << 16, F32), pltpu.bitcast(u & jnp.uint32(0xFFFF0000), F32))


def _moe_kernel(se_ref, sm_ref, sn_ref, sbase_ref, xs_hbm, wg_ref, wu_ref, wd_ref, out_hbm,
                xb_ref, yb_ref, wgb_ref, wub_ref, wdb_ref, gsem, ssem):
    b = pl.program_id(0)
    nb = pl.num_programs(0)
    n = sn_ref[b]
    slot = lax.rem(b, 2)

    def issue_gather(blk, sl):
        base = pl.multiple_of(sbase_ref[blk], MOE_BLK)
        for m in range(1, MOE_SB + 1):
            @pl.when((sn_ref[blk] > 0) & (sm_ref[blk] == m))
            def _():
                rows = m * MOE_BLK
                pltpu.make_async_copy(xs_hbm.at[pl.ds(base, rows), :], xb_ref.at[sl, pl.ds(0, rows), :],
                                      gsem.at[sl]).start()

    def wait_gather(blocks, sl):
        rows = pl.ds(0, pl.multiple_of(blocks * MOE_BLK, MOE_BLK))
        pltpu.make_async_copy(xs_hbm.at[rows, :], xb_ref.at[sl, rows, :], gsem.at[sl]).wait()

    m_cur = sm_ref[b]
    m_prev = sm_ref[jnp.maximum(b - 1, 0)]
    has_prev = (b > 0) & (m_prev > 0)

    def out_copy(rows):
        dst = pl.ds(pl.multiple_of(sbase_ref[b], MOE_BLK), rows)
        return pltpu.make_async_copy(yb_ref.at[pl.ds(0, rows), :], out_hbm.at[dst, :], ssem.at[0])

    def wait_prev_out():
        rows = pl.ds(0, pl.multiple_of(m_prev * MOE_BLK, MOE_BLK))
        pltpu.make_async_copy(yb_ref.at[rows, :], out_hbm.at[rows, :], ssem.at[0]).wait()

    def write_out(rows):
        out_copy(rows).start()

        @pl.when(b == nb - 1)
        def _():
            out_copy(rows).wait()

    @pl.when(b == 0)
    def _():
        issue_gather(0, 0)

    @pl.when(b + 1 < nb)
    def _():
        issue_gather(b + 1, 1 - slot)

    def compute(rows):
        x_lo, x_hi = _unpack_bf16_pair(xb_ref[slot, 0:rows, :])
        x_lo, x_hi = x_lo.astype(BF16), x_hi.astype(BF16)
        gate = _dot(x_lo, wgb_ref[0:HALF, :]) + _dot(x_hi, wgb_ref[HALF:D_MODEL, :])
        up = _dot(x_lo, wub_ref[0:HALF, :]) + _dot(x_hi, wub_ref[HALF:D_MODEL, :])
        hid = (gate * jax.nn.sigmoid(gate) * up).astype(BF16)

        @pl.when(has_prev)
        def _():
            wait_prev_out()

        yb_ref[0:rows, :] = _pack_bf16_pair(_dot(hid, wdb_ref[:, 0:HALF]), _dot(hid, wdb_ref[:, HALF:D_MODEL]))
        write_out(rows)

    @pl.when(n > 0)
    def _():
        prev_e = se_ref[jnp.maximum(b - 1, 0)]

        @pl.when((b == 0) | (prev_e != se_ref[b]))
        def _():
            wgb_ref[...] = wg_ref[...].astype(BF16)
            wub_ref[...] = wu_ref[...].astype(BF16)
            wdb_ref[...] = wd_ref[...].astype(BF16)

        wait_gather(m_cur, slot)
        for m in range(1, MOE_SB + 1):
            @pl.when(m_cur == m)
            def _():
                compute(m * MOE_BLK)

    @pl.when((n == 0) & (m_cur > 0))
    def _():
        @pl.when(has_prev)
        def _():
            wait_prev_out()

        yb_ref[...] = jnp.zeros_like(yb_ref)
        for m in range(1, MOE_SB + 1):
            @pl.when(m_cur == m)
            def _():
                write_out(m * MOE_BLK)

    @pl.when((m_cur == 0) & has_prev)
    def _():
        wait_prev_out()


def _moe(x_slots, plan, w_gate, w_up, w_down, n_out_rows):
    sb_e, sb_m, sb_n, sb_base = plan
    sb_rows = MOE_SB * MOE_BLK
    wspec = lambda s: pl.BlockSpec((None,) + s, lambda b, se, *_: (se[b], 0, 0))
    grid_spec = pltpu.PrefetchScalarGridSpec(
        num_scalar_prefetch=4,
        grid=(sb_e.shape[0],),
        in_specs=[pl.BlockSpec(memory_space=pl.ANY),
                  wspec((D_MODEL, D_EXPERT)), wspec((D_MODEL, D_EXPERT)), wspec((D_EXPERT, D_MODEL))],
        out_specs=pl.BlockSpec(memory_space=pl.ANY),
        scratch_shapes=[pltpu.VMEM((2, sb_rows, HALF), jnp.uint32), pltpu.VMEM((sb_rows, HALF), jnp.uint32),
                        pltpu.VMEM((D_MODEL, D_EXPERT), BF16), pltpu.VMEM((D_MODEL, D_EXPERT), BF16),
                        pltpu.VMEM((D_EXPERT, D_MODEL), BF16), pltpu.SemaphoreType.DMA((2,)),
                        pltpu.SemaphoreType.DMA((1,))],
    )
    return pl.pallas_call(
        _moe_kernel,
        grid_spec=grid_spec,
        out_shape=jax.ShapeDtypeStruct((n_out_rows, HALF), jnp.uint32),
        compiler_params=pltpu.CompilerParams(dimension_semantics=("arbitrary",),
                                             vmem_limit_bytes=MOE_VMEM_LIMIT),
        name="moe_experts",
    )(sb_e, sb_m, sb_n, sb_base, x_slots, w_gate, w_up, w_down)


def _dispatch_kernel(dest_ref, zb_ref, hn_ref, out_hbm, buf_ref, zero_ref, sem, zsem):
    i = pl.program_id(0)
    nt = pl.num_programs(0)
    tm = hn_ref.shape[0]
    t_all = nt * tm
    slot = lax.rem(i, 2)
    nblk = zb_ref.shape[0]

    def zero_copy(blk):
        dst = pl.ds(pl.multiple_of(blk * MOE_BLK, MOE_BLK), MOE_BLK)
        return pltpu.make_async_copy(zero_ref, out_hbm.at[dst, :], zsem.at[0])

    @pl.when(i == 0)
    def _():
        zero_ref[...] = jnp.zeros_like(zero_ref)

        def start(blk, c):
            @pl.when(zb_ref[blk] == 1)
            def _():
                zero_copy(blk).start()
            return c

        def wait(blk, c):
            @pl.when(zb_ref[blk] == 1)
            def _():
                zero_copy(blk).wait()
            return c

        lax.fori_loop(0, nblk, start, 0)
        lax.fori_loop(0, nblk, wait, 0)

    def wait_rows(sl):
        for k in range(2):
            pltpu.make_async_copy(buf_ref.at[sl], out_hbm.at[pl.ds(0, tm), :], sem.at[sl]).wait()

    @pl.when(i >= 2)
    def _():
        wait_rows(slot)

    buf_ref[slot] = hn_ref[...]
    base = i * tm

    def grp(g, c):
        for j in range(DMA_UNROLL):
            row = g * (DMA_UNROLL // 2) + j // 2
            d = dest_ref[(j % 2) * t_all + base + row]
            pltpu.make_async_copy(buf_ref.at[slot, pl.ds(row, 1), :], out_hbm.at[pl.ds(d, 1), :],
                                  sem.at[slot]).start()
        return c

    lax.fori_loop(0, 2 * tm // DMA_UNROLL, grp, 0)

    @pl.when(i == nt - 1)
    def _():
        wait_rows(slot)

        @pl.when(nt > 1)
        def _():
            wait_rows(1 - slot)


def _dispatch(dest, zero_blk, hn_pk, n_slots):
    tm = TOK_TILE
    grid_spec = pltpu.PrefetchScalarGridSpec(
        num_scalar_prefetch=2,
        grid=(hn_pk.shape[0] // tm,),
        in_specs=[pl.BlockSpec((tm, HALF), lambda i, d, z: (i, 0))],
        out_specs=pl.BlockSpec(memory_space=pl.ANY),
        scratch_shapes=[pltpu.VMEM((2, tm, HALF), jnp.uint32), pltpu.VMEM((MOE_BLK, HALF), jnp.uint32),
                        pltpu.SemaphoreType.DMA((2,)), pltpu.SemaphoreType.DMA((1,))],
    )
    return pl.pallas_call(
        _dispatch_kernel,
        grid_spec=grid_spec,
        out_shape=jax.ShapeDtypeStruct((n_slots, HALF), jnp.uint32),
        compiler_params=pltpu.CompilerParams(dimension_semantics=("arbitrary",), vmem_limit_bytes=VMEM_LIMIT),
        name="moe_dispatch",
    )(dest, zero_blk, hn_pk)


def _moe_plan(rtt, cnt):
    t_all = rtt.shape[1]
    n_assign = 2 * t_all
    e_flat = rtt[0:2].astype(jnp.int32).reshape(-1)
    rank = rtt[4:6].astype(jnp.int32).reshape(-1)
    counts = cnt[0, 0:N_EXPERTS].astype(jnp.int32)
    padded = (counts + MOE_BLK - 1) // MOE_BLK * MOE_BLK
    pad_end = jnp.cumsum(padded)
    pad_start = pad_end - padded
    before = jnp.arange(N_EXPERTS, dtype=jnp.int32)[:, None] < e_flat[None, :]
    dest = rank + jnp.sum(jnp.where(before, padded[:, None], 0), axis=0)
    nblk = -(-(n_assign + N_EXPERTS * (MOE_BLK - 1)) // MOE_BLK)
    n_slots = nblk * MOE_BLK
    blk = jnp.arange(nblk, dtype=jnp.int32)
    last_of_expert = jnp.any((blk[:, None] == (pad_end // MOE_BLK - 1)[None, :]) & (counts > 0)[None, :], axis=1)
    zero_blk = (last_of_expert | (blk >= pad_end[-1] // MOE_BLK)).astype(jnp.int32)
    k_e = padded // MOE_BLK
    sbc = (k_e + MOE_SB - 1) // MOE_SB
    sb_end = jnp.cumsum(sbc)
    sb_start = sb_end - sbc
    n_sb = (nblk + (MOE_SB - 1) * N_EXPERTS) // MOE_SB
    s = jnp.arange(n_sb, dtype=jnp.int32)
    sb_e = jnp.minimum(jnp.sum((s[:, None] >= sb_end[None, :]).astype(jnp.int32), axis=1), N_EXPERTS - 1)
    j = s - sb_start[sb_e]
    real = s < sb_end[-1]
    tail_blk = pad_end[-1] // MOE_BLK + MOE_SB * (s - sb_end[-1])
    sb_m = jnp.where(real, jnp.clip(k_e[sb_e] - MOE_SB * j, 0, MOE_SB), jnp.clip(nblk - tail_blk, 0, MOE_SB))
    sb_n = jnp.where(real, jnp.clip(counts[sb_e] - MOE_SB * MOE_BLK * j, 0, MOE_SB * MOE_BLK), 0)
    sb_base = jnp.where(real, pad_start[sb_e] + MOE_SB * MOE_BLK * j, jnp.minimum(tail_blk, nblk - 1) * MOE_BLK)
    i32 = lambda a: a.astype(jnp.int32)
    return (i32(sb_e), i32(sb_m), i32(sb_n), i32(sb_base)), i32(dest), zero_blk, n_slots


def _ple_out_kernel(dest_ref, h1_ref, rt_ref, pp_ref, ps_ref, nple_ref, wpg_ref, wp_ref, nfin_ref, ys_hbm,
                    op_ref, os_ref, yb_ref, sem, *, n_prompt_tiles):
    i = pl.program_id(0)
    nt = pl.num_programs(0)
    tm = h1_ref.shape[0]
    t_all = nt * tm
    slot = lax.rem(i, 2)

    def issue(tile, sl):
        base = tile * tm

        def grp(g, c):
            for j in range(DMA_UNROLL):
                row = g * (DMA_UNROLL // 2) + j // 2
                d = dest_ref[(j % 2) * t_all + base + row]
                pltpu.make_async_copy(ys_hbm.at[pl.ds(d, 1), :], yb_ref.at[sl, j % 2, pl.ds(row, 1), :],
                                      sem.at[sl]).start()
            return c

        lax.fori_loop(0, 2 * tm // DMA_UNROLL, grp, 0)

    @pl.when(i == 0)
    def _():
        issue(0, 0)

    def wait_rows(sl):
        for k in range(2):
            pltpu.make_async_copy(ys_hbm.at[pl.ds(0, tm), :], yb_ref.at[sl, k], sem.at[sl]).wait()

    wait_rows(slot)
    is_p = i < n_prompt_tiles
    rt = rt_ref[...]
    w0, w1 = rt[:, 2:3], rt[:, 3:4]
    lo0, hi0 = _unpack_bf16_pair(yb_ref[slot, 0])
    lo1, hi1 = _unpack_bf16_pair(yb_ref[slot, 1])
    h2 = h1_ref[...] + jnp.concatenate([w0 * lo0 + w1 * lo1, w0 * hi0 + w1 * hi1], axis=1)
    hb = _rms(h2, nple_ref[...]).astype(BF16)
    pb = jnp.where(is_p, pp_ref[...], ps_ref[...]).astype(BF16)
    nbase = jnp.minimum(i + 1, nt - 1) * tm
    n_chunk = 8
    cw, ca = D_MODEL // n_chunk, 2 * tm // (2 * n_chunk)
    issued = 0
    upd = []
    for c in range(n_chunk):
        cs = slice(c * cw, (c + 1) * cw)
        for step in range(2):
            if step == 0:
                part = jax.nn.sigmoid(_dot(hb, wpg_ref[:, cs]))
            else:
                upd.append(_dot(pb, wp_ref[:, cs]) * part)
            for a in range(issued, issued + ca):
                d = dest_ref[(a % 2) * t_all + nbase + a // 2]
                pltpu.make_async_copy(ys_hbm.at[pl.ds(d, 1), :],
                                      yb_ref.at[1 - slot, a % 2, pl.ds(a // 2, 1), :], sem.at[1 - slot]).start()
            issued += ca

    @pl.when(i == nt - 1)
    def _():
        wait_rows(1 - slot)

    h3 = h2 + jnp.concatenate(upd, axis=1)
    y = _rms(h3, nfin_ref[...])

    @pl.when(is_p)
    def _():
        op_ref[...] = y

    @pl.when(jnp.logical_not(is_p))
    def _():
        os_ref[...] = y


def _ple_out(dest, h1, rt, pp, ps, nple, wpg, wp, nfin, y_slots):
    tm = TOK_TILE
    npt, nst = pp.shape[0] // tm, ps.shape[0] // tm
    row = lambda w: pl.BlockSpec((tm, w), lambda i, d: (i, 0))
    prow = lambda w: pl.BlockSpec((tm, w), lambda i, d: (jnp.minimum(i, npt - 1), 0))
    srow = lambda w: pl.BlockSpec((tm, w), lambda i, d: (jnp.maximum(i - npt, 0), 0))
    const = lambda shape: pl.BlockSpec(shape, lambda i, d: (0,) * len(shape), pipeline_mode=pl.Buffered(1))
    grid_spec = pltpu.PrefetchScalarGridSpec(
        num_scalar_prefetch=1,
        grid=(npt + nst,),
        in_specs=[row(D_MODEL), row(LANES), prow(D_PLE), srow(D_PLE), const((1, D_MODEL)), const(wpg.shape),
                  const(wp.shape), const((1, D_MODEL)), pl.BlockSpec(memory_space=pl.ANY)],
        out_specs=[prow(D_MODEL), srow(D_MODEL)],
        scratch_shapes=[pltpu.VMEM((2, 2, tm, HALF), jnp.uint32), pltpu.SemaphoreType.DMA((2,))],
    )
    return pl.pallas_call(
        functools.partial(_ple_out_kernel, n_prompt_tiles=npt),
        grid_spec=grid_spec,
        out_shape=[jax.ShapeDtypeStruct((pp.shape[0], D_MODEL), F32),
                   jax.ShapeDtypeStruct((ps.shape[0], D_MODEL), F32)],
        compiler_params=pltpu.CompilerParams(dimension_semantics=("arbitrary",), vmem_limit_bytes=VMEM_LIMIT),
        name="ple_out",
    )(dest, h1, rt, pp, ps, nple, wpg, wp, nfin, y_slots)


def _s5_params(lam_re, lam_im, log_dt, b_re, b_im, c_re, c_im, d_skip):
    dt = jnp.exp(log_dt)[:, None]
    mag = jnp.exp(lam_re * dt)
    ab_re = mag * jnp.cos(lam_im * dt)
    ab_im = mag * jnp.sin(lam_im * dt)
    den = lam_re * lam_re + lam_im * lam_im
    nr = ab_re - 1.0
    f_re = (nr * lam_re + ab_im * lam_im) / den
    f_im = (ab_im * lam_re - nr * lam_im) / den
    bb_re = f_re[..., None] * b_re - f_im[..., None] * b_im
    bb_im = f_re[..., None] * b_im + f_im[..., None] * b_re
    eye = jnp.eye(S5_GB, dtype=F32)

    def bu_w(bb):
        bb = bb.reshape(S5_NGB, S5_GB, S5_STATE, S5_GROUP)
        w = jnp.einsum('nlph,lm->nlhmp', bb, eye)
        return w.reshape(S5_NGB, S5_GB * S5_GROUP, S5_SL)

    def c_w(c):
        c = c.reshape(S5_NGB, S5_GB, S5_GROUP, S5_STATE)
        w = jnp.einsum('nlhp,lm->nlpmh', c, eye)
        return w.reshape(S5_NGB, S5_SL, S5_GB * S5_GROUP)

    wbu = jnp.concatenate([bu_w(bb_re), bu_w(bb_im)], axis=2).astype(BF16)
    wc = jnp.concatenate([c_w(c_re), -c_w(c_im)], axis=1).astype(BF16)
    a_re = ab_re.reshape(S5_NGB, 1, S5_SL)
    a_im = ab_im.reshape(S5_NGB, 1, S5_SL)
    dsk = d_skip.reshape(S5_NGB, 1, S5_GB * S5_GROUP)
    return wbu, wc, a_re, a_im, dsk


def kernel(x_prompt, x_sample, p_prompt, p_sample, state_gla, state_s5_re, state_s5_im, norm_mix, w_in, gla_w_gate_up, gla_gate_bias, gla_norm, s5_lam_re, s5_lam_im, s5_log_dt, s5_b_re, s5_b_im, s5_c_re, s5_c_im, s5_d, s5_glu_w, s5_glu_b, s5_norm, w_out, norm_ffn, router_group, router_expert, w_gate, w_up, w_down, norm_ple, w_ple, w_ple_gate, norm_final):
    depth = w_in.shape[0]
    assert depth == 1
    i = 0
    bp, lp, _ = x_prompt.shape
    bs, ls, _ = x_sample.shape
    tp, ts = bp * lp, bs * ls
    t_all = tp + ts

    wm = _w_in_prep(w_in[i])
    wgu = jnp.pad(gla_w_gate_up[i], ((0, LANES - GLA_RANK), (0, 0))).astype(BF16)
    gbias = gla_gate_bias[i].reshape(1, QK_W)
    wbu, wc, a_re, a_im, dsk = _s5_params(s5_lam_re[i], s5_lam_im[i], s5_log_dt[i], s5_b_re[i], s5_b_im[i],
                                          s5_c_re[i], s5_c_im[i], s5_d[i])
    glu_w = s5_glu_w[i].astype(BF16)
    wo = w_out[i].astype(BF16)
    wr32 = jnp.pad(jnp.concatenate([router_group[i], router_expert[i]], axis=1),
                   ((0, 0), (0, LANES - N_EGROUPS - N_EXPERTS)))
    wr_hi = wr32.astype(BF16)
    wr = jnp.stack([wr_hi, (wr32 - wr_hi.astype(F32)).astype(BF16)])
    wpg = w_ple_gate[i].astype(BF16)
    wp = w_ple[i].astype(BF16)
    vec = lambda a: a.reshape(1, -1)

    xp = x_prompt.reshape(tp, D_MODEL)
    xs = x_sample.reshape(ts, D_MODEL)

    qp, kp, vp, rp, lap, up = _in_proj(xp, vec(norm_mix[i]), wm, wgu, gbias, BF16)
    qs, ks, vs, rs, las, us = _in_proj(xs, vec(norm_mix[i]), wm, wgu, gbias, BF16)
    ogp, gla_p = _gla_prompt(qp, kp, vp, rp, lap, vec(gla_norm[i]), bp, lp)
    ogs, gla_s = _gla_sample(qs, ks, vs, rs, las, vec(gla_norm[i]), state_gla[i], bs, ls)
    wbu2, wc2, a_re2, a_im2 = _s5_prompt_params(wbu, wc, a_re, a_im, bp)
    ysp, re_p, im_p = _s5_prompt(up.reshape(bp, lp, D_S5), wbu2, wc2, a_re2, a_im2, dsk, tc=512)
    yss, re_s, im_s = _s5(us.reshape(bs, ls, D_S5), wbu, wc, a_re, a_im, dsk,
                          state_s5_re[i].reshape(bs, -1), state_s5_im[i].reshape(bs, -1),
                          nb=32, tc=ls, has_state=True)

    h1, hn_all, rt_all, rtt, cnt = _mix_out(xp, xs, ogp, ogs, ysp.reshape(tp, D_S5), yss.reshape(ts, D_S5),
                                            glu_w, vec(s5_glu_b[i]), vec(s5_norm[i]), wo, vec(norm_ffn[i]), wr)

    plan, dest, zero_blk, n_slots = _moe_plan(rtt, cnt)
    x_slots = _dispatch(dest, zero_blk, hn_all, n_slots)
    y_slots = _moe(x_slots, plan, w_gate[i], w_up[i], w_down[i], n_slots)

    y_p, y_s = _ple_out(dest, h1, rt_all, p_prompt[i].reshape(tp, D_PLE), p_sample[i].reshape(ts, D_PLE),
                        vec(norm_ple[i]), wpg, wp, vec(norm_final), y_slots)

    s5shape = lambda a, b: a.reshape(1, b, S5_GROUPS, S5_STATE)
    return (y_p.reshape(bp, lp, D_MODEL), y_s.reshape(bs, ls, D_MODEL),
            gla_p[None], s5shape(re_p, bp), s5shape(im_p, bp),
            gla_s[None], s5shape(re_s, bs), s5shape(im_s, bs))
```

```python
import functools
import math

import jax
import jax.numpy as jnp
from jax import lax
from jax.experimental import pallas as pl
from jax.experimental.pallas import tpu as pltpu

F32 = jnp.float32
BF16 = jnp.bfloat16

D_MODEL = 2048
D_GLA = 1024
D_S5 = 1024
GLA_HEADS = 4
GLA_DV = 256
GLA_DK = 128
GLA_RANK = 16
GLA_CHUNK = 64
S5_GROUP = 16
S5_GROUPS = 64
S5_STATE = 64
N_EGROUPS = 4
N_EPG = 8
N_EXPERTS = 32
D_EXPERT = 512
D_PLE = 256
EPS = 1e-6

LANES = 128
QK_W = GLA_HEADS * GLA_DK
S5_GB = 8
S5_NGB = S5_GROUPS // S5_GB
S5_SL = S5_GB * S5_STATE
TOK_TILE = 256
MOE_BLK = 128
VMEM_LIMIT = 56 * 1024 * 1024
MOE_VMEM_LIMIT = 60 * 1024 * 1024


def _const_spec(shape):
    nd = len(shape)
    return pl.BlockSpec(shape, lambda *_: (0,) * nd, pipeline_mode=pl.Buffered(1))


def _rms(x, g):
    return x * lax.rsqrt(jnp.mean(x * x, axis=-1, keepdims=True) + EPS) * g


def _dot(a, b):
    return jnp.dot(a, b, preferred_element_type=F32)


def _log_sigmoid(x):
    return -(jnp.maximum(-x, 0.0) + jnp.log1p(jnp.exp(-jnp.abs(x))))


N_QKVR = 2 * QK_W + 2 * D_GLA
W_IN_COLS = N_QKVR + GLA_RANK + D_S5


def _w_in_prep_kernel(w_ref, o_ref):
    o_ref[:, 0:N_QKVR] = w_ref[:, 0:N_QKVR].astype(BF16)
    tail = w_ref[:, N_QKVR:W_IN_COLS]
    o_ref[:, N_QKVR:N_QKVR + D_S5] = tail[:, GLA_RANK:GLA_RANK + D_S5].astype(BF16)
    o_ref[:, N_QKVR + D_S5:N_QKVR + D_S5 + LANES] = tail[:, 0:LANES].astype(BF16)


def _w_in_prep(wi):
    rows = 256
    return pl.pallas_call(
        _w_in_prep_kernel,
        grid=(D_MODEL // rows,),
        in_specs=[pl.BlockSpec((rows, W_IN_COLS), lambda i: (i, 0))],
        out_specs=pl.BlockSpec((rows, N_QKVR + D_S5 + LANES), lambda i: (i, 0)),
        out_shape=jax.ShapeDtypeStruct((D_MODEL, N_QKVR + D_S5 + LANES), BF16),
        compiler_params=pltpu.CompilerParams(dimension_semantics=("arbitrary",), vmem_limit_bytes=VMEM_LIMIT),
        name="w_in_prep",
    )(wi)


def _in_proj_kernel(x_ref, g_ref, wm_ref, wgu_ref, gb_ref,
                    q_ref, k_ref, v_ref, r_ref, la_ref, u_ref):
    hb = _rms(x_ref[...], g_ref[...]).astype(BF16)

    def seg(a, b):
        return _dot(hb, wm_ref[:, a:b])

    q_ref[...] = (seg(0, QK_W) * (GLA_DK ** -0.5)).astype(q_ref.dtype)
    k_ref[...] = seg(QK_W, 2 * QK_W).astype(k_ref.dtype)
    v_ref[...] = seg(2 * QK_W, 2 * QK_W + D_GLA).astype(v_ref.dtype)
    r_ref[...] = seg(2 * QK_W + D_GLA, N_QKVR).astype(r_ref.dtype)
    u_ref[...] = seg(N_QKVR, N_QKVR + D_S5)
    zg = seg(N_QKVR + D_S5, N_QKVR + D_S5 + LANES)
    xg = _dot(zg.astype(BF16), wgu_ref[...]) + gb_ref[...]
    la_ref[...] = _log_sigmoid(xg) * (1.0 / 16.0)


def _in_proj(x2d, g, wm, wgu, gbias, act_dtype):
    t = x2d.shape[0]
    tm = TOK_TILE
    row = lambda w: pl.BlockSpec((tm, w), lambda i: (i, 0))
    return pl.pallas_call(
        _in_proj_kernel,
        grid=(t // tm,),
        in_specs=[row(D_MODEL), _const_spec((1, D_MODEL)), _const_spec(wm.shape),
                  _const_spec(wgu.shape), _const_spec((1, QK_W))],
        out_specs=[row(QK_W), row(QK_W), row(D_GLA), row(D_GLA), row(QK_W), row(D_S5)],
        out_shape=[jax.ShapeDtypeStruct((t, QK_W), act_dtype), jax.ShapeDtypeStruct((t, QK_W), act_dtype),
                   jax.ShapeDtypeStruct((t, D_GLA), act_dtype), jax.ShapeDtypeStruct((t, D_GLA), act_dtype),
                   jax.ShapeDtypeStruct((t, QK_W), F32), jax.ShapeDtypeStruct((t, D_S5), F32)],
        compiler_params=pltpu.CompilerParams(dimension_semantics=("arbitrary",), vmem_limit_bytes=VMEM_LIMIT),
        name="in_proj",
    )(x2d, g, wm, wgu, gbias)


_NT = (((1,), (1,)), ((), ()))
_TN = (((0,), (0,)), ((), ()))


def _gla_pre(q, k, la, c):
    r = q.shape[0]
    shift = int(math.log2(c))
    ri = lax.broadcasted_iota(jnp.int32, (r, r), 0)
    si = lax.broadcasted_iota(jnp.int32, (r, r), 1)
    mask = ((ri >> shift) == (si >> shift)) & (ri >= si)
    tri = jnp.where(mask, 1.0, 0.0).astype(BF16)
    hi = la.astype(BF16)
    r1 = la - hi.astype(F32)
    mid = r1.astype(BF16)
    lo = (r1 - mid.astype(F32)).astype(BF16)
    cum = _dot(tri, hi) + _dot(tri, mid) + _dot(tri, lo)
    last = jnp.concatenate([jnp.broadcast_to(cum[(i + 1) * c - 1:(i + 1) * c, :], (c, cum.shape[1]))
                            for i in range(r // c)], axis=0)
    qe = (q * jnp.exp(cum)).astype(BF16)
    ke = (k * jnp.exp(-cum)).astype(BF16)
    kd = (k * jnp.exp(last - cum)).astype(BF16)
    return qe, ke, kd, cum, mask


def _gla_intra(qe, ke, v, mask):
    sc = lax.dot_general(qe, ke, _NT, preferred_element_type=F32)
    return _dot(jnp.where(mask, sc, 0.0).astype(BF16), v)


def _gla_finish(o, r, g):
    rf = r.astype(F32)
    return _rms(o, g) * (rf * jax.nn.sigmoid(rf))


GLA_SAFE_LOG_DECAY = -60.0


def _gla_token_step(t, rows, q, k, v, a_all, st_ref, oacc_ref):
    m = rows == t
    a = jnp.sum(jnp.where(m, a_all, 0.0), axis=0, keepdims=True)
    kt = jnp.where(m, k, 0.0).astype(BF16)
    qt = jnp.where(m, q, 0.0).astype(BF16)
    vt = jnp.where(m, v, jnp.zeros_like(v))
    st = st_ref[...] * a + lax.dot_general(vt, kt, _TN, preferred_element_type=F32)
    st_ref[...] = st
    oacc_ref[...] += lax.dot_general(qt, st.astype(BF16), _NT, preferred_element_type=F32)


def _gla_prompt_kernel(q_ref, k_ref, v_ref, r_ref, la_ref, g_ref, o_ref, sfin_ref, st_ref, oacc_ref,
                       *, n_chunks):
    j = pl.program_id(1)

    @pl.when(j == 0)
    def _():
        st_ref[...] = jnp.zeros_like(st_ref)

    c = GLA_CHUNK
    qe, ke, kd, cum, mask = _gla_pre(q_ref[...].astype(F32), k_ref[...].astype(F32), la_ref[...], c)
    g = g_ref[...]
    safe = jnp.min(cum) >= GLA_SAFE_LOG_DECAY

    @pl.when(safe)
    def _():
        for h in range(GLA_HEADS):
            kc = slice(h * GLA_DK, (h + 1) * GLA_DK)
            vc = slice(h * GLA_DV, (h + 1) * GLA_DV)
            v = v_ref[:, vc]
            qe_h, kd_h = qe[:, kc], kd[:, kc]
            o = _gla_intra(qe_h, ke[:, kc], v, mask)
            st = st_ref[h]
            inter = []
            for ci in range(n_chunks):
                rows = slice(ci * c, (ci + 1) * c)
                inter.append(lax.dot_general(qe_h[rows], st.astype(BF16), _NT, preferred_element_type=F32))
                dec = jnp.exp(cum[(ci + 1) * c - 1:(ci + 1) * c, kc])
                st = st * dec + lax.dot_general(v[rows], kd_h[rows], _TN, preferred_element_type=F32)
            st_ref[h] = st
            o = o + jnp.concatenate(inter, axis=0)
            o_ref[:, vc] = _gla_finish(o, r_ref[:, vc], g).astype(o_ref.dtype)

    @pl.when(jnp.logical_not(safe))
    def _():
        n_rows = n_chunks * c
        rows = lax.broadcasted_iota(jnp.int32, (n_rows, 1), 0)
        for h in range(GLA_HEADS):
            kc = slice(h * GLA_DK, (h + 1) * GLA_DK)
            vc = slice(h * GLA_DV, (h + 1) * GLA_DV)
            q, k, v = q_ref[:, kc].astype(F32), k_ref[:, kc].astype(F32), v_ref[:, vc]
            a_all = jnp.exp(la_ref[:, kc])
            oacc_ref[...] = jnp.zeros_like(oacc_ref)

            def step(t, carry):
                _gla_token_step(t, rows, q, k, v, a_all, st_ref.at[h], oacc_ref)
                return carry

            lax.fori_loop(0, n_rows, step, 0)
            o_ref[:, vc] = _gla_finish(oacc_ref[...], r_ref[:, vc], g).astype(o_ref.dtype)

    @pl.when(j == pl.num_programs(1) - 1)
    def _():
        for h in range(GLA_HEADS):
            sfin_ref[0, h] = st_ref[h].T


def _gla_prompt(q, k, v, r, la, g, batch, seq):
    rb = 4 * GLA_CHUNK
    nj = seq // rb
    row = lambda w: pl.BlockSpec((rb, w), lambda b, j: (b * nj + j, 0))
    return pl.pallas_call(
        functools.partial(_gla_prompt_kernel, n_chunks=rb // GLA_CHUNK),
        grid=(batch, nj),
        in_specs=[row(QK_W), row(QK_W), row(D_GLA), row(D_GLA), row(QK_W), _const_spec((1, GLA_DV))],
        out_specs=[row(D_GLA),
                   pl.BlockSpec((1, GLA_HEADS, GLA_DK, GLA_DV), lambda b, j: (b, 0, 0, 0))],
        out_shape=[jax.ShapeDtypeStruct((batch * seq, D_GLA), BF16),
                   jax.ShapeDtypeStruct((batch, GLA_HEADS, GLA_DK, GLA_DV), F32)],
        scratch_shapes=[pltpu.VMEM((GLA_HEADS, GLA_DV, GLA_DK), F32), pltpu.VMEM((rb, GLA_DV), F32)],
        compiler_params=pltpu.CompilerParams(dimension_semantics=("arbitrary", "arbitrary"),
                                             vmem_limit_bytes=VMEM_LIMIT),
        name="gla_prompt",
    )(q, k, v, r, la, g)


def _gla_sample_kernel(q_ref, k_ref, v_ref, r_ref, la_ref, g_ref, s0_ref, o_ref, sfin_ref, st_ref, oacc_ref,
                       *, n_seq, seq):
    qe, ke, kd, cum, mask = _gla_pre(q_ref[...].astype(F32), k_ref[...].astype(F32), la_ref[...], seq)
    g = g_ref[...]
    safe = jnp.min(cum) >= GLA_SAFE_LOG_DECAY

    @pl.when(safe)
    def _():
        lasts = jnp.concatenate([cum[(s + 1) * seq - 1:(s + 1) * seq, :] for s in range(n_seq)]
                                + [jnp.zeros((GLA_DK - n_seq, cum.shape[1]), F32)], axis=0)
        pair = 2 * seq
        upper = lax.broadcasted_iota(jnp.int32, (pair, GLA_DK), 0) < seq
        for h in range(GLA_HEADS):
            kc = slice(h * GLA_DK, (h + 1) * GLA_DK)
            vc = slice(h * GLA_DV, (h + 1) * GLA_DV)
            v = v_ref[:, vc]
            qe_h, kd_h = qe[:, kc], kd[:, kc]
            o = _gla_intra(qe_h, ke[:, kc], v, mask)
            dec_t = jnp.exp(lasts[:, kc].T)
            inter = []
            for p in range(n_seq // 2):
                rows = slice(p * pair, (p + 1) * pair)
                qe_p, kd_p, v_p = qe_h[rows], kd_h[rows], v[rows]
                for half in range(2):
                    s = 2 * p + half
                    s0 = s0_ref[s, h]
                    o_s = _dot(qe_p, s0.astype(BF16))
                    inter.append(o_s[half * seq:(half + 1) * seq])
                    kd_s = jnp.where(upper if half == 0 else ~upper, kd_p, jnp.zeros_like(kd_p))
                    dec = jnp.broadcast_to(dec_t[:, s:s + 1], (GLA_DK, GLA_DV))
                    sfin_ref[s, h] = s0 * dec + lax.dot_general(kd_s, v_p, _TN, preferred_element_type=F32)
            o = o + jnp.concatenate(inter, axis=0)
            o_ref[:, vc] = _gla_finish(o, r_ref[:, vc], g).astype(o_ref.dtype)

    @pl.when(jnp.logical_not(safe))
    def _():
        n_rows = n_seq * seq
        rows = lax.broadcasted_iota(jnp.int32, (n_rows, 1), 0)
        for h in range(GLA_HEADS):
            kc = slice(h * GLA_DK, (h + 1) * GLA_DK)
            vc = slice(h * GLA_DV, (h + 1) * GLA_DV)
            q, k, v = q_ref[:, kc].astype(F32), k_ref[:, kc].astype(F32), v_ref[:, vc]
            a_all = jnp.exp(la_ref[:, kc])
            oacc_ref[...] = jnp.zeros_like(oacc_ref)

            def step(t, carry):
                s = t // seq

                @pl.when(t % seq == 0)
                def _():
                    st_ref[...] = s0_ref[s, h].T

                _gla_token_step(t, rows, q, k, v, a_all, st_ref, oacc_ref)

                @pl.when(t % seq == seq - 1)
                def _():
                    sfin_ref[s, h] = st_ref[...].T

                return carry

            lax.fori_loop(0, n_rows, step, 0)
            o_ref[:, vc] = _gla_finish(oacc_ref[...], r_ref[:, vc], g).astype(o_ref.dtype)


def _gla_sample(q, k, v, r, la, g, s0, batch, seq):
    ns = 16
    rb = ns * seq
    row = lambda w: pl.BlockSpec((rb, w), lambda i: (i, 0))
    st = pl.BlockSpec((ns, GLA_HEADS, GLA_DK, GLA_DV), lambda i: (i, 0, 0, 0))
    return pl.pallas_call(
        functools.partial(_gla_sample_kernel, n_seq=ns, seq=seq),
        grid=(batch // ns,),
        in_specs=[row(QK_W), row(QK_W), row(D_GLA), row(D_GLA), row(QK_W), _const_spec((1, GLA_DV)), st],
        out_specs=[row(D_GLA), st],
        out_shape=[jax.ShapeDtypeStruct((batch * seq, D_GLA), BF16),
                   jax.ShapeDtypeStruct((batch, GLA_HEADS, GLA_DK, GLA_DV), F32)],
        scratch_shapes=[pltpu.VMEM((GLA_DV, GLA_DK), F32), pltpu.VMEM((rb, GLA_DV), F32)],
        compiler_params=pltpu.CompilerParams(dimension_semantics=("arbitrary",), vmem_limit_bytes=VMEM_LIMIT),
        name="gla_sample",
    )(q, k, v, r, la, g, s0)


def _s5_kernel(u_ref, wbu_ref, wc_ref, are_ref, aim_ref, d_ref, h0r_ref, h0i_ref,
               y_ref, sre_ref, sim_ref, bu_ref, xs_ref, car_ref, *, nb, tc, has_state):
    j = pl.program_id(2)

    @pl.when(j == 0)
    def _():
        if has_state:
            car_ref[0] = h0r_ref[...]
            car_ref[1] = h0i_ref[...]
        else:
            car_ref[...] = jnp.zeros_like(car_ref)

    u2 = u_ref[...].reshape(nb * tc, LANES)
    ub = u2.astype(BF16)
    nl = S5_SL // LANES
    for l in range(2 * nl):
        bu_ref[l] = _dot(ub, wbu_ref[0, :, l * LANES:(l + 1) * LANES])
    a_r = [jnp.broadcast_to(are_ref[0, :, l * LANES:(l + 1) * LANES], (nb, LANES)) for l in range(nl)]
    a_i = [jnp.broadcast_to(aim_ref[0, :, l * LANES:(l + 1) * LANES], (nb, LANES)) for l in range(nl)]

    def step(t, carry):
        rows = pl.ds(t, nb, stride=tc)
        out = []
        for l in range(nl):
            xr, xi = carry[2 * l], carry[2 * l + 1]
            nr = a_r[l] * xr - a_i[l] * xi + bu_ref[l, rows, :]
            ni = a_r[l] * xi + a_i[l] * xr + bu_ref[nl + l, rows, :]
            xs_ref[l, rows, :] = nr
            xs_ref[nl + l, rows, :] = ni
            out += [nr, ni]
        return tuple(out)

    init = []
    for l in range(nl):
        init += [car_ref[0, :, l * LANES:(l + 1) * LANES], car_ref[1, :, l * LANES:(l + 1) * LANES]]
    fin = lax.fori_loop(0, tc, step, tuple(init), unroll=8)
    xr = jnp.concatenate([fin[2 * l] for l in range(nl)], axis=1)
    xi = jnp.concatenate([fin[2 * l + 1] for l in range(nl)], axis=1)
    car_ref[0] = xr
    car_ref[1] = xi
    y = d_ref[0] * u2
    for l in range(2 * nl):
        y = y + _dot(xs_ref[l].astype(BF16), wc_ref[0, l * LANES:(l + 1) * LANES, :])
    y_ref[...] = y.reshape(nb, tc, LANES)

    @pl.when(j == pl.num_programs(2) - 1)
    def _():
        sre_ref[...] = xr
        sim_ref[...] = xi


def _s5(u3d, wbu, wc, a_re, a_im, dsk, h0r, h0i, nb, tc, has_state):
    batch, seq, _ = u3d.shape
    grid = (S5_NGB, batch // nb, seq // tc)
    st = pl.BlockSpec((nb, S5_SL), lambda g, b, j: (b, g))
    par = lambda w: pl.BlockSpec((1, 1, w), lambda g, b, j: (g, 0, 0))
    ublk = pl.BlockSpec((nb, tc, LANES), lambda g, b, j: (b, j, g))
    return pl.pallas_call(
        functools.partial(_s5_kernel, nb=nb, tc=tc, has_state=has_state),
        grid=grid,
        in_specs=[ublk,
                  pl.BlockSpec((1, LANES, 2 * S5_SL), lambda g, b, j: (g, 0, 0)),
                  pl.BlockSpec((1, 2 * S5_SL, LANES), lambda g, b, j: (g, 0, 0)),
                  par(S5_SL), par(S5_SL), par(LANES), st, st],
        out_specs=[ublk, st, st],
        out_shape=[jax.ShapeDtypeStruct(u3d.shape, F32),
                   jax.ShapeDtypeStruct((batch, S5_GROUPS * S5_STATE), F32),
                   jax.ShapeDtypeStruct((batch, S5_GROUPS * S5_STATE), F32)],
        scratch_shapes=[pltpu.VMEM((2 * S5_SL // LANES, nb * tc, LANES), F32),
                        pltpu.VMEM((2 * S5_SL // LANES, nb * tc, LANES), F32),
                        pltpu.VMEM((2, nb, S5_SL), F32)],
        compiler_params=pltpu.CompilerParams(dimension_semantics=("arbitrary",) * 3,
                                             vmem_limit_bytes=VMEM_LIMIT),
        name="s5_state" if has_state else "s5_zero",
    )(u3d, wbu, wc, a_re, a_im, dsk, h0r, h0i)


S5_J = 2
S5_HL = S5_SL // S5_J


def _s5_prompt_kernel(u_ref, wbu_ref, wc_ref, are_ref, aim_ref, d_ref, y_ref, sre_ref, sim_ref,
                      u2_ref, lhs_ref, bu_ref, xs_ref, y2_ref, yo_ref, car_ref, *, nb, tc):
    g = pl.program_id(0)
    j = pl.program_id(1)
    rows = nb * S5_J

    @pl.when((g == 0) & (j == 0))
    def _():
        lhs_ref[...] = jnp.zeros_like(lhs_ref)

    @pl.when(j == 0)
    def _():
        car_ref[...] = jnp.zeros_like(car_ref)

    u2 = u_ref[...].reshape(nb * tc, LANES)
    u2_ref[...] = u2

    def build(t, c):
        u4 = u2_ref[pl.ds(t, nb, stride=tc), :]
        for jj in range(S5_J):
            lhs_ref[t, jj * nb:(jj + 1) * nb, jj * LANES:(jj + 1) * LANES] = u4
        return c

    lax.fori_loop(0, tc, build, 0, unroll=8)
    lhs = lhs_ref[...].reshape(tc * rows, S5_J * LANES).astype(BF16)
    bu_ref[...] = _dot(lhs, wbu_ref[0]).reshape(tc, rows, 2 * S5_HL)
    ar = are_ref[0]
    ai = aim_ref[0]

    def step(t, carry):
        xr, xi = carry
        tile = bu_ref[t]
        nr = ar * xr - ai * xi + tile[:, 0:S5_HL]
        ni = ar * xi + ai * xr + tile[:, S5_HL:2 * S5_HL]
        xs_ref[t] = jnp.concatenate([nr, ni], axis=1)
        return nr, ni

    xr, xi = lax.fori_loop(0, tc, step, (car_ref[0], car_ref[1]), unroll=8)
    car_ref[0] = xr
    car_ref[1] = xi
    xs = xs_ref[...].reshape(tc * rows, 2 * S5_HL).astype(BF16)
    y2_ref[...] = _dot(xs, wc_ref[0]).reshape(tc, rows, S5_J * LANES)
    first_half = lax.broadcasted_iota(jnp.int32, (rows, LANES), 0) < nb

    def unperm(t, c):
        t2 = y2_ref[t]
        part = jnp.where(first_half, t2[:, 0:LANES], t2[:, LANES:2 * LANES])
        tot = part + pltpu.roll(part, nb, axis=0)
        yo_ref[pl.ds(t, nb, stride=tc), :] = tot[0:nb]
        return c

    lax.fori_loop(0, tc, unperm, 0, unroll=8)
    y_ref[...] = (yo_ref[...] + d_ref[0] * u2).reshape(nb, tc, LANES)

    @pl.when(j == pl.num_programs(1) - 1)
    def _():
        sre_ref[...] = jnp.concatenate([xr[jj * nb:(jj + 1) * nb] for jj in range(S5_J)], axis=1)
        sim_ref[...] = jnp.concatenate([xi[jj * nb:(jj + 1) * nb] for jj in range(S5_J)], axis=1)


def _s5_prompt(u3d, wbu2, wc2, a_re2, a_im2, dsk, tc):
    nb, seq, _ = u3d.shape
    rows = nb * S5_J
    assert rows == 8
    st = pl.BlockSpec((nb, S5_SL), lambda g, j: (0, g))
    gblk = lambda s: pl.BlockSpec((1,) + s, lambda g, j: (g, 0, 0))
    ublk = pl.BlockSpec((nb, tc, LANES), lambda g, j: (0, j, g))
    return pl.pallas_call(
        functools.partial(_s5_prompt_kernel, nb=nb, tc=tc),
        grid=(S5_NGB, seq // tc),
        in_specs=[ublk, gblk((S5_J * LANES, 2 * S5_HL)), gblk((2 * S5_HL, S5_J * LANES)),
                  gblk((rows, S5_HL)), gblk((rows, S5_HL)), gblk((1, LANES))],
        out_specs=[ublk, st, st],
        out_shape=[jax.ShapeDtypeStruct(u3d.shape, F32),
                   jax.ShapeDtypeStruct((nb, S5_GROUPS * S5_STATE), F32),
                   jax.ShapeDtypeStruct((nb, S5_GROUPS * S5_STATE), F32)],
        scratch_shapes=[pltpu.VMEM((nb * tc, LANES), F32),
                        pltpu.VMEM((tc, rows, S5_J * LANES), F32),
                        pltpu.VMEM((tc, rows, 2 * S5_HL), F32),
                        pltpu.VMEM((tc, rows, 2 * S5_HL), F32),
                        pltpu.VMEM((tc, rows, S5_J * LANES), F32),
                        pltpu.VMEM((nb * tc, LANES), F32),
                        pltpu.VMEM((2, rows, S5_HL), F32)],
        compiler_params=pltpu.CompilerParams(dimension_semantics=("arbitrary",) * 2,
                                             vmem_limit_bytes=VMEM_LIMIT),
        name="s5_prompt",
    )(u3d, wbu2, wc2, a_re2, a_im2, dsk)


def _s5_prompt_params(wbu, wc, a_re, a_im, nb):
    h = S5_HL
    top = jnp.concatenate([wbu[:, :, 0:h], wbu[:, :, S5_SL:S5_SL + h]], axis=2)
    bot = jnp.concatenate([wbu[:, :, h:2 * h], wbu[:, :, S5_SL + h:S5_SL + 2 * h]], axis=2)
    wbu2 = jnp.concatenate([top, bot], axis=1)
    wc_j = [jnp.concatenate([wc[:, jj * h:(jj + 1) * h, :], wc[:, S5_SL + jj * h:S5_SL + (jj + 1) * h, :]],
                            axis=1) for jj in range(S5_J)]
    wc2 = jnp.concatenate(wc_j, axis=2)
    tile = lambda a: jnp.repeat(a.reshape(S5_NGB, S5_J, h), nb, axis=1)
    return wbu2, wc2, tile(a_re), tile(a_im)


def _mix_out_kernel(xp_ref, xs_ref, ogp_ref, ogs_ref, ysp_ref, yss_ref, glu_w_ref, glu_b_ref, s5n_ref, wo_ref,
                    nffn_ref, wr_ref, h1_ref, hn_ref, rt_ref, rtt_ref, cnt_ref, *, n_prompt_tiles):
    is_p = pl.program_id(0) < n_prompt_tiles
    x = jnp.where(is_p, xp_ref[...], xs_ref[...])
    og = jnp.where(is_p, ogp_ref[...], ogs_ref[...])
    y = jax.nn.gelu(jnp.where(is_p, ysp_ref[...], yss_ref[...]))
    y = y * jax.nn.sigmoid(_dot(y.astype(BF16), glu_w_ref[...]) + glu_b_ref[...])
    yn = _rms(y, s5n_ref[...]).astype(BF16)
    mix = _dot(og, wo_ref[0:D_GLA, :]) + _dot(yn, wo_ref[D_GLA:D_GLA + D_S5, :])
    h1 = x + mix
    h1_ref[...] = h1
    hn = _rms(h1, nffn_ref[...])
    hn_ref[...] = _pack_bf16_pair(hn[:, 0:HALF], hn[:, HALF:D_MODEL])
    hn_hi = hn.astype(BF16)
    hn_lo = (hn - hn_hi.astype(F32)).astype(BF16)
    logits = _dot(hn_hi, wr_ref[0]) + _dot(hn_hi, wr_ref[1]) + _dot(hn_lo, wr_ref[0])
    rt = _route(logits)

    @pl.when(pl.program_id(0) == 0)
    def _():
        cnt_ref[...] = jnp.zeros_like(cnt_ref)

    tm = rt.shape[0]
    lane = lax.broadcasted_iota(jnp.int32, rt.shape, 1).astype(F32)
    oh0 = lane == rt[:, 0:1]
    oh1 = lane == rt[:, 1:2]
    both = jnp.where(oh0 | oh1, 1.0, 0.0)
    ri = lax.broadcasted_iota(jnp.int32, (tm, tm), 0)
    ci = lax.broadcasted_iota(jnp.int32, (tm, tm), 1)
    before = _dot(jnp.where(ri > ci, 1.0, 0.0).astype(BF16), both.astype(BF16)) + cnt_ref[...]
    rank0 = jnp.sum(jnp.where(oh0, before, 0.0), axis=-1, keepdims=True)
    rank1 = jnp.sum(jnp.where(oh1, before, 0.0), axis=-1, keepdims=True)
    cnt_ref[...] += jnp.sum(both, axis=0, keepdims=True)
    rt = jnp.where(lane == 4.0, rank0, jnp.where(lane == 5.0, rank1, rt))
    rt_ref[...] = rt
    rtt_ref[...] = rt.T[0:8, :]


def _route(logits):
    col = lax.broadcasted_iota(jnp.int32, logits.shape, 1)
    colf = col.astype(F32)
    neg = -jnp.inf

    def first_argmax(vals):
        m = jnp.max(vals, axis=-1, keepdims=True)
        idx = jnp.min(jnp.where(vals == m, colf, float(LANES)), axis=-1, keepdims=True)
        return m, idx

    lg = jnp.where(col < N_EGROUPS, logits, neg)
    gmax, gsel = first_argmax(lg)
    p_g = 1.0 / jnp.sum(jnp.exp(lg - gmax), axis=-1, keepdims=True)
    ecol = col - N_EGROUPS
    egrp = (ecol >> 3).astype(F32)
    in_group = (ecol >= 0) & (ecol < N_EXPERTS) & (egrp == gsel)
    le = jnp.where(in_group, logits, neg)
    m1, i1 = first_argmax(le)
    le2 = jnp.where(colf == i1, neg, le)
    m2, i2 = first_argmax(le2)
    e2 = jnp.exp(m2 - m1)
    den = 1.0 + e2
    w1 = p_g * (1.0 / den)
    w2 = p_g * (e2 / den)
    e1f = i1 - float(N_EGROUPS)
    e2f = i2 - float(N_EGROUPS)
    out = jnp.where(col == 0, e1f, jnp.where(col == 1, e2f, jnp.where(col == 2, w1, jnp.where(col == 3, w2, 0.0))))
    return out


def _mix_out(xp, xs, ogp, ogs, ysp, yss, glu_w, glu_b, s5n, wo, nffn, wr):
    tm = TOK_TILE
    npt, nst = xp.shape[0] // tm, xs.shape[0] // tm
    t = (npt + nst) * tm
    row = lambda w: pl.BlockSpec((tm, w), lambda i: (i, 0))
    prow = lambda w: pl.BlockSpec((tm, w), lambda i: (jnp.minimum(i, npt - 1), 0))
    srow = lambda w: pl.BlockSpec((tm, w), lambda i: (jnp.maximum(i - npt, 0), 0))
    return pl.pallas_call(
        functools.partial(_mix_out_kernel, n_prompt_tiles=npt),
        grid=(npt + nst,),
        in_specs=[prow(D_MODEL), srow(D_MODEL), prow(D_GLA), srow(D_GLA), prow(D_S5), srow(D_S5),
                  _const_spec(glu_w.shape), _const_spec((1, D_S5)), _const_spec((1, D_S5)),
                  _const_spec(wo.shape), _const_spec((1, D_MODEL)), _const_spec(wr.shape)],
        out_specs=[row(D_MODEL), row(HALF), row(LANES), pl.BlockSpec((8, tm), lambda i: (0, i)),
                   pl.BlockSpec((1, LANES), lambda i: (0, 0))],
        out_shape=[jax.ShapeDtypeStruct((t, D_MODEL), F32), jax.ShapeDtypeStruct((t, HALF), jnp.uint32),
                   jax.ShapeDtypeStruct((t, LANES), F32), jax.ShapeDtypeStruct((8, t), F32),
                   jax.ShapeDtypeStruct((1, LANES), F32)],
        compiler_params=pltpu.CompilerParams(dimension_semantics=("arbitrary",), vmem_limit_bytes=VMEM_LIMIT),
        name="mix_out",
    )(xp, xs, ogp, ogs, ysp, yss, glu_w, glu_b, s5n, wo, nffn, wr)


DMA_UNROLL = 8


MOE_SB = 6
HALF = D_MODEL // 2


def _pack_bf16_pair(lo, hi):
    def bits(x):
        b = pltpu.bitcast(x, jnp.uint32)
        return (b + jnp.uint32(0x7FFF) + ((b >> 16) & jnp.uint32(1))) >> 16
    return bits(lo) | (bits(hi) << 16)


def _unpack_bf16_pair(u):
    return (pltpu.bitcast(u << 16, F32), pltpu.bitcast(u & jnp.uint32(0xFFFF0000), F32))


def _moe_kernel(se_ref, sm_ref, sn_ref, sbase_ref, xs_hbm, wg_ref, wu_ref, wd_ref, out_hbm,
                xb_ref, yb_ref, wgb_ref, wub_ref, wdb_ref, gsem, ssem):
    b = pl.program_id(0)
    nb = pl.num_programs(0)
    n = sn_ref[b]
    slot = lax.rem(b, 2)

    def issue_gather(blk, sl):
        base = pl.multiple_of(sbase_ref[blk], MOE_BLK)
        for m in range(1, MOE_SB + 1):
            @pl.when((sn_ref[blk] > 0) & (sm_ref[blk] == m))
            def _():
                rows = m * MOE_BLK
                pltpu.make_async_copy(xs_hbm.at[pl.ds(base, rows), :], xb_ref.at[sl, pl.ds(0, rows), :],
                                      gsem.at[sl]).start()

    def wait_gather(blocks, sl):
        rows = pl.ds(0, pl.multiple_of(blocks * MOE_BLK, MOE_BLK))
        pltpu.make_async_copy(xs_hbm.at[rows, :], xb_ref.at[sl, rows, :], gsem.at[sl]).wait()

    m_cur = sm_ref[b]
    m_prev = sm_ref[jnp.maximum(b - 1, 0)]
    has_prev = (b > 0) & (m_prev > 0)

    def out_copy(rows):
        dst = pl.ds(pl.multiple_of(sbase_ref[b], MOE_BLK), rows)
        return pltpu.make_async_copy(yb_ref.at[pl.ds(0, rows), :], out_hbm.at[dst, :], ssem.at[0])

    def wait_prev_out():
        rows = pl.ds(0, pl.multiple_of(m_prev * MOE_BLK, MOE_BLK))
        pltpu.make_async_copy(yb_ref.at[rows, :], out_hbm.at[rows, :], ssem.at[0]).wait()

    def write_out(rows):
        out_copy(rows).start()

        @pl.when(b == nb - 1)
        def _():
            out_copy(rows).wait()

    @pl.when(b == 0)
    def _():
        issue_gather(0, 0)

    @pl.when(b + 1 < nb)
    def _():
        issue_gather(b + 1, 1 - slot)

    def compute(rows):
        x_lo, x_hi = _unpack_bf16_pair(xb_ref[slot, 0:rows, :])
        x_lo, x_hi = x_lo.astype(BF16), x_hi.astype(BF16)
        gate = _dot(x_lo, wgb_ref[0:HALF, :]) + _dot(x_hi, wgb_ref[HALF:D_MODEL, :])
        up = _dot(x_lo, wub_ref[0:HALF, :]) + _dot(x_hi, wub_ref[HALF:D_MODEL, :])
        hid = (gate * jax.nn.sigmoid(gate) * up).astype(BF16)

        @pl.when(has_prev)
        def _():
            wait_prev_out()

        yb_ref[0:rows, :] = _pack_bf16_pair(_dot(hid, wdb_ref[:, 0:HALF]), _dot(hid, wdb_ref[:, HALF:D_MODEL]))
        write_out(rows)

    @pl.when(n > 0)
    def _():
        prev_e = se_ref[jnp.maximum(b - 1, 0)]

        @pl.when((b == 0) | (prev_e != se_ref[b]))
        def _():
            wgb_ref[...] = wg_ref[...].astype(BF16)
            wub_ref[...] = wu_ref[...].astype(BF16)
            wdb_ref[...] = wd_ref[...].astype(BF16)

        wait_gather(m_cur, slot)
        for m in range(1, MOE_SB + 1):
            @pl.when(m_cur == m)
            def _():
                compute(m * MOE_BLK)

    @pl.when((n == 0) & (m_cur > 0))
    def _():
        @pl.when(has_prev)
        def _():
            wait_prev_out()

        yb_ref[...] = jnp.zeros_like(yb_ref)
        for m in range(1, MOE_SB + 1):
            @pl.when(m_cur == m)
            def _():
                write_out(m * MOE_BLK)

    @pl.when((m_cur == 0) & has_prev)
    def _():
        wait_prev_out()


def _moe(x_slots, plan, w_gate, w_up, w_down, n_out_rows):
    sb_e, sb_m, sb_n, sb_base = plan
    sb_rows = MOE_SB * MOE_BLK
    wspec = lambda s: pl.BlockSpec((None,) + s, lambda b, se, *_: (se[b], 0, 0))
    grid_spec = pltpu.PrefetchScalarGridSpec(
        num_scalar_prefetch=4,
        grid=(sb_e.shape[0],),
        in_specs=[pl.BlockSpec(memory_space=pl.ANY),
                  wspec((D_MODEL, D_EXPERT)), wspec((D_MODEL, D_EXPERT)), wspec((D_EXPERT, D_MODEL))],
        out_specs=pl.BlockSpec(memory_space=pl.ANY),
        scratch_shapes=[pltpu.VMEM((2, sb_rows, HALF), jnp.uint32), pltpu.VMEM((sb_rows, HALF), jnp.uint32),
                        pltpu.VMEM((D_MODEL, D_EXPERT), BF16), pltpu.VMEM((D_MODEL, D_EXPERT), BF16),
                        pltpu.VMEM((D_EXPERT, D_MODEL), BF16), pltpu.SemaphoreType.DMA((2,)),
                        pltpu.SemaphoreType.DMA((1,))],
    )
    return pl.pallas_call(
        _moe_kernel,
        grid_spec=grid_spec,
        out_shape=jax.ShapeDtypeStruct((n_out_rows, HALF), jnp.uint32),
        compiler_params=pltpu.CompilerParams(dimension_semantics=("arbitrary",),
                                             vmem_limit_bytes=MOE_VMEM_LIMIT),
        name="moe_experts",
    )(sb_e, sb_m, sb_n, sb_base, x_slots, w_gate, w_up, w_down)


def _dispatch_kernel(dest_ref, zb_ref, hn_ref, out_hbm, buf_ref, zero_ref, sem, zsem):
    i = pl.program_id(0)
    nt = pl.num_programs(0)
    tm = hn_ref.shape[0]
    t_all = nt * tm
    slot = lax.rem(i, 2)
    nblk = zb_ref.shape[0]

    def zero_copy(blk):
        dst = pl.ds(pl.multiple_of(blk * MOE_BLK, MOE_BLK), MOE_BLK)
        return pltpu.make_async_copy(zero_ref, out_hbm.at[dst, :], zsem.at[0])

    @pl.when(i == 0)
    def _():
        zero_ref[...] = jnp.zeros_like(zero_ref)

        def start(blk, c):
            @pl.when(zb_ref[blk] == 1)
            def _():
                zero_copy(blk).start()
            return c

        def wait(blk, c):
            @pl.when(zb_ref[blk] == 1)
            def _():
                zero_copy(blk).wait()
            return c

        lax.fori_loop(0, nblk, start, 0)
        lax.fori_loop(0, nblk, wait, 0)

    def wait_rows(sl):
        for k in range(2):
            pltpu.make_async_copy(buf_ref.at[sl], out_hbm.at[pl.ds(0, tm), :], sem.at[sl]).wait()

    @pl.when(i >= 2)
    def _():
        wait_rows(slot)

    buf_ref[slot] = hn_ref[...]
    base = i * tm
    for a in range(2 * tm):
        row, k = a // 2, a % 2
        d = dest_ref[k * t_all + base + row]
        pltpu.make_async_copy(buf_ref.at[slot, pl.ds(row, 1), :], out_hbm.at[pl.ds(d, 1), :],
                              sem.at[slot]).start()

    @pl.when(i == nt - 1)
    def _():
        wait_rows(slot)

        @pl.when(nt > 1)
        def _():
            wait_rows(1 - slot)


def _dispatch(dest, zero_blk, hn_pk, n_slots):
    tm = TOK_TILE
    grid_spec = pltpu.PrefetchScalarGridSpec(
        num_scalar_prefetch=2,
        grid=(hn_pk.shape[0] // tm,),
        in_specs=[pl.BlockSpec((tm, HALF), lambda i, d, z: (i, 0))],
        out_specs=pl.BlockSpec(memory_space=pl.ANY),
        scratch_shapes=[pltpu.VMEM((2, tm, HALF), jnp.uint32), pltpu.VMEM((MOE_BLK, HALF), jnp.uint32),
                        pltpu.SemaphoreType.DMA((2,)), pltpu.SemaphoreType.DMA((1,))],
    )
    return pl.pallas_call(
        _dispatch_kernel,
        grid_spec=grid_spec,
        out_shape=jax.ShapeDtypeStruct((n_slots, HALF), jnp.uint32),
        compiler_params=pltpu.CompilerParams(dimension_semantics=("arbitrary",), vmem_limit_bytes=VMEM_LIMIT),
        name="moe_dispatch",
    )(dest, zero_blk, hn_pk)


def _moe_plan(rtt, cnt):
    t_all = rtt.shape[1]
    n_assign = 2 * t_all
    e_flat = rtt[0:2].astype(jnp.int32).reshape(-1)
    rank = rtt[4:6].astype(jnp.int32).reshape(-1)
    counts = cnt[0, 0:N_EXPERTS].astype(jnp.int32)
    padded = (counts + MOE_BLK - 1) // MOE_BLK * MOE_BLK
    pad_end = jnp.cumsum(padded)
    pad_start = pad_end - padded
    before = jnp.arange(N_EXPERTS, dtype=jnp.int32)[:, None] < e_flat[None, :]
    dest = rank + jnp.sum(jnp.where(before, padded[:, None], 0), axis=0)
    nblk = -(-(n_assign + N_EXPERTS * (MOE_BLK - 1)) // MOE_BLK)
    n_slots = nblk * MOE_BLK
    blk = jnp.arange(nblk, dtype=jnp.int32)
    last_of_expert = jnp.any((blk[:, None] == (pad_end // MOE_BLK - 1)[None, :]) & (counts > 0)[None, :], axis=1)
    zero_blk = (last_of_expert | (blk >= pad_end[-1] // MOE_BLK)).astype(jnp.int32)
    k_e = padded // MOE_BLK
    sbc = (k_e + MOE_SB - 1) // MOE_SB
    sb_end = jnp.cumsum(sbc)
    sb_start = sb_end - sbc
    n_sb = (nblk + (MOE_SB - 1) * N_EXPERTS) // MOE_SB
    s = jnp.arange(n_sb, dtype=jnp.int32)
    sb_e = jnp.minimum(jnp.sum((s[:, None] >= sb_end[None, :]).astype(jnp.int32), axis=1), N_EXPERTS - 1)
    j = s - sb_start[sb_e]
    real = s < sb_end[-1]
    tail_blk = pad_end[-1] // MOE_BLK + MOE_SB * (s - sb_end[-1])
    sb_m = jnp.where(real, jnp.clip(k_e[sb_e] - MOE_SB * j, 0, MOE_SB), jnp.clip(nblk - tail_blk, 0, MOE_SB))
    sb_n = jnp.where(real, jnp.clip(counts[sb_e] - MOE_SB * MOE_BLK * j, 0, MOE_SB * MOE_BLK), 0)
    sb_base = jnp.where(real, pad_start[sb_e] + MOE_SB * MOE_BLK * j, jnp.minimum(tail_blk, nblk - 1) * MOE_BLK)
    i32 = lambda a: a.astype(jnp.int32)
    return (i32(sb_e), i32(sb_m), i32(sb_n), i32(sb_base)), i32(dest), zero_blk, n_slots


def _ple_out_kernel(dest_ref, h1_ref, rt_ref, pp_ref, ps_ref, nple_ref, wpg_ref, wp_ref, nfin_ref, ys_hbm,
                    op_ref, os_ref, yb_ref, sem, *, n_prompt_tiles):
    i = pl.program_id(0)
    nt = pl.num_programs(0)
    tm = h1_ref.shape[0]
    t_all = nt * tm
    slot = lax.rem(i, 2)

    def issue(tile, sl):
        base = tile * tm

        def grp(g, c):
            for j in range(DMA_UNROLL):
                row = g * (DMA_UNROLL // 2) + j // 2
                d = dest_ref[(j % 2) * t_all + base + row]
                pltpu.make_async_copy(ys_hbm.at[pl.ds(d, 1), :], yb_ref.at[sl, j % 2, pl.ds(row, 1), :],
                                      sem.at[sl]).start()
            return c

        lax.fori_loop(0, 2 * tm // DMA_UNROLL, grp, 0)

    @pl.when(i == 0)
    def _():
        issue(0, 0)

    def wait_rows(sl):
        for k in range(2):
            pltpu.make_async_copy(ys_hbm.at[pl.ds(0, tm), :], yb_ref.at[sl, k], sem.at[sl]).wait()

    wait_rows(slot)
    is_p = i < n_prompt_tiles
    rt = rt_ref[...]
    w0, w1 = rt[:, 2:3], rt[:, 3:4]
    lo0, hi0 = _unpack_bf16_pair(yb_ref[slot, 0])
    lo1, hi1 = _unpack_bf16_pair(yb_ref[slot, 1])
    h2 = h1_ref[...] + jnp.concatenate([w0 * lo0 + w1 * lo1, w0 * hi0 + w1 * hi1], axis=1)
    hb = _rms(h2, nple_ref[...]).astype(BF16)
    nbase = jnp.minimum(i + 1, nt - 1) * tm
    n_chunk = 8
    cw, ca = D_MODEL // n_chunk, 2 * tm // n_chunk
    gates = []
    for c in range(n_chunk):
        gates.append(jax.nn.sigmoid(_dot(hb, wpg_ref[:, c * cw:(c + 1) * cw])))
        for a in range(c * ca, (c + 1) * ca):
            d = dest_ref[(a % 2) * t_all + nbase + a // 2]
            pltpu.make_async_copy(ys_hbm.at[pl.ds(d, 1), :], yb_ref.at[1 - slot, a % 2, pl.ds(a // 2, 1), :],
                                  sem.at[1 - slot]).start()
    gate = jnp.concatenate(gates, axis=1)

    @pl.when(i == nt - 1)
    def _():
        wait_rows(1 - slot)

    p = jnp.where(is_p, pp_ref[...], ps_ref[...])
    h3 = h2 + _dot(p.astype(BF16), wp_ref[...]) * gate
    y = _rms(h3, nfin_ref[...])

    @pl.when(is_p)
    def _():
        op_ref[...] = y

    @pl.when(jnp.logical_not(is_p))
    def _():
        os_ref[...] = y


def _ple_out(dest, h1, rt, pp, ps, nple, wpg, wp, nfin, y_slots):
    tm = TOK_TILE
    npt, nst = pp.shape[0] // tm, ps.shape[0] // tm
    row = lambda w: pl.BlockSpec((tm, w), lambda i, d: (i, 0))
    prow = lambda w: pl.BlockSpec((tm, w), lambda i, d: (jnp.minimum(i, npt - 1), 0))
    srow = lambda w: pl.BlockSpec((tm, w), lambda i, d: (jnp.maximum(i - npt, 0), 0))
    const = lambda shape: pl.BlockSpec(shape, lambda i, d: (0,) * len(shape), pipeline_mode=pl.Buffered(1))
    grid_spec = pltpu.PrefetchScalarGridSpec(
        num_scalar_prefetch=1,
        grid=(npt + nst,),
        in_specs=[row(D_MODEL), row(LANES), prow(D_PLE), srow(D_PLE), const((1, D_MODEL)), const(wpg.shape),
                  const(wp.shape), const((1, D_MODEL)), pl.BlockSpec(memory_space=pl.ANY)],
        out_specs=[prow(D_MODEL), srow(D_MODEL)],
        scratch_shapes=[pltpu.VMEM((2, 2, tm, HALF), jnp.uint32), pltpu.SemaphoreType.DMA((2,))],
    )
    return pl.pallas_call(
        functools.partial(_ple_out_kernel, n_prompt_tiles=npt),
        grid_spec=grid_spec,
        out_shape=[jax.ShapeDtypeStruct((pp.shape[0], D_MODEL), F32),
                   jax.ShapeDtypeStruct((ps.shape[0], D_MODEL), F32)],
        compiler_params=pltpu.CompilerParams(dimension_semantics=("arbitrary",), vmem_limit_bytes=VMEM_LIMIT),
        name="ple_out",
    )(dest, h1, rt, pp, ps, nple, wpg, wp, nfin, y_slots)


def _s5_params(lam_re, lam_im, log_dt, b_re, b_im, c_re, c_im, d_skip):
    dt = jnp.exp(log_dt)[:, None]
    mag = jnp.exp(lam_re * dt)
    ab_re = mag * jnp.cos(lam_im * dt)
    ab_im = mag * jnp.sin(lam_im * dt)
    den = lam_re * lam_re + lam_im * lam_im
    nr = ab_re - 1.0
    f_re = (nr * lam_re + ab_im * lam_im) / den
    f_im = (ab_im * lam_re - nr * lam_im) / den
    bb_re = f_re[..., None] * b_re - f_im[..., None] * b_im
    bb_im = f_re[..., None] * b_im + f_im[..., None] * b_re
    eye = jnp.eye(S5_GB, dtype=F32)

    def bu_w(bb):
        bb = bb.reshape(S5_NGB, S5_GB, S5_STATE, S5_GROUP)
        w = jnp.einsum('nlph,lm->nlhmp', bb, eye)
        return w.reshape(S5_NGB, S5_GB * S5_GROUP, S5_SL)

    def c_w(c):
        c = c.reshape(S5_NGB, S5_GB, S5_GROUP, S5_STATE)
        w = jnp.einsum('nlhp,lm->nlpmh', c, eye)
        return w.reshape(S5_NGB, S5_SL, S5_GB * S5_GROUP)

    wbu = jnp.concatenate([bu_w(bb_re), bu_w(bb_im)], axis=2).astype(BF16)
    wc = jnp.concatenate([c_w(c_re), -c_w(c_im)], axis=1).astype(BF16)
    a_re = ab_re.reshape(S5_NGB, 1, S5_SL)
    a_im = ab_im.reshape(S5_NGB, 1, S5_SL)
    dsk = d_skip.reshape(S5_NGB, 1, S5_GB * S5_GROUP)
    return wbu, wc, a_re, a_im, dsk


def kernel(x_prompt, x_sample, p_prompt, p_sample, state_gla, state_s5_re, state_s5_im, norm_mix, w_in, gla_w_gate_up, gla_gate_bias, gla_norm, s5_lam_re, s5_lam_im, s5_log_dt, s5_b_re, s5_b_im, s5_c_re, s5_c_im, s5_d, s5_glu_w, s5_glu_b, s5_norm, w_out, norm_ffn, router_group, router_expert, w_gate, w_up, w_down, norm_ple, w_ple, w_ple_gate, norm_final):
    depth = w_in.shape[0]
    assert depth == 1
    i = 0
    bp, lp, _ = x_prompt.shape
    bs, ls, _ = x_sample.shape
    tp, ts = bp * lp, bs * ls
    t_all = tp + ts

    wm = _w_in_prep(w_in[i])
    wgu = jnp.pad(gla_w_gate_up[i], ((0, LANES - GLA_RANK), (0, 0))).astype(BF16)
    gbias = gla_gate_bias[i].reshape(1, QK_W)
    wbu, wc, a_re, a_im, dsk = _s5_params(s5_lam_re[i], s5_lam_im[i], s5_log_dt[i], s5_b_re[i], s5_b_im[i],
                                          s5_c_re[i], s5_c_im[i], s5_d[i])
    glu_w = s5_glu_w[i].astype(BF16)
    wo = w_out[i].astype(BF16)
    wr32 = jnp.pad(jnp.concatenate([router_group[i], router_expert[i]], axis=1),
                   ((0, 0), (0, LANES - N_EGROUPS - N_EXPERTS)))
    wr_hi = wr32.astype(BF16)
    wr = jnp.stack([wr_hi, (wr32 - wr_hi.astype(F32)).astype(BF16)])
    wpg = w_ple_gate[i].astype(BF16)
    wp = w_ple[i].astype(BF16)
    vec = lambda a: a.reshape(1, -1)

    xp = x_prompt.reshape(tp, D_MODEL)
    xs = x_sample.reshape(ts, D_MODEL)

    qp, kp, vp, rp, lap, up = _in_proj(xp, vec(norm_mix[i]), wm, wgu, gbias, BF16)
    qs, ks, vs, rs, las, us = _in_proj(xs, vec(norm_mix[i]), wm, wgu, gbias, BF16)
    ogp, gla_p = _gla_prompt(qp, kp, vp, rp, lap, vec(gla_norm[i]), bp, lp)
    ogs, gla_s = _gla_sample(qs, ks, vs, rs, las, vec(gla_norm[i]), state_gla[i], bs, ls)
    wbu2, wc2, a_re2, a_im2 = _s5_prompt_params(wbu, wc, a_re, a_im, bp)
    ysp, re_p, im_p = _s5_prompt(up.reshape(bp, lp, D_S5), wbu2, wc2, a_re2, a_im2, dsk, tc=512)
    yss, re_s, im_s = _s5(us.reshape(bs, ls, D_S5), wbu, wc, a_re, a_im, dsk,
                          state_s5_re[i].reshape(bs, -1), state_s5_im[i].reshape(bs, -1),
                          nb=32, tc=ls, has_state=True)

    h1, hn_all, rt_all, rtt, cnt = _mix_out(xp, xs, ogp, ogs, ysp.reshape(tp, D_S5), yss.reshape(ts, D_S5),
                                            glu_w, vec(s5_glu_b[i]), vec(s5_norm[i]), wo, vec(norm_ffn[i]), wr)

    plan, dest, zero_blk, n_slots = _moe_plan(rtt, cnt)
    x_slots = _dispatch(dest, zero_blk, hn_all, n_slots)
    y_slots = _moe(x_slots, plan, w_gate[i], w_up[i], w_down[i], n_slots)

    y_p, y_s = _ple_out(dest, h1, rt_all, p_prompt[i].reshape(tp, D_PLE), p_sample[i].reshape(ts, D_PLE),
                        vec(norm_ple[i]), wpg, wp, vec(norm_final), y_slots)

    s5shape = lambda a, b: a.reshape(1, b, S5_GROUPS, S5_STATE)
    return (y_p.reshape(bp, lp, D_MODEL), y_s.reshape(bs, ls, D_MODEL),
            gla_p[None], s5shape(re_p, bp), s5shape(im_p, bp),
            gla_s[None], s5shape(re_s, bs), s5shape(im_s, bs))
```

```python
import functools
import math

import jax
import jax.numpy as jnp
from jax import lax
from jax.experimental import pallas as pl
from jax.experimental.pallas import tpu as pltpu

F32 = jnp.float32
BF16 = jnp.bfloat16

D_MODEL = 2048
D_GLA = 1024
D_S5 = 1024
GLA_HEADS = 4
GLA_DV = 256
GLA_DK = 128
GLA_RANK = 16
GLA_CHUNK = 64
S5_GROUP = 16
S5_GROUPS = 64
S5_STATE = 64
N_EGROUPS = 4
N_EPG = 8
N_EXPERTS = 32
D_EXPERT = 512
D_PLE = 256
EPS = 1e-6

LANES = 128
QK_W = GLA_HEADS * GLA_DK
S5_GB = 8
S5_NGB = S5_GROUPS // S5_GB
S5_SL = S5_GB * S5_STATE
TOK_TILE = 256
MOE_BLK = 128
VMEM_LIMIT = 56 * 1024 * 1024
MOE_VMEM_LIMIT = 60 * 1024 * 1024


def _const_spec(shape):
    nd = len(shape)
    return pl.BlockSpec(shape, lambda *_: (0,) * nd, pipeline_mode=pl.Buffered(1))


def _rms(x, g):
    return x * lax.rsqrt(jnp.mean(x * x, axis=-1, keepdims=True) + EPS) * g


def _dot(a, b):
    return jnp.dot(a, b, preferred_element_type=F32)


def _log_sigmoid(x):
    return -(jnp.maximum(-x, 0.0) + jnp.log1p(jnp.exp(-jnp.abs(x))))


N_QKVR = 2 * QK_W + 2 * D_GLA
W_IN_COLS = N_QKVR + GLA_RANK + D_S5


def _w_in_prep_kernel(w_ref, o_ref):
    o_ref[:, 0:N_QKVR] = w_ref[:, 0:N_QKVR].astype(BF16)
    tail = w_ref[:, N_QKVR:W_IN_COLS]
    o_ref[:, N_QKVR:N_QKVR + D_S5] = tail[:, GLA_RANK:GLA_RANK + D_S5].astype(BF16)
    o_ref[:, N_QKVR + D_S5:N_QKVR + D_S5 + LANES] = tail[:, 0:LANES].astype(BF16)


def _w_in_prep(wi):
    rows = 256
    return pl.pallas_call(
        _w_in_prep_kernel,
        grid=(D_MODEL // rows,),
        in_specs=[pl.BlockSpec((rows, W_IN_COLS), lambda i: (i, 0))],
        out_specs=pl.BlockSpec((rows, N_QKVR + D_S5 + LANES), lambda i: (i, 0)),
        out_shape=jax.ShapeDtypeStruct((D_MODEL, N_QKVR + D_S5 + LANES), BF16),
        compiler_params=pltpu.CompilerParams(dimension_semantics=("arbitrary",), vmem_limit_bytes=VMEM_LIMIT),
        name="w_in_prep",
    )(wi)


def _in_proj_kernel(x_ref, g_ref, wm_ref, wgu_ref, gb_ref,
                    q_ref, k_ref, v_ref, r_ref, la_ref, u_ref):
    hb = _rms(x_ref[...], g_ref[...]).astype(BF16)

    def seg(a, b):
        return _dot(hb, wm_ref[:, a:b])

    q_ref[...] = (seg(0, QK_W) * (GLA_DK ** -0.5)).astype(q_ref.dtype)
    k_ref[...] = seg(QK_W, 2 * QK_W).astype(k_ref.dtype)
    v_ref[...] = seg(2 * QK_W, 2 * QK_W + D_GLA).astype(v_ref.dtype)
    r_ref[...] = seg(2 * QK_W + D_GLA, N_QKVR).astype(r_ref.dtype)
    u_ref[...] = seg(N_QKVR, N_QKVR + D_S5)
    zg = seg(N_QKVR + D_S5, N_QKVR + D_S5 + LANES)
    xg = _dot(zg.astype(BF16), wgu_ref[...]) + gb_ref[...]
    la_ref[...] = _log_sigmoid(xg) * (1.0 / 16.0)


def _in_proj(x2d, g, wm, wgu, gbias, act_dtype):
    t = x2d.shape[0]
    tm = TOK_TILE
    row = lambda w: pl.BlockSpec((tm, w), lambda i: (i, 0))
    return pl.pallas_call(
        _in_proj_kernel,
        grid=(t // tm,),
        in_specs=[row(D_MODEL), _const_spec((1, D_MODEL)), _const_spec(wm.shape),
                  _const_spec(wgu.shape), _const_spec((1, QK_W))],
        out_specs=[row(QK_W), row(QK_W), row(D_GLA), row(D_GLA), row(QK_W), row(D_S5)],
        out_shape=[jax.ShapeDtypeStruct((t, QK_W), act_dtype), jax.ShapeDtypeStruct((t, QK_W), act_dtype),
                   jax.ShapeDtypeStruct((t, D_GLA), act_dtype), jax.ShapeDtypeStruct((t, D_GLA), act_dtype),
                   jax.ShapeDtypeStruct((t, QK_W), F32), jax.ShapeDtypeStruct((t, D_S5), F32)],
        compiler_params=pltpu.CompilerParams(dimension_semantics=("arbitrary",), vmem_limit_bytes=VMEM_LIMIT),
        name="in_proj",
    )(x2d, g, wm, wgu, gbias)


_NT = (((1,), (1,)), ((), ()))
_TN = (((0,), (0,)), ((), ()))


def _gla_pre(q, k, la, c):
    r = q.shape[0]
    shift = int(math.log2(c))
    ri = lax.broadcasted_iota(jnp.int32, (r, r), 0)
    si = lax.broadcasted_iota(jnp.int32, (r, r), 1)
    mask = ((ri >> shift) == (si >> shift)) & (ri >= si)
    tri = jnp.where(mask, 1.0, 0.0).astype(BF16)
    hi = la.astype(BF16)
    r1 = la - hi.astype(F32)
    mid = r1.astype(BF16)
    lo = (r1 - mid.astype(F32)).astype(BF16)
    cum = _dot(tri, hi) + _dot(tri, mid) + _dot(tri, lo)
    last = jnp.concatenate([jnp.broadcast_to(cum[(i + 1) * c - 1:(i + 1) * c, :], (c, cum.shape[1]))
                            for i in range(r // c)], axis=0)
    qe = (q * jnp.exp(cum)).astype(BF16)
    ke = (k * jnp.exp(-cum)).astype(BF16)
    kd = (k * jnp.exp(last - cum)).astype(BF16)
    return qe, ke, kd, cum, mask


def _gla_intra(qe, ke, v, mask):
    sc = lax.dot_general(qe, ke, _NT, preferred_element_type=F32)
    return _dot(jnp.where(mask, sc, 0.0).astype(BF16), v)


def _gla_finish(o, r, g):
    rf = r.astype(F32)
    return _rms(o, g) * (rf * jax.nn.sigmoid(rf))


GLA_SAFE_LOG_DECAY = -60.0


def _gla_token_step(t, rows, q, k, v, a_all, st_ref, oacc_ref):
    m = rows == t
    a = jnp.sum(jnp.where(m, a_all, 0.0), axis=0, keepdims=True)
    kt = jnp.where(m, k, 0.0).astype(BF16)
    qt = jnp.where(m, q, 0.0).astype(BF16)
    vt = jnp.where(m, v, jnp.zeros_like(v))
    st = st_ref[...] * a + lax.dot_general(vt, kt, _TN, preferred_element_type=F32)
    st_ref[...] = st
    oacc_ref[...] += lax.dot_general(qt, st.astype(BF16), _NT, preferred_element_type=F32)


def _gla_prompt_kernel(q_ref, k_ref, v_ref, r_ref, la_ref, g_ref, o_ref, sfin_ref, st_ref, oacc_ref,
                       *, n_chunks):
    j = pl.program_id(1)

    @pl.when(j == 0)
    def _():
        st_ref[...] = jnp.zeros_like(st_ref)

    c = GLA_CHUNK
    qe, ke, kd, cum, mask = _gla_pre(q_ref[...].astype(F32), k_ref[...].astype(F32), la_ref[...], c)
    g = g_ref[...]
    safe = jnp.min(cum) >= GLA_SAFE_LOG_DECAY

    @pl.when(safe)
    def _():
        for h in range(GLA_HEADS):
            kc = slice(h * GLA_DK, (h + 1) * GLA_DK)
            vc = slice(h * GLA_DV, (h + 1) * GLA_DV)
            v = v_ref[:, vc]
            qe_h, kd_h = qe[:, kc], kd[:, kc]
            o = _gla_intra(qe_h, ke[:, kc], v, mask)
            st = st_ref[h]
            inter = []
            for ci in range(n_chunks):
                rows = slice(ci * c, (ci + 1) * c)
                inter.append(lax.dot_general(qe_h[rows], st.astype(BF16), _NT, preferred_element_type=F32))
                dec = jnp.exp(cum[(ci + 1) * c - 1:(ci + 1) * c, kc])
                st = st * dec + lax.dot_general(v[rows], kd_h[rows], _TN, preferred_element_type=F32)
            st_ref[h] = st
            o = o + jnp.concatenate(inter, axis=0)
            o_ref[:, vc] = _gla_finish(o, r_ref[:, vc], g).astype(o_ref.dtype)

    @pl.when(jnp.logical_not(safe))
    def _():
        n_rows = n_chunks * c
        rows = lax.broadcasted_iota(jnp.int32, (n_rows, 1), 0)
        for h in range(GLA_HEADS):
            kc = slice(h * GLA_DK, (h + 1) * GLA_DK)
            vc = slice(h * GLA_DV, (h + 1) * GLA_DV)
            q, k, v = q_ref[:, kc].astype(F32), k_ref[:, kc].astype(F32), v_ref[:, vc]
            a_all = jnp.exp(la_ref[:, kc])
            oacc_ref[...] = jnp.zeros_like(oacc_ref)

            def step(t, carry):
                _gla_token_step(t, rows, q, k, v, a_all, st_ref.at[h], oacc_ref)
                return carry

            lax.fori_loop(0, n_rows, step, 0)
            o_ref[:, vc] = _gla_finish(oacc_ref[...], r_ref[:, vc], g).astype(o_ref.dtype)

    @pl.when(j == pl.num_programs(1) - 1)
    def _():
        for h in range(GLA_HEADS):
            sfin_ref[0, h] = st_ref[h].T


def _gla_prompt(q, k, v, r, la, g, batch, seq):
    rb = 4 * GLA_CHUNK
    nj = seq // rb
    row = lambda w: pl.BlockSpec((rb, w), lambda b, j: (b * nj + j, 0))
    return pl.pallas_call(
        functools.partial(_gla_prompt_kernel, n_chunks=rb // GLA_CHUNK),
        grid=(batch, nj),
        in_specs=[row(QK_W), row(QK_W), row(D_GLA), row(D_GLA), row(QK_W), _const_spec((1, GLA_DV))],
        out_specs=[row(D_GLA),
                   pl.BlockSpec((1, GLA_HEADS, GLA_DK, GLA_DV), lambda b, j: (b, 0, 0, 0))],
        out_shape=[jax.ShapeDtypeStruct((batch * seq, D_GLA), BF16),
                   jax.ShapeDtypeStruct((batch, GLA_HEADS, GLA_DK, GLA_DV), F32)],
        scratch_shapes=[pltpu.VMEM((GLA_HEADS, GLA_DV, GLA_DK), F32), pltpu.VMEM((rb, GLA_DV), F32)],
        compiler_params=pltpu.CompilerParams(dimension_semantics=("arbitrary", "arbitrary"),
                                             vmem_limit_bytes=VMEM_LIMIT),
        name="gla_prompt",
    )(q, k, v, r, la, g)


def _gla_sample_kernel(q_ref, k_ref, v_ref, r_ref, la_ref, g_ref, s0_ref, o_ref, sfin_ref, st_ref, oacc_ref,
                       *, n_seq, seq):
    qe, ke, kd, cum, mask = _gla_pre(q_ref[...].astype(F32), k_ref[...].astype(F32), la_ref[...], seq)
    g = g_ref[...]
    safe = jnp.min(cum) >= GLA_SAFE_LOG_DECAY

    @pl.when(safe)
    def _():
        lasts = jnp.concatenate([cum[(s + 1) * seq - 1:(s + 1) * seq, :] for s in range(n_seq)]
                                + [jnp.zeros((GLA_DK - n_seq, cum.shape[1]), F32)], axis=0)
        pair = 2 * seq
        upper = lax.broadcasted_iota(jnp.int32, (pair, GLA_DK), 0) < seq
        for h in range(GLA_HEADS):
            kc = slice(h * GLA_DK, (h + 1) * GLA_DK)
            vc = slice(h * GLA_DV, (h + 1) * GLA_DV)
            v = v_ref[:, vc]
            qe_h, kd_h = qe[:, kc], kd[:, kc]
            o = _gla_intra(qe_h, ke[:, kc], v, mask)
            dec_t = jnp.exp(lasts[:, kc].T)
            inter = []
            for p in range(n_seq // 2):
                rows = slice(p * pair, (p + 1) * pair)
                qe_p, kd_p, v_p = qe_h[rows], kd_h[rows], v[rows]
                for half in range(2):
                    s = 2 * p + half
                    s0 = s0_ref[s, h]
                    o_s = _dot(qe_p, s0.astype(BF16))
                    inter.append(o_s[half * seq:(half + 1) * seq])
                    kd_s = jnp.where(upper if half == 0 else ~upper, kd_p, jnp.zeros_like(kd_p))
                    dec = jnp.broadcast_to(dec_t[:, s:s + 1], (GLA_DK, GLA_DV))
                    sfin_ref[s, h] = s0 * dec + lax.dot_general(kd_s, v_p, _TN, preferred_element_type=F32)
            o = o + jnp.concatenate(inter, axis=0)
            o_ref[:, vc] = _gla_finish(o, r_ref[:, vc], g).astype(o_ref.dtype)

    @pl.when(jnp.logical_not(safe))
    def _():
        n_rows = n_seq * seq
        rows = lax.broadcasted_iota(jnp.int32, (n_rows, 1), 0)
        for h in range(GLA_HEADS):
            kc = slice(h * GLA_DK, (h + 1) * GLA_DK)
            vc = slice(h * GLA_DV, (h + 1) * GLA_DV)
            q, k, v = q_ref[:, kc].astype(F32), k_ref[:, kc].astype(F32), v_ref[:, vc]
            a_all = jnp.exp(la_ref[:, kc])
            oacc_ref[...] = jnp.zeros_like(oacc_ref)

            def step(t, carry):
                s = t // seq

                @pl.when(t % seq == 0)
                def _():
                    st_ref[...] = s0_ref[s, h].T

                _gla_token_step(t, rows, q, k, v, a_all, st_ref, oacc_ref)

                @pl.when(t % seq == seq - 1)
                def _():
                    sfin_ref[s, h] = st_ref[...].T

                return carry

            lax.fori_loop(0, n_rows, step, 0)
            o_ref[:, vc] = _gla_finish(oacc_ref[...], r_ref[:, vc], g).astype(o_ref.dtype)


def _gla_sample(q, k, v, r, la, g, s0, batch, seq):
    ns = 16
    rb = ns * seq
    row = lambda w: pl.BlockSpec((rb, w), lambda i: (i, 0))
    st = pl.BlockSpec((ns, GLA_HEADS, GLA_DK, GLA_DV), lambda i: (i, 0, 0, 0))
    return pl.pallas_call(
        functools.partial(_gla_sample_kernel, n_seq=ns, seq=seq),
        grid=(batch // ns,),
        in_specs=[row(QK_W), row(QK_W), row(D_GLA), row(D_GLA), row(QK_W), _const_spec((1, GLA_DV)), st],
        out_specs=[row(D_GLA), st],
        out_shape=[jax.ShapeDtypeStruct((batch * seq, D_GLA), BF16),
                   jax.ShapeDtypeStruct((batch, GLA_HEADS, GLA_DK, GLA_DV), F32)],
        scratch_shapes=[pltpu.VMEM((GLA_DV, GLA_DK), F32), pltpu.VMEM((rb, GLA_DV), F32)],
        compiler_params=pltpu.CompilerParams(dimension_semantics=("arbitrary",), vmem_limit_bytes=VMEM_LIMIT),
        name="gla_sample",
    )(q, k, v, r, la, g, s0)


def _s5_kernel(u_ref, wbu_ref, wc_ref, are_ref, aim_ref, d_ref, h0r_ref, h0i_ref,
               y_ref, sre_ref, sim_ref, bu_ref, xs_ref, car_ref, *, nb, tc, has_state):
    j = pl.program_id(2)

    @pl.when(j == 0)
    def _():
        if has_state:
            car_ref[0] = h0r_ref[...]
            car_ref[1] = h0i_ref[...]
        else:
            car_ref[...] = jnp.zeros_like(car_ref)

    u2 = u_ref[...].reshape(nb * tc, LANES)
    ub = u2.astype(BF16)
    nl = S5_SL // LANES
    for l in range(2 * nl):
        bu_ref[l] = _dot(ub, wbu_ref[0, :, l * LANES:(l + 1) * LANES])
    a_r = [jnp.broadcast_to(are_ref[0, :, l * LANES:(l + 1) * LANES], (nb, LANES)) for l in range(nl)]
    a_i = [jnp.broadcast_to(aim_ref[0, :, l * LANES:(l + 1) * LANES], (nb, LANES)) for l in range(nl)]

    def step(t, carry):
        rows = pl.ds(t, nb, stride=tc)
        out = []
        for l in range(nl):
            xr, xi = carry[2 * l], carry[2 * l + 1]
            nr = a_r[l] * xr - a_i[l] * xi + bu_ref[l, rows, :]
            ni = a_r[l] * xi + a_i[l] * xr + bu_ref[nl + l, rows, :]
            xs_ref[l, rows, :] = nr
            xs_ref[nl + l, rows, :] = ni
            out += [nr, ni]
        return tuple(out)

    init = []
    for l in range(nl):
        init += [car_ref[0, :, l * LANES:(l + 1) * LANES], car_ref[1, :, l * LANES:(l + 1) * LANES]]
    fin = lax.fori_loop(0, tc, step, tuple(init), unroll=8)
    xr = jnp.concatenate([fin[2 * l] for l in range(nl)], axis=1)
    xi = jnp.concatenate([fin[2 * l + 1] for l in range(nl)], axis=1)
    car_ref[0] = xr
    car_ref[1] = xi
    y = d_ref[0] * u2
    for l in range(2 * nl):
        y = y + _dot(xs_ref[l].astype(BF16), wc_ref[0, l * LANES:(l + 1) * LANES, :])
    y_ref[...] = y.reshape(nb, tc, LANES)

    @pl.when(j == pl.num_programs(2) - 1)
    def _():
        sre_ref[...] = xr
        sim_ref[...] = xi


def _s5(u3d, wbu, wc, a_re, a_im, dsk, h0r, h0i, nb, tc, has_state):
    batch, seq, _ = u3d.shape
    grid = (S5_NGB, batch // nb, seq // tc)
    st = pl.BlockSpec((nb, S5_SL), lambda g, b, j: (b, g))
    par = lambda w: pl.BlockSpec((1, 1, w), lambda g, b, j: (g, 0, 0))
    ublk = pl.BlockSpec((nb, tc, LANES), lambda g, b, j: (b, j, g))
    return pl.pallas_call(
        functools.partial(_s5_kernel, nb=nb, tc=tc, has_state=has_state),
        grid=grid,
        in_specs=[ublk,
                  pl.BlockSpec((1, LANES, 2 * S5_SL), lambda g, b, j: (g, 0, 0)),
                  pl.BlockSpec((1, 2 * S5_SL, LANES), lambda g, b, j: (g, 0, 0)),
                  par(S5_SL), par(S5_SL), par(LANES), st, st],
        out_specs=[ublk, st, st],
        out_shape=[jax.ShapeDtypeStruct(u3d.shape, F32),
                   jax.ShapeDtypeStruct((batch, S5_GROUPS * S5_STATE), F32),
                   jax.ShapeDtypeStruct((batch, S5_GROUPS * S5_STATE), F32)],
        scratch_shapes=[pltpu.VMEM((2 * S5_SL // LANES, nb * tc, LANES), F32),
                        pltpu.VMEM((2 * S5_SL // LANES, nb * tc, LANES), F32),
                        pltpu.VMEM((2, nb, S5_SL), F32)],
        compiler_params=pltpu.CompilerParams(dimension_semantics=("arbitrary",) * 3,
                                             vmem_limit_bytes=VMEM_LIMIT),
        name="s5_state" if has_state else "s5_zero",
    )(u3d, wbu, wc, a_re, a_im, dsk, h0r, h0i)


S5_J = 2
S5_HL = S5_SL // S5_J
S5_PAIR = 2


def _s5_prompt_kernel(u_ref, wbu_ref, wc_ref, are_ref, aim_ref, d_ref, y_ref, sre_ref, sim_ref,
                      u2_ref, lhs_ref, bu_ref, xs_ref, y2_ref, yo_ref, car_ref, *, nb, tc):
    g = pl.program_id(0)
    j = pl.program_id(1)
    rows = nb * S5_J
    npair = S5_PAIR

    @pl.when((g == 0) & (j == 0))
    def _():
        lhs_ref[...] = jnp.zeros_like(lhs_ref)

    @pl.when(j == 0)
    def _():
        car_ref[...] = jnp.zeros_like(car_ref)

    u2 = u_ref[...].reshape(nb * tc, npair * LANES)
    for p in range(npair):
        u2_ref[p] = u2[:, p * LANES:(p + 1) * LANES]

    def build(t, c):
        for p in range(npair):
            u4 = u2_ref[p, pl.ds(t, nb, stride=tc), :]
            for jj in range(S5_J):
                lhs_ref[p, t, jj * nb:(jj + 1) * nb, jj * LANES:(jj + 1) * LANES] = u4
        return c

    lax.fori_loop(0, tc, build, 0, unroll=8)
    for p in range(npair):
        lhs = lhs_ref[p].reshape(tc * rows, S5_J * LANES).astype(BF16)
        bu_ref[p] = _dot(lhs, wbu_ref[p]).reshape(tc, rows, 2 * S5_HL)
    ar = [are_ref[p] for p in range(npair)]
    ai = [aim_ref[p] for p in range(npair)]

    def step(t, carry):
        out = []
        for p in range(npair):
            xr, xi = carry[2 * p], carry[2 * p + 1]
            tile = bu_ref[p, t]
            nr = ar[p] * xr - ai[p] * xi + tile[:, 0:S5_HL]
            ni = ar[p] * xi + ai[p] * xr + tile[:, S5_HL:2 * S5_HL]
            xs_ref[p, t] = jnp.concatenate([nr, ni], axis=1)
            out += [nr, ni]
        return tuple(out)

    init = tuple(car_ref[p, c] for p in range(npair) for c in range(2))
    fin = lax.fori_loop(0, tc, step, init, unroll=8)
    for p in range(npair):
        car_ref[p, 0] = fin[2 * p]
        car_ref[p, 1] = fin[2 * p + 1]
        xs = xs_ref[p].reshape(tc * rows, 2 * S5_HL).astype(BF16)
        y2_ref[p] = _dot(xs, wc_ref[p]).reshape(tc, rows, S5_J * LANES)
    first_half = lax.broadcasted_iota(jnp.int32, (rows, LANES), 0) < nb

    def unperm(t, c):
        for p in range(npair):
            t2 = y2_ref[p, t]
            part = jnp.where(first_half, t2[:, 0:LANES], t2[:, LANES:2 * LANES])
            yo_ref[p, pl.ds(t, nb, stride=tc), :] = (part + pltpu.roll(part, nb, axis=0))[0:nb]
        return c

    lax.fori_loop(0, tc, unperm, 0, unroll=8)
    y = jnp.concatenate([yo_ref[p] + d_ref[p] * u2_ref[p] for p in range(npair)], axis=1)
    y_ref[...] = y.reshape(nb, tc, npair * LANES)

    @pl.when(j == pl.num_programs(1) - 1)
    def _():
        def gather_state(x):
            return jnp.concatenate([x[jj * nb:(jj + 1) * nb] for jj in range(S5_J)], axis=1)

        sre_ref[...] = jnp.concatenate([gather_state(fin[2 * p]) for p in range(npair)], axis=1)
        sim_ref[...] = jnp.concatenate([gather_state(fin[2 * p + 1]) for p in range(npair)], axis=1)


def _s5_prompt(u3d, wbu2, wc2, a_re2, a_im2, dsk, tc):
    nb, seq, _ = u3d.shape
    rows = nb * S5_J
    assert rows == 8
    np_ = S5_PAIR
    st = pl.BlockSpec((nb, np_ * S5_SL), lambda g, j: (0, g))
    gblk = lambda s: pl.BlockSpec((np_,) + s, lambda g, j: (g, 0, 0))
    ublk = pl.BlockSpec((nb, tc, np_ * LANES), lambda g, j: (0, j, g))
    return pl.pallas_call(
        functools.partial(_s5_prompt_kernel, nb=nb, tc=tc),
        grid=(S5_NGB // np_, seq // tc),
        in_specs=[ublk, gblk((S5_J * LANES, 2 * S5_HL)), gblk((2 * S5_HL, S5_J * LANES)),
                  gblk((rows, S5_HL)), gblk((rows, S5_HL)), gblk((1, LANES))],
        out_specs=[ublk, st, st],
        out_shape=[jax.ShapeDtypeStruct(u3d.shape, F32),
                   jax.ShapeDtypeStruct((nb, S5_GROUPS * S5_STATE), F32),
                   jax.ShapeDtypeStruct((nb, S5_GROUPS * S5_STATE), F32)],
        scratch_shapes=[pltpu.VMEM((np_, nb * tc, LANES), F32),
                        pltpu.VMEM((np_, tc, rows, S5_J * LANES), F32),
                        pltpu.VMEM((np_, tc, rows, 2 * S5_HL), F32),
                        pltpu.VMEM((np_, tc, rows, 2 * S5_HL), F32),
                        pltpu.VMEM((np_, tc, rows, S5_J * LANES), F32),
                        pltpu.VMEM((np_, nb * tc, LANES), F32),
                        pltpu.VMEM((np_, 2, rows, S5_HL), F32)],
        compiler_params=pltpu.CompilerParams(dimension_semantics=("arbitrary",) * 2,
                                             vmem_limit_bytes=VMEM_LIMIT),
        name="s5_prompt",
    )(u3d, wbu2, wc2, a_re2, a_im2, dsk)


def _s5_prompt_params(wbu, wc, a_re, a_im, nb):
    h = S5_HL
    top = jnp.concatenate([wbu[:, :, 0:h], wbu[:, :, S5_SL:S5_SL + h]], axis=2)
    bot = jnp.concatenate([wbu[:, :, h:2 * h], wbu[:, :, S5_SL + h:S5_SL + 2 * h]], axis=2)
    wbu2 = jnp.concatenate([top, bot], axis=1)
    wc_j = [jnp.concatenate([wc[:, jj * h:(jj + 1) * h, :], wc[:, S5_SL + jj * h:S5_SL + (jj + 1) * h, :]],
                            axis=1) for jj in range(S5_J)]
    wc2 = jnp.concatenate(wc_j, axis=2)
    tile = lambda a: jnp.repeat(a.reshape(S5_NGB, S5_J, h), nb, axis=1)
    return wbu2, wc2, tile(a_re), tile(a_im)


def _mix_out_kernel(xp_ref, xs_ref, ogp_ref, ogs_ref, ysp_ref, yss_ref, glu_w_ref, glu_b_ref, s5n_ref, wo_ref,
                    nffn_ref, wr_ref, h1_ref, hn_ref, rt_ref, rtt_ref, cnt_ref, *, n_prompt_tiles):
    is_p = pl.program_id(0) < n_prompt_tiles
    x = jnp.where(is_p, xp_ref[...], xs_ref[...])
    og = jnp.where(is_p, ogp_ref[...], ogs_ref[...])
    y = jax.nn.gelu(jnp.where(is_p, ysp_ref[...], yss_ref[...]))
    y = y * jax.nn.sigmoid(_dot(y.astype(BF16), glu_w_ref[...]) + glu_b_ref[...])
    yn = _rms(y, s5n_ref[...]).astype(BF16)
    mix = _dot(og, wo_ref[0:D_GLA, :]) + _dot(yn, wo_ref[D_GLA:D_GLA + D_S5, :])
    h1 = x + mix
    h1_ref[...] = h1
    hn = _rms(h1, nffn_ref[...])
    hn_ref[...] = _pack_bf16_pair(hn[:, 0:HALF], hn[:, HALF:D_MODEL])
    hn_hi = hn.astype(BF16)
    hn_lo = (hn - hn_hi.astype(F32)).astype(BF16)
    logits = _dot(hn_hi, wr_ref[0]) + _dot(hn_hi, wr_ref[1]) + _dot(hn_lo, wr_ref[0])
    rt = _route(logits)

    @pl.when(pl.program_id(0) == 0)
    def _():
        cnt_ref[...] = jnp.zeros_like(cnt_ref)

    tm = rt.shape[0]
    lane = lax.broadcasted_iota(jnp.int32, rt.shape, 1).astype(F32)
    oh0 = lane == rt[:, 0:1]
    oh1 = lane == rt[:, 1:2]
    both = jnp.where(oh0 | oh1, 1.0, 0.0)
    ri = lax.broadcasted_iota(jnp.int32, (tm, tm), 0)
    ci = lax.broadcasted_iota(jnp.int32, (tm, tm), 1)
    before = _dot(jnp.where(ri > ci, 1.0, 0.0).astype(BF16), both.astype(BF16)) + cnt_ref[...]
    rank0 = jnp.sum(jnp.where(oh0, before, 0.0), axis=-1, keepdims=True)
    rank1 = jnp.sum(jnp.where(oh1, before, 0.0), axis=-1, keepdims=True)
    cnt_ref[...] += jnp.sum(both, axis=0, keepdims=True)
    rt = jnp.where(lane == 4.0, rank0, jnp.where(lane == 5.0, rank1, rt))
    rt_ref[...] = rt
    rtt_ref[...] = rt.T[0:8, :]


def _route(logits):
    col = lax.broadcasted_iota(jnp.int32, logits.shape, 1)
    colf = col.astype(F32)
    neg = -jnp.inf

    def first_argmax(vals):
        m = jnp.max(vals, axis=-1, keepdims=True)
        idx = jnp.min(jnp.where(vals == m, colf, float(LANES)), axis=-1, keepdims=True)
        return m, idx

    lg = jnp.where(col < N_EGROUPS, logits, neg)
    gmax, gsel = first_argmax(lg)
    p_g = 1.0 / jnp.sum(jnp.exp(lg - gmax), axis=-1, keepdims=True)
    ecol = col - N_EGROUPS
    egrp = (ecol >> 3).astype(F32)
    in_group = (ecol >= 0) & (ecol < N_EXPERTS) & (egrp == gsel)
    le = jnp.where(in_group, logits, neg)
    m1, i1 = first_argmax(le)
    le2 = jnp.where(colf == i1, neg, le)
    m2, i2 = first_argmax(le2)
    e2 = jnp.exp(m2 - m1)
    den = 1.0 + e2
    w1 = p_g * (1.0 / den)
    w2 = p_g * (e2 / den)
    e1f = i1 - float(N_EGROUPS)
    e2f = i2 - float(N_EGROUPS)
    out = jnp.where(col == 0, e1f, jnp.where(col == 1, e2f, jnp.where(col == 2, w1, jnp.where(col == 3, w2, 0.0))))
    return out


def _mix_out(xp, xs, ogp, ogs, ysp, yss, glu_w, glu_b, s5n, wo, nffn, wr):
    tm = TOK_TILE
    npt, nst = xp.shape[0] // tm, xs.shape[0] // tm
    t = (npt + nst) * tm
    row = lambda w: pl.BlockSpec((tm, w), lambda i: (i, 0))
    prow = lambda w: pl.BlockSpec((tm, w), lambda i: (jnp.minimum(i, npt - 1), 0))
    srow = lambda w: pl.BlockSpec((tm, w), lambda i: (jnp.maximum(i - npt, 0), 0))
    return pl.pallas_call(
        functools.partial(_mix_out_kernel, n_prompt_tiles=npt),
        grid=(npt + nst,),
        in_specs=[prow(D_MODEL), srow(D_MODEL), prow(D_GLA), srow(D_GLA), prow(D_S5), srow(D_S5),
                  _const_spec(glu_w.shape), _const_spec((1, D_S5)), _const_spec((1, D_S5)),
                  _const_spec(wo.shape), _const_spec((1, D_MODEL)), _const_spec(wr.shape)],
        out_specs=[row(D_MODEL), row(HALF), row(LANES), pl.BlockSpec((8, tm), lambda i: (0, i)),
                   pl.BlockSpec((1, LANES), lambda i: (0, 0))],
        out_shape=[jax.ShapeDtypeStruct((t, D_MODEL), F32), jax.ShapeDtypeStruct((t, HALF), jnp.uint32),
                   jax.ShapeDtypeStruct((t, LANES), F32), jax.ShapeDtypeStruct((8, t), F32),
                   jax.ShapeDtypeStruct((1, LANES), F32)],
        compiler_params=pltpu.CompilerParams(dimension_semantics=("arbitrary",), vmem_limit_bytes=VMEM_LIMIT),
        name="mix_out",
    )(xp, xs, ogp, ogs, ysp, yss, glu_w, glu_b, s5n, wo, nffn, wr)


DMA_UNROLL = 8


MOE_SB = 6
HALF = D_MODEL // 2


def _pack_bf16_pair(lo, hi):
    def bits(x):
        b = pltpu.bitcast(x, jnp.uint32)
        return (b + jnp.uint32(0x7FFF) + ((b >> 16) & jnp.uint32(1))) >> 16
    return bits(lo) | (bits(hi) << 16)


def _unpack_bf16_pair(u):
    return (pltpu.bitcast(u << 16, F32), pltpu.bitcast(u & jnp.uint32(0xFFFF0000), F32))


def _moe_kernel(se_ref, sm_ref, sn_ref, sbase_ref, xs_hbm, wg_ref, wu_ref, wd_ref, out_hbm,
                xb_ref, yb_ref, wgb_ref, wub_ref, wdb_ref, gsem, ssem):
    b = pl.program_id(0)
    nb = pl.num_programs(0)
    n = sn_ref[b]
    slot = lax.rem(b, 2)

    def issue_gather(blk, sl):
        base = pl.multiple_of(sbase_ref[blk], MOE_BLK)
        for m in range(1, MOE_SB + 1):
            @pl.when((sn_ref[blk] > 0) & (sm_ref[blk] == m))
            def _():
                rows = m * MOE_BLK
                pltpu.make_async_copy(xs_hbm.at[pl.ds(base, rows), :], xb_ref.at[sl, pl.ds(0, rows), :],
                                      gsem.at[sl]).start()

    def wait_gather(blocks, sl):
        rows = pl.ds(0, pl.multiple_of(blocks * MOE_BLK, MOE_BLK))
        pltpu.make_async_copy(xs_hbm.at[rows, :], xb_ref.at[sl, rows, :], gsem.at[sl]).wait()

    m_cur = sm_ref[b]
    m_prev = sm_ref[jnp.maximum(b - 1, 0)]
    has_prev = (b > 0) & (m_prev > 0)

    def out_copy(rows):
        dst = pl.ds(pl.multiple_of(sbase_ref[b], MOE_BLK), rows)
        return pltpu.make_async_copy(yb_ref.at[pl.ds(0, rows), :], out_hbm.at[dst, :], ssem.at[0])

    def wait_prev_out():
        rows = pl.ds(0, pl.multiple_of(m_prev * MOE_BLK, MOE_BLK))
        pltpu.make_async_copy(yb_ref.at[rows, :], out_hbm.at[rows, :], ssem.at[0]).wait()

    def write_out(rows):
        out_copy(rows).start()

        @pl.when(b == nb - 1)
        def _():
            out_copy(rows).wait()

    @pl.when(b == 0)
    def _():
        issue_gather(0, 0)

    @pl.when(b + 1 < nb)
    def _():
        issue_gather(b + 1, 1 - slot)

    def compute(rows):
        x_lo, x_hi = _unpack_bf16_pair(xb_ref[slot, 0:rows, :])
        x_lo, x_hi = x_lo.astype(BF16), x_hi.astype(BF16)
        gate = _dot(x_lo, wgb_ref[0:HALF, :]) + _dot(x_hi, wgb_ref[HALF:D_MODEL, :])
        up = _dot(x_lo, wub_ref[0:HALF, :]) + _dot(x_hi, wub_ref[HALF:D_MODEL, :])
        hid = (gate * jax.nn.sigmoid(gate) * up).astype(BF16)

        @pl.when(has_prev)
        def _():
            wait_prev_out()

        yb_ref[0:rows, :] = _pack_bf16_pair(_dot(hid, wdb_ref[:, 0:HALF]), _dot(hid, wdb_ref[:, HALF:D_MODEL]))
        write_out(rows)

    @pl.when(n > 0)
    def _():
        prev_e = se_ref[jnp.maximum(b - 1, 0)]

        @pl.when((b == 0) | (prev_e != se_ref[b]))
        def _():
            wgb_ref[...] = wg_ref[...].astype(BF16)
            wub_ref[...] = wu_ref[...].astype(BF16)
            wdb_ref[...] = wd_ref[...].astype(BF16)

        wait_gather(m_cur, slot)
        for m in range(1, MOE_SB + 1):
            @pl.when(m_cur == m)
            def _():
                compute(m * MOE_BLK)

    @pl.when((n == 0) & (m_cur > 0))
    def _():
        @pl.when(has_prev)
        def _():
            wait_prev_out()

        yb_ref[...] = jnp.zeros_like(yb_ref)
        for m in range(1, MOE_SB + 1):
            @pl.when(m_cur == m)
            def _():
                write_out(m * MOE_BLK)

    @pl.when((m_cur == 0) & has_prev)
    def _():
        wait_prev_out()


def _moe(x_slots, plan, w_gate, w_up, w_down, n_out_rows):
    sb_e, sb_m, sb_n, sb_base = plan
    sb_rows = MOE_SB * MOE_BLK
    wspec = lambda s: pl.BlockSpec((None,) + s, lambda b, se, *_: (se[b], 0, 0))
    grid_spec = pltpu.PrefetchScalarGridSpec(
        num_scalar_prefetch=4,
        grid=(sb_e.shape[0],),
        in_specs=[pl.BlockSpec(memory_space=pl.ANY),
                  wspec((D_MODEL, D_EXPERT)), wspec((D_MODEL, D_EXPERT)), wspec((D_EXPERT, D_MODEL))],
        out_specs=pl.BlockSpec(memory_space=pl.ANY),
        scratch_shapes=[pltpu.VMEM((2, sb_rows, HALF), jnp.uint32), pltpu.VMEM((sb_rows, HALF), jnp.uint32),
                        pltpu.VMEM((D_MODEL, D_EXPERT), BF16), pltpu.VMEM((D_MODEL, D_EXPERT), BF16),
                        pltpu.VMEM((D_EXPERT, D_MODEL), BF16), pltpu.SemaphoreType.DMA((2,)),
                        pltpu.SemaphoreType.DMA((1,))],
    )
    return pl.pallas_call(
        _moe_kernel,
        grid_spec=grid_spec,
        out_shape=jax.ShapeDtypeStruct((n_out_rows, HALF), jnp.uint32),
        compiler_params=pltpu.CompilerParams(dimension_semantics=("arbitrary",),
                                             vmem_limit_bytes=MOE_VMEM_LIMIT),
        name="moe_experts",
    )(sb_e, sb_m, sb_n, sb_base, x_slots, w_gate, w_up, w_down)


def _dispatch_kernel(dest_ref, zb_ref, hn_ref, out_hbm, buf_ref, zero_ref, sem, zsem):
    i = pl.program_id(0)
    nt = pl.num_programs(0)
    tm = hn_ref.shape[0]
    t_all = nt * tm
    slot = lax.rem(i, 2)
    nblk = zb_ref.shape[0]

    def zero_copy(blk):
        dst = pl.ds(pl.multiple_of(blk * MOE_BLK, MOE_BLK), MOE_BLK)
        return pltpu.make_async_copy(zero_ref, out_hbm.at[dst, :], zsem.at[0])

    @pl.when(i == 0)
    def _():
        zero_ref[...] = jnp.zeros_like(zero_ref)

        def start(blk, c):
            @pl.when(zb_ref[blk] == 1)
            def _():
                zero_copy(blk).start()
            return c

        def wait(blk, c):
            @pl.when(zb_ref[blk] == 1)
            def _():
                zero_copy(blk).wait()
            return c

        lax.fori_loop(0, nblk, start, 0)
        lax.fori_loop(0, nblk, wait, 0)

    def wait_rows(sl):
        for k in range(2):
            pltpu.make_async_copy(buf_ref.at[sl], out_hbm.at[pl.ds(0, tm), :], sem.at[sl]).wait()

    @pl.when(i >= 2)
    def _():
        wait_rows(slot)

    buf_ref[slot] = hn_ref[...]
    base = i * tm
    for a in range(2 * tm):
        row, k = a // 2, a % 2
        d = dest_ref[k * t_all + base + row]
        pltpu.make_async_copy(buf_ref.at[slot, pl.ds(row, 1), :], out_hbm.at[pl.ds(d, 1), :],
                              sem.at[slot]).start()

    @pl.when(i == nt - 1)
    def _():
        wait_rows(slot)

        @pl.when(nt > 1)
        def _():
            wait_rows(1 - slot)


def _dispatch(dest, zero_blk, hn_pk, n_slots):
    tm = TOK_TILE
    grid_spec = pltpu.PrefetchScalarGridSpec(
        num_scalar_prefetch=2,
        grid=(hn_pk.shape[0] // tm,),
        in_specs=[pl.BlockSpec((tm, HALF), lambda i, d, z: (i, 0))],
        out_specs=pl.BlockSpec(memory_space=pl.ANY),
        scratch_shapes=[pltpu.VMEM((2, tm, HALF), jnp.uint32), pltpu.VMEM((MOE_BLK, HALF), jnp.uint32),
                        pltpu.SemaphoreType.DMA((2,)), pltpu.SemaphoreType.DMA((1,))],
    )
    return pl.pallas_call(
        _dispatch_kernel,
        grid_spec=grid_spec,
        out_shape=jax.ShapeDtypeStruct((n_slots, HALF), jnp.uint32),
        compiler_params=pltpu.CompilerParams(dimension_semantics=("arbitrary",), vmem_limit_bytes=VMEM_LIMIT),
        name="moe_dispatch",
    )(dest, zero_blk, hn_pk)


def _moe_plan(rtt, cnt):
    t_all = rtt.shape[1]
    n_assign = 2 * t_all
    e_flat = rtt[0:2].astype(jnp.int32).reshape(-1)
    rank = rtt[4:6].astype(jnp.int32).reshape(-1)
    counts = cnt[0, 0:N_EXPERTS].astype(jnp.int32)
    padded = (counts + MOE_BLK - 1) // MOE_BLK * MOE_BLK
    pad_end = jnp.cumsum(padded)
    pad_start = pad_end - padded
    before = jnp.arange(N_EXPERTS, dtype=jnp.int32)[:, None] < e_flat[None, :]
    dest = rank + jnp.sum(jnp.where(before, padded[:, None], 0), axis=0)
    nblk = -(-(n_assign + N_EXPERTS * (MOE_BLK - 1)) // MOE_BLK)
    n_slots = nblk * MOE_BLK
    blk = jnp.arange(nblk, dtype=jnp.int32)
    last_of_expert = jnp.any((blk[:, None] == (pad_end // MOE_BLK - 1)[None, :]) & (counts > 0)[None, :], axis=1)
    zero_blk = (last_of_expert | (blk >= pad_end[-1] // MOE_BLK)).astype(jnp.int32)
    k_e = padded // MOE_BLK
    sbc = (k_e + MOE_SB - 1) // MOE_SB
    sb_end = jnp.cumsum(sbc)
    sb_start = sb_end - sbc
    n_sb = (nblk + (MOE_SB - 1) * N_EXPERTS) // MOE_SB
    s = jnp.arange(n_sb, dtype=jnp.int32)
    sb_e = jnp.minimum(jnp.sum((s[:, None] >= sb_end[None, :]).astype(jnp.int32), axis=1), N_EXPERTS - 1)
    j = s - sb_start[sb_e]
    real = s < sb_end[-1]
    tail_blk = pad_end[-1] // MOE_BLK + MOE_SB * (s - sb_end[-1])
    sb_m = jnp.where(real, jnp.clip(k_e[sb_e] - MOE_SB * j, 0, MOE_SB), jnp.clip(nblk - tail_blk, 0, MOE_SB))
    sb_n = jnp.where(real, jnp.clip(counts[sb_e] - MOE_SB * MOE_BLK * j, 0, MOE_SB * MOE_BLK), 0)
    sb_base = jnp.where(real, pad_start[sb_e] + MOE_SB * MOE_BLK * j, jnp.minimum(tail_blk, nblk - 1) * MOE_BLK)
    i32 = lambda a: a.astype(jnp.int32)
    return (i32(sb_e), i32(sb_m), i32(sb_n), i32(sb_base)), i32(dest), zero_blk, n_slots


def _ple_out_kernel(dest_ref, h1_ref, rt_ref, pp_ref, ps_ref, nple_ref, wpg_ref, wp_ref, nfin_ref, ys_hbm,
                    op_ref, os_ref, yb_ref, sem, *, n_prompt_tiles):
    i = pl.program_id(0)
    nt = pl.num_programs(0)
    tm = h1_ref.shape[0]
    t_all = nt * tm
    slot = lax.rem(i, 2)

    def issue(tile, sl):
        base = tile * tm

        def grp(g, c):
            for j in range(DMA_UNROLL):
                row = g * (DMA_UNROLL // 2) + j // 2
                d = dest_ref[(j % 2) * t_all + base + row]
                pltpu.make_async_copy(ys_hbm.at[pl.ds(d, 1), :], yb_ref.at[sl, j % 2, pl.ds(row, 1), :],
                                      sem.at[sl]).start()
            return c

        lax.fori_loop(0, 2 * tm // DMA_UNROLL, grp, 0)

    @pl.when(i == 0)
    def _():
        issue(0, 0)

    def wait_rows(sl):
        for k in range(2):
            pltpu.make_async_copy(ys_hbm.at[pl.ds(0, tm), :], yb_ref.at[sl, k], sem.at[sl]).wait()

    wait_rows(slot)
    is_p = i < n_prompt_tiles
    rt = rt_ref[...]
    w0, w1 = rt[:, 2:3], rt[:, 3:4]
    lo0, hi0 = _unpack_bf16_pair(yb_ref[slot, 0])
    lo1, hi1 = _unpack_bf16_pair(yb_ref[slot, 1])
    h2 = h1_ref[...] + jnp.concatenate([w0 * lo0 + w1 * lo1, w0 * hi0 + w1 * hi1], axis=1)
    hb = _rms(h2, nple_ref[...]).astype(BF16)
    nbase = jnp.minimum(i + 1, nt - 1) * tm
    n_chunk = 8
    cw, ca = D_MODEL // n_chunk, 2 * tm // n_chunk
    gates = []
    for c in range(n_chunk):
        gates.append(jax.nn.sigmoid(_dot(hb, wpg_ref[:, c * cw:(c + 1) * cw])))
        for a in range(c * ca, (c + 1) * ca):
            d = dest_ref[(a % 2) * t_all + nbase + a // 2]
            pltpu.make_async_copy(ys_hbm.at[pl.ds(d, 1), :], yb_ref.at[1 - slot, a % 2, pl.ds(a // 2, 1), :],
                                  sem.at[1 - slot]).start()
    gate = jnp.concatenate(gates, axis=1)

    @pl.when(i == nt - 1)
    def _():
        wait_rows(1 - slot)

    p = jnp.where(is_p, pp_ref[...], ps_ref[...])
    h3 = h2 + _dot(p.astype(BF16), wp_ref[...]) * gate
    y = _rms(h3, nfin_ref[...])

    @pl.when(is_p)
    def _():
        op_ref[...] = y

    @pl.when(jnp.logical_not(is_p))
    def _():
        os_ref[...] = y


def _ple_out(dest, h1, rt, pp, ps, nple, wpg, wp, nfin, y_slots):
    tm = TOK_TILE
    npt, nst = pp.shape[0] // tm, ps.shape[0] // tm
    row = lambda w: pl.BlockSpec((tm, w), lambda i, d: (i, 0))
    prow = lambda w: pl.BlockSpec((tm, w), lambda i, d: (jnp.minimum(i, npt - 1), 0))
    srow = lambda w: pl.BlockSpec((tm, w), lambda i, d: (jnp.maximum(i - npt, 0), 0))
    const = lambda shape: pl.BlockSpec(shape, lambda i, d: (0,) * len(shape), pipeline_mode=pl.Buffered(1))
    grid_spec = pltpu.PrefetchScalarGridSpec(
        num_scalar_prefetch=1,
        grid=(npt + nst,),
        in_specs=[row(D_MODEL), row(LANES), prow(D_PLE), srow(D_PLE), const((1, D_MODEL)), const(wpg.shape),
                  const(wp.shape), const((1, D_MODEL)), pl.BlockSpec(memory_space=pl.ANY)],
        out_specs=[prow(D_MODEL), srow(D_MODEL)],
        scratch_shapes=[pltpu.VMEM((2, 2, tm, HALF), jnp.uint32), pltpu.SemaphoreType.DMA((2,))],
    )
    return pl.pallas_call(
        functools.partial(_ple_out_kernel, n_prompt_tiles=npt),
        grid_spec=grid_spec,
        out_shape=[jax.ShapeDtypeStruct((pp.shape[0], D_MODEL), F32),
                   jax.ShapeDtypeStruct((ps.shape[0], D_MODEL), F32)],
        compiler_params=pltpu.CompilerParams(dimension_semantics=("arbitrary",), vmem_limit_bytes=VMEM_LIMIT),
        name="ple_out",
    )(dest, h1, rt, pp, ps, nple, wpg, wp, nfin, y_slots)


def _s5_params(lam_re, lam_im, log_dt, b_re, b_im, c_re, c_im, d_skip):
    dt = jnp.exp(log_dt)[:, None]
    mag = jnp.exp(lam_re * dt)
    ab_re = mag * jnp.cos(lam_im * dt)
    ab_im = mag * jnp.sin(lam_im * dt)
    den = lam_re * lam_re + lam_im * lam_im
    nr = ab_re - 1.0
    f_re = (nr * lam_re + ab_im * lam_im) / den
    f_im = (ab_im * lam_re - nr * lam_im) / den
    bb_re = f_re[..., None] * b_re - f_im[..., None] * b_im
    bb_im = f_re[..., None] * b_im + f_im[..., None] * b_re
    eye = jnp.eye(S5_GB, dtype=F32)

    def bu_w(bb):
        bb = bb.reshape(S5_NGB, S5_GB, S5_STATE, S5_GROUP)
        w = jnp.einsum('nlph,lm->nlhmp', bb, eye)
        return w.reshape(S5_NGB, S5_GB * S5_GROUP, S5_SL)

    def c_w(c):
        c = c.reshape(S5_NGB, S5_GB, S5_GROUP, S5_STATE)
        w = jnp.einsum('nlhp,lm->nlpmh', c, eye)
        return w.reshape(S5_NGB, S5_SL, S5_GB * S5_GROUP)

    wbu = jnp.concatenate([bu_w(bb_re), bu_w(bb_im)], axis=2).astype(BF16)
    wc = jnp.concatenate([c_w(c_re), -c_w(c_im)], axis=1).astype(BF16)
    a_re = ab_re.reshape(S5_NGB, 1, S5_SL)
    a_im = ab_im.reshape(S5_NGB, 1, S5_SL)
    dsk = d_skip.reshape(S5_NGB, 1, S5_GB * S5_GROUP)
    return wbu, wc, a_re, a_im, dsk


def kernel(x_prompt, x_sample, p_prompt, p_sample, state_gla, state_s5_re, state_s5_im, norm_mix, w_in, gla_w_gate_up, gla_gate_bias, gla_norm, s5_lam_re, s5_lam_im, s5_log_dt, s5_b_re, s5_b_im, s5_c_re, s5_c_im, s5_d, s5_glu_w, s5_glu_b, s5_norm, w_out, norm_ffn, router_group, router_expert, w_gate, w_up, w_down, norm_ple, w_ple, w_ple_gate, norm_final):
    depth = w_in.shape[0]
    assert depth == 1
    i = 0
    bp, lp, _ = x_prompt.shape
    bs, ls, _ = x_sample.shape
    tp, ts = bp * lp, bs * ls
    t_all = tp + ts

    wm = _w_in_prep(w_in[i])
    wgu = jnp.pad(gla_w_gate_up[i], ((0, LANES - GLA_RANK), (0, 0))).astype(BF16)
    gbias = gla_gate_bias[i].reshape(1, QK_W)
    wbu, wc, a_re, a_im, dsk = _s5_params(s5_lam_re[i], s5_lam_im[i], s5_log_dt[i], s5_b_re[i], s5_b_im[i],
                                          s5_c_re[i], s5_c_im[i], s5_d[i])
    glu_w = s5_glu_w[i].astype(BF16)
    wo = w_out[i].astype(BF16)
    wr32 = jnp.pad(jnp.concatenate([router_group[i], router_expert[i]], axis=1),
                   ((0, 0), (0, LANES - N_EGROUPS - N_EXPERTS)))
    wr_hi = wr32.astype(BF16)
    wr = jnp.stack([wr_hi, (wr32 - wr_hi.astype(F32)).astype(BF16)])
    wpg = w_ple_gate[i].astype(BF16)
    wp = w_ple[i].astype(BF16)
    vec = lambda a: a.reshape(1, -1)

    xp = x_prompt.reshape(tp, D_MODEL)
    xs = x_sample.reshape(ts, D_MODEL)

    qp, kp, vp, rp, lap, up = _in_proj(xp, vec(norm_mix[i]), wm, wgu, gbias, BF16)
    qs, ks, vs, rs, las, us = _in_proj(xs, vec(norm_mix[i]), wm, wgu, gbias, BF16)
    ogp, gla_p = _gla_prompt(qp, kp, vp, rp, lap, vec(gla_norm[i]), bp, lp)
    ogs, gla_s = _gla_sample(qs, ks, vs, rs, las, vec(gla_norm[i]), state_gla[i], bs, ls)
    wbu2, wc2, a_re2, a_im2 = _s5_prompt_params(wbu, wc, a_re, a_im, bp)
    ysp, re_p, im_p = _s5_prompt(up.reshape(bp, lp, D_S5), wbu2, wc2, a_re2, a_im2, dsk, tc=256)
    yss, re_s, im_s = _s5(us.reshape(bs, ls, D_S5), wbu, wc, a_re, a_im, dsk,
                          state_s5_re[i].reshape(bs, -1), state_s5_im[i].reshape(bs, -1),
                          nb=32, tc=ls, has_state=True)

    h1, hn_all, rt_all, rtt, cnt = _mix_out(xp, xs, ogp, ogs, ysp.reshape(tp, D_S5), yss.reshape(ts, D_S5),
                                            glu_w, vec(s5_glu_b[i]), vec(s5_norm[i]), wo, vec(norm_ffn[i]), wr)

    plan, dest, zero_blk, n_slots = _moe_plan(rtt, cnt)
    x_slots = _dispatch(dest, zero_blk, hn_all, n_slots)
    y_slots = _moe(x_slots, plan, w_gate[i], w_up[i], w_down[i], n_slots)

    y_p, y_s = _ple_out(dest, h1, rt_all, p_prompt[i].reshape(tp, D_PLE), p_sample[i].reshape(ts, D_PLE),
                        vec(norm_ple[i]), wpg, wp, vec(norm_final), y_slots)

    s5shape = lambda a, b: a.reshape(1, b, S5_GROUPS, S5_STATE)
    return (y_p.reshape(bp, lp, D_MODEL), y_s.reshape(bs, ls, D_MODEL),
            gla_p[None], s5shape(re_p, bp), s5shape(im_p, bp),
            gla_s[None], s5shape(re_s, bs), s5shape(im_s, bs))
```

```python
import functools
import math

import jax
import jax.numpy as jnp
from jax import lax
from jax.experimental import pallas as pl
from jax.experimental.pallas import tpu as pltpu

F32 = jnp.float32
BF16 = jnp.bfloat16

D_MODEL = 2048
D_GLA = 1024
D_S5 = 1024
GLA_HEADS = 4
GLA_DV = 256
GLA_DK = 128
GLA_RANK = 16
GLA_CHUNK = 64
S5_GROUP = 16
S5_GROUPS = 64
S5_STATE = 64
N_EGROUPS = 4
N_EPG = 8
N_EXPERTS = 32
D_EXPERT = 512
D_PLE = 256
EPS = 1e-6

LANES = 128
QK_W = GLA_HEADS * GLA_DK
S5_GB = 8
S5_NGB = S5_GROUPS // S5_GB
S5_SL = S5_GB * S5_STATE
TOK_TILE = 256
MOE_BLK = 128
VMEM_LIMIT = 56 * 1024 * 1024
MOE_VMEM_LIMIT = 60 * 1024 * 1024


def _const_spec(shape):
    nd = len(shape)
    return pl.BlockSpec(shape, lambda *_: (0,) * nd, pipeline_mode=pl.Buffered(1))


def _rms(x, g):
    return x * lax.rsqrt(jnp.mean(x * x, axis=-1, keepdims=True) + EPS) * g


def _dot(a, b):
    return jnp.dot(a, b, preferred_element_type=F32)


def _log_sigmoid(x):
    return -(jnp.maximum(-x, 0.0) + jnp.log1p(jnp.exp(-jnp.abs(x))))


N_QKVR = 2 * QK_W + 2 * D_GLA
W_IN_COLS = N_QKVR + GLA_RANK + D_S5


def _w_in_prep_kernel(w_ref, o_ref):
    o_ref[:, 0:N_QKVR] = w_ref[:, 0:N_QKVR].astype(BF16)
    tail = w_ref[:, N_QKVR:W_IN_COLS]
    o_ref[:, N_QKVR:N_QKVR + D_S5] = tail[:, GLA_RANK:GLA_RANK + D_S5].astype(BF16)
    o_ref[:, N_QKVR + D_S5:N_QKVR + D_S5 + LANES] = tail[:, 0:LANES].astype(BF16)


def _w_in_prep(wi):
    rows = 256
    return pl.pallas_call(
        _w_in_prep_kernel,
        grid=(D_MODEL // rows,),
        in_specs=[pl.BlockSpec((rows, W_IN_COLS), lambda i: (i, 0))],
        out_specs=pl.BlockSpec((rows, N_QKVR + D_S5 + LANES), lambda i: (i, 0)),
        out_shape=jax.ShapeDtypeStruct((D_MODEL, N_QKVR + D_S5 + LANES), BF16),
        compiler_params=pltpu.CompilerParams(dimension_semantics=("arbitrary",), vmem_limit_bytes=VMEM_LIMIT),
        name="w_in_prep",
    )(wi)


def _in_proj_kernel(x_ref, g_ref, wm_ref, wgu_ref, gb_ref,
                    q_ref, k_ref, v_ref, r_ref, la_ref, u_ref):
    hb = _rms(x_ref[...], g_ref[...]).astype(BF16)

    def seg(a, b):
        return _dot(hb, wm_ref[:, a:b])

    q_ref[...] = (seg(0, QK_W) * (GLA_DK ** -0.5)).astype(q_ref.dtype)
    k_ref[...] = seg(QK_W, 2 * QK_W).astype(k_ref.dtype)
    v_ref[...] = seg(2 * QK_W, 2 * QK_W + D_GLA).astype(v_ref.dtype)
    r_ref[...] = seg(2 * QK_W + D_GLA, N_QKVR).astype(r_ref.dtype)
    u_ref[...] = seg(N_QKVR, N_QKVR + D_S5)
    zg = seg(N_QKVR + D_S5, N_QKVR + D_S5 + LANES)
    xg = _dot(zg.astype(BF16), wgu_ref[...]) + gb_ref[...]
    la_ref[...] = _log_sigmoid(xg) * (1.0 / 16.0)


def _in_proj(x2d, g, wm, wgu, gbias, act_dtype):
    t = x2d.shape[0]
    tm = TOK_TILE
    row = lambda w: pl.BlockSpec((tm, w), lambda i: (i, 0))
    return pl.pallas_call(
        _in_proj_kernel,
        grid=(t // tm,),
        in_specs=[row(D_MODEL), _const_spec((1, D_MODEL)), _const_spec(wm.shape),
                  _const_spec(wgu.shape), _const_spec((1, QK_W))],
        out_specs=[row(QK_W), row(QK_W), row(D_GLA), row(D_GLA), row(QK_W), row(D_S5)],
        out_shape=[jax.ShapeDtypeStruct((t, QK_W), act_dtype), jax.ShapeDtypeStruct((t, QK_W), act_dtype),
                   jax.ShapeDtypeStruct((t, D_GLA), act_dtype), jax.ShapeDtypeStruct((t, D_GLA), act_dtype),
                   jax.ShapeDtypeStruct((t, QK_W), F32), jax.ShapeDtypeStruct((t, D_S5), F32)],
        compiler_params=pltpu.CompilerParams(dimension_semantics=("arbitrary",), vmem_limit_bytes=VMEM_LIMIT),
        name="in_proj",
    )(x2d, g, wm, wgu, gbias)


_NT = (((1,), (1,)), ((), ()))
_TN = (((0,), (0,)), ((), ()))


def _gla_pre(q, k, la, c):
    r = q.shape[0]
    shift = int(math.log2(c))
    ri = lax.broadcasted_iota(jnp.int32, (r, r), 0)
    si = lax.broadcasted_iota(jnp.int32, (r, r), 1)
    mask = ((ri >> shift) == (si >> shift)) & (ri >= si)
    tri = jnp.where(mask, 1.0, 0.0).astype(BF16)
    hi = la.astype(BF16)
    r1 = la - hi.astype(F32)
    mid = r1.astype(BF16)
    lo = (r1 - mid.astype(F32)).astype(BF16)
    cum = _dot(tri, hi) + _dot(tri, mid) + _dot(tri, lo)
    last = jnp.concatenate([jnp.broadcast_to(cum[(i + 1) * c - 1:(i + 1) * c, :], (c, cum.shape[1]))
                            for i in range(r // c)], axis=0)
    qe = (q * jnp.exp(cum)).astype(BF16)
    ke = (k * jnp.exp(-cum)).astype(BF16)
    kd = (k * jnp.exp(last - cum)).astype(BF16)
    return qe, ke, kd, cum, mask


def _gla_intra(qe, ke, v, mask):
    sc = lax.dot_general(qe, ke, _NT, preferred_element_type=F32)
    return _dot(jnp.where(mask, sc, 0.0).astype(BF16), v)


def _gla_finish(o, r, g):
    rf = r.astype(F32)
    return _rms(o, g) * (rf * jax.nn.sigmoid(rf))


GLA_SAFE_LOG_DECAY = -60.0


def _gla_token_step(t, rows, q, k, v, a_all, st_ref, oacc_ref):
    m = rows == t
    a = jnp.sum(jnp.where(m, a_all, 0.0), axis=0, keepdims=True)
    kt = jnp.where(m, k, 0.0).astype(BF16)
    qt = jnp.where(m, q, 0.0).astype(BF16)
    vt = jnp.where(m, v, jnp.zeros_like(v))
    st = st_ref[...] * a + lax.dot_general(vt, kt, _TN, preferred_element_type=F32)
    st_ref[...] = st
    oacc_ref[...] += lax.dot_general(qt, st.astype(BF16), _NT, preferred_element_type=F32)


def _gla_prompt_kernel(q_ref, k_ref, v_ref, r_ref, la_ref, g_ref, o_ref, sfin_ref, st_ref, oacc_ref,
                       *, n_chunks):
    j = pl.program_id(1)

    @pl.when(j == 0)
    def _():
        st_ref[...] = jnp.zeros_like(st_ref)

    c = GLA_CHUNK
    qe, ke, kd, cum, mask = _gla_pre(q_ref[...].astype(F32), k_ref[...].astype(F32), la_ref[...], c)
    g = g_ref[...]
    safe = jnp.min(cum) >= GLA_SAFE_LOG_DECAY

    @pl.when(safe)
    def _():
        for h in range(GLA_HEADS):
            kc = slice(h * GLA_DK, (h + 1) * GLA_DK)
            vc = slice(h * GLA_DV, (h + 1) * GLA_DV)
            v = v_ref[:, vc]
            qe_h, kd_h = qe[:, kc], kd[:, kc]
            o = _gla_intra(qe_h, ke[:, kc], v, mask)
            st = st_ref[h]
            inter = []
            for ci in range(n_chunks):
                rows = slice(ci * c, (ci + 1) * c)
                inter.append(lax.dot_general(qe_h[rows], st.astype(BF16), _NT, preferred_element_type=F32))
                dec = jnp.exp(cum[(ci + 1) * c - 1:(ci + 1) * c, kc])
                st = st * dec + lax.dot_general(v[rows], kd_h[rows], _TN, preferred_element_type=F32)
            st_ref[h] = st
            o = o + jnp.concatenate(inter, axis=0)
            o_ref[:, vc] = _gla_finish(o, r_ref[:, vc], g).astype(o_ref.dtype)

    @pl.when(jnp.logical_not(safe))
    def _():
        n_rows = n_chunks * c
        rows = lax.broadcasted_iota(jnp.int32, (n_rows, 1), 0)
        for h in range(GLA_HEADS):
            kc = slice(h * GLA_DK, (h + 1) * GLA_DK)
            vc = slice(h * GLA_DV, (h + 1) * GLA_DV)
            q, k, v = q_ref[:, kc].astype(F32), k_ref[:, kc].astype(F32), v_ref[:, vc]
            a_all = jnp.exp(la_ref[:, kc])
            oacc_ref[...] = jnp.zeros_like(oacc_ref)

            def step(t, carry):
                _gla_token_step(t, rows, q, k, v, a_all, st_ref.at[h], oacc_ref)
                return carry

            lax.fori_loop(0, n_rows, step, 0)
            o_ref[:, vc] = _gla_finish(oacc_ref[...], r_ref[:, vc], g).astype(o_ref.dtype)

    @pl.when(j == pl.num_programs(1) - 1)
    def _():
        for h in range(GLA_HEADS):
            sfin_ref[0, h] = st_ref[h].T


def _gla_prompt(q, k, v, r, la, g, batch, seq):
    rb = 4 * GLA_CHUNK
    nj = seq // rb
    row = lambda w: pl.BlockSpec((rb, w), lambda b, j: (b * nj + j, 0))
    return pl.pallas_call(
        functools.partial(_gla_prompt_kernel, n_chunks=rb // GLA_CHUNK),
        grid=(batch, nj),
        in_specs=[row(QK_W), row(QK_W), row(D_GLA), row(D_GLA), row(QK_W), _const_spec((1, GLA_DV))],
        out_specs=[row(D_GLA),
                   pl.BlockSpec((1, GLA_HEADS, GLA_DK, GLA_DV), lambda b, j: (b, 0, 0, 0))],
        out_shape=[jax.ShapeDtypeStruct((batch * seq, D_GLA), BF16),
                   jax.ShapeDtypeStruct((batch, GLA_HEADS, GLA_DK, GLA_DV), F32)],
        scratch_shapes=[pltpu.VMEM((GLA_HEADS, GLA_DV, GLA_DK), F32), pltpu.VMEM((rb, GLA_DV), F32)],
        compiler_params=pltpu.CompilerParams(dimension_semantics=("arbitrary", "arbitrary"),
                                             vmem_limit_bytes=VMEM_LIMIT),
        name="gla_prompt",
    )(q, k, v, r, la, g)


def _gla_sample_kernel(q_ref, k_ref, v_ref, r_ref, la_ref, g_ref, s0_ref, o_ref, sfin_ref, st_ref, oacc_ref,
                       *, n_seq, seq):
    qe, ke, kd, cum, mask = _gla_pre(q_ref[...].astype(F32), k_ref[...].astype(F32), la_ref[...], seq)
    g = g_ref[...]
    safe = jnp.min(cum) >= GLA_SAFE_LOG_DECAY

    @pl.when(safe)
    def _():
        lasts = jnp.concatenate([cum[(s + 1) * seq - 1:(s + 1) * seq, :] for s in range(n_seq)]
                                + [jnp.zeros((GLA_DK - n_seq, cum.shape[1]), F32)], axis=0)
        pair = 2 * seq
        upper = lax.broadcasted_iota(jnp.int32, (pair, GLA_DK), 0) < seq
        for h in range(GLA_HEADS):
            kc = slice(h * GLA_DK, (h + 1) * GLA_DK)
            vc = slice(h * GLA_DV, (h + 1) * GLA_DV)
            v = v_ref[:, vc]
            qe_h, kd_h = qe[:, kc], kd[:, kc]
            o = _gla_intra(qe_h, ke[:, kc], v, mask)
            dec_t = jnp.exp(lasts[:, kc].T)
            inter = []
            for p in range(n_seq // 2):
                rows = slice(p * pair, (p + 1) * pair)
                qe_p, kd_p, v_p = qe_h[rows], kd_h[rows], v[rows]
                for half in range(2):
                    s = 2 * p + half
                    s0 = s0_ref[s, h]
                    o_s = _dot(qe_p, s0.astype(BF16))
                    inter.append(o_s[half * seq:(half + 1) * seq])
                    kd_s = jnp.where(upper if half == 0 else ~upper, kd_p, jnp.zeros_like(kd_p))
                    dec = jnp.broadcast_to(dec_t[:, s:s + 1], (GLA_DK, GLA_DV))
                    sfin_ref[s, h] = s0 * dec + lax.dot_general(kd_s, v_p, _TN, preferred_element_type=F32)
            o = o + jnp.concatenate(inter, axis=0)
            o_ref[:, vc] = _gla_finish(o, r_ref[:, vc], g).astype(o_ref.dtype)

    @pl.when(jnp.logical_not(safe))
    def _():
        n_rows = n_seq * seq
        rows = lax.broadcasted_iota(jnp.int32, (n_rows, 1), 0)
        for h in range(GLA_HEADS):
            kc = slice(h * GLA_DK, (h + 1) * GLA_DK)
            vc = slice(h * GLA_DV, (h + 1) * GLA_DV)
            q, k, v = q_ref[:, kc].astype(F32), k_ref[:, kc].astype(F32), v_ref[:, vc]
            a_all = jnp.exp(la_ref[:, kc])
            oacc_ref[...] = jnp.zeros_like(oacc_ref)

            def step(t, carry):
                s = t // seq

                @pl.when(t % seq == 0)
                def _():
                    st_ref[...] = s0_ref[s, h].T

                _gla_token_step(t, rows, q, k, v, a_all, st_ref, oacc_ref)

                @pl.when(t % seq == seq - 1)
                def _():
                    sfin_ref[s, h] = st_ref[...].T

                return carry

            lax.fori_loop(0, n_rows, step, 0)
            o_ref[:, vc] = _gla_finish(oacc_ref[...], r_ref[:, vc], g).astype(o_ref.dtype)


def _gla_sample(q, k, v, r, la, g, s0, batch, seq):
    ns = 16
    rb = ns * seq
    row = lambda w: pl.BlockSpec((rb, w), lambda i: (i, 0))
    st = pl.BlockSpec((ns, GLA_HEADS, GLA_DK, GLA_DV), lambda i: (i, 0, 0, 0))
    return pl.pallas_call(
        functools.partial(_gla_sample_kernel, n_seq=ns, seq=seq),
        grid=(batch // ns,),
        in_specs=[row(QK_W), row(QK_W), row(D_GLA), row(D_GLA), row(QK_W), _const_spec((1, GLA_DV)), st],
        out_specs=[row(D_GLA), st],
        out_shape=[jax.ShapeDtypeStruct((batch * seq, D_GLA), BF16),
                   jax.ShapeDtypeStruct((batch, GLA_HEADS, GLA_DK, GLA_DV), F32)],
        scratch_shapes=[pltpu.VMEM((GLA_DV, GLA_DK), F32), pltpu.VMEM((rb, GLA_DV), F32)],
        compiler_params=pltpu.CompilerParams(dimension_semantics=("arbitrary",), vmem_limit_bytes=VMEM_LIMIT),
        name="gla_sample",
    )(q, k, v, r, la, g, s0)


def _s5_kernel(u_ref, wbu_ref, wc_ref, are_ref, aim_ref, d_ref, h0r_ref, h0i_ref,
               y_ref, sre_ref, sim_ref, bu_ref, xs_ref, car_ref, *, nb, tc, has_state):
    j = pl.program_id(2)

    @pl.when(j == 0)
    def _():
        if has_state:
            car_ref[0] = h0r_ref[...]
            car_ref[1] = h0i_ref[...]
        else:
            car_ref[...] = jnp.zeros_like(car_ref)

    u2 = u_ref[...].reshape(nb * tc, LANES)
    ub = u2.astype(BF16)
    nl = S5_SL // LANES
    for l in range(2 * nl):
        bu_ref[l] = _dot(ub, wbu_ref[0, :, l * LANES:(l + 1) * LANES])
    a_r = [jnp.broadcast_to(are_ref[0, :, l * LANES:(l + 1) * LANES], (nb, LANES)) for l in range(nl)]
    a_i = [jnp.broadcast_to(aim_ref[0, :, l * LANES:(l + 1) * LANES], (nb, LANES)) for l in range(nl)]

    def step(t, carry):
        rows = pl.ds(t, nb, stride=tc)
        out = []
        for l in range(nl):
            xr, xi = carry[2 * l], carry[2 * l + 1]
            nr = a_r[l] * xr - a_i[l] * xi + bu_ref[l, rows, :]
            ni = a_r[l] * xi + a_i[l] * xr + bu_ref[nl + l, rows, :]
            xs_ref[l, rows, :] = nr
            xs_ref[nl + l, rows, :] = ni
            out += [nr, ni]
        return tuple(out)

    init = []
    for l in range(nl):
        init += [car_ref[0, :, l * LANES:(l + 1) * LANES], car_ref[1, :, l * LANES:(l + 1) * LANES]]
    fin = lax.fori_loop(0, tc, step, tuple(init), unroll=8)
    xr = jnp.concatenate([fin[2 * l] for l in range(nl)], axis=1)
    xi = jnp.concatenate([fin[2 * l + 1] for l in range(nl)], axis=1)
    car_ref[0] = xr
    car_ref[1] = xi
    y = d_ref[0] * u2
    for l in range(2 * nl):
        y = y + _dot(xs_ref[l].astype(BF16), wc_ref[0, l * LANES:(l + 1) * LANES, :])
    y_ref[...] = y.reshape(nb, tc, LANES)

    @pl.when(j == pl.num_programs(2) - 1)
    def _():
        sre_ref[...] = xr
        sim_ref[...] = xi


def _s5(u3d, wbu, wc, a_re, a_im, dsk, h0r, h0i, nb, tc, has_state):
    batch, seq, _ = u3d.shape
    grid = (S5_NGB, batch // nb, seq // tc)
    st = pl.BlockSpec((nb, S5_SL), lambda g, b, j: (b, g))
    par = lambda w: pl.BlockSpec((1, 1, w), lambda g, b, j: (g, 0, 0))
    ublk = pl.BlockSpec((nb, tc, LANES), lambda g, b, j: (b, j, g))
    return pl.pallas_call(
        functools.partial(_s5_kernel, nb=nb, tc=tc, has_state=has_state),
        grid=grid,
        in_specs=[ublk,
                  pl.BlockSpec((1, LANES, 2 * S5_SL), lambda g, b, j: (g, 0, 0)),
                  pl.BlockSpec((1, 2 * S5_SL, LANES), lambda g, b, j: (g, 0, 0)),
                  par(S5_SL), par(S5_SL), par(LANES), st, st],
        out_specs=[ublk, st, st],
        out_shape=[jax.ShapeDtypeStruct(u3d.shape, F32),
                   jax.ShapeDtypeStruct((batch, S5_GROUPS * S5_STATE), F32),
                   jax.ShapeDtypeStruct((batch, S5_GROUPS * S5_STATE), F32)],
        scratch_shapes=[pltpu.VMEM((2 * S5_SL // LANES, nb * tc, LANES), F32),
                        pltpu.VMEM((2 * S5_SL // LANES, nb * tc, LANES), F32),
                        pltpu.VMEM((2, nb, S5_SL), F32)],
        compiler_params=pltpu.CompilerParams(dimension_semantics=("arbitrary",) * 3,
                                             vmem_limit_bytes=VMEM_LIMIT),
        name="s5_state" if has_state else "s5_zero",
    )(u3d, wbu, wc, a_re, a_im, dsk, h0r, h0i)


S5_J = 2
S5_HL = S5_SL // S5_J
S5_PAIR = 2


def _s5_prompt_kernel(u_ref, wbu_ref, wc_ref, are_ref, aim_ref, d_ref, y_ref, sre_ref, sim_ref,
                      u2_ref, lhs_ref, bu_ref, xs_ref, y2_ref, yo_ref, car_ref, *, nb, tc):
    g = pl.program_id(0)
    j = pl.program_id(1)
    rows = nb * S5_J
    npair = S5_PAIR

    @pl.when((g == 0) & (j == 0))
    def _():
        lhs_ref[...] = jnp.zeros_like(lhs_ref)

    @pl.when(j == 0)
    def _():
        car_ref[...] = jnp.zeros_like(car_ref)

    u2 = u_ref[...].reshape(nb * tc, npair * LANES)
    for p in range(npair):
        u2_ref[p] = u2[:, p * LANES:(p + 1) * LANES]

    def build(t, c):
        for p in range(npair):
            u4 = u2_ref[p, pl.ds(t, nb, stride=tc), :]
            for jj in range(S5_J):
                lhs_ref[p, t, jj * nb:(jj + 1) * nb, jj * LANES:(jj + 1) * LANES] = u4
        return c

    lax.fori_loop(0, tc, build, 0, unroll=8)
    for p in range(npair):
        lhs = lhs_ref[p].reshape(tc * rows, S5_J * LANES).astype(BF16)
        bu_ref[p] = _dot(lhs, wbu_ref[p]).reshape(tc, rows, 2 * S5_HL)
    ar = [are_ref[p] for p in range(npair)]
    ai = [aim_ref[p] for p in range(npair)]

    def step(t, carry):
        out = []
        for p in range(npair):
            xr, xi = carry[2 * p], carry[2 * p + 1]
            tile = bu_ref[p, t]
            nr = ar[p] * xr - ai[p] * xi + tile[:, 0:S5_HL]
            ni = ar[p] * xi + ai[p] * xr + tile[:, S5_HL:2 * S5_HL]
            xs_ref[p, t] = jnp.concatenate([nr, ni], axis=1)
            out += [nr, ni]
        return tuple(out)

    init = tuple(car_ref[p, c] for p in range(npair) for c in range(2))
    fin = lax.fori_loop(0, tc, step, init, unroll=8)
    for p in range(npair):
        car_ref[p, 0] = fin[2 * p]
        car_ref[p, 1] = fin[2 * p + 1]
        xs = xs_ref[p].reshape(tc * rows, 2 * S5_HL).astype(BF16)
        y2_ref[p] = _dot(xs, wc_ref[p]).reshape(tc, rows, S5_J * LANES)
    first_half = lax.broadcasted_iota(jnp.int32, (rows, LANES), 0) < nb

    def unperm(t, c):
        for p in range(npair):
            t2 = y2_ref[p, t]
            part = jnp.where(first_half, t2[:, 0:LANES], t2[:, LANES:2 * LANES])
            yo_ref[p, pl.ds(t, nb, stride=tc), :] = (part + pltpu.roll(part, nb, axis=0))[0:nb]
        return c

    lax.fori_loop(0, tc, unperm, 0, unroll=8)
    y = jnp.concatenate([yo_ref[p] + d_ref[p] * u2_ref[p] for p in range(npair)], axis=1)
    y_ref[...] = y.reshape(nb, tc, npair * LANES)

    @pl.when(j == pl.num_programs(1) - 1)
    def _():
        def gather_state(x):
            return jnp.concatenate([x[jj * nb:(jj + 1) * nb] for jj in range(S5_J)], axis=1)

        sre_ref[...] = jnp.concatenate([gather_state(fin[2 * p]) for p in range(npair)], axis=1)
        sim_ref[...] = jnp.concatenate([gather_state(fin[2 * p + 1]) for p in range(npair)], axis=1)


def _s5_prompt(u3d, wbu2, wc2, a_re2, a_im2, dsk, tc):
    nb, seq, _ = u3d.shape
    rows = nb * S5_J
    assert rows == 8
    np_ = S5_PAIR
    st = pl.BlockSpec((nb, np_ * S5_SL), lambda g, j: (0, g))
    gblk = lambda s: pl.BlockSpec((np_,) + s, lambda g, j: (g, 0, 0))
    ublk = pl.BlockSpec((nb, tc, np_ * LANES), lambda g, j: (0, j, g))
    return pl.pallas_call(
        functools.partial(_s5_prompt_kernel, nb=nb, tc=tc),
        grid=(S5_NGB // np_, seq // tc),
        in_specs=[ublk, gblk((S5_J * LANES, 2 * S5_HL)), gblk((2 * S5_HL, S5_J * LANES)),
                  gblk((rows, S5_HL)), gblk((rows, S5_HL)), gblk((1, LANES))],
        out_specs=[ublk, st, st],
        out_shape=[jax.ShapeDtypeStruct(u3d.shape, F32),
                   jax.ShapeDtypeStruct((nb, S5_GROUPS * S5_STATE), F32),
                   jax.ShapeDtypeStruct((nb, S5_GROUPS * S5_STATE), F32)],
        scratch_shapes=[pltpu.VMEM((np_, nb * tc, LANES), F32),
                        pltpu.VMEM((np_, tc, rows, S5_J * LANES), F32),
                        pltpu.VMEM((np_, tc, rows, 2 * S5_HL), F32),
                        pltpu.VMEM((np_, tc, rows, 2 * S5_HL), F32),
                        pltpu.VMEM((np_, tc, rows, S5_J * LANES), F32),
                        pltpu.VMEM((np_, nb * tc, LANES), F32),
                        pltpu.VMEM((np_, 2, rows, S5_HL), F32)],
        compiler_params=pltpu.CompilerParams(dimension_semantics=("arbitrary",) * 2,
                                             vmem_limit_bytes=VMEM_LIMIT),
        name="s5_prompt",
    )(u3d, wbu2, wc2, a_re2, a_im2, dsk)


def _s5_prompt_params(wbu, wc, a_re, a_im, nb):
    h = S5_HL
    top = jnp.concatenate([wbu[:, :, 0:h], wbu[:, :, S5_SL:S5_SL + h]], axis=2)
    bot = jnp.concatenate([wbu[:, :, h:2 * h], wbu[:, :, S5_SL + h:S5_SL + 2 * h]], axis=2)
    wbu2 = jnp.concatenate([top, bot], axis=1)
    wc_j = [jnp.concatenate([wc[:, jj * h:(jj + 1) * h, :], wc[:, S5_SL + jj * h:S5_SL + (jj + 1) * h, :]],
                            axis=1) for jj in range(S5_J)]
    wc2 = jnp.concatenate(wc_j, axis=2)
    tile = lambda a: jnp.repeat(a.reshape(S5_NGB, S5_J, h), nb, axis=1)
    return wbu2, wc2, tile(a_re), tile(a_im)


def _mix_out_kernel(xp_ref, xs_ref, ogp_ref, ogs_ref, ysp_ref, yss_ref, glu_w_ref, glu_b_ref, s5n_ref, wo_ref,
                    nffn_ref, wr_ref, h1_ref, hn_ref, rt_ref, rtt_ref, cnt_ref, *, n_prompt_tiles):
    is_p = pl.program_id(0) < n_prompt_tiles

    def rows_logits(rs):
        x = jnp.where(is_p, xp_ref[rs, :], xs_ref[rs, :])
        og = jnp.where(is_p, ogp_ref[rs, :], ogs_ref[rs, :])
        y = jax.nn.gelu(jnp.where(is_p, ysp_ref[rs, :], yss_ref[rs, :]))
        y = y * jax.nn.sigmoid(_dot(y.astype(BF16), glu_w_ref[...]) + glu_b_ref[...])
        yn = _rms(y, s5n_ref[...]).astype(BF16)
        mix = _dot(og, wo_ref[0:D_GLA, :]) + _dot(yn, wo_ref[D_GLA:D_GLA + D_S5, :])
        h1 = x + mix
        h1_ref[rs, :] = h1
        hn = _rms(h1, nffn_ref[...])
        hn_ref[rs, :] = _pack_bf16_pair(hn[:, 0:HALF], hn[:, HALF:D_MODEL])
        hn_hi = hn.astype(BF16)
        hn_lo = (hn - hn_hi.astype(F32)).astype(BF16)
        return _dot(hn_hi, wr_ref[0]) + _dot(hn_hi, wr_ref[1]) + _dot(hn_lo, wr_ref[0])

    half_rows = xp_ref.shape[0] // 2
    logits = jnp.concatenate([rows_logits(slice(h * half_rows, (h + 1) * half_rows)) for h in range(2)], axis=0)
    rt = _route(logits)

    @pl.when(pl.program_id(0) == 0)
    def _():
        cnt_ref[...] = jnp.zeros_like(cnt_ref)

    tm = rt.shape[0]
    lane = lax.broadcasted_iota(jnp.int32, rt.shape, 1).astype(F32)
    oh0 = lane == rt[:, 0:1]
    oh1 = lane == rt[:, 1:2]
    both = jnp.where(oh0 | oh1, 1.0, 0.0)
    ri = lax.broadcasted_iota(jnp.int32, (tm, tm), 0)
    ci = lax.broadcasted_iota(jnp.int32, (tm, tm), 1)
    before = _dot(jnp.where(ri > ci, 1.0, 0.0).astype(BF16), both.astype(BF16)) + cnt_ref[...]
    rank0 = jnp.sum(jnp.where(oh0, before, 0.0), axis=-1, keepdims=True)
    rank1 = jnp.sum(jnp.where(oh1, before, 0.0), axis=-1, keepdims=True)
    cnt_ref[...] += jnp.sum(both, axis=0, keepdims=True)
    rt = jnp.where(lane == 4.0, rank0, jnp.where(lane == 5.0, rank1, rt))
    rt_ref[...] = rt
    rtt_ref[...] = rt.T[0:8, :]


def _route(logits):
    col = lax.broadcasted_iota(jnp.int32, logits.shape, 1)
    colf = col.astype(F32)
    neg = -jnp.inf

    def first_argmax(vals):
        m = jnp.max(vals, axis=-1, keepdims=True)
        idx = jnp.min(jnp.where(vals == m, colf, float(LANES)), axis=-1, keepdims=True)
        return m, idx

    lg = jnp.where(col < N_EGROUPS, logits, neg)
    gmax, gsel = first_argmax(lg)
    p_g = 1.0 / jnp.sum(jnp.exp(lg - gmax), axis=-1, keepdims=True)
    ecol = col - N_EGROUPS
    egrp = (ecol >> 3).astype(F32)
    in_group = (ecol >= 0) & (ecol < N_EXPERTS) & (egrp == gsel)
    le = jnp.where(in_group, logits, neg)
    m1, i1 = first_argmax(le)
    le2 = jnp.where(colf == i1, neg, le)
    m2, i2 = first_argmax(le2)
    e2 = jnp.exp(m2 - m1)
    den = 1.0 + e2
    w1 = p_g * (1.0 / den)
    w2 = p_g * (e2 / den)
    e1f = i1 - float(N_EGROUPS)
    e2f = i2 - float(N_EGROUPS)
    out = jnp.where(col == 0, e1f, jnp.where(col == 1, e2f, jnp.where(col == 2, w1, jnp.where(col == 3, w2, 0.0))))
    return out


def _mix_out(xp, xs, ogp, ogs, ysp, yss, glu_w, glu_b, s5n, wo, nffn, wr):
    tm = TOK_TILE
    npt, nst = xp.shape[0] // tm, xs.shape[0] // tm
    t = (npt + nst) * tm
    row = lambda w: pl.BlockSpec((tm, w), lambda i: (i, 0))
    prow = lambda w: pl.BlockSpec((tm, w), lambda i: (jnp.minimum(i, npt - 1), 0))
    srow = lambda w: pl.BlockSpec((tm, w), lambda i: (jnp.maximum(i - npt, 0), 0))
    return pl.pallas_call(
        functools.partial(_mix_out_kernel, n_prompt_tiles=npt),
        grid=(npt + nst,),
        in_specs=[prow(D_MODEL), srow(D_MODEL), prow(D_GLA), srow(D_GLA), prow(D_S5), srow(D_S5),
                  _const_spec(glu_w.shape), _const_spec((1, D_S5)), _const_spec((1, D_S5)),
                  _const_spec(wo.shape), _const_spec((1, D_MODEL)), _const_spec(wr.shape)],
        out_specs=[row(D_MODEL), row(HALF), row(LANES), pl.BlockSpec((8, tm), lambda i: (0, i)),
                   pl.BlockSpec((1, LANES), lambda i: (0, 0))],
        out_shape=[jax.ShapeDtypeStruct((t, D_MODEL), F32), jax.ShapeDtypeStruct((t, HALF), jnp.uint32),
                   jax.ShapeDtypeStruct((t, LANES), F32), jax.ShapeDtypeStruct((8, t), F32),
                   jax.ShapeDtypeStruct((1, LANES), F32)],
        compiler_params=pltpu.CompilerParams(dimension_semantics=("arbitrary",), vmem_limit_bytes=VMEM_LIMIT),
        name="mix_out",
    )(xp, xs, ogp, ogs, ysp, yss, glu_w, glu_b, s5n, wo, nffn, wr)


DMA_UNROLL = 8


MOE_SB = 6
HALF = D_MODEL // 2


def _pack_bf16_pair(lo, hi):
    def bits(x):
        b = pltpu.bitcast(x, jnp.uint32)
        return (b + jnp.uint32(0x7FFF) + ((b >> 16) & jnp.uint32(1))) >> 16
    return bits(lo) | (bits(hi) << 16)


def _unpack_bf16_pair(u):
    return (pltpu.bitcast(u << 16, F32), pltpu.bitcast(u & jnp.uint32(0xFFFF0000), F32))


def _moe_kernel(se_ref, sm_ref, sn_ref, sbase_ref, xs_hbm, wg_ref, wu_ref, wd_ref, out_hbm,
                xb_ref, yb_ref, wgb_ref, wub_ref, wdb_ref, gsem, ssem):
    b = pl.program_id(0)
    nb = pl.num_programs(0)
    n = sn_ref[b]
    slot = lax.rem(b, 2)

    def issue_gather(blk, sl):
        base = pl.multiple_of(sbase_ref[blk], MOE_BLK)
        for m in range(1, MOE_SB + 1):
            @pl.when((sn_ref[blk] > 0) & (sm_ref[blk] == m))
            def _():
                rows = m * MOE_BLK
                pltpu.make_async_copy(xs_hbm.at[pl.ds(base, rows), :], xb_ref.at[sl, pl.ds(0, rows), :],
                                      gsem.at[sl]).start()

    def wait_gather(blocks, sl):
        rows = pl.ds(0, pl.multiple_of(blocks * MOE_BLK, MOE_BLK))
        pltpu.make_async_copy(xs_hbm.at[rows, :], xb_ref.at[sl, rows, :], gsem.at[sl]).wait()

    m_cur = sm_ref[b]
    m_prev = sm_ref[jnp.maximum(b - 1, 0)]
    has_prev = (b > 0) & (m_prev > 0)

    def out_copy(rows):
        dst = pl.ds(pl.multiple_of(sbase_ref[b], MOE_BLK), rows)
        return pltpu.make_async_copy(yb_ref.at[pl.ds(0, rows), :], out_hbm.at[dst, :], ssem.at[0])

    def wait_prev_out():
        rows = pl.ds(0, pl.multiple_of(m_prev * MOE_BLK, MOE_BLK))
        pltpu.make_async_copy(yb_ref.at[rows, :], out_hbm.at[rows, :], ssem.at[0]).wait()

    def write_out(rows):
        out_copy(rows).start()

        @pl.when(b == nb - 1)
        def _():
            out_copy(rows).wait()

    @pl.when(b == 0)
    def _():
        issue_gather(0, 0)

    @pl.when(b + 1 < nb)
    def _():
        issue_gather(b + 1, 1 - slot)

    def compute(rows):
        x_lo, x_hi = _unpack_bf16_pair(xb_ref[slot, 0:rows, :])
        x_lo, x_hi = x_lo.astype(BF16), x_hi.astype(BF16)
        gate = _dot(x_lo, wgb_ref[0:HALF, :]) + _dot(x_hi, wgb_ref[HALF:D_MODEL, :])
        up = _dot(x_lo, wub_ref[0:HALF, :]) + _dot(x_hi, wub_ref[HALF:D_MODEL, :])
        hid = (gate * jax.nn.sigmoid(gate) * up).astype(BF16)

        @pl.when(has_prev)
        def _():
            wait_prev_out()

        yb_ref[0:rows, :] = _pack_bf16_pair(_dot(hid, wdb_ref[:, 0:HALF]), _dot(hid, wdb_ref[:, HALF:D_MODEL]))
        write_out(rows)

    @pl.when(n > 0)
    def _():
        prev_e = se_ref[jnp.maximum(b - 1, 0)]

        @pl.when((b == 0) | (prev_e != se_ref[b]))
        def _():
            wgb_ref[...] = wg_ref[...].astype(BF16)
            wub_ref[...] = wu_ref[...].astype(BF16)
            wdb_ref[...] = wd_ref[...].astype(BF16)

        wait_gather(m_cur, slot)
        for m in range(1, MOE_SB + 1):
            @pl.when(m_cur == m)
            def _():
                compute(m * MOE_BLK)

    @pl.when((n == 0) & (m_cur > 0))
    def _():
        @pl.when(has_prev)
        def _():
            wait_prev_out()

        yb_ref[...] = jnp.zeros_like(yb_ref)
        for m in range(1, MOE_SB + 1):
            @pl.when(m_cur == m)
            def _():
                write_out(m * MOE_BLK)

    @pl.when((m_cur == 0) & has_prev)
    def _():
        wait_prev_out()


def _moe(x_slots, plan, w_gate, w_up, w_down, n_out_rows):
    sb_e, sb_m, sb_n, sb_base = plan
    sb_rows = MOE_SB * MOE_BLK
    wspec = lambda s: pl.BlockSpec((None,) + s, lambda b, se, *_: (se[b], 0, 0))
    grid_spec = pltpu.PrefetchScalarGridSpec(
        num_scalar_prefetch=4,
        grid=(sb_e.shape[0],),
        in_specs=[pl.BlockSpec(memory_space=pl.ANY),
                  wspec((D_MODEL, D_EXPERT)), wspec((D_MODEL, D_EXPERT)), wspec((D_EXPERT, D_MODEL))],
        out_specs=pl.BlockSpec(memory_space=pl.ANY),
        scratch_shapes=[pltpu.VMEM((2, sb_rows, HALF), jnp.uint32), pltpu.VMEM((sb_rows, HALF), jnp.uint32),
                        pltpu.VMEM((D_MODEL, D_EXPERT), BF16), pltpu.VMEM((D_MODEL, D_EXPERT), BF16),
                        pltpu.VMEM((D_EXPERT, D_MODEL), BF16), pltpu.SemaphoreType.DMA((2,)),
                        pltpu.SemaphoreType.DMA((1,))],
    )
    return pl.pallas_call(
        _moe_kernel,
        grid_spec=grid_spec,
        out_shape=jax.ShapeDtypeStruct((n_out_rows, HALF), jnp.uint32),
        compiler_params=pltpu.CompilerParams(dimension_semantics=("arbitrary",),
                                             vmem_limit_bytes=MOE_VMEM_LIMIT),
        name="moe_experts",
    )(sb_e, sb_m, sb_n, sb_base, x_slots, w_gate, w_up, w_down)


def _dispatch_kernel(dest_ref, zb_ref, hn_ref, out_hbm, buf_ref, zero_ref, sem, zsem):
    i = pl.program_id(0)
    nt = pl.num_programs(0)
    tm = hn_ref.shape[0]
    t_all = nt * tm
    slot = lax.rem(i, 2)
    nblk = zb_ref.shape[0]

    def zero_copy(blk):
        dst = pl.ds(pl.multiple_of(blk * MOE_BLK, MOE_BLK), MOE_BLK)
        return pltpu.make_async_copy(zero_ref, out_hbm.at[dst, :], zsem.at[0])

    @pl.when(i == 0)
    def _():
        zero_ref[...] = jnp.zeros_like(zero_ref)

        def start(blk, c):
            @pl.when(zb_ref[blk] == 1)
            def _():
                zero_copy(blk).start()
            return c

        def wait(blk, c):
            @pl.when(zb_ref[blk] == 1)
            def _():
                zero_copy(blk).wait()
            return c

        lax.fori_loop(0, nblk, start, 0)
        lax.fori_loop(0, nblk, wait, 0)

    def wait_rows(sl):
        for k in range(2):
            pltpu.make_async_copy(buf_ref.at[sl], out_hbm.at[pl.ds(0, tm), :], sem.at[sl]).wait()

    @pl.when(i >= 2)
    def _():
        wait_rows(slot)

    buf_ref[slot] = hn_ref[...]
    base = i * tm
    for a in range(2 * tm):
        row, k = a // 2, a % 2
        d = dest_ref[k * t_all + base + row]
        pltpu.make_async_copy(buf_ref.at[slot, pl.ds(row, 1), :], out_hbm.at[pl.ds(d, 1), :],
                              sem.at[slot]).start()

    @pl.when(i == nt - 1)
    def _():
        wait_rows(slot)

        @pl.when(nt > 1)
        def _():
            wait_rows(1 - slot)


def _dispatch(dest, zero_blk, hn_pk, n_slots):
    tm = TOK_TILE
    grid_spec = pltpu.PrefetchScalarGridSpec(
        num_scalar_prefetch=2,
        grid=(hn_pk.shape[0] // tm,),
        in_specs=[pl.BlockSpec((tm, HALF), lambda i, d, z: (i, 0))],
        out_specs=pl.BlockSpec(memory_space=pl.ANY),
        scratch_shapes=[pltpu.VMEM((2, tm, HALF), jnp.uint32), pltpu.VMEM((MOE_BLK, HALF), jnp.uint32),
                        pltpu.SemaphoreType.DMA((2,)), pltpu.SemaphoreType.DMA((1,))],
    )
    return pl.pallas_call(
        _dispatch_kernel,
        grid_spec=grid_spec,
        out_shape=jax.ShapeDtypeStruct((n_slots, HALF), jnp.uint32),
        compiler_params=pltpu.CompilerParams(dimension_semantics=("arbitrary",), vmem_limit_bytes=VMEM_LIMIT),
        name="moe_dispatch",
    )(dest, zero_blk, hn_pk)


def _moe_plan(rtt, cnt):
    t_all = rtt.shape[1]
    n_assign = 2 * t_all
    e_flat = rtt[0:2].astype(jnp.int32).reshape(-1)
    rank = rtt[4:6].astype(jnp.int32).reshape(-1)
    counts = cnt[0, 0:N_EXPERTS].astype(jnp.int32)
    padded = (counts + MOE_BLK - 1) // MOE_BLK * MOE_BLK
    pad_end = jnp.cumsum(padded)
    pad_start = pad_end - padded
    before = jnp.arange(N_EXPERTS, dtype=jnp.int32)[:, None] < e_flat[None, :]
    dest = rank + jnp.sum(jnp.where(before, padded[:, None], 0), axis=0)
    nblk = -(-(n_assign + N_EXPERTS * (MOE_BLK - 1)) // MOE_BLK)
    n_slots = nblk * MOE_BLK
    blk = jnp.arange(nblk, dtype=jnp.int32)
    last_of_expert = jnp.any((blk[:, None] == (pad_end // MOE_BLK - 1)[None, :]) & (counts > 0)[None, :], axis=1)
    zero_blk = (last_of_expert | (blk >= pad_end[-1] // MOE_BLK)).astype(jnp.int32)
    k_e = padded // MOE_BLK
    sbc = (k_e + MOE_SB - 1) // MOE_SB
    sb_end = jnp.cumsum(sbc)
    sb_start = sb_end - sbc
    n_sb = (nblk + (MOE_SB - 1) * N_EXPERTS) // MOE_SB
    s = jnp.arange(n_sb, dtype=jnp.int32)
    sb_e = jnp.minimum(jnp.sum((s[:, None] >= sb_end[None, :]).astype(jnp.int32), axis=1), N_EXPERTS - 1)
    j = s - sb_start[sb_e]
    real = s < sb_end[-1]
    tail_blk = pad_end[-1] // MOE_BLK + MOE_SB * (s - sb_end[-1])
    sb_m = jnp.where(real, jnp.clip(k_e[sb_e] - MOE_SB * j, 0, MOE_SB), jnp.clip(nblk - tail_blk, 0, MOE_SB))
    sb_n = jnp.where(real, jnp.clip(counts[sb_e] - MOE_SB * MOE_BLK * j, 0, MOE_SB * MOE_BLK), 0)
    sb_base = jnp.where(real, pad_start[sb_e] + MOE_SB * MOE_BLK * j, jnp.minimum(tail_blk, nblk - 1) * MOE_BLK)
    i32 = lambda a: a.astype(jnp.int32)
    return (i32(sb_e), i32(sb_m), i32(sb_n), i32(sb_base)), i32(dest), zero_blk, n_slots


def _ple_out_kernel(dest_ref, h1_ref, rt_ref, pp_ref, ps_ref, nple_ref, wpg_ref, wp_ref, nfin_ref, ys_hbm,
                    op_ref, os_ref, yb_ref, sem, *, n_prompt_tiles):
    i = pl.program_id(0)
    nt = pl.num_programs(0)
    tm = h1_ref.shape[0]
    t_all = nt * tm
    slot = lax.rem(i, 2)

    def issue(tile, sl):
        base = tile * tm

        def grp(g, c):
            for j in range(DMA_UNROLL):
                row = g * (DMA_UNROLL // 2) + j // 2
                d = dest_ref[(j % 2) * t_all + base + row]
                pltpu.make_async_copy(ys_hbm.at[pl.ds(d, 1), :], yb_ref.at[sl, j % 2, pl.ds(row, 1), :],
                                      sem.at[sl]).start()
            return c

        lax.fori_loop(0, 2 * tm // DMA_UNROLL, grp, 0)

    @pl.when(i == 0)
    def _():
        issue(0, 0)

    def wait_rows(sl):
        for k in range(2):
            pltpu.make_async_copy(ys_hbm.at[pl.ds(0, tm), :], yb_ref.at[sl, k], sem.at[sl]).wait()

    wait_rows(slot)
    is_p = i < n_prompt_tiles
    rt = rt_ref[...]
    w0, w1 = rt[:, 2:3], rt[:, 3:4]
    lo0, hi0 = _unpack_bf16_pair(yb_ref[slot, 0])
    lo1, hi1 = _unpack_bf16_pair(yb_ref[slot, 1])
    h2 = h1_ref[...] + jnp.concatenate([w0 * lo0 + w1 * lo1, w0 * hi0 + w1 * hi1], axis=1)
    hb = _rms(h2, nple_ref[...]).astype(BF16)
    nbase = jnp.minimum(i + 1, nt - 1) * tm
    n_chunk = 8
    cw, ca = D_MODEL // n_chunk, 2 * tm // n_chunk
    gates = []
    for c in range(n_chunk):
        gates.append(jax.nn.sigmoid(_dot(hb, wpg_ref[:, c * cw:(c + 1) * cw])))
        for a in range(c * ca, (c + 1) * ca):
            d = dest_ref[(a % 2) * t_all + nbase + a // 2]
            pltpu.make_async_copy(ys_hbm.at[pl.ds(d, 1), :], yb_ref.at[1 - slot, a % 2, pl.ds(a // 2, 1), :],
                                  sem.at[1 - slot]).start()
    gate = jnp.concatenate(gates, axis=1)

    @pl.when(i == nt - 1)
    def _():
        wait_rows(1 - slot)

    p = jnp.where(is_p, pp_ref[...], ps_ref[...])
    h3 = h2 + _dot(p.astype(BF16), wp_ref[...]) * gate
    y = _rms(h3, nfin_ref[...])

    @pl.when(is_p)
    def _():
        op_ref[...] = y

    @pl.when(jnp.logical_not(is_p))
    def _():
        os_ref[...] = y


def _ple_out(dest, h1, rt, pp, ps, nple, wpg, wp, nfin, y_slots):
    tm = TOK_TILE
    npt, nst = pp.shape[0] // tm, ps.shape[0] // tm
    row = lambda w: pl.BlockSpec((tm, w), lambda i, d: (i, 0))
    prow = lambda w: pl.BlockSpec((tm, w), lambda i, d: (jnp.minimum(i, npt - 1), 0))
    srow = lambda w: pl.BlockSpec((tm, w), lambda i, d: (jnp.maximum(i - npt, 0), 0))
    const = lambda shape: pl.BlockSpec(shape, lambda i, d: (0,) * len(shape), pipeline_mode=pl.Buffered(1))
    grid_spec = pltpu.PrefetchScalarGridSpec(
        num_scalar_prefetch=1,
        grid=(npt + nst,),
        in_specs=[row(D_MODEL), row(LANES), prow(D_PLE), srow(D_PLE), const((1, D_MODEL)), const(wpg.shape),
                  const(wp.shape), const((1, D_MODEL)), pl.BlockSpec(memory_space=pl.ANY)],
        out_specs=[prow(D_MODEL), srow(D_MODEL)],
        scratch_shapes=[pltpu.VMEM((2, 2, tm, HALF), jnp.uint32), pltpu.SemaphoreType.DMA((2,))],
    )
    return pl.pallas_call(
        functools.partial(_ple_out_kernel, n_prompt_tiles=npt),
        grid_spec=grid_spec,
        out_shape=[jax.ShapeDtypeStruct((pp.shape[0], D_MODEL), F32),
                   jax.ShapeDtypeStruct((ps.shape[0], D_MODEL), F32)],
        compiler_params=pltpu.CompilerParams(dimension_semantics=("arbitrary",), vmem_limit_bytes=VMEM_LIMIT),
        name="ple_out",
    )(dest, h1, rt, pp, ps, nple, wpg, wp, nfin, y_slots)


def _s5_params(lam_re, lam_im, log_dt, b_re, b_im, c_re, c_im, d_skip):
    dt = jnp.exp(log_dt)[:, None]
    mag = jnp.exp(lam_re * dt)
    ab_re = mag * jnp.cos(lam_im * dt)
    ab_im = mag * jnp.sin(lam_im * dt)
    den = lam_re * lam_re + lam_im * lam_im
    nr = ab_re - 1.0
    f_re = (nr * lam_re + ab_im * lam_im) / den
    f_im = (ab_im * lam_re - nr * lam_im) / den
    bb_re = f_re[..., None] * b_re - f_im[..., None] * b_im
    bb_im = f_re[..., None] * b_im + f_im[..., None] * b_re
    eye = jnp.eye(S5_GB, dtype=F32)

    def bu_w(bb):
        bb = bb.reshape(S5_NGB, S5_GB, S5_STATE, S5_GROUP)
        w = jnp.einsum('nlph,lm->nlhmp', bb, eye)
        return w.reshape(S5_NGB, S5_GB * S5_GROUP, S5_SL)

    def c_w(c):
        c = c.reshape(S5_NGB, S5_GB, S5_GROUP, S5_STATE)
        w = jnp.einsum('nlhp,lm->nlpmh', c, eye)
        return w.reshape(S5_NGB, S5_SL, S5_GB * S5_GROUP)

    wbu = jnp.concatenate([bu_w(bb_re), bu_w(bb_im)], axis=2).astype(BF16)
    wc = jnp.concatenate([c_w(c_re), -c_w(c_im)], axis=1).astype(BF16)
    a_re = ab_re.reshape(S5_NGB, 1, S5_SL)
    a_im = ab_im.reshape(S5_NGB, 1, S5_SL)
    dsk = d_skip.reshape(S5_NGB, 1, S5_GB * S5_GROUP)
    return wbu, wc, a_re, a_im, dsk


def kernel(x_prompt, x_sample, p_prompt, p_sample, state_gla, state_s5_re, state_s5_im, norm_mix, w_in, gla_w_gate_up, gla_gate_bias, gla_norm, s5_lam_re, s5_lam_im, s5_log_dt, s5_b_re, s5_b_im, s5_c_re, s5_c_im, s5_d, s5_glu_w, s5_glu_b, s5_norm, w_out, norm_ffn, router_group, router_expert, w_gate, w_up, w_down, norm_ple, w_ple, w_ple_gate, norm_final):
    depth = w_in.shape[0]
    assert depth == 1
    i = 0
    bp, lp, _ = x_prompt.shape
    bs, ls, _ = x_sample.shape
    tp, ts = bp * lp, bs * ls
    t_all = tp + ts

    wm = _w_in_prep(w_in[i])
    wgu = jnp.pad(gla_w_gate_up[i], ((0, LANES - GLA_RANK), (0, 0))).astype(BF16)
    gbias = gla_gate_bias[i].reshape(1, QK_W)
    wbu, wc, a_re, a_im, dsk = _s5_params(s5_lam_re[i], s5_lam_im[i], s5_log_dt[i], s5_b_re[i], s5_b_im[i],
                                          s5_c_re[i], s5_c_im[i], s5_d[i])
    glu_w = s5_glu_w[i].astype(BF16)
    wo = w_out[i].astype(BF16)
    wr32 = jnp.pad(jnp.concatenate([router_group[i], router_expert[i]], axis=1),
                   ((0, 0), (0, LANES - N_EGROUPS - N_EXPERTS)))
    wr_hi = wr32.astype(BF16)
    wr = jnp.stack([wr_hi, (wr32 - wr_hi.astype(F32)).astype(BF16)])
    wpg = w_ple_gate[i].astype(BF16)
    wp = w_ple[i].astype(BF16)
    vec = lambda a: a.reshape(1, -1)

    xp = x_prompt.reshape(tp, D_MODEL)
    xs = x_sample.reshape(ts, D_MODEL)

    qp, kp, vp, rp, lap, up = _in_proj(xp, vec(norm_mix[i]), wm, wgu, gbias, BF16)
    qs, ks, vs, rs, las, us = _in_proj(xs, vec(norm_mix[i]), wm, wgu, gbias, BF16)
    ogp, gla_p = _gla_prompt(qp, kp, vp, rp, lap, vec(gla_norm[i]), bp, lp)
    ogs, gla_s = _gla_sample(qs, ks, vs, rs, las, vec(gla_norm[i]), state_gla[i], bs, ls)
    wbu2, wc2, a_re2, a_im2 = _s5_prompt_params(wbu, wc, a_re, a_im, bp)
    ysp, re_p, im_p = _s5_prompt(up.reshape(bp, lp, D_S5), wbu2, wc2, a_re2, a_im2, dsk, tc=256)
    yss, re_s, im_s = _s5(us.reshape(bs, ls, D_S5), wbu, wc, a_re, a_im, dsk,
                          state_s5_re[i].reshape(bs, -1), state_s5_im[i].reshape(bs, -1),
                          nb=32, tc=ls, has_state=True)

    h1, hn_all, rt_all, rtt, cnt = _mix_out(xp, xs, ogp, ogs, ysp.reshape(tp, D_S5), yss.reshape(ts, D_S5),
                                            glu_w, vec(s5_glu_b[i]), vec(s5_norm[i]), wo, vec(norm_ffn[i]), wr)

    plan, dest, zero_blk, n_slots = _moe_plan(rtt, cnt)
    x_slots = _dispatch(dest, zero_blk, hn_all, n_slots)
    y_slots = _moe(x_slots, plan, w_gate[i], w_up[i], w_down[i], n_slots)

    y_p, y_s = _ple_out(dest, h1, rt_all, p_prompt[i].reshape(tp, D_PLE), p_sample[i].reshape(ts, D_PLE),
                        vec(norm_ple[i]), wpg, wp, vec(norm_final), y_slots)

    s5shape = lambda a, b: a.reshape(1, b, S5_GROUPS, S5_STATE)
    return (y_p.reshape(bp, lp, D_MODEL), y_s.reshape(bs, ls, D_MODEL),
            gla_p[None], s5shape(re_p, bp), s5shape(im_p, bp),
            gla_s[None], s5shape(re_s, bs), s5shape(im_s, bs))
```

```python
import functools
import math

import jax
import jax.numpy as jnp
from jax import lax
from jax.experimental import pallas as pl
from jax.experimental.pallas import tpu as pltpu

F32 = jnp.float32
BF16 = jnp.bfloat16

D_MODEL = 2048
D_GLA = 1024
D_S5 = 1024
GLA_HEADS = 4
GLA_DV = 256
GLA_DK = 128
GLA_RANK = 16
GLA_CHUNK = 64
S5_GROUP = 16
S5_GROUPS = 64
S5_STATE = 64
N_EGROUPS = 4
N_EPG = 8
N_EXPERTS = 32
D_EXPERT = 512
D_PLE = 256
EPS = 1e-6

LANES = 128
QK_W = GLA_HEADS * GLA_DK
S5_GB = 8
S5_NGB = S5_GROUPS // S5_GB
S5_SL = S5_GB * S5_STATE
TOK_TILE = 256
MOE_BLK = 128
VMEM_LIMIT = 56 * 1024 * 1024
MOE_VMEM_LIMIT = 60 * 1024 * 1024


def _const_spec(shape):
    nd = len(shape)
    return pl.BlockSpec(shape, lambda *_: (0,) * nd, pipeline_mode=pl.Buffered(1))


def _rms(x, g):
    return x * lax.rsqrt(jnp.mean(x * x, axis=-1, keepdims=True) + EPS) * g


def _dot(a, b):
    return jnp.dot(a, b, preferred_element_type=F32)


def _log_sigmoid(x):
    return -(jnp.maximum(-x, 0.0) + jnp.log1p(jnp.exp(-jnp.abs(x))))


N_QKVR = 2 * QK_W + 2 * D_GLA
W_IN_COLS = N_QKVR + GLA_RANK + D_S5


def _w_in_prep_kernel(w_ref, o_ref):
    o_ref[:, 0:N_QKVR] = w_ref[:, 0:N_QKVR].astype(BF16)
    tail = w_ref[:, N_QKVR:W_IN_COLS]
    o_ref[:, N_QKVR:N_QKVR + D_S5] = tail[:, GLA_RANK:GLA_RANK + D_S5].astype(BF16)
    o_ref[:, N_QKVR + D_S5:N_QKVR + D_S5 + LANES] = tail[:, 0:LANES].astype(BF16)


def _w_in_prep(wi):
    rows = 256
    return pl.pallas_call(
        _w_in_prep_kernel,
        grid=(D_MODEL // rows,),
        in_specs=[pl.BlockSpec((rows, W_IN_COLS), lambda i: (i, 0))],
        out_specs=pl.BlockSpec((rows, N_QKVR + D_S5 + LANES), lambda i: (i, 0)),
        out_shape=jax.ShapeDtypeStruct((D_MODEL, N_QKVR + D_S5 + LANES), BF16),
        compiler_params=pltpu.CompilerParams(dimension_semantics=("arbitrary",), vmem_limit_bytes=VMEM_LIMIT),
        name="w_in_prep",
    )(wi)


def _in_proj_kernel(x_ref, g_ref, wm_ref, wgu_ref, gb_ref,
                    q_ref, k_ref, v_ref, r_ref, la_ref, u_ref):
    hb = _rms(x_ref[...], g_ref[...]).astype(BF16)

    def seg(a, b):
        return _dot(hb, wm_ref[:, a:b])

    q_ref[...] = (seg(0, QK_W) * (GLA_DK ** -0.5)).astype(q_ref.dtype)
    k_ref[...] = seg(QK_W, 2 * QK_W).astype(k_ref.dtype)
    v_ref[...] = seg(2 * QK_W, 2 * QK_W + D_GLA).astype(v_ref.dtype)
    r_ref[...] = seg(2 * QK_W + D_GLA, N_QKVR).astype(r_ref.dtype)
    u_ref[...] = seg(N_QKVR, N_QKVR + D_S5)
    zg = seg(N_QKVR + D_S5, N_QKVR + D_S5 + LANES)
    xg = _dot(zg.astype(BF16), wgu_ref[...]) + gb_ref[...]
    la_ref[...] = _log_sigmoid(xg) * (1.0 / 16.0)


def _in_proj(x2d, g, wm, wgu, gbias, act_dtype):
    t = x2d.shape[0]
    tm = TOK_TILE
    row = lambda w: pl.BlockSpec((tm, w), lambda i: (i, 0))
    return pl.pallas_call(
        _in_proj_kernel,
        grid=(t // tm,),
        in_specs=[row(D_MODEL), _const_spec((1, D_MODEL)), _const_spec(wm.shape),
                  _const_spec(wgu.shape), _const_spec((1, QK_W))],
        out_specs=[row(QK_W), row(QK_W), row(D_GLA), row(D_GLA), row(QK_W), row(D_S5)],
        out_shape=[jax.ShapeDtypeStruct((t, QK_W), act_dtype), jax.ShapeDtypeStruct((t, QK_W), act_dtype),
                   jax.ShapeDtypeStruct((t, D_GLA), act_dtype), jax.ShapeDtypeStruct((t, D_GLA), act_dtype),
                   jax.ShapeDtypeStruct((t, QK_W), F32), jax.ShapeDtypeStruct((t, D_S5), F32)],
        compiler_params=pltpu.CompilerParams(dimension_semantics=("arbitrary",), vmem_limit_bytes=VMEM_LIMIT),
        name="in_proj",
    )(x2d, g, wm, wgu, gbias)


_NT = (((1,), (1,)), ((), ()))
_TN = (((0,), (0,)), ((), ()))


def _gla_pre(q, k, la, c):
    r = q.shape[0]
    shift = int(math.log2(c))
    ri = lax.broadcasted_iota(jnp.int32, (r, r), 0)
    si = lax.broadcasted_iota(jnp.int32, (r, r), 1)
    mask = ((ri >> shift) == (si >> shift)) & (ri >= si)
    tri = jnp.where(mask, 1.0, 0.0).astype(BF16)
    hi = la.astype(BF16)
    r1 = la - hi.astype(F32)
    mid = r1.astype(BF16)
    lo = (r1 - mid.astype(F32)).astype(BF16)
    cum = _dot(tri, hi) + _dot(tri, mid) + _dot(tri, lo)
    last = jnp.concatenate([jnp.broadcast_to(cum[(i + 1) * c - 1:(i + 1) * c, :], (c, cum.shape[1]))
                            for i in range(r // c)], axis=0)
    qe = (q * jnp.exp(cum)).astype(BF16)
    ke = (k * jnp.exp(-cum)).astype(BF16)
    kd = (k * jnp.exp(last - cum)).astype(BF16)
    return qe, ke, kd, cum, mask


def _gla_intra(qe, ke, v, mask):
    sc = lax.dot_general(qe, ke, _NT, preferred_element_type=F32)
    return _dot(jnp.where(mask, sc, 0.0).astype(BF16), v)


def _gla_finish(o, r, g):
    rf = r.astype(F32)
    return _rms(o, g) * (rf * jax.nn.sigmoid(rf))


GLA_SAFE_LOG_DECAY = -60.0


def _gla_token_step(t, rows, q, k, v, a_all, st_ref, oacc_ref):
    m = rows == t
    a = jnp.sum(jnp.where(m, a_all, 0.0), axis=0, keepdims=True)
    kt = jnp.where(m, k, 0.0).astype(BF16)
    qt = jnp.where(m, q, 0.0).astype(BF16)
    vt = jnp.where(m, v, jnp.zeros_like(v))
    st = st_ref[...] * a + lax.dot_general(vt, kt, _TN, preferred_element_type=F32)
    st_ref[...] = st
    oacc_ref[...] += lax.dot_general(qt, st.astype(BF16), _NT, preferred_element_type=F32)


def _gla_prompt_kernel(q_ref, k_ref, v_ref, r_ref, la_ref, g_ref, o_ref, sfin_ref, st_ref, oacc_ref,
                       *, n_chunks):
    j = pl.program_id(1)

    @pl.when(j == 0)
    def _():
        st_ref[...] = jnp.zeros_like(st_ref)

    c = GLA_CHUNK
    qe, ke, kd, cum, mask = _gla_pre(q_ref[...].astype(F32), k_ref[...].astype(F32), la_ref[...], c)
    g = g_ref[...]
    safe = jnp.min(cum) >= GLA_SAFE_LOG_DECAY

    @pl.when(safe)
    def _():
        for h in range(GLA_HEADS):
            kc = slice(h * GLA_DK, (h + 1) * GLA_DK)
            vc = slice(h * GLA_DV, (h + 1) * GLA_DV)
            v = v_ref[:, vc]
            qe_h, kd_h = qe[:, kc], kd[:, kc]
            o = _gla_intra(qe_h, ke[:, kc], v, mask)
            st = st_ref[h]
            inter = []
            for ci in range(n_chunks):
                rows = slice(ci * c, (ci + 1) * c)
                inter.append(lax.dot_general(qe_h[rows], st.astype(BF16), _NT, preferred_element_type=F32))
                dec = jnp.exp(cum[(ci + 1) * c - 1:(ci + 1) * c, kc])
                st = st * dec + lax.dot_general(v[rows], kd_h[rows], _TN, preferred_element_type=F32)
            st_ref[h] = st
            o = o + jnp.concatenate(inter, axis=0)
            o_ref[:, vc] = _gla_finish(o, r_ref[:, vc], g).astype(o_ref.dtype)

    @pl.when(jnp.logical_not(safe))
    def _():
        n_rows = n_chunks * c
        rows = lax.broadcasted_iota(jnp.int32, (n_rows, 1), 0)
        for h in range(GLA_HEADS):
            kc = slice(h * GLA_DK, (h + 1) * GLA_DK)
            vc = slice(h * GLA_DV, (h + 1) * GLA_DV)
            q, k, v = q_ref[:, kc].astype(F32), k_ref[:, kc].astype(F32), v_ref[:, vc]
            a_all = jnp.exp(la_ref[:, kc])
            oacc_ref[...] = jnp.zeros_like(oacc_ref)

            def step(t, carry):
                _gla_token_step(t, rows, q, k, v, a_all, st_ref.at[h], oacc_ref)
                return carry

            lax.fori_loop(0, n_rows, step, 0)
            o_ref[:, vc] = _gla_finish(oacc_ref[...], r_ref[:, vc], g).astype(o_ref.dtype)

    @pl.when(j == pl.num_programs(1) - 1)
    def _():
        for h in range(GLA_HEADS):
            sfin_ref[0, h] = st_ref[h].T


def _gla_prompt(q, k, v, r, la, g, batch, seq):
    rb = 4 * GLA_CHUNK
    nj = seq // rb
    row = lambda w: pl.BlockSpec((rb, w), lambda b, j: (b * nj + j, 0))
    return pl.pallas_call(
        functools.partial(_gla_prompt_kernel, n_chunks=rb // GLA_CHUNK),
        grid=(batch, nj),
        in_specs=[row(QK_W), row(QK_W), row(D_GLA), row(D_GLA), row(QK_W), _const_spec((1, GLA_DV))],
        out_specs=[row(D_GLA),
                   pl.BlockSpec((1, GLA_HEADS, GLA_DK, GLA_DV), lambda b, j: (b, 0, 0, 0))],
        out_shape=[jax.ShapeDtypeStruct((batch * seq, D_GLA), BF16),
                   jax.ShapeDtypeStruct((batch, GLA_HEADS, GLA_DK, GLA_DV), F32)],
        scratch_shapes=[pltpu.VMEM((GLA_HEADS, GLA_DV, GLA_DK), F32), pltpu.VMEM((rb, GLA_DV), F32)],
        compiler_params=pltpu.CompilerParams(dimension_semantics=("arbitrary", "arbitrary"),
                                             vmem_limit_bytes=VMEM_LIMIT),
        name="gla_prompt",
    )(q, k, v, r, la, g)


def _gla_sample_kernel(q_ref, k_ref, v_ref, r_ref, la_ref, g_ref, s0_ref, o_ref, sfin_ref, st_ref, oacc_ref,
                       *, n_seq, seq):
    qe, ke, kd, cum, mask = _gla_pre(q_ref[...].astype(F32), k_ref[...].astype(F32), la_ref[...], seq)
    g = g_ref[...]
    safe = jnp.min(cum) >= GLA_SAFE_LOG_DECAY

    @pl.when(safe)
    def _():
        lasts = jnp.concatenate([cum[(s + 1) * seq - 1:(s + 1) * seq, :] for s in range(n_seq)]
                                + [jnp.zeros((GLA_DK - n_seq, cum.shape[1]), F32)], axis=0)
        pair = 2 * seq
        upper = lax.broadcasted_iota(jnp.int32, (pair, GLA_DK), 0) < seq
        for h in range(GLA_HEADS):
            kc = slice(h * GLA_DK, (h + 1) * GLA_DK)
            vc = slice(h * GLA_DV, (h + 1) * GLA_DV)
            v = v_ref[:, vc]
            qe_h, kd_h = qe[:, kc], kd[:, kc]
            o = _gla_intra(qe_h, ke[:, kc], v, mask)
            dec_t = jnp.exp(lasts[:, kc].T)
            inter = []
            for p in range(n_seq // 2):
                rows = slice(p * pair, (p + 1) * pair)
                qe_p, kd_p, v_p = qe_h[rows], kd_h[rows], v[rows]
                for half in range(2):
                    s = 2 * p + half
                    s0 = s0_ref[s, h]
                    o_s = _dot(qe_p, s0.astype(BF16))
                    inter.append(o_s[half * seq:(half + 1) * seq])
                    kd_s = jnp.where(upper if half == 0 else ~upper, kd_p, jnp.zeros_like(kd_p))
                    dec = jnp.broadcast_to(dec_t[:, s:s + 1], (GLA_DK, GLA_DV))
                    sfin_ref[s, h] = s0 * dec + lax.dot_general(kd_s, v_p, _TN, preferred_element_type=F32)
            o = o + jnp.concatenate(inter, axis=0)
            o_ref[:, vc] = _gla_finish(o, r_ref[:, vc], g).astype(o_ref.dtype)

    @pl.when(jnp.logical_not(safe))
    def _():
        n_rows = n_seq * seq
        rows = lax.broadcasted_iota(jnp.int32, (n_rows, 1), 0)
        for h in range(GLA_HEADS):
            kc = slice(h * GLA_DK, (h + 1) * GLA_DK)
            vc = slice(h * GLA_DV, (h + 1) * GLA_DV)
            q, k, v = q_ref[:, kc].astype(F32), k_ref[:, kc].astype(F32), v_ref[:, vc]
            a_all = jnp.exp(la_ref[:, kc])
            oacc_ref[...] = jnp.zeros_like(oacc_ref)

            def step(t, carry):
                s = t // seq

                @pl.when(t % seq == 0)
                def _():
                    st_ref[...] = s0_ref[s, h].T

                _gla_token_step(t, rows, q, k, v, a_all, st_ref, oacc_ref)

                @pl.when(t % seq == seq - 1)
                def _():
                    sfin_ref[s, h] = st_ref[...].T

                return carry

            lax.fori_loop(0, n_rows, step, 0)
            o_ref[:, vc] = _gla_finish(oacc_ref[...], r_ref[:, vc], g).astype(o_ref.dtype)


def _gla_sample(q, k, v, r, la, g, s0, batch, seq):
    ns = 16
    rb = ns * seq
    row = lambda w: pl.BlockSpec((rb, w), lambda i: (i, 0))
    st = pl.BlockSpec((ns, GLA_HEADS, GLA_DK, GLA_DV), lambda i: (i, 0, 0, 0))
    return pl.pallas_call(
        functools.partial(_gla_sample_kernel, n_seq=ns, seq=seq),
        grid=(batch // ns,),
        in_specs=[row(QK_W), row(QK_W), row(D_GLA), row(D_GLA), row(QK_W), _const_spec((1, GLA_DV)), st],
        out_specs=[row(D_GLA), st],
        out_shape=[jax.ShapeDtypeStruct((batch * seq, D_GLA), BF16),
                   jax.ShapeDtypeStruct((batch, GLA_HEADS, GLA_DK, GLA_DV), F32)],
        scratch_shapes=[pltpu.VMEM((GLA_DV, GLA_DK), F32), pltpu.VMEM((rb, GLA_DV), F32)],
        compiler_params=pltpu.CompilerParams(dimension_semantics=("arbitrary",), vmem_limit_bytes=VMEM_LIMIT),
        name="gla_sample",
    )(q, k, v, r, la, g, s0)


def _s5_kernel(u_ref, wbu_ref, wc_ref, are_ref, aim_ref, d_ref, h0r_ref, h0i_ref,
               y_ref, sre_ref, sim_ref, bu_ref, xs_ref, car_ref, *, nb, tc, has_state):
    j = pl.program_id(2)

    @pl.when(j == 0)
    def _():
        if has_state:
            car_ref[0] = h0r_ref[...]
            car_ref[1] = h0i_ref[...]
        else:
            car_ref[...] = jnp.zeros_like(car_ref)

    u2 = u_ref[...].reshape(nb * tc, LANES)
    ub = u2.astype(BF16)
    nl = S5_SL // LANES
    for l in range(2 * nl):
        bu_ref[l] = _dot(ub, wbu_ref[0, :, l * LANES:(l + 1) * LANES])
    a_r = [jnp.broadcast_to(are_ref[0, :, l * LANES:(l + 1) * LANES], (nb, LANES)) for l in range(nl)]
    a_i = [jnp.broadcast_to(aim_ref[0, :, l * LANES:(l + 1) * LANES], (nb, LANES)) for l in range(nl)]

    def step(t, carry):
        rows = pl.ds(t, nb, stride=tc)
        out = []
        for l in range(nl):
            xr, xi = carry[2 * l], carry[2 * l + 1]
            nr = a_r[l] * xr - a_i[l] * xi + bu_ref[l, rows, :]
            ni = a_r[l] * xi + a_i[l] * xr + bu_ref[nl + l, rows, :]
            xs_ref[l, rows, :] = nr
            xs_ref[nl + l, rows, :] = ni
            out += [nr, ni]
        return tuple(out)

    init = []
    for l in range(nl):
        init += [car_ref[0, :, l * LANES:(l + 1) * LANES], car_ref[1, :, l * LANES:(l + 1) * LANES]]
    fin = lax.fori_loop(0, tc, step, tuple(init), unroll=8)
    xr = jnp.concatenate([fin[2 * l] for l in range(nl)], axis=1)
    xi = jnp.concatenate([fin[2 * l + 1] for l in range(nl)], axis=1)
    car_ref[0] = xr
    car_ref[1] = xi
    y = d_ref[0] * u2
    for l in range(2 * nl):
        y = y + _dot(xs_ref[l].astype(BF16), wc_ref[0, l * LANES:(l + 1) * LANES, :])
    y_ref[...] = y.reshape(nb, tc, LANES)

    @pl.when(j == pl.num_programs(2) - 1)
    def _():
        sre_ref[...] = xr
        sim_ref[...] = xi


def _s5(u3d, wbu, wc, a_re, a_im, dsk, h0r, h0i, nb, tc, has_state):
    batch, seq, _ = u3d.shape
    grid = (S5_NGB, batch // nb, seq // tc)
    st = pl.BlockSpec((nb, S5_SL), lambda g, b, j: (b, g))
    par = lambda w: pl.BlockSpec((1, 1, w), lambda g, b, j: (g, 0, 0))
    ublk = pl.BlockSpec((nb, tc, LANES), lambda g, b, j: (b, j, g))
    return pl.pallas_call(
        functools.partial(_s5_kernel, nb=nb, tc=tc, has_state=has_state),
        grid=grid,
        in_specs=[ublk,
                  pl.BlockSpec((1, LANES, 2 * S5_SL), lambda g, b, j: (g, 0, 0)),
                  pl.BlockSpec((1, 2 * S5_SL, LANES), lambda g, b, j: (g, 0, 0)),
                  par(S5_SL), par(S5_SL), par(LANES), st, st],
        out_specs=[ublk, st, st],
        out_shape=[jax.ShapeDtypeStruct(u3d.shape, F32),
                   jax.ShapeDtypeStruct((batch, S5_GROUPS * S5_STATE), F32),
                   jax.ShapeDtypeStruct((batch, S5_GROUPS * S5_STATE), F32)],
        scratch_shapes=[pltpu.VMEM((2 * S5_SL // LANES, nb * tc, LANES), F32),
                        pltpu.VMEM((2 * S5_SL // LANES, nb * tc, LANES), F32),
                        pltpu.VMEM((2, nb, S5_SL), F32)],
        compiler_params=pltpu.CompilerParams(dimension_semantics=("arbitrary",) * 3,
                                             vmem_limit_bytes=VMEM_LIMIT),
        name="s5_state" if has_state else "s5_zero",
    )(u3d, wbu, wc, a_re, a_im, dsk, h0r, h0i)


S5_J = 2
S5_HL = S5_SL // S5_J
S5_PAIR = 2


def _s5_prompt_kernel(u_ref, wbu_ref, wc_ref, are_ref, aim_ref, d_ref, y_ref, sre_ref, sim_ref,
                      u2_ref, lhs_ref, bu_ref, xs_ref, y2_ref, yo_ref, car_ref, *, nb, tc):
    g = pl.program_id(0)
    j = pl.program_id(1)
    rows = nb * S5_J
    npair = S5_PAIR

    @pl.when((g == 0) & (j == 0))
    def _():
        lhs_ref[...] = jnp.zeros_like(lhs_ref)

    @pl.when(j == 0)
    def _():
        car_ref[...] = jnp.zeros_like(car_ref)

    u2 = u_ref[...].reshape(nb * tc, npair * LANES)
    for p in range(npair):
        u2_ref[p] = u2[:, p * LANES:(p + 1) * LANES]

    def build(t, c):
        for p in range(npair):
            u4 = u2_ref[p, pl.ds(t, nb, stride=tc), :]
            for jj in range(S5_J):
                lhs_ref[p, t, jj * nb:(jj + 1) * nb, jj * LANES:(jj + 1) * LANES] = u4
        return c

    lax.fori_loop(0, tc, build, 0, unroll=True)
    for p in range(npair):
        lhs = lhs_ref[p].reshape(tc * rows, S5_J * LANES).astype(BF16)
        bu_ref[p] = _dot(lhs, wbu_ref[p]).reshape(tc, rows, 2 * S5_HL)
    ar = [are_ref[p] for p in range(npair)]
    ai = [aim_ref[p] for p in range(npair)]

    def step(t, carry):
        out = []
        for p in range(npair):
            xr, xi = carry[2 * p], carry[2 * p + 1]
            tile = bu_ref[p, t]
            nr = ar[p] * xr - ai[p] * xi + tile[:, 0:S5_HL]
            ni = ar[p] * xi + ai[p] * xr + tile[:, S5_HL:2 * S5_HL]
            xs_ref[p, t] = jnp.concatenate([nr, ni], axis=1)
            out += [nr, ni]
        return tuple(out)

    init = tuple(car_ref[p, c] for p in range(npair) for c in range(2))
    fin = lax.fori_loop(0, tc, step, init, unroll=True)
    for p in range(npair):
        car_ref[p, 0] = fin[2 * p]
        car_ref[p, 1] = fin[2 * p + 1]
        xs = xs_ref[p].reshape(tc * rows, 2 * S5_HL).astype(BF16)
        y2_ref[p] = _dot(xs, wc_ref[p]).reshape(tc, rows, S5_J * LANES)
    first_half = lax.broadcasted_iota(jnp.int32, (rows, LANES), 0) < nb

    def unperm(t, c):
        for p in range(npair):
            t2 = y2_ref[p, t]
            part = jnp.where(first_half, t2[:, 0:LANES], t2[:, LANES:2 * LANES])
            yo_ref[p, pl.ds(t, nb, stride=tc), :] = (part + pltpu.roll(part, nb, axis=0))[0:nb]
        return c

    lax.fori_loop(0, tc, unperm, 0, unroll=True)
    y = jnp.concatenate([yo_ref[p] + d_ref[p] * u2_ref[p] for p in range(npair)], axis=1)
    y_ref[...] = y.reshape(nb, tc, npair * LANES)

    @pl.when(j == pl.num_programs(1) - 1)
    def _():
        def gather_state(x):
            return jnp.concatenate([x[jj * nb:(jj + 1) * nb] for jj in range(S5_J)], axis=1)

        sre_ref[...] = jnp.concatenate([gather_state(fin[2 * p]) for p in range(npair)], axis=1)
        sim_ref[...] = jnp.concatenate([gather_state(fin[2 * p + 1]) for p in range(npair)], axis=1)


def _s5_prompt(u3d, wbu2, wc2, a_re2, a_im2, dsk, tc):
    nb, seq, _ = u3d.shape
    rows = nb * S5_J
    assert rows == 8
    np_ = S5_PAIR
    st = pl.BlockSpec((nb, np_ * S5_SL), lambda g, j: (0, g))
    gblk = lambda s: pl.BlockSpec((np_,) + s, lambda g, j: (g, 0, 0))
    ublk = pl.BlockSpec((nb, tc, np_ * LANES), lambda g, j: (0, j, g))
    return pl.pallas_call(
        functools.partial(_s5_prompt_kernel, nb=nb, tc=tc),
        grid=(S5_NGB // np_, seq // tc),
        in_specs=[ublk, gblk((S5_J * LANES, 2 * S5_HL)), gblk((2 * S5_HL, S5_J * LANES)),
                  gblk((rows, S5_HL)), gblk((rows, S5_HL)), gblk((1, LANES))],
        out_specs=[ublk, st, st],
        out_shape=[jax.ShapeDtypeStruct(u3d.shape, F32),
                   jax.ShapeDtypeStruct((nb, S5_GROUPS * S5_STATE), F32),
                   jax.ShapeDtypeStruct((nb, S5_GROUPS * S5_STATE), F32)],
        scratch_shapes=[pltpu.VMEM((np_, nb * tc, LANES), F32),
                        pltpu.VMEM((np_, tc, rows, S5_J * LANES), F32),
                        pltpu.VMEM((np_, tc, rows, 2 * S5_HL), F32),
                        pltpu.VMEM((np_, tc, rows, 2 * S5_HL), F32),
                        pltpu.VMEM((np_, tc, rows, S5_J * LANES), F32),
                        pltpu.VMEM((np_, nb * tc, LANES), F32),
                        pltpu.VMEM((np_, 2, rows, S5_HL), F32)],
        compiler_params=pltpu.CompilerParams(dimension_semantics=("arbitrary",) * 2,
                                             vmem_limit_bytes=VMEM_LIMIT),
        name="s5_prompt",
    )(u3d, wbu2, wc2, a_re2, a_im2, dsk)


def _s5_prompt_params(wbu, wc, a_re, a_im, nb):
    h = S5_HL
    top = jnp.concatenate([wbu[:, :, 0:h], wbu[:, :, S5_SL:S5_SL + h]], axis=2)
    bot = jnp.concatenate([wbu[:, :, h:2 * h], wbu[:, :, S5_SL + h:S5_SL + 2 * h]], axis=2)
    wbu2 = jnp.concatenate([top, bot], axis=1)
    wc_j = [jnp.concatenate([wc[:, jj * h:(jj + 1) * h, :], wc[:, S5_SL + jj * h:S5_SL + (jj + 1) * h, :]],
                            axis=1) for jj in range(S5_J)]
    wc2 = jnp.concatenate(wc_j, axis=2)
    tile = lambda a: jnp.repeat(a.reshape(S5_NGB, S5_J, h), nb, axis=1)
    return wbu2, wc2, tile(a_re), tile(a_im)


def _mix_out_kernel(xp_ref, xs_ref, ogp_ref, ogs_ref, ysp_ref, yss_ref, glu_w_ref, glu_b_ref, s5n_ref, wo_ref,
                    nffn_ref, wr_ref, h1_ref, hn_ref, rt_ref, rtt_ref, cnt_ref, *, n_prompt_tiles):
    is_p = pl.program_id(0) < n_prompt_tiles
    x = jnp.where(is_p, xp_ref[...], xs_ref[...])
    og = jnp.where(is_p, ogp_ref[...], ogs_ref[...])
    y = jax.nn.gelu(jnp.where(is_p, ysp_ref[...], yss_ref[...]))
    y = y * jax.nn.sigmoid(_dot(y.astype(BF16), glu_w_ref[...]) + glu_b_ref[...])
    yn = _rms(y, s5n_ref[...]).astype(BF16)
    mix = _dot(og, wo_ref[0:D_GLA, :]) + _dot(yn, wo_ref[D_GLA:D_GLA + D_S5, :])
    h1 = x + mix
    h1_ref[...] = h1
    hn = _rms(h1, nffn_ref[...])
    hn_ref[...] = _pack_bf16_pair(hn[:, 0:HALF], hn[:, HALF:D_MODEL])
    hn_hi = hn.astype(BF16)
    hn_lo = (hn - hn_hi.astype(F32)).astype(BF16)
    logits = _dot(hn_hi, wr_ref[0]) + _dot(hn_hi, wr_ref[1]) + _dot(hn_lo, wr_ref[0])
    rt = _route(logits)

    @pl.when(pl.program_id(0) == 0)
    def _():
        cnt_ref[...] = jnp.zeros_like(cnt_ref)

    tm = rt.shape[0]
    lane = lax.broadcasted_iota(jnp.int32, rt.shape, 1).astype(F32)
    oh0 = lane == rt[:, 0:1]
    oh1 = lane == rt[:, 1:2]
    both = jnp.where(oh0 | oh1, 1.0, 0.0)
    ri = lax.broadcasted_iota(jnp.int32, (tm, tm), 0)
    ci = lax.broadcasted_iota(jnp.int32, (tm, tm), 1)
    before = _dot(jnp.where(ri > ci, 1.0, 0.0).astype(BF16), both.astype(BF16)) + cnt_ref[...]
    rank0 = jnp.sum(jnp.where(oh0, before, 0.0), axis=-1, keepdims=True)
    rank1 = jnp.sum(jnp.where(oh1, before, 0.0), axis=-1, keepdims=True)
    cnt_ref[...] += jnp.sum(both, axis=0, keepdims=True)
    rt = jnp.where(lane == 4.0, rank0, jnp.where(lane == 5.0, rank1, rt))
    rt_ref[...] = rt
    rtt_ref[...] = rt.T[0:8, :]


def _route(logits):
    col = lax.broadcasted_iota(jnp.int32, logits.shape, 1)
    colf = col.astype(F32)
    neg = -jnp.inf

    def first_argmax(vals):
        m = jnp.max(vals, axis=-1, keepdims=True)
        idx = jnp.min(jnp.where(vals == m, colf, float(LANES)), axis=-1, keepdims=True)
        return m, idx

    lg = jnp.where(col < N_EGROUPS, logits, neg)
    gmax, gsel = first_argmax(lg)
    p_g = 1.0 / jnp.sum(jnp.exp(lg - gmax), axis=-1, keepdims=True)
    ecol = col - N_EGROUPS
    egrp = (ecol >> 3).astype(F32)
    in_group = (ecol >= 0) & (ecol < N_EXPERTS) & (egrp == gsel)
    le = jnp.where(in_group, logits, neg)
    m1, i1 = first_argmax(le)
    le2 = jnp.where(colf == i1, neg, le)
    m2, i2 = first_argmax(le2)
    e2 = jnp.exp(m2 - m1)
    den = 1.0 + e2
    w1 = p_g * (1.0 / den)
    w2 = p_g * (e2 / den)
    e1f = i1 - float(N_EGROUPS)
    e2f = i2 - float(N_EGROUPS)
    out = jnp.where(col == 0, e1f, jnp.where(col == 1, e2f, jnp.where(col == 2, w1, jnp.where(col == 3, w2, 0.0))))
    return out


def _mix_out(xp, xs, ogp, ogs, ysp, yss, glu_w, glu_b, s5n, wo, nffn, wr):
    tm = TOK_TILE
    npt, nst = xp.shape[0] // tm, xs.shape[0] // tm
    t = (npt + nst) * tm
    row = lambda w: pl.BlockSpec((tm, w), lambda i: (i, 0))
    prow = lambda w: pl.BlockSpec((tm, w), lambda i: (jnp.minimum(i, npt - 1), 0))
    srow = lambda w: pl.BlockSpec((tm, w), lambda i: (jnp.maximum(i - npt, 0), 0))
    return pl.pallas_call(
        functools.partial(_mix_out_kernel, n_prompt_tiles=npt),
        grid=(npt + nst,),
        in_specs=[prow(D_MODEL), srow(D_MODEL), prow(D_GLA), srow(D_GLA), prow(D_S5), srow(D_S5),
                  _const_spec(glu_w.shape), _const_spec((1, D_S5)), _const_spec((1, D_S5)),
                  _const_spec(wo.shape), _const_spec((1, D_MODEL)), _const_spec(wr.shape)],
        out_specs=[row(D_MODEL), row(HALF), row(LANES), pl.BlockSpec((8, tm), lambda i: (0, i)),
                   pl.BlockSpec((1, LANES), lambda i: (0, 0))],
        out_shape=[jax.ShapeDtypeStruct((t, D_MODEL), F32), jax.ShapeDtypeStruct((t, HALF), jnp.uint32),
                   jax.ShapeDtypeStruct((t, LANES), F32), jax.ShapeDtypeStruct((8, t), F32),
                   jax.ShapeDtypeStruct((1, LANES), F32)],
        compiler_params=pltpu.CompilerParams(dimension_semantics=("arbitrary",), vmem_limit_bytes=VMEM_LIMIT),
        name="mix_out",
    )(xp, xs, ogp, ogs, ysp, yss, glu_w, glu_b, s5n, wo, nffn, wr)


DMA_UNROLL = 8


MOE_SB = 6
HALF = D_MODEL // 2


def _pack_bf16_pair(lo, hi):
    def bits(x):
        b = pltpu.bitcast(x, jnp.uint32)
        return (b + jnp.uint32(0x7FFF) + ((b >> 16) & jnp.uint32(1))) >> 16
    return bits(lo) | (bits(hi) << 16)


def _unpack_bf16_pair(u):
    return (pltpu.bitcast(u << 16, F32), pltpu.bitcast(u & jnp.uint32(0xFFFF0000), F32))


def _moe_kernel(se_ref, sm_ref, sn_ref, sbase_ref, xs_hbm, wg_ref, wu_ref, wd_ref, out_hbm,
                xb_ref, yb_ref, wgb_ref, wub_ref, wdb_ref, gsem, ssem):
    b = pl.program_id(0)
    nb = pl.num_programs(0)
    n = sn_ref[b]
    slot = lax.rem(b, 2)

    def issue_gather(blk, sl):
        base = pl.multiple_of(sbase_ref[blk], MOE_BLK)
        for m in range(1, MOE_SB + 1):
            @pl.when((sn_ref[blk] > 0) & (sm_ref[blk] == m))
            def _():
                rows = m * MOE_BLK
                pltpu.make_async_copy(xs_hbm.at[pl.ds(base, rows), :], xb_ref.at[sl, pl.ds(0, rows), :],
                                      gsem.at[sl]).start()

    def wait_gather(blocks, sl):
        rows = pl.ds(0, pl.multiple_of(blocks * MOE_BLK, MOE_BLK))
        pltpu.make_async_copy(xs_hbm.at[rows, :], xb_ref.at[sl, rows, :], gsem.at[sl]).wait()

    m_cur = sm_ref[b]
    m_prev = sm_ref[jnp.maximum(b - 1, 0)]
    has_prev = (b > 0) & (m_prev > 0)

    def out_copy(rows):
        dst = pl.ds(pl.multiple_of(sbase_ref[b], MOE_BLK), rows)
        return pltpu.make_async_copy(yb_ref.at[pl.ds(0, rows), :], out_hbm.at[dst, :], ssem.at[0])

    def wait_prev_out():
        rows = pl.ds(0, pl.multiple_of(m_prev * MOE_BLK, MOE_BLK))
        pltpu.make_async_copy(yb_ref.at[rows, :], out_hbm.at[rows, :], ssem.at[0]).wait()

    def write_out(rows):
        out_copy(rows).start()

        @pl.when(b == nb - 1)
        def _():
            out_copy(rows).wait()

    @pl.when(b == 0)
    def _():
        issue_gather(0, 0)

    @pl.when(b + 1 < nb)
    def _():
        issue_gather(b + 1, 1 - slot)

    def compute(rows):
        x_lo, x_hi = _unpack_bf16_pair(xb_ref[slot, 0:rows, :])
        x_lo, x_hi = x_lo.astype(BF16), x_hi.astype(BF16)
        gate = _dot(x_lo, wgb_ref[0:HALF, :]) + _dot(x_hi, wgb_ref[HALF:D_MODEL, :])
        up = _dot(x_lo, wub_ref[0:HALF, :]) + _dot(x_hi, wub_ref[HALF:D_MODEL, :])
        hid = (gate * jax.nn.sigmoid(gate) * up).astype(BF16)

        @pl.when(has_prev)
        def _():
            wait_prev_out()

        yb_ref[0:rows, :] = _pack_bf16_pair(_dot(hid, wdb_ref[:, 0:HALF]), _dot(hid, wdb_ref[:, HALF:D_MODEL]))
        write_out(rows)

    @pl.when(n > 0)
    def _():
        prev_e = se_ref[jnp.maximum(b - 1, 0)]

        @pl.when((b == 0) | (prev_e != se_ref[b]))
        def _():
            wgb_ref[...] = wg_ref[...].astype(BF16)
            wub_ref[...] = wu_ref[...].astype(BF16)
            wdb_ref[...] = wd_ref[...].astype(BF16)

        wait_gather(m_cur, slot)
        for m in range(1, MOE_SB + 1):
            @pl.when(m_cur == m)
            def _():
                compute(m * MOE_BLK)

    @pl.when((n == 0) & (m_cur > 0))
    def _():
        @pl.when(has_prev)
        def _():
            wait_prev_out()

        yb_ref[...] = jnp.zeros_like(yb_ref)
        for m in range(1, MOE_SB + 1):
            @pl.when(m_cur == m)
            def _():
                write_out(m * MOE_BLK)

    @pl.when((m_cur == 0) & has_prev)
    def _():
        wait_prev_out()


def _moe(x_slots, plan, w_gate, w_up, w_down, n_out_rows):
    sb_e, sb_m, sb_n, sb_base = plan
    sb_rows = MOE_SB * MOE_BLK
    wspec = lambda s: pl.BlockSpec((None,) + s, lambda b, se, *_: (se[b], 0, 0))
    grid_spec = pltpu.PrefetchScalarGridSpec(
        num_scalar_prefetch=4,
        grid=(sb_e.shape[0],),
        in_specs=[pl.BlockSpec(memory_space=pl.ANY),
                  wspec((D_MODEL, D_EXPERT)), wspec((D_MODEL, D_EXPERT)), wspec((D_EXPERT, D_MODEL))],
        out_specs=pl.BlockSpec(memory_space=pl.ANY),
        scratch_shapes=[pltpu.VMEM((2, sb_rows, HALF), jnp.uint32), pltpu.VMEM((sb_rows, HALF), jnp.uint32),
                        pltpu.VMEM((D_MODEL, D_EXPERT), BF16), pltpu.VMEM((D_MODEL, D_EXPERT), BF16),
                        pltpu.VMEM((D_EXPERT, D_MODEL), BF16), pltpu.SemaphoreType.DMA((2,)),
                        pltpu.SemaphoreType.DMA((1,))],
    )
    return pl.pallas_call(
        _moe_kernel,
        grid_spec=grid_spec,
        out_shape=jax.ShapeDtypeStruct((n_out_rows, HALF), jnp.uint32),
        compiler_params=pltpu.CompilerParams(dimension_semantics=("arbitrary",),
                                             vmem_limit_bytes=MOE_VMEM_LIMIT),
        name="moe_experts",
    )(sb_e, sb_m, sb_n, sb_base, x_slots, w_gate, w_up, w_down)


def _dispatch_kernel(dest_ref, zb_ref, hn_ref, out_hbm, buf_ref, zero_ref, sem, zsem):
    i = pl.program_id(0)
    nt = pl.num_programs(0)
    tm = hn_ref.shape[0]
    t_all = nt * tm
    slot = lax.rem(i, 2)
    nblk = zb_ref.shape[0]

    def zero_copy(blk):
        dst = pl.ds(pl.multiple_of(blk * MOE_BLK, MOE_BLK), MOE_BLK)
        return pltpu.make_async_copy(zero_ref, out_hbm.at[dst, :], zsem.at[0])

    @pl.when(i == 0)
    def _():
        zero_ref[...] = jnp.zeros_like(zero_ref)

        def start(blk, c):
            @pl.when(zb_ref[blk] == 1)
            def _():
                zero_copy(blk).start()
            return c

        def wait(blk, c):
            @pl.when(zb_ref[blk] == 1)
            def _():
                zero_copy(blk).wait()
            return c

        lax.fori_loop(0, nblk, start, 0)
        lax.fori_loop(0, nblk, wait, 0)

    def wait_rows(sl):
        for k in range(2):
            pltpu.make_async_copy(buf_ref.at[sl], out_hbm.at[pl.ds(0, tm), :], sem.at[sl]).wait()

    @pl.when(i >= 2)
    def _():
        wait_rows(slot)

    buf_ref[slot] = hn_ref[...]
    base = i * tm
    for a in range(2 * tm):
        row, k = a // 2, a % 2
        d = dest_ref[k * t_all + base + row]
        pltpu.make_async_copy(buf_ref.at[slot, pl.ds(row, 1), :], out_hbm.at[pl.ds(d, 1), :],
                              sem.at[slot]).start()

    @pl.when(i == nt - 1)
    def _():
        wait_rows(slot)

        @pl.when(nt > 1)
        def _():
            wait_rows(1 - slot)


def _dispatch(dest, zero_blk, hn_pk, n_slots):
    tm = TOK_TILE
    grid_spec = pltpu.PrefetchScalarGridSpec(
        num_scalar_prefetch=2,
        grid=(hn_pk.shape[0] // tm,),
        in_specs=[pl.BlockSpec((tm, HALF), lambda i, d, z: (i, 0))],
        out_specs=pl.BlockSpec(memory_space=pl.ANY),
        scratch_shapes=[pltpu.VMEM((2, tm, HALF), jnp.uint32), pltpu.VMEM((MOE_BLK, HALF), jnp.uint32),
                        pltpu.SemaphoreType.DMA((2,)), pltpu.SemaphoreType.DMA((1,))],
    )
    return pl.pallas_call(
        _dispatch_kernel,
        grid_spec=grid_spec,
        out_shape=jax.ShapeDtypeStruct((n_slots, HALF), jnp.uint32),
        compiler_params=pltpu.CompilerParams(dimension_semantics=("arbitrary",), vmem_limit_bytes=VMEM_LIMIT),
        name="moe_dispatch",
    )(dest, zero_blk, hn_pk)


def _moe_plan(rtt, cnt):
    t_all = rtt.shape[1]
    n_assign = 2 * t_all
    e_flat = rtt[0:2].astype(jnp.int32).reshape(-1)
    rank = rtt[4:6].astype(jnp.int32).reshape(-1)
    counts = cnt[0, 0:N_EXPERTS].astype(jnp.int32)
    padded = (counts + MOE_BLK - 1) // MOE_BLK * MOE_BLK
    pad_end = jnp.cumsum(padded)
    pad_start = pad_end - padded
    before = jnp.arange(N_EXPERTS, dtype=jnp.int32)[:, None] < e_flat[None, :]
    dest = rank + jnp.sum(jnp.where(before, padded[:, None], 0), axis=0)
    nblk = -(-(n_assign + N_EXPERTS * (MOE_BLK - 1)) // MOE_BLK)
    n_slots = nblk * MOE_BLK
    blk = jnp.arange(nblk, dtype=jnp.int32)
    last_of_expert = jnp.any((blk[:, None] == (pad_end // MOE_BLK - 1)[None, :]) & (counts > 0)[None, :], axis=1)
    zero_blk = (last_of_expert | (blk >= pad_end[-1] // MOE_BLK)).astype(jnp.int32)
    k_e = padded // MOE_BLK
    sbc = (k_e + MOE_SB - 1) // MOE_SB
    sb_end = jnp.cumsum(sbc)
    sb_start = sb_end - sbc
    n_sb = (nblk + (MOE_SB - 1) * N_EXPERTS) // MOE_SB
    s = jnp.arange(n_sb, dtype=jnp.int32)
    sb_e = jnp.minimum(jnp.sum((s[:, None] >= sb_end[None, :]).astype(jnp.int32), axis=1), N_EXPERTS - 1)
    j = s - sb_start[sb_e]
    real = s < sb_end[-1]
    tail_blk = pad_end[-1] // MOE_BLK + MOE_SB * (s - sb_end[-1])
    sb_m = jnp.where(real, jnp.clip(k_e[sb_e] - MOE_SB * j, 0, MOE_SB), jnp.clip(nblk - tail_blk, 0, MOE_SB))
    sb_n = jnp.where(real, jnp.clip(counts[sb_e] - MOE_SB * MOE_BLK * j, 0, MOE_SB * MOE_BLK), 0)
    sb_base = jnp.where(real, pad_start[sb_e] + MOE_SB * MOE_BLK * j, jnp.minimum(tail_blk, nblk - 1) * MOE_BLK)
    i32 = lambda a: a.astype(jnp.int32)
    return (i32(sb_e), i32(sb_m), i32(sb_n), i32(sb_base)), i32(dest), zero_blk, n_slots


def _ple_out_kernel(dest_ref, h1_ref, rt_ref, pp_ref, ps_ref, nple_ref, wpg_ref, wp_ref, nfin_ref, ys_hbm,
                    op_ref, os_ref, yb_ref, sem, *, n_prompt_tiles):
    i = pl.program_id(0)
    nt = pl.num_programs(0)
    tm = h1_ref.shape[0]
    t_all = nt * tm
    slot = lax.rem(i, 2)

    def issue(tile, sl):
        base = tile * tm

        def grp(g, c):
            for j in range(DMA_UNROLL):
                row = g * (DMA_UNROLL // 2) + j // 2
                d = dest_ref[(j % 2) * t_all + base + row]
                pltpu.make_async_copy(ys_hbm.at[pl.ds(d, 1), :], yb_ref.at[sl, j % 2, pl.ds(row, 1), :],
                                      sem.at[sl]).start()
            return c

        lax.fori_loop(0, 2 * tm // DMA_UNROLL, grp, 0)

    @pl.when(i == 0)
    def _():
        issue(0, 0)

    def wait_rows(sl):
        for k in range(2):
            pltpu.make_async_copy(ys_hbm.at[pl.ds(0, tm), :], yb_ref.at[sl, k], sem.at[sl]).wait()

    wait_rows(slot)
    is_p = i < n_prompt_tiles
    rt = rt_ref[...]
    w0, w1 = rt[:, 2:3], rt[:, 3:4]
    lo0, hi0 = _unpack_bf16_pair(yb_ref[slot, 0])
    lo1, hi1 = _unpack_bf16_pair(yb_ref[slot, 1])
    h2 = h1_ref[...] + jnp.concatenate([w0 * lo0 + w1 * lo1, w0 * hi0 + w1 * hi1], axis=1)
    hb = _rms(h2, nple_ref[...]).astype(BF16)
    nbase = jnp.minimum(i + 1, nt - 1) * tm
    n_chunk = 8
    cw, ca = D_MODEL // n_chunk, 2 * tm // n_chunk
    gates = []
    for c in range(n_chunk):
        gates.append(jax.nn.sigmoid(_dot(hb, wpg_ref[:, c * cw:(c + 1) * cw])))
        for a in range(c * ca, (c + 1) * ca):
            d = dest_ref[(a % 2) * t_all + nbase + a // 2]
            pltpu.make_async_copy(ys_hbm.at[pl.ds(d, 1), :], yb_ref.at[1 - slot, a % 2, pl.ds(a // 2, 1), :],
                                  sem.at[1 - slot]).start()
    gate = jnp.concatenate(gates, axis=1)

    @pl.when(i == nt - 1)
    def _():
        wait_rows(1 - slot)

    p = jnp.where(is_p, pp_ref[...], ps_ref[...])
    h3 = h2 + _dot(p.astype(BF16), wp_ref[...]) * gate
    y = _rms(h3, nfin_ref[...])

    @pl.when(is_p)
    def _():
        op_ref[...] = y

    @pl.when(jnp.logical_not(is_p))
    def _():
        os_ref[...] = y


def _ple_out(dest, h1, rt, pp, ps, nple, wpg, wp, nfin, y_slots):
    tm = TOK_TILE
    npt, nst = pp.shape[0] // tm, ps.shape[0] // tm
    row = lambda w: pl.BlockSpec((tm, w), lambda i, d: (i, 0))
    prow = lambda w: pl.BlockSpec((tm, w), lambda i, d: (jnp.minimum(i, npt - 1), 0))
    srow = lambda w: pl.BlockSpec((tm, w), lambda i, d: (jnp.maximum(i - npt, 0), 0))
    const = lambda shape: pl.BlockSpec(shape, lambda i, d: (0,) * len(shape), pipeline_mode=pl.Buffered(1))
    grid_spec = pltpu.PrefetchScalarGridSpec(
        num_scalar_prefetch=1,
        grid=(npt + nst,),
        in_specs=[row(D_MODEL), row(LANES), prow(D_PLE), srow(D_PLE), const((1, D_MODEL)), const(wpg.shape),
                  const(wp.shape), const((1, D_MODEL)), pl.BlockSpec(memory_space=pl.ANY)],
        out_specs=[prow(D_MODEL), srow(D_MODEL)],
        scratch_shapes=[pltpu.VMEM((2, 2, tm, HALF), jnp.uint32), pltpu.SemaphoreType.DMA((2,))],
    )
    return pl.pallas_call(
        functools.partial(_ple_out_kernel, n_prompt_tiles=npt),
        grid_spec=grid_spec,
        out_shape=[jax.ShapeDtypeStruct((pp.shape[0], D_MODEL), F32),
                   jax.ShapeDtypeStruct((ps.shape[0], D_MODEL), F32)],
        compiler_params=pltpu.CompilerParams(dimension_semantics=("arbitrary",), vmem_limit_bytes=VMEM_LIMIT),
        name="ple_out",
    )(dest, h1, rt, pp, ps, nple, wpg, wp, nfin, y_slots)


def _s5_params(lam_re, lam_im, log_dt, b_re, b_im, c_re, c_im, d_skip):
    dt = jnp.exp(log_dt)[:, None]
    mag = jnp.exp(lam_re * dt)
    ab_re = mag * jnp.cos(lam_im * dt)
    ab_im = mag * jnp.sin(lam_im * dt)
    den = lam_re * lam_re + lam_im * lam_im
    nr = ab_re - 1.0
    f_re = (nr * lam_re + ab_im * lam_im) / den
    f_im = (ab_im * lam_re - nr * lam_im) / den
    bb_re = f_re[..., None] * b_re - f_im[..., None] * b_im
    bb_im = f_re[..., None] * b_im + f_im[..., None] * b_re
    eye = jnp.eye(S5_GB, dtype=F32)

    def bu_w(bb):
        bb = bb.reshape(S5_NGB, S5_GB, S5_STATE, S5_GROUP)
        w = jnp.einsum('nlph,lm->nlhmp', bb, eye)
        return w.reshape(S5_NGB, S5_GB * S5_GROUP, S5_SL)

    def c_w(c):
        c = c.reshape(S5_NGB, S5_GB, S5_GROUP, S5_STATE)
        w = jnp.einsum('nlhp,lm->nlpmh', c, eye)
        return w.reshape(S5_NGB, S5_SL, S5_GB * S5_GROUP)

    wbu = jnp.concatenate([bu_w(bb_re), bu_w(bb_im)], axis=2).astype(BF16)
    wc = jnp.concatenate([c_w(c_re), -c_w(c_im)], axis=1).astype(BF16)
    a_re = ab_re.reshape(S5_NGB, 1, S5_SL)
    a_im = ab_im.reshape(S5_NGB, 1, S5_SL)
    dsk = d_skip.reshape(S5_NGB, 1, S5_GB * S5_GROUP)
    return wbu, wc, a_re, a_im, dsk


def kernel(x_prompt, x_sample, p_prompt, p_sample, state_gla, state_s5_re, state_s5_im, norm_mix, w_in, gla_w_gate_up, gla_gate_bias, gla_norm, s5_lam_re, s5_lam_im, s5_log_dt, s5_b_re, s5_b_im, s5_c_re, s5_c_im, s5_d, s5_glu_w, s5_glu_b, s5_norm, w_out, norm_ffn, router_group, router_expert, w_gate, w_up, w_down, norm_ple, w_ple, w_ple_gate, norm_final):
    depth = w_in.shape[0]
    assert depth == 1
    i = 0
    bp, lp, _ = x_prompt.shape
    bs, ls, _ = x_sample.shape
    tp, ts = bp * lp, bs * ls
    t_all = tp + ts

    wm = _w_in_prep(w_in[i])
    wgu = jnp.pad(gla_w_gate_up[i], ((0, LANES - GLA_RANK), (0, 0))).astype(BF16)
    gbias = gla_gate_bias[i].reshape(1, QK_W)
    wbu, wc, a_re, a_im, dsk = _s5_params(s5_lam_re[i], s5_lam_im[i], s5_log_dt[i], s5_b_re[i], s5_b_im[i],
                                          s5_c_re[i], s5_c_im[i], s5_d[i])
    glu_w = s5_glu_w[i].astype(BF16)
    wo = w_out[i].astype(BF16)
    wr32 = jnp.pad(jnp.concatenate([router_group[i], router_expert[i]], axis=1),
                   ((0, 0), (0, LANES - N_EGROUPS - N_EXPERTS)))
    wr_hi = wr32.astype(BF16)
    wr = jnp.stack([wr_hi, (wr32 - wr_hi.astype(F32)).astype(BF16)])
    wpg = w_ple_gate[i].astype(BF16)
    wp = w_ple[i].astype(BF16)
    vec = lambda a: a.reshape(1, -1)

    xp = x_prompt.reshape(tp, D_MODEL)
    xs = x_sample.reshape(ts, D_MODEL)

    qp, kp, vp, rp, lap, up = _in_proj(xp, vec(norm_mix[i]), wm, wgu, gbias, BF16)
    qs, ks, vs, rs, las, us = _in_proj(xs, vec(norm_mix[i]), wm, wgu, gbias, BF16)
    ogp, gla_p = _gla_prompt(qp, kp, vp, rp, lap, vec(gla_norm[i]), bp, lp)
    ogs, gla_s = _gla_sample(qs, ks, vs, rs, las, vec(gla_norm[i]), state_gla[i], bs, ls)
    wbu2, wc2, a_re2, a_im2 = _s5_prompt_params(wbu, wc, a_re, a_im, bp)
    ysp, re_p, im_p = _s5_prompt(up.reshape(bp, lp, D_S5), wbu2, wc2, a_re2, a_im2, dsk, tc=256)
    yss, re_s, im_s = _s5(us.reshape(bs, ls, D_S5), wbu, wc, a_re, a_im, dsk,
                          state_s5_re[i].reshape(bs, -1), state_s5_im[i].reshape(bs, -1),
                          nb=32, tc=ls, has_state=True)

    h1, hn_all, rt_all, rtt, cnt = _mix_out(xp, xs, ogp, ogs, ysp.reshape(tp, D_S5), yss.reshape(ts, D_S5),
                                            glu_w, vec(s5_glu_b[i]), vec(s5_norm[i]), wo, vec(norm_ffn[i]), wr)

    plan, dest, zero_blk, n_slots = _moe_plan(rtt, cnt)
    x_slots = _dispatch(dest, zero_blk, hn_all, n_slots)
    y_slots = _moe(x_slots, plan, w_gate[i], w_up[i], w_down[i], n_slots)

    y_p, y_s = _ple_out(dest, h1, rt_all, p_prompt[i].reshape(tp, D_PLE), p_sample[i].reshape(ts, D_PLE),
                        vec(norm_ple[i]), wpg, wp, vec(norm_final), y_slots)

    s5shape = lambda a, b: a.reshape(1, b, S5_GROUPS, S5_STATE)
    return (y_p.reshape(bp, lp, D_MODEL), y_s.reshape(bs, ls, D_MODEL),
            gla_p[None], s5shape(re_p, bp), s5shape(im_p, bp),
            gla_s[None], s5shape(re_s, bs), s5shape(im_s, bs))
```

```python
import functools
import math

import jax
import jax.numpy as jnp
from jax import lax
from jax.experimental import pallas as pl
from jax.experimental.pallas import tpu as pltpu

F32 = jnp.float32
BF16 = jnp.bfloat16

D_MODEL = 2048
D_GLA = 1024
D_S5 = 1024
GLA_HEADS = 4
GLA_DV = 256
GLA_DK = 128
GLA_RANK = 16
GLA_CHUNK = 64
S5_GROUP = 16
S5_GROUPS = 64
S5_STATE = 64
N_EGROUPS = 4
N_EPG = 8
N_EXPERTS = 32
D_EXPERT = 512
D_PLE = 256
EPS = 1e-6

LANES = 128
QK_W = GLA_HEADS * GLA_DK
S5_GB = 8
S5_NGB = S5_GROUPS // S5_GB
S5_SL = S5_GB * S5_STATE
TOK_TILE = 256
MOE_BLK = 128
VMEM_LIMIT = 56 * 1024 * 1024
MOE_VMEM_LIMIT = 60 * 1024 * 1024


def _const_spec(shape):
    nd = len(shape)
    return pl.BlockSpec(shape, lambda *_: (0,) * nd, pipeline_mode=pl.Buffered(1))


def _rms(x, g):
    return x * lax.rsqrt(jnp.mean(x * x, axis=-1, keepdims=True) + EPS) * g


def _dot(a, b):
    return jnp.dot(a, b, preferred_element_type=F32)


def _log_sigmoid(x):
    return -(jnp.maximum(-x, 0.0) + jnp.log1p(jnp.exp(-jnp.abs(x))))


N_QKVR = 2 * QK_W + 2 * D_GLA
W_IN_COLS = N_QKVR + GLA_RANK + D_S5


def _w_in_prep_kernel(w_ref, o_ref):
    o_ref[:, 0:N_QKVR] = w_ref[:, 0:N_QKVR].astype(BF16)
    tail = w_ref[:, N_QKVR:W_IN_COLS]
    o_ref[:, N_QKVR:N_QKVR + D_S5] = tail[:, GLA_RANK:GLA_RANK + D_S5].astype(BF16)
    o_ref[:, N_QKVR + D_S5:N_QKVR + D_S5 + LANES] = tail[:, 0:LANES].astype(BF16)


def _w_in_prep(wi):
    rows = 256
    return pl.pallas_call(
        _w_in_prep_kernel,
        grid=(D_MODEL // rows,),
        in_specs=[pl.BlockSpec((rows, W_IN_COLS), lambda i: (i, 0))],
        out_specs=pl.BlockSpec((rows, N_QKVR + D_S5 + LANES), lambda i: (i, 0)),
        out_shape=jax.ShapeDtypeStruct((D_MODEL, N_QKVR + D_S5 + LANES), BF16),
        compiler_params=pltpu.CompilerParams(dimension_semantics=("arbitrary",), vmem_limit_bytes=VMEM_LIMIT),
        name="w_in_prep",
    )(wi)


def _in_proj_kernel(x_ref, g_ref, wm_ref, wgu_ref, gb_ref,
                    q_ref, k_ref, v_ref, r_ref, la_ref, u_ref):
    hb = _rms(x_ref[...], g_ref[...]).astype(BF16)

    def seg(a, b):
        return _dot(hb, wm_ref[:, a:b])

    q_ref[...] = (seg(0, QK_W) * (GLA_DK ** -0.5)).astype(q_ref.dtype)
    k_ref[...] = seg(QK_W, 2 * QK_W).astype(k_ref.dtype)
    v_ref[...] = seg(2 * QK_W, 2 * QK_W + D_GLA).astype(v_ref.dtype)
    r_ref[...] = seg(2 * QK_W + D_GLA, N_QKVR).astype(r_ref.dtype)
    u_ref[...] = seg(N_QKVR, N_QKVR + D_S5)
    zg = seg(N_QKVR + D_S5, N_QKVR + D_S5 + LANES)
    xg = _dot(zg.astype(BF16), wgu_ref[...]) + gb_ref[...]
    la_ref[...] = _log_sigmoid(xg) * (1.0 / 16.0)


def _in_proj(x2d, g, wm, wgu, gbias, act_dtype):
    t = x2d.shape[0]
    tm = TOK_TILE
    row = lambda w: pl.BlockSpec((tm, w), lambda i: (i, 0))
    return pl.pallas_call(
        _in_proj_kernel,
        grid=(t // tm,),
        in_specs=[row(D_MODEL), _const_spec((1, D_MODEL)), _const_spec(wm.shape),
                  _const_spec(wgu.shape), _const_spec((1, QK_W))],
        out_specs=[row(QK_W), row(QK_W), row(D_GLA), row(D_GLA), row(QK_W), row(D_S5)],
        out_shape=[jax.ShapeDtypeStruct((t, QK_W), act_dtype), jax.ShapeDtypeStruct((t, QK_W), act_dtype),
                   jax.ShapeDtypeStruct((t, D_GLA), act_dtype), jax.ShapeDtypeStruct((t, D_GLA), act_dtype),
                   jax.ShapeDtypeStruct((t, QK_W), F32), jax.ShapeDtypeStruct((t, D_S5), F32)],
        compiler_params=pltpu.CompilerParams(dimension_semantics=("arbitrary",), vmem_limit_bytes=VMEM_LIMIT),
        name="in_proj",
    )(x2d, g, wm, wgu, gbias)


_NT = (((1,), (1,)), ((), ()))
_TN = (((0,), (0,)), ((), ()))


def _gla_pre(q, k, la, c):
    r = q.shape[0]
    shift = int(math.log2(c))
    ri = lax.broadcasted_iota(jnp.int32, (r, r), 0)
    si = lax.broadcasted_iota(jnp.int32, (r, r), 1)
    mask = ((ri >> shift) == (si >> shift)) & (ri >= si)
    tri = jnp.where(mask, 1.0, 0.0).astype(BF16)
    hi = la.astype(BF16)
    r1 = la - hi.astype(F32)
    mid = r1.astype(BF16)
    lo = (r1 - mid.astype(F32)).astype(BF16)
    cum = _dot(tri, hi) + _dot(tri, mid) + _dot(tri, lo)
    last = jnp.concatenate([jnp.broadcast_to(cum[(i + 1) * c - 1:(i + 1) * c, :], (c, cum.shape[1]))
                            for i in range(r // c)], axis=0)
    qe = (q * jnp.exp(cum)).astype(BF16)
    ke = (k * jnp.exp(-cum)).astype(BF16)
    kd = (k * jnp.exp(last - cum)).astype(BF16)
    return qe, ke, kd, cum, mask


def _gla_intra(qe, ke, v, mask):
    sc = lax.dot_general(qe, ke, _NT, preferred_element_type=F32)
    return _dot(jnp.where(mask, sc, 0.0).astype(BF16), v)


def _gla_finish(o, r, g):
    rf = r.astype(F32)
    return _rms(o, g) * (rf * jax.nn.sigmoid(rf))


GLA_SAFE_LOG_DECAY = -60.0


def _gla_token_step(t, rows, q, k, v, a_all, st_ref, oacc_ref):
    m = rows == t
    a = jnp.sum(jnp.where(m, a_all, 0.0), axis=0, keepdims=True)
    kt = jnp.where(m, k, 0.0).astype(BF16)
    qt = jnp.where(m, q, 0.0).astype(BF16)
    vt = jnp.where(m, v, jnp.zeros_like(v))
    st = st_ref[...] * a + lax.dot_general(vt, kt, _TN, preferred_element_type=F32)
    st_ref[...] = st
    oacc_ref[...] += lax.dot_general(qt, st.astype(BF16), _NT, preferred_element_type=F32)


def _gla_prompt_kernel(q_ref, k_ref, v_ref, r_ref, la_ref, g_ref, o_ref, sfin_ref, st_ref, oacc_ref,
                       *, n_chunks):
    j = pl.program_id(1)

    @pl.when(j == 0)
    def _():
        st_ref[...] = jnp.zeros_like(st_ref)

    c = GLA_CHUNK
    qe, ke, kd, cum, mask = _gla_pre(q_ref[...].astype(F32), k_ref[...].astype(F32), la_ref[...], c)
    g = g_ref[...]
    safe = jnp.min(cum) >= GLA_SAFE_LOG_DECAY

    @pl.when(safe)
    def _():
        for h in range(GLA_HEADS):
            kc = slice(h * GLA_DK, (h + 1) * GLA_DK)
            vc = slice(h * GLA_DV, (h + 1) * GLA_DV)
            v = v_ref[:, vc]
            qe_h, kd_h = qe[:, kc], kd[:, kc]
            o = _gla_intra(qe_h, ke[:, kc], v, mask)
            st = st_ref[h]
            inter = []
            for ci in range(n_chunks):
                rows = slice(ci * c, (ci + 1) * c)
                inter.append(lax.dot_general(qe_h[rows], st.astype(BF16), _NT, preferred_element_type=F32))
                dec = jnp.exp(cum[(ci + 1) * c - 1:(ci + 1) * c, kc])
                st = st * dec + lax.dot_general(v[rows], kd_h[rows], _TN, preferred_element_type=F32)
            st_ref[h] = st
            o = o + jnp.concatenate(inter, axis=0)
            o_ref[:, vc] = _gla_finish(o, r_ref[:, vc], g).astype(o_ref.dtype)

    @pl.when(jnp.logical_not(safe))
    def _():
        n_rows = n_chunks * c
        rows = lax.broadcasted_iota(jnp.int32, (n_rows, 1), 0)
        for h in range(GLA_HEADS):
            kc = slice(h * GLA_DK, (h + 1) * GLA_DK)
            vc = slice(h * GLA_DV, (h + 1) * GLA_DV)
            q, k, v = q_ref[:, kc].astype(F32), k_ref[:, kc].astype(F32), v_ref[:, vc]
            a_all = jnp.exp(la_ref[:, kc])
            oacc_ref[...] = jnp.zeros_like(oacc_ref)

            def step(t, carry):
                _gla_token_step(t, rows, q, k, v, a_all, st_ref.at[h], oacc_ref)
                return carry

            lax.fori_loop(0, n_rows, step, 0)
            o_ref[:, vc] = _gla_finish(oacc_ref[...], r_ref[:, vc], g).astype(o_ref.dtype)

    @pl.when(j == pl.num_programs(1) - 1)
    def _():
        for h in range(GLA_HEADS):
            sfin_ref[0, h] = st_ref[h].T


def _gla_prompt(q, k, v, r, la, g, batch, seq):
    rb = 4 * GLA_CHUNK
    nj = seq // rb
    row = lambda w: pl.BlockSpec((rb, w), lambda b, j: (b * nj + j, 0))
    return pl.pallas_call(
        functools.partial(_gla_prompt_kernel, n_chunks=rb // GLA_CHUNK),
        grid=(batch, nj),
        in_specs=[row(QK_W), row(QK_W), row(D_GLA), row(D_GLA), row(QK_W), _const_spec((1, GLA_DV))],
        out_specs=[row(D_GLA),
                   pl.BlockSpec((1, GLA_HEADS, GLA_DK, GLA_DV), lambda b, j: (b, 0, 0, 0))],
        out_shape=[jax.ShapeDtypeStruct((batch * seq, D_GLA), BF16),
                   jax.ShapeDtypeStruct((batch, GLA_HEADS, GLA_DK, GLA_DV), F32)],
        scratch_shapes=[pltpu.VMEM((GLA_HEADS, GLA_DV, GLA_DK), F32), pltpu.VMEM((rb, GLA_DV), F32)],
        compiler_params=pltpu.CompilerParams(dimension_semantics=("arbitrary", "arbitrary"),
                                             vmem_limit_bytes=VMEM_LIMIT),
        name="gla_prompt",
    )(q, k, v, r, la, g)


def _gla_sample_kernel(q_ref, k_ref, v_ref, r_ref, la_ref, g_ref, s0_ref, o_ref, sfin_ref, st_ref, oacc_ref,
                       *, n_seq, seq):
    qe, ke, kd, cum, mask = _gla_pre(q_ref[...].astype(F32), k_ref[...].astype(F32), la_ref[...], seq)
    g = g_ref[...]
    safe = jnp.min(cum) >= GLA_SAFE_LOG_DECAY

    @pl.when(safe)
    def _():
        lasts = jnp.concatenate([cum[(s + 1) * seq - 1:(s + 1) * seq, :] for s in range(n_seq)]
                                + [jnp.zeros((GLA_DK - n_seq, cum.shape[1]), F32)], axis=0)
        pair = 2 * seq
        upper = lax.broadcasted_iota(jnp.int32, (pair, GLA_DK), 0) < seq
        for h in range(GLA_HEADS):
            kc = slice(h * GLA_DK, (h + 1) * GLA_DK)
            vc = slice(h * GLA_DV, (h + 1) * GLA_DV)
            v = v_ref[:, vc]
            qe_h, kd_h = qe[:, kc], kd[:, kc]
            o = _gla_intra(qe_h, ke[:, kc], v, mask)
            dec_t = jnp.exp(lasts[:, kc].T)
            inter = []
            for p in range(n_seq // 2):
                rows = slice(p * pair, (p + 1) * pair)
                qe_p, kd_p, v_p = qe_h[rows], kd_h[rows], v[rows]
                for half in range(2):
                    s = 2 * p + half
                    s0 = s0_ref[s, h]
                    o_s = _dot(qe_p, s0.astype(BF16))
                    inter.append(o_s[half * seq:(half + 1) * seq])
                    kd_s = jnp.where(upper if half == 0 else ~upper, kd_p, jnp.zeros_like(kd_p))
                    dec = jnp.broadcast_to(dec_t[:, s:s + 1], (GLA_DK, GLA_DV))
                    sfin_ref[s, h] = s0 * dec + lax.dot_general(kd_s, v_p, _TN, preferred_element_type=F32)
            o = o + jnp.concatenate(inter, axis=0)
            o_ref[:, vc] = _gla_finish(o, r_ref[:, vc], g).astype(o_ref.dtype)

    @pl.when(jnp.logical_not(safe))
    def _():
        n_rows = n_seq * seq
        rows = lax.broadcasted_iota(jnp.int32, (n_rows, 1), 0)
        for h in range(GLA_HEADS):
            kc = slice(h * GLA_DK, (h + 1) * GLA_DK)
            vc = slice(h * GLA_DV, (h + 1) * GLA_DV)
            q, k, v = q_ref[:, kc].astype(F32), k_ref[:, kc].astype(F32), v_ref[:, vc]
            a_all = jnp.exp(la_ref[:, kc])
            oacc_ref[...] = jnp.zeros_like(oacc_ref)

            def step(t, carry):
                s = t // seq

                @pl.when(t % seq == 0)
                def _():
                    st_ref[...] = s0_ref[s, h].T

                _gla_token_step(t, rows, q, k, v, a_all, st_ref, oacc_ref)

                @pl.when(t % seq == seq - 1)
                def _():
                    sfin_ref[s, h] = st_ref[...].T

                return carry

            lax.fori_loop(0, n_rows, step, 0)
            o_ref[:, vc] = _gla_finish(oacc_ref[...], r_ref[:, vc], g).astype(o_ref.dtype)


def _gla_sample(q, k, v, r, la, g, s0, batch, seq):
    ns = 16
    rb = ns * seq
    row = lambda w: pl.BlockSpec((rb, w), lambda i: (i, 0))
    st = pl.BlockSpec((ns, GLA_HEADS, GLA_DK, GLA_DV), lambda i: (i, 0, 0, 0))
    return pl.pallas_call(
        functools.partial(_gla_sample_kernel, n_seq=ns, seq=seq),
        grid=(batch // ns,),
        in_specs=[row(QK_W), row(QK_W), row(D_GLA), row(D_GLA), row(QK_W), _const_spec((1, GLA_DV)), st],
        out_specs=[row(D_GLA), st],
        out_shape=[jax.ShapeDtypeStruct((batch * seq, D_GLA), BF16),
                   jax.ShapeDtypeStruct((batch, GLA_HEADS, GLA_DK, GLA_DV), F32)],
        scratch_shapes=[pltpu.VMEM((GLA_DV, GLA_DK), F32), pltpu.VMEM((rb, GLA_DV), F32)],
        compiler_params=pltpu.CompilerParams(dimension_semantics=("arbitrary",), vmem_limit_bytes=VMEM_LIMIT),
        name="gla_sample",
    )(q, k, v, r, la, g, s0)


def _s5_kernel(u_ref, wbu_ref, wc_ref, are_ref, aim_ref, d_ref, h0r_ref, h0i_ref,
               y_ref, sre_ref, sim_ref, bu_ref, xs_ref, car_ref, *, nb, tc, has_state):
    j = pl.program_id(2)

    @pl.when(j == 0)
    def _():
        if has_state:
            car_ref[0] = h0r_ref[...]
            car_ref[1] = h0i_ref[...]
        else:
            car_ref[...] = jnp.zeros_like(car_ref)

    u2 = u_ref[...].reshape(nb * tc, LANES)
    ub = u2.astype(BF16)
    nl = S5_SL // LANES
    for l in range(2 * nl):
        bu_ref[l] = _dot(ub, wbu_ref[0, :, l * LANES:(l + 1) * LANES])
    a_r = [jnp.broadcast_to(are_ref[0, :, l * LANES:(l + 1) * LANES], (nb, LANES)) for l in range(nl)]
    a_i = [jnp.broadcast_to(aim_ref[0, :, l * LANES:(l + 1) * LANES], (nb, LANES)) for l in range(nl)]

    def step(t, carry):
        rows = pl.ds(t, nb, stride=tc)
        out = []
        for l in range(nl):
            xr, xi = carry[2 * l], carry[2 * l + 1]
            nr = a_r[l] * xr - a_i[l] * xi + bu_ref[l, rows, :]
            ni = a_r[l] * xi + a_i[l] * xr + bu_ref[nl + l, rows, :]
            xs_ref[l, rows, :] = nr
            xs_ref[nl + l, rows, :] = ni
            out += [nr, ni]
        return tuple(out)

    init = []
    for l in range(nl):
        init += [car_ref[0, :, l * LANES:(l + 1) * LANES], car_ref[1, :, l * LANES:(l + 1) * LANES]]
    fin = lax.fori_loop(0, tc, step, tuple(init), unroll=8)
    xr = jnp.concatenate([fin[2 * l] for l in range(nl)], axis=1)
    xi = jnp.concatenate([fin[2 * l + 1] for l in range(nl)], axis=1)
    car_ref[0] = xr
    car_ref[1] = xi
    y = d_ref[0] * u2
    for l in range(2 * nl):
        y = y + _dot(xs_ref[l].astype(BF16), wc_ref[0, l * LANES:(l + 1) * LANES, :])
    y_ref[...] = y.reshape(nb, tc, LANES)

    @pl.when(j == pl.num_programs(2) - 1)
    def _():
        sre_ref[...] = xr
        sim_ref[...] = xi


def _s5(u3d, wbu, wc, a_re, a_im, dsk, h0r, h0i, nb, tc, has_state):
    batch, seq, _ = u3d.shape
    grid = (S5_NGB, batch // nb, seq // tc)
    st = pl.BlockSpec((nb, S5_SL), lambda g, b, j: (b, g))
    par = lambda w: pl.BlockSpec((1, 1, w), lambda g, b, j: (g, 0, 0))
    ublk = pl.BlockSpec((nb, tc, LANES), lambda g, b, j: (b, j, g))
    return pl.pallas_call(
        functools.partial(_s5_kernel, nb=nb, tc=tc, has_state=has_state),
        grid=grid,
        in_specs=[ublk,
                  pl.BlockSpec((1, LANES, 2 * S5_SL), lambda g, b, j: (g, 0, 0)),
                  pl.BlockSpec((1, 2 * S5_SL, LANES), lambda g, b, j: (g, 0, 0)),
                  par(S5_SL), par(S5_SL), par(LANES), st, st],
        out_specs=[ublk, st, st],
        out_shape=[jax.ShapeDtypeStruct(u3d.shape, F32),
                   jax.ShapeDtypeStruct((batch, S5_GROUPS * S5_STATE), F32),
                   jax.ShapeDtypeStruct((batch, S5_GROUPS * S5_STATE), F32)],
        scratch_shapes=[pltpu.VMEM((2 * S5_SL // LANES, nb * tc, LANES), F32),
                        pltpu.VMEM((2 * S5_SL // LANES, nb * tc, LANES), F32),
                        pltpu.VMEM((2, nb, S5_SL), F32)],
        compiler_params=pltpu.CompilerParams(dimension_semantics=("arbitrary",) * 3,
                                             vmem_limit_bytes=VMEM_LIMIT),
        name="s5_state" if has_state else "s5_zero",
    )(u3d, wbu, wc, a_re, a_im, dsk, h0r, h0i)


S5_J = 2
S5_HL = S5_SL // S5_J
S5_PAIR = 2


def _s5_prompt_kernel(u_ref, wbu_ref, wc_ref, are_ref, aim_ref, d_ref, y_ref, sre_ref, sim_ref,
                      u2_ref, lhs_ref, bu_ref, xs_ref, y2_ref, yo_ref, car_ref, *, nb, tc):
    g = pl.program_id(0)
    j = pl.program_id(1)
    rows = nb * S5_J
    npair = S5_PAIR

    @pl.when((g == 0) & (j == 0))
    def _():
        lhs_ref[...] = jnp.zeros_like(lhs_ref)

    @pl.when(j == 0)
    def _():
        car_ref[...] = jnp.zeros_like(car_ref)

    u2 = u_ref[...].reshape(nb * tc, npair * LANES)
    for p in range(npair):
        u2_ref[p] = u2[:, p * LANES:(p + 1) * LANES]

    def build(t, c):
        for p in range(npair):
            u4 = u2_ref[p, pl.ds(t, nb, stride=tc), :]
            for jj in range(S5_J):
                lhs_ref[p, t, jj * nb:(jj + 1) * nb, jj * LANES:(jj + 1) * LANES] = u4
        return c

    lax.fori_loop(0, tc, build, 0, unroll=True)
    for p in range(npair):
        lhs = lhs_ref[p].reshape(tc * rows, S5_J * LANES).astype(BF16)
        bu_ref[p] = _dot(lhs, wbu_ref[p]).reshape(tc, rows, 2 * S5_HL)
    ar = [are_ref[p] for p in range(npair)]
    ai = [aim_ref[p] for p in range(npair)]

    def step(t, carry):
        out = []
        for p in range(npair):
            xr, xi = carry[2 * p], carry[2 * p + 1]
            tile = bu_ref[p, t]
            nr = ar[p] * xr - ai[p] * xi + tile[:, 0:S5_HL]
            ni = ar[p] * xi + ai[p] * xr + tile[:, S5_HL:2 * S5_HL]
            xs_ref[p, t] = jnp.concatenate([nr, ni], axis=1)
            out += [nr, ni]
        return tuple(out)

    init = tuple(car_ref[p, c] for p in range(npair) for c in range(2))
    fin = lax.fori_loop(0, tc, step, init, unroll=True)
    for p in range(npair):
        car_ref[p, 0] = fin[2 * p]
        car_ref[p, 1] = fin[2 * p + 1]
        xs = xs_ref[p].reshape(tc * rows, 2 * S5_HL).astype(BF16)
        y2_ref[p] = _dot(xs, wc_ref[p]).reshape(tc, rows, S5_J * LANES)
    first_half = lax.broadcasted_iota(jnp.int32, (rows, LANES), 0) < nb

    def unperm(t, c):
        for p in range(npair):
            t2 = y2_ref[p, t]
            part = jnp.where(first_half, t2[:, 0:LANES], t2[:, LANES:2 * LANES])
            yo_ref[p, pl.ds(t, nb, stride=tc), :] = (part + pltpu.roll(part, nb, axis=0))[0:nb]
        return c

    lax.fori_loop(0, tc, unperm, 0, unroll=True)
    y = jnp.concatenate([yo_ref[p] + d_ref[p] * u2_ref[p] for p in range(npair)], axis=1)
    y_ref[...] = y.reshape(nb, tc, npair * LANES)

    @pl.when(j == pl.num_programs(1) - 1)
    def _():
        def gather_state(x):
            return jnp.concatenate([x[jj * nb:(jj + 1) * nb] for jj in range(S5_J)], axis=1)

        sre_ref[...] = jnp.concatenate([gather_state(fin[2 * p]) for p in range(npair)], axis=1)
        sim_ref[...] = jnp.concatenate([gather_state(fin[2 * p + 1]) for p in range(npair)], axis=1)


def _s5_prompt(u3d, wbu2, wc2, a_re2, a_im2, dsk, tc):
    nb, seq, _ = u3d.shape
    rows = nb * S5_J
    assert rows == 8
    np_ = S5_PAIR
    st = pl.BlockSpec((nb, np_ * S5_SL), lambda g, j: (0, g))
    gblk = lambda s: pl.BlockSpec((np_,) + s, lambda g, j: (g, 0, 0))
    ublk = pl.BlockSpec((nb, tc, np_ * LANES), lambda g, j: (0, j, g))
    return pl.pallas_call(
        functools.partial(_s5_prompt_kernel, nb=nb, tc=tc),
        grid=(S5_NGB // np_, seq // tc),
        in_specs=[ublk, gblk((S5_J * LANES, 2 * S5_HL)), gblk((2 * S5_HL, S5_J * LANES)),
                  gblk((rows, S5_HL)), gblk((rows, S5_HL)), gblk((1, LANES))],
        out_specs=[ublk, st, st],
        out_shape=[jax.ShapeDtypeStruct(u3d.shape, F32),
                   jax.ShapeDtypeStruct((nb, S5_GROUPS * S5_STATE), F32),
                   jax.ShapeDtypeStruct((nb, S5_GROUPS * S5_STATE), F32)],
        scratch_shapes=[pltpu.VMEM((np_, nb * tc, LANES), F32),
                        pltpu.VMEM((np_, tc, rows, S5_J * LANES), F32),
                        pltpu.VMEM((np_, tc, rows, 2 * S5_HL), F32),
                        pltpu.VMEM((np_, tc, rows, 2 * S5_HL), F32),
                        pltpu.VMEM((np_, tc, rows, S5_J * LANES), F32),
                        pltpu.VMEM((np_, nb * tc, LANES), F32),
                        pltpu.VMEM((np_, 2, rows, S5_HL), F32)],
        compiler_params=pltpu.CompilerParams(dimension_semantics=("arbitrary",) * 2,
                                             vmem_limit_bytes=VMEM_LIMIT),
        name="s5_prompt",
    )(u3d, wbu2, wc2, a_re2, a_im2, dsk)


def _s5_prompt_params(wbu, wc, a_re, a_im, nb):
    h = S5_HL
    top = jnp.concatenate([wbu[:, :, 0:h], wbu[:, :, S5_SL:S5_SL + h]], axis=2)
    bot = jnp.concatenate([wbu[:, :, h:2 * h], wbu[:, :, S5_SL + h:S5_SL + 2 * h]], axis=2)
    wbu2 = jnp.concatenate([top, bot], axis=1)
    wc_j = [jnp.concatenate([wc[:, jj * h:(jj + 1) * h, :], wc[:, S5_SL + jj * h:S5_SL + (jj + 1) * h, :]],
                            axis=1) for jj in range(S5_J)]
    wc2 = jnp.concatenate(wc_j, axis=2)
    tile = lambda a: jnp.repeat(a.reshape(S5_NGB, S5_J, h), nb, axis=1)
    return wbu2, wc2, tile(a_re), tile(a_im)


def _mix_out_kernel(xp_ref, xs_ref, ogp_ref, ogs_ref, ysp_ref, yss_ref, glu_w_ref, glu_b_ref, s5n_ref, wo_ref,
                    nffn_ref, wr_ref, h1_ref, hn_ref, rt_ref, rtt_ref, cnt_ref, *, n_prompt_tiles):
    is_p = pl.program_id(0) < n_prompt_tiles
    x = jnp.where(is_p, xp_ref[...], xs_ref[...])
    og = jnp.where(is_p, ogp_ref[...], ogs_ref[...])
    y = jax.nn.gelu(jnp.where(is_p, ysp_ref[...], yss_ref[...]))
    y = y * jax.nn.sigmoid(_dot(y.astype(BF16), glu_w_ref[...]) + glu_b_ref[...])
    yn = _rms(y, s5n_ref[...]).astype(BF16)
    mix = _dot(og, wo_ref[0:D_GLA, :]) + _dot(yn, wo_ref[D_GLA:D_GLA + D_S5, :])
    h1 = x + mix
    h1_ref[...] = h1
    hn = _rms(h1, nffn_ref[...])
    hn_ref[...] = _pack_bf16_pair(hn[:, 0:HALF], hn[:, HALF:D_MODEL])
    hn_hi = hn.astype(BF16)
    hn_lo = (hn - hn_hi.astype(F32)).astype(BF16)
    logits = _dot(hn_hi, wr_ref[0]) + _dot(hn_hi, wr_ref[1]) + _dot(hn_lo, wr_ref[0])
    rt = _route(logits)

    @pl.when(pl.program_id(0) == 0)
    def _():
        cnt_ref[...] = jnp.zeros_like(cnt_ref)

    tm = rt.shape[0]
    lane = lax.broadcasted_iota(jnp.int32, rt.shape, 1).astype(F32)
    oh0 = lane == rt[:, 0:1]
    oh1 = lane == rt[:, 1:2]
    both = jnp.where(oh0 | oh1, 1.0, 0.0)
    ri = lax.broadcasted_iota(jnp.int32, (tm, tm), 0)
    ci = lax.broadcasted_iota(jnp.int32, (tm, tm), 1)
    before = _dot(jnp.where(ri > ci, 1.0, 0.0).astype(BF16), both.astype(BF16)) + cnt_ref[...]
    rank0 = jnp.sum(jnp.where(oh0, before, 0.0), axis=-1, keepdims=True)
    rank1 = jnp.sum(jnp.where(oh1, before, 0.0), axis=-1, keepdims=True)
    cnt_ref[...] += jnp.sum(both, axis=0, keepdims=True)
    rt = jnp.where(lane == 4.0, rank0, jnp.where(lane == 5.0, rank1, rt))
    rt_ref[...] = rt
    rtt_ref[...] = rt.T[0:8, :]


def _route(logits):
    col = lax.broadcasted_iota(jnp.int32, logits.shape, 1)
    colf = col.astype(F32)
    neg = -jnp.inf

    def first_argmax(vals):
        m = jnp.max(vals, axis=-1, keepdims=True)
        idx = jnp.min(jnp.where(vals == m, colf, float(LANES)), axis=-1, keepdims=True)
        return m, idx

    lg = jnp.where(col < N_EGROUPS, logits, neg)
    gmax, gsel = first_argmax(lg)
    p_g = 1.0 / jnp.sum(jnp.exp(lg - gmax), axis=-1, keepdims=True)
    ecol = col - N_EGROUPS
    egrp = (ecol >> 3).astype(F32)
    in_group = (ecol >= 0) & (ecol < N_EXPERTS) & (egrp == gsel)
    le = jnp.where(in_group, logits, neg)
    m1, i1 = first_argmax(le)
    le2 = jnp.where(colf == i1, neg, le)
    m2, i2 = first_argmax(le2)
    e2 = jnp.exp(m2 - m1)
    den = 1.0 + e2
    w1 = p_g * (1.0 / den)
    w2 = p_g * (e2 / den)
    e1f = i1 - float(N_EGROUPS)
    e2f = i2 - float(N_EGROUPS)
    out = jnp.where(col == 0, e1f, jnp.where(col == 1, e2f, jnp.where(col == 2, w1, jnp.where(col == 3, w2, 0.0))))
    return out


def _mix_out(xp, xs, ogp, ogs, ysp, yss, glu_w, glu_b, s5n, wo, nffn, wr):
    tm = TOK_TILE
    npt, nst = xp.shape[0] // tm, xs.shape[0] // tm
    t = (npt + nst) * tm
    row = lambda w: pl.BlockSpec((tm, w), lambda i: (i, 0))
    prow = lambda w: pl.BlockSpec((tm, w), lambda i: (jnp.minimum(i, npt - 1), 0))
    srow = lambda w: pl.BlockSpec((tm, w), lambda i: (jnp.maximum(i - npt, 0), 0))
    return pl.pallas_call(
        functools.partial(_mix_out_kernel, n_prompt_tiles=npt),
        grid=(npt + nst,),
        in_specs=[prow(D_MODEL), srow(D_MODEL), prow(D_GLA), srow(D_GLA), prow(D_S5), srow(D_S5),
                  _const_spec(glu_w.shape), _const_spec((1, D_S5)), _const_spec((1, D_S5)),
                  _const_spec(wo.shape), _const_spec((1, D_MODEL)), _const_spec(wr.shape)],
        out_specs=[row(D_MODEL), row(HALF), row(LANES), pl.BlockSpec((8, tm), lambda i: (0, i)),
                   pl.BlockSpec((1, LANES), lambda i: (0, 0))],
        out_shape=[jax.ShapeDtypeStruct((t, D_MODEL), F32), jax.ShapeDtypeStruct((t, HALF), jnp.uint32),
                   jax.ShapeDtypeStruct((t, LANES), F32), jax.ShapeDtypeStruct((8, t), F32),
                   jax.ShapeDtypeStruct((1, LANES), F32)],
        compiler_params=pltpu.CompilerParams(dimension_semantics=("arbitrary",), vmem_limit_bytes=VMEM_LIMIT),
        name="mix_out",
    )(xp, xs, ogp, ogs, ysp, yss, glu_w, glu_b, s5n, wo, nffn, wr)


DMA_UNROLL = 8


MOE_SB = 6
HALF = D_MODEL // 2


def _pack_bf16_pair(lo, hi):
    def bits(x):
        b = pltpu.bitcast(x, jnp.uint32)
        return (b + jnp.uint32(0x7FFF) + ((b >> 16) & jnp.uint32(1))) >> 16
    return bits(lo) | (bits(hi) << 16)


def _unpack_bf16_pair(u):
    return (pltpu.bitcast(u << 16, F32), pltpu.bitcast(u & jnp.uint32(0xFFFF0000), F32))


def _moe_kernel(se_ref, sm_ref, sn_ref, sbase_ref, xs_hbm, wg_ref, wu_ref, wd_ref, out_hbm,
                xb_ref, yb_ref, gsem, ssem):
    b = pl.program_id(0)
    nb = pl.num_programs(0)
    n = sn_ref[b]
    slot = lax.rem(b, 2)

    def issue_gather(blk, sl):
        base = pl.multiple_of(sbase_ref[blk], MOE_BLK)
        for m in range(1, MOE_SB + 1):
            @pl.when((sn_ref[blk] > 0) & (sm_ref[blk] == m))
            def _():
                rows = m * MOE_BLK
                pltpu.make_async_copy(xs_hbm.at[pl.ds(base, rows), :], xb_ref.at[sl, pl.ds(0, rows), :],
                                      gsem.at[sl]).start()

    def wait_gather(blocks, sl):
        rows = pl.ds(0, pl.multiple_of(blocks * MOE_BLK, MOE_BLK))
        pltpu.make_async_copy(xs_hbm.at[rows, :], xb_ref.at[sl, rows, :], gsem.at[sl]).wait()

    m_cur = sm_ref[b]
    m_prev = sm_ref[jnp.maximum(b - 1, 0)]
    has_prev = (b > 0) & (m_prev > 0)

    def out_copy(rows):
        dst = pl.ds(pl.multiple_of(sbase_ref[b], MOE_BLK), rows)
        return pltpu.make_async_copy(yb_ref.at[pl.ds(0, rows), :], out_hbm.at[dst, :], ssem.at[0])

    def wait_prev_out():
        rows = pl.ds(0, pl.multiple_of(m_prev * MOE_BLK, MOE_BLK))
        pltpu.make_async_copy(yb_ref.at[rows, :], out_hbm.at[rows, :], ssem.at[0]).wait()

    def write_out(rows):
        out_copy(rows).start()

        @pl.when(b == nb - 1)
        def _():
            out_copy(rows).wait()

    @pl.when(b == 0)
    def _():
        issue_gather(0, 0)

    @pl.when(b + 1 < nb)
    def _():
        issue_gather(b + 1, 1 - slot)

    def compute(rows):
        x_lo, x_hi = _unpack_bf16_pair(xb_ref[slot, 0:rows, :])
        x_lo, x_hi = x_lo.astype(BF16), x_hi.astype(BF16)

        def w(ref, r, c):
            return ref[r, c].astype(BF16)

        lo, hi, full = slice(0, HALF), slice(HALF, D_MODEL), slice(None)
        gate = _dot(x_lo, w(wg_ref, lo, full)) + _dot(x_hi, w(wg_ref, hi, full))
        up = _dot(x_lo, w(wu_ref, lo, full)) + _dot(x_hi, w(wu_ref, hi, full))
        hid = (gate * jax.nn.sigmoid(gate) * up).astype(BF16)

        @pl.when(has_prev)
        def _():
            wait_prev_out()

        yb_ref[0:rows, :] = _pack_bf16_pair(_dot(hid, w(wd_ref, full, lo)), _dot(hid, w(wd_ref, full, hi)))
        write_out(rows)

    @pl.when(n > 0)
    def _():
        wait_gather(m_cur, slot)
        for m in range(1, MOE_SB + 1):
            @pl.when(m_cur == m)
            def _():
                compute(m * MOE_BLK)

    @pl.when((n == 0) & (m_cur > 0))
    def _():
        @pl.when(has_prev)
        def _():
            wait_prev_out()

        yb_ref[...] = jnp.zeros_like(yb_ref)
        for m in range(1, MOE_SB + 1):
            @pl.when(m_cur == m)
            def _():
                write_out(m * MOE_BLK)

    @pl.when((m_cur == 0) & has_prev)
    def _():
        wait_prev_out()


def _moe(x_slots, plan, w_gate, w_up, w_down, n_out_rows):
    sb_e, sb_m, sb_n, sb_base = plan
    sb_rows = MOE_SB * MOE_BLK
    wspec = lambda s: pl.BlockSpec((None,) + s, lambda b, se, *_: (se[b], 0, 0))
    grid_spec = pltpu.PrefetchScalarGridSpec(
        num_scalar_prefetch=4,
        grid=(sb_e.shape[0],),
        in_specs=[pl.BlockSpec(memory_space=pl.ANY),
                  wspec((D_MODEL, D_EXPERT)), wspec((D_MODEL, D_EXPERT)), wspec((D_EXPERT, D_MODEL))],
        out_specs=pl.BlockSpec(memory_space=pl.ANY),
        scratch_shapes=[pltpu.VMEM((2, sb_rows, HALF), jnp.uint32), pltpu.VMEM((sb_rows, HALF), jnp.uint32),
                        pltpu.SemaphoreType.DMA((2,)), pltpu.SemaphoreType.DMA((1,))],
    )
    return pl.pallas_call(
        _moe_kernel,
        grid_spec=grid_spec,
        out_shape=jax.ShapeDtypeStruct((n_out_rows, HALF), jnp.uint32),
        compiler_params=pltpu.CompilerParams(dimension_semantics=("arbitrary",),
                                             vmem_limit_bytes=MOE_VMEM_LIMIT),
        name="moe_experts",
    )(sb_e, sb_m, sb_n, sb_base, x_slots, w_gate, w_up, w_down)


def _dispatch_kernel(dest_ref, zb_ref, hn_ref, out_hbm, buf_ref, zero_ref, sem, zsem):
    i = pl.program_id(0)
    nt = pl.num_programs(0)
    tm = hn_ref.shape[0]
    t_all = nt * tm
    slot = lax.rem(i, 2)
    nblk = zb_ref.shape[0]

    def zero_copy(blk):
        dst = pl.ds(pl.multiple_of(blk * MOE_BLK, MOE_BLK), MOE_BLK)
        return pltpu.make_async_copy(zero_ref, out_hbm.at[dst, :], zsem.at[0])

    @pl.when(i == 0)
    def _():
        zero_ref[...] = jnp.zeros_like(zero_ref)

        def start(blk, c):
            @pl.when(zb_ref[blk] == 1)
            def _():
                zero_copy(blk).start()
            return c

        def wait(blk, c):
            @pl.when(zb_ref[blk] == 1)
            def _():
                zero_copy(blk).wait()
            return c

        lax.fori_loop(0, nblk, start, 0)
        lax.fori_loop(0, nblk, wait, 0)

    def wait_rows(sl):
        for k in range(2):
            pltpu.make_async_copy(buf_ref.at[sl], out_hbm.at[pl.ds(0, tm), :], sem.at[sl]).wait()

    @pl.when(i >= 2)
    def _():
        wait_rows(slot)

    buf_ref[slot] = hn_ref[...]
    base = i * tm
    for a in range(2 * tm):
        row, k = a // 2, a % 2
        d = dest_ref[k * t_all + base + row]
        pltpu.make_async_copy(buf_ref.at[slot, pl.ds(row, 1), :], out_hbm.at[pl.ds(d, 1), :],
                              sem.at[slot]).start()

    @pl.when(i == nt - 1)
    def _():
        wait_rows(slot)

        @pl.when(nt > 1)
        def _():
            wait_rows(1 - slot)


def _dispatch(dest, zero_blk, hn_pk, n_slots):
    tm = TOK_TILE
    grid_spec = pltpu.PrefetchScalarGridSpec(
        num_scalar_prefetch=2,
        grid=(hn_pk.shape[0] // tm,),
        in_specs=[pl.BlockSpec((tm, HALF), lambda i, d, z: (i, 0))],
        out_specs=pl.BlockSpec(memory_space=pl.ANY),
        scratch_shapes=[pltpu.VMEM((2, tm, HALF), jnp.uint32), pltpu.VMEM((MOE_BLK, HALF), jnp.uint32),
                        pltpu.SemaphoreType.DMA((2,)), pltpu.SemaphoreType.DMA((1,))],
    )
    return pl.pallas_call(
        _dispatch_kernel,
        grid_spec=grid_spec,
        out_shape=jax.ShapeDtypeStruct((n_slots, HALF), jnp.uint32),
        compiler_params=pltpu.CompilerParams(dimension_semantics=("arbitrary",), vmem_limit_bytes=VMEM_LIMIT),
        name="moe_dispatch",
    )(dest, zero_blk, hn_pk)


def _moe_plan(rtt, cnt):
    t_all = rtt.shape[1]
    n_assign = 2 * t_all
    e_flat = rtt[0:2].astype(jnp.int32).reshape(-1)
    rank = rtt[4:6].astype(jnp.int32).reshape(-1)
    counts = cnt[0, 0:N_EXPERTS].astype(jnp.int32)
    padded = (counts + MOE_BLK - 1) // MOE_BLK * MOE_BLK
    pad_end = jnp.cumsum(padded)
    pad_start = pad_end - padded
    before = jnp.arange(N_EXPERTS, dtype=jnp.int32)[:, None] < e_flat[None, :]
    dest = rank + jnp.sum(jnp.where(before, padded[:, None], 0), axis=0)
    nblk = -(-(n_assign + N_EXPERTS * (MOE_BLK - 1)) // MOE_BLK)
    n_slots = nblk * MOE_BLK
    blk = jnp.arange(nblk, dtype=jnp.int32)
    last_of_expert = jnp.any((blk[:, None] == (pad_end // MOE_BLK - 1)[None, :]) & (counts > 0)[None, :], axis=1)
    zero_blk = (last_of_expert | (blk >= pad_end[-1] // MOE_BLK)).astype(jnp.int32)
    k_e = padded // MOE_BLK
    sbc = (k_e + MOE_SB - 1) // MOE_SB
    sb_end = jnp.cumsum(sbc)
    sb_start = sb_end - sbc
    n_sb = (nblk + (MOE_SB - 1) * N_EXPERTS) // MOE_SB
    s = jnp.arange(n_sb, dtype=jnp.int32)
    sb_e = jnp.minimum(jnp.sum((s[:, None] >= sb_end[None, :]).astype(jnp.int32), axis=1), N_EXPERTS - 1)
    j = s - sb_start[sb_e]
    real = s < sb_end[-1]
    tail_blk = pad_end[-1] // MOE_BLK + MOE_SB * (s - sb_end[-1])
    sb_m = jnp.where(real, jnp.clip(k_e[sb_e] - MOE_SB * j, 0, MOE_SB), jnp.clip(nblk - tail_blk, 0, MOE_SB))
    sb_n = jnp.where(real, jnp.clip(counts[sb_e] - MOE_SB * MOE_BLK * j, 0, MOE_SB * MOE_BLK), 0)
    sb_base = jnp.where(real, pad_start[sb_e] + MOE_SB * MOE_BLK * j, jnp.minimum(tail_blk, nblk - 1) * MOE_BLK)
    i32 = lambda a: a.astype(jnp.int32)
    return (i32(sb_e), i32(sb_m), i32(sb_n), i32(sb_base)), i32(dest), zero_blk, n_slots


def _ple_out_kernel(dest_ref, h1_ref, rt_ref, pp_ref, ps_ref, nple_ref, wpg_ref, wp_ref, nfin_ref, ys_hbm,
                    op_ref, os_ref, yb_ref, sem, *, n_prompt_tiles):
    i = pl.program_id(0)
    nt = pl.num_programs(0)
    tm = h1_ref.shape[0]
    t_all = nt * tm
    slot = lax.rem(i, 2)

    def issue(tile, sl):
        base = tile * tm

        def grp(g, c):
            for j in range(DMA_UNROLL):
                row = g * (DMA_UNROLL // 2) + j // 2
                d = dest_ref[(j % 2) * t_all + base + row]
                pltpu.make_async_copy(ys_hbm.at[pl.ds(d, 1), :], yb_ref.at[sl, j % 2, pl.ds(row, 1), :],
                                      sem.at[sl]).start()
            return c

        lax.fori_loop(0, 2 * tm // DMA_UNROLL, grp, 0)

    @pl.when(i == 0)
    def _():
        issue(0, 0)

    def wait_rows(sl):
        for k in range(2):
            pltpu.make_async_copy(ys_hbm.at[pl.ds(0, tm), :], yb_ref.at[sl, k], sem.at[sl]).wait()

    wait_rows(slot)
    is_p = i < n_prompt_tiles
    rt = rt_ref[...]
    w0, w1 = rt[:, 2:3], rt[:, 3:4]
    lo0, hi0 = _unpack_bf16_pair(yb_ref[slot, 0])
    lo1, hi1 = _unpack_bf16_pair(yb_ref[slot, 1])
    h2 = h1_ref[...] + jnp.concatenate([w0 * lo0 + w1 * lo1, w0 * hi0 + w1 * hi1], axis=1)
    hb = _rms(h2, nple_ref[...]).astype(BF16)
    nbase = jnp.minimum(i + 1, nt - 1) * tm
    n_chunk = 8
    cw, ca = D_MODEL // n_chunk, 2 * tm // n_chunk
    gates = []
    for c in range(n_chunk):
        gates.append(jax.nn.sigmoid(_dot(hb, wpg_ref[:, c * cw:(c + 1) * cw])))
        for a in range(c * ca, (c + 1) * ca):
            d = dest_ref[(a % 2) * t_all + nbase + a // 2]
            pltpu.make_async_copy(ys_hbm.at[pl.ds(d, 1), :], yb_ref.at[1 - slot, a % 2, pl.ds(a // 2, 1), :],
                                  sem.at[1 - slot]).start()
    gate = jnp.concatenate(gates, axis=1)

    @pl.when(i == nt - 1)
    def _():
        wait_rows(1 - slot)

    p = jnp.where(is_p, pp_ref[...], ps_ref[...])
    h3 = h2 + _dot(p.astype(BF16), wp_ref[...]) * gate
    y = _rms(h3, nfin_ref[...])

    @pl.when(is_p)
    def _():
        op_ref[...] = y

    @pl.when(jnp.logical_not(is_p))
    def _():
        os_ref[...] = y


def _ple_out(dest, h1, rt, pp, ps, nple, wpg, wp, nfin, y_slots):
    tm = TOK_TILE
    npt, nst = pp.shape[0] // tm, ps.shape[0] // tm
    row = lambda w: pl.BlockSpec((tm, w), lambda i, d: (i, 0))
    prow = lambda w: pl.BlockSpec((tm, w), lambda i, d: (jnp.minimum(i, npt - 1), 0))
    srow = lambda w: pl.BlockSpec((tm, w), lambda i, d: (jnp.maximum(i - npt, 0), 0))
    const = lambda shape: pl.BlockSpec(shape, lambda i, d: (0,) * len(shape), pipeline_mode=pl.Buffered(1))
    grid_spec = pltpu.PrefetchScalarGridSpec(
        num_scalar_prefetch=1,
        grid=(npt + nst,),
        in_specs=[row(D_MODEL), row(LANES), prow(D_PLE), srow(D_PLE), const((1, D_MODEL)), const(wpg.shape),
                  const(wp.shape), const((1, D_MODEL)), pl.BlockSpec(memory_space=pl.ANY)],
        out_specs=[prow(D_MODEL), srow(D_MODEL)],
        scratch_shapes=[pltpu.VMEM((2, 2, tm, HALF), jnp.uint32), pltpu.SemaphoreType.DMA((2,))],
    )
    return pl.pallas_call(
        functools.partial(_ple_out_kernel, n_prompt_tiles=npt),
        grid_spec=grid_spec,
        out_shape=[jax.ShapeDtypeStruct((pp.shape[0], D_MODEL), F32),
                   jax.ShapeDtypeStruct((ps.shape[0], D_MODEL), F32)],
        compiler_params=pltpu.CompilerParams(dimension_semantics=("arbitrary",), vmem_limit_bytes=VMEM_LIMIT),
        name="ple_out",
    )(dest, h1, rt, pp, ps, nple, wpg, wp, nfin, y_slots)


def _s5_params(lam_re, lam_im, log_dt, b_re, b_im, c_re, c_im, d_skip):
    dt = jnp.exp(log_dt)[:, None]
    mag = jnp.exp(lam_re * dt)
    ab_re = mag * jnp.cos(lam_im * dt)
    ab_im = mag * jnp.sin(lam_im * dt)
    den = lam_re * lam_re + lam_im * lam_im
    nr = ab_re - 1.0
    f_re = (nr * lam_re + ab_im * lam_im) / den
    f_im = (ab_im * lam_re - nr * lam_im) / den
    bb_re = f_re[..., None] * b_re - f_im[..., None] * b_im
    bb_im = f_re[..., None] * b_im + f_im[..., None] * b_re
    eye = jnp.eye(S5_GB, dtype=F32)

    def bu_w(bb):
        bb = bb.reshape(S5_NGB, S5_GB, S5_STATE, S5_GROUP)
        w = jnp.einsum('nlph,lm->nlhmp', bb, eye)
        return w.reshape(S5_NGB, S5_GB * S5_GROUP, S5_SL)

    def c_w(c):
        c = c.reshape(S5_NGB, S5_GB, S5_GROUP, S5_STATE)
        w = jnp.einsum('nlhp,lm->nlpmh', c, eye)
        return w.reshape(S5_NGB, S5_SL, S5_GB * S5_GROUP)

    wbu = jnp.concatenate([bu_w(bb_re), bu_w(bb_im)], axis=2).astype(BF16)
    wc = jnp.concatenate([c_w(c_re), -c_w(c_im)], axis=1).astype(BF16)
    a_re = ab_re.reshape(S5_NGB, 1, S5_SL)
    a_im = ab_im.reshape(S5_NGB, 1, S5_SL)
    dsk = d_skip.reshape(S5_NGB, 1, S5_GB * S5_GROUP)
    return wbu, wc, a_re, a_im, dsk


def kernel(x_prompt, x_sample, p_prompt, p_sample, state_gla, state_s5_re, state_s5_im, norm_mix, w_in, gla_w_gate_up, gla_gate_bias, gla_norm, s5_lam_re, s5_lam_im, s5_log_dt, s5_b_re, s5_b_im, s5_c_re, s5_c_im, s5_d, s5_glu_w, s5_glu_b, s5_norm, w_out, norm_ffn, router_group, router_expert, w_gate, w_up, w_down, norm_ple, w_ple, w_ple_gate, norm_final):
    depth = w_in.shape[0]
    assert depth == 1
    i = 0
    bp, lp, _ = x_prompt.shape
    bs, ls, _ = x_sample.shape
    tp, ts = bp * lp, bs * ls
    t_all = tp + ts

    wm = _w_in_prep(w_in[i])
    wgu = jnp.pad(gla_w_gate_up[i], ((0, LANES - GLA_RANK), (0, 0))).astype(BF16)
    gbias = gla_gate_bias[i].reshape(1, QK_W)
    wbu, wc, a_re, a_im, dsk = _s5_params(s5_lam_re[i], s5_lam_im[i], s5_log_dt[i], s5_b_re[i], s5_b_im[i],
                                          s5_c_re[i], s5_c_im[i], s5_d[i])
    glu_w = s5_glu_w[i].astype(BF16)
    wo = w_out[i].astype(BF16)
    wr32 = jnp.pad(jnp.concatenate([router_group[i], router_expert[i]], axis=1),
                   ((0, 0), (0, LANES - N_EGROUPS - N_EXPERTS)))
    wr_hi = wr32.astype(BF16)
    wr = jnp.stack([wr_hi, (wr32 - wr_hi.astype(F32)).astype(BF16)])
    wpg = w_ple_gate[i].astype(BF16)
    wp = w_ple[i].astype(BF16)
    vec = lambda a: a.reshape(1, -1)

    xp = x_prompt.reshape(tp, D_MODEL)
    xs = x_sample.reshape(ts, D_MODEL)

    qp, kp, vp, rp, lap, up = _in_proj(xp, vec(norm_mix[i]), wm, wgu, gbias, BF16)
    qs, ks, vs, rs, las, us = _in_proj(xs, vec(norm_mix[i]), wm, wgu, gbias, BF16)
    ogp, gla_p = _gla_prompt(qp, kp, vp, rp, lap, vec(gla_norm[i]), bp, lp)
    ogs, gla_s = _gla_sample(qs, ks, vs, rs, las, vec(gla_norm[i]), state_gla[i], bs, ls)
    wbu2, wc2, a_re2, a_im2 = _s5_prompt_params(wbu, wc, a_re, a_im, bp)
    ysp, re_p, im_p = _s5_prompt(up.reshape(bp, lp, D_S5), wbu2, wc2, a_re2, a_im2, dsk, tc=256)
    yss, re_s, im_s = _s5(us.reshape(bs, ls, D_S5), wbu, wc, a_re, a_im, dsk,
                          state_s5_re[i].reshape(bs, -1), state_s5_im[i].reshape(bs, -1),
                          nb=32, tc=ls, has_state=True)

    h1, hn_all, rt_all, rtt, cnt = _mix_out(xp, xs, ogp, ogs, ysp.reshape(tp, D_S5), yss.reshape(ts, D_S5),
                                            glu_w, vec(s5_glu_b[i]), vec(s5_norm[i]), wo, vec(norm_ffn[i]), wr)

    plan, dest, zero_blk, n_slots = _moe_plan(rtt, cnt)
    x_slots = _dispatch(dest, zero_blk, hn_all, n_slots)
    y_slots = _moe(x_slots, plan, w_gate[i], w_up[i], w_down[i], n_slots)

    y_p, y_s = _ple_out(dest, h1, rt_all, p_prompt[i].reshape(tp, D_PLE), p_sample[i].reshape(ts, D_PLE),
                        vec(norm_ple[i]), wpg, wp, vec(norm_final), y_slots)

    s5shape = lambda a, b: a.reshape(1, b, S5_GROUPS, S5_STATE)
    return (y_p.reshape(bp, lp, D_MODEL), y_s.reshape(bs, ls, D_MODEL),
            gla_p[None], s5shape(re_p, bp), s5shape(im_p, bp),
            gla_s[None], s5shape(re_s, bs), s5shape(im_s, bs))
```

```python
import functools
import math

import jax
import jax.numpy as jnp
from jax import lax
from jax.experimental import pallas as pl
from jax.experimental.pallas import tpu as pltpu

F32 = jnp.float32
BF16 = jnp.bfloat16

D_MODEL = 2048
D_GLA = 1024
D_S5 = 1024
GLA_HEADS = 4
GLA_DV = 256
GLA_DK = 128
GLA_RANK = 16
GLA_CHUNK = 64
S5_GROUP = 16
S5_GROUPS = 64
S5_STATE = 64
N_EGROUPS = 4
N_EPG = 8
N_EXPERTS = 32
D_EXPERT = 512
D_PLE = 256
EPS = 1e-6

LANES = 128
QK_W = GLA_HEADS * GLA_DK
S5_GB = 8
S5_NGB = S5_GROUPS // S5_GB
S5_SL = S5_GB * S5_STATE
TOK_TILE = 256
MOE_BLK = 128
VMEM_LIMIT = 56 * 1024 * 1024
MOE_VMEM_LIMIT = 60 * 1024 * 1024


def _const_spec(shape):
    nd = len(shape)
    return pl.BlockSpec(shape, lambda *_: (0,) * nd, pipeline_mode=pl.Buffered(1))


def _rms(x, g):
    return x * lax.rsqrt(jnp.mean(x * x, axis=-1, keepdims=True) + EPS) * g


def _dot(a, b):
    return jnp.dot(a, b, preferred_element_type=F32)


def _log_sigmoid(x):
    return -(jnp.maximum(-x, 0.0) + jnp.log1p(jnp.exp(-jnp.abs(x))))


N_QKVR = 2 * QK_W + 2 * D_GLA
W_IN_COLS = N_QKVR + GLA_RANK + D_S5


def _w_in_prep_kernel(w_ref, o_ref):
    o_ref[:, 0:N_QKVR] = w_ref[:, 0:N_QKVR].astype(BF16)
    tail = w_ref[:, N_QKVR:W_IN_COLS]
    o_ref[:, N_QKVR:N_QKVR + D_S5] = tail[:, GLA_RANK:GLA_RANK + D_S5].astype(BF16)
    o_ref[:, N_QKVR + D_S5:N_QKVR + D_S5 + LANES] = tail[:, 0:LANES].astype(BF16)


def _w_in_prep(wi):
    rows = 256
    return pl.pallas_call(
        _w_in_prep_kernel,
        grid=(D_MODEL // rows,),
        in_specs=[pl.BlockSpec((rows, W_IN_COLS), lambda i: (i, 0))],
        out_specs=pl.BlockSpec((rows, N_QKVR + D_S5 + LANES), lambda i: (i, 0)),
        out_shape=jax.ShapeDtypeStruct((D_MODEL, N_QKVR + D_S5 + LANES), BF16),
        compiler_params=pltpu.CompilerParams(dimension_semantics=("arbitrary",), vmem_limit_bytes=VMEM_LIMIT),
        name="w_in_prep",
    )(wi)


def _in_proj_kernel(x_ref, g_ref, wm_ref, wgu_ref, gb_ref,
                    q_ref, k_ref, v_ref, r_ref, la_ref, u_ref):
    hb = _rms(x_ref[...], g_ref[...]).astype(BF16)

    def seg(a, b):
        return _dot(hb, wm_ref[:, a:b])

    q_ref[...] = (seg(0, QK_W) * (GLA_DK ** -0.5)).astype(q_ref.dtype)
    k_ref[...] = seg(QK_W, 2 * QK_W).astype(k_ref.dtype)
    v_ref[...] = seg(2 * QK_W, 2 * QK_W + D_GLA).astype(v_ref.dtype)
    r_ref[...] = seg(2 * QK_W + D_GLA, N_QKVR).astype(r_ref.dtype)
    u_ref[...] = seg(N_QKVR, N_QKVR + D_S5)
    zg = seg(N_QKVR + D_S5, N_QKVR + D_S5 + LANES)
    xg = _dot(zg.astype(BF16), wgu_ref[...]) + gb_ref[...]
    la_ref[...] = _log_sigmoid(xg) * (1.0 / 16.0)


def _in_proj(x2d, g, wm, wgu, gbias, act_dtype):
    t = x2d.shape[0]
    tm = TOK_TILE
    row = lambda w: pl.BlockSpec((tm, w), lambda i: (i, 0))
    return pl.pallas_call(
        _in_proj_kernel,
        grid=(t // tm,),
        in_specs=[row(D_MODEL), _const_spec((1, D_MODEL)), _const_spec(wm.shape),
                  _const_spec(wgu.shape), _const_spec((1, QK_W))],
        out_specs=[row(QK_W), row(QK_W), row(D_GLA), row(D_GLA), row(QK_W), row(D_S5)],
        out_shape=[jax.ShapeDtypeStruct((t, QK_W), act_dtype), jax.ShapeDtypeStruct((t, QK_W), act_dtype),
                   jax.ShapeDtypeStruct((t, D_GLA), act_dtype), jax.ShapeDtypeStruct((t, D_GLA), act_dtype),
                   jax.ShapeDtypeStruct((t, QK_W), F32), jax.ShapeDtypeStruct((t, D_S5), F32)],
        compiler_params=pltpu.CompilerParams(dimension_semantics=("arbitrary",), vmem_limit_bytes=VMEM_LIMIT),
        name="in_proj",
    )(x2d, g, wm, wgu, gbias)


_NT = (((1,), (1,)), ((), ()))
_TN = (((0,), (0,)), ((), ()))


def _gla_pre(q, k, la, c):
    r = q.shape[0]
    shift = int(math.log2(c))
    ri = lax.broadcasted_iota(jnp.int32, (r, r), 0)
    si = lax.broadcasted_iota(jnp.int32, (r, r), 1)
    mask = ((ri >> shift) == (si >> shift)) & (ri >= si)
    tri = jnp.where(mask, 1.0, 0.0).astype(BF16)
    hi = la.astype(BF16)
    r1 = la - hi.astype(F32)
    mid = r1.astype(BF16)
    lo = (r1 - mid.astype(F32)).astype(BF16)
    cum = _dot(tri, hi) + _dot(tri, mid) + _dot(tri, lo)
    last = jnp.concatenate([jnp.broadcast_to(cum[(i + 1) * c - 1:(i + 1) * c, :], (c, cum.shape[1]))
                            for i in range(r // c)], axis=0)
    qe = (q * jnp.exp(cum)).astype(BF16)
    ke = (k * jnp.exp(-cum)).astype(BF16)
    kd = (k * jnp.exp(last - cum)).astype(BF16)
    return qe, ke, kd, cum, mask


def _gla_intra(qe, ke, v, mask):
    sc = lax.dot_general(qe, ke, _NT, preferred_element_type=F32)
    return _dot(jnp.where(mask, sc, 0.0).astype(BF16), v)


def _gla_finish(o, r, g):
    rf = r.astype(F32)
    return _rms(o, g) * (rf * jax.nn.sigmoid(rf))


GLA_SAFE_LOG_DECAY = -60.0


def _gla_token_step(t, rows, q, k, v, a_all, st_ref, oacc_ref):
    m = rows == t
    a = jnp.sum(jnp.where(m, a_all, 0.0), axis=0, keepdims=True)
    kt = jnp.where(m, k, 0.0).astype(BF16)
    qt = jnp.where(m, q, 0.0).astype(BF16)
    vt = jnp.where(m, v, jnp.zeros_like(v))
    st = st_ref[...] * a + lax.dot_general(vt, kt, _TN, preferred_element_type=F32)
    st_ref[...] = st
    oacc_ref[...] += lax.dot_general(qt, st.astype(BF16), _NT, preferred_element_type=F32)


def _gla_prompt_kernel(q_ref, k_ref, v_ref, r_ref, la_ref, g_ref, o_ref, sfin_ref, st_ref, oacc_ref,
                       *, n_chunks):
    j = pl.program_id(1)

    @pl.when(j == 0)
    def _():
        st_ref[...] = jnp.zeros_like(st_ref)

    c = GLA_CHUNK
    qe, ke, kd, cum, mask = _gla_pre(q_ref[...].astype(F32), k_ref[...].astype(F32), la_ref[...], c)
    g = g_ref[...]
    safe = jnp.min(cum) >= GLA_SAFE_LOG_DECAY

    @pl.when(safe)
    def _():
        for h in range(GLA_HEADS):
            kc = slice(h * GLA_DK, (h + 1) * GLA_DK)
            vc = slice(h * GLA_DV, (h + 1) * GLA_DV)
            v = v_ref[:, vc]
            qe_h, kd_h = qe[:, kc], kd[:, kc]
            o = _gla_intra(qe_h, ke[:, kc], v, mask)
            st = st_ref[h]
            inter = []
            for ci in range(n_chunks):
                rows = slice(ci * c, (ci + 1) * c)
                inter.append(lax.dot_general(qe_h[rows], st.astype(BF16), _NT, preferred_element_type=F32))
                dec = jnp.exp(cum[(ci + 1) * c - 1:(ci + 1) * c, kc])
                st = st * dec + lax.dot_general(v[rows], kd_h[rows], _TN, preferred_element_type=F32)
            st_ref[h] = st
            o = o + jnp.concatenate(inter, axis=0)
            o_ref[:, vc] = _gla_finish(o, r_ref[:, vc], g).astype(o_ref.dtype)

    @pl.when(jnp.logical_not(safe))
    def _():
        n_rows = n_chunks * c
        rows = lax.broadcasted_iota(jnp.int32, (n_rows, 1), 0)
        for h in range(GLA_HEADS):
            kc = slice(h * GLA_DK, (h + 1) * GLA_DK)
            vc = slice(h * GLA_DV, (h + 1) * GLA_DV)
            q, k, v = q_ref[:, kc].astype(F32), k_ref[:, kc].astype(F32), v_ref[:, vc]
            a_all = jnp.exp(la_ref[:, kc])
            oacc_ref[...] = jnp.zeros_like(oacc_ref)

            def step(t, carry):
                _gla_token_step(t, rows, q, k, v, a_all, st_ref.at[h], oacc_ref)
                return carry

            lax.fori_loop(0, n_rows, step, 0)
            o_ref[:, vc] = _gla_finish(oacc_ref[...], r_ref[:, vc], g).astype(o_ref.dtype)

    @pl.when(j == pl.num_programs(1) - 1)
    def _():
        for h in range(GLA_HEADS):
            sfin_ref[0, h] = st_ref[h].T


def _gla_prompt(q, k, v, r, la, g, batch, seq):
    rb = 4 * GLA_CHUNK
    nj = seq // rb
    row = lambda w: pl.BlockSpec((rb, w), lambda b, j: (b * nj + j, 0))
    return pl.pallas_call(
        functools.partial(_gla_prompt_kernel, n_chunks=rb // GLA_CHUNK),
        grid=(batch, nj),
        in_specs=[row(QK_W), row(QK_W), row(D_GLA), row(D_GLA), row(QK_W), _const_spec((1, GLA_DV))],
        out_specs=[row(D_GLA),
                   pl.BlockSpec((1, GLA_HEADS, GLA_DK, GLA_DV), lambda b, j: (b, 0, 0, 0))],
        out_shape=[jax.ShapeDtypeStruct((batch * seq, D_GLA), BF16),
                   jax.ShapeDtypeStruct((batch, GLA_HEADS, GLA_DK, GLA_DV), F32)],
        scratch_shapes=[pltpu.VMEM((GLA_HEADS, GLA_DV, GLA_DK), F32), pltpu.VMEM((rb, GLA_DV), F32)],
        compiler_params=pltpu.CompilerParams(dimension_semantics=("arbitrary", "arbitrary"),
                                             vmem_limit_bytes=VMEM_LIMIT),
        name="gla_prompt",
    )(q, k, v, r, la, g)


def _gla_sample_kernel(q_ref, k_ref, v_ref, r_ref, la_ref, g_ref, s0_ref, o_ref, sfin_ref, st_ref, oacc_ref,
                       *, n_seq, seq):
    qe, ke, kd, cum, mask = _gla_pre(q_ref[...].astype(F32), k_ref[...].astype(F32), la_ref[...], seq)
    g = g_ref[...]
    safe = jnp.min(cum) >= GLA_SAFE_LOG_DECAY

    @pl.when(safe)
    def _():
        lasts = jnp.concatenate([cum[(s + 1) * seq - 1:(s + 1) * seq, :] for s in range(n_seq)]
                                + [jnp.zeros((GLA_DK - n_seq, cum.shape[1]), F32)], axis=0)
        pair = 2 * seq
        upper = lax.broadcasted_iota(jnp.int32, (pair, GLA_DK), 0) < seq
        for h in range(GLA_HEADS):
            kc = slice(h * GLA_DK, (h + 1) * GLA_DK)
            vc = slice(h * GLA_DV, (h + 1) * GLA_DV)
            v = v_ref[:, vc]
            qe_h, kd_h = qe[:, kc], kd[:, kc]
            o = _gla_intra(qe_h, ke[:, kc], v, mask)
            dec_t = jnp.exp(lasts[:, kc].T)
            inter = []
            for p in range(n_seq // 2):
                rows = slice(p * pair, (p + 1) * pair)
                qe_p, kd_p, v_p = qe_h[rows], kd_h[rows], v[rows]
                for half in range(2):
                    s = 2 * p + half
                    s0 = s0_ref[s, h]
                    o_s = _dot(qe_p, s0.astype(BF16))
                    inter.append(o_s[half * seq:(half + 1) * seq])
                    kd_s = jnp.where(upper if half == 0 else ~upper, kd_p, jnp.zeros_like(kd_p))
                    dec = jnp.broadcast_to(dec_t[:, s:s + 1], (GLA_DK, GLA_DV))
                    sfin_ref[s, h] = s0 * dec + lax.dot_general(kd_s, v_p, _TN, preferred_element_type=F32)
            o = o + jnp.concatenate(inter, axis=0)
            o_ref[:, vc] = _gla_finish(o, r_ref[:, vc], g).astype(o_ref.dtype)

    @pl.when(jnp.logical_not(safe))
    def _():
        n_rows = n_seq * seq
        rows = lax.broadcasted_iota(jnp.int32, (n_rows, 1), 0)
        for h in range(GLA_HEADS):
            kc = slice(h * GLA_DK, (h + 1) * GLA_DK)
            vc = slice(h * GLA_DV, (h + 1) * GLA_DV)
            q, k, v = q_ref[:, kc].astype(F32), k_ref[:, kc].astype(F32), v_ref[:, vc]
            a_all = jnp.exp(la_ref[:, kc])
            oacc_ref[...] = jnp.zeros_like(oacc_ref)

            def step(t, carry):
                s = t // seq

                @pl.when(t % seq == 0)
                def _():
                    st_ref[...] = s0_ref[s, h].T

                _gla_token_step(t, rows, q, k, v, a_all, st_ref, oacc_ref)

                @pl.when(t % seq == seq - 1)
                def _():
                    sfin_ref[s, h] = st_ref[...].T

                return carry

            lax.fori_loop(0, n_rows, step, 0)
            o_ref[:, vc] = _gla_finish(oacc_ref[...], r_ref[:, vc], g).astype(o_ref.dtype)


def _gla_sample(q, k, v, r, la, g, s0, batch, seq):
    ns = 16
    rb = ns * seq
    row = lambda w: pl.BlockSpec((rb, w), lambda i: (i, 0))
    st = pl.BlockSpec((ns, GLA_HEADS, GLA_DK, GLA_DV), lambda i: (i, 0, 0, 0))
    return pl.pallas_call(
        functools.partial(_gla_sample_kernel, n_seq=ns, seq=seq),
        grid=(batch // ns,),
        in_specs=[row(QK_W), row(QK_W), row(D_GLA), row(D_GLA), row(QK_W), _const_spec((1, GLA_DV)), st],
        out_specs=[row(D_GLA), st],
        out_shape=[jax.ShapeDtypeStruct((batch * seq, D_GLA), BF16),
                   jax.ShapeDtypeStruct((batch, GLA_HEADS, GLA_DK, GLA_DV), F32)],
        scratch_shapes=[pltpu.VMEM((GLA_DV, GLA_DK), F32), pltpu.VMEM((rb, GLA_DV), F32)],
        compiler_params=pltpu.CompilerParams(dimension_semantics=("arbitrary",), vmem_limit_bytes=VMEM_LIMIT),
        name="gla_sample",
    )(q, k, v, r, la, g, s0)


def _s5_kernel(u_ref, wbu_ref, wc_ref, are_ref, aim_ref, d_ref, h0r_ref, h0i_ref,
               y_ref, sre_ref, sim_ref, bu_ref, xs_ref, car_ref, *, nb, tc, has_state):
    j = pl.program_id(2)

    @pl.when(j == 0)
    def _():
        if has_state:
            car_ref[0] = h0r_ref[...]
            car_ref[1] = h0i_ref[...]
        else:
            car_ref[...] = jnp.zeros_like(car_ref)

    u2 = u_ref[...].reshape(nb * tc, LANES)
    ub = u2.astype(BF16)
    nl = S5_SL // LANES
    for l in range(2 * nl):
        bu_ref[l] = _dot(ub, wbu_ref[0, :, l * LANES:(l + 1) * LANES])
    a_r = [jnp.broadcast_to(are_ref[0, :, l * LANES:(l + 1) * LANES], (nb, LANES)) for l in range(nl)]
    a_i = [jnp.broadcast_to(aim_ref[0, :, l * LANES:(l + 1) * LANES], (nb, LANES)) for l in range(nl)]

    def step(t, carry):
        rows = pl.ds(t, nb, stride=tc)
        out = []
        for l in range(nl):
            xr, xi = carry[2 * l], carry[2 * l + 1]
            nr = a_r[l] * xr - a_i[l] * xi + bu_ref[l, rows, :]
            ni = a_r[l] * xi + a_i[l] * xr + bu_ref[nl + l, rows, :]
            xs_ref[l, rows, :] = nr
            xs_ref[nl + l, rows, :] = ni
            out += [nr, ni]
        return tuple(out)

    init = []
    for l in range(nl):
        init += [car_ref[0, :, l * LANES:(l + 1) * LANES], car_ref[1, :, l * LANES:(l + 1) * LANES]]
    fin = lax.fori_loop(0, tc, step, tuple(init), unroll=8)
    xr = jnp.concatenate([fin[2 * l] for l in range(nl)], axis=1)
    xi = jnp.concatenate([fin[2 * l + 1] for l in range(nl)], axis=1)
    car_ref[0] = xr
    car_ref[1] = xi
    y = d_ref[0] * u2
    for l in range(2 * nl):
        y = y + _dot(xs_ref[l].astype(BF16), wc_ref[0, l * LANES:(l + 1) * LANES, :])
    y_ref[...] = y.reshape(nb, tc, LANES)

    @pl.when(j == pl.num_programs(2) - 1)
    def _():
        sre_ref[...] = xr
        sim_ref[...] = xi


def _s5(u3d, wbu, wc, a_re, a_im, dsk, h0r, h0i, nb, tc, has_state):
    batch, seq, _ = u3d.shape
    grid = (S5_NGB, batch // nb, seq // tc)
    st = pl.BlockSpec((nb, S5_SL), lambda g, b, j: (b, g))
    par = lambda w: pl.BlockSpec((1, 1, w), lambda g, b, j: (g, 0, 0))
    ublk = pl.BlockSpec((nb, tc, LANES), lambda g, b, j: (b, j, g))
    return pl.pallas_call(
        functools.partial(_s5_kernel, nb=nb, tc=tc, has_state=has_state),
        grid=grid,
        in_specs=[ublk,
                  pl.BlockSpec((1, LANES, 2 * S5_SL), lambda g, b, j: (g, 0, 0)),
                  pl.BlockSpec((1, 2 * S5_SL, LANES), lambda g, b, j: (g, 0, 0)),
                  par(S5_SL), par(S5_SL), par(LANES), st, st],
        out_specs=[ublk, st, st],
        out_shape=[jax.ShapeDtypeStruct(u3d.shape, F32),
                   jax.ShapeDtypeStruct((batch, S5_GROUPS * S5_STATE), F32),
                   jax.ShapeDtypeStruct((batch, S5_GROUPS * S5_STATE), F32)],
        scratch_shapes=[pltpu.VMEM((2 * S5_SL // LANES, nb * tc, LANES), F32),
                        pltpu.VMEM((2 * S5_SL // LANES, nb * tc, LANES), F32),
                        pltpu.VMEM((2, nb, S5_SL), F32)],
        compiler_params=pltpu.CompilerParams(dimension_semantics=("arbitrary",) * 3,
                                             vmem_limit_bytes=VMEM_LIMIT),
        name="s5_state" if has_state else "s5_zero",
    )(u3d, wbu, wc, a_re, a_im, dsk, h0r, h0i)


S5_J = 2
S5_HL = S5_SL // S5_J
S5_PAIR = 2


def _s5_prompt_kernel(u_ref, wbu_ref, wc_ref, are_ref, aim_ref, d_ref, y_ref, sre_ref, sim_ref,
                      u2_ref, lhs_ref, bu_ref, xs_ref, y2_ref, yo_ref, car_ref, *, nb, tc):
    g = pl.program_id(0)
    j = pl.program_id(1)
    rows = nb * S5_J
    npair = S5_PAIR

    @pl.when((g == 0) & (j == 0))
    def _():
        lhs_ref[...] = jnp.zeros_like(lhs_ref)

    @pl.when(j == 0)
    def _():
        car_ref[...] = jnp.zeros_like(car_ref)

    u2 = u_ref[...].reshape(nb * tc, npair * LANES)
    for p in range(npair):
        u2_ref[p] = u2[:, p * LANES:(p + 1) * LANES]

    def build(t, c):
        for p in range(npair):
            u4 = u2_ref[p, pl.ds(t, nb, stride=tc), :]
            for jj in range(S5_J):
                lhs_ref[p, t, jj * nb:(jj + 1) * nb, jj * LANES:(jj + 1) * LANES] = u4
        return c

    lax.fori_loop(0, tc, build, 0, unroll=True)
    for p in range(npair):
        lhs = lhs_ref[p].reshape(tc * rows, S5_J * LANES).astype(BF16)
        bu_ref[p] = _dot(lhs, wbu_ref[p]).reshape(tc, rows, 2 * S5_HL)
    ar = [are_ref[p] for p in range(npair)]
    ai = [aim_ref[p] for p in range(npair)]

    def step(t, carry):
        out = []
        for p in range(npair):
            xr, xi = carry[2 * p], carry[2 * p + 1]
            tile = bu_ref[p, t]
            nr = ar[p] * xr - ai[p] * xi + tile[:, 0:S5_HL]
            ni = ar[p] * xi + ai[p] * xr + tile[:, S5_HL:2 * S5_HL]
            xs_ref[p, t] = jnp.concatenate([nr, ni], axis=1)
            out += [nr, ni]
        return tuple(out)

    init = tuple(car_ref[p, c] for p in range(npair) for c in range(2))
    fin = lax.fori_loop(0, tc, step, init, unroll=True)
    for p in range(npair):
        car_ref[p, 0] = fin[2 * p]
        car_ref[p, 1] = fin[2 * p + 1]
        xs = xs_ref[p].reshape(tc * rows, 2 * S5_HL).astype(BF16)
        y2_ref[p] = _dot(xs, wc_ref[p]).reshape(tc, rows, S5_J * LANES)
    first_half = lax.broadcasted_iota(jnp.int32, (rows, LANES), 0) < nb

    def unperm(t, c):
        for p in range(npair):
            t2 = y2_ref[p, t]
            part = jnp.where(first_half, t2[:, 0:LANES], t2[:, LANES:2 * LANES])
            yo_ref[p, pl.ds(t, nb, stride=tc), :] = (part + pltpu.roll(part, nb, axis=0))[0:nb]
        return c

    lax.fori_loop(0, tc, unperm, 0, unroll=True)
    y = jnp.concatenate([yo_ref[p] + d_ref[p] * u2_ref[p] for p in range(npair)], axis=1)
    y_ref[...] = y.reshape(nb, tc, npair * LANES)

    @pl.when(j == pl.num_programs(1) - 1)
    def _():
        def gather_state(x):
            return jnp.concatenate([x[jj * nb:(jj + 1) * nb] for jj in range(S5_J)], axis=1)

        sre_ref[...] = jnp.concatenate([gather_state(fin[2 * p]) for p in range(npair)], axis=1)
        sim_ref[...] = jnp.concatenate([gather_state(fin[2 * p + 1]) for p in range(npair)], axis=1)


def _s5_prompt(u3d, wbu2, wc2, a_re2, a_im2, dsk, tc):
    nb, seq, _ = u3d.shape
    rows = nb * S5_J
    assert rows == 8
    np_ = S5_PAIR
    st = pl.BlockSpec((nb, np_ * S5_SL), lambda g, j: (0, g))
    gblk = lambda s: pl.BlockSpec((np_,) + s, lambda g, j: (g, 0, 0))
    ublk = pl.BlockSpec((nb, tc, np_ * LANES), lambda g, j: (0, j, g))
    return pl.pallas_call(
        functools.partial(_s5_prompt_kernel, nb=nb, tc=tc),
        grid=(S5_NGB // np_, seq // tc),
        in_specs=[ublk, gblk((S5_J * LANES, 2 * S5_HL)), gblk((2 * S5_HL, S5_J * LANES)),
                  gblk((rows, S5_HL)), gblk((rows, S5_HL)), gblk((1, LANES))],
        out_specs=[ublk, st, st],
        out_shape=[jax.ShapeDtypeStruct(u3d.shape, F32),
                   jax.ShapeDtypeStruct((nb, S5_GROUPS * S5_STATE), F32),
                   jax.ShapeDtypeStruct((nb, S5_GROUPS * S5_STATE), F32)],
        scratch_shapes=[pltpu.VMEM((np_, nb * tc, LANES), F32),
                        pltpu.VMEM((np_, tc, rows, S5_J * LANES), F32),
                        pltpu.VMEM((np_, tc, rows, 2 * S5_HL), F32),
                        pltpu.VMEM((np_, tc, rows, 2 * S5_HL), F32),
                        pltpu.VMEM((np_, tc, rows, S5_J * LANES), F32),
                        pltpu.VMEM((np_, nb * tc, LANES), F32),
                        pltpu.VMEM((np_, 2, rows, S5_HL), F32)],
        compiler_params=pltpu.CompilerParams(dimension_semantics=("arbitrary",) * 2,
                                             vmem_limit_bytes=VMEM_LIMIT),
        name="s5_prompt",
    )(u3d, wbu2, wc2, a_re2, a_im2, dsk)


def _s5_prompt_params(wbu, wc, a_re, a_im, nb):
    h = S5_HL
    top = jnp.concatenate([wbu[:, :, 0:h], wbu[:, :, S5_SL:S5_SL + h]], axis=2)
    bot = jnp.concatenate([wbu[:, :, h:2 * h], wbu[:, :, S5_SL + h:S5_SL + 2 * h]], axis=2)
    wbu2 = jnp.concatenate([top, bot], axis=1)
    wc_j = [jnp.concatenate([wc[:, jj * h:(jj + 1) * h, :], wc[:, S5_SL + jj * h:S5_SL + (jj + 1) * h, :]],
                            axis=1) for jj in range(S5_J)]
    wc2 = jnp.concatenate(wc_j, axis=2)
    tile = lambda a: jnp.repeat(a.reshape(S5_NGB, S5_J, h), nb, axis=1)
    return wbu2, wc2, tile(a_re), tile(a_im)


def _mix_out_kernel(xp_ref, xs_ref, ogp_ref, ogs_ref, ysp_ref, yss_ref, glu_w_ref, glu_b_ref, s5n_ref, wo_ref,
                    nffn_ref, wr_ref, h1_ref, hn_ref, rt_ref, rtt_ref, cnt_ref, *, n_prompt_tiles):
    is_p = pl.program_id(0) < n_prompt_tiles
    x = jnp.where(is_p, xp_ref[...], xs_ref[...])
    og = jnp.where(is_p, ogp_ref[...], ogs_ref[...])
    y = jax.nn.gelu(jnp.where(is_p, ysp_ref[...], yss_ref[...]))
    y = y * jax.nn.sigmoid(_dot(y.astype(BF16), glu_w_ref[...]) + glu_b_ref[...])
    yn = _rms(y, s5n_ref[...]).astype(BF16)
    mix = _dot(og, wo_ref[0:D_GLA, :]) + _dot(yn, wo_ref[D_GLA:D_GLA + D_S5, :])
    h1 = x + mix
    h1_ref[...] = h1
    hn = _rms(h1, nffn_ref[...])
    hn_ref[...] = _pack_bf16_pair(hn[:, 0:HALF], hn[:, HALF:D_MODEL])
    hn_hi = hn.astype(BF16)
    hn_lo = (hn - hn_hi.astype(F32)).astype(BF16)
    logits = _dot(hn_hi, wr_ref[0]) + _dot(hn_hi, wr_ref[1]) + _dot(hn_lo, wr_ref[0])
    rt = _route(logits)

    @pl.when(pl.program_id(0) == 0)
    def _():
        cnt_ref[...] = jnp.zeros_like(cnt_ref)

    tm = rt.shape[0]
    lane = lax.broadcasted_iota(jnp.int32, rt.shape, 1).astype(F32)
    oh0 = lane == rt[:, 0:1]
    oh1 = lane == rt[:, 1:2]
    both = jnp.where(oh0 | oh1, 1.0, 0.0)
    ri = lax.broadcasted_iota(jnp.int32, (tm, tm), 0)
    ci = lax.broadcasted_iota(jnp.int32, (tm, tm), 1)
    before = _dot(jnp.where(ri > ci, 1.0, 0.0).astype(BF16), both.astype(BF16)) + cnt_ref[...]
    rank0 = jnp.sum(jnp.where(oh0, before, 0.0), axis=-1, keepdims=True)
    rank1 = jnp.sum(jnp.where(oh1, before, 0.0), axis=-1, keepdims=True)
    cnt_ref[...] += jnp.sum(both, axis=0, keepdims=True)
    rt = jnp.where(lane == 4.0, rank0, jnp.where(lane == 5.0, rank1, rt))
    rt_ref[...] = rt
    rtt_ref[...] = rt.T[0:8, :]


def _route(logits):
    col = lax.broadcasted_iota(jnp.int32, logits.shape, 1)
    colf = col.astype(F32)
    neg = -jnp.inf

    def first_argmax(vals):
        m = jnp.max(vals, axis=-1, keepdims=True)
        idx = jnp.min(jnp.where(vals == m, colf, float(LANES)), axis=-1, keepdims=True)
        return m, idx

    lg = jnp.where(col < N_EGROUPS, logits, neg)
    gmax, gsel = first_argmax(lg)
    p_g = 1.0 / jnp.sum(jnp.exp(lg - gmax), axis=-1, keepdims=True)
    ecol = col - N_EGROUPS
    egrp = (ecol >> int(math.log2(N_EPG))).astype(F32)
    in_group = (ecol >= 0) & (ecol < N_EXPERTS) & (egrp == gsel)
    le = jnp.where(in_group, logits, neg)
    m1, i1 = first_argmax(le)
    le2 = jnp.where(colf == i1, neg, le)
    m2, i2 = first_argmax(le2)
    e2 = jnp.exp(m2 - m1)
    den = 1.0 + e2
    w1 = p_g * (1.0 / den)
    w2 = p_g * (e2 / den)
    e1f = i1 - float(N_EGROUPS)
    e2f = i2 - float(N_EGROUPS)
    out = jnp.where(col == 0, e1f, jnp.where(col == 1, e2f, jnp.where(col == 2, w1, jnp.where(col == 3, w2, 0.0))))
    return out


def _mix_out(xp, xs, ogp, ogs, ysp, yss, glu_w, glu_b, s5n, wo, nffn, wr):
    tm = TOK_TILE
    npt, nst = xp.shape[0] // tm, xs.shape[0] // tm
    t = (npt + nst) * tm
    row = lambda w: pl.BlockSpec((tm, w), lambda i: (i, 0))
    prow = lambda w: pl.BlockSpec((tm, w), lambda i: (jnp.minimum(i, npt - 1), 0))
    srow = lambda w: pl.BlockSpec((tm, w), lambda i: (jnp.maximum(i - npt, 0), 0))
    return pl.pallas_call(
        functools.partial(_mix_out_kernel, n_prompt_tiles=npt),
        grid=(npt + nst,),
        in_specs=[prow(D_MODEL), srow(D_MODEL), prow(D_GLA), srow(D_GLA), prow(D_S5), srow(D_S5),
                  _const_spec(glu_w.shape), _const_spec((1, D_S5)), _const_spec((1, D_S5)),
                  _const_spec(wo.shape), _const_spec((1, D_MODEL)), _const_spec(wr.shape)],
        out_specs=[row(D_MODEL), row(HALF), row(LANES), pl.BlockSpec((8, tm), lambda i: (0, i)),
                   pl.BlockSpec((1, LANES), lambda i: (0, 0))],
        out_shape=[jax.ShapeDtypeStruct((t, D_MODEL), F32), jax.ShapeDtypeStruct((t, HALF), jnp.uint32),
                   jax.ShapeDtypeStruct((t, LANES), F32), jax.ShapeDtypeStruct((8, t), F32),
                   jax.ShapeDtypeStruct((1, LANES), F32)],
        compiler_params=pltpu.CompilerParams(dimension_semantics=("arbitrary",), vmem_limit_bytes=VMEM_LIMIT),
        name="mix_out",
    )(xp, xs, ogp, ogs, ysp, yss, glu_w, glu_b, s5n, wo, nffn, wr)


DMA_UNROLL = 8


MOE_SB = 6
HALF = D_MODEL // 2


def _pack_bf16_pair(lo, hi):
    def bits(x):
        b = pltpu.bitcast(x, jnp.uint32)
        return (b + jnp.uint32(0x7FFF) + ((b >> 16) & jnp.uint32(1))) >> 16
    return bits(lo) | (bits(hi) << 16)


def _unpack_bf16_pair(u):
    return (pltpu.bitcast(u << 16, F32), pltpu.bitcast(u & jnp.uint32(0xFFFF0000), F32))


def _moe_kernel(se_ref, sm_ref, sn_ref, sbase_ref, xs_hbm, wg_ref, wu_ref, wd_ref, out_hbm,
                xb_ref, yb_ref, gsem, ssem):
    b = pl.program_id(0)
    nb = pl.num_programs(0)
    n = sn_ref[b]
    slot = lax.rem(b, 2)

    def issue_gather(blk, sl):
        base = pl.multiple_of(sbase_ref[blk], MOE_BLK)
        for m in range(1, MOE_SB + 1):
            @pl.when((sn_ref[blk] > 0) & (sm_ref[blk] == m))
            def _():
                rows = m * MOE_BLK
                pltpu.make_async_copy(xs_hbm.at[pl.ds(base, rows), :], xb_ref.at[sl, pl.ds(0, rows), :],
                                      gsem.at[sl]).start()

    def wait_gather(blocks, sl):
        rows = pl.ds(0, pl.multiple_of(blocks * MOE_BLK, MOE_BLK))
        pltpu.make_async_copy(xs_hbm.at[rows, :], xb_ref.at[sl, rows, :], gsem.at[sl]).wait()

    m_cur = sm_ref[b]
    m_prev = sm_ref[jnp.maximum(b - 1, 0)]
    has_prev = (b > 0) & (m_prev > 0)

    def out_copy(rows):
        dst = pl.ds(pl.multiple_of(sbase_ref[b], MOE_BLK), rows)
        return pltpu.make_async_copy(yb_ref.at[pl.ds(0, rows), :], out_hbm.at[dst, :], ssem.at[0])

    def wait_prev_out():
        rows = pl.ds(0, pl.multiple_of(m_prev * MOE_BLK, MOE_BLK))
        pltpu.make_async_copy(yb_ref.at[rows, :], out_hbm.at[rows, :], ssem.at[0]).wait()

    def write_out(rows):
        out_copy(rows).start()

        @pl.when(b == nb - 1)
        def _():
            out_copy(rows).wait()

    @pl.when(b == 0)
    def _():
        issue_gather(0, 0)

    @pl.when(b + 1 < nb)
    def _():
        issue_gather(b + 1, 1 - slot)

    def compute(rows):
        x_lo, x_hi = _unpack_bf16_pair(xb_ref[slot, 0:rows, :])
        x_lo, x_hi = x_lo.astype(BF16), x_hi.astype(BF16)

        def w(ref, r, c):
            return ref[r, c].astype(BF16)

        lo, hi, full = slice(0, HALF), slice(HALF, D_MODEL), slice(None)
        gate = _dot(x_lo, w(wg_ref, lo, full)) + _dot(x_hi, w(wg_ref, hi, full))
        up = _dot(x_lo, w(wu_ref, lo, full)) + _dot(x_hi, w(wu_ref, hi, full))
        hid = (gate * jax.nn.sigmoid(gate) * up).astype(BF16)

        @pl.when(has_prev)
        def _():
            wait_prev_out()

        yb_ref[0:rows, :] = _pack_bf16_pair(_dot(hid, w(wd_ref, full, lo)), _dot(hid, w(wd_ref, full, hi)))
        write_out(rows)

    @pl.when(n > 0)
    def _():
        wait_gather(m_cur, slot)
        for m in range(1, MOE_SB + 1):
            @pl.when(m_cur == m)
            def _():
                compute(m * MOE_BLK)

    @pl.when((n == 0) & (m_cur > 0))
    def _():
        @pl.when(has_prev)
        def _():
            wait_prev_out()

        yb_ref[...] = jnp.zeros_like(yb_ref)
        for m in range(1, MOE_SB + 1):
            @pl.when(m_cur == m)
            def _():
                write_out(m * MOE_BLK)

    @pl.when((m_cur == 0) & has_prev)
    def _():
        wait_prev_out()


def _moe(x_slots, plan, w_gate, w_up, w_down, n_out_rows):
    sb_e, sb_m, sb_n, sb_base = plan
    sb_rows = MOE_SB * MOE_BLK
    wspec = lambda s: pl.BlockSpec((None,) + s, lambda b, se, *_: (se[b], 0, 0))
    grid_spec = pltpu.PrefetchScalarGridSpec(
        num_scalar_prefetch=4,
        grid=(sb_e.shape[0],),
        in_specs=[pl.BlockSpec(memory_space=pl.ANY),
                  wspec((D_MODEL, D_EXPERT)), wspec((D_MODEL, D_EXPERT)), wspec((D_EXPERT, D_MODEL))],
        out_specs=pl.BlockSpec(memory_space=pl.ANY),
        scratch_shapes=[pltpu.VMEM((2, sb_rows, HALF), jnp.uint32), pltpu.VMEM((sb_rows, HALF), jnp.uint32),
                        pltpu.SemaphoreType.DMA((2,)), pltpu.SemaphoreType.DMA((1,))],
    )
    return pl.pallas_call(
        _moe_kernel,
        grid_spec=grid_spec,
        out_shape=jax.ShapeDtypeStruct((n_out_rows, HALF), jnp.uint32),
        compiler_params=pltpu.CompilerParams(dimension_semantics=("arbitrary",),
                                             vmem_limit_bytes=MOE_VMEM_LIMIT),
        name="moe_experts",
    )(sb_e, sb_m, sb_n, sb_base, x_slots, w_gate, w_up, w_down)


def _dispatch_kernel(dest_ref, zb_ref, hn_ref, out_hbm, buf_ref, zero_ref, sem, zsem):
    i = pl.program_id(0)
    nt = pl.num_programs(0)
    tm = hn_ref.shape[0]
    t_all = nt * tm
    slot = lax.rem(i, 2)
    nblk = zb_ref.shape[0]

    def zero_copy(blk):
        dst = pl.ds(pl.multiple_of(blk * MOE_BLK, MOE_BLK), MOE_BLK)
        return pltpu.make_async_copy(zero_ref, out_hbm.at[dst, :], zsem.at[0])

    @pl.when(i == 0)
    def _():
        zero_ref[...] = jnp.zeros_like(zero_ref)

        def start(blk, c):
            @pl.when(zb_ref[blk] == 1)
            def _():
                zero_copy(blk).start()
            return c

        def wait(blk, c):
            @pl.when(zb_ref[blk] == 1)
            def _():
                zero_copy(blk).wait()
            return c

        lax.fori_loop(0, nblk, start, 0)
        lax.fori_loop(0, nblk, wait, 0)

    def wait_rows(sl):
        for k in range(2):
            pltpu.make_async_copy(buf_ref.at[sl], out_hbm.at[pl.ds(0, tm), :], sem.at[sl]).wait()

    @pl.when(i >= 2)
    def _():
        wait_rows(slot)

    buf_ref[slot] = hn_ref[...]
    base = i * tm
    for a in range(2 * tm):
        row, k = a // 2, a % 2
        d = dest_ref[k * t_all + base + row]
        pltpu.make_async_copy(buf_ref.at[slot, pl.ds(row, 1), :], out_hbm.at[pl.ds(d, 1), :],
                              sem.at[slot]).start()

    @pl.when(i == nt - 1)
    def _():
        wait_rows(slot)

        @pl.when(nt > 1)
        def _():
            wait_rows(1 - slot)


def _dispatch(dest, zero_blk, hn_pk, n_slots):
    tm = TOK_TILE
    grid_spec = pltpu.PrefetchScalarGridSpec(
        num_scalar_prefetch=2,
        grid=(hn_pk.shape[0] // tm,),
        in_specs=[pl.BlockSpec((tm, HALF), lambda i, d, z: (i, 0))],
        out_specs=pl.BlockSpec(memory_space=pl.ANY),
        scratch_shapes=[pltpu.VMEM((2, tm, HALF), jnp.uint32), pltpu.VMEM((MOE_BLK, HALF), jnp.uint32),
                        pltpu.SemaphoreType.DMA((2,)), pltpu.SemaphoreType.DMA((1,))],
    )
    return pl.pallas_call(
        _dispatch_kernel,
        grid_spec=grid_spec,
        out_shape=jax.ShapeDtypeStruct((n_slots, HALF), jnp.uint32),
        compiler_params=pltpu.CompilerParams(dimension_semantics=("arbitrary",), vmem_limit_bytes=VMEM_LIMIT),
        name="moe_dispatch",
    )(dest, zero_blk, hn_pk)


def _moe_plan(rtt, cnt):
    t_all = rtt.shape[1]
    n_assign = 2 * t_all
    e_flat = rtt[0:2].astype(jnp.int32).reshape(-1)
    rank = rtt[4:6].astype(jnp.int32).reshape(-1)
    counts = cnt[0, 0:N_EXPERTS].astype(jnp.int32)
    padded = (counts + MOE_BLK - 1) // MOE_BLK * MOE_BLK
    pad_end = jnp.cumsum(padded)
    pad_start = pad_end - padded
    before = jnp.arange(N_EXPERTS, dtype=jnp.int32)[:, None] < e_flat[None, :]
    dest = rank + jnp.sum(jnp.where(before, padded[:, None], 0), axis=0)
    nblk = -(-(n_assign + N_EXPERTS * (MOE_BLK - 1)) // MOE_BLK)
    n_slots = nblk * MOE_BLK
    blk = jnp.arange(nblk, dtype=jnp.int32)
    last_of_expert = jnp.any((blk[:, None] == (pad_end // MOE_BLK - 1)[None, :]) & (counts > 0)[None, :], axis=1)
    zero_blk = (last_of_expert | (blk >= pad_end[-1] // MOE_BLK)).astype(jnp.int32)
    k_e = padded // MOE_BLK
    sbc = (k_e + MOE_SB - 1) // MOE_SB
    sb_end = jnp.cumsum(sbc)
    sb_start = sb_end - sbc
    n_sb = (nblk + (MOE_SB - 1) * N_EXPERTS) // MOE_SB
    s = jnp.arange(n_sb, dtype=jnp.int32)
    sb_e = jnp.minimum(jnp.sum((s[:, None] >= sb_end[None, :]).astype(jnp.int32), axis=1), N_EXPERTS - 1)
    j = s - sb_start[sb_e]
    real = s < sb_end[-1]
    tail_blk = pad_end[-1] // MOE_BLK + MOE_SB * (s - sb_end[-1])
    sb_m = jnp.where(real, jnp.clip(k_e[sb_e] - MOE_SB * j, 0, MOE_SB), jnp.clip(nblk - tail_blk, 0, MOE_SB))
    sb_n = jnp.where(real, jnp.clip(counts[sb_e] - MOE_SB * MOE_BLK * j, 0, MOE_SB * MOE_BLK), 0)
    sb_base = jnp.where(real, pad_start[sb_e] + MOE_SB * MOE_BLK * j, jnp.minimum(tail_blk, nblk - 1) * MOE_BLK)
    i32 = lambda a: a.astype(jnp.int32)
    return (i32(sb_e), i32(sb_m), i32(sb_n), i32(sb_base)), i32(dest), zero_blk, n_slots


def _ple_out_kernel(dest_ref, h1_ref, rt_ref, pp_ref, ps_ref, nple_ref, wpg_ref, wp_ref, nfin_ref, ys_hbm,
                    op_ref, os_ref, yb_ref, sem, *, n_prompt_tiles):
    i = pl.program_id(0)
    nt = pl.num_programs(0)
    tm = h1_ref.shape[0]
    t_all = nt * tm
    slot = lax.rem(i, 2)

    def issue(tile, sl):
        base = tile * tm

        def grp(g, c):
            for j in range(DMA_UNROLL):
                row = g * (DMA_UNROLL // 2) + j // 2
                d = dest_ref[(j % 2) * t_all + base + row]
                pltpu.make_async_copy(ys_hbm.at[pl.ds(d, 1), :], yb_ref.at[sl, j % 2, pl.ds(row, 1), :],
                                      sem.at[sl]).start()
            return c

        lax.fori_loop(0, 2 * tm // DMA_UNROLL, grp, 0)

    @pl.when(i == 0)
    def _():
        issue(0, 0)

    def wait_rows(sl):
        for k in range(2):
            pltpu.make_async_copy(ys_hbm.at[pl.ds(0, tm), :], yb_ref.at[sl, k], sem.at[sl]).wait()

    wait_rows(slot)
    is_p = i < n_prompt_tiles
    rt = rt_ref[...]
    w0, w1 = rt[:, 2:3], rt[:, 3:4]
    lo0, hi0 = _unpack_bf16_pair(yb_ref[slot, 0])
    lo1, hi1 = _unpack_bf16_pair(yb_ref[slot, 1])
    h2 = h1_ref[...] + jnp.concatenate([w0 * lo0 + w1 * lo1, w0 * hi0 + w1 * hi1], axis=1)
    hb = _rms(h2, nple_ref[...]).astype(BF16)
    nbase = jnp.minimum(i + 1, nt - 1) * tm
    n_chunk = 8
    cw, ca = D_MODEL // n_chunk, 2 * tm // n_chunk
    gates = []
    for c in range(n_chunk):
        gates.append(jax.nn.sigmoid(_dot(hb, wpg_ref[:, c * cw:(c + 1) * cw])))
        for a in range(c * ca, (c + 1) * ca):
            d = dest_ref[(a % 2) * t_all + nbase + a // 2]
            pltpu.make_async_copy(ys_hbm.at[pl.ds(d, 1), :], yb_ref.at[1 - slot, a % 2, pl.ds(a // 2, 1), :],
                                  sem.at[1 - slot]).start()
    gate = jnp.concatenate(gates, axis=1)

    @pl.when(i == nt - 1)
    def _():
        wait_rows(1 - slot)

    p = jnp.where(is_p, pp_ref[...], ps_ref[...])
    h3 = h2 + _dot(p.astype(BF16), wp_ref[...]) * gate
    y = _rms(h3, nfin_ref[...])

    @pl.when(is_p)
    def _():
        op_ref[...] = y

    @pl.when(jnp.logical_not(is_p))
    def _():
        os_ref[...] = y


def _ple_out(dest, h1, rt, pp, ps, nple, wpg, wp, nfin, y_slots):
    tm = TOK_TILE
    npt, nst = pp.shape[0] // tm, ps.shape[0] // tm
    row = lambda w: pl.BlockSpec((tm, w), lambda i, d: (i, 0))
    prow = lambda w: pl.BlockSpec((tm, w), lambda i, d: (jnp.minimum(i, npt - 1), 0))
    srow = lambda w: pl.BlockSpec((tm, w), lambda i, d: (jnp.maximum(i - npt, 0), 0))
    const = lambda shape: pl.BlockSpec(shape, lambda i, d: (0,) * len(shape), pipeline_mode=pl.Buffered(1))
    grid_spec = pltpu.PrefetchScalarGridSpec(
        num_scalar_prefetch=1,
        grid=(npt + nst,),
        in_specs=[row(D_MODEL), row(LANES), prow(D_PLE), srow(D_PLE), const((1, D_MODEL)), const(wpg.shape),
                  const(wp.shape), const((1, D_MODEL)), pl.BlockSpec(memory_space=pl.ANY)],
        out_specs=[prow(D_MODEL), srow(D_MODEL)],
        scratch_shapes=[pltpu.VMEM((2, 2, tm, HALF), jnp.uint32), pltpu.SemaphoreType.DMA((2,))],
    )
    return pl.pallas_call(
        functools.partial(_ple_out_kernel, n_prompt_tiles=npt),
        grid_spec=grid_spec,
        out_shape=[jax.ShapeDtypeStruct((pp.shape[0], D_MODEL), F32),
                   jax.ShapeDtypeStruct((ps.shape[0], D_MODEL), F32)],
        compiler_params=pltpu.CompilerParams(dimension_semantics=("arbitrary",), vmem_limit_bytes=VMEM_LIMIT),
        name="ple_out",
    )(dest, h1, rt, pp, ps, nple, wpg, wp, nfin, y_slots)


def _s5_params(lam_re, lam_im, log_dt, b_re, b_im, c_re, c_im, d_skip):
    dt = jnp.exp(log_dt)[:, None]
    mag = jnp.exp(lam_re * dt)
    ab_re = mag * jnp.cos(lam_im * dt)
    ab_im = mag * jnp.sin(lam_im * dt)
    den = lam_re * lam_re + lam_im * lam_im
    nr = ab_re - 1.0
    f_re = (nr * lam_re + ab_im * lam_im) / den
    f_im = (ab_im * lam_re - nr * lam_im) / den
    bb_re = f_re[..., None] * b_re - f_im[..., None] * b_im
    bb_im = f_re[..., None] * b_im + f_im[..., None] * b_re
    eye = jnp.eye(S5_GB, dtype=F32)

    def bu_w(bb):
        bb = bb.reshape(S5_NGB, S5_GB, S5_STATE, S5_GROUP)
        w = jnp.einsum('nlph,lm->nlhmp', bb, eye)
        return w.reshape(S5_NGB, S5_GB * S5_GROUP, S5_SL)

    def c_w(c):
        c = c.reshape(S5_NGB, S5_GB, S5_GROUP, S5_STATE)
        w = jnp.einsum('nlhp,lm->nlpmh', c, eye)
        return w.reshape(S5_NGB, S5_SL, S5_GB * S5_GROUP)

    wbu = jnp.concatenate([bu_w(bb_re), bu_w(bb_im)], axis=2).astype(BF16)
    wc = jnp.concatenate([c_w(c_re), -c_w(c_im)], axis=1).astype(BF16)
    a_re = ab_re.reshape(S5_NGB, 1, S5_SL)
    a_im = ab_im.reshape(S5_NGB, 1, S5_SL)
    dsk = d_skip.reshape(S5_NGB, 1, S5_GB * S5_GROUP)
    return wbu, wc, a_re, a_im, dsk


def kernel(x_prompt, x_sample, p_prompt, p_sample, state_gla, state_s5_re, state_s5_im, norm_mix, w_in, gla_w_gate_up, gla_gate_bias, gla_norm, s5_lam_re, s5_lam_im, s5_log_dt, s5_b_re, s5_b_im, s5_c_re, s5_c_im, s5_d, s5_glu_w, s5_glu_b, s5_norm, w_out, norm_ffn, router_group, router_expert, w_gate, w_up, w_down, norm_ple, w_ple, w_ple_gate, norm_final):
    depth = w_in.shape[0]
    assert depth == 1
    i = 0
    bp, lp, _ = x_prompt.shape
    bs, ls, _ = x_sample.shape
    tp, ts = bp * lp, bs * ls
    t_all = tp + ts

    wm = _w_in_prep(w_in[i])
    wgu = jnp.pad(gla_w_gate_up[i], ((0, LANES - GLA_RANK), (0, 0))).astype(BF16)
    gbias = gla_gate_bias[i].reshape(1, QK_W)
    wbu, wc, a_re, a_im, dsk = _s5_params(s5_lam_re[i], s5_lam_im[i], s5_log_dt[i], s5_b_re[i], s5_b_im[i],
                                          s5_c_re[i], s5_c_im[i], s5_d[i])
    glu_w = s5_glu_w[i].astype(BF16)
    wo = w_out[i].astype(BF16)
    wr32 = jnp.pad(jnp.concatenate([router_group[i], router_expert[i]], axis=1),
                   ((0, 0), (0, LANES - N_EGROUPS - N_EXPERTS)))
    wr_hi = wr32.astype(BF16)
    wr = jnp.stack([wr_hi, (wr32 - wr_hi.astype(F32)).astype(BF16)])
    wpg = w_ple_gate[i].astype(BF16)
    wp = w_ple[i].astype(BF16)
    vec = lambda a: a.reshape(1, -1)

    xp = x_prompt.reshape(tp, D_MODEL)
    xs = x_sample.reshape(ts, D_MODEL)

    qp, kp, vp, rp, lap, up = _in_proj(xp, vec(norm_mix[i]), wm, wgu, gbias, BF16)
    qs, ks, vs, rs, las, us = _in_proj(xs, vec(norm_mix[i]), wm, wgu, gbias, BF16)
    ogp, gla_p = _gla_prompt(qp, kp, vp, rp, lap, vec(gla_norm[i]), bp, lp)
    ogs, gla_s = _gla_sample(qs, ks, vs, rs, las, vec(gla_norm[i]), state_gla[i], bs, ls)
    wbu2, wc2, a_re2, a_im2 = _s5_prompt_params(wbu, wc, a_re, a_im, bp)
    ysp, re_p, im_p = _s5_prompt(up.reshape(bp, lp, D_S5), wbu2, wc2, a_re2, a_im2, dsk, tc=256)
    yss, re_s, im_s = _s5(us.reshape(bs, ls, D_S5), wbu, wc, a_re, a_im, dsk,
                          state_s5_re[i].reshape(bs, -1), state_s5_im[i].reshape(bs, -1),
                          nb=bs, tc=ls, has_state=True)

    h1, hn_all, rt_all, rtt, cnt = _mix_out(xp, xs, ogp, ogs, ysp.reshape(tp, D_S5), yss.reshape(ts, D_S5),
                                            glu_w, vec(s5_glu_b[i]), vec(s5_norm[i]), wo, vec(norm_ffn[i]), wr)

    plan, dest, zero_blk, n_slots = _moe_plan(rtt, cnt)
    x_slots = _dispatch(dest, zero_blk, hn_all, n_slots)
    y_slots = _moe(x_slots, plan, w_gate[i], w_up[i], w_down[i], n_slots)

    y_p, y_s = _ple_out(dest, h1, rt_all, p_prompt[i].reshape(tp, D_PLE), p_sample[i].reshape(ts, D_PLE),
                        vec(norm_ple[i]), wpg, wp, vec(norm_final), y_slots)

    s5shape = lambda a, b: a.reshape(1, b, S5_GROUPS, S5_STATE)
    return (y_p.reshape(bp, lp, D_MODEL), y_s.reshape(bs, ls, D_MODEL),
            gla_p[None], s5shape(re_p, bp), s5shape(im_p, bp),
            gla_s[None], s5shape(re_s, bs), s5shape(im_s, bs))
```

```python
import functools
import math

import jax
import jax.numpy as jnp
from jax import lax
from jax.experimental import pallas as pl
from jax.experimental.pallas import tpu as pltpu

F32 = jnp.float32
BF16 = jnp.bfloat16

D_MODEL = 2048
D_GLA = 1024
D_S5 = 1024
GLA_HEADS = 4
GLA_DV = 256
GLA_DK = 128
GLA_RANK = 16
GLA_CHUNK = 64
S5_GROUP = 16
S5_GROUPS = 64
S5_STATE = 64
N_EGROUPS = 4
N_EPG = 8
N_EXPERTS = 32
D_EXPERT = 512
D_PLE = 256
EPS = 1e-6

LANES = 128
QK_W = GLA_HEADS * GLA_DK
S5_GB = 8
S5_NGB = S5_GROUPS // S5_GB
S5_SL = S5_GB * S5_STATE
TOK_TILE = 256
MOE_BLK = 128
VMEM_LIMIT = 56 * 1024 * 1024
MOE_VMEM_LIMIT = 60 * 1024 * 1024


def _const_spec(shape):
    nd = len(shape)
    return pl.BlockSpec(shape, lambda *_: (0,) * nd, pipeline_mode=pl.Buffered(1))


def _rms(x, g):
    return x * lax.rsqrt(jnp.mean(x * x, axis=-1, keepdims=True) + EPS) * g


def _dot(a, b):
    return jnp.dot(a, b, preferred_element_type=F32)


def _log_sigmoid(x):
    return -(jnp.maximum(-x, 0.0) + jnp.log1p(jnp.exp(-jnp.abs(x))))


N_QKVR = 2 * QK_W + 2 * D_GLA
W_IN_COLS = N_QKVR + GLA_RANK + D_S5


def _w_in_prep_kernel(w_ref, o_ref):
    o_ref[:, 0:N_QKVR] = w_ref[:, 0:N_QKVR].astype(BF16)
    tail = w_ref[:, N_QKVR:W_IN_COLS]
    o_ref[:, N_QKVR:N_QKVR + D_S5] = tail[:, GLA_RANK:GLA_RANK + D_S5].astype(BF16)
    o_ref[:, N_QKVR + D_S5:N_QKVR + D_S5 + LANES] = tail[:, 0:LANES].astype(BF16)


def _w_in_prep(wi):
    rows = 256
    return pl.pallas_call(
        _w_in_prep_kernel,
        grid=(D_MODEL // rows,),
        in_specs=[pl.BlockSpec((rows, W_IN_COLS), lambda i: (i, 0))],
        out_specs=pl.BlockSpec((rows, N_QKVR + D_S5 + LANES), lambda i: (i, 0)),
        out_shape=jax.ShapeDtypeStruct((D_MODEL, N_QKVR + D_S5 + LANES), BF16),
        compiler_params=pltpu.CompilerParams(dimension_semantics=("arbitrary",), vmem_limit_bytes=VMEM_LIMIT),
        name="w_in_prep",
    )(wi)


def _in_proj_kernel(x_ref, g_ref, wm_ref, wgu_ref, gb_ref,
                    q_ref, k_ref, v_ref, r_ref, la_ref, u_ref):
    hb = _rms(x_ref[...], g_ref[...]).astype(BF16)

    def seg(a, b):
        return _dot(hb, wm_ref[:, a:b])

    q_ref[...] = (seg(0, QK_W) * (GLA_DK ** -0.5)).astype(q_ref.dtype)
    k_ref[...] = seg(QK_W, 2 * QK_W).astype(k_ref.dtype)
    v_ref[...] = seg(2 * QK_W, 2 * QK_W + D_GLA).astype(v_ref.dtype)
    r_ref[...] = seg(2 * QK_W + D_GLA, N_QKVR).astype(r_ref.dtype)
    u_ref[...] = seg(N_QKVR, N_QKVR + D_S5)
    zg = seg(N_QKVR + D_S5, N_QKVR + D_S5 + LANES)
    xg = _dot(zg.astype(BF16), wgu_ref[...]) + gb_ref[...]
    la_ref[...] = _log_sigmoid(xg) * (1.0 / 16.0)


def _in_proj(x2d, g, wm, wgu, gbias, act_dtype):
    t = x2d.shape[0]
    tm = TOK_TILE
    row = lambda w: pl.BlockSpec((tm, w), lambda i: (i, 0))
    return pl.pallas_call(
        _in_proj_kernel,
        grid=(t // tm,),
        in_specs=[row(D_MODEL), _const_spec((1, D_MODEL)), _const_spec(wm.shape),
                  _const_spec(wgu.shape), _const_spec((1, QK_W))],
        out_specs=[row(QK_W), row(QK_W), row(D_GLA), row(D_GLA), row(QK_W), row(D_S5)],
        out_shape=[jax.ShapeDtypeStruct((t, QK_W), act_dtype), jax.ShapeDtypeStruct((t, QK_W), act_dtype),
                   jax.ShapeDtypeStruct((t, D_GLA), act_dtype), jax.ShapeDtypeStruct((t, D_GLA), act_dtype),
                   jax.ShapeDtypeStruct((t, QK_W), F32), jax.ShapeDtypeStruct((t, D_S5), F32)],
        compiler_params=pltpu.CompilerParams(dimension_semantics=("arbitrary",), vmem_limit_bytes=VMEM_LIMIT),
        name="in_proj",
    )(x2d, g, wm, wgu, gbias)


_NT = (((1,), (1,)), ((), ()))
_TN = (((0,), (0,)), ((), ()))


def _gla_pre(q, k, la, c):
    r = q.shape[0]
    shift = int(math.log2(c))
    ri = lax.broadcasted_iota(jnp.int32, (r, r), 0)
    si = lax.broadcasted_iota(jnp.int32, (r, r), 1)
    mask = ((ri >> shift) == (si >> shift)) & (ri >= si)
    tri = jnp.where(mask, 1.0, 0.0).astype(BF16)
    hi = la.astype(BF16)
    r1 = la - hi.astype(F32)
    mid = r1.astype(BF16)
    lo = (r1 - mid.astype(F32)).astype(BF16)
    cum = _dot(tri, hi) + _dot(tri, mid) + _dot(tri, lo)
    last = jnp.concatenate([jnp.broadcast_to(cum[(i + 1) * c - 1:(i + 1) * c, :], (c, cum.shape[1]))
                            for i in range(r // c)], axis=0)
    qe = (q * jnp.exp(cum)).astype(BF16)
    ke = (k * jnp.exp(-cum)).astype(BF16)
    kd = (k * jnp.exp(last - cum)).astype(BF16)
    return qe, ke, kd, cum, mask


def _gla_intra(qe, ke, v, mask):
    sc = lax.dot_general(qe, ke, _NT, preferred_element_type=F32)
    return _dot(jnp.where(mask, sc, 0.0).astype(BF16), v)


def _gla_finish(o, r, g):
    rf = r.astype(F32)
    return _rms(o, g) * (rf * jax.nn.sigmoid(rf))


GLA_SAFE_LOG_DECAY = -60.0


def _gla_token_step(t, rows, q, k, v, a_all, st_ref, oacc_ref):
    m = rows == t
    a = jnp.sum(jnp.where(m, a_all, 0.0), axis=0, keepdims=True)
    kt = jnp.where(m, k, 0.0).astype(BF16)
    qt = jnp.where(m, q, 0.0).astype(BF16)
    vt = jnp.where(m, v, jnp.zeros_like(v))
    st = st_ref[...] * a + lax.dot_general(vt, kt, _TN, preferred_element_type=F32)
    st_ref[...] = st
    oacc_ref[...] += lax.dot_general(qt, st.astype(BF16), _NT, preferred_element_type=F32)


def _gla_prompt_kernel(q_ref, k_ref, v_ref, r_ref, la_ref, g_ref, o_ref, sfin_ref, st_ref, oacc_ref,
                       *, n_chunks):
    j = pl.program_id(1)
    nseq = q_ref.shape[0]

    @pl.when(j == 0)
    def _():
        st_ref[...] = jnp.zeros_like(st_ref)

    c = GLA_CHUNK
    pre = [_gla_pre(q_ref[s].astype(F32), k_ref[s].astype(F32), la_ref[s], c) for s in range(nseq)]
    g = g_ref[...]
    safe = functools.reduce(jnp.minimum, [jnp.min(p[3]) for p in pre]) >= GLA_SAFE_LOG_DECAY

    @pl.when(safe)
    def _():
        for s in range(nseq):
            qe, ke, kd, cum, mask = pre[s]
            for h in range(GLA_HEADS):
                kc = slice(h * GLA_DK, (h + 1) * GLA_DK)
                vc = slice(h * GLA_DV, (h + 1) * GLA_DV)
                v = v_ref[s, :, vc]
                qe_h, kd_h = qe[:, kc], kd[:, kc]
                o = _gla_intra(qe_h, ke[:, kc], v, mask)
                st = st_ref[s, h]
                inter = []
                for ci in range(n_chunks):
                    rows = slice(ci * c, (ci + 1) * c)
                    inter.append(lax.dot_general(qe_h[rows], st.astype(BF16), _NT, preferred_element_type=F32))
                    dec = jnp.exp(cum[(ci + 1) * c - 1:(ci + 1) * c, kc])
                    st = st * dec + lax.dot_general(v[rows], kd_h[rows], _TN, preferred_element_type=F32)
                st_ref[s, h] = st
                o = o + jnp.concatenate(inter, axis=0)
                o_ref[s, :, vc] = _gla_finish(o, r_ref[s, :, vc], g).astype(o_ref.dtype)

    @pl.when(jnp.logical_not(safe))
    def _():
        n_rows = n_chunks * c
        rows = lax.broadcasted_iota(jnp.int32, (n_rows, 1), 0)
        for s in range(nseq):
            for h in range(GLA_HEADS):
                kc = slice(h * GLA_DK, (h + 1) * GLA_DK)
                vc = slice(h * GLA_DV, (h + 1) * GLA_DV)
                q, k, v = q_ref[s, :, kc].astype(F32), k_ref[s, :, kc].astype(F32), v_ref[s, :, vc]
                a_all = jnp.exp(la_ref[s, :, kc])
                oacc_ref[...] = jnp.zeros_like(oacc_ref)

                def step(t, carry, s=s, h=h, q=q, k=k, v=v, a_all=a_all):
                    _gla_token_step(t, rows, q, k, v, a_all, st_ref.at[s, h], oacc_ref)
                    return carry

                lax.fori_loop(0, n_rows, step, 0)
                o_ref[s, :, vc] = _gla_finish(oacc_ref[...], r_ref[s, :, vc], g).astype(o_ref.dtype)

    @pl.when(j == pl.num_programs(1) - 1)
    def _():
        for s in range(nseq):
            for h in range(GLA_HEADS):
                sfin_ref[s, h] = st_ref[s, h].T


def _gla_prompt(q, k, v, r, la, g, batch, seq):
    rb = 4 * GLA_CHUNK
    ns = 2
    row = lambda w: pl.BlockSpec((ns, rb, w), lambda b, j: (b, j, 0))
    seq3 = lambda a: a.reshape(batch, seq, a.shape[-1])
    o, sfin = pl.pallas_call(
        functools.partial(_gla_prompt_kernel, n_chunks=rb // GLA_CHUNK),
        grid=(batch // ns, seq // rb),
        in_specs=[row(QK_W), row(QK_W), row(D_GLA), row(D_GLA), row(QK_W), _const_spec((1, GLA_DV))],
        out_specs=[row(D_GLA),
                   pl.BlockSpec((ns, GLA_HEADS, GLA_DK, GLA_DV), lambda b, j: (b, 0, 0, 0))],
        out_shape=[jax.ShapeDtypeStruct((batch, seq, D_GLA), BF16),
                   jax.ShapeDtypeStruct((batch, GLA_HEADS, GLA_DK, GLA_DV), F32)],
        scratch_shapes=[pltpu.VMEM((ns, GLA_HEADS, GLA_DV, GLA_DK), F32), pltpu.VMEM((rb, GLA_DV), F32)],
        compiler_params=pltpu.CompilerParams(dimension_semantics=("arbitrary", "arbitrary"),
                                             vmem_limit_bytes=VMEM_LIMIT),
        name="gla_prompt",
    )(seq3(q), seq3(k), seq3(v), seq3(r), seq3(la), g)
    return o.reshape(batch * seq, D_GLA), sfin


def _gla_sample_kernel(q_ref, k_ref, v_ref, r_ref, la_ref, g_ref, s0_ref, o_ref, sfin_ref, st_ref, oacc_ref,
                       *, n_seq, seq):
    qe, ke, kd, cum, mask = _gla_pre(q_ref[...].astype(F32), k_ref[...].astype(F32), la_ref[...], seq)
    g = g_ref[...]
    safe = jnp.min(cum) >= GLA_SAFE_LOG_DECAY

    @pl.when(safe)
    def _():
        lasts = jnp.concatenate([cum[(s + 1) * seq - 1:(s + 1) * seq, :] for s in range(n_seq)]
                                + [jnp.zeros((GLA_DK - n_seq, cum.shape[1]), F32)], axis=0)
        pair = 2 * seq
        upper = lax.broadcasted_iota(jnp.int32, (pair, GLA_DK), 0) < seq
        for h in range(GLA_HEADS):
            kc = slice(h * GLA_DK, (h + 1) * GLA_DK)
            vc = slice(h * GLA_DV, (h + 1) * GLA_DV)
            v = v_ref[:, vc]
            qe_h, kd_h = qe[:, kc], kd[:, kc]
            o = _gla_intra(qe_h, ke[:, kc], v, mask)
            dec_t = jnp.exp(lasts[:, kc].T)
            inter = []
            for p in range(n_seq // 2):
                rows = slice(p * pair, (p + 1) * pair)
                qe_p, kd_p, v_p = qe_h[rows], kd_h[rows], v[rows]
                for half in range(2):
                    s = 2 * p + half
                    s0 = s0_ref[s, h]
                    o_s = _dot(qe_p, s0.astype(BF16))
                    inter.append(o_s[half * seq:(half + 1) * seq])
                    kd_s = jnp.where(upper if half == 0 else ~upper, kd_p, jnp.zeros_like(kd_p))
                    dec = jnp.broadcast_to(dec_t[:, s:s + 1], (GLA_DK, GLA_DV))
                    sfin_ref[s, h] = s0 * dec + lax.dot_general(kd_s, v_p, _TN, preferred_element_type=F32)
            o = o + jnp.concatenate(inter, axis=0)
            o_ref[:, vc] = _gla_finish(o, r_ref[:, vc], g).astype(o_ref.dtype)

    @pl.when(jnp.logical_not(safe))
    def _():
        n_rows = n_seq * seq
        rows = lax.broadcasted_iota(jnp.int32, (n_rows, 1), 0)
        for h in range(GLA_HEADS):
            kc = slice(h * GLA_DK, (h + 1) * GLA_DK)
            vc = slice(h * GLA_DV, (h + 1) * GLA_DV)
            q, k, v = q_ref[:, kc].astype(F32), k_ref[:, kc].astype(F32), v_ref[:, vc]
            a_all = jnp.exp(la_ref[:, kc])
            oacc_ref[...] = jnp.zeros_like(oacc_ref)

            def step(t, carry):
                s = t // seq

                @pl.when(t % seq == 0)
                def _():
                    st_ref[...] = s0_ref[s, h].T

                _gla_token_step(t, rows, q, k, v, a_all, st_ref, oacc_ref)

                @pl.when(t % seq == seq - 1)
                def _():
                    sfin_ref[s, h] = st_ref[...].T

                return carry

            lax.fori_loop(0, n_rows, step, 0)
            o_ref[:, vc] = _gla_finish(oacc_ref[...], r_ref[:, vc], g).astype(o_ref.dtype)


def _gla_sample(q, k, v, r, la, g, s0, batch, seq):
    ns = 16
    rb = ns * seq
    row = lambda w: pl.BlockSpec((rb, w), lambda i: (i, 0))
    st = pl.BlockSpec((ns, GLA_HEADS, GLA_DK, GLA_DV), lambda i: (i, 0, 0, 0))
    return pl.pallas_call(
        functools.partial(_gla_sample_kernel, n_seq=ns, seq=seq),
        grid=(batch // ns,),
        in_specs=[row(QK_W), row(QK_W), row(D_GLA), row(D_GLA), row(QK_W), _const_spec((1, GLA_DV)), st],
        out_specs=[row(D_GLA), st],
        out_shape=[jax.ShapeDtypeStruct((batch * seq, D_GLA), BF16),
                   jax.ShapeDtypeStruct((batch, GLA_HEADS, GLA_DK, GLA_DV), F32)],
        scratch_shapes=[pltpu.VMEM((GLA_DV, GLA_DK), F32), pltpu.VMEM((rb, GLA_DV), F32)],
        compiler_params=pltpu.CompilerParams(dimension_semantics=("arbitrary",), vmem_limit_bytes=VMEM_LIMIT),
        name="gla_sample",
    )(q, k, v, r, la, g, s0)


def _s5_kernel(u_ref, wbu_ref, wc_ref, are_ref, aim_ref, d_ref, h0r_ref, h0i_ref,
               y_ref, sre_ref, sim_ref, bu_ref, xs_ref, car_ref, *, nb, tc, has_state):
    j = pl.program_id(2)

    @pl.when(j == 0)
    def _():
        if has_state:
            car_ref[0] = h0r_ref[...]
            car_ref[1] = h0i_ref[...]
        else:
            car_ref[...] = jnp.zeros_like(car_ref)

    u2 = u_ref[...].reshape(nb * tc, LANES)
    ub = u2.astype(BF16)
    nl = S5_SL // LANES
    for l in range(2 * nl):
        bu_ref[l] = _dot(ub, wbu_ref[0, :, l * LANES:(l + 1) * LANES])
    a_r = [jnp.broadcast_to(are_ref[0, :, l * LANES:(l + 1) * LANES], (nb, LANES)) for l in range(nl)]
    a_i = [jnp.broadcast_to(aim_ref[0, :, l * LANES:(l + 1) * LANES], (nb, LANES)) for l in range(nl)]

    def step(t, carry):
        rows = pl.ds(t, nb, stride=tc)
        out = []
        for l in range(nl):
            xr, xi = carry[2 * l], carry[2 * l + 1]
            nr = a_r[l] * xr - a_i[l] * xi + bu_ref[l, rows, :]
            ni = a_r[l] * xi + a_i[l] * xr + bu_ref[nl + l, rows, :]
            xs_ref[l, rows, :] = nr
            xs_ref[nl + l, rows, :] = ni
            out += [nr, ni]
        return tuple(out)

    init = []
    for l in range(nl):
        init += [car_ref[0, :, l * LANES:(l + 1) * LANES], car_ref[1, :, l * LANES:(l + 1) * LANES]]
    fin = lax.fori_loop(0, tc, step, tuple(init), unroll=8)
    xr = jnp.concatenate([fin[2 * l] for l in range(nl)], axis=1)
    xi = jnp.concatenate([fin[2 * l + 1] for l in range(nl)], axis=1)
    car_ref[0] = xr
    car_ref[1] = xi
    y = d_ref[0] * u2
    for l in range(2 * nl):
        y = y + _dot(xs_ref[l].astype(BF16), wc_ref[0, l * LANES:(l + 1) * LANES, :])
    y_ref[...] = y.reshape(nb, tc, LANES)

    @pl.when(j == pl.num_programs(2) - 1)
    def _():
        sre_ref[...] = xr
        sim_ref[...] = xi


def _s5(u3d, wbu, wc, a_re, a_im, dsk, h0r, h0i, nb, tc, has_state):
    batch, seq, _ = u3d.shape
    grid = (S5_NGB, batch // nb, seq // tc)
    st = pl.BlockSpec((nb, S5_SL), lambda g, b, j: (b, g))
    par = lambda w: pl.BlockSpec((1, 1, w), lambda g, b, j: (g, 0, 0))
    ublk = pl.BlockSpec((nb, tc, LANES), lambda g, b, j: (b, j, g))
    return pl.pallas_call(
        functools.partial(_s5_kernel, nb=nb, tc=tc, has_state=has_state),
        grid=grid,
        in_specs=[ublk,
                  pl.BlockSpec((1, LANES, 2 * S5_SL), lambda g, b, j: (g, 0, 0)),
                  pl.BlockSpec((1, 2 * S5_SL, LANES), lambda g, b, j: (g, 0, 0)),
                  par(S5_SL), par(S5_SL), par(LANES), st, st],
        out_specs=[ublk, st, st],
        out_shape=[jax.ShapeDtypeStruct(u3d.shape, F32),
                   jax.ShapeDtypeStruct((batch, S5_GROUPS * S5_STATE), F32),
                   jax.ShapeDtypeStruct((batch, S5_GROUPS * S5_STATE), F32)],
        scratch_shapes=[pltpu.VMEM((2 * S5_SL // LANES, nb * tc, LANES), F32),
                        pltpu.VMEM((2 * S5_SL // LANES, nb * tc, LANES), F32),
                        pltpu.VMEM((2, nb, S5_SL), F32)],
        compiler_params=pltpu.CompilerParams(dimension_semantics=("arbitrary",) * 3,
                                             vmem_limit_bytes=VMEM_LIMIT),
        name="s5_state" if has_state else "s5_zero",
    )(u3d, wbu, wc, a_re, a_im, dsk, h0r, h0i)


S5_J = 2
S5_HL = S5_SL // S5_J
S5_PAIR = 2


def _s5_prompt_kernel(u_ref, wbu_ref, wc_ref, are_ref, aim_ref, d_ref, y_ref, sre_ref, sim_ref,
                      u2_ref, lhs_ref, bu_ref, xs_ref, y2_ref, yo_ref, car_ref, *, nb, tc):
    g = pl.program_id(0)
    j = pl.program_id(1)
    rows = nb * S5_J
    npair = S5_PAIR

    @pl.when((g == 0) & (j == 0))
    def _():
        lhs_ref[...] = jnp.zeros_like(lhs_ref)

    @pl.when(j == 0)
    def _():
        car_ref[...] = jnp.zeros_like(car_ref)

    u2 = u_ref[...].reshape(nb * tc, npair * LANES)
    for p in range(npair):
        u2_ref[p] = u2[:, p * LANES:(p + 1) * LANES]

    def build(t, c):
        for p in range(npair):
            u4 = u2_ref[p, pl.ds(t, nb, stride=tc), :]
            for jj in range(S5_J):
                lhs_ref[p, t, jj * nb:(jj + 1) * nb, jj * LANES:(jj + 1) * LANES] = u4
        return c

    lax.fori_loop(0, tc, build, 0, unroll=True)
    for p in range(npair):
        lhs = lhs_ref[p].reshape(tc * rows, S5_J * LANES).astype(BF16)
        bu_ref[p] = _dot(lhs, wbu_ref[p]).reshape(tc, rows, 2 * S5_HL)
    ar = [are_ref[p] for p in range(npair)]
    ai = [aim_ref[p] for p in range(npair)]

    def step(t, carry):
        out = []
        for p in range(npair):
            xr, xi = carry[2 * p], carry[2 * p + 1]
            tile = bu_ref[p, t]
            nr = ar[p] * xr - ai[p] * xi + tile[:, 0:S5_HL]
            ni = ar[p] * xi + ai[p] * xr + tile[:, S5_HL:2 * S5_HL]
            xs_ref[p, t] = jnp.concatenate([nr, ni], axis=1)
            out += [nr, ni]
        return tuple(out)

    init = tuple(car_ref[p, c] for p in range(npair) for c in range(2))
    fin = lax.fori_loop(0, tc, step, init, unroll=True)
    for p in range(npair):
        car_ref[p, 0] = fin[2 * p]
        car_ref[p, 1] = fin[2 * p + 1]
        xs = xs_ref[p].reshape(tc * rows, 2 * S5_HL).astype(BF16)
        y2_ref[p] = _dot(xs, wc_ref[p]).reshape(tc, rows, S5_J * LANES)
    first_half = lax.broadcasted_iota(jnp.int32, (rows, LANES), 0) < nb

    def unperm(t, c):
        for p in range(npair):
            t2 = y2_ref[p, t]
            part = jnp.where(first_half, t2[:, 0:LANES], t2[:, LANES:2 * LANES])
            yo_ref[p, pl.ds(t, nb, stride=tc), :] = (part + pltpu.roll(part, nb, axis=0))[0:nb]
        return c

    lax.fori_loop(0, tc, unperm, 0, unroll=True)
    y = jnp.concatenate([yo_ref[p] + d_ref[p] * u2_ref[p] for p in range(npair)], axis=1)
    y_ref[...] = y.reshape(nb, tc, npair * LANES)

    @pl.when(j == pl.num_programs(1) - 1)
    def _():
        def gather_state(x):
            return jnp.concatenate([x[jj * nb:(jj + 1) * nb] for jj in range(S5_J)], axis=1)

        sre_ref[...] = jnp.concatenate([gather_state(fin[2 * p]) for p in range(npair)], axis=1)
        sim_ref[...] = jnp.concatenate([gather_state(fin[2 * p + 1]) for p in range(npair)], axis=1)


def _s5_prompt(u3d, wbu2, wc2, a_re2, a_im2, dsk, tc):
    nb, seq, _ = u3d.shape
    rows = nb * S5_J
    assert rows == 8
    np_ = S5_PAIR
    st = pl.BlockSpec((nb, np_ * S5_SL), lambda g, j: (0, g))
    gblk = lambda s: pl.BlockSpec((np_,) + s, lambda g, j: (g, 0, 0))
    ublk = pl.BlockSpec((nb, tc, np_ * LANES), lambda g, j: (0, j, g))
    return pl.pallas_call(
        functools.partial(_s5_prompt_kernel, nb=nb, tc=tc),
        grid=(S5_NGB // np_, seq // tc),
        in_specs=[ublk, gblk((S5_J * LANES, 2 * S5_HL)), gblk((2 * S5_HL, S5_J * LANES)),
                  gblk((rows, S5_HL)), gblk((rows, S5_HL)), gblk((1, LANES))],
        out_specs=[ublk, st, st],
        out_shape=[jax.ShapeDtypeStruct(u3d.shape, F32),
                   jax.ShapeDtypeStruct((nb, S5_GROUPS * S5_STATE), F32),
                   jax.ShapeDtypeStruct((nb, S5_GROUPS * S5_STATE), F32)],
        scratch_shapes=[pltpu.VMEM((np_, nb * tc, LANES), F32),
                        pltpu.VMEM((np_, tc, rows, S5_J * LANES), F32),
                        pltpu.VMEM((np_, tc, rows, 2 * S5_HL), F32),
                        pltpu.VMEM((np_, tc, rows, 2 * S5_HL), F32),
                        pltpu.VMEM((np_, tc, rows, S5_J * LANES), F32),
                        pltpu.VMEM((np_, nb * tc, LANES), F32),
                        pltpu.VMEM((np_, 2, rows, S5_HL), F32)],
        compiler_params=pltpu.CompilerParams(dimension_semantics=("arbitrary",) * 2,
                                             vmem_limit_bytes=VMEM_LIMIT),
        name="s5_prompt",
    )(u3d, wbu2, wc2, a_re2, a_im2, dsk)


def _s5_prompt_params(wbu, wc, a_re, a_im, nb):
    h = S5_HL
    top = jnp.concatenate([wbu[:, :, 0:h], wbu[:, :, S5_SL:S5_SL + h]], axis=2)
    bot = jnp.concatenate([wbu[:, :, h:2 * h], wbu[:, :, S5_SL + h:S5_SL + 2 * h]], axis=2)
    wbu2 = jnp.concatenate([top, bot], axis=1)
    wc_j = [jnp.concatenate([wc[:, jj * h:(jj + 1) * h, :], wc[:, S5_SL + jj * h:S5_SL + (jj + 1) * h, :]],
                            axis=1) for jj in range(S5_J)]
    wc2 = jnp.concatenate(wc_j, axis=2)
    tile = lambda a: jnp.repeat(a.reshape(S5_NGB, S5_J, h), nb, axis=1)
    return wbu2, wc2, tile(a_re), tile(a_im)


def _mix_out_kernel(xp_ref, xs_ref, ogp_ref, ogs_ref, ysp_ref, yss_ref, glu_w_ref, glu_b_ref, s5n_ref, wo_ref,
                    nffn_ref, wr_ref, h1_ref, hn_ref, rt_ref, rtt_ref, cnt_ref, *, n_prompt_tiles):
    is_p = pl.program_id(0) < n_prompt_tiles
    x = jnp.where(is_p, xp_ref[...], xs_ref[...])
    og = jnp.where(is_p, ogp_ref[...], ogs_ref[...])
    y = jax.nn.gelu(jnp.where(is_p, ysp_ref[...], yss_ref[...]))
    y = y * jax.nn.sigmoid(_dot(y.astype(BF16), glu_w_ref[...]) + glu_b_ref[...])
    yn = _rms(y, s5n_ref[...]).astype(BF16)
    mix = _dot(og, wo_ref[0:D_GLA, :]) + _dot(yn, wo_ref[D_GLA:D_GLA + D_S5, :])
    h1 = x + mix
    h1_ref[...] = h1
    hn = _rms(h1, nffn_ref[...])
    hn_ref[...] = _pack_bf16_pair(hn[:, 0:HALF], hn[:, HALF:D_MODEL])
    hn_hi = hn.astype(BF16)
    hn_lo = (hn - hn_hi.astype(F32)).astype(BF16)
    logits = _dot(hn_hi, wr_ref[0]) + _dot(hn_hi, wr_ref[1]) + _dot(hn_lo, wr_ref[0])
    rt = _route(logits)

    @pl.when(pl.program_id(0) == 0)
    def _():
        cnt_ref[...] = jnp.zeros_like(cnt_ref)

    tm = rt.shape[0]
    lane = lax.broadcasted_iota(jnp.int32, rt.shape, 1).astype(F32)
    oh0 = lane == rt[:, 0:1]
    oh1 = lane == rt[:, 1:2]
    both = jnp.where(oh0 | oh1, 1.0, 0.0)
    ri = lax.broadcasted_iota(jnp.int32, (tm, tm), 0)
    ci = lax.broadcasted_iota(jnp.int32, (tm, tm), 1)
    before = _dot(jnp.where(ri > ci, 1.0, 0.0).astype(BF16), both.astype(BF16)) + cnt_ref[...]
    rank0 = jnp.sum(jnp.where(oh0, before, 0.0), axis=-1, keepdims=True)
    rank1 = jnp.sum(jnp.where(oh1, before, 0.0), axis=-1, keepdims=True)
    cnt_ref[...] += jnp.sum(both, axis=0, keepdims=True)
    rt = jnp.where(lane == 4.0, rank0, jnp.where(lane == 5.0, rank1, rt))
    rt_ref[...] = rt
    rtt_ref[...] = rt.T[0:8, :]


def _route(logits):
    col = lax.broadcasted_iota(jnp.int32, logits.shape, 1)
    colf = col.astype(F32)
    neg = -jnp.inf

    def first_argmax(vals):
        m = jnp.max(vals, axis=-1, keepdims=True)
        idx = jnp.min(jnp.where(vals == m, colf, float(LANES)), axis=-1, keepdims=True)
        return m, idx

    lg = jnp.where(col < N_EGROUPS, logits, neg)
    gmax, gsel = first_argmax(lg)
    p_g = 1.0 / jnp.sum(jnp.exp(lg - gmax), axis=-1, keepdims=True)
    ecol = col - N_EGROUPS
    egrp = (ecol >> int(math.log2(N_EPG))).astype(F32)
    in_group = (ecol >= 0) & (ecol < N_EXPERTS) & (egrp == gsel)
    le = jnp.where(in_group, logits, neg)
    m1, i1 = first_argmax(le)
    le2 = jnp.where(colf == i1, neg, le)
    m2, i2 = first_argmax(le2)
    e2 = jnp.exp(m2 - m1)
    den = 1.0 + e2
    w1 = p_g * (1.0 / den)
    w2 = p_g * (e2 / den)
    e1f = i1 - float(N_EGROUPS)
    e2f = i2 - float(N_EGROUPS)
    out = jnp.where(col == 0, e1f, jnp.where(col == 1, e2f, jnp.where(col == 2, w1, jnp.where(col == 3, w2, 0.0))))
    return out


def _mix_out(xp, xs, ogp, ogs, ysp, yss, glu_w, glu_b, s5n, wo, nffn, wr):
    tm = TOK_TILE
    npt, nst = xp.shape[0] // tm, xs.shape[0] // tm
    t = (npt + nst) * tm
    row = lambda w: pl.BlockSpec((tm, w), lambda i: (i, 0))
    prow = lambda w: pl.BlockSpec((tm, w), lambda i: (jnp.minimum(i, npt - 1), 0))
    srow = lambda w: pl.BlockSpec((tm, w), lambda i: (jnp.maximum(i - npt, 0), 0))
    return pl.pallas_call(
        functools.partial(_mix_out_kernel, n_prompt_tiles=npt),
        grid=(npt + nst,),
        in_specs=[prow(D_MODEL), srow(D_MODEL), prow(D_GLA), srow(D_GLA), prow(D_S5), srow(D_S5),
                  _const_spec(glu_w.shape), _const_spec((1, D_S5)), _const_spec((1, D_S5)),
                  _const_spec(wo.shape), _const_spec((1, D_MODEL)), _const_spec(wr.shape)],
        out_specs=[row(D_MODEL), row(HALF), row(LANES), pl.BlockSpec((8, tm), lambda i: (0, i)),
                   pl.BlockSpec((1, LANES), lambda i: (0, 0))],
        out_shape=[jax.ShapeDtypeStruct((t, D_MODEL), F32), jax.ShapeDtypeStruct((t, HALF), jnp.uint32),
                   jax.ShapeDtypeStruct((t, LANES), F32), jax.ShapeDtypeStruct((8, t), F32),
                   jax.ShapeDtypeStruct((1, LANES), F32)],
        compiler_params=pltpu.CompilerParams(dimension_semantics=("arbitrary",), vmem_limit_bytes=VMEM_LIMIT),
        name="mix_out",
    )(xp, xs, ogp, ogs, ysp, yss, glu_w, glu_b, s5n, wo, nffn, wr)


DMA_UNROLL = 8


MOE_SB = 6
HALF = D_MODEL // 2


def _pack_bf16_pair(lo, hi):
    def bits(x):
        b = pltpu.bitcast(x, jnp.uint32)
        return (b + jnp.uint32(0x7FFF) + ((b >> 16) & jnp.uint32(1))) >> 16
    return bits(lo) | (bits(hi) << 16)


def _unpack_bf16_pair(u):
    return (pltpu.bitcast(u << 16, F32), pltpu.bitcast(u & jnp.uint32(0xFFFF0000), F32))


def _moe_kernel(se_ref, sm_ref, sn_ref, sbase_ref, xs_hbm, wg_ref, wu_ref, wd_ref, out_hbm,
                xb_ref, yb_ref, gsem, ssem):
    b = pl.program_id(0)
    nb = pl.num_programs(0)
    n = sn_ref[b]
    slot = lax.rem(b, 2)

    def issue_gather(blk, sl):
        base = pl.multiple_of(sbase_ref[blk], MOE_BLK)
        for m in range(1, MOE_SB + 1):
            @pl.when((sn_ref[blk] > 0) & (sm_ref[blk] == m))
            def _():
                rows = m * MOE_BLK
                pltpu.make_async_copy(xs_hbm.at[pl.ds(base, rows), :], xb_ref.at[sl, pl.ds(0, rows), :],
                                      gsem.at[sl]).start()

    def wait_gather(blocks, sl):
        rows = pl.ds(0, pl.multiple_of(blocks * MOE_BLK, MOE_BLK))
        pltpu.make_async_copy(xs_hbm.at[rows, :], xb_ref.at[sl, rows, :], gsem.at[sl]).wait()

    m_cur = sm_ref[b]
    m_prev = sm_ref[jnp.maximum(b - 1, 0)]
    has_prev = (b > 0) & (m_prev > 0)

    def out_copy(rows):
        dst = pl.ds(pl.multiple_of(sbase_ref[b], MOE_BLK), rows)
        return pltpu.make_async_copy(yb_ref.at[pl.ds(0, rows), :], out_hbm.at[dst, :], ssem.at[0])

    def wait_prev_out():
        rows = pl.ds(0, pl.multiple_of(m_prev * MOE_BLK, MOE_BLK))
        pltpu.make_async_copy(yb_ref.at[rows, :], out_hbm.at[rows, :], ssem.at[0]).wait()

    def write_out(rows):
        out_copy(rows).start()

        @pl.when(b == nb - 1)
        def _():
            out_copy(rows).wait()

    @pl.when(b == 0)
    def _():
        issue_gather(0, 0)

    @pl.when(b + 1 < nb)
    def _():
        issue_gather(b + 1, 1 - slot)

    def compute(rows):
        x_lo, x_hi = _unpack_bf16_pair(xb_ref[slot, 0:rows, :])
        x_lo, x_hi = x_lo.astype(BF16), x_hi.astype(BF16)

        def w(ref, r, c):
            return ref[r, c].astype(BF16)

        lo, hi, full = slice(0, HALF), slice(HALF, D_MODEL), slice(None)
        gate = _dot(x_lo, w(wg_ref, lo, full)) + _dot(x_hi, w(wg_ref, hi, full))
        up = _dot(x_lo, w(wu_ref, lo, full)) + _dot(x_hi, w(wu_ref, hi, full))
        hid = (gate * jax.nn.sigmoid(gate) * up).astype(BF16)

        @pl.when(has_prev)
        def _():
            wait_prev_out()

        yb_ref[0:rows, :] = _pack_bf16_pair(_dot(hid, w(wd_ref, full, lo)), _dot(hid, w(wd_ref, full, hi)))
        write_out(rows)

    @pl.when(n > 0)
    def _():
        wait_gather(m_cur, slot)
        for m in range(1, MOE_SB + 1):
            @pl.when(m_cur == m)
            def _():
                compute(m * MOE_BLK)

    @pl.when((n == 0) & (m_cur > 0))
    def _():
        @pl.when(has_prev)
        def _():
            wait_prev_out()

        yb_ref[...] = jnp.zeros_like(yb_ref)
        for m in range(1, MOE_SB + 1):
            @pl.when(m_cur == m)
            def _():
                write_out(m * MOE_BLK)

    @pl.when((m_cur == 0) & has_prev)
    def _():
        wait_prev_out()


def _moe(x_slots, plan, w_gate, w_up, w_down, n_out_rows):
    sb_e, sb_m, sb_n, sb_base = plan
    sb_rows = MOE_SB * MOE_BLK
    wspec = lambda s: pl.BlockSpec((None,) + s, lambda b, se, *_: (se[b], 0, 0))
    grid_spec = pltpu.PrefetchScalarGridSpec(
        num_scalar_prefetch=4,
        grid=(sb_e.shape[0],),
        in_specs=[pl.BlockSpec(memory_space=pl.ANY),
                  wspec((D_MODEL, D_EXPERT)), wspec((D_MODEL, D_EXPERT)), wspec((D_EXPERT, D_MODEL))],
        out_specs=pl.BlockSpec(memory_space=pl.ANY),
        scratch_shapes=[pltpu.VMEM((2, sb_rows, HALF), jnp.uint32), pltpu.VMEM((sb_rows, HALF), jnp.uint32),
                        pltpu.SemaphoreType.DMA((2,)), pltpu.SemaphoreType.DMA((1,))],
    )
    return pl.pallas_call(
        _moe_kernel,
        grid_spec=grid_spec,
        out_shape=jax.ShapeDtypeStruct((n_out_rows, HALF), jnp.uint32),
        compiler_params=pltpu.CompilerParams(dimension_semantics=("arbitrary",),
                                             vmem_limit_bytes=MOE_VMEM_LIMIT),
        name="moe_experts",
    )(sb_e, sb_m, sb_n, sb_base, x_slots, w_gate, w_up, w_down)


def _dispatch_kernel(dest_ref, zb_ref, hn_ref, out_hbm, buf_ref, zero_ref, sem, zsem):
    i = pl.program_id(0)
    nt = pl.num_programs(0)
    tm = hn_ref.shape[0]
    t_all = nt * tm
    slot = lax.rem(i, 2)
    nblk = zb_ref.shape[0]

    def zero_copy(blk):
        dst = pl.ds(pl.multiple_of(blk * MOE_BLK, MOE_BLK), MOE_BLK)
        return pltpu.make_async_copy(zero_ref, out_hbm.at[dst, :], zsem.at[0])

    @pl.when(i == 0)
    def _():
        zero_ref[...] = jnp.zeros_like(zero_ref)

        def start(blk, c):
            @pl.when(zb_ref[blk] == 1)
            def _():
                zero_copy(blk).start()
            return c

        def wait(blk, c):
            @pl.when(zb_ref[blk] == 1)
            def _():
                zero_copy(blk).wait()
            return c

        lax.fori_loop(0, nblk, start, 0)
        lax.fori_loop(0, nblk, wait, 0)

    def wait_rows(sl):
        for k in range(2):
            pltpu.make_async_copy(buf_ref.at[sl], out_hbm.at[pl.ds(0, tm), :], sem.at[sl]).wait()

    @pl.when(i >= 2)
    def _():
        wait_rows(slot)

    buf_ref[slot] = hn_ref[...]
    base = i * tm
    for a in range(2 * tm):
        row, k = a // 2, a % 2
        d = dest_ref[k * t_all + base + row]
        pltpu.make_async_copy(buf_ref.at[slot, pl.ds(row, 1), :], out_hbm.at[pl.ds(d, 1), :],
                              sem.at[slot]).start()

    @pl.when(i == nt - 1)
    def _():
        wait_rows(slot)

        @pl.when(nt > 1)
        def _():
            wait_rows(1 - slot)


def _dispatch(dest, zero_blk, hn_pk, n_slots):
    tm = TOK_TILE
    grid_spec = pltpu.PrefetchScalarGridSpec(
        num_scalar_prefetch=2,
        grid=(hn_pk.shape[0] // tm,),
        in_specs=[pl.BlockSpec((tm, HALF), lambda i, d, z: (i, 0))],
        out_specs=pl.BlockSpec(memory_space=pl.ANY),
        scratch_shapes=[pltpu.VMEM((2, tm, HALF), jnp.uint32), pltpu.VMEM((MOE_BLK, HALF), jnp.uint32),
                        pltpu.SemaphoreType.DMA((2,)), pltpu.SemaphoreType.DMA((1,))],
    )
    return pl.pallas_call(
        _dispatch_kernel,
        grid_spec=grid_spec,
        out_shape=jax.ShapeDtypeStruct((n_slots, HALF), jnp.uint32),
        compiler_params=pltpu.CompilerParams(dimension_semantics=("arbitrary",), vmem_limit_bytes=VMEM_LIMIT),
        name="moe_dispatch",
    )(dest, zero_blk, hn_pk)


def _moe_plan(rtt, cnt):
    t_all = rtt.shape[1]
    n_assign = 2 * t_all
    e_flat = rtt[0:2].astype(jnp.int32).reshape(-1)
    rank = rtt[4:6].astype(jnp.int32).reshape(-1)
    counts = cnt[0, 0:N_EXPERTS].astype(jnp.int32)
    padded = (counts + MOE_BLK - 1) // MOE_BLK * MOE_BLK
    pad_end = jnp.cumsum(padded)
    pad_start = pad_end - padded
    before = jnp.arange(N_EXPERTS, dtype=jnp.int32)[:, None] < e_flat[None, :]
    dest = rank + jnp.sum(jnp.where(before, padded[:, None], 0), axis=0)
    nblk = -(-(n_assign + N_EXPERTS * (MOE_BLK - 1)) // MOE_BLK)
    n_slots = nblk * MOE_BLK
    blk = jnp.arange(nblk, dtype=jnp.int32)
    last_of_expert = jnp.any((blk[:, None] == (pad_end // MOE_BLK - 1)[None, :]) & (counts > 0)[None, :], axis=1)
    zero_blk = (last_of_expert | (blk >= pad_end[-1] // MOE_BLK)).astype(jnp.int32)
    k_e = padded // MOE_BLK
    sbc = (k_e + MOE_SB - 1) // MOE_SB
    sb_end = jnp.cumsum(sbc)
    sb_start = sb_end - sbc
    n_sb = (nblk + (MOE_SB - 1) * N_EXPERTS) // MOE_SB
    s = jnp.arange(n_sb, dtype=jnp.int32)
    sb_e = jnp.minimum(jnp.sum((s[:, None] >= sb_end[None, :]).astype(jnp.int32), axis=1), N_EXPERTS - 1)
    j = s - sb_start[sb_e]
    real = s < sb_end[-1]
    tail_blk = pad_end[-1] // MOE_BLK + MOE_SB * (s - sb_end[-1])
    sb_m = jnp.where(real, jnp.clip(k_e[sb_e] - MOE_SB * j, 0, MOE_SB), jnp.clip(nblk - tail_blk, 0, MOE_SB))
    sb_n = jnp.where(real, jnp.clip(counts[sb_e] - MOE_SB * MOE_BLK * j, 0, MOE_SB * MOE_BLK), 0)
    sb_base = jnp.where(real, pad_start[sb_e] + MOE_SB * MOE_BLK * j, jnp.minimum(tail_blk, nblk - 1) * MOE_BLK)
    i32 = lambda a: a.astype(jnp.int32)
    return (i32(sb_e), i32(sb_m), i32(sb_n), i32(sb_base)), i32(dest), zero_blk, n_slots


def _ple_out_kernel(dest_ref, h1_ref, rt_ref, pp_ref, ps_ref, nple_ref, wpg_ref, wp_ref, nfin_ref, ys_hbm,
                    op_ref, os_ref, yb_ref, sem, *, n_prompt_tiles):
    i = pl.program_id(0)
    nt = pl.num_programs(0)
    tm = h1_ref.shape[0]
    t_all = nt * tm
    slot = lax.rem(i, 2)

    def issue(tile, sl):
        base = tile * tm

        def grp(g, c):
            for j in range(DMA_UNROLL):
                row = g * (DMA_UNROLL // 2) + j // 2
                d = dest_ref[(j % 2) * t_all + base + row]
                pltpu.make_async_copy(ys_hbm.at[pl.ds(d, 1), :], yb_ref.at[sl, j % 2, pl.ds(row, 1), :],
                                      sem.at[sl]).start()
            return c

        lax.fori_loop(0, 2 * tm // DMA_UNROLL, grp, 0)

    @pl.when(i == 0)
    def _():
        issue(0, 0)

    def wait_rows(sl):
        for k in range(2):
            pltpu.make_async_copy(ys_hbm.at[pl.ds(0, tm), :], yb_ref.at[sl, k], sem.at[sl]).wait()

    wait_rows(slot)
    is_p = i < n_prompt_tiles
    rt = rt_ref[...]
    w0, w1 = rt[:, 2:3], rt[:, 3:4]
    lo0, hi0 = _unpack_bf16_pair(yb_ref[slot, 0])
    lo1, hi1 = _unpack_bf16_pair(yb_ref[slot, 1])
    h2 = h1_ref[...] + jnp.concatenate([w0 * lo0 + w1 * lo1, w0 * hi0 + w1 * hi1], axis=1)
    hb = _rms(h2, nple_ref[...]).astype(BF16)
    nbase = jnp.minimum(i + 1, nt - 1) * tm
    n_chunk = 8
    cw, ca = D_MODEL // n_chunk, 2 * tm // n_chunk
    gates = []
    for c in range(n_chunk):
        gates.append(jax.nn.sigmoid(_dot(hb, wpg_ref[:, c * cw:(c + 1) * cw])))
        for a in range(c * ca, (c + 1) * ca):
            d = dest_ref[(a % 2) * t_all + nbase + a // 2]
            pltpu.make_async_copy(ys_hbm.at[pl.ds(d, 1), :], yb_ref.at[1 - slot, a % 2, pl.ds(a // 2, 1), :],
                                  sem.at[1 - slot]).start()
    gate = jnp.concatenate(gates, axis=1)

    @pl.when(i == nt - 1)
    def _():
        wait_rows(1 - slot)

    p = jnp.where(is_p, pp_ref[...], ps_ref[...])
    h3 = h2 + _dot(p.astype(BF16), wp_ref[...]) * gate
    y = _rms(h3, nfin_ref[...])

    @pl.when(is_p)
    def _():
        op_ref[...] = y

    @pl.when(jnp.logical_not(is_p))
    def _():
        os_ref[...] = y


def _ple_out(dest, h1, rt, pp, ps, nple, wpg, wp, nfin, y_slots):
    tm = TOK_TILE
    npt, nst = pp.shape[0] // tm, ps.shape[0] // tm
    row = lambda w: pl.BlockSpec((tm, w), lambda i, d: (i, 0))
    prow = lambda w: pl.BlockSpec((tm, w), lambda i, d: (jnp.minimum(i, npt - 1), 0))
    srow = lambda w: pl.BlockSpec((tm, w), lambda i, d: (jnp.maximum(i - npt, 0), 0))
    const = lambda shape: pl.BlockSpec(shape, lambda i, d: (0,) * len(shape), pipeline_mode=pl.Buffered(1))
    grid_spec = pltpu.PrefetchScalarGridSpec(
        num_scalar_prefetch=1,
        grid=(npt + nst,),
        in_specs=[row(D_MODEL), row(LANES), prow(D_PLE), srow(D_PLE), const((1, D_MODEL)), const(wpg.shape),
                  const(wp.shape), const((1, D_MODEL)), pl.BlockSpec(memory_space=pl.ANY)],
        out_specs=[prow(D_MODEL), srow(D_MODEL)],
        scratch_shapes=[pltpu.VMEM((2, 2, tm, HALF), jnp.uint32), pltpu.SemaphoreType.DMA((2,))],
    )
    return pl.pallas_call(
        functools.partial(_ple_out_kernel, n_prompt_tiles=npt),
        grid_spec=grid_spec,
        out_shape=[jax.ShapeDtypeStruct((pp.shape[0], D_MODEL), F32),
                   jax.ShapeDtypeStruct((ps.shape[0], D_MODEL), F32)],
        compiler_params=pltpu.CompilerParams(dimension_semantics=("arbitrary",), vmem_limit_bytes=VMEM_LIMIT),
        name="ple_out",
    )(dest, h1, rt, pp, ps, nple, wpg, wp, nfin, y_slots)


def _s5_params(lam_re, lam_im, log_dt, b_re, b_im, c_re, c_im, d_skip):
    dt = jnp.exp(log_dt)[:, None]
    mag = jnp.exp(lam_re * dt)
    ab_re = mag * jnp.cos(lam_im * dt)
    ab_im = mag * jnp.sin(lam_im * dt)
    den = lam_re * lam_re + lam_im * lam_im
    nr = ab_re - 1.0
    f_re = (nr * lam_re + ab_im * lam_im) / den
    f_im = (ab_im * lam_re - nr * lam_im) / den
    bb_re = f_re[..., None] * b_re - f_im[..., None] * b_im
    bb_im = f_re[..., None] * b_im + f_im[..., None] * b_re
    eye = jnp.eye(S5_GB, dtype=F32)

    def bu_w(bb):
        bb = bb.reshape(S5_NGB, S5_GB, S5_STATE, S5_GROUP)
        w = jnp.einsum('nlph,lm->nlhmp', bb, eye)
        return w.reshape(S5_NGB, S5_GB * S5_GROUP, S5_SL)

    def c_w(c):
        c = c.reshape(S5_NGB, S5_GB, S5_GROUP, S5_STATE)
        w = jnp.einsum('nlhp,lm->nlpmh', c, eye)
        return w.reshape(S5_NGB, S5_SL, S5_GB * S5_GROUP)

    wbu = jnp.concatenate([bu_w(bb_re), bu_w(bb_im)], axis=2).astype(BF16)
    wc = jnp.concatenate([c_w(c_re), -c_w(c_im)], axis=1).astype(BF16)
    a_re = ab_re.reshape(S5_NGB, 1, S5_SL)
    a_im = ab_im.reshape(S5_NGB, 1, S5_SL)
    dsk = d_skip.reshape(S5_NGB, 1, S5_GB * S5_GROUP)
    return wbu, wc, a_re, a_im, dsk


def kernel(x_prompt, x_sample, p_prompt, p_sample, state_gla, state_s5_re, state_s5_im, norm_mix, w_in, gla_w_gate_up, gla_gate_bias, gla_norm, s5_lam_re, s5_lam_im, s5_log_dt, s5_b_re, s5_b_im, s5_c_re, s5_c_im, s5_d, s5_glu_w, s5_glu_b, s5_norm, w_out, norm_ffn, router_group, router_expert, w_gate, w_up, w_down, norm_ple, w_ple, w_ple_gate, norm_final):
    depth = w_in.shape[0]
    assert depth == 1
    i = 0
    bp, lp, _ = x_prompt.shape
    bs, ls, _ = x_sample.shape
    tp, ts = bp * lp, bs * ls
    t_all = tp + ts

    wm = _w_in_prep(w_in[i])
    wgu = jnp.pad(gla_w_gate_up[i], ((0, LANES - GLA_RANK), (0, 0))).astype(BF16)
    gbias = gla_gate_bias[i].reshape(1, QK_W)
    wbu, wc, a_re, a_im, dsk = _s5_params(s5_lam_re[i], s5_lam_im[i], s5_log_dt[i], s5_b_re[i], s5_b_im[i],
                                          s5_c_re[i], s5_c_im[i], s5_d[i])
    glu_w = s5_glu_w[i].astype(BF16)
    wo = w_out[i].astype(BF16)
    wr32 = jnp.pad(jnp.concatenate([router_group[i], router_expert[i]], axis=1),
                   ((0, 0), (0, LANES - N_EGROUPS - N_EXPERTS)))
    wr_hi = wr32.astype(BF16)
    wr = jnp.stack([wr_hi, (wr32 - wr_hi.astype(F32)).astype(BF16)])
    wpg = w_ple_gate[i].astype(BF16)
    wp = w_ple[i].astype(BF16)
    vec = lambda a: a.reshape(1, -1)

    xp = x_prompt.reshape(tp, D_MODEL)
    xs = x_sample.reshape(ts, D_MODEL)

    qp, kp, vp, rp, lap, up = _in_proj(xp, vec(norm_mix[i]), wm, wgu, gbias, BF16)
    qs, ks, vs, rs, las, us = _in_proj(xs, vec(norm_mix[i]), wm, wgu, gbias, BF16)
    ogp, gla_p = _gla_prompt(qp, kp, vp, rp, lap, vec(gla_norm[i]), bp, lp)
    ogs, gla_s = _gla_sample(qs, ks, vs, rs, las, vec(gla_norm[i]), state_gla[i], bs, ls)
    wbu2, wc2, a_re2, a_im2 = _s5_prompt_params(wbu, wc, a_re, a_im, bp)
    ysp, re_p, im_p = _s5_prompt(up.reshape(bp, lp, D_S5), wbu2, wc2, a_re2, a_im2, dsk, tc=256)
    yss, re_s, im_s = _s5(us.reshape(bs, ls, D_S5), wbu, wc, a_re, a_im, dsk,
                          state_s5_re[i].reshape(bs, -1), state_s5_im[i].reshape(bs, -1),
                          nb=bs, tc=ls, has_state=True)

    h1, hn_all, rt_all, rtt, cnt = _mix_out(xp, xs, ogp, ogs, ysp.reshape(tp, D_S5), yss.reshape(ts, D_S5),
                                            glu_w, vec(s5_glu_b[i]), vec(s5_norm[i]), wo, vec(norm_ffn[i]), wr)

    plan, dest, zero_blk, n_slots = _moe_plan(rtt, cnt)
    x_slots = _dispatch(dest, zero_blk, hn_all, n_slots)
    y_slots = _moe(x_slots, plan, w_gate[i], w_up[i], w_down[i], n_slots)

    y_p, y_s = _ple_out(dest, h1, rt_all, p_prompt[i].reshape(tp, D_PLE), p_sample[i].reshape(ts, D_PLE),
                        vec(norm_ple[i]), wpg, wp, vec(norm_final), y_slots)

    s5shape = lambda a, b: a.reshape(1, b, S5_GROUPS, S5_STATE)
    return (y_p.reshape(bp, lp, D_MODEL), y_s.reshape(bs, ls, D_MODEL),
            gla_p[None], s5shape(re_p, bp), s5shape(im_p, bp),
            gla_s[None], s5shape(re_s, bs), s5shape(im_s, bs))
```

```python
import functools
import math

import jax
import jax.numpy as jnp
from jax import lax
from jax.experimental import pallas as pl
from jax.experimental.pallas import tpu as pltpu

F32 = jnp.float32
BF16 = jnp.bfloat16

D_MODEL = 2048
D_GLA = 1024
D_S5 = 1024
GLA_HEADS = 4
GLA_DV = 256
GLA_DK = 128
GLA_RANK = 16
GLA_CHUNK = 64
S5_GROUP = 16
S5_GROUPS = 64
S5_STATE = 64
N_EGROUPS = 4
N_EPG = 8
N_EXPERTS = 32
D_EXPERT = 512
D_PLE = 256
EPS = 1e-6

LANES = 128
QK_W = GLA_HEADS * GLA_DK
S5_GB = 8
S5_NGB = S5_GROUPS // S5_GB
S5_SL = S5_GB * S5_STATE
TOK_TILE = 256
MOE_BLK = 128
VMEM_LIMIT = 56 * 1024 * 1024
MOE_VMEM_LIMIT = 60 * 1024 * 1024


def _const_spec(shape):
    nd = len(shape)
    return pl.BlockSpec(shape, lambda *_: (0,) * nd, pipeline_mode=pl.Buffered(1))


def _rms(x, g):
    return x * lax.rsqrt(jnp.mean(x * x, axis=-1, keepdims=True) + EPS) * g


def _dot(a, b):
    return jnp.dot(a, b, preferred_element_type=F32)


def _log_sigmoid(x):
    return -(jnp.maximum(-x, 0.0) + jnp.log1p(jnp.exp(-jnp.abs(x))))


N_QKVR = 2 * QK_W + 2 * D_GLA
W_IN_COLS = N_QKVR + GLA_RANK + D_S5


def _w_in_prep_kernel(w_ref, o_ref):
    o_ref[:, 0:N_QKVR] = w_ref[:, 0:N_QKVR].astype(BF16)
    tail = w_ref[:, N_QKVR:W_IN_COLS]
    o_ref[:, N_QKVR:N_QKVR + D_S5] = tail[:, GLA_RANK:GLA_RANK + D_S5].astype(BF16)
    o_ref[:, N_QKVR + D_S5:N_QKVR + D_S5 + LANES] = tail[:, 0:LANES].astype(BF16)


def _w_in_prep(wi):
    rows = 256
    return pl.pallas_call(
        _w_in_prep_kernel,
        grid=(D_MODEL // rows,),
        in_specs=[pl.BlockSpec((rows, W_IN_COLS), lambda i: (i, 0))],
        out_specs=pl.BlockSpec((rows, N_QKVR + D_S5 + LANES), lambda i: (i, 0)),
        out_shape=jax.ShapeDtypeStruct((D_MODEL, N_QKVR + D_S5 + LANES), BF16),
        compiler_params=pltpu.CompilerParams(dimension_semantics=("arbitrary",), vmem_limit_bytes=VMEM_LIMIT),
        name="w_in_prep",
    )(wi)


def _in_proj_kernel(x_ref, g_ref, wm_ref, wgu_ref, gb_ref,
                    q_ref, k_ref, v_ref, r_ref, la_ref, u_ref):
    hb = _rms(x_ref[...], g_ref[...]).astype(BF16)

    def seg(a, b):
        return _dot(hb, wm_ref[:, a:b])

    q_ref[...] = (seg(0, QK_W) * (GLA_DK ** -0.5)).astype(q_ref.dtype)
    k_ref[...] = seg(QK_W, 2 * QK_W).astype(k_ref.dtype)
    v_ref[...] = seg(2 * QK_W, 2 * QK_W + D_GLA).astype(v_ref.dtype)
    r_ref[...] = seg(2 * QK_W + D_GLA, N_QKVR).astype(r_ref.dtype)
    u_ref[...] = seg(N_QKVR, N_QKVR + D_S5)
    zg = seg(N_QKVR + D_S5, N_QKVR + D_S5 + LANES)
    xg = _dot(zg.astype(BF16), wgu_ref[...]) + gb_ref[...]
    la_ref[...] = _log_sigmoid(xg) * (1.0 / 16.0)


def _in_proj(x2d, g, wm, wgu, gbias, act_dtype):
    t = x2d.shape[0]
    tm = TOK_TILE
    row = lambda w: pl.BlockSpec((tm, w), lambda i: (i, 0))
    return pl.pallas_call(
        _in_proj_kernel,
        grid=(t // tm,),
        in_specs=[row(D_MODEL), _const_spec((1, D_MODEL)), _const_spec(wm.shape),
                  _const_spec(wgu.shape), _const_spec((1, QK_W))],
        out_specs=[row(QK_W), row(QK_W), row(D_GLA), row(D_GLA), row(QK_W), row(D_S5)],
        out_shape=[jax.ShapeDtypeStruct((t, QK_W), act_dtype), jax.ShapeDtypeStruct((t, QK_W), act_dtype),
                   jax.ShapeDtypeStruct((t, D_GLA), act_dtype), jax.ShapeDtypeStruct((t, D_GLA), act_dtype),
                   jax.ShapeDtypeStruct((t, QK_W), F32), jax.ShapeDtypeStruct((t, D_S5), F32)],
        compiler_params=pltpu.CompilerParams(dimension_semantics=("arbitrary",), vmem_limit_bytes=VMEM_LIMIT),
        name="in_proj",
    )(x2d, g, wm, wgu, gbias)


_NT = (((1,), (1,)), ((), ()))
_TN = (((0,), (0,)), ((), ()))


def _gla_pre(q, k, la, c):
    r = q.shape[0]
    shift = int(math.log2(c))
    ri = lax.broadcasted_iota(jnp.int32, (r, r), 0)
    si = lax.broadcasted_iota(jnp.int32, (r, r), 1)
    mask = ((ri >> shift) == (si >> shift)) & (ri >= si)
    tri = jnp.where(mask, 1.0, 0.0).astype(BF16)
    hi = la.astype(BF16)
    r1 = la - hi.astype(F32)
    mid = r1.astype(BF16)
    lo = (r1 - mid.astype(F32)).astype(BF16)
    cum = _dot(tri, hi) + _dot(tri, mid) + _dot(tri, lo)
    last = jnp.concatenate([jnp.broadcast_to(cum[(i + 1) * c - 1:(i + 1) * c, :], (c, cum.shape[1]))
                            for i in range(r // c)], axis=0)
    qe = (q * jnp.exp(cum)).astype(BF16)
    ke = (k * jnp.exp(-cum)).astype(BF16)
    kd = (k * jnp.exp(last - cum)).astype(BF16)
    return qe, ke, kd, cum, mask


def _gla_intra(qe, ke, v, mask):
    sc = lax.dot_general(qe, ke, _NT, preferred_element_type=F32)
    return _dot(jnp.where(mask, sc, 0.0).astype(BF16), v)


def _gla_finish(o, r, g):
    rf = r.astype(F32)
    return _rms(o, g) * (rf * jax.nn.sigmoid(rf))


GLA_SAFE_LOG_DECAY = -60.0


def _gla_token_step(t, rows, q, k, v, a_all, st_ref, oacc_ref):
    m = rows == t
    a = jnp.sum(jnp.where(m, a_all, 0.0), axis=0, keepdims=True)
    kt = jnp.where(m, k, 0.0).astype(BF16)
    qt = jnp.where(m, q, 0.0).astype(BF16)
    vt = jnp.where(m, v, jnp.zeros_like(v))
    st = st_ref[...] * a + lax.dot_general(vt, kt, _TN, preferred_element_type=F32)
    st_ref[...] = st
    oacc_ref[...] += lax.dot_general(qt, st.astype(BF16), _NT, preferred_element_type=F32)


def _gla_prompt_kernel(q_ref, k_ref, v_ref, r_ref, la_ref, g_ref, o_ref, sfin_ref, st_ref, oacc_ref,
                       *, n_chunks):
    j = pl.program_id(1)
    nseq = q_ref.shape[0]

    @pl.when(j == 0)
    def _():
        st_ref[...] = jnp.zeros_like(st_ref)

    c = GLA_CHUNK
    pre = [_gla_pre(q_ref[s].astype(F32), k_ref[s].astype(F32), la_ref[s], c) for s in range(nseq)]
    g = g_ref[...]
    safe = functools.reduce(jnp.minimum, [jnp.min(p[3]) for p in pre]) >= GLA_SAFE_LOG_DECAY

    @pl.when(safe)
    def _():
        for s in range(nseq):
            qe, ke, kd, cum, mask = pre[s]
            for h in range(GLA_HEADS):
                kc = slice(h * GLA_DK, (h + 1) * GLA_DK)
                vc = slice(h * GLA_DV, (h + 1) * GLA_DV)
                v = v_ref[s, :, vc]
                qe_h, kd_h = qe[:, kc], kd[:, kc]
                o = _gla_intra(qe_h, ke[:, kc], v, mask)
                st = st_ref[s, h]
                inter = []
                for ci in range(n_chunks):
                    rows = slice(ci * c, (ci + 1) * c)
                    inter.append(lax.dot_general(qe_h[rows], st.astype(BF16), _NT, preferred_element_type=F32))
                    dec = jnp.exp(cum[(ci + 1) * c - 1:(ci + 1) * c, kc])
                    st = st * dec + lax.dot_general(v[rows], kd_h[rows], _TN, preferred_element_type=F32)
                st_ref[s, h] = st
                o = o + jnp.concatenate(inter, axis=0)
                o_ref[s, :, vc] = _gla_finish(o, r_ref[s, :, vc], g).astype(o_ref.dtype)

    @pl.when(jnp.logical_not(safe))
    def _():
        n_rows = n_chunks * c
        rows = lax.broadcasted_iota(jnp.int32, (n_rows, 1), 0)
        for s in range(nseq):
            for h in range(GLA_HEADS):
                kc = slice(h * GLA_DK, (h + 1) * GLA_DK)
                vc = slice(h * GLA_DV, (h + 1) * GLA_DV)
                q, k, v = q_ref[s, :, kc].astype(F32), k_ref[s, :, kc].astype(F32), v_ref[s, :, vc]
                a_all = jnp.exp(la_ref[s, :, kc])
                oacc_ref[...] = jnp.zeros_like(oacc_ref)

                def step(t, carry, s=s, h=h, q=q, k=k, v=v, a_all=a_all):
                    _gla_token_step(t, rows, q, k, v, a_all, st_ref.at[s, h], oacc_ref)
                    return carry

                lax.fori_loop(0, n_rows, step, 0)
                o_ref[s, :, vc] = _gla_finish(oacc_ref[...], r_ref[s, :, vc], g).astype(o_ref.dtype)

    @pl.when(j == pl.num_programs(1) - 1)
    def _():
        for s in range(nseq):
            for h in range(GLA_HEADS):
                sfin_ref[s, h] = st_ref[s, h].T


def _gla_prompt(q, k, v, r, la, g, batch, seq):
    rb = 4 * GLA_CHUNK
    ns = 2
    row = lambda w: pl.BlockSpec((ns, rb, w), lambda b, j: (b, j, 0))
    seq3 = lambda a: a.reshape(batch, seq, a.shape[-1])
    o, sfin = pl.pallas_call(
        functools.partial(_gla_prompt_kernel, n_chunks=rb // GLA_CHUNK),
        grid=(batch // ns, seq // rb),
        in_specs=[row(QK_W), row(QK_W), row(D_GLA), row(D_GLA), row(QK_W), _const_spec((1, GLA_DV))],
        out_specs=[row(D_GLA),
                   pl.BlockSpec((ns, GLA_HEADS, GLA_DK, GLA_DV), lambda b, j: (b, 0, 0, 0))],
        out_shape=[jax.ShapeDtypeStruct((batch, seq, D_GLA), BF16),
                   jax.ShapeDtypeStruct((batch, GLA_HEADS, GLA_DK, GLA_DV), F32)],
        scratch_shapes=[pltpu.VMEM((ns, GLA_HEADS, GLA_DV, GLA_DK), F32), pltpu.VMEM((rb, GLA_DV), F32)],
        compiler_params=pltpu.CompilerParams(dimension_semantics=("arbitrary", "arbitrary"),
                                             vmem_limit_bytes=VMEM_LIMIT),
        name="gla_prompt",
    )(seq3(q), seq3(k), seq3(v), seq3(r), seq3(la), g)
    return o.reshape(batch * seq, D_GLA), sfin


def _gla_sample_kernel(q_ref, k_ref, v_ref, r_ref, la_ref, g_ref, s0_ref, o_ref, sfin_ref, st_ref, oacc_ref,
                       *, n_seq, seq):
    qe, ke, kd, cum, mask = _gla_pre(q_ref[...].astype(F32), k_ref[...].astype(F32), la_ref[...], seq)
    g = g_ref[...]
    safe = jnp.min(cum) >= GLA_SAFE_LOG_DECAY

    @pl.when(safe)
    def _():
        lasts = jnp.concatenate([cum[(s + 1) * seq - 1:(s + 1) * seq, :] for s in range(n_seq)]
                                + [jnp.zeros((GLA_DK - n_seq, cum.shape[1]), F32)], axis=0)
        pair = 2 * seq
        upper = lax.broadcasted_iota(jnp.int32, (pair, GLA_DK), 0) < seq
        for h in range(GLA_HEADS):
            kc = slice(h * GLA_DK, (h + 1) * GLA_DK)
            vc = slice(h * GLA_DV, (h + 1) * GLA_DV)
            v = v_ref[:, vc]
            qe_h, kd_h = qe[:, kc], kd[:, kc]
            o = _gla_intra(qe_h, ke[:, kc], v, mask)
            dec_t = jnp.exp(lasts[:, kc].T)
            inter = []
            for p in range(n_seq // 2):
                rows = slice(p * pair, (p + 1) * pair)
                qe_p, kd_p, v_p = qe_h[rows], kd_h[rows], v[rows]
                for half in range(2):
                    s = 2 * p + half
                    s0 = s0_ref[s, h]
                    o_s = _dot(qe_p, s0.astype(BF16))
                    inter.append(o_s[half * seq:(half + 1) * seq])
                    kd_s = jnp.where(upper if half == 0 else ~upper, kd_p, jnp.zeros_like(kd_p))
                    dec = jnp.broadcast_to(dec_t[:, s:s + 1], (GLA_DK, GLA_DV))
                    sfin_ref[s, h] = s0 * dec + lax.dot_general(kd_s, v_p, _TN, preferred_element_type=F32)
            o = o + jnp.concatenate(inter, axis=0)
            o_ref[:, vc] = _gla_finish(o, r_ref[:, vc], g).astype(o_ref.dtype)

    @pl.when(jnp.logical_not(safe))
    def _():
        n_rows = n_seq * seq
        rows = lax.broadcasted_iota(jnp.int32, (n_rows, 1), 0)
        for h in range(GLA_HEADS):
            kc = slice(h * GLA_DK, (h + 1) * GLA_DK)
            vc = slice(h * GLA_DV, (h + 1) * GLA_DV)
            q, k, v = q_ref[:, kc].astype(F32), k_ref[:, kc].astype(F32), v_ref[:, vc]
            a_all = jnp.exp(la_ref[:, kc])
            oacc_ref[...] = jnp.zeros_like(oacc_ref)

            def step(t, carry):
                s = t // seq

                @pl.when(t % seq == 0)
                def _():
                    st_ref[...] = s0_ref[s, h].T

                _gla_token_step(t, rows, q, k, v, a_all, st_ref, oacc_ref)

                @pl.when(t % seq == seq - 1)
                def _():
                    sfin_ref[s, h] = st_ref[...].T

                return carry

            lax.fori_loop(0, n_rows, step, 0)
            o_ref[:, vc] = _gla_finish(oacc_ref[...], r_ref[:, vc], g).astype(o_ref.dtype)


def _gla_sample(q, k, v, r, la, g, s0, batch, seq):
    ns = 16
    rb = ns * seq
    row = lambda w: pl.BlockSpec((rb, w), lambda i: (i, 0))
    st = pl.BlockSpec((ns, GLA_HEADS, GLA_DK, GLA_DV), lambda i: (i, 0, 0, 0))
    return pl.pallas_call(
        functools.partial(_gla_sample_kernel, n_seq=ns, seq=seq),
        grid=(batch // ns,),
        in_specs=[row(QK_W), row(QK_W), row(D_GLA), row(D_GLA), row(QK_W), _const_spec((1, GLA_DV)), st],
        out_specs=[row(D_GLA), st],
        out_shape=[jax.ShapeDtypeStruct((batch * seq, D_GLA), BF16),
                   jax.ShapeDtypeStruct((batch, GLA_HEADS, GLA_DK, GLA_DV), F32)],
        scratch_shapes=[pltpu.VMEM((GLA_DV, GLA_DK), F32), pltpu.VMEM((rb, GLA_DV), F32)],
        compiler_params=pltpu.CompilerParams(dimension_semantics=("arbitrary",), vmem_limit_bytes=VMEM_LIMIT),
        name="gla_sample",
    )(q, k, v, r, la, g, s0)


def _s5_kernel(u_ref, wbu_ref, wc_ref, are_ref, aim_ref, d_ref, h0r_ref, h0i_ref,
               y_ref, sre_ref, sim_ref, bu_ref, xs_ref, car_ref, *, nb, tc, has_state):
    j = pl.program_id(2)

    @pl.when(j == 0)
    def _():
        if has_state:
            car_ref[0] = h0r_ref[...]
            car_ref[1] = h0i_ref[...]
        else:
            car_ref[...] = jnp.zeros_like(car_ref)

    u2 = u_ref[...].reshape(nb * tc, LANES)
    ub = u2.astype(BF16)
    nl = S5_SL // LANES
    for l in range(2 * nl):
        bu_ref[l] = _dot(ub, wbu_ref[0, :, l * LANES:(l + 1) * LANES])
    a_r = [jnp.broadcast_to(are_ref[0, :, l * LANES:(l + 1) * LANES], (nb, LANES)) for l in range(nl)]
    a_i = [jnp.broadcast_to(aim_ref[0, :, l * LANES:(l + 1) * LANES], (nb, LANES)) for l in range(nl)]

    def step(t, carry):
        rows = pl.ds(t, nb, stride=tc)
        out = []
        for l in range(nl):
            xr, xi = carry[2 * l], carry[2 * l + 1]
            nr = a_r[l] * xr - a_i[l] * xi + bu_ref[l, rows, :]
            ni = a_r[l] * xi + a_i[l] * xr + bu_ref[nl + l, rows, :]
            xs_ref[l, rows, :] = nr
            xs_ref[nl + l, rows, :] = ni
            out += [nr, ni]
        return tuple(out)

    init = []
    for l in range(nl):
        init += [car_ref[0, :, l * LANES:(l + 1) * LANES], car_ref[1, :, l * LANES:(l + 1) * LANES]]
    fin = lax.fori_loop(0, tc, step, tuple(init), unroll=8)
    xr = jnp.concatenate([fin[2 * l] for l in range(nl)], axis=1)
    xi = jnp.concatenate([fin[2 * l + 1] for l in range(nl)], axis=1)
    car_ref[0] = xr
    car_ref[1] = xi
    y = d_ref[0] * u2
    for l in range(2 * nl):
        y = y + _dot(xs_ref[l].astype(BF16), wc_ref[0, l * LANES:(l + 1) * LANES, :])
    y_ref[...] = y.reshape(nb, tc, LANES)

    @pl.when(j == pl.num_programs(2) - 1)
    def _():
        sre_ref[...] = xr
        sim_ref[...] = xi


def _s5(u3d, wbu, wc, a_re, a_im, dsk, h0r, h0i, nb, tc, has_state):
    batch, seq, _ = u3d.shape
    grid = (S5_NGB, batch // nb, seq // tc)
    st = pl.BlockSpec((nb, S5_SL), lambda g, b, j: (b, g))
    par = lambda w: pl.BlockSpec((1, 1, w), lambda g, b, j: (g, 0, 0))
    ublk = pl.BlockSpec((nb, tc, LANES), lambda g, b, j: (b, j, g))
    return pl.pallas_call(
        functools.partial(_s5_kernel, nb=nb, tc=tc, has_state=has_state),
        grid=grid,
        in_specs=[ublk,
                  pl.BlockSpec((1, LANES, 2 * S5_SL), lambda g, b, j: (g, 0, 0)),
                  pl.BlockSpec((1, 2 * S5_SL, LANES), lambda g, b, j: (g, 0, 0)),
                  par(S5_SL), par(S5_SL), par(LANES), st, st],
        out_specs=[ublk, st, st],
        out_shape=[jax.ShapeDtypeStruct(u3d.shape, F32),
                   jax.ShapeDtypeStruct((batch, S5_GROUPS * S5_STATE), F32),
                   jax.ShapeDtypeStruct((batch, S5_GROUPS * S5_STATE), F32)],
        scratch_shapes=[pltpu.VMEM((2 * S5_SL // LANES, nb * tc, LANES), F32),
                        pltpu.VMEM((2 * S5_SL // LANES, nb * tc, LANES), F32),
                        pltpu.VMEM((2, nb, S5_SL), F32)],
        compiler_params=pltpu.CompilerParams(dimension_semantics=("arbitrary",) * 3,
                                             vmem_limit_bytes=VMEM_LIMIT),
        name="s5_state" if has_state else "s5_zero",
    )(u3d, wbu, wc, a_re, a_im, dsk, h0r, h0i)


S5_J = 2
S5_HL = S5_SL // S5_J
S5_PAIR = 2


def _s5_prompt_kernel(u_ref, wbu_ref, wc_ref, are_ref, aim_ref, d_ref, y_ref, sre_ref, sim_ref,
                      u2_ref, lhs_ref, bu_ref, xs_ref, y2_ref, yo_ref, car_ref, *, nb, tc):
    g = pl.program_id(0)
    j = pl.program_id(1)
    rows = nb * S5_J
    npair = S5_PAIR

    @pl.when((g == 0) & (j == 0))
    def _():
        lhs_ref[...] = jnp.zeros_like(lhs_ref)

    @pl.when(j == 0)
    def _():
        car_ref[...] = jnp.zeros_like(car_ref)

    u2 = u_ref[...].reshape(nb * tc, npair * LANES)
    for p in range(npair):
        u2_ref[p] = u2[:, p * LANES:(p + 1) * LANES]

    def build(t, c):
        for p in range(npair):
            u4 = u2_ref[p, pl.ds(t, nb, stride=tc), :]
            for jj in range(S5_J):
                lhs_ref[p, t, jj * nb:(jj + 1) * nb, jj * LANES:(jj + 1) * LANES] = u4
        return c

    lax.fori_loop(0, tc, build, 0, unroll=True)
    for p in range(npair):
        lhs = lhs_ref[p].reshape(tc * rows, S5_J * LANES).astype(BF16)
        bu_ref[p] = _dot(lhs, wbu_ref[p]).reshape(tc, rows, 2 * S5_HL)
    ar = [are_ref[p] for p in range(npair)]
    ai = [aim_ref[p] for p in range(npair)]

    def step(t, carry):
        out = []
        for p in range(npair):
            xr, xi = carry[2 * p], carry[2 * p + 1]
            tile = bu_ref[p, t]
            nr = ar[p] * xr - ai[p] * xi + tile[:, 0:S5_HL]
            ni = ar[p] * xi + ai[p] * xr + tile[:, S5_HL:2 * S5_HL]
            xs_ref[p, t] = jnp.concatenate([nr, ni], axis=1)
            out += [nr, ni]
        return tuple(out)

    init = tuple(car_ref[p, c] for p in range(npair) for c in range(2))
    fin = lax.fori_loop(0, tc, step, init, unroll=True)
    for p in range(npair):
        car_ref[p, 0] = fin[2 * p]
        car_ref[p, 1] = fin[2 * p + 1]
        xs = xs_ref[p].reshape(tc * rows, 2 * S5_HL).astype(BF16)
        y2_ref[p] = _dot(xs, wc_ref[p]).reshape(tc, rows, S5_J * LANES)
    first_half = lax.broadcasted_iota(jnp.int32, (rows, LANES), 0) < nb

    def unperm(t, c):
        for p in range(npair):
            t2 = y2_ref[p, t]
            part = jnp.where(first_half, t2[:, 0:LANES], t2[:, LANES:2 * LANES])
            yo_ref[p, pl.ds(t, nb, stride=tc), :] = (part + pltpu.roll(part, nb, axis=0))[0:nb]
        return c

    lax.fori_loop(0, tc, unperm, 0, unroll=True)
    y = jnp.concatenate([yo_ref[p] + d_ref[p] * u2_ref[p] for p in range(npair)], axis=1)
    y_ref[...] = y.reshape(nb, tc, npair * LANES)

    @pl.when(j == pl.num_programs(1) - 1)
    def _():
        def gather_state(x):
            return jnp.concatenate([x[jj * nb:(jj + 1) * nb] for jj in range(S5_J)], axis=1)

        sre_ref[...] = jnp.concatenate([gather_state(fin[2 * p]) for p in range(npair)], axis=1)
        sim_ref[...] = jnp.concatenate([gather_state(fin[2 * p + 1]) for p in range(npair)], axis=1)


def _s5_prompt(u3d, wbu2, wc2, a_re2, a_im2, dsk, tc):
    nb, seq, _ = u3d.shape
    rows = nb * S5_J
    assert rows == 8
    np_ = S5_PAIR
    st = pl.BlockSpec((nb, np_ * S5_SL), lambda g, j: (0, g))
    gblk = lambda s: pl.BlockSpec((np_,) + s, lambda g, j: (g, 0, 0))
    ublk = pl.BlockSpec((nb, tc, np_ * LANES), lambda g, j: (0, j, g))
    return pl.pallas_call(
        functools.partial(_s5_prompt_kernel, nb=nb, tc=tc),
        grid=(S5_NGB // np_, seq // tc),
        in_specs=[ublk, gblk((S5_J * LANES, 2 * S5_HL)), gblk((2 * S5_HL, S5_J * LANES)),
                  gblk((rows, S5_HL)), gblk((rows, S5_HL)), gblk((1, LANES))],
        out_specs=[ublk, st, st],
        out_shape=[jax.ShapeDtypeStruct(u3d.shape, F32),
                   jax.ShapeDtypeStruct((nb, S5_GROUPS * S5_STATE), F32),
                   jax.ShapeDtypeStruct((nb, S5_GROUPS * S5_STATE), F32)],
        scratch_shapes=[pltpu.VMEM((np_, nb * tc, LANES), F32),
                        pltpu.VMEM((np_, tc, rows, S5_J * LANES), F32),
                        pltpu.VMEM((np_, tc, rows, 2 * S5_HL), F32),
                        pltpu.VMEM((np_, tc, rows, 2 * S5_HL), F32),
                        pltpu.VMEM((np_, tc, rows, S5_J * LANES), F32),
                        pltpu.VMEM((np_, nb * tc, LANES), F32),
                        pltpu.VMEM((np_, 2, rows, S5_HL), F32)],
        compiler_params=pltpu.CompilerParams(dimension_semantics=("arbitrary",) * 2,
                                             vmem_limit_bytes=VMEM_LIMIT),
        name="s5_prompt",
    )(u3d, wbu2, wc2, a_re2, a_im2, dsk)


def _s5_prompt_params(wbu, wc, a_re, a_im, nb):
    h = S5_HL
    top = jnp.concatenate([wbu[:, :, 0:h], wbu[:, :, S5_SL:S5_SL + h]], axis=2)
    bot = jnp.concatenate([wbu[:, :, h:2 * h], wbu[:, :, S5_SL + h:S5_SL + 2 * h]], axis=2)
    wbu2 = jnp.concatenate([top, bot], axis=1)
    wc_j = [jnp.concatenate([wc[:, jj * h:(jj + 1) * h, :], wc[:, S5_SL + jj * h:S5_SL + (jj + 1) * h, :]],
                            axis=1) for jj in range(S5_J)]
    wc2 = jnp.concatenate(wc_j, axis=2)
    tile = lambda a: jnp.repeat(a.reshape(S5_NGB, S5_J, h), nb, axis=1)
    return wbu2, wc2, tile(a_re), tile(a_im)


def _mix_out_kernel(xp_ref, xs_ref, ogp_ref, ogs_ref, ysp_ref, yss_ref, glu_w_ref, glu_b_ref, s5n_ref, wo_ref,
                    nffn_ref, wr_ref, h1_ref, hn_ref, rt_ref, rtt_ref, cnt_ref, *, n_prompt_tiles):
    is_p = pl.program_id(0) < n_prompt_tiles
    x = jnp.where(is_p, xp_ref[...], xs_ref[...])
    og = jnp.where(is_p, ogp_ref[...], ogs_ref[...])
    y = jax.nn.gelu(jnp.where(is_p, ysp_ref[...], yss_ref[...]))
    y = y * jax.nn.sigmoid(_dot(y.astype(BF16), glu_w_ref[...]) + glu_b_ref[...])
    yn = _rms(y, s5n_ref[...]).astype(BF16)
    mix = _dot(og, wo_ref[0:D_GLA, :]) + _dot(yn, wo_ref[D_GLA:D_GLA + D_S5, :])
    h1 = x + mix
    h1_ref[...] = h1
    hn = _rms(h1, nffn_ref[...])
    hn_ref[...] = _pack_bf16_pair(hn[:, 0:HALF], hn[:, HALF:D_MODEL])
    hn_hi = hn.astype(BF16)
    hn_lo = (hn - hn_hi.astype(F32)).astype(BF16)
    logits = _dot(hn_hi, wr_ref[0]) + _dot(hn_hi, wr_ref[1]) + _dot(hn_lo, wr_ref[0])
    rt = _route(logits)

    @pl.when(pl.program_id(0) == 0)
    def _():
        cnt_ref[...] = jnp.zeros_like(cnt_ref)

    tm = rt.shape[0]
    lane = lax.broadcasted_iota(jnp.int32, rt.shape, 1).astype(F32)
    oh0 = lane == rt[:, 0:1]
    oh1 = lane == rt[:, 1:2]
    both = jnp.where(oh0 | oh1, 1.0, 0.0)
    ri = lax.broadcasted_iota(jnp.int32, (tm, tm), 0)
    ci = lax.broadcasted_iota(jnp.int32, (tm, tm), 1)
    before = _dot(jnp.where(ri > ci, 1.0, 0.0).astype(BF16), both.astype(BF16)) + cnt_ref[...]
    rank0 = jnp.sum(jnp.where(oh0, before, 0.0), axis=-1, keepdims=True)
    rank1 = jnp.sum(jnp.where(oh1, before, 0.0), axis=-1, keepdims=True)
    cnt_ref[...] += jnp.sum(both, axis=0, keepdims=True)
    rt = jnp.where(lane == 4.0, rank0, jnp.where(lane == 5.0, rank1, rt))
    rt_ref[...] = rt
    rtt_ref[...] = rt.T[0:8, :]


def _route(logits):
    col = lax.broadcasted_iota(jnp.int32, logits.shape, 1)
    colf = col.astype(F32)
    neg = -jnp.inf

    def first_argmax(vals):
        m = jnp.max(vals, axis=-1, keepdims=True)
        idx = jnp.min(jnp.where(vals == m, colf, float(LANES)), axis=-1, keepdims=True)
        return m, idx

    lg = jnp.where(col < N_EGROUPS, logits, neg)
    gmax, gsel = first_argmax(lg)
    p_g = 1.0 / jnp.sum(jnp.exp(lg - gmax), axis=-1, keepdims=True)
    ecol = col - N_EGROUPS
    egrp = (ecol >> int(math.log2(N_EPG))).astype(F32)
    in_group = (ecol >= 0) & (ecol < N_EXPERTS) & (egrp == gsel)
    le = jnp.where(in_group, logits, neg)
    m1, i1 = first_argmax(le)
    le2 = jnp.where(colf == i1, neg, le)
    m2, i2 = first_argmax(le2)
    e2 = jnp.exp(m2 - m1)
    den = 1.0 + e2
    w1 = p_g * (1.0 / den)
    w2 = p_g * (e2 / den)
    e1f = i1 - float(N_EGROUPS)
    e2f = i2 - float(N_EGROUPS)
    out = jnp.where(col == 0, e1f, jnp.where(col == 1, e2f, jnp.where(col == 2, w1, jnp.where(col == 3, w2, 0.0))))
    return out


def _mix_out(xp, xs, ogp, ogs, ysp, yss, glu_w, glu_b, s5n, wo, nffn, wr):
    tm = TOK_TILE
    npt, nst = xp.shape[0] // tm, xs.shape[0] // tm
    t = (npt + nst) * tm
    row = lambda w: pl.BlockSpec((tm, w), lambda i: (i, 0))
    prow = lambda w: pl.BlockSpec((tm, w), lambda i: (jnp.minimum(i, npt - 1), 0))
    srow = lambda w: pl.BlockSpec((tm, w), lambda i: (jnp.maximum(i - npt, 0), 0))
    return pl.pallas_call(
        functools.partial(_mix_out_kernel, n_prompt_tiles=npt),
        grid=(npt + nst,),
        in_specs=[prow(D_MODEL), srow(D_MODEL), prow(D_GLA), srow(D_GLA), prow(D_S5), srow(D_S5),
                  _const_spec(glu_w.shape), _const_spec((1, D_S5)), _const_spec((1, D_S5)),
                  _const_spec(wo.shape), _const_spec((1, D_MODEL)), _const_spec(wr.shape)],
        out_specs=[row(D_MODEL), row(HALF), row(LANES), pl.BlockSpec((8, tm), lambda i: (0, i)),
                   pl.BlockSpec((1, LANES), lambda i: (0, 0))],
        out_shape=[jax.ShapeDtypeStruct((t, D_MODEL), F32), jax.ShapeDtypeStruct((t, HALF), jnp.uint32),
                   jax.ShapeDtypeStruct((t, LANES), F32), jax.ShapeDtypeStruct((8, t), F32),
                   jax.ShapeDtypeStruct((1, LANES), F32)],
        compiler_params=pltpu.CompilerParams(dimension_semantics=("arbitrary",), vmem_limit_bytes=VMEM_LIMIT),
        name="mix_out",
    )(xp, xs, ogp, ogs, ysp, yss, glu_w, glu_b, s5n, wo, nffn, wr)


DMA_UNROLL = 8


MOE_SB = 6
HALF = D_MODEL // 2


def _pack_bf16_pair(lo, hi):
    def bits(x):
        b = pltpu.bitcast(x, jnp.uint32)
        return (b + jnp.uint32(0x7FFF) + ((b >> 16) & jnp.uint32(1))) >> 16
    return bits(lo) | (bits(hi) << 16)


def _unpack_bf16_pair(u):
    return (pltpu.bitcast(u << 16, F32), pltpu.bitcast(u & jnp.uint32(0xFFFF0000), F32))


def _moe_kernel(se_ref, sm_ref, sn_ref, sbase_ref, xs_hbm, wg_ref, wu_ref, wd_ref, out_hbm,
                xb_ref, yb_ref, gsem, ssem):
    b = pl.program_id(0)
    nb = pl.num_programs(0)
    n = sn_ref[b]
    slot = lax.rem(b, 2)

    def issue_gather(blk, sl):
        base = pl.multiple_of(sbase_ref[blk], MOE_BLK)
        for m in range(1, MOE_SB + 1):
            @pl.when((sn_ref[blk] > 0) & (sm_ref[blk] == m))
            def _():
                rows = m * MOE_BLK
                pltpu.make_async_copy(xs_hbm.at[pl.ds(base, rows), :], xb_ref.at[sl, pl.ds(0, rows), :],
                                      gsem.at[sl]).start()

    def wait_gather(blocks, sl):
        rows = pl.ds(0, pl.multiple_of(blocks * MOE_BLK, MOE_BLK))
        pltpu.make_async_copy(xs_hbm.at[rows, :], xb_ref.at[sl, rows, :], gsem.at[sl]).wait()

    m_cur = sm_ref[b]
    m_prev = sm_ref[jnp.maximum(b - 1, 0)]
    has_prev = (b > 0) & (m_prev > 0)

    def out_copy(rows):
        dst = pl.ds(pl.multiple_of(sbase_ref[b], MOE_BLK), rows)
        return pltpu.make_async_copy(yb_ref.at[pl.ds(0, rows), :], out_hbm.at[dst, :], ssem.at[0])

    def wait_prev_out():
        rows = pl.ds(0, pl.multiple_of(m_prev * MOE_BLK, MOE_BLK))
        pltpu.make_async_copy(yb_ref.at[rows, :], out_hbm.at[rows, :], ssem.at[0]).wait()

    def write_out(rows):
        out_copy(rows).start()

        @pl.when(b == nb - 1)
        def _():
            out_copy(rows).wait()

    @pl.when(b == 0)
    def _():
        issue_gather(0, 0)

    @pl.when(b + 1 < nb)
    def _():
        issue_gather(b + 1, 1 - slot)

    def compute(rows):
        x_lo, x_hi = _unpack_bf16_pair(xb_ref[slot, 0:rows, :])
        x_lo, x_hi = x_lo.astype(BF16), x_hi.astype(BF16)

        def w(ref, r, c):
            return ref[r, c].astype(BF16)

        lo, hi, full = slice(0, HALF), slice(HALF, D_MODEL), slice(None)
        gate = _dot(x_lo, w(wg_ref, lo, full)) + _dot(x_hi, w(wg_ref, hi, full))
        up = _dot(x_lo, w(wu_ref, lo, full)) + _dot(x_hi, w(wu_ref, hi, full))
        hid = (gate * jax.nn.sigmoid(gate) * up).astype(BF16)

        @pl.when(has_prev)
        def _():
            wait_prev_out()

        yb_ref[0:rows, :] = _pack_bf16_pair(_dot(hid, w(wd_ref, full, lo)), _dot(hid, w(wd_ref, full, hi)))
        write_out(rows)

    @pl.when(n > 0)
    def _():
        wait_gather(m_cur, slot)
        for m in range(1, MOE_SB + 1):
            @pl.when(m_cur == m)
            def _():
                compute(m * MOE_BLK)

    @pl.when((n == 0) & (m_cur > 0))
    def _():
        @pl.when(has_prev)
        def _():
            wait_prev_out()

        yb_ref[...] = jnp.zeros_like(yb_ref)
        for m in range(1, MOE_SB + 1):
            @pl.when(m_cur == m)
            def _():
                write_out(m * MOE_BLK)

    @pl.when((m_cur == 0) & has_prev)
    def _():
        wait_prev_out()


def _moe(x_slots, plan, w_gate, w_up, w_down, n_out_rows):
    sb_e, sb_m, sb_n, sb_base = plan
    sb_rows = MOE_SB * MOE_BLK
    wspec = lambda s: pl.BlockSpec((None,) + s, lambda b, se, *_: (se[b], 0, 0))
    grid_spec = pltpu.PrefetchScalarGridSpec(
        num_scalar_prefetch=4,
        grid=(sb_e.shape[0],),
        in_specs=[pl.BlockSpec(memory_space=pl.ANY),
                  wspec((D_MODEL, D_EXPERT)), wspec((D_MODEL, D_EXPERT)), wspec((D_EXPERT, D_MODEL))],
        out_specs=pl.BlockSpec(memory_space=pl.ANY),
        scratch_shapes=[pltpu.VMEM((2, sb_rows, HALF), jnp.uint32), pltpu.VMEM((sb_rows, HALF), jnp.uint32),
                        pltpu.SemaphoreType.DMA((2,)), pltpu.SemaphoreType.DMA((1,))],
    )
    return pl.pallas_call(
        _moe_kernel,
        grid_spec=grid_spec,
        out_shape=jax.ShapeDtypeStruct((n_out_rows, HALF), jnp.uint32),
        compiler_params=pltpu.CompilerParams(dimension_semantics=("arbitrary",),
                                             vmem_limit_bytes=MOE_VMEM_LIMIT),
        name="moe_experts",
    )(sb_e, sb_m, sb_n, sb_base, x_slots, w_gate, w_up, w_down)


def _dispatch_kernel(dest_ref, zb_ref, hn_ref, out_hbm, buf_ref, zero_ref, sem, zsem):
    i = pl.program_id(0)
    nt = pl.num_programs(0)
    tm = hn_ref.shape[0]
    t_all = nt * tm
    slot = lax.rem(i, 2)
    nblk = zb_ref.shape[0]

    def zero_copy(blk):
        dst = pl.ds(pl.multiple_of(blk * MOE_BLK, MOE_BLK), MOE_BLK)
        return pltpu.make_async_copy(zero_ref, out_hbm.at[dst, :], zsem.at[0])

    @pl.when(i == 0)
    def _():
        zero_ref[...] = jnp.zeros_like(zero_ref)

        def start(blk, c):
            @pl.when(zb_ref[blk] == 1)
            def _():
                zero_copy(blk).start()
            return c

        def wait(blk, c):
            @pl.when(zb_ref[blk] == 1)
            def _():
                zero_copy(blk).wait()
            return c

        lax.fori_loop(0, nblk, start, 0)
        lax.fori_loop(0, nblk, wait, 0)

    def wait_rows(sl):
        for k in range(2):
            pltpu.make_async_copy(buf_ref.at[sl], out_hbm.at[pl.ds(0, tm), :], sem.at[sl]).wait()

    @pl.when(i >= 2)
    def _():
        wait_rows(slot)

    buf_ref[slot] = hn_ref[...]
    base = i * tm
    for a in range(2 * tm):
        row, k = a // 2, a % 2
        d = dest_ref[k * t_all + base + row]
        pltpu.make_async_copy(buf_ref.at[slot, pl.ds(row, 1), :], out_hbm.at[pl.ds(d, 1), :],
                              sem.at[slot]).start()

    @pl.when(i == nt - 1)
    def _():
        wait_rows(slot)

        @pl.when(nt > 1)
        def _():
            wait_rows(1 - slot)


def _dispatch(dest, zero_blk, hn_pk, n_slots):
    tm = TOK_TILE
    grid_spec = pltpu.PrefetchScalarGridSpec(
        num_scalar_prefetch=2,
        grid=(hn_pk.shape[0] // tm,),
        in_specs=[pl.BlockSpec((tm, HALF), lambda i, d, z: (i, 0))],
        out_specs=pl.BlockSpec(memory_space=pl.ANY),
        scratch_shapes=[pltpu.VMEM((2, tm, HALF), jnp.uint32), pltpu.VMEM((MOE_BLK, HALF), jnp.uint32),
                        pltpu.SemaphoreType.DMA((2,)), pltpu.SemaphoreType.DMA((1,))],
    )
    return pl.pallas_call(
        _dispatch_kernel,
        grid_spec=grid_spec,
        out_shape=jax.ShapeDtypeStruct((n_slots, HALF), jnp.uint32),
        compiler_params=pltpu.CompilerParams(dimension_semantics=("arbitrary",), vmem_limit_bytes=VMEM_LIMIT),
        name="moe_dispatch",
    )(dest, zero_blk, hn_pk)


def _moe_plan(rtt, cnt):
    t_all = rtt.shape[1]
    n_assign = 2 * t_all
    e_flat = rtt[0:2].astype(jnp.int32).reshape(-1)
    rank = rtt[4:6].astype(jnp.int32).reshape(-1)
    counts = cnt[0, 0:N_EXPERTS].astype(jnp.int32)
    padded = (counts + MOE_BLK - 1) // MOE_BLK * MOE_BLK
    pad_end = jnp.cumsum(padded)
    pad_start = pad_end - padded
    before = jnp.arange(N_EXPERTS, dtype=jnp.int32)[:, None] < e_flat[None, :]
    dest = rank + jnp.sum(jnp.where(before, padded[:, None], 0), axis=0)
    nblk = -(-(n_assign + N_EXPERTS * (MOE_BLK - 1)) // MOE_BLK)
    n_slots = nblk * MOE_BLK
    blk = jnp.arange(nblk, dtype=jnp.int32)
    last_of_expert = jnp.any((blk[:, None] == (pad_end // MOE_BLK - 1)[None, :]) & (counts > 0)[None, :], axis=1)
    zero_blk = (last_of_expert | (blk >= pad_end[-1] // MOE_BLK)).astype(jnp.int32)
    k_e = padded // MOE_BLK
    sbc = (k_e + MOE_SB - 1) // MOE_SB
    sb_end = jnp.cumsum(sbc)
    sb_start = sb_end - sbc
    n_sb = (nblk + (MOE_SB - 1) * N_EXPERTS) // MOE_SB
    s = jnp.arange(n_sb, dtype=jnp.int32)
    sb_e = jnp.minimum(jnp.sum((s[:, None] >= sb_end[None, :]).astype(jnp.int32), axis=1), N_EXPERTS - 1)
    j = s - sb_start[sb_e]
    real = s < sb_end[-1]
    tail_blk = pad_end[-1] // MOE_BLK + MOE_SB * (s - sb_end[-1])
    sb_m = jnp.where(real, jnp.clip(k_e[sb_e] - MOE_SB * j, 0, MOE_SB), jnp.clip(nblk - tail_blk, 0, MOE_SB))
    sb_n = jnp.where(real, jnp.clip(counts[sb_e] - MOE_SB * MOE_BLK * j, 0, MOE_SB * MOE_BLK), 0)
    sb_base = jnp.where(real, pad_start[sb_e] + MOE_SB * MOE_BLK * j, jnp.minimum(tail_blk, nblk - 1) * MOE_BLK)
    i32 = lambda a: a.astype(jnp.int32)
    return (i32(sb_e), i32(sb_m), i32(sb_n), i32(sb_base)), i32(dest), zero_blk, n_slots


def _ple_out_kernel(dest_ref, h1_ref, rt_ref, pp_ref, ps_ref, nple_ref, wpg_ref, wp_ref, nfin_ref, ys_hbm,
                    op_ref, os_ref, yb_ref, sem, *, n_prompt_tiles):
    i = pl.program_id(0)
    nt = pl.num_programs(0)
    tm = h1_ref.shape[0]
    t_all = nt * tm
    slot = lax.rem(i, 2)

    def issue(tile, sl):
        base = tile * tm

        def grp(g, c):
            for j in range(DMA_UNROLL):
                row = g * (DMA_UNROLL // 2) + j // 2
                d = dest_ref[(j % 2) * t_all + base + row]
                pltpu.make_async_copy(ys_hbm.at[pl.ds(d, 1), :], yb_ref.at[sl, j % 2, pl.ds(row, 1), :],
                                      sem.at[sl]).start()
            return c

        lax.fori_loop(0, 2 * tm // DMA_UNROLL, grp, 0)

    @pl.when(i == 0)
    def _():
        issue(0, 0)

    def wait_rows(sl):
        for k in range(2):
            pltpu.make_async_copy(ys_hbm.at[pl.ds(0, tm), :], yb_ref.at[sl, k], sem.at[sl]).wait()

    wait_rows(slot)
    is_p = i < n_prompt_tiles
    rt = rt_ref[...]
    w0, w1 = rt[:, 2:3], rt[:, 3:4]
    lo0, hi0 = _unpack_bf16_pair(yb_ref[slot, 0])
    lo1, hi1 = _unpack_bf16_pair(yb_ref[slot, 1])
    h2 = h1_ref[...] + jnp.concatenate([w0 * lo0 + w1 * lo1, w0 * hi0 + w1 * hi1], axis=1)
    hb = _rms(h2, nple_ref[...]).astype(BF16)
    nbase = jnp.minimum(i + 1, nt - 1) * tm
    n_chunk = 8
    cw, ca = D_MODEL // n_chunk, 2 * tm // n_chunk
    gates = []
    for c in range(n_chunk):
        gates.append(jax.nn.sigmoid(_dot(hb, wpg_ref[:, c * cw:(c + 1) * cw])))
        for a in range(c * ca, (c + 1) * ca):
            d = dest_ref[(a % 2) * t_all + nbase + a // 2]
            pltpu.make_async_copy(ys_hbm.at[pl.ds(d, 1), :], yb_ref.at[1 - slot, a % 2, pl.ds(a // 2, 1), :],
                                  sem.at[1 - slot]).start()
    gate = jnp.concatenate(gates, axis=1)

    @pl.when(i == nt - 1)
    def _():
        wait_rows(1 - slot)

    p = jnp.where(is_p, pp_ref[...], ps_ref[...])
    h3 = h2 + _dot(p.astype(BF16), wp_ref[...]) * gate
    y = _rms(h3, nfin_ref[...])

    @pl.when(is_p)
    def _():
        op_ref[...] = y

    @pl.when(jnp.logical_not(is_p))
    def _():
        os_ref[...] = y


def _ple_out(dest, h1, rt, pp, ps, nple, wpg, wp, nfin, y_slots):
    tm = TOK_TILE
    npt, nst = pp.shape[0] // tm, ps.shape[0] // tm
    row = lambda w: pl.BlockSpec((tm, w), lambda i, d: (i, 0))
    prow = lambda w: pl.BlockSpec((tm, w), lambda i, d: (jnp.minimum(i, npt - 1), 0))
    srow = lambda w: pl.BlockSpec((tm, w), lambda i, d: (jnp.maximum(i - npt, 0), 0))
    const = lambda shape: pl.BlockSpec(shape, lambda i, d: (0,) * len(shape), pipeline_mode=pl.Buffered(1))
    grid_spec = pltpu.PrefetchScalarGridSpec(
        num_scalar_prefetch=1,
        grid=(npt + nst,),
        in_specs=[row(D_MODEL), row(LANES), prow(D_PLE), srow(D_PLE), const((1, D_MODEL)), const(wpg.shape),
                  const(wp.shape), const((1, D_MODEL)), pl.BlockSpec(memory_space=pl.ANY)],
        out_specs=[prow(D_MODEL), srow(D_MODEL)],
        scratch_shapes=[pltpu.VMEM((2, 2, tm, HALF), jnp.uint32), pltpu.SemaphoreType.DMA((2,))],
    )
    return pl.pallas_call(
        functools.partial(_ple_out_kernel, n_prompt_tiles=npt),
        grid_spec=grid_spec,
        out_shape=[jax.ShapeDtypeStruct((pp.shape[0], D_MODEL), F32),
                   jax.ShapeDtypeStruct((ps.shape[0], D_MODEL), F32)],
        compiler_params=pltpu.CompilerParams(dimension_semantics=("arbitrary",), vmem_limit_bytes=VMEM_LIMIT),
        name="ple_out",
    )(dest, h1, rt, pp, ps, nple, wpg, wp, nfin, y_slots)


def _s5_params(lam_re, lam_im, log_dt, b_re, b_im, c_re, c_im, d_skip):
    dt = jnp.exp(log_dt)[:, None]
    mag = jnp.exp(lam_re * dt)
    ab_re = mag * jnp.cos(lam_im * dt)
    ab_im = mag * jnp.sin(lam_im * dt)
    den = lam_re * lam_re + lam_im * lam_im
    nr = ab_re - 1.0
    f_re = (nr * lam_re + ab_im * lam_im) / den
    f_im = (ab_im * lam_re - nr * lam_im) / den
    bb_re = f_re[..., None] * b_re - f_im[..., None] * b_im
    bb_im = f_re[..., None] * b_im + f_im[..., None] * b_re
    eye = jnp.eye(S5_GB, dtype=F32)

    def bu_w(bb):
        bb = bb.reshape(S5_NGB, S5_GB, S5_STATE, S5_GROUP)
        w = jnp.einsum('nlph,lm->nlhmp', bb, eye)
        return w.reshape(S5_NGB, S5_GB * S5_GROUP, S5_SL)

    def c_w(c):
        c = c.reshape(S5_NGB, S5_GB, S5_GROUP, S5_STATE)
        w = jnp.einsum('nlhp,lm->nlpmh', c, eye)
        return w.reshape(S5_NGB, S5_SL, S5_GB * S5_GROUP)

    wbu = jnp.concatenate([bu_w(bb_re), bu_w(bb_im)], axis=2).astype(BF16)
    wc = jnp.concatenate([c_w(c_re), -c_w(c_im)], axis=1).astype(BF16)
    a_re = ab_re.reshape(S5_NGB, 1, S5_SL)
    a_im = ab_im.reshape(S5_NGB, 1, S5_SL)
    dsk = d_skip.reshape(S5_NGB, 1, S5_GB * S5_GROUP)
    return wbu, wc, a_re, a_im, dsk


def kernel(x_prompt, x_sample, p_prompt, p_sample, state_gla, state_s5_re, state_s5_im, norm_mix, w_in, gla_w_gate_up, gla_gate_bias, gla_norm, s5_lam_re, s5_lam_im, s5_log_dt, s5_b_re, s5_b_im, s5_c_re, s5_c_im, s5_d, s5_glu_w, s5_glu_b, s5_norm, w_out, norm_ffn, router_group, router_expert, w_gate, w_up, w_down, norm_ple, w_ple, w_ple_gate, norm_final):
    depth = w_in.shape[0]
    assert depth == 1
    i = 0
    bp, lp, _ = x_prompt.shape
    bs, ls, _ = x_sample.shape
    tp, ts = bp * lp, bs * ls
    t_all = tp + ts

    wm = _w_in_prep(w_in[i].astype(BF16))
    wgu = jnp.pad(gla_w_gate_up[i], ((0, LANES - GLA_RANK), (0, 0))).astype(BF16)
    gbias = gla_gate_bias[i].reshape(1, QK_W)
    wbu, wc, a_re, a_im, dsk = _s5_params(s5_lam_re[i], s5_lam_im[i], s5_log_dt[i], s5_b_re[i], s5_b_im[i],
                                          s5_c_re[i], s5_c_im[i], s5_d[i])
    glu_w = s5_glu_w[i].astype(BF16)
    wo = w_out[i].astype(BF16)
    wr32 = jnp.pad(jnp.concatenate([router_group[i], router_expert[i]], axis=1),
                   ((0, 0), (0, LANES - N_EGROUPS - N_EXPERTS)))
    wr_hi = wr32.astype(BF16)
    wr = jnp.stack([wr_hi, (wr32 - wr_hi.astype(F32)).astype(BF16)])
    wpg = w_ple_gate[i].astype(BF16)
    wp = w_ple[i].astype(BF16)
    vec = lambda a: a.reshape(1, -1)

    xp = x_prompt.reshape(tp, D_MODEL)
    xs = x_sample.reshape(ts, D_MODEL)

    qp, kp, vp, rp, lap, up = _in_proj(xp, vec(norm_mix[i]), wm, wgu, gbias, BF16)
    qs, ks, vs, rs, las, us = _in_proj(xs, vec(norm_mix[i]), wm, wgu, gbias, BF16)
    ogp, gla_p = _gla_prompt(qp, kp, vp, rp, lap, vec(gla_norm[i]), bp, lp)
    ogs, gla_s = _gla_sample(qs, ks, vs, rs, las, vec(gla_norm[i]), state_gla[i], bs, ls)
    wbu2, wc2, a_re2, a_im2 = _s5_prompt_params(wbu, wc, a_re, a_im, bp)
    ysp, re_p, im_p = _s5_prompt(up.reshape(bp, lp, D_S5), wbu2, wc2, a_re2, a_im2, dsk, tc=256)
    yss, re_s, im_s = _s5(us.reshape(bs, ls, D_S5), wbu, wc, a_re, a_im, dsk,
                          state_s5_re[i].reshape(bs, -1), state_s5_im[i].reshape(bs, -1),
                          nb=bs, tc=ls, has_state=True)

    h1, hn_all, rt_all, rtt, cnt = _mix_out(xp, xs, ogp, ogs, ysp.reshape(tp, D_S5), yss.reshape(ts, D_S5),
                                            glu_w, vec(s5_glu_b[i]), vec(s5_norm[i]), wo, vec(norm_ffn[i]), wr)

    plan, dest, zero_blk, n_slots = _moe_plan(rtt, cnt)
    x_slots = _dispatch(dest, zero_blk, hn_all, n_slots)
    y_slots = _moe(x_slots, plan, w_gate[i], w_up[i], w_down[i], n_slots)

    y_p, y_s = _ple_out(dest, h1, rt_all, p_prompt[i].reshape(tp, D_PLE), p_sample[i].reshape(ts, D_PLE),
                        vec(norm_ple[i]), wpg, wp, vec(norm_final), y_slots)

    s5shape = lambda a, b: a.reshape(1, b, S5_GROUPS, S5_STATE)
    return (y_p.reshape(bp, lp, D_MODEL), y_s.reshape(bs, ls, D_MODEL),
            gla_p[None], s5shape(re_p, bp), s5shape(im_p, bp),
            gla_s[None], s5shape(re_s, bs), s5shape(im_s, bs))
```

```python
import functools
import math

import jax
import jax.numpy as jnp
from jax import lax
from jax.experimental import pallas as pl
from jax.experimental.pallas import tpu as pltpu

F32 = jnp.float32
BF16 = jnp.bfloat16

D_MODEL = 2048
D_GLA = 1024
D_S5 = 1024
GLA_HEADS = 4
GLA_DV = 256
GLA_DK = 128
GLA_RANK = 16
GLA_CHUNK = 64
S5_GROUP = 16
S5_GROUPS = 64
S5_STATE = 64
N_EGROUPS = 4
N_EPG = 8
N_EXPERTS = 32
D_EXPERT = 512
D_PLE = 256
EPS = 1e-6

LANES = 128
QK_W = GLA_HEADS * GLA_DK
S5_GB = 8
S5_NGB = S5_GROUPS // S5_GB
S5_SL = S5_GB * S5_STATE
TOK_TILE = 256
MOE_BLK = 128
VMEM_LIMIT = 56 * 1024 * 1024
MOE_VMEM_LIMIT = 60 * 1024 * 1024


def _const_spec(shape):
    nd = len(shape)
    return pl.BlockSpec(shape, lambda *_: (0,) * nd, pipeline_mode=pl.Buffered(1))


def _rms(x, g):
    return x * lax.rsqrt(jnp.mean(x * x, axis=-1, keepdims=True) + EPS) * g


def _dot(a, b):
    return jnp.dot(a, b, preferred_element_type=F32)


def _log_sigmoid(x):
    return -(jnp.maximum(-x, 0.0) + jnp.log1p(jnp.exp(-jnp.abs(x))))


N_QKVR = 2 * QK_W + 2 * D_GLA
W_IN_COLS = N_QKVR + GLA_RANK + D_S5


def _w_in_prep_kernel(w_ref, o_ref):
    o_ref[:, 0:N_QKVR] = w_ref[:, 0:N_QKVR].astype(BF16)
    tail = w_ref[:, N_QKVR:W_IN_COLS]
    o_ref[:, N_QKVR:N_QKVR + D_S5] = tail[:, GLA_RANK:GLA_RANK + D_S5].astype(BF16)
    o_ref[:, N_QKVR + D_S5:N_QKVR + D_S5 + LANES] = tail[:, 0:LANES].astype(BF16)


def _w_in_prep(wi):
    rows = 256
    return pl.pallas_call(
        _w_in_prep_kernel,
        grid=(D_MODEL // rows,),
        in_specs=[pl.BlockSpec((rows, W_IN_COLS), lambda i: (i, 0))],
        out_specs=pl.BlockSpec((rows, N_QKVR + D_S5 + LANES), lambda i: (i, 0)),
        out_shape=jax.ShapeDtypeStruct((D_MODEL, N_QKVR + D_S5 + LANES), BF16),
        compiler_params=pltpu.CompilerParams(dimension_semantics=("arbitrary",), vmem_limit_bytes=VMEM_LIMIT),
        name="w_in_prep",
    )(wi)


def _in_proj_kernel(x_ref, g_ref, wm_ref, wgu_ref, gb_ref,
                    q_ref, k_ref, v_ref, r_ref, la_ref, u_ref):
    hb = _rms(x_ref[...], g_ref[...]).astype(BF16)

    def seg(a, b):
        return _dot(hb, wm_ref[:, a:b])

    q_ref[...] = (seg(0, QK_W) * (GLA_DK ** -0.5)).astype(q_ref.dtype)
    k_ref[...] = seg(QK_W, 2 * QK_W).astype(k_ref.dtype)
    v_ref[...] = seg(2 * QK_W, 2 * QK_W + D_GLA).astype(v_ref.dtype)
    r_ref[...] = seg(2 * QK_W + D_GLA, N_QKVR).astype(r_ref.dtype)
    u_ref[...] = seg(N_QKVR, N_QKVR + D_S5)
    zg = seg(N_QKVR + D_S5, N_QKVR + D_S5 + LANES)
    xg = _dot(zg.astype(BF16), wgu_ref[...]) + gb_ref[...]
    la_ref[...] = _log_sigmoid(xg) * (1.0 / 16.0)


def _in_proj(x2d, g, wm, wgu, gbias, act_dtype):
    t = x2d.shape[0]
    tm = TOK_TILE
    row = lambda w: pl.BlockSpec((tm, w), lambda i: (i, 0))
    return pl.pallas_call(
        _in_proj_kernel,
        grid=(t // tm,),
        in_specs=[row(D_MODEL), _const_spec((1, D_MODEL)), _const_spec(wm.shape),
                  _const_spec(wgu.shape), _const_spec((1, QK_W))],
        out_specs=[row(QK_W), row(QK_W), row(D_GLA), row(D_GLA), row(QK_W), row(D_S5)],
        out_shape=[jax.ShapeDtypeStruct((t, QK_W), act_dtype), jax.ShapeDtypeStruct((t, QK_W), act_dtype),
                   jax.ShapeDtypeStruct((t, D_GLA), act_dtype), jax.ShapeDtypeStruct((t, D_GLA), act_dtype),
                   jax.ShapeDtypeStruct((t, QK_W), F32), jax.ShapeDtypeStruct((t, D_S5), F32)],
        compiler_params=pltpu.CompilerParams(dimension_semantics=("arbitrary",), vmem_limit_bytes=VMEM_LIMIT),
        name="in_proj",
    )(x2d, g, wm, wgu, gbias)


_NT = (((1,), (1,)), ((), ()))
_TN = (((0,), (0,)), ((), ()))


def _gla_pre(q, k, la, c):
    r = q.shape[0]
    shift = int(math.log2(c))
    ri = lax.broadcasted_iota(jnp.int32, (r, r), 0)
    si = lax.broadcasted_iota(jnp.int32, (r, r), 1)
    mask = ((ri >> shift) == (si >> shift)) & (ri >= si)
    tri = jnp.where(mask, 1.0, 0.0).astype(BF16)
    hi = la.astype(BF16)
    r1 = la - hi.astype(F32)
    mid = r1.astype(BF16)
    lo = (r1 - mid.astype(F32)).astype(BF16)
    cum = _dot(tri, hi) + _dot(tri, mid) + _dot(tri, lo)
    last = jnp.concatenate([jnp.broadcast_to(cum[(i + 1) * c - 1:(i + 1) * c, :], (c, cum.shape[1]))
                            for i in range(r // c)], axis=0)
    qe = (q * jnp.exp(cum)).astype(BF16)
    ke = (k * jnp.exp(-cum)).astype(BF16)
    kd = (k * jnp.exp(last - cum)).astype(BF16)
    return qe, ke, kd, cum, mask


def _gla_intra(qe, ke, v, mask):
    sc = lax.dot_general(qe, ke, _NT, preferred_element_type=F32)
    return _dot(jnp.where(mask, sc, 0.0).astype(BF16), v)


def _gla_finish(o, r, g):
    rf = r.astype(F32)
    return _rms(o, g) * (rf * jax.nn.sigmoid(rf))


GLA_SAFE_LOG_DECAY = -60.0


def _gla_token_step(t, rows, q, k, v, a_all, st_ref, oacc_ref):
    m = rows == t
    a = jnp.sum(jnp.where(m, a_all, 0.0), axis=0, keepdims=True)
    kt = jnp.where(m, k, 0.0).astype(BF16)
    qt = jnp.where(m, q, 0.0).astype(BF16)
    vt = jnp.where(m, v, jnp.zeros_like(v))
    st = st_ref[...] * a + lax.dot_general(vt, kt, _TN, preferred_element_type=F32)
    st_ref[...] = st
    oacc_ref[...] += lax.dot_general(qt, st.astype(BF16), _NT, preferred_element_type=F32)


def _gla_prompt_kernel(q_ref, k_ref, v_ref, r_ref, la_ref, g_ref, o_ref, sfin_ref, st_ref, oacc_ref,
                       *, n_chunks):
    j = pl.program_id(1)
    nseq = q_ref.shape[0]

    @pl.when(j == 0)
    def _():
        st_ref[...] = jnp.zeros_like(st_ref)

    c = GLA_CHUNK
    pre = [_gla_pre(q_ref[s].astype(F32), k_ref[s].astype(F32), la_ref[s], c) for s in range(nseq)]
    g = g_ref[...]
    safe = functools.reduce(jnp.minimum, [jnp.min(p[3]) for p in pre]) >= GLA_SAFE_LOG_DECAY

    @pl.when(safe)
    def _():
        for s in range(nseq):
            qe, ke, kd, cum, mask = pre[s]
            for h in range(GLA_HEADS):
                kc = slice(h * GLA_DK, (h + 1) * GLA_DK)
                vc = slice(h * GLA_DV, (h + 1) * GLA_DV)
                v = v_ref[s, :, vc]
                qe_h, kd_h = qe[:, kc], kd[:, kc]
                o = _gla_intra(qe_h, ke[:, kc], v, mask)
                st = st_ref[s, h]
                inter = []
                for ci in range(n_chunks):
                    rows = slice(ci * c, (ci + 1) * c)
                    inter.append(lax.dot_general(qe_h[rows], st.astype(BF16), _NT, preferred_element_type=F32))
                    dec = jnp.exp(cum[(ci + 1) * c - 1:(ci + 1) * c, kc])
                    st = st * dec + lax.dot_general(v[rows], kd_h[rows], _TN, preferred_element_type=F32)
                st_ref[s, h] = st
                o = o + jnp.concatenate(inter, axis=0)
                o_ref[s, :, vc] = _gla_finish(o, r_ref[s, :, vc], g).astype(o_ref.dtype)

    @pl.when(jnp.logical_not(safe))
    def _():
        n_rows = n_chunks * c
        rows = lax.broadcasted_iota(jnp.int32, (n_rows, 1), 0)
        for s in range(nseq):
            for h in range(GLA_HEADS):
                kc = slice(h * GLA_DK, (h + 1) * GLA_DK)
                vc = slice(h * GLA_DV, (h + 1) * GLA_DV)
                q, k, v = q_ref[s, :, kc].astype(F32), k_ref[s, :, kc].astype(F32), v_ref[s, :, vc]
                a_all = jnp.exp(la_ref[s, :, kc])
                oacc_ref[...] = jnp.zeros_like(oacc_ref)

                def step(t, carry, s=s, h=h, q=q, k=k, v=v, a_all=a_all):
                    _gla_token_step(t, rows, q, k, v, a_all, st_ref.at[s, h], oacc_ref)
                    return carry

                lax.fori_loop(0, n_rows, step, 0)
                o_ref[s, :, vc] = _gla_finish(oacc_ref[...], r_ref[s, :, vc], g).astype(o_ref.dtype)

    @pl.when(j == pl.num_programs(1) - 1)
    def _():
        for s in range(nseq):
            for h in range(GLA_HEADS):
                sfin_ref[s, h] = st_ref[s, h].T


def _gla_prompt(q, k, v, r, la, g, batch, seq):
    rb = 4 * GLA_CHUNK
    ns = 2
    row = lambda w: pl.BlockSpec((ns, rb, w), lambda b, j: (b, j, 0))
    seq3 = lambda a: a.reshape(batch, seq, a.shape[-1])
    o, sfin = pl.pallas_call(
        functools.partial(_gla_prompt_kernel, n_chunks=rb // GLA_CHUNK),
        grid=(batch // ns, seq // rb),
        in_specs=[row(QK_W), row(QK_W), row(D_GLA), row(D_GLA), row(QK_W), _const_spec((1, GLA_DV))],
        out_specs=[row(D_GLA),
                   pl.BlockSpec((ns, GLA_HEADS, GLA_DK, GLA_DV), lambda b, j: (b, 0, 0, 0))],
        out_shape=[jax.ShapeDtypeStruct((batch, seq, D_GLA), BF16),
                   jax.ShapeDtypeStruct((batch, GLA_HEADS, GLA_DK, GLA_DV), F32)],
        scratch_shapes=[pltpu.VMEM((ns, GLA_HEADS, GLA_DV, GLA_DK), F32), pltpu.VMEM((rb, GLA_DV), F32)],
        compiler_params=pltpu.CompilerParams(dimension_semantics=("arbitrary", "arbitrary"),
                                             vmem_limit_bytes=VMEM_LIMIT),
        name="gla_prompt",
    )(seq3(q), seq3(k), seq3(v), seq3(r), seq3(la), g)
    return o.reshape(batch * seq, D_GLA), sfin


def _gla_sample_kernel(q_ref, k_ref, v_ref, r_ref, la_ref, g_ref, s0_ref, o_ref, sfin_ref, st_ref, oacc_ref,
                       *, n_seq, seq):
    qe, ke, kd, cum, mask = _gla_pre(q_ref[...].astype(F32), k_ref[...].astype(F32), la_ref[...], seq)
    g = g_ref[...]
    safe = jnp.min(cum) >= GLA_SAFE_LOG_DECAY

    @pl.when(safe)
    def _():
        lasts = jnp.concatenate([cum[(s + 1) * seq - 1:(s + 1) * seq, :] for s in range(n_seq)]
                                + [jnp.zeros((GLA_DK - n_seq, cum.shape[1]), F32)], axis=0)
        pair = 2 * seq
        upper = lax.broadcasted_iota(jnp.int32, (pair, GLA_DK), 0) < seq
        for h in range(GLA_HEADS):
            kc = slice(h * GLA_DK, (h + 1) * GLA_DK)
            vc = slice(h * GLA_DV, (h + 1) * GLA_DV)
            v = v_ref[:, vc]
            qe_h, kd_h = qe[:, kc], kd[:, kc]
            o = _gla_intra(qe_h, ke[:, kc], v, mask)
            dec_t = jnp.exp(lasts[:, kc].T)
            inter = []
            for p in range(n_seq // 2):
                rows = slice(p * pair, (p + 1) * pair)
                qe_p, kd_p, v_p = qe_h[rows], kd_h[rows], v[rows]
                for half in range(2):
                    s = 2 * p + half
                    s0 = s0_ref[s, h]
                    o_s = _dot(qe_p, s0.astype(BF16))
                    inter.append(o_s[half * seq:(half + 1) * seq])
                    kd_s = jnp.where(upper if half == 0 else ~upper, kd_p, jnp.zeros_like(kd_p))
                    dec = jnp.broadcast_to(dec_t[:, s:s + 1], (GLA_DK, GLA_DV))
                    sfin_ref[s, h] = s0 * dec + lax.dot_general(kd_s, v_p, _TN, preferred_element_type=F32)
            o = o + jnp.concatenate(inter, axis=0)
            o_ref[:, vc] = _gla_finish(o, r_ref[:, vc], g).astype(o_ref.dtype)

    @pl.when(jnp.logical_not(safe))
    def _():
        n_rows = n_seq * seq
        rows = lax.broadcasted_iota(jnp.int32, (n_rows, 1), 0)
        for h in range(GLA_HEADS):
            kc = slice(h * GLA_DK, (h + 1) * GLA_DK)
            vc = slice(h * GLA_DV, (h + 1) * GLA_DV)
            q, k, v = q_ref[:, kc].astype(F32), k_ref[:, kc].astype(F32), v_ref[:, vc]
            a_all = jnp.exp(la_ref[:, kc])
            oacc_ref[...] = jnp.zeros_like(oacc_ref)

            def step(t, carry):
                s = t // seq

                @pl.when(t % seq == 0)
                def _():
                    st_ref[...] = s0_ref[s, h].T

                _gla_token_step(t, rows, q, k, v, a_all, st_ref, oacc_ref)

                @pl.when(t % seq == seq - 1)
                def _():
                    sfin_ref[s, h] = st_ref[...].T

                return carry

            lax.fori_loop(0, n_rows, step, 0)
            o_ref[:, vc] = _gla_finish(oacc_ref[...], r_ref[:, vc], g).astype(o_ref.dtype)


def _gla_sample(q, k, v, r, la, g, s0, batch, seq):
    ns = 16
    rb = ns * seq
    row = lambda w: pl.BlockSpec((rb, w), lambda i: (i, 0))
    st = pl.BlockSpec((ns, GLA_HEADS, GLA_DK, GLA_DV), lambda i: (i, 0, 0, 0))
    return pl.pallas_call(
        functools.partial(_gla_sample_kernel, n_seq=ns, seq=seq),
        grid=(batch // ns,),
        in_specs=[row(QK_W), row(QK_W), row(D_GLA), row(D_GLA), row(QK_W), _const_spec((1, GLA_DV)), st],
        out_specs=[row(D_GLA), st],
        out_shape=[jax.ShapeDtypeStruct((batch * seq, D_GLA), BF16),
                   jax.ShapeDtypeStruct((batch, GLA_HEADS, GLA_DK, GLA_DV), F32)],
        scratch_shapes=[pltpu.VMEM((GLA_DV, GLA_DK), F32), pltpu.VMEM((rb, GLA_DV), F32)],
        compiler_params=pltpu.CompilerParams(dimension_semantics=("arbitrary",), vmem_limit_bytes=VMEM_LIMIT),
        name="gla_sample",
    )(q, k, v, r, la, g, s0)


def _s5_kernel(u_ref, wbu_ref, wc_ref, are_ref, aim_ref, d_ref, h0r_ref, h0i_ref,
               y_ref, sre_ref, sim_ref, bu_ref, xs_ref, car_ref, *, nb, tc, has_state):
    j = pl.program_id(2)

    @pl.when(j == 0)
    def _():
        if has_state:
            car_ref[0] = h0r_ref[...]
            car_ref[1] = h0i_ref[...]
        else:
            car_ref[...] = jnp.zeros_like(car_ref)

    u2 = u_ref[...].reshape(nb * tc, LANES)
    ub = u2.astype(BF16)
    nl = S5_SL // LANES
    for l in range(2 * nl):
        bu_ref[l] = _dot(ub, wbu_ref[0, :, l * LANES:(l + 1) * LANES])
    a_r = [jnp.broadcast_to(are_ref[0, :, l * LANES:(l + 1) * LANES], (nb, LANES)) for l in range(nl)]
    a_i = [jnp.broadcast_to(aim_ref[0, :, l * LANES:(l + 1) * LANES], (nb, LANES)) for l in range(nl)]

    def step(t, carry):
        rows = pl.ds(t, nb, stride=tc)
        out = []
        for l in range(nl):
            xr, xi = carry[2 * l], carry[2 * l + 1]
            nr = a_r[l] * xr - a_i[l] * xi + bu_ref[l, rows, :]
            ni = a_r[l] * xi + a_i[l] * xr + bu_ref[nl + l, rows, :]
            xs_ref[l, rows, :] = nr
            xs_ref[nl + l, rows, :] = ni
            out += [nr, ni]
        return tuple(out)

    init = []
    for l in range(nl):
        init += [car_ref[0, :, l * LANES:(l + 1) * LANES], car_ref[1, :, l * LANES:(l + 1) * LANES]]
    fin = lax.fori_loop(0, tc, step, tuple(init), unroll=8)
    xr = jnp.concatenate([fin[2 * l] for l in range(nl)], axis=1)
    xi = jnp.concatenate([fin[2 * l + 1] for l in range(nl)], axis=1)
    car_ref[0] = xr
    car_ref[1] = xi
    y = d_ref[0] * u2
    for l in range(2 * nl):
        y = y + _dot(xs_ref[l].astype(BF16), wc_ref[0, l * LANES:(l + 1) * LANES, :])
    y_ref[...] = y.reshape(nb, tc, LANES)

    @pl.when(j == pl.num_programs(2) - 1)
    def _():
        sre_ref[...] = xr
        sim_ref[...] = xi


def _s5(u3d, wbu, wc, a_re, a_im, dsk, h0r, h0i, nb, tc, has_state):
    batch, seq, _ = u3d.shape
    grid = (S5_NGB, batch // nb, seq // tc)
    st = pl.BlockSpec((nb, S5_SL), lambda g, b, j: (b, g))
    par = lambda w: pl.BlockSpec((1, 1, w), lambda g, b, j: (g, 0, 0))
    ublk = pl.BlockSpec((nb, tc, LANES), lambda g, b, j: (b, j, g))
    return pl.pallas_call(
        functools.partial(_s5_kernel, nb=nb, tc=tc, has_state=has_state),
        grid=grid,
        in_specs=[ublk,
                  pl.BlockSpec((1, LANES, 2 * S5_SL), lambda g, b, j: (g, 0, 0)),
                  pl.BlockSpec((1, 2 * S5_SL, LANES), lambda g, b, j: (g, 0, 0)),
                  par(S5_SL), par(S5_SL), par(LANES), st, st],
        out_specs=[ublk, st, st],
        out_shape=[jax.ShapeDtypeStruct(u3d.shape, F32),
                   jax.ShapeDtypeStruct((batch, S5_GROUPS * S5_STATE), F32),
                   jax.ShapeDtypeStruct((batch, S5_GROUPS * S5_STATE), F32)],
        scratch_shapes=[pltpu.VMEM((2 * S5_SL // LANES, nb * tc, LANES), F32),
                        pltpu.VMEM((2 * S5_SL // LANES, nb * tc, LANES), F32),
                        pltpu.VMEM((2, nb, S5_SL), F32)],
        compiler_params=pltpu.CompilerParams(dimension_semantics=("arbitrary",) * 3,
                                             vmem_limit_bytes=VMEM_LIMIT),
        name="s5_state" if has_state else "s5_zero",
    )(u3d, wbu, wc, a_re, a_im, dsk, h0r, h0i)


S5_J = 2
S5_HL = S5_SL // S5_J
S5_PAIR = 2


def _s5_prompt_kernel(u_ref, wbu_ref, wc_ref, are_ref, aim_ref, d_ref, y_ref, sre_ref, sim_ref,
                      u2_ref, lhs_ref, bu_ref, xs_ref, y2_ref, yo_ref, car_ref, *, nb, tc):
    g = pl.program_id(0)
    j = pl.program_id(1)
    rows = nb * S5_J
    npair = S5_PAIR

    @pl.when((g == 0) & (j == 0))
    def _():
        lhs_ref[...] = jnp.zeros_like(lhs_ref)

    @pl.when(j == 0)
    def _():
        car_ref[...] = jnp.zeros_like(car_ref)

    u2 = u_ref[...].reshape(nb * tc, npair * LANES)
    for p in range(npair):
        u2_ref[p] = u2[:, p * LANES:(p + 1) * LANES]

    def build(t, c):
        for p in range(npair):
            u4 = u2_ref[p, pl.ds(t, nb, stride=tc), :]
            for jj in range(S5_J):
                lhs_ref[p, t, jj * nb:(jj + 1) * nb, jj * LANES:(jj + 1) * LANES] = u4
        return c

    lax.fori_loop(0, tc, build, 0, unroll=True)
    for p in range(npair):
        lhs = lhs_ref[p].reshape(tc * rows, S5_J * LANES).astype(BF16)
        bu_ref[p] = _dot(lhs, wbu_ref[p]).reshape(tc, rows, 2 * S5_HL)
    ar = [are_ref[p] for p in range(npair)]
    ai = [aim_ref[p] for p in range(npair)]

    def step(t, carry):
        out = []
        for p in range(npair):
            xr, xi = carry[2 * p], carry[2 * p + 1]
            tile = bu_ref[p, t]
            nr = ar[p] * xr - ai[p] * xi + tile[:, 0:S5_HL]
            ni = ar[p] * xi + ai[p] * xr + tile[:, S5_HL:2 * S5_HL]
            xs_ref[p, t] = jnp.concatenate([nr, ni], axis=1)
            out += [nr, ni]
        return tuple(out)

    init = tuple(car_ref[p, c] for p in range(npair) for c in range(2))
    fin = lax.fori_loop(0, tc, step, init, unroll=True)
    for p in range(npair):
        car_ref[p, 0] = fin[2 * p]
        car_ref[p, 1] = fin[2 * p + 1]
        xs = xs_ref[p].reshape(tc * rows, 2 * S5_HL).astype(BF16)
        y2_ref[p] = _dot(xs, wc_ref[p]).reshape(tc, rows, S5_J * LANES)
    first_half = lax.broadcasted_iota(jnp.int32, (rows, LANES), 0) < nb

    def unperm(t, c):
        for p in range(npair):
            t2 = y2_ref[p, t]
            part = jnp.where(first_half, t2[:, 0:LANES], t2[:, LANES:2 * LANES])
            yo_ref[p, pl.ds(t, nb, stride=tc), :] = (part + pltpu.roll(part, nb, axis=0))[0:nb]
        return c

    lax.fori_loop(0, tc, unperm, 0, unroll=True)
    y = jnp.concatenate([yo_ref[p] + d_ref[p] * u2_ref[p] for p in range(npair)], axis=1)
    y_ref[...] = y.reshape(nb, tc, npair * LANES)

    @pl.when(j == pl.num_programs(1) - 1)
    def _():
        def gather_state(x):
            return jnp.concatenate([x[jj * nb:(jj + 1) * nb] for jj in range(S5_J)], axis=1)

        sre_ref[...] = jnp.concatenate([gather_state(fin[2 * p]) for p in range(npair)], axis=1)
        sim_ref[...] = jnp.concatenate([gather_state(fin[2 * p + 1]) for p in range(npair)], axis=1)


def _s5_prompt(u3d, wbu2, wc2, a_re2, a_im2, dsk, tc):
    nb, seq, _ = u3d.shape
    rows = nb * S5_J
    assert rows == 8
    np_ = S5_PAIR
    st = pl.BlockSpec((nb, np_ * S5_SL), lambda g, j: (0, g))
    gblk = lambda s: pl.BlockSpec((np_,) + s, lambda g, j: (g, 0, 0))
    ublk = pl.BlockSpec((nb, tc, np_ * LANES), lambda g, j: (0, j, g))
    return pl.pallas_call(
        functools.partial(_s5_prompt_kernel, nb=nb, tc=tc),
        grid=(S5_NGB // np_, seq // tc),
        in_specs=[ublk, gblk((S5_J * LANES, 2 * S5_HL)), gblk((2 * S5_HL, S5_J * LANES)),
                  gblk((rows, S5_HL)), gblk((rows, S5_HL)), gblk((1, LANES))],
        out_specs=[ublk, st, st],
        out_shape=[jax.ShapeDtypeStruct(u3d.shape, F32),
                   jax.ShapeDtypeStruct((nb, S5_GROUPS * S5_STATE), F32),
                   jax.ShapeDtypeStruct((nb, S5_GROUPS * S5_STATE), F32)],
        scratch_shapes=[pltpu.VMEM((np_, nb * tc, LANES), F32),
                        pltpu.VMEM((np_, tc, rows, S5_J * LANES), F32),
                        pltpu.VMEM((np_, tc, rows, 2 * S5_HL), F32),
                        pltpu.VMEM((np_, tc, rows, 2 * S5_HL), F32),
                        pltpu.VMEM((np_, tc, rows, S5_J * LANES), F32),
                        pltpu.VMEM((np_, nb * tc, LANES), F32),
                        pltpu.VMEM((np_, 2, rows, S5_HL), F32)],
        compiler_params=pltpu.CompilerParams(dimension_semantics=("arbitrary",) * 2,
                                             vmem_limit_bytes=VMEM_LIMIT),
        name="s5_prompt",
    )(u3d, wbu2, wc2, a_re2, a_im2, dsk)


def _s5_prompt_params(wbu, wc, a_re, a_im, nb):
    h = S5_HL
    top = jnp.concatenate([wbu[:, :, 0:h], wbu[:, :, S5_SL:S5_SL + h]], axis=2)
    bot = jnp.concatenate([wbu[:, :, h:2 * h], wbu[:, :, S5_SL + h:S5_SL + 2 * h]], axis=2)
    wbu2 = jnp.concatenate([top, bot], axis=1)
    wc_j = [jnp.concatenate([wc[:, jj * h:(jj + 1) * h, :], wc[:, S5_SL + jj * h:S5_SL + (jj + 1) * h, :]],
                            axis=1) for jj in range(S5_J)]
    wc2 = jnp.concatenate(wc_j, axis=2)
    tile = lambda a: jnp.repeat(a.reshape(S5_NGB, S5_J, h), nb, axis=1)
    return wbu2, wc2, tile(a_re), tile(a_im)


def _mix_out_kernel(xp_ref, xs_ref, ogp_ref, ogs_ref, ysp_ref, yss_ref, glu_w_ref, glu_b_ref, s5n_ref, wo_ref,
                    nffn_ref, wr_ref, h1_ref, hn_ref, rt_ref, rtt_ref, cnt_ref, *, n_prompt_tiles):
    is_p = pl.program_id(0) < n_prompt_tiles
    x = jnp.where(is_p, xp_ref[...], xs_ref[...])
    og = jnp.where(is_p, ogp_ref[...], ogs_ref[...])
    y = jax.nn.gelu(jnp.where(is_p, ysp_ref[...], yss_ref[...]))
    y = y * jax.nn.sigmoid(_dot(y.astype(BF16), glu_w_ref[...]) + glu_b_ref[...])
    yn = _rms(y, s5n_ref[...]).astype(BF16)
    mix = _dot(og, wo_ref[0:D_GLA, :]) + _dot(yn, wo_ref[D_GLA:D_GLA + D_S5, :])
    h1 = x + mix
    h1_ref[...] = h1
    hn = _rms(h1, nffn_ref[...])
    hn_ref[...] = _pack_bf16_pair(hn[:, 0:HALF], hn[:, HALF:D_MODEL])
    hn_hi = hn.astype(BF16)
    hn_lo = (hn - hn_hi.astype(F32)).astype(BF16)
    logits = _dot(hn_hi, wr_ref[0]) + _dot(hn_hi, wr_ref[1]) + _dot(hn_lo, wr_ref[0])
    rt = _route(logits)

    @pl.when(pl.program_id(0) == 0)
    def _():
        cnt_ref[...] = jnp.zeros_like(cnt_ref)

    tm = rt.shape[0]
    lane = lax.broadcasted_iota(jnp.int32, rt.shape, 1).astype(F32)
    oh0 = lane == rt[:, 0:1]
    oh1 = lane == rt[:, 1:2]
    both = jnp.where(oh0 | oh1, 1.0, 0.0)
    ri = lax.broadcasted_iota(jnp.int32, (tm, tm), 0)
    ci = lax.broadcasted_iota(jnp.int32, (tm, tm), 1)
    before = _dot(jnp.where(ri > ci, 1.0, 0.0).astype(BF16), both.astype(BF16)) + cnt_ref[...]
    rank0 = jnp.sum(jnp.where(oh0, before, 0.0), axis=-1, keepdims=True)
    rank1 = jnp.sum(jnp.where(oh1, before, 0.0), axis=-1, keepdims=True)
    cnt_ref[...] += jnp.sum(both, axis=0, keepdims=True)
    rt = jnp.where(lane == 4.0, rank0, jnp.where(lane == 5.0, rank1, rt))
    rt_ref[...] = rt
    rtt_ref[...] = rt.T[0:8, :]


def _route(logits):
    col = lax.broadcasted_iota(jnp.int32, logits.shape, 1)
    colf = col.astype(F32)
    neg = -jnp.inf

    def first_argmax(vals):
        m = jnp.max(vals, axis=-1, keepdims=True)
        idx = jnp.min(jnp.where(vals == m, colf, float(LANES)), axis=-1, keepdims=True)
        return m, idx

    lg = jnp.where(col < N_EGROUPS, logits, neg)
    gmax, gsel = first_argmax(lg)
    p_g = 1.0 / jnp.sum(jnp.exp(lg - gmax), axis=-1, keepdims=True)
    ecol = col - N_EGROUPS
    egrp = (ecol >> int(math.log2(N_EPG))).astype(F32)
    in_group = (ecol >= 0) & (ecol < N_EXPERTS) & (egrp == gsel)
    le = jnp.where(in_group, logits, neg)
    m1, i1 = first_argmax(le)
    le2 = jnp.where(colf == i1, neg, le)
    m2, i2 = first_argmax(le2)
    e2 = jnp.exp(m2 - m1)
    den = 1.0 + e2
    w1 = p_g * (1.0 / den)
    w2 = p_g * (e2 / den)
    e1f = i1 - float(N_EGROUPS)
    e2f = i2 - float(N_EGROUPS)
    out = jnp.where(col == 0, e1f, jnp.where(col == 1, e2f, jnp.where(col == 2, w1, jnp.where(col == 3, w2, 0.0))))
    return out


def _mix_out(xp, xs, ogp, ogs, ysp, yss, glu_w, glu_b, s5n, wo, nffn, wr):
    tm = TOK_TILE
    npt, nst = xp.shape[0] // tm, xs.shape[0] // tm
    t = (npt + nst) * tm
    row = lambda w: pl.BlockSpec((tm, w), lambda i: (i, 0))
    prow = lambda w: pl.BlockSpec((tm, w), lambda i: (jnp.minimum(i, npt - 1), 0))
    srow = lambda w: pl.BlockSpec((tm, w), lambda i: (jnp.maximum(i - npt, 0), 0))
    return pl.pallas_call(
        functools.partial(_mix_out_kernel, n_prompt_tiles=npt),
        grid=(npt + nst,),
        in_specs=[prow(D_MODEL), srow(D_MODEL), prow(D_GLA), srow(D_GLA), prow(D_S5), srow(D_S5),
                  _const_spec(glu_w.shape), _const_spec((1, D_S5)), _const_spec((1, D_S5)),
                  _const_spec(wo.shape), _const_spec((1, D_MODEL)), _const_spec(wr.shape)],
        out_specs=[row(D_MODEL), row(HALF), row(LANES), pl.BlockSpec((8, tm), lambda i: (0, i)),
                   pl.BlockSpec((1, LANES), lambda i: (0, 0))],
        out_shape=[jax.ShapeDtypeStruct((t, D_MODEL), F32), jax.ShapeDtypeStruct((t, HALF), jnp.uint32),
                   jax.ShapeDtypeStruct((t, LANES), F32), jax.ShapeDtypeStruct((8, t), F32),
                   jax.ShapeDtypeStruct((1, LANES), F32)],
        compiler_params=pltpu.CompilerParams(dimension_semantics=("arbitrary",), vmem_limit_bytes=VMEM_LIMIT),
        name="mix_out",
    )(xp, xs, ogp, ogs, ysp, yss, glu_w, glu_b, s5n, wo, nffn, wr)


DMA_UNROLL = 8


MOE_SB = 6
HALF = D_MODEL // 2


def _pack_bf16_pair(lo, hi):
    def bits(x):
        b = pltpu.bitcast(x, jnp.uint32)
        return (b + jnp.uint32(0x7FFF) + ((b >> 16) & jnp.uint32(1))) >> 16
    return bits(lo) | (bits(hi) << 16)


def _unpack_bf16_pair(u):
    return (pltpu.bitcast(u << 16, F32), pltpu.bitcast(u & jnp.uint32(0xFFFF0000), F32))


def _moe_kernel(se_ref, sm_ref, sn_ref, sbase_ref, xs_hbm, wg_ref, wu_ref, wd_ref, out_hbm,
                xb_ref, yb_ref, gsem, ssem):
    b = pl.program_id(0)
    nb = pl.num_programs(0)
    n = sn_ref[b]
    slot = lax.rem(b, 2)

    def issue_gather(blk, sl):
        base = pl.multiple_of(sbase_ref[blk], MOE_BLK)
        for m in range(1, MOE_SB + 1):
            @pl.when((sn_ref[blk] > 0) & (sm_ref[blk] == m))
            def _():
                rows = m * MOE_BLK
                pltpu.make_async_copy(xs_hbm.at[pl.ds(base, rows), :], xb_ref.at[sl, pl.ds(0, rows), :],
                                      gsem.at[sl]).start()

    def wait_gather(blocks, sl):
        rows = pl.ds(0, pl.multiple_of(blocks * MOE_BLK, MOE_BLK))
        pltpu.make_async_copy(xs_hbm.at[rows, :], xb_ref.at[sl, rows, :], gsem.at[sl]).wait()

    m_cur = sm_ref[b]
    m_prev = sm_ref[jnp.maximum(b - 1, 0)]
    has_prev = (b > 0) & (m_prev > 0)

    def out_copy(rows):
        dst = pl.ds(pl.multiple_of(sbase_ref[b], MOE_BLK), rows)
        return pltpu.make_async_copy(yb_ref.at[pl.ds(0, rows), :], out_hbm.at[dst, :], ssem.at[0])

    def wait_prev_out():
        rows = pl.ds(0, pl.multiple_of(m_prev * MOE_BLK, MOE_BLK))
        pltpu.make_async_copy(yb_ref.at[rows, :], out_hbm.at[rows, :], ssem.at[0]).wait()

    def write_out(rows):
        out_copy(rows).start()

        @pl.when(b == nb - 1)
        def _():
            out_copy(rows).wait()

    @pl.when(b == 0)
    def _():
        issue_gather(0, 0)

    @pl.when(b + 1 < nb)
    def _():
        issue_gather(b + 1, 1 - slot)

    def compute(rows):
        x_lo, x_hi = _unpack_bf16_pair(xb_ref[slot, 0:rows, :])
        x_lo, x_hi = x_lo.astype(BF16), x_hi.astype(BF16)

        def w(ref, r, c):
            return ref[r, c].astype(BF16)

        lo, hi, full = slice(0, HALF), slice(HALF, D_MODEL), slice(None)
        gate = _dot(x_lo, w(wg_ref, lo, full)) + _dot(x_hi, w(wg_ref, hi, full))
        up = _dot(x_lo, w(wu_ref, lo, full)) + _dot(x_hi, w(wu_ref, hi, full))
        hid = (gate * jax.nn.sigmoid(gate) * up).astype(BF16)

        @pl.when(has_prev)
        def _():
            wait_prev_out()

        yb_ref[0:rows, :] = _pack_bf16_pair(_dot(hid, w(wd_ref, full, lo)), _dot(hid, w(wd_ref, full, hi)))
        write_out(rows)

    @pl.when(n > 0)
    def _():
        wait_gather(m_cur, slot)
        for m in range(1, MOE_SB + 1):
            @pl.when(m_cur == m)
            def _():
                compute(m * MOE_BLK)

    @pl.when((n == 0) & (m_cur > 0))
    def _():
        @pl.when(has_prev)
        def _():
            wait_prev_out()

        yb_ref[...] = jnp.zeros_like(yb_ref)
        for m in range(1, MOE_SB + 1):
            @pl.when(m_cur == m)
            def _():
                write_out(m * MOE_BLK)

    @pl.when((m_cur == 0) & has_prev)
    def _():
        wait_prev_out()


def _moe(x_slots, plan, w_gate, w_up, w_down, n_out_rows):
    sb_e, sb_m, sb_n, sb_base = plan
    sb_rows = MOE_SB * MOE_BLK
    wspec = lambda s: pl.BlockSpec((None,) + s, lambda b, se, *_: (se[b], 0, 0))
    grid_spec = pltpu.PrefetchScalarGridSpec(
        num_scalar_prefetch=4,
        grid=(sb_e.shape[0],),
        in_specs=[pl.BlockSpec(memory_space=pl.ANY),
                  wspec((D_MODEL, D_EXPERT)), wspec((D_MODEL, D_EXPERT)), wspec((D_EXPERT, D_MODEL))],
        out_specs=pl.BlockSpec(memory_space=pl.ANY),
        scratch_shapes=[pltpu.VMEM((2, sb_rows, HALF), jnp.uint32), pltpu.VMEM((sb_rows, HALF), jnp.uint32),
                        pltpu.SemaphoreType.DMA((2,)), pltpu.SemaphoreType.DMA((1,))],
    )
    return pl.pallas_call(
        _moe_kernel,
        grid_spec=grid_spec,
        out_shape=jax.ShapeDtypeStruct((n_out_rows, HALF), jnp.uint32),
        compiler_params=pltpu.CompilerParams(dimension_semantics=("arbitrary",),
                                             vmem_limit_bytes=MOE_VMEM_LIMIT),
        name="moe_experts",
    )(sb_e, sb_m, sb_n, sb_base, x_slots, w_gate, w_up, w_down)


def _dispatch_kernel(dest_ref, zb_ref, hn_ref, out_hbm, buf_ref, zero_ref, sem, zsem):
    i = pl.program_id(0)
    nt = pl.num_programs(0)
    tm = hn_ref.shape[0]
    t_all = nt * tm
    slot = lax.rem(i, 2)
    nblk = zb_ref.shape[0]

    def zero_copy(blk):
        dst = pl.ds(pl.multiple_of(blk * MOE_BLK, MOE_BLK), MOE_BLK)
        return pltpu.make_async_copy(zero_ref, out_hbm.at[dst, :], zsem.at[0])

    @pl.when(i == 0)
    def _():
        zero_ref[...] = jnp.zeros_like(zero_ref)

        def start(blk, c):
            @pl.when(zb_ref[blk] == 1)
            def _():
                zero_copy(blk).start()
            return c

        def wait(blk, c):
            @pl.when(zb_ref[blk] == 1)
            def _():
                zero_copy(blk).wait()
            return c

        lax.fori_loop(0, nblk, start, 0)
        lax.fori_loop(0, nblk, wait, 0)

    def wait_rows(sl):
        for k in range(2):
            pltpu.make_async_copy(buf_ref.at[sl], out_hbm.at[pl.ds(0, tm), :], sem.at[sl]).wait()

    @pl.when(i >= 2)
    def _():
        wait_rows(slot)

    buf_ref[slot] = hn_ref[...]
    base = i * tm
    for a in range(2 * tm):
        row, k = a // 2, a % 2
        d = dest_ref[k * t_all + base + row]
        pltpu.make_async_copy(buf_ref.at[slot, pl.ds(row, 1), :], out_hbm.at[pl.ds(d, 1), :],
                              sem.at[slot]).start()

    @pl.when(i == nt - 1)
    def _():
        wait_rows(slot)

        @pl.when(nt > 1)
        def _():
            wait_rows(1 - slot)


def _dispatch(dest, zero_blk, hn_pk, n_slots):
    tm = 2 * TOK_TILE
    grid_spec = pltpu.PrefetchScalarGridSpec(
        num_scalar_prefetch=2,
        grid=(hn_pk.shape[0] // tm,),
        in_specs=[pl.BlockSpec((tm, HALF), lambda i, d, z: (i, 0))],
        out_specs=pl.BlockSpec(memory_space=pl.ANY),
        scratch_shapes=[pltpu.VMEM((2, tm, HALF), jnp.uint32), pltpu.VMEM((MOE_BLK, HALF), jnp.uint32),
                        pltpu.SemaphoreType.DMA((2,)), pltpu.SemaphoreType.DMA((1,))],
    )
    return pl.pallas_call(
        _dispatch_kernel,
        grid_spec=grid_spec,
        out_shape=jax.ShapeDtypeStruct((n_slots, HALF), jnp.uint32),
        compiler_params=pltpu.CompilerParams(dimension_semantics=("arbitrary",), vmem_limit_bytes=VMEM_LIMIT),
        name="moe_dispatch",
    )(dest, zero_blk, hn_pk)


def _moe_plan(rtt, cnt):
    t_all = rtt.shape[1]
    n_assign = 2 * t_all
    e_flat = rtt[0:2].astype(jnp.int32).reshape(-1)
    rank = rtt[4:6].astype(jnp.int32).reshape(-1)
    counts = cnt[0, 0:N_EXPERTS].astype(jnp.int32)
    padded = (counts + MOE_BLK - 1) // MOE_BLK * MOE_BLK
    pad_end = jnp.cumsum(padded)
    pad_start = pad_end - padded
    before = jnp.arange(N_EXPERTS, dtype=jnp.int32)[:, None] < e_flat[None, :]
    dest = rank + jnp.sum(jnp.where(before, padded[:, None], 0), axis=0)
    nblk = -(-(n_assign + N_EXPERTS * (MOE_BLK - 1)) // MOE_BLK)
    n_slots = nblk * MOE_BLK
    blk = jnp.arange(nblk, dtype=jnp.int32)
    last_of_expert = jnp.any((blk[:, None] == (pad_end // MOE_BLK - 1)[None, :]) & (counts > 0)[None, :], axis=1)
    zero_blk = (last_of_expert | (blk >= pad_end[-1] // MOE_BLK)).astype(jnp.int32)
    k_e = padded // MOE_BLK
    sbc = (k_e + MOE_SB - 1) // MOE_SB
    sb_end = jnp.cumsum(sbc)
    sb_start = sb_end - sbc
    n_sb = (nblk + (MOE_SB - 1) * N_EXPERTS) // MOE_SB
    s = jnp.arange(n_sb, dtype=jnp.int32)
    sb_e = jnp.minimum(jnp.sum((s[:, None] >= sb_end[None, :]).astype(jnp.int32), axis=1), N_EXPERTS - 1)
    j = s - sb_start[sb_e]
    real = s < sb_end[-1]
    tail_blk = pad_end[-1] // MOE_BLK + MOE_SB * (s - sb_end[-1])
    sb_m = jnp.where(real, jnp.clip(k_e[sb_e] - MOE_SB * j, 0, MOE_SB), jnp.clip(nblk - tail_blk, 0, MOE_SB))
    sb_n = jnp.where(real, jnp.clip(counts[sb_e] - MOE_SB * MOE_BLK * j, 0, MOE_SB * MOE_BLK), 0)
    sb_base = jnp.where(real, pad_start[sb_e] + MOE_SB * MOE_BLK * j, jnp.minimum(tail_blk, nblk - 1) * MOE_BLK)
    i32 = lambda a: a.astype(jnp.int32)
    return (i32(sb_e), i32(sb_m), i32(sb_n), i32(sb_base)), i32(dest), zero_blk, n_slots


def _ple_out_kernel(dest_ref, h1_ref, rt_ref, pp_ref, ps_ref, nple_ref, wpg_ref, wp_ref, nfin_ref, ys_hbm,
                    op_ref, os_ref, yb_ref, sem, *, n_prompt_tiles):
    i = pl.program_id(0)
    nt = pl.num_programs(0)
    tm = h1_ref.shape[0]
    t_all = nt * tm
    slot = lax.rem(i, 2)

    def issue(tile, sl):
        base = tile * tm

        def grp(g, c):
            for j in range(DMA_UNROLL):
                row = g * (DMA_UNROLL // 2) + j // 2
                d = dest_ref[(j % 2) * t_all + base + row]
                pltpu.make_async_copy(ys_hbm.at[pl.ds(d, 1), :], yb_ref.at[sl, j % 2, pl.ds(row, 1), :],
                                      sem.at[sl]).start()
            return c

        lax.fori_loop(0, 2 * tm // DMA_UNROLL, grp, 0)

    @pl.when(i == 0)
    def _():
        issue(0, 0)

    def wait_rows(sl):
        for k in range(2):
            pltpu.make_async_copy(ys_hbm.at[pl.ds(0, tm), :], yb_ref.at[sl, k], sem.at[sl]).wait()

    wait_rows(slot)
    is_p = i < n_prompt_tiles
    rt = rt_ref[...]
    w0, w1 = rt[:, 2:3], rt[:, 3:4]
    lo0, hi0 = _unpack_bf16_pair(yb_ref[slot, 0])
    lo1, hi1 = _unpack_bf16_pair(yb_ref[slot, 1])
    h2 = h1_ref[...] + jnp.concatenate([w0 * lo0 + w1 * lo1, w0 * hi0 + w1 * hi1], axis=1)
    hb = _rms(h2, nple_ref[...]).astype(BF16)
    nbase = jnp.minimum(i + 1, nt - 1) * tm
    n_chunk = 8
    cw, ca = D_MODEL // n_chunk, 2 * tm // n_chunk
    gates = []
    for c in range(n_chunk):
        gates.append(jax.nn.sigmoid(_dot(hb, wpg_ref[:, c * cw:(c + 1) * cw])))
        for a in range(c * ca, (c + 1) * ca):
            d = dest_ref[(a % 2) * t_all + nbase + a // 2]
            pltpu.make_async_copy(ys_hbm.at[pl.ds(d, 1), :], yb_ref.at[1 - slot, a % 2, pl.ds(a // 2, 1), :],
                                  sem.at[1 - slot]).start()
    gate = jnp.concatenate(gates, axis=1)

    @pl.when(i == nt - 1)
    def _():
        wait_rows(1 - slot)

    p = jnp.where(is_p, pp_ref[...], ps_ref[...])
    h3 = h2 + _dot(p.astype(BF16), wp_ref[...]) * gate
    y = _rms(h3, nfin_ref[...])

    @pl.when(is_p)
    def _():
        op_ref[...] = y

    @pl.when(jnp.logical_not(is_p))
    def _():
        os_ref[...] = y


def _ple_out(dest, h1, rt, pp, ps, nple, wpg, wp, nfin, y_slots):
    tm = TOK_TILE
    npt, nst = pp.shape[0] // tm, ps.shape[0] // tm
    row = lambda w: pl.BlockSpec((tm, w), lambda i, d: (i, 0))
    prow = lambda w: pl.BlockSpec((tm, w), lambda i, d: (jnp.minimum(i, npt - 1), 0))
    srow = lambda w: pl.BlockSpec((tm, w), lambda i, d: (jnp.maximum(i - npt, 0), 0))
    const = lambda shape: pl.BlockSpec(shape, lambda i, d: (0,) * len(shape), pipeline_mode=pl.Buffered(1))
    grid_spec = pltpu.PrefetchScalarGridSpec(
        num_scalar_prefetch=1,
        grid=(npt + nst,),
        in_specs=[row(D_MODEL), row(LANES), prow(D_PLE), srow(D_PLE), const((1, D_MODEL)), const(wpg.shape),
                  const(wp.shape), const((1, D_MODEL)), pl.BlockSpec(memory_space=pl.ANY)],
        out_specs=[prow(D_MODEL), srow(D_MODEL)],
        scratch_shapes=[pltpu.VMEM((2, 2, tm, HALF), jnp.uint32), pltpu.SemaphoreType.DMA((2,))],
    )
    return pl.pallas_call(
        functools.partial(_ple_out_kernel, n_prompt_tiles=npt),
        grid_spec=grid_spec,
        out_shape=[jax.ShapeDtypeStruct((pp.shape[0], D_MODEL), F32),
                   jax.ShapeDtypeStruct((ps.shape[0], D_MODEL), F32)],
        compiler_params=pltpu.CompilerParams(dimension_semantics=("arbitrary",), vmem_limit_bytes=VMEM_LIMIT),
        name="ple_out",
    )(dest, h1, rt, pp, ps, nple, wpg, wp, nfin, y_slots)


def _s5_params(lam_re, lam_im, log_dt, b_re, b_im, c_re, c_im, d_skip):
    dt = jnp.exp(log_dt)[:, None]
    mag = jnp.exp(lam_re * dt)
    ab_re = mag * jnp.cos(lam_im * dt)
    ab_im = mag * jnp.sin(lam_im * dt)
    den = lam_re * lam_re + lam_im * lam_im
    nr = ab_re - 1.0
    f_re = (nr * lam_re + ab_im * lam_im) / den
    f_im = (ab_im * lam_re - nr * lam_im) / den
    bb_re = f_re[..., None] * b_re - f_im[..., None] * b_im
    bb_im = f_re[..., None] * b_im + f_im[..., None] * b_re
    eye = jnp.eye(S5_GB, dtype=F32)

    def bu_w(bb):
        bb = bb.reshape(S5_NGB, S5_GB, S5_STATE, S5_GROUP)
        w = jnp.einsum('nlph,lm->nlhmp', bb, eye)
        return w.reshape(S5_NGB, S5_GB * S5_GROUP, S5_SL)

    def c_w(c):
        c = c.reshape(S5_NGB, S5_GB, S5_GROUP, S5_STATE)
        w = jnp.einsum('nlhp,lm->nlpmh', c, eye)
        return w.reshape(S5_NGB, S5_SL, S5_GB * S5_GROUP)

    wbu = jnp.concatenate([bu_w(bb_re), bu_w(bb_im)], axis=2).astype(BF16)
    wc = jnp.concatenate([c_w(c_re), -c_w(c_im)], axis=1).astype(BF16)
    a_re = ab_re.reshape(S5_NGB, 1, S5_SL)
    a_im = ab_im.reshape(S5_NGB, 1, S5_SL)
    dsk = d_skip.reshape(S5_NGB, 1, S5_GB * S5_GROUP)
    return wbu, wc, a_re, a_im, dsk


def kernel(x_prompt, x_sample, p_prompt, p_sample, state_gla, state_s5_re, state_s5_im, norm_mix, w_in, gla_w_gate_up, gla_gate_bias, gla_norm, s5_lam_re, s5_lam_im, s5_log_dt, s5_b_re, s5_b_im, s5_c_re, s5_c_im, s5_d, s5_glu_w, s5_glu_b, s5_norm, w_out, norm_ffn, router_group, router_expert, w_gate, w_up, w_down, norm_ple, w_ple, w_ple_gate, norm_final):
    depth = w_in.shape[0]
    assert depth == 1
    i = 0
    bp, lp, _ = x_prompt.shape
    bs, ls, _ = x_sample.shape
    tp, ts = bp * lp, bs * ls
    t_all = tp + ts

    wm = _w_in_prep(w_in[i].astype(BF16))
    wgu = jnp.pad(gla_w_gate_up[i], ((0, LANES - GLA_RANK), (0, 0))).astype(BF16)
    gbias = gla_gate_bias[i].reshape(1, QK_W)
    wbu, wc, a_re, a_im, dsk = _s5_params(s5_lam_re[i], s5_lam_im[i], s5_log_dt[i], s5_b_re[i], s5_b_im[i],
                                          s5_c_re[i], s5_c_im[i], s5_d[i])
    glu_w = s5_glu_w[i].astype(BF16)
    wo = w_out[i].astype(BF16)
    wr32 = jnp.pad(jnp.concatenate([router_group[i], router_expert[i]], axis=1),
                   ((0, 0), (0, LANES - N_EGROUPS - N_EXPERTS)))
    wr_hi = wr32.astype(BF16)
    wr = jnp.stack([wr_hi, (wr32 - wr_hi.astype(F32)).astype(BF16)])
    wpg = w_ple_gate[i].astype(BF16)
    wp = w_ple[i].astype(BF16)
    vec = lambda a: a.reshape(1, -1)

    xp = x_prompt.reshape(tp, D_MODEL)
    xs = x_sample.reshape(ts, D_MODEL)

    qp, kp, vp, rp, lap, up = _in_proj(xp, vec(norm_mix[i]), wm, wgu, gbias, BF16)
    qs, ks, vs, rs, las, us = _in_proj(xs, vec(norm_mix[i]), wm, wgu, gbias, BF16)
    ogp, gla_p = _gla_prompt(qp, kp, vp, rp, lap, vec(gla_norm[i]), bp, lp)
    ogs, gla_s = _gla_sample(qs, ks, vs, rs, las, vec(gla_norm[i]), state_gla[i], bs, ls)
    wbu2, wc2, a_re2, a_im2 = _s5_prompt_params(wbu, wc, a_re, a_im, bp)
    ysp, re_p, im_p = _s5_prompt(up.reshape(bp, lp, D_S5), wbu2, wc2, a_re2, a_im2, dsk, tc=256)
    yss, re_s, im_s = _s5(us.reshape(bs, ls, D_S5), wbu, wc, a_re, a_im, dsk,
                          state_s5_re[i].reshape(bs, -1), state_s5_im[i].reshape(bs, -1),
                          nb=bs, tc=ls, has_state=True)

    h1, hn_all, rt_all, rtt, cnt = _mix_out(xp, xs, ogp, ogs, ysp.reshape(tp, D_S5), yss.reshape(ts, D_S5),
                                            glu_w, vec(s5_glu_b[i]), vec(s5_norm[i]), wo, vec(norm_ffn[i]), wr)

    plan, dest, zero_blk, n_slots = _moe_plan(rtt, cnt)
    x_slots = _dispatch(dest, zero_blk, hn_all, n_slots)
    y_slots = _moe(x_slots, plan, w_gate[i], w_up[i], w_down[i], n_slots)

    y_p, y_s = _ple_out(dest, h1, rt_all, p_prompt[i].reshape(tp, D_PLE), p_sample[i].reshape(ts, D_PLE),
                        vec(norm_ple[i]), wpg, wp, vec(norm_final), y_slots)

    s5shape = lambda a, b: a.reshape(1, b, S5_GROUPS, S5_STATE)
    return (y_p.reshape(bp, lp, D_MODEL), y_s.reshape(bs, ls, D_MODEL),
            gla_p[None], s5shape(re_p, bp), s5shape(im_p, bp),
            gla_s[None], s5shape(re_s, bs), s5shape(im_s, bs))
```

```python
import functools
import math

import jax
import jax.numpy as jnp
from jax import lax
from jax.experimental import pallas as pl
from jax.experimental.pallas import tpu as pltpu

F32 = jnp.float32
BF16 = jnp.bfloat16

D_MODEL = 2048
D_GLA = 1024
D_S5 = 1024
GLA_HEADS = 4
GLA_DV = 256
GLA_DK = 128
GLA_RANK = 16
GLA_CHUNK = 64
S5_GROUP = 16
S5_GROUPS = 64
S5_STATE = 64
N_EGROUPS = 4
N_EPG = 8
N_EXPERTS = 32
D_EXPERT = 512
D_PLE = 256
EPS = 1e-6

LANES = 128
QK_W = GLA_HEADS * GLA_DK
S5_GB = 8
S5_NGB = S5_GROUPS // S5_GB
S5_SL = S5_GB * S5_STATE
TOK_TILE = 256
MOE_BLK = 128
VMEM_LIMIT = 56 * 1024 * 1024
MOE_VMEM_LIMIT = 60 * 1024 * 1024


def _const_spec(shape):
    nd = len(shape)
    return pl.BlockSpec(shape, lambda *_: (0,) * nd, pipeline_mode=pl.Buffered(1))


def _rms(x, g):
    return x * lax.rsqrt(jnp.mean(x * x, axis=-1, keepdims=True) + EPS) * g


def _dot(a, b):
    return jnp.dot(a, b, preferred_element_type=F32)


def _log_sigmoid(x):
    return -(jnp.maximum(-x, 0.0) + jnp.log1p(jnp.exp(-jnp.abs(x))))


N_QKVR = 2 * QK_W + 2 * D_GLA
W_IN_COLS = N_QKVR + GLA_RANK + D_S5


def _w_in_prep_kernel(w_ref, o_ref):
    o_ref[:, 0:N_QKVR] = w_ref[:, 0:N_QKVR].astype(BF16)
    tail = w_ref[:, N_QKVR:W_IN_COLS]
    o_ref[:, N_QKVR:N_QKVR + D_S5] = tail[:, GLA_RANK:GLA_RANK + D_S5].astype(BF16)
    o_ref[:, N_QKVR + D_S5:N_QKVR + D_S5 + LANES] = tail[:, 0:LANES].astype(BF16)


def _w_in_prep(wi):
    rows = 256
    return pl.pallas_call(
        _w_in_prep_kernel,
        grid=(D_MODEL // rows,),
        in_specs=[pl.BlockSpec((rows, W_IN_COLS), lambda i: (i, 0))],
        out_specs=pl.BlockSpec((rows, N_QKVR + D_S5 + LANES), lambda i: (i, 0)),
        out_shape=jax.ShapeDtypeStruct((D_MODEL, N_QKVR + D_S5 + LANES), BF16),
        compiler_params=pltpu.CompilerParams(dimension_semantics=("arbitrary",), vmem_limit_bytes=VMEM_LIMIT),
        name="w_in_prep",
    )(wi)


def _in_proj_kernel(x_ref, g_ref, wm_ref, wgu_ref, gb_ref,
                    q_ref, k_ref, v_ref, r_ref, la_ref, u_ref):
    hb = _rms(x_ref[...], g_ref[...]).astype(BF16)

    def seg(a, b):
        return _dot(hb, wm_ref[:, a:b])

    q_ref[...] = (seg(0, QK_W) * (GLA_DK ** -0.5)).astype(q_ref.dtype)
    k_ref[...] = seg(QK_W, 2 * QK_W).astype(k_ref.dtype)
    v_ref[...] = seg(2 * QK_W, 2 * QK_W + D_GLA).astype(v_ref.dtype)
    r_ref[...] = seg(2 * QK_W + D_GLA, N_QKVR).astype(r_ref.dtype)
    u_ref[...] = seg(N_QKVR, N_QKVR + D_S5)
    zg = seg(N_QKVR + D_S5, N_QKVR + D_S5 + LANES)
    xg = _dot(zg.astype(BF16), wgu_ref[...]) + gb_ref[...]
    la_ref[...] = _log_sigmoid(xg) * (1.0 / 16.0)


def _in_proj(x2d, g, wm, wgu, gbias, act_dtype):
    t = x2d.shape[0]
    tm = TOK_TILE
    row = lambda w: pl.BlockSpec((tm, w), lambda i: (i, 0))
    return pl.pallas_call(
        _in_proj_kernel,
        grid=(t // tm,),
        in_specs=[row(D_MODEL), _const_spec((1, D_MODEL)), _const_spec(wm.shape),
                  _const_spec(wgu.shape), _const_spec((1, QK_W))],
        out_specs=[row(QK_W), row(QK_W), row(D_GLA), row(D_GLA), row(QK_W), row(D_S5)],
        out_shape=[jax.ShapeDtypeStruct((t, QK_W), act_dtype), jax.ShapeDtypeStruct((t, QK_W), act_dtype),
                   jax.ShapeDtypeStruct((t, D_GLA), act_dtype), jax.ShapeDtypeStruct((t, D_GLA), act_dtype),
                   jax.ShapeDtypeStruct((t, QK_W), F32), jax.ShapeDtypeStruct((t, D_S5), F32)],
        compiler_params=pltpu.CompilerParams(dimension_semantics=("arbitrary",), vmem_limit_bytes=VMEM_LIMIT),
        name="in_proj",
    )(x2d, g, wm, wgu, gbias)


_NT = (((1,), (1,)), ((), ()))
_TN = (((0,), (0,)), ((), ()))


def _gla_pre(q, k, la, c):
    r = q.shape[0]
    shift = int(math.log2(c))
    ri = lax.broadcasted_iota(jnp.int32, (r, r), 0)
    si = lax.broadcasted_iota(jnp.int32, (r, r), 1)
    mask = ((ri >> shift) == (si >> shift)) & (ri >= si)
    tri = jnp.where(mask, 1.0, 0.0).astype(BF16)
    hi = la.astype(BF16)
    r1 = la - hi.astype(F32)
    mid = r1.astype(BF16)
    lo = (r1 - mid.astype(F32)).astype(BF16)
    cum = _dot(tri, hi) + _dot(tri, mid) + _dot(tri, lo)
    last = jnp.concatenate([jnp.broadcast_to(cum[(i + 1) * c - 1:(i + 1) * c, :], (c, cum.shape[1]))
                            for i in range(r // c)], axis=0)
    qe = (q * jnp.exp(cum)).astype(BF16)
    ke = (k * jnp.exp(-cum)).astype(BF16)
    kd = (k * jnp.exp(last - cum)).astype(BF16)
    return qe, ke, kd, cum, mask


def _gla_intra(qe, ke, v, mask):
    sc = lax.dot_general(qe, ke, _NT, preferred_element_type=F32)
    return _dot(jnp.where(mask, sc, 0.0).astype(BF16), v)


def _gla_finish(o, r, g):
    rf = r.astype(F32)
    return _rms(o, g) * (rf * jax.nn.sigmoid(rf))


GLA_SAFE_LOG_DECAY = -60.0


def _gla_token_step(t, rows, q, k, v, a_all, st_ref, oacc_ref):
    m = rows == t
    a = jnp.sum(jnp.where(m, a_all, 0.0), axis=0, keepdims=True)
    kt = jnp.where(m, k, 0.0).astype(BF16)
    qt = jnp.where(m, q, 0.0).astype(BF16)
    vt = jnp.where(m, v, jnp.zeros_like(v))
    st = st_ref[...] * a + lax.dot_general(vt, kt, _TN, preferred_element_type=F32)
    st_ref[...] = st
    oacc_ref[...] += lax.dot_general(qt, st.astype(BF16), _NT, preferred_element_type=F32)


def _gla_prompt_kernel(q_ref, k_ref, v_ref, r_ref, la_ref, g_ref, o_ref, sfin_ref, st_ref, oacc_ref,
                       *, n_chunks):
    j = pl.program_id(1)
    nseq = q_ref.shape[0]

    @pl.when(j == 0)
    def _():
        st_ref[...] = jnp.zeros_like(st_ref)

    c = GLA_CHUNK
    pre = [_gla_pre(q_ref[s].astype(F32), k_ref[s].astype(F32), la_ref[s], c) for s in range(nseq)]
    g = g_ref[...]
    safe = functools.reduce(jnp.minimum, [jnp.min(p[3]) for p in pre]) >= GLA_SAFE_LOG_DECAY

    @pl.when(safe)
    def _():
        for s in range(nseq):
            qe, ke, kd, cum, mask = pre[s]
            for h in range(GLA_HEADS):
                kc = slice(h * GLA_DK, (h + 1) * GLA_DK)
                vc = slice(h * GLA_DV, (h + 1) * GLA_DV)
                v = v_ref[s, :, vc]
                qe_h, kd_h = qe[:, kc], kd[:, kc]
                o = _gla_intra(qe_h, ke[:, kc], v, mask)
                st = st_ref[s, h]
                inter = []
                for ci in range(n_chunks):
                    rows = slice(ci * c, (ci + 1) * c)
                    inter.append(lax.dot_general(qe_h[rows], st.astype(BF16), _NT, preferred_element_type=F32))
                    dec = jnp.exp(cum[(ci + 1) * c - 1:(ci + 1) * c, kc])
                    st = st * dec + lax.dot_general(v[rows], kd_h[rows], _TN, preferred_element_type=F32)
                st_ref[s, h] = st
                o = o + jnp.concatenate(inter, axis=0)
                o_ref[s, :, vc] = _gla_finish(o, r_ref[s, :, vc], g).astype(o_ref.dtype)

    @pl.when(jnp.logical_not(safe))
    def _():
        n_rows = n_chunks * c
        rows = lax.broadcasted_iota(jnp.int32, (n_rows, 1), 0)
        for s in range(nseq):
            for h in range(GLA_HEADS):
                kc = slice(h * GLA_DK, (h + 1) * GLA_DK)
                vc = slice(h * GLA_DV, (h + 1) * GLA_DV)
                q, k, v = q_ref[s, :, kc].astype(F32), k_ref[s, :, kc].astype(F32), v_ref[s, :, vc]
                a_all = jnp.exp(la_ref[s, :, kc])
                oacc_ref[...] = jnp.zeros_like(oacc_ref)

                def step(t, carry, s=s, h=h, q=q, k=k, v=v, a_all=a_all):
                    _gla_token_step(t, rows, q, k, v, a_all, st_ref.at[s, h], oacc_ref)
                    return carry

                lax.fori_loop(0, n_rows, step, 0)
                o_ref[s, :, vc] = _gla_finish(oacc_ref[...], r_ref[s, :, vc], g).astype(o_ref.dtype)

    @pl.when(j == pl.num_programs(1) - 1)
    def _():
        for s in range(nseq):
            for h in range(GLA_HEADS):
                sfin_ref[s, h] = st_ref[s, h].T


def _gla_prompt(q, k, v, r, la, g, batch, seq):
    rb = 4 * GLA_CHUNK
    ns = 2
    row = lambda w: pl.BlockSpec((ns, rb, w), lambda b, j: (b, j, 0))
    seq3 = lambda a: a.reshape(batch, seq, a.shape[-1])
    o, sfin = pl.pallas_call(
        functools.partial(_gla_prompt_kernel, n_chunks=rb // GLA_CHUNK),
        grid=(batch // ns, seq // rb),
        in_specs=[row(QK_W), row(QK_W), row(D_GLA), row(D_GLA), row(QK_W), _const_spec((1, GLA_DV))],
        out_specs=[row(D_GLA),
                   pl.BlockSpec((ns, GLA_HEADS, GLA_DK, GLA_DV), lambda b, j: (b, 0, 0, 0))],
        out_shape=[jax.ShapeDtypeStruct((batch, seq, D_GLA), BF16),
                   jax.ShapeDtypeStruct((batch, GLA_HEADS, GLA_DK, GLA_DV), F32)],
        scratch_shapes=[pltpu.VMEM((ns, GLA_HEADS, GLA_DV, GLA_DK), F32), pltpu.VMEM((rb, GLA_DV), F32)],
        compiler_params=pltpu.CompilerParams(dimension_semantics=("arbitrary", "arbitrary"),
                                             vmem_limit_bytes=VMEM_LIMIT),
        name="gla_prompt",
    )(seq3(q), seq3(k), seq3(v), seq3(r), seq3(la), g)
    return o.reshape(batch * seq, D_GLA), sfin


def _gla_sample_kernel(q_ref, k_ref, v_ref, r_ref, la_ref, g_ref, s0_ref, o_ref, sfin_ref, st_ref, oacc_ref,
                       *, n_seq, seq):
    qe, ke, kd, cum, mask = _gla_pre(q_ref[...].astype(F32), k_ref[...].astype(F32), la_ref[...], seq)
    g = g_ref[...]
    safe = jnp.min(cum) >= GLA_SAFE_LOG_DECAY

    @pl.when(safe)
    def _():
        lasts = jnp.concatenate([cum[(s + 1) * seq - 1:(s + 1) * seq, :] for s in range(n_seq)]
                                + [jnp.zeros((GLA_DK - n_seq, cum.shape[1]), F32)], axis=0)
        pair = 2 * seq
        upper = lax.broadcasted_iota(jnp.int32, (pair, GLA_DK), 0) < seq
        for h in range(GLA_HEADS):
            kc = slice(h * GLA_DK, (h + 1) * GLA_DK)
            vc = slice(h * GLA_DV, (h + 1) * GLA_DV)
            v = v_ref[:, vc]
            qe_h, kd_h = qe[:, kc], kd[:, kc]
            o = _gla_intra(qe_h, ke[:, kc], v, mask)
            dec_t = jnp.exp(lasts[:, kc].T)
            inter = []
            for p in range(n_seq // 2):
                rows = slice(p * pair, (p + 1) * pair)
                qe_p, kd_p, v_p = qe_h[rows], kd_h[rows], v[rows]
                for half in range(2):
                    s = 2 * p + half
                    s0 = s0_ref[s, h]
                    o_s = _dot(qe_p, s0.astype(BF16))
                    inter.append(o_s[half * seq:(half + 1) * seq])
                    kd_s = jnp.where(upper if half == 0 else ~upper, kd_p, jnp.zeros_like(kd_p))
                    dec = jnp.broadcast_to(dec_t[:, s:s + 1], (GLA_DK, GLA_DV))
                    sfin_ref[s, h] = s0 * dec + lax.dot_general(kd_s, v_p, _TN, preferred_element_type=F32)
            o = o + jnp.concatenate(inter, axis=0)
            o_ref[:, vc] = _gla_finish(o, r_ref[:, vc], g).astype(o_ref.dtype)

    @pl.when(jnp.logical_not(safe))
    def _():
        n_rows = n_seq * seq
        rows = lax.broadcasted_iota(jnp.int32, (n_rows, 1), 0)
        for h in range(GLA_HEADS):
            kc = slice(h * GLA_DK, (h + 1) * GLA_DK)
            vc = slice(h * GLA_DV, (h + 1) * GLA_DV)
            q, k, v = q_ref[:, kc].astype(F32), k_ref[:, kc].astype(F32), v_ref[:, vc]
            a_all = jnp.exp(la_ref[:, kc])
            oacc_ref[...] = jnp.zeros_like(oacc_ref)

            def step(t, carry):
                s = t // seq

                @pl.when(t % seq == 0)
                def _():
                    st_ref[...] = s0_ref[s, h].T

                _gla_token_step(t, rows, q, k, v, a_all, st_ref, oacc_ref)

                @pl.when(t % seq == seq - 1)
                def _():
                    sfin_ref[s, h] = st_ref[...].T

                return carry

            lax.fori_loop(0, n_rows, step, 0)
            o_ref[:, vc] = _gla_finish(oacc_ref[...], r_ref[:, vc], g).astype(o_ref.dtype)


def _gla_sample(q, k, v, r, la, g, s0, batch, seq):
    ns = 16
    rb = ns * seq
    row = lambda w: pl.BlockSpec((rb, w), lambda i: (i, 0))
    st = pl.BlockSpec((ns, GLA_HEADS, GLA_DK, GLA_DV), lambda i: (i, 0, 0, 0))
    return pl.pallas_call(
        functools.partial(_gla_sample_kernel, n_seq=ns, seq=seq),
        grid=(batch // ns,),
        in_specs=[row(QK_W), row(QK_W), row(D_GLA), row(D_GLA), row(QK_W), _const_spec((1, GLA_DV)), st],
        out_specs=[row(D_GLA), st],
        out_shape=[jax.ShapeDtypeStruct((batch * seq, D_GLA), BF16),
                   jax.ShapeDtypeStruct((batch, GLA_HEADS, GLA_DK, GLA_DV), F32)],
        scratch_shapes=[pltpu.VMEM((GLA_DV, GLA_DK), F32), pltpu.VMEM((rb, GLA_DV), F32)],
        compiler_params=pltpu.CompilerParams(dimension_semantics=("arbitrary",), vmem_limit_bytes=VMEM_LIMIT),
        name="gla_sample",
    )(q, k, v, r, la, g, s0)


def _s5_kernel(u_ref, wbu_ref, wc_ref, are_ref, aim_ref, d_ref, h0r_ref, h0i_ref,
               y_ref, sre_ref, sim_ref, bu_ref, xs_ref, car_ref, *, nb, tc, has_state):
    j = pl.program_id(2)

    @pl.when(j == 0)
    def _():
        if has_state:
            car_ref[0] = h0r_ref[...]
            car_ref[1] = h0i_ref[...]
        else:
            car_ref[...] = jnp.zeros_like(car_ref)

    u2 = u_ref[...].reshape(nb * tc, LANES)
    ub = u2.astype(BF16)
    nl = S5_SL // LANES
    for l in range(2 * nl):
        bu_ref[l] = _dot(ub, wbu_ref[0, :, l * LANES:(l + 1) * LANES])
    a_r = [jnp.broadcast_to(are_ref[0, :, l * LANES:(l + 1) * LANES], (nb, LANES)) for l in range(nl)]
    a_i = [jnp.broadcast_to(aim_ref[0, :, l * LANES:(l + 1) * LANES], (nb, LANES)) for l in range(nl)]

    def step(t, carry):
        rows = pl.ds(t, nb, stride=tc)
        out = []
        for l in range(nl):
            xr, xi = carry[2 * l], carry[2 * l + 1]
            nr = a_r[l] * xr - a_i[l] * xi + bu_ref[l, rows, :]
            ni = a_r[l] * xi + a_i[l] * xr + bu_ref[nl + l, rows, :]
            xs_ref[l, rows, :] = nr
            xs_ref[nl + l, rows, :] = ni
            out += [nr, ni]
        return tuple(out)

    init = []
    for l in range(nl):
        init += [car_ref[0, :, l * LANES:(l + 1) * LANES], car_ref[1, :, l * LANES:(l + 1) * LANES]]
    fin = lax.fori_loop(0, tc, step, tuple(init), unroll=8)
    xr = jnp.concatenate([fin[2 * l] for l in range(nl)], axis=1)
    xi = jnp.concatenate([fin[2 * l + 1] for l in range(nl)], axis=1)
    car_ref[0] = xr
    car_ref[1] = xi
    y = d_ref[0] * u2
    for l in range(2 * nl):
        y = y + _dot(xs_ref[l].astype(BF16), wc_ref[0, l * LANES:(l + 1) * LANES, :])
    y_ref[...] = y.reshape(nb, tc, LANES)

    @pl.when(j == pl.num_programs(2) - 1)
    def _():
        sre_ref[...] = xr
        sim_ref[...] = xi


def _s5(u3d, wbu, wc, a_re, a_im, dsk, h0r, h0i, nb, tc, has_state):
    batch, seq, _ = u3d.shape
    grid = (S5_NGB, batch // nb, seq // tc)
    st = pl.BlockSpec((nb, S5_SL), lambda g, b, j: (b, g))
    par = lambda w: pl.BlockSpec((1, 1, w), lambda g, b, j: (g, 0, 0))
    ublk = pl.BlockSpec((nb, tc, LANES), lambda g, b, j: (b, j, g))
    return pl.pallas_call(
        functools.partial(_s5_kernel, nb=nb, tc=tc, has_state=has_state),
        grid=grid,
        in_specs=[ublk,
                  pl.BlockSpec((1, LANES, 2 * S5_SL), lambda g, b, j: (g, 0, 0)),
                  pl.BlockSpec((1, 2 * S5_SL, LANES), lambda g, b, j: (g, 0, 0)),
                  par(S5_SL), par(S5_SL), par(LANES), st, st],
        out_specs=[ublk, st, st],
        out_shape=[jax.ShapeDtypeStruct(u3d.shape, F32),
                   jax.ShapeDtypeStruct((batch, S5_GROUPS * S5_STATE), F32),
                   jax.ShapeDtypeStruct((batch, S5_GROUPS * S5_STATE), F32)],
        scratch_shapes=[pltpu.VMEM((2 * S5_SL // LANES, nb * tc, LANES), F32),
                        pltpu.VMEM((2 * S5_SL // LANES, nb * tc, LANES), F32),
                        pltpu.VMEM((2, nb, S5_SL), F32)],
        compiler_params=pltpu.CompilerParams(dimension_semantics=("arbitrary",) * 3,
                                             vmem_limit_bytes=VMEM_LIMIT),
        name="s5_state" if has_state else "s5_zero",
    )(u3d, wbu, wc, a_re, a_im, dsk, h0r, h0i)


S5_J = 2
S5_HL = S5_SL // S5_J
S5_PAIR = 2


def _s5_prompt_kernel(u_ref, wbu_ref, wc_ref, are_ref, aim_ref, d_ref, y_ref, sre_ref, sim_ref,
                      u2_ref, lhs_ref, bu_ref, xs_ref, y2_ref, yo_ref, car_ref, *, nb, tc):
    g = pl.program_id(0)
    j = pl.program_id(1)
    rows = nb * S5_J
    npair = S5_PAIR

    @pl.when((g == 0) & (j == 0))
    def _():
        lhs_ref[...] = jnp.zeros_like(lhs_ref)

    @pl.when(j == 0)
    def _():
        car_ref[...] = jnp.zeros_like(car_ref)

    u2 = u_ref[...].reshape(nb * tc, npair * LANES)
    for p in range(npair):
        u2_ref[p] = u2[:, p * LANES:(p + 1) * LANES]

    def build(t, c):
        for p in range(npair):
            u4 = u2_ref[p, pl.ds(t, nb, stride=tc), :]
            for jj in range(S5_J):
                lhs_ref[p, t, jj * nb:(jj + 1) * nb, jj * LANES:(jj + 1) * LANES] = u4
        return c

    lax.fori_loop(0, tc, build, 0, unroll=True)
    for p in range(npair):
        lhs = lhs_ref[p].reshape(tc * rows, S5_J * LANES).astype(BF16)
        bu_ref[p] = _dot(lhs, wbu_ref[p]).reshape(tc, rows, 2 * S5_HL)
    ar = [are_ref[p] for p in range(npair)]
    ai = [aim_ref[p] for p in range(npair)]

    def step(t, carry):
        out = []
        for p in range(npair):
            xr, xi = carry[2 * p], carry[2 * p + 1]
            tile = bu_ref[p, t]
            nr = ar[p] * xr - ai[p] * xi + tile[:, 0:S5_HL]
            ni = ar[p] * xi + ai[p] * xr + tile[:, S5_HL:2 * S5_HL]
            xs_ref[p, t] = jnp.concatenate([nr, ni], axis=1)
            out += [nr, ni]
        return tuple(out)

    init = tuple(car_ref[p, c] for p in range(npair) for c in range(2))
    fin = lax.fori_loop(0, tc, step, init, unroll=True)
    for p in range(npair):
        car_ref[p, 0] = fin[2 * p]
        car_ref[p, 1] = fin[2 * p + 1]
        xs = xs_ref[p].reshape(tc * rows, 2 * S5_HL).astype(BF16)
        y2_ref[p] = _dot(xs, wc_ref[p]).reshape(tc, rows, S5_J * LANES)
    first_half = lax.broadcasted_iota(jnp.int32, (rows, LANES), 0) < nb

    def unperm(t, c):
        for p in range(npair):
            t2 = y2_ref[p, t]
            part = jnp.where(first_half, t2[:, 0:LANES], t2[:, LANES:2 * LANES])
            yo_ref[p, pl.ds(t, nb, stride=tc), :] = (part + pltpu.roll(part, nb, axis=0))[0:nb]
        return c

    lax.fori_loop(0, tc, unperm, 0, unroll=True)
    y = jnp.concatenate([yo_ref[p] + d_ref[p] * u2_ref[p] for p in range(npair)], axis=1)
    y_ref[...] = y.reshape(nb, tc, npair * LANES)

    @pl.when(j == pl.num_programs(1) - 1)
    def _():
        def gather_state(x):
            return jnp.concatenate([x[jj * nb:(jj + 1) * nb] for jj in range(S5_J)], axis=1)

        sre_ref[...] = jnp.concatenate([gather_state(fin[2 * p]) for p in range(npair)], axis=1)
        sim_ref[...] = jnp.concatenate([gather_state(fin[2 * p + 1]) for p in range(npair)], axis=1)


def _s5_prompt(u3d, wbu2, wc2, a_re2, a_im2, dsk, tc):
    nb, seq, _ = u3d.shape
    rows = nb * S5_J
    assert rows == 8
    np_ = S5_PAIR
    st = pl.BlockSpec((nb, np_ * S5_SL), lambda g, j: (0, g))
    gblk = lambda s: pl.BlockSpec((np_,) + s, lambda g, j: (g, 0, 0))
    ublk = pl.BlockSpec((nb, tc, np_ * LANES), lambda g, j: (0, j, g))
    return pl.pallas_call(
        functools.partial(_s5_prompt_kernel, nb=nb, tc=tc),
        grid=(S5_NGB // np_, seq // tc),
        in_specs=[ublk, gblk((S5_J * LANES, 2 * S5_HL)), gblk((2 * S5_HL, S5_J * LANES)),
                  gblk((rows, S5_HL)), gblk((rows, S5_HL)), gblk((1, LANES))],
        out_specs=[ublk, st, st],
        out_shape=[jax.ShapeDtypeStruct(u3d.shape, F32),
                   jax.ShapeDtypeStruct((nb, S5_GROUPS * S5_STATE), F32),
                   jax.ShapeDtypeStruct((nb, S5_GROUPS * S5_STATE), F32)],
        scratch_shapes=[pltpu.VMEM((np_, nb * tc, LANES), F32),
                        pltpu.VMEM((np_, tc, rows, S5_J * LANES), F32),
                        pltpu.VMEM((np_, tc, rows, 2 * S5_HL), F32),
                        pltpu.VMEM((np_, tc, rows, 2 * S5_HL), F32),
                        pltpu.VMEM((np_, tc, rows, S5_J * LANES), F32),
                        pltpu.VMEM((np_, nb * tc, LANES), F32),
                        pltpu.VMEM((np_, 2, rows, S5_HL), F32)],
        compiler_params=pltpu.CompilerParams(dimension_semantics=("arbitrary",) * 2,
                                             vmem_limit_bytes=VMEM_LIMIT),
        name="s5_prompt",
    )(u3d, wbu2, wc2, a_re2, a_im2, dsk)


def _s5_prompt_params(wbu, wc, a_re, a_im, nb):
    h = S5_HL
    top = jnp.concatenate([wbu[:, :, 0:h], wbu[:, :, S5_SL:S5_SL + h]], axis=2)
    bot = jnp.concatenate([wbu[:, :, h:2 * h], wbu[:, :, S5_SL + h:S5_SL + 2 * h]], axis=2)
    wbu2 = jnp.concatenate([top, bot], axis=1)
    wc_j = [jnp.concatenate([wc[:, jj * h:(jj + 1) * h, :], wc[:, S5_SL + jj * h:S5_SL + (jj + 1) * h, :]],
                            axis=1) for jj in range(S5_J)]
    wc2 = jnp.concatenate(wc_j, axis=2)
    tile = lambda a: jnp.repeat(a.reshape(S5_NGB, S5_J, h), nb, axis=1)
    return wbu2, wc2, tile(a_re), tile(a_im)


def _mix_out_kernel(xp_ref, xs_ref, ogp_ref, ogs_ref, ysp_ref, yss_ref, glu_w_ref, glu_b_ref, s5n_ref, wo_ref,
                    nffn_ref, wr_ref, h1_ref, hn_ref, rt_ref, rtt_ref, cnt_ref, *, n_prompt_tiles):
    is_p = pl.program_id(0) < n_prompt_tiles
    x = jnp.where(is_p, xp_ref[...], xs_ref[...])
    og = jnp.where(is_p, ogp_ref[...], ogs_ref[...])
    y = jax.nn.gelu(jnp.where(is_p, ysp_ref[...], yss_ref[...]))
    y = y * jax.nn.sigmoid(_dot(y.astype(BF16), glu_w_ref[...]) + glu_b_ref[...])
    yn = _rms(y, s5n_ref[...]).astype(BF16)
    mix = _dot(og, wo_ref[0:D_GLA, :]) + _dot(yn, wo_ref[D_GLA:D_GLA + D_S5, :])
    h1 = x + mix
    h1_ref[...] = h1
    hn = _rms(h1, nffn_ref[...])
    hn_ref[...] = _pack_bf16_pair(hn[:, 0:HALF], hn[:, HALF:D_MODEL])
    hn_hi = hn.astype(BF16)
    hn_lo = (hn - hn_hi.astype(F32)).astype(BF16)
    logits = _dot(hn_hi, wr_ref[0]) + _dot(hn_hi, wr_ref[1]) + _dot(hn_lo, wr_ref[0])
    rt = _route(logits)

    @pl.when(pl.program_id(0) == 0)
    def _():
        cnt_ref[...] = jnp.zeros_like(cnt_ref)

    tm = rt.shape[0]
    lane = lax.broadcasted_iota(jnp.int32, rt.shape, 1).astype(F32)
    oh0 = lane == rt[:, 0:1]
    oh1 = lane == rt[:, 1:2]
    both = jnp.where(oh0 | oh1, 1.0, 0.0)
    ri = lax.broadcasted_iota(jnp.int32, (tm, tm), 0)
    ci = lax.broadcasted_iota(jnp.int32, (tm, tm), 1)
    before = _dot(jnp.where(ri > ci, 1.0, 0.0).astype(BF16), both.astype(BF16)) + cnt_ref[...]
    rank0 = jnp.sum(jnp.where(oh0, before, 0.0), axis=-1, keepdims=True)
    rank1 = jnp.sum(jnp.where(oh1, before, 0.0), axis=-1, keepdims=True)
    cnt_ref[...] += jnp.sum(both, axis=0, keepdims=True)
    rt = jnp.where(lane == 4.0, rank0, jnp.where(lane == 5.0, rank1, rt))
    rt_ref[...] = rt
    rtt_ref[...] = rt.T[0:8, :]


def _route(logits):
    col = lax.broadcasted_iota(jnp.int32, logits.shape, 1)
    colf = col.astype(F32)
    neg = -jnp.inf

    def first_argmax(vals):
        m = jnp.max(vals, axis=-1, keepdims=True)
        idx = jnp.min(jnp.where(vals == m, colf, float(LANES)), axis=-1, keepdims=True)
        return m, idx

    lg = jnp.where(col < N_EGROUPS, logits, neg)
    gmax, gsel = first_argmax(lg)
    p_g = 1.0 / jnp.sum(jnp.exp(lg - gmax), axis=-1, keepdims=True)
    ecol = col - N_EGROUPS
    egrp = (ecol >> int(math.log2(N_EPG))).astype(F32)
    in_group = (ecol >= 0) & (ecol < N_EXPERTS) & (egrp == gsel)
    le = jnp.where(in_group, logits, neg)
    m1, i1 = first_argmax(le)
    le2 = jnp.where(colf == i1, neg, le)
    m2, i2 = first_argmax(le2)
    e2 = jnp.exp(m2 - m1)
    den = 1.0 + e2
    w1 = p_g * (1.0 / den)
    w2 = p_g * (e2 / den)
    e1f = i1 - float(N_EGROUPS)
    e2f = i2 - float(N_EGROUPS)
    out = jnp.where(col == 0, e1f, jnp.where(col == 1, e2f, jnp.where(col == 2, w1, jnp.where(col == 3, w2, 0.0))))
    return out


def _mix_out(xp, xs, ogp, ogs, ysp, yss, glu_w, glu_b, s5n, wo, nffn, wr):
    tm = TOK_TILE
    npt, nst = xp.shape[0] // tm, xs.shape[0] // tm
    t = (npt + nst) * tm
    row = lambda w: pl.BlockSpec((tm, w), lambda i: (i, 0))
    prow = lambda w: pl.BlockSpec((tm, w), lambda i: (jnp.minimum(i, npt - 1), 0))
    srow = lambda w: pl.BlockSpec((tm, w), lambda i: (jnp.maximum(i - npt, 0), 0))
    return pl.pallas_call(
        functools.partial(_mix_out_kernel, n_prompt_tiles=npt),
        grid=(npt + nst,),
        in_specs=[prow(D_MODEL), srow(D_MODEL), prow(D_GLA), srow(D_GLA), prow(D_S5), srow(D_S5),
                  _const_spec(glu_w.shape), _const_spec((1, D_S5)), _const_spec((1, D_S5)),
                  _const_spec(wo.shape), _const_spec((1, D_MODEL)), _const_spec(wr.shape)],
        out_specs=[row(D_MODEL), row(HALF), row(LANES), pl.BlockSpec((8, tm), lambda i: (0, i)),
                   pl.BlockSpec((1, LANES), lambda i: (0, 0))],
        out_shape=[jax.ShapeDtypeStruct((t, D_MODEL), F32), jax.ShapeDtypeStruct((t, HALF), jnp.uint32),
                   jax.ShapeDtypeStruct((t, LANES), F32), jax.ShapeDtypeStruct((8, t), F32),
                   jax.ShapeDtypeStruct((1, LANES), F32)],
        compiler_params=pltpu.CompilerParams(dimension_semantics=("arbitrary",), vmem_limit_bytes=VMEM_LIMIT),
        name="mix_out",
    )(xp, xs, ogp, ogs, ysp, yss, glu_w, glu_b, s5n, wo, nffn, wr)


DMA_UNROLL = 8


MOE_SB = 6
HALF = D_MODEL // 2


def _pack_bf16_pair(lo, hi):
    def bits(x):
        b = pltpu.bitcast(x, jnp.uint32)
        return (b + jnp.uint32(0x7FFF) + ((b >> 16) & jnp.uint32(1))) >> 16
    return bits(lo) | (bits(hi) << 16)


def _unpack_bf16_pair(u):
    return (pltpu.bitcast(u << 16, F32), pltpu.bitcast(u & jnp.uint32(0xFFFF0000), F32))


def _moe_kernel(se_ref, sm_ref, sn_ref, sbase_ref, xs_hbm, wg_ref, wu_ref, wd_ref, out_hbm,
                xb_ref, yb_ref, gsem, ssem):
    b = pl.program_id(0)
    nb = pl.num_programs(0)
    n = sn_ref[b]
    slot = lax.rem(b, 2)

    def issue_gather(blk, sl):
        base = pl.multiple_of(sbase_ref[blk], MOE_BLK)
        for m in range(1, MOE_SB + 1):
            @pl.when((sn_ref[blk] > 0) & (sm_ref[blk] == m))
            def _():
                rows = m * MOE_BLK
                pltpu.make_async_copy(xs_hbm.at[pl.ds(base, rows), :], xb_ref.at[sl, pl.ds(0, rows), :],
                                      gsem.at[sl]).start()

    def wait_gather(blocks, sl):
        rows = pl.ds(0, pl.multiple_of(blocks * MOE_BLK, MOE_BLK))
        pltpu.make_async_copy(xs_hbm.at[rows, :], xb_ref.at[sl, rows, :], gsem.at[sl]).wait()

    m_cur = sm_ref[b]
    m_prev = sm_ref[jnp.maximum(b - 1, 0)]
    has_prev = (b > 0) & (m_prev > 0)

    def out_copy(rows):
        dst = pl.ds(pl.multiple_of(sbase_ref[b], MOE_BLK), rows)
        return pltpu.make_async_copy(yb_ref.at[pl.ds(0, rows), :], out_hbm.at[dst, :], ssem.at[0])

    def wait_prev_out():
        rows = pl.ds(0, pl.multiple_of(m_prev * MOE_BLK, MOE_BLK))
        pltpu.make_async_copy(yb_ref.at[rows, :], out_hbm.at[rows, :], ssem.at[0]).wait()

    def write_out(rows):
        out_copy(rows).start()

        @pl.when(b == nb - 1)
        def _():
            out_copy(rows).wait()

    @pl.when(b == 0)
    def _():
        issue_gather(0, 0)

    @pl.when(b + 1 < nb)
    def _():
        issue_gather(b + 1, 1 - slot)

    def compute(rows):
        x_lo, x_hi = _unpack_bf16_pair(xb_ref[slot, 0:rows, :])
        x_lo, x_hi = x_lo.astype(BF16), x_hi.astype(BF16)

        def w(ref, r, c):
            return ref[r, c].astype(BF16)

        lo, hi, full = slice(0, HALF), slice(HALF, D_MODEL), slice(None)
        gate = _dot(x_lo, w(wg_ref, lo, full)) + _dot(x_hi, w(wg_ref, hi, full))
        up = _dot(x_lo, w(wu_ref, lo, full)) + _dot(x_hi, w(wu_ref, hi, full))
        hid = (gate * jax.nn.sigmoid(gate) * up).astype(BF16)

        @pl.when(has_prev)
        def _():
            wait_prev_out()

        yb_ref[0:rows, :] = _pack_bf16_pair(_dot(hid, w(wd_ref, full, lo)), _dot(hid, w(wd_ref, full, hi)))
        write_out(rows)

    @pl.when(n > 0)
    def _():
        wait_gather(m_cur, slot)
        for m in range(1, MOE_SB + 1):
            @pl.when(m_cur == m)
            def _():
                compute(m * MOE_BLK)

    @pl.when((n == 0) & (m_cur > 0))
    def _():
        @pl.when(has_prev)
        def _():
            wait_prev_out()

        yb_ref[...] = jnp.zeros_like(yb_ref)
        for m in range(1, MOE_SB + 1):
            @pl.when(m_cur == m)
            def _():
                write_out(m * MOE_BLK)

    @pl.when((m_cur == 0) & has_prev)
    def _():
        wait_prev_out()


def _moe(x_slots, plan, w_gate, w_up, w_down, n_out_rows):
    sb_e, sb_m, sb_n, sb_base = plan
    sb_rows = MOE_SB * MOE_BLK
    wspec = lambda s: pl.BlockSpec((None,) + s, lambda b, se, *_: (se[b], 0, 0))
    grid_spec = pltpu.PrefetchScalarGridSpec(
        num_scalar_prefetch=4,
        grid=(sb_e.shape[0],),
        in_specs=[pl.BlockSpec(memory_space=pl.ANY),
                  wspec((D_MODEL, D_EXPERT)), wspec((D_MODEL, D_EXPERT)), wspec((D_EXPERT, D_MODEL))],
        out_specs=pl.BlockSpec(memory_space=pl.ANY),
        scratch_shapes=[pltpu.VMEM((2, sb_rows, HALF), jnp.uint32), pltpu.VMEM((sb_rows, HALF), jnp.uint32),
                        pltpu.SemaphoreType.DMA((2,)), pltpu.SemaphoreType.DMA((1,))],
    )
    return pl.pallas_call(
        _moe_kernel,
        grid_spec=grid_spec,
        out_shape=jax.ShapeDtypeStruct((n_out_rows, HALF), jnp.uint32),
        compiler_params=pltpu.CompilerParams(dimension_semantics=("arbitrary",),
                                             vmem_limit_bytes=MOE_VMEM_LIMIT),
        name="moe_experts",
    )(sb_e, sb_m, sb_n, sb_base, x_slots, w_gate, w_up, w_down)


def _dispatch_kernel(dest_ref, zb_ref, hn_ref, out_hbm, buf_ref, zero_ref, sem, zsem):
    i = pl.program_id(0)
    nt = pl.num_programs(0)
    tm = hn_ref.shape[0]
    t_all = nt * tm
    slot = lax.rem(i, 2)
    nblk = zb_ref.shape[0]

    def zero_copy(blk):
        dst = pl.ds(pl.multiple_of(blk * MOE_BLK, MOE_BLK), MOE_BLK)
        return pltpu.make_async_copy(zero_ref, out_hbm.at[dst, :], zsem.at[0])

    @pl.when(i == 0)
    def _():
        zero_ref[...] = jnp.zeros_like(zero_ref)

        def start(blk, c):
            @pl.when(zb_ref[blk] == 1)
            def _():
                zero_copy(blk).start()
            return c

        def wait(blk, c):
            @pl.when(zb_ref[blk] == 1)
            def _():
                zero_copy(blk).wait()
            return c

        lax.fori_loop(0, nblk, start, 0)
        lax.fori_loop(0, nblk, wait, 0)

    def wait_rows(sl):
        for k in range(2):
            pltpu.make_async_copy(buf_ref.at[sl], out_hbm.at[pl.ds(0, tm), :], sem.at[sl]).wait()

    @pl.when(i >= 2)
    def _():
        wait_rows(slot)

    buf_ref[slot] = hn_ref[...]
    base = i * tm
    for a in range(2 * tm):
        row, k = a // 2, a % 2
        d = dest_ref[k * t_all + base + row]
        pltpu.make_async_copy(buf_ref.at[slot, pl.ds(row, 1), :], out_hbm.at[pl.ds(d, 1), :],
                              sem.at[slot]).start(priority=k)

    @pl.when(i == nt - 1)
    def _():
        wait_rows(slot)

        @pl.when(nt > 1)
        def _():
            wait_rows(1 - slot)


def _dispatch(dest, zero_blk, hn_pk, n_slots):
    tm = 2 * TOK_TILE
    grid_spec = pltpu.PrefetchScalarGridSpec(
        num_scalar_prefetch=2,
        grid=(hn_pk.shape[0] // tm,),
        in_specs=[pl.BlockSpec((tm, HALF), lambda i, d, z: (i, 0))],
        out_specs=pl.BlockSpec(memory_space=pl.ANY),
        scratch_shapes=[pltpu.VMEM((2, tm, HALF), jnp.uint32), pltpu.VMEM((MOE_BLK, HALF), jnp.uint32),
                        pltpu.SemaphoreType.DMA((2,)), pltpu.SemaphoreType.DMA((1,))],
    )
    return pl.pallas_call(
        _dispatch_kernel,
        grid_spec=grid_spec,
        out_shape=jax.ShapeDtypeStruct((n_slots, HALF), jnp.uint32),
        compiler_params=pltpu.CompilerParams(dimension_semantics=("arbitrary",), vmem_limit_bytes=VMEM_LIMIT),
        name="moe_dispatch",
    )(dest, zero_blk, hn_pk)


def _moe_plan(rtt, cnt):
    t_all = rtt.shape[1]
    n_assign = 2 * t_all
    e_flat = rtt[0:2].astype(jnp.int32).reshape(-1)
    rank = rtt[4:6].astype(jnp.int32).reshape(-1)
    counts = cnt[0, 0:N_EXPERTS].astype(jnp.int32)
    padded = (counts + MOE_BLK - 1) // MOE_BLK * MOE_BLK
    pad_end = jnp.cumsum(padded)
    pad_start = pad_end - padded
    before = jnp.arange(N_EXPERTS, dtype=jnp.int32)[:, None] < e_flat[None, :]
    dest = rank + jnp.sum(jnp.where(before, padded[:, None], 0), axis=0)
    nblk = -(-(n_assign + N_EXPERTS * (MOE_BLK - 1)) // MOE_BLK)
    n_slots = nblk * MOE_BLK
    blk = jnp.arange(nblk, dtype=jnp.int32)
    last_of_expert = jnp.any((blk[:, None] == (pad_end // MOE_BLK - 1)[None, :]) & (counts > 0)[None, :], axis=1)
    zero_blk = (last_of_expert | (blk >= pad_end[-1] // MOE_BLK)).astype(jnp.int32)
    k_e = padded // MOE_BLK
    sbc = (k_e + MOE_SB - 1) // MOE_SB
    sb_end = jnp.cumsum(sbc)
    sb_start = sb_end - sbc
    n_sb = (nblk + (MOE_SB - 1) * N_EXPERTS) // MOE_SB
    s = jnp.arange(n_sb, dtype=jnp.int32)
    sb_e = jnp.minimum(jnp.sum((s[:, None] >= sb_end[None, :]).astype(jnp.int32), axis=1), N_EXPERTS - 1)
    j = s - sb_start[sb_e]
    real = s < sb_end[-1]
    tail_blk = pad_end[-1] // MOE_BLK + MOE_SB * (s - sb_end[-1])
    sb_m = jnp.where(real, jnp.clip(k_e[sb_e] - MOE_SB * j, 0, MOE_SB), jnp.clip(nblk - tail_blk, 0, MOE_SB))
    sb_n = jnp.where(real, jnp.clip(counts[sb_e] - MOE_SB * MOE_BLK * j, 0, MOE_SB * MOE_BLK), 0)
    sb_base = jnp.where(real, pad_start[sb_e] + MOE_SB * MOE_BLK * j, jnp.minimum(tail_blk, nblk - 1) * MOE_BLK)
    i32 = lambda a: a.astype(jnp.int32)
    return (i32(sb_e), i32(sb_m), i32(sb_n), i32(sb_base)), i32(dest), zero_blk, n_slots


def _ple_out_kernel(dest_ref, h1_ref, rt_ref, pp_ref, ps_ref, nple_ref, wpg_ref, wp_ref, nfin_ref, ys_hbm,
                    op_ref, os_ref, yb_ref, sem, *, n_prompt_tiles):
    i = pl.program_id(0)
    nt = pl.num_programs(0)
    tm = h1_ref.shape[0]
    t_all = nt * tm
    slot = lax.rem(i, 2)

    def issue(tile, sl):
        base = tile * tm

        def grp(g, c):
            for j in range(DMA_UNROLL):
                row = g * (DMA_UNROLL // 2) + j // 2
                d = dest_ref[(j % 2) * t_all + base + row]
                pltpu.make_async_copy(ys_hbm.at[pl.ds(d, 1), :], yb_ref.at[sl, j % 2, pl.ds(row, 1), :],
                                      sem.at[sl]).start()
            return c

        lax.fori_loop(0, 2 * tm // DMA_UNROLL, grp, 0)

    @pl.when(i == 0)
    def _():
        issue(0, 0)

    def wait_rows(sl):
        for k in range(2):
            pltpu.make_async_copy(ys_hbm.at[pl.ds(0, tm), :], yb_ref.at[sl, k], sem.at[sl]).wait()

    wait_rows(slot)
    is_p = i < n_prompt_tiles
    rt = rt_ref[...]
    w0, w1 = rt[:, 2:3], rt[:, 3:4]
    lo0, hi0 = _unpack_bf16_pair(yb_ref[slot, 0])
    lo1, hi1 = _unpack_bf16_pair(yb_ref[slot, 1])
    h2 = h1_ref[...] + jnp.concatenate([w0 * lo0 + w1 * lo1, w0 * hi0 + w1 * hi1], axis=1)
    hb = _rms(h2, nple_ref[...]).astype(BF16)
    nbase = jnp.minimum(i + 1, nt - 1) * tm
    n_chunk = 8
    cw, ca = D_MODEL // n_chunk, 2 * tm // n_chunk
    gates = []
    for c in range(n_chunk):
        gates.append(jax.nn.sigmoid(_dot(hb, wpg_ref[:, c * cw:(c + 1) * cw])))
        for a in range(c * ca, (c + 1) * ca):
            d = dest_ref[(a % 2) * t_all + nbase + a // 2]
            pltpu.make_async_copy(ys_hbm.at[pl.ds(d, 1), :], yb_ref.at[1 - slot, a % 2, pl.ds(a // 2, 1), :],
                                  sem.at[1 - slot]).start(priority=a % 2)
    gate = jnp.concatenate(gates, axis=1)

    @pl.when(i == nt - 1)
    def _():
        wait_rows(1 - slot)

    p = jnp.where(is_p, pp_ref[...], ps_ref[...])
    h3 = h2 + _dot(p.astype(BF16), wp_ref[...]) * gate
    y = _rms(h3, nfin_ref[...])

    @pl.when(is_p)
    def _():
        op_ref[...] = y

    @pl.when(jnp.logical_not(is_p))
    def _():
        os_ref[...] = y


def _ple_out(dest, h1, rt, pp, ps, nple, wpg, wp, nfin, y_slots):
    tm = TOK_TILE
    npt, nst = pp.shape[0] // tm, ps.shape[0] // tm
    row = lambda w: pl.BlockSpec((tm, w), lambda i, d: (i, 0))
    prow = lambda w: pl.BlockSpec((tm, w), lambda i, d: (jnp.minimum(i, npt - 1), 0))
    srow = lambda w: pl.BlockSpec((tm, w), lambda i, d: (jnp.maximum(i - npt, 0), 0))
    const = lambda shape: pl.BlockSpec(shape, lambda i, d: (0,) * len(shape), pipeline_mode=pl.Buffered(1))
    grid_spec = pltpu.PrefetchScalarGridSpec(
        num_scalar_prefetch=1,
        grid=(npt + nst,),
        in_specs=[row(D_MODEL), row(LANES), prow(D_PLE), srow(D_PLE), const((1, D_MODEL)), const(wpg.shape),
                  const(wp.shape), const((1, D_MODEL)), pl.BlockSpec(memory_space=pl.ANY)],
        out_specs=[prow(D_MODEL), srow(D_MODEL)],
        scratch_shapes=[pltpu.VMEM((2, 2, tm, HALF), jnp.uint32), pltpu.SemaphoreType.DMA((2,))],
    )
    return pl.pallas_call(
        functools.partial(_ple_out_kernel, n_prompt_tiles=npt),
        grid_spec=grid_spec,
        out_shape=[jax.ShapeDtypeStruct((pp.shape[0], D_MODEL), F32),
                   jax.ShapeDtypeStruct((ps.shape[0], D_MODEL), F32)],
        compiler_params=pltpu.CompilerParams(dimension_semantics=("arbitrary",), vmem_limit_bytes=VMEM_LIMIT),
        name="ple_out",
    )(dest, h1, rt, pp, ps, nple, wpg, wp, nfin, y_slots)


def _s5_params(lam_re, lam_im, log_dt, b_re, b_im, c_re, c_im, d_skip):
    dt = jnp.exp(log_dt)[:, None]
    mag = jnp.exp(lam_re * dt)
    ab_re = mag * jnp.cos(lam_im * dt)
    ab_im = mag * jnp.sin(lam_im * dt)
    den = lam_re * lam_re + lam_im * lam_im
    nr = ab_re - 1.0
    f_re = (nr * lam_re + ab_im * lam_im) / den
    f_im = (ab_im * lam_re - nr * lam_im) / den
    bb_re = f_re[..., None] * b_re - f_im[..., None] * b_im
    bb_im = f_re[..., None] * b_im + f_im[..., None] * b_re
    eye = jnp.eye(S5_GB, dtype=F32)

    def bu_w(bb):
        bb = bb.reshape(S5_NGB, S5_GB, S5_STATE, S5_GROUP)
        w = jnp.einsum('nlph,lm->nlhmp', bb, eye)
        return w.reshape(S5_NGB, S5_GB * S5_GROUP, S5_SL)

    def c_w(c):
        c = c.reshape(S5_NGB, S5_GB, S5_GROUP, S5_STATE)
        w = jnp.einsum('nlhp,lm->nlpmh', c, eye)
        return w.reshape(S5_NGB, S5_SL, S5_GB * S5_GROUP)

    wbu = jnp.concatenate([bu_w(bb_re), bu_w(bb_im)], axis=2).astype(BF16)
    wc = jnp.concatenate([c_w(c_re), -c_w(c_im)], axis=1).astype(BF16)
    a_re = ab_re.reshape(S5_NGB, 1, S5_SL)
    a_im = ab_im.reshape(S5_NGB, 1, S5_SL)
    dsk = d_skip.reshape(S5_NGB, 1, S5_GB * S5_GROUP)
    return wbu, wc, a_re, a_im, dsk


def kernel(x_prompt, x_sample, p_prompt, p_sample, state_gla, state_s5_re, state_s5_im, norm_mix, w_in, gla_w_gate_up, gla_gate_bias, gla_norm, s5_lam_re, s5_lam_im, s5_log_dt, s5_b_re, s5_b_im, s5_c_re, s5_c_im, s5_d, s5_glu_w, s5_glu_b, s5_norm, w_out, norm_ffn, router_group, router_expert, w_gate, w_up, w_down, norm_ple, w_ple, w_ple_gate, norm_final):
    depth = w_in.shape[0]
    assert depth == 1
    i = 0
    bp, lp, _ = x_prompt.shape
    bs, ls, _ = x_sample.shape
    tp, ts = bp * lp, bs * ls
    t_all = tp + ts

    wm = _w_in_prep(w_in[i].astype(BF16))
    wgu = jnp.pad(gla_w_gate_up[i], ((0, LANES - GLA_RANK), (0, 0))).astype(BF16)
    gbias = gla_gate_bias[i].reshape(1, QK_W)
    wbu, wc, a_re, a_im, dsk = _s5_params(s5_lam_re[i], s5_lam_im[i], s5_log_dt[i], s5_b_re[i], s5_b_im[i],
                                          s5_c_re[i], s5_c_im[i], s5_d[i])
    glu_w = s5_glu_w[i].astype(BF16)
    wo = w_out[i].astype(BF16)
    wr32 = jnp.pad(jnp.concatenate([router_group[i], router_expert[i]], axis=1),
                   ((0, 0), (0, LANES - N_EGROUPS - N_EXPERTS)))
    wr_hi = wr32.astype(BF16)
    wr = jnp.stack([wr_hi, (wr32 - wr_hi.astype(F32)).astype(BF16)])
    wpg = w_ple_gate[i].astype(BF16)
    wp = w_ple[i].astype(BF16)
    vec = lambda a: a.reshape(1, -1)

    xp = x_prompt.reshape(tp, D_MODEL)
    xs = x_sample.reshape(ts, D_MODEL)

    qp, kp, vp, rp, lap, up = _in_proj(xp, vec(norm_mix[i]), wm, wgu, gbias, BF16)
    qs, ks, vs, rs, las, us = _in_proj(xs, vec(norm_mix[i]), wm, wgu, gbias, BF16)
    ogp, gla_p = _gla_prompt(qp, kp, vp, rp, lap, vec(gla_norm[i]), bp, lp)
    ogs, gla_s = _gla_sample(qs, ks, vs, rs, las, vec(gla_norm[i]), state_gla[i], bs, ls)
    wbu2, wc2, a_re2, a_im2 = _s5_prompt_params(wbu, wc, a_re, a_im, bp)
    ysp, re_p, im_p = _s5_prompt(up.reshape(bp, lp, D_S5), wbu2, wc2, a_re2, a_im2, dsk, tc=256)
    yss, re_s, im_s = _s5(us.reshape(bs, ls, D_S5), wbu, wc, a_re, a_im, dsk,
                          state_s5_re[i].reshape(bs, -1), state_s5_im[i].reshape(bs, -1),
                          nb=bs, tc=ls, has_state=True)

    h1, hn_all, rt_all, rtt, cnt = _mix_out(xp, xs, ogp, ogs, ysp.reshape(tp, D_S5), yss.reshape(ts, D_S5),
                                            glu_w, vec(s5_glu_b[i]), vec(s5_norm[i]), wo, vec(norm_ffn[i]), wr)

    plan, dest, zero_blk, n_slots = _moe_plan(rtt, cnt)
    x_slots = _dispatch(dest, zero_blk, hn_all, n_slots)
    y_slots = _moe(x_slots, plan, w_gate[i], w_up[i], w_down[i], n_slots)

    y_p, y_s = _ple_out(dest, h1, rt_all, p_prompt[i].reshape(tp, D_PLE), p_sample[i].reshape(ts, D_PLE),
                        vec(norm_ple[i]), wpg, wp, vec(norm_final), y_slots)

    s5shape = lambda a, b: a.reshape(1, b, S5_GROUPS, S5_STATE)
    return (y_p.reshape(bp, lp, D_MODEL), y_s.reshape(bs, ls, D_MODEL),
            gla_p[None], s5shape(re_p, bp), s5shape(im_p, bp),
            gla_s[None], s5shape(re_s, bs), s5shape(im_s, bs))
```
